```python
import jax, jax.numpy as jnp
from jax import lax
import numpy as np

D_MODEL = 1024
BATCH = 8
SEQ = 8192
DEPTH = 1

HEAD_DIM = 64
D_MIX = D_MODEL
D_CONV = D_MIX // 2
D_ATTN = D_MIX - D_CONV
N_HEADS = D_ATTN // HEAD_DIM
CONV_WIDTH = 3
DILATED_BRANCHES = ((128, 1), (512, 4), (2048, 16))
BLOCK = 128
D_FF = ((-(-8 * D_MODEL // 3) + 255) // 256) * 256
D_IN = 3 * D_CONV + 3 * D_ATTN
EPS = 1e-6

kernel_name = "hybrid_shortconv_dilated_swa_swiglu"


def rms_norm(x, g):
    xf = x.astype(jnp.float32)
    y = xf * lax.rsqrt(jnp.mean(xf * xf, axis=-1, keepdims=True) + EPS)
    return (y * g.astype(jnp.float32)).astype(x.dtype)


def short_conv(u, w):
    return lax.conv_general_dilated(
        u, w[:, None, :].astype(u.dtype), window_strides=(1,),
        padding=[(CONV_WIDTH - 1, 0)], dimension_numbers=("NWC", "WIO", "NWC"),
        feature_group_count=u.shape[-1])


def dilated_branch(q, k, v, window, dil):
    B, S, H, dh = q.shape
    M = S // dil
    L = window // dil
    nb = -(-M // BLOCK)
    Mp = nb * BLOCK
    pad = Mp - M

    def to_sub(t):
        return t.reshape(B, M, dil, H, dh).transpose(0, 2, 1, 3, 4).reshape(B * dil, M, H, dh)

    qs, ks, vs = to_sub(q), to_sub(k), to_sub(v)
    Bd = B * dil
    qb = jnp.pad(qs, ((0, 0), (0, pad), (0, 0), (0, 0))).reshape(Bd, nb, BLOCK, H, dh)

    def band(t):
        tp = jnp.pad(t, ((0, 0), (BLOCK, pad), (0, 0), (0, 0)))
        prev = tp[:, :Mp].reshape(Bd, nb, BLOCK, H, dh)
        cur = tp[:, BLOCK:].reshape(Bd, nb, BLOCK, H, dh)
        return jnp.concatenate([prev, cur], axis=2)

    kw, vw = band(ks), band(vs)
    s = jnp.einsum("bnqhd,bnkhd->bnhqk", qb, kw, preferred_element_type=jnp.float32)

    i = jnp.arange(BLOCK)[:, None]
    j = jnp.arange(2 * BLOCK)[None, :]
    dist = BLOCK + i - j
    kpos = (jnp.arange(nb)[:, None, None] - 1) * BLOCK + j[None]
    valid = ((dist >= 0) & (dist <= L))[None] & (kpos >= 0)
    s = jnp.where(valid[None, :, None], s, -jnp.inf)

    m = jnp.max(s, axis=-1, keepdims=True)
    e = jnp.exp(s - m)
    den = jnp.sum(e, axis=-1, keepdims=True)
    p = e / den
    lse = (m + jnp.log(den))[..., 0]
    o = jnp.einsum("bnhqk,bnkhd->bnqhd", p.astype(vw.dtype), vw)

    o = o.reshape(Bd, Mp, H, dh)[:, :M]
    lse = lse.transpose(0, 1, 3, 2).reshape(Bd, Mp, H)[:, :M]
    o = o.reshape(B, dil, M, H, dh).transpose(0, 2, 1, 3, 4).reshape(B, S, H, dh)
    lse = lse.reshape(B, dil, M, H).transpose(0, 2, 1, 3).reshape(B, S, H)
    return o, lse


def dilated_mixture(q, k, v):
    outs, lses = [], []
    for window, dil in DILATED_BRANCHES:
        o, lse = dilated_branch(q, k, v, window, dil)
        outs.append(o)
        lses.append(lse)
    w = jax.nn.softmax(jnp.stack(lses, axis=0), axis=0)
    o = jnp.sum(w[..., None] * jnp.stack(outs, axis=0).astype(jnp.float32), axis=0)
    return o.astype(q.dtype)


def _fwd_setup_inputs(seed: int = 0) -> dict:
    key = jax.random.key(seed)
    ks = jax.random.split(key, 13)
    f32 = jnp.float32

    def gain(k_, n):
        return 1.0 + 0.05 * jax.random.normal(k_, (DEPTH, n), f32)

    return {
        "x": jax.random.normal(ks[0], (BATCH, SEQ, D_MODEL), f32),
        "g_mix": gain(ks[1], D_MODEL),
        "w_in": jax.random.normal(ks[2], (DEPTH, D_MODEL, D_IN), f32) * D_MODEL ** -0.5,
        "conv_w": jax.random.normal(ks[3], (DEPTH, CONV_WIDTH, D_CONV), f32) * CONV_WIDTH ** -0.5,
        "g_q": gain(ks[4], HEAD_DIM),
        "g_k": gain(ks[5], HEAD_DIM),
        "g_conv_out": gain(ks[6], D_CONV),
        "g_attn_out": gain(ks[7], D_ATTN),
        "w_out": jax.random.normal(ks[8], (DEPTH, D_MIX, D_MODEL), f32) * D_MIX ** -0.5,
        "g_ffn": gain(ks[9], D_MODEL),
        "w_gate": jax.random.normal(ks[10], (DEPTH, D_MODEL, D_FF), f32) * D_MODEL ** -0.5,
        "w_up": jax.random.normal(ks[11], (DEPTH, D_MODEL, D_FF), f32) * D_MODEL ** -0.5,
        "w_down": jax.random.normal(ks[12], (DEPTH, D_FF, D_MODEL), f32) * D_FF ** -0.5,
    }


def _fwd_reference(x, g_mix, w_in, conv_w, g_q, g_k, g_conv_out, g_attn_out, w_out,
              g_ffn, w_gate, w_up, w_down):
    B, S, _ = x.shape
    splits = [D_CONV, 2 * D_CONV, 3 * D_CONV, 3 * D_CONV + D_ATTN, 3 * D_CONV + 2 * D_ATTN]
    for l in range(DEPTH):
        h = rms_norm(x, g_mix[l])
        z = h @ w_in[l]
        u, gb, gc, q, k, v = jnp.split(z, splits, axis=-1)

        y_conv = gb * short_conv(gc * u, conv_w[l])

        q = rms_norm(q.reshape(B, S, N_HEADS, HEAD_DIM), g_q[l]) * (HEAD_DIM ** -0.5)
        k = rms_norm(k.reshape(B, S, N_HEADS, HEAD_DIM), g_k[l])
        v = v.reshape(B, S, N_HEADS, HEAD_DIM)
        y_attn = dilated_mixture(q, k, v).reshape(B, S, D_ATTN)

        mix = jnp.concatenate([rms_norm(y_conv, g_conv_out[l]),
                               rms_norm(y_attn, g_attn_out[l])], axis=-1)
        x = x + mix @ w_out[l]

        h = rms_norm(x, g_ffn[l])
        x = x + (jax.nn.silu(h @ w_gate[l]) * (h @ w_up[l])) @ w_down[l]
    return x


import jax as _jax
import jax.numpy as _jnp

TWIN_FORMAT = 'train_step'
FWD_PARAMS = ['x', 'g_mix', 'w_in', 'conv_w', 'g_q', 'g_k', 'g_conv_out', 'g_attn_out', 'w_out', 'g_ffn', 'w_gate', 'w_up', 'w_down']
TWIN_WEIGHTS = ['g_mix', 'w_in', 'conv_w', 'g_q', 'g_k', 'g_conv_out', 'g_attn_out', 'w_out', 'g_ffn', 'w_gate', 'w_up', 'w_down']
TWIN_DIFF_INPUT = 'x'
TWIN_INPUTS = ['x', 'g_mix', 'w_in', 'conv_w', 'g_q', 'g_k', 'g_conv_out', 'g_attn_out', 'w_out', 'g_ffn', 'w_gate', 'w_up', 'w_down', 'loss_target', 'm_g_mix', 'm_w_in', 'm_conv_w', 'm_g_q', 'm_g_k', 'm_g_conv_out', 'm_g_attn_out', 'm_w_out', 'm_g_ffn', 'm_w_gate', 'm_w_up', 'm_w_down', 'v_g_mix', 'v_w_in', 'v_conv_w', 'v_g_q', 'v_g_k', 'v_g_conv_out', 'v_g_attn_out', 'v_w_out', 'v_g_ffn', 'v_w_gate', 'v_w_up', 'v_w_down']
TWIN_OUTPUTS = ['loss', 'grad_x', 'grad_g_mix', 'grad_w_in', 'grad_conv_w', 'grad_g_q', 'grad_g_k', 'grad_g_conv_out', 'grad_g_attn_out', 'grad_w_out', 'grad_g_ffn', 'grad_w_gate', 'grad_w_up', 'grad_w_down', 'delta_g_mix', 'delta_w_in', 'delta_conv_w', 'delta_g_q', 'delta_g_k', 'delta_g_conv_out', 'delta_g_attn_out', 'delta_w_out', 'delta_g_ffn', 'delta_w_gate', 'delta_w_up', 'delta_w_down', 'new_m_g_mix', 'new_m_w_in', 'new_m_conv_w', 'new_m_g_q', 'new_m_g_k', 'new_m_g_conv_out', 'new_m_g_attn_out', 'new_m_w_out', 'new_m_g_ffn', 'new_m_w_gate', 'new_m_w_up', 'new_m_w_down', 'new_v_g_mix', 'new_v_w_in', 'new_v_conv_w', 'new_v_g_q', 'new_v_g_k', 'new_v_g_conv_out', 'new_v_g_attn_out', 'new_v_w_out', 'new_v_g_ffn', 'new_v_w_gate', 'new_v_w_up', 'new_v_w_down']
TWIN_LEAF_KINDS = {'loss': 'loss', 'grad_x': 'grad_x', 'grad_g_mix': 'grad_w', 'grad_w_in': 'grad_w', 'grad_conv_w': 'grad_w', 'grad_g_q': 'grad_w', 'grad_g_k': 'grad_w', 'grad_g_conv_out': 'grad_w', 'grad_g_attn_out': 'grad_w', 'grad_w_out': 'grad_w', 'grad_g_ffn': 'grad_w', 'grad_w_gate': 'grad_w', 'grad_w_up': 'grad_w', 'grad_w_down': 'grad_w', 'delta_g_mix': 'delta_w', 'delta_w_in': 'delta_w', 'delta_conv_w': 'delta_w', 'delta_g_q': 'delta_w', 'delta_g_k': 'delta_w', 'delta_g_conv_out': 'delta_w', 'delta_g_attn_out': 'delta_w', 'delta_w_out': 'delta_w', 'delta_g_ffn': 'delta_w', 'delta_w_gate': 'delta_w', 'delta_w_up': 'delta_w', 'delta_w_down': 'delta_w', 'new_m_g_mix': 'new_m', 'new_m_w_in': 'new_m', 'new_m_conv_w': 'new_m', 'new_m_g_q': 'new_m', 'new_m_g_k': 'new_m', 'new_m_g_conv_out': 'new_m', 'new_m_g_attn_out': 'new_m', 'new_m_w_out': 'new_m', 'new_m_g_ffn': 'new_m', 'new_m_w_gate': 'new_m', 'new_m_w_up': 'new_m', 'new_m_w_down': 'new_m', 'new_v_g_mix': 'new_v', 'new_v_w_in': 'new_v', 'new_v_conv_w': 'new_v', 'new_v_g_q': 'new_v', 'new_v_g_k': 'new_v', 'new_v_g_conv_out': 'new_v', 'new_v_g_attn_out': 'new_v', 'new_v_w_out': 'new_v', 'new_v_g_ffn': 'new_v', 'new_v_w_gate': 'new_v', 'new_v_w_up': 'new_v', 'new_v_w_down': 'new_v'}


def _forward(args):
    return _fwd_reference(*[args[k] for k in FWD_PARAMS])


def _output_shape():
    out = _jax.eval_shape(lambda: _forward(_fwd_setup_inputs(0)))
    return out.shape, out.dtype

N_MICROBATCH = 1
ADAM_LR = 0.001
ADAM_B1 = 0.9
ADAM_B2 = 0.999
ADAM_EPS = 1e-08
ADAM_WD = 0.01
ADAM_STEP = 10
PER_EXAMPLE_BATCH_AXIS = {'x': 0, 'loss_target': 0}
SHARED_INPUTS = []
_WEIGHT_DTYPES = {'g_mix': _jnp.float32, 'w_in': _jnp.float32, 'conv_w': _jnp.float32, 'g_q': _jnp.float32, 'g_k': _jnp.float32, 'g_conv_out': _jnp.float32, 'g_attn_out': _jnp.float32, 'w_out': _jnp.float32, 'g_ffn': _jnp.float32, 'w_gate': _jnp.float32, 'w_up': _jnp.float32, 'w_down': _jnp.float32}
MOMENT_SCALE = {'g_mix': 1.308798e+00, 'w_in': 6.773002e-01, 'conv_w': 2.801624e+00, 'g_q': 1.768769e+00, 'g_k': 1.619637e+00, 'g_conv_out': 7.847602e+01, 'g_attn_out': 6.288054e+01, 'w_out': 1.397262e+00, 'g_ffn': 4.946556e+01, 'w_gate': 2.976582e-01, 'w_up': 3.076621e-01, 'w_down': 4.766184e-01}


def _to_microbatches(a, axis):
    t = _jnp.moveaxis(a, axis, 0)
    t = t.reshape((N_MICROBATCH, t.shape[0] // N_MICROBATCH) + t.shape[1:])
    return _jnp.moveaxis(t, 1, axis + 1)


def setup_inputs(seed: int = 0) -> dict:
    inp = _fwd_setup_inputs(seed)
    key = _jax.random.fold_in(_jax.random.key(seed), 7919)
    shape, _ = _output_shape()
    out = dict(inp)
    out["loss_target"] = _jax.random.normal(_jax.random.fold_in(key, 0), shape, _jnp.float32)
    for i, name in enumerate(TWIN_WEIGHTS):
        w = inp[name].astype(_jnp.float32)
        if MOMENT_SCALE is None:
            s = _jnp.sqrt(_jnp.mean(_jnp.square(w)) + 1e-30)
        else:
            s = MOMENT_SCALE[name]
        km, kv = _jax.random.split(_jax.random.fold_in(key, i + 1))
        out[name] = w
        out["m_" + name] = s * _jax.random.normal(km, w.shape, _jnp.float32)
        out["v_" + name] = (s * s) * _jax.random.uniform(kv, w.shape, _jnp.float32, 0.5, 1.5)
    if N_MICROBATCH > 1:
        for name, axis in PER_EXAMPLE_BATCH_AXIS.items():
            out[name] = _to_microbatches(out[name], axis)
    return {'x': out['x'], 'g_mix': out['g_mix'], 'w_in': out['w_in'], 'conv_w': out['conv_w'], 'g_q': out['g_q'], 'g_k': out['g_k'], 'g_conv_out': out['g_conv_out'], 'g_attn_out': out['g_attn_out'], 'w_out': out['w_out'], 'g_ffn': out['g_ffn'], 'w_gate': out['w_gate'], 'w_up': out['w_up'], 'w_down': out['w_down'], 'loss_target': out['loss_target'], 'm_g_mix': out['m_g_mix'], 'm_w_in': out['m_w_in'], 'm_conv_w': out['m_conv_w'], 'm_g_q': out['m_g_q'], 'm_g_k': out['m_g_k'], 'm_g_conv_out': out['m_g_conv_out'], 'm_g_attn_out': out['m_g_attn_out'], 'm_w_out': out['m_w_out'], 'm_g_ffn': out['m_g_ffn'], 'm_w_gate': out['m_w_gate'], 'm_w_up': out['m_w_up'], 'm_w_down': out['m_w_down'], 'v_g_mix': out['v_g_mix'], 'v_w_in': out['v_w_in'], 'v_conv_w': out['v_conv_w'], 'v_g_q': out['v_g_q'], 'v_g_k': out['v_g_k'], 'v_g_conv_out': out['v_g_conv_out'], 'v_g_attn_out': out['v_g_attn_out'], 'v_w_out': out['v_w_out'], 'v_g_ffn': out['v_g_ffn'], 'v_w_gate': out['v_w_gate'], 'v_w_up': out['v_w_up'], 'v_w_down': out['v_w_down']}


def _loss(weights, diff, rest, loss_target):
    with _jax.named_scope("forward"):
        args = {**rest, TWIN_DIFF_INPUT: diff, **{k: w.astype(_WEIGHT_DTYPES[k]) for k, w in weights.items()}}
        y = _forward(args)
    with _jax.named_scope("loss_head"):
        err = _jnp.square(y.astype(_jnp.float32) - loss_target)
        return 0.5 * _jnp.sum(_jnp.mean(err, axis=-1)) if err.ndim else 0.5 * err


def _adamw(w, g, m, v):
    m = ADAM_B1 * m + (1.0 - ADAM_B1) * g
    v = ADAM_B2 * v + (1.0 - ADAM_B2) * _jnp.square(g)
    m_hat = m / (1.0 - ADAM_B1 ** ADAM_STEP)
    v_hat = v / (1.0 - ADAM_B2 ** ADAM_STEP)
    delta = -ADAM_LR * (m_hat / (_jnp.sqrt(v_hat) + ADAM_EPS) + ADAM_WD * w)
    return delta, m, v


def reference(x, g_mix, w_in, conv_w, g_q, g_k, g_conv_out, g_attn_out, w_out, g_ffn, w_gate, w_up, w_down, loss_target, m_g_mix, m_w_in, m_conv_w, m_g_q, m_g_k, m_g_conv_out, m_g_attn_out, m_w_out, m_g_ffn, m_w_gate, m_w_up, m_w_down, v_g_mix, v_w_in, v_conv_w, v_g_q, v_g_k, v_g_conv_out, v_g_attn_out, v_w_out, v_g_ffn, v_w_gate, v_w_up, v_w_down):
    given = dict(x=x, g_mix=g_mix, w_in=w_in, conv_w=conv_w, g_q=g_q, g_k=g_k, g_conv_out=g_conv_out, g_attn_out=g_attn_out, w_out=w_out, g_ffn=g_ffn, w_gate=w_gate, w_up=w_up, w_down=w_down, loss_target=loss_target, m_g_mix=m_g_mix, m_w_in=m_w_in, m_conv_w=m_conv_w, m_g_q=m_g_q, m_g_k=m_g_k, m_g_conv_out=m_g_conv_out, m_g_attn_out=m_g_attn_out, m_w_out=m_w_out, m_g_ffn=m_g_ffn, m_w_gate=m_w_gate, m_w_up=m_w_up, m_w_down=m_w_down, v_g_mix=v_g_mix, v_w_in=v_w_in, v_conv_w=v_conv_w, v_g_q=v_g_q, v_g_k=v_g_k, v_g_conv_out=v_g_conv_out, v_g_attn_out=v_g_attn_out, v_w_out=v_w_out, v_g_ffn=v_g_ffn, v_w_gate=v_w_gate, v_w_up=v_w_up, v_w_down=v_w_down)
    weights = {n: given[n] for n in TWIN_WEIGHTS}
    shared = {n: given[n] for n in SHARED_INPUTS}
    per_example = {n: given[n] for n in ['x']}
    grad_fn = _jax.value_and_grad(_loss, argnums=(0, 1))

    def one_microbatch(ex, loss_target):
        ex = dict(ex)
        diff = ex.pop(TWIN_DIFF_INPUT)
        return grad_fn(weights, diff, {**shared, **ex}, loss_target)

    if N_MICROBATCH == 1:
        loss, (grad_w, grad_x) = one_microbatch(per_example, given["loss_target"])
    else:
        def body(carry, xs):
            loss_sum, grad_sum = carry
            l_k, (gw_k, gx_k) = one_microbatch(xs[0], xs[1])
            with _jax.named_scope("update"):
                return (loss_sum + l_k, _jax.tree.map(_jnp.add, grad_sum, gw_k)), gx_k

        init = (_jnp.zeros((), _jnp.float32), _jax.tree.map(_jnp.zeros_like, weights))
        (loss, grad_w), grad_x = _jax.lax.scan(body, init, (per_example, given["loss_target"]))
    with _jax.named_scope("update"):
        delta_w, new_m, new_v = {}, {}, {}
        for n in TWIN_WEIGHTS:
            delta_w[n], new_m[n], new_v[n] = _adamw(weights[n], grad_w[n], given["m_" + n], given["v_" + n])
    return (loss, grad_x, *[grad_w[n] for n in TWIN_WEIGHTS], *[delta_w[n] for n in TWIN_WEIGHTS],
            *[new_m[n] for n in TWIN_WEIGHTS], *[new_v[n] for n in TWIN_WEIGHTS])
```

```python
import jax
import jax.numpy as jnp
from jax import lax
from jax.experimental import pallas as pl
from jax.experimental.pallas import tpu as pltpu

F32 = jnp.float32
BF16 = jnp.bfloat16

D_MODEL = 1024
D_CONV = 512
D_ATTN = 512
D_IN = 3072
D_FF = 2816
HEAD_DIM = 64
N_PAIRS = 4
BLOCK = 128
DILATIONS = (1, 4, 16)
N_DEV = 8
EPS = 1e-6
NEG = -1e30

ADAM_LR = 0.001
ADAM_B1 = 0.9
ADAM_B2 = 0.999
ADAM_EPS = 1e-08
ADAM_WD = 0.01
ADAM_STEP = 10

NT = (((1,), (1,)), ((), ()))
NN = (((1,), (0,)), ((), ()))
TN = (((0,), (0,)), ((), ()))
MESH = pl.DeviceIdType.MESH

VMEM_LIMIT = 56 * 1024 * 1024


def _params(semantics=None, vmem=None):
    kw = {}
    if semantics is not None:
        kw["dimension_semantics"] = semantics
    if vmem is not None:
        kw["vmem_limit_bytes"] = vmem
    return pltpu.CompilerParams(**kw)


def _dot(a, b, dims):
    return lax.dot_general(a, b, dims, preferred_element_type=F32)


def _const_spec(shape):
    n = len(shape)
    return pl.BlockSpec(shape, lambda *_: (0,) * n, pipeline_mode=pl.Buffered(1))


def _rms(x):
    r = lax.rsqrt(jnp.mean(x * x, axis=-1, keepdims=True) + EPS)
    return x * r, r


def _rms_bwd(dy, xh, r, g):
    t = dy * g
    return r * (t - xh * jnp.mean(t * xh, axis=-1, keepdims=True))


def _head_blockdiag(scale):
    i = lax.broadcasted_iota(jnp.int32, (128, 128), 0) // HEAD_DIM
    j = lax.broadcasted_iota(jnp.int32, (128, 128), 1) // HEAD_DIM
    return jnp.where(i == j, scale, 0.0).astype(F32)


def _head_sum(x, bd):
    return lax.dot_general(x, bd, NN, precision=lax.Precision.HIGHEST, preferred_element_type=F32)


def _place():
    return lax.axis_index("x"), lax.axis_index("y"), lax.axis_index("c")


def _all_gather_weights(parts, out_dtypes):
    n = len(parts)

    def body(*refs):
        ins, outs = refs[:n], refs[n:2 * n]
        send_sems, recv_sems = refs[2 * n], refs[2 * n + 1]
        x, y, c = _place()
        me, sibling = (x, y, c), (x, y, 1 - c)
        chips = [(x, 1 - y), (1 - x, y), (1 - x, 1 - y)]

        def index(p):
            return 4 * p[0] + 2 * p[1] + p[2]

        for a in range(n):
            outs[a][index(me)] = ins[a][...].astype(outs[a].dtype)

        def copy(a, k, block, to):
            blk = outs[a].at[index(block)]
            return pltpu.make_async_remote_copy(
                src_ref=blk, dst_ref=blk, send_sem=send_sems.at[7 * a + k], recv_sem=recv_sems.at[7 * a + k],
                device_id=to, device_id_type=MESH)

        first = []
        for a in range(n):
            first.append(copy(a, 0, me, sibling))
            first += [copy(a, 1 + j, me, (*chip, c)) for j, chip in enumerate(chips)]
        for cp in first:
            cp.start()
        passed = []
        for a in range(n):
            for j, chip in enumerate(chips):
                copy(a, 1 + j, (*chip, c), me).wait_recv()
                fwd = copy(a, 4 + j, (*chip, c), sibling)
                fwd.start()
                passed.append(fwd)
        for a in range(n):
            copy(a, 0, sibling, me).wait_recv()
            for j, chip in enumerate(chips):
                copy(a, 4 + j, (*chip, 1 - c), me).wait_recv()
        for cp in first + passed:
            cp.wait_send()

    vm = pl.BlockSpec(memory_space=pltpu.VMEM)
    return pl.pallas_call(
        body, name="weight_all_gather",
        out_shape=[jax.ShapeDtypeStruct((N_DEV,) + p.shape, dt) for p, dt in zip(parts, out_dtypes)],
        in_specs=[vm] * n, out_specs=[vm] * n,
        scratch_shapes=[pltpu.SemaphoreType.DMA((7 * n,)), pltpu.SemaphoreType.DMA((7 * n,))],
        compiler_params=_params(vmem=VMEM_LIMIT),
    )(*parts)


def _all_reduce_small(v):
    def body(v_ref, out_ref, gathered, send_sems, recv_sems):
        x, y, c = _place()
        mine = 4 * x + 2 * y + c
        gathered[mine] = v_ref[...]

        def copy(k):
            peer = (1 - x if k & 4 else x, 1 - y if k & 2 else y, 1 - c if k & 1 else c)
            return pltpu.make_async_remote_copy(
                src_ref=gathered.at[mine], dst_ref=gathered.at[mine],
                send_sem=send_sems.at[k - 1], recv_sem=recv_sems.at[k - 1], device_id=peer, device_id_type=MESH)

        copies = [copy(k) for k in range(1, N_DEV)]
        for cp in copies:
            cp.start()
        for cp in copies:
            cp.wait_recv()
        for cp in copies:
            cp.wait_send()
        total = gathered[0]
        for d in range(1, N_DEV):
            total = total + gathered[d]
        out_ref[...] = total

    vm = pl.BlockSpec(memory_space=pltpu.VMEM)
    return pl.pallas_call(
        body, name="small_all_reduce",
        out_shape=jax.ShapeDtypeStruct(v.shape, F32),
        in_specs=[vm], out_specs=vm,
        scratch_shapes=[pltpu.VMEM((N_DEV,) + v.shape, F32),
                        pltpu.SemaphoreType.DMA((N_DEV - 1,)), pltpu.SemaphoreType.DMA((N_DEV - 1,))],
    )(v)


def _reduce_scatter_cores(grads):
    n = len(grads)

    def body(*refs):
        ins, outs = refs[:n], refs[n:2 * n]
        send_sems, recv_sems = refs[2 * n], refs[2 * n + 1]
        x, y, c = _place()
        copies = []
        for a in range(n):
            for k in range(4):
                copies.append(pltpu.make_async_remote_copy(
                    src_ref=ins[a].at[k, 1 - c], dst_ref=outs[a].at[k],
                    send_sem=send_sems.at[4 * a + k], recv_sem=recv_sems.at[4 * a + k],
                    device_id=(x, y, 1 - c), device_id_type=MESH))
        for cp in copies:
            cp.start()
        for cp in copies:
            cp.wait_recv()
        for cp in copies:
            cp.wait_send()

    hbm = pl.BlockSpec(memory_space=pl.ANY)
    return pl.pallas_call(
        body, name="grad_reduce_scatter_cores",
        out_shape=[jax.ShapeDtypeStruct((4,) + g.shape[2:], g.dtype) for g in grads],
        in_specs=[hbm] * n, out_specs=[hbm] * n,
        scratch_shapes=[pltpu.SemaphoreType.DMA((4 * n,)), pltpu.SemaphoreType.DMA((4 * n,))],
    )(*grads)


def _reduce_scatter_chips(partials):
    n = len(partials)

    def body(*refs):
        ins, outs = refs[:n], refs[n:2 * n]
        send_sems, recv_sems = refs[2 * n], refs[2 * n + 1]
        x, y, c = _place()
        copies = []
        for a in range(n):
            for j in (1, 2, 3):
                px = 1 - x if j & 2 else x
                py = 1 - y if j & 1 else y
                copies.append(pltpu.make_async_remote_copy(
                    src_ref=ins[a].at[2 * px + py], dst_ref=outs[a].at[j - 1],
                    send_sem=send_sems.at[3 * a + j - 1], recv_sem=recv_sems.at[3 * a + j - 1],
                    device_id=(px, py, c), device_id_type=MESH))
        for cp in copies:
            cp.start()
        for cp in copies:
            cp.wait_recv()
        for cp in copies:
            cp.wait_send()

    hbm = pl.BlockSpec(memory_space=pl.ANY)
    return pl.pallas_call(
        body, name="grad_reduce_scatter_chips",
        out_shape=[jax.ShapeDtypeStruct((3,) + p.shape[1:], p.dtype) for p in partials],
        in_specs=[hbm] * n, out_specs=[hbm] * n,
        scratch_shapes=[pltpu.SemaphoreType.DMA((3 * n,)), pltpu.SemaphoreType.DMA((3 * n,))],
    )(*partials)


def _add_core_partials(core, grads, received):
    n = len(grads)

    def body(core_ref, *refs):
        del core_ref
        gs, rs, outs = refs[:n], refs[n:2 * n], refs[2 * n:]
        for a in range(n):
            outs[a][...] = (gs[a][...].astype(F32) + rs[a][...].astype(F32)).astype(outs[a].dtype)

    in_specs = [pl.BlockSpec((None, None) + g.shape[2:], lambda k, cref: (k, cref[0], 0, 0)) for g in grads]
    in_specs += [pl.BlockSpec((None,) + r.shape[1:], lambda k, cref: (k, 0, 0)) for r in received]
    out_specs = [pl.BlockSpec((None,) + r.shape[1:], lambda k, cref: (k, 0, 0)) for r in received]
    return pl.pallas_call(
        body, name="grad_add_core_partials",
        grid_spec=pltpu.PrefetchScalarGridSpec(num_scalar_prefetch=1, grid=(4,), in_specs=in_specs, out_specs=out_specs),
        out_shape=[jax.ShapeDtypeStruct(r.shape, r.dtype) for r in received],
        compiler_params=_params(("arbitrary",)),
    )(core, *grads, *received)


def _add_chip_partials(chip, partials, received):
    n = len(partials)

    def body(chip_ref, *refs):
        del chip_ref
        ps, rs, outs = refs[:n], refs[n:2 * n], refs[2 * n:]
        for a in range(n):
            outs[a][...] = ((ps[a][...].astype(F32) + rs[a][0].astype(F32)) + rs[a][1].astype(F32)) + rs[a][2].astype(F32)

    in_specs = [pl.BlockSpec((None,) + p.shape[1:], lambda i, cref: (cref[0], 0, 0)) for p in partials]
    in_specs += [pl.BlockSpec(r.shape, lambda i, cref: (0, 0, 0)) for r in received]
    out_specs = [pl.BlockSpec(p.shape[1:], lambda i, cref: (0, 0)) for p in partials]
    return pl.pallas_call(
        body, name="grad_add_chip_partials",
        grid_spec=pltpu.PrefetchScalarGridSpec(num_scalar_prefetch=1, grid=(1,), in_specs=in_specs, out_specs=out_specs),
        out_shape=[jax.ShapeDtypeStruct(p.shape[1:], F32) for p in partials],
        compiler_params=_params(("arbitrary",), vmem=VMEM_LIMIT),
    )(chip, *partials, *received)


def _in_projection(x, g_mix, w_in_t, tm):
    s = x.shape[0]

    def body(x_ref, g_ref, w_ref, h_ref, z_ref):
        xh, _ = _rms(x_ref[...])
        h = (xh * g_ref[...]).astype(BF16)
        h_ref[...] = h
        for n0 in range(0, D_IN, 512):
            z_ref[:, n0:n0 + 512] = _dot(h, w_ref[n0:n0 + 512, :], NT).astype(BF16)

    return pl.pallas_call(
        body, name="in_projection", grid=(s // tm,),
        in_specs=[pl.BlockSpec((tm, D_MODEL), lambda i: (i, 0)), _const_spec((1, D_MODEL)), _const_spec((D_IN, D_MODEL))],
        out_specs=[pl.BlockSpec((tm, D_MODEL), lambda i: (i, 0)), pl.BlockSpec((tm, D_IN), lambda i: (i, 0))],
        out_shape=[jax.ShapeDtypeStruct((s, D_MODEL), BF16), jax.ShapeDtypeStruct((s, D_IN), BF16)],
        compiler_params=_params(("parallel",), vmem=VMEM_LIMIT),
    )(x, g_mix, w_in_t)


def _band_masks():
    i = lax.broadcasted_iota(jnp.int32, (BLOCK, 2 * BLOCK), 0)
    j = lax.broadcasted_iota(jnp.int32, (BLOCK, 2 * BLOCK), 1)
    dist = BLOCK + i - j
    band = (dist >= 0) & (dist <= BLOCK)
    return jnp.where(band, 0.0, NEG).astype(F32), jnp.where(band & (j >= BLOCK), 0.0, NEG).astype(F32)


def _block_rows(nb, d, r):
    base = nb * (BLOCK * d)
    prev = jnp.maximum(base - BLOCK * d, 0)
    if d == 1:
        return pl.ds(pl.multiple_of(base, BLOCK), BLOCK), pl.ds(pl.multiple_of(prev, BLOCK), BLOCK)
    return pl.ds(base + r, BLOCK, stride=d), pl.ds(prev + r, BLOCK, stride=d)


def _attention_forward(z, gq2, gk2):
    s = z.shape[0]
    chunk = 512

    def body(zq_ref, zk_ref, zv_ref, gq_ref, gk_ref, o_ref, lse_ref, qf, kf, vf):
        bd = _head_blockdiag(1.0 / HEAD_DIM)
        gq = gq_ref[...] * (HEAD_DIM ** -0.5)
        gk = gk_ref[...]

        def prep(i, carry):
            rows = pl.ds(pl.multiple_of(i * chunk, chunk), chunk)
            zq = zq_ref[rows, :].astype(F32)
            zk = zk_ref[rows, :].astype(F32)
            qf[rows, :] = (zq * lax.rsqrt(_head_sum(zq * zq, bd) + EPS)) * gq
            kf[rows, :] = (zk * lax.rsqrt(_head_sum(zk * zk, bd) + EPS)) * gk
            vf[rows, :] = zv_ref[rows, :].astype(F32)
            return carry

        lax.fori_loop(0, s // chunk, prep, 0)

        mask_band, mask_first = _band_masks()
        head0 = lax.broadcasted_iota(jnp.int32, (BLOCK, 128), 1) < HEAD_DIM

        for branch, d in enumerate(DILATIONS):
            for r in range(d):

                def block(nb, carry, d=d, r=r, branch=branch):
                    cur, prv = _block_rows(nb, d, r)
                    qb = qf[cur, :]
                    kc = jnp.concatenate([kf[prv, :], kf[cur, :]], axis=0).astype(BF16)
                    vc = jnp.concatenate([vf[prv, :], vf[cur, :]], axis=0).astype(BF16)
                    mask = jnp.where(nb > 0, mask_band, mask_first)
                    outs, lses = [], []
                    for sel in (head0, ~head0):
                        qh = jnp.where(sel, qb, 0.0).astype(BF16)
                        sc = _dot(qh, kc, NT) + mask
                        m = jnp.max(sc, axis=-1, keepdims=True)
                        e = jnp.exp(sc - m)
                        den = jnp.sum(e, axis=-1, keepdims=True)
                        p = (e / den).astype(BF16)
                        outs.append(_dot(p, vc, NN))
                        lses.append(m + jnp.log(den))
                    o_new = jnp.where(head0, outs[0], outs[1])
                    l_new = jnp.where(head0, lses[0], lses[1])
                    if branch == 0:
                        o_ref[cur, :] = o_new
                        lse_ref[cur, :] = l_new
                    else:
                        l_old = lse_ref[cur, :]
                        top = jnp.maximum(l_old, l_new)
                        l_all = top + jnp.log(jnp.exp(l_old - top) + jnp.exp(l_new - top))
                        o_ref[cur, :] = o_ref[cur, :] * jnp.exp(l_old - l_all) + o_new * jnp.exp(l_new - l_all)
                        lse_ref[cur, :] = l_all
                    return carry

                lax.fori_loop(0, s // (BLOCK * d), block, 0)

    def col(c0):
        return pl.BlockSpec((s, 128), lambda p: (0, c0 + p))

    return pl.pallas_call(
        body, name="attention_forward", grid=(N_PAIRS,),
        in_specs=[col(12), col(16), col(20), _const_spec((1, 128)), _const_spec((1, 128))],
        out_specs=[pl.BlockSpec((s, 128), lambda p: (0, p)), pl.BlockSpec((s, 128), lambda p: (0, p))],
        out_shape=[jax.ShapeDtypeStruct((s, D_ATTN), F32), jax.ShapeDtypeStruct((s, D_ATTN), F32)],
        scratch_shapes=[pltpu.VMEM((s, 128), F32)] * 3,
        compiler_params=_params(("arbitrary",), vmem=VMEM_LIMIT),
    )(z, z, z, gq2, gk2)


def _conv_forward(zt, zprev, w, first):
    u, gb, gc = zt[:, :D_CONV], zt[:, D_CONV:2 * D_CONV], zt[:, 2 * D_CONV:]
    cu = gc * u
    cu_prev = jnp.where(first, 0.0, zprev[:, 2 * D_CONV:] * zprev[:, :D_CONV])
    row = lax.broadcasted_iota(jnp.int32, cu.shape, 0)
    cu1 = jnp.where(row >= 1, pltpu.roll(cu, 1, 0), cu_prev[7:8, :])
    cu2 = jnp.where(row >= 2, pltpu.roll(cu, 2, 0), jnp.where(row == 1, cu_prev[7:8, :], cu_prev[6:7, :]))
    cv = w[0:1, :] * cu2 + w[1:2, :] * cu1 + w[2:3, :] * cu
    return u, gb, gc, cu, cu1, cu2, cv


def _mix_out(z, y_attn, x, conv_w, g_conv_out, g_attn_out, g_ffn, w_out, tm):
    s = x.shape[0]
    hb = tm // 8

    def body(z_ref, zp_ref, ya_ref, x_ref, cw_ref, gc_ref, ga_ref, gf_ref, w_ref, mix_ref, x2_ref, h2_ref):
        first = pl.program_id(0) == 0
        zt = z_ref[...].astype(F32)
        _, gb, _, _, _, _, cv = _conv_forward(zt, zp_ref[...].astype(F32), cw_ref[...], first)
        nc, _ = _rms(gb * cv)
        na, _ = _rms(ya_ref[...])
        mix = jnp.concatenate([nc * gc_ref[...], na * ga_ref[...]], axis=-1).astype(BF16)
        mix_ref[...] = mix
        x2 = x_ref[...] + _dot(mix, w_ref[...], NN)
        x2_ref[...] = x2
        xh, _ = _rms(x2)
        h2_ref[...] = (xh * gf_ref[...]).astype(BF16)

    tile = lambda w: pl.BlockSpec((tm, w), lambda i: (i, 0))
    return pl.pallas_call(
        body, name="mix_out", grid=(s // tm,),
        in_specs=[tile(3 * D_CONV), pl.BlockSpec((8, 3 * D_CONV), lambda i: (jnp.maximum(i * hb - 1, 0), 0)),
                  tile(D_ATTN), tile(D_MODEL), _const_spec((8, D_CONV)), _const_spec((1, D_CONV)), _const_spec((1, D_ATTN)),
                  _const_spec((1, D_MODEL)), _const_spec((D_MODEL, D_MODEL))],
        out_specs=[tile(D_MODEL)] * 3,
        out_shape=[jax.ShapeDtypeStruct((s, D_MODEL), BF16), jax.ShapeDtypeStruct((s, D_MODEL), F32),
                   jax.ShapeDtypeStruct((s, D_MODEL), BF16)],
        compiler_params=_params(("arbitrary",), vmem=VMEM_LIMIT),
    )(z, z, y_attn, x, conv_w, g_conv_out, g_attn_out, g_ffn, w_out)


FF_CHUNK = 256


def _ffn_forward(h2, x2, target, w_gate_t, w_up_t, w_down, tm):
    s = h2.shape[0]

    def body(h_ref, x2_ref, t_ref, wg_ref, wu_ref, wd_ref, a_ref, b_ref, dy_ref, loss_ref):
        @pl.when(pl.program_id(0) == 0)
        def _():
            loss_ref[...] = jnp.zeros_like(loss_ref)

        h = h_ref[...]
        y = x2_ref[...]
        for c0 in range(0, D_FF, FF_CHUNK):
            rows = slice(c0, c0 + FF_CHUNK)
            a = _dot(h, wg_ref[rows, :], NT)
            b = _dot(h, wu_ref[rows, :], NT)
            a_ref[:, rows] = a.astype(BF16)
            b_ref[:, rows] = b.astype(BF16)
            f = (a * jax.nn.sigmoid(a)) * b
            y = y + _dot(f.astype(BF16), wd_ref[rows, :], NN)
        err = y - t_ref[...]
        dy_ref[...] = err * (1.0 / D_MODEL)
        loss_ref[...] += jnp.sum(err * err)

    tile = lambda w: pl.BlockSpec((tm, w), lambda i: (i, 0))
    return pl.pallas_call(
        body, name="ffn_forward", grid=(s // tm,),
        in_specs=[tile(D_MODEL), tile(D_MODEL), tile(D_MODEL)] + [_const_spec((D_FF, D_MODEL))] * 3,
        out_specs=[tile(D_FF), tile(D_FF), tile(D_MODEL), pl.BlockSpec((8, 128), lambda i: (0, 0))],
        out_shape=[jax.ShapeDtypeStruct((s, D_FF), BF16), jax.ShapeDtypeStruct((s, D_FF), BF16),
                   jax.ShapeDtypeStruct((s, D_MODEL), F32), jax.ShapeDtypeStruct((8, 128), F32)],
        compiler_params=_params(("arbitrary",), vmem=VMEM_LIMIT),
    )(h2, x2, target, w_gate_t, w_up_t, w_down)


def _ffn_backward(dy, a, b, x2, g_ffn, w_gate_t, w_up_t, w_down, tm):
    s = dy.shape[0]

    def body(dy_ref, a_ref, b_ref, x2_ref, g_ref, wg_ref, wu_ref, wd_ref, da_ref, db_ref, f_ref, dx2_ref, dg_ref):
        @pl.when(pl.program_id(0) == 0)
        def _():
            dg_ref[...] = jnp.zeros_like(dg_ref)

        dy_f = dy_ref[...]
        dyb = dy_f.astype(BF16)
        dh = jnp.zeros((tm, D_MODEL), F32)
        for c0 in range(0, D_FF, FF_CHUNK):
            rows = slice(c0, c0 + FF_CHUNK)
            df = _dot(dyb, wd_ref[rows, :], NT)
            av = a_ref[:, rows].astype(F32)
            bv = b_ref[:, rows].astype(F32)
            sig = jax.nn.sigmoid(av)
            silu = av * sig
            f_ref[:, rows] = (silu * bv).astype(BF16)
            da = (df * bv * (sig * (1.0 + av * (1.0 - sig)))).astype(BF16)
            db = (df * silu).astype(BF16)
            da_ref[:, rows] = da
            db_ref[:, rows] = db
            dh = dh + _dot(da, wg_ref[rows, :], NN) + _dot(db, wu_ref[rows, :], NN)
        xh, r = _rms(x2_ref[...])
        dg_ref[...] += jnp.sum(dh * xh, axis=0, keepdims=True)
        dx2_ref[...] = dy_f + _rms_bwd(dh, xh, r, g_ref[...])

    tile = lambda w: pl.BlockSpec((tm, w), lambda i: (i, 0))
    return pl.pallas_call(
        body, name="ffn_backward", grid=(s // tm,),
        in_specs=[tile(D_MODEL), tile(D_FF), tile(D_FF), tile(D_MODEL), _const_spec((1, D_MODEL))]
        + [_const_spec((D_FF, D_MODEL))] * 3,
        out_specs=[tile(D_FF), tile(D_FF), tile(D_FF), tile(D_MODEL), pl.BlockSpec((1, D_MODEL), lambda i: (0, 0))],
        out_shape=[jax.ShapeDtypeStruct((s, D_FF), BF16)] * 3
        + [jax.ShapeDtypeStruct((s, D_MODEL), F32), jax.ShapeDtypeStruct((1, D_MODEL), F32)],
        compiler_params=_params(("arbitrary",), vmem=VMEM_LIMIT),
    )(dy, a, b, x2, g_ffn, w_gate_t, w_up_t, w_down)


def _out_backward(dx2, z, y_attn, conv_w, g_conv_out, g_attn_out, w_out, tm):
    s = dx2.shape[0]
    hb = tm // 8

    def body(dx2_ref, z_ref, zp_ref, ya_ref, cw_ref, gc_ref, ga_ref, w_ref, dya_ref, dgb_ref, dcv_ref, dgc_ref, dga_ref):
        first = pl.program_id(0) == 0

        @pl.when(first)
        def _():
            dgc_ref[...] = jnp.zeros_like(dgc_ref)
            dga_ref[...] = jnp.zeros_like(dga_ref)

        dmix = _dot(dx2_ref[...].astype(BF16), w_ref[...], NT)
        _, gb, _, _, _, _, cv = _conv_forward(z_ref[...].astype(F32), zp_ref[...].astype(F32), cw_ref[...], first)
        ych, rc = _rms(gb * cv)
        dnc = dmix[:, :D_CONV]
        dgc_ref[...] += jnp.sum(dnc * ych, axis=0, keepdims=True)
        dyc = _rms_bwd(dnc, ych, rc, gc_ref[...])
        dgb_ref[...] = (dyc * cv).astype(BF16)
        dcv_ref[...] = dyc * gb
        yah, ra = _rms(ya_ref[...])
        dna = dmix[:, D_CONV:]
        dga_ref[...] += jnp.sum(dna * yah, axis=0, keepdims=True)
        dya_ref[...] = _rms_bwd(dna, yah, ra, ga_ref[...])

    tile = lambda w: pl.BlockSpec((tm, w), lambda i: (i, 0))
    vec = pl.BlockSpec((1, D_CONV), lambda i: (0, 0))
    return pl.pallas_call(
        body, name="out_backward", grid=(s // tm,),
        in_specs=[tile(D_MODEL), tile(3 * D_CONV), pl.BlockSpec((8, 3 * D_CONV), lambda i: (jnp.maximum(i * hb - 1, 0), 0)),
                  tile(D_ATTN), _const_spec((8, D_CONV)), _const_spec((1, D_CONV)), _const_spec((1, D_ATTN)),
                  _const_spec((D_MODEL, D_MODEL))],
        out_specs=[tile(D_ATTN), tile(D_CONV), tile(D_CONV), vec, vec],
        out_shape=[jax.ShapeDtypeStruct((s, D_ATTN), F32), jax.ShapeDtypeStruct((s, D_CONV), BF16),
                   jax.ShapeDtypeStruct((s, D_CONV), F32), jax.ShapeDtypeStruct((1, D_CONV), F32),
                   jax.ShapeDtypeStruct((1, D_ATTN), F32)],
        compiler_params=_params(("arbitrary",), vmem=VMEM_LIMIT),
    )(dx2, z, z, y_attn, conv_w, g_conv_out, g_attn_out, w_out)


def _attention_backward(z, o, lse, dya, gq2, gk2):
    s = z.shape[0]
    chunk = 512

    def body(z_hbm, o_hbm, lse_hbm, do_hbm, gq_ref, gk_ref, dzq_ref, dzk_ref, dzv_ref, dgq_ref, dgk_ref,
             qf, kf, vf, dof, stats, dq, dk, dv, zbuf, fbuf, sem):
        pair = pl.program_id(0)
        bd_mean = _head_blockdiag(1.0 / HEAD_DIM)
        bd_sum = _head_blockdiag(1.0)
        gq = gq_ref[...] * (HEAD_DIM ** -0.5)
        gk = gk_ref[...]
        lane = lax.broadcasted_iota(jnp.int32, (chunk, 128), 1) % HEAD_DIM

        def fetch_z(col, rows):
            cp = pltpu.make_async_copy(z_hbm.at[rows, pl.ds(pl.multiple_of((col + pair) * 128, 128), 128)], zbuf, sem)
            cp.start()
            cp.wait()
            return zbuf[...].astype(F32)

        def fetch_f(src, rows):
            cp = pltpu.make_async_copy(src.at[rows, pl.ds(pl.multiple_of(pair * 128, 128), 128)], fbuf, sem)
            cp.start()
            cp.wait()
            return fbuf[...]

        def prep(i, carry):
            rows = pl.ds(pl.multiple_of(i * chunk, chunk), chunk)
            zq = fetch_z(12, rows)
            qf[rows, :] = (zq * lax.rsqrt(_head_sum(zq * zq, bd_mean) + EPS)) * gq
            zk = fetch_z(16, rows)
            kf[rows, :] = (zk * lax.rsqrt(_head_sum(zk * zk, bd_mean) + EPS)) * gk
            vf[rows, :] = fetch_z(20, rows)
            do = fetch_f(do_hbm, rows)
            dof[rows, :] = do
            delta = _head_sum(do * fetch_f(o_hbm, rows), bd_sum)
            lse_w = fetch_f(lse_hbm, rows)
            stats[rows, :] = jnp.where(lane == 0, lse_w, jnp.where(lane == 1, delta, 0.0))
            zero = jnp.zeros((chunk, 128), F32)
            dq[rows, :] = zero
            dk[rows, :] = zero
            dv[rows, :] = zero
            return carry

        lax.fori_loop(0, s // chunk, prep, 0)

        mask_band, mask_first = _band_masks()
        head0 = lax.broadcasted_iota(jnp.int32, (BLOCK, 128), 1) < HEAD_DIM

        for d in DILATIONS:
            for r in range(d):

                def block(nb, carry, d=d, r=r):
                    cur, prv = _block_rows(nb, d, r)
                    qb = qf[cur, :]
                    dob = dof[cur, :]
                    st = stats[cur, :]
                    kc = jnp.concatenate([kf[prv, :], kf[cur, :]], axis=0).astype(BF16)
                    vc = jnp.concatenate([vf[prv, :], vf[cur, :]], axis=0).astype(BF16)
                    mask = jnp.where(nb > 0, mask_band, mask_first)
                    dq_blk = jnp.zeros((BLOCK, 128), F32)
                    dk_blk = jnp.zeros((2 * BLOCK, 128), F32)
                    dv_blk = jnp.zeros((2 * BLOCK, 128), F32)
                    for h, sel in enumerate((head0, ~head0)):
                        qh = jnp.where(sel, qb, 0.0).astype(BF16)
                        doh = jnp.where(sel, dob, 0.0).astype(BF16)
                        lse_h = st[:, HEAD_DIM * h:HEAD_DIM * h + 1]
                        delta_h = st[:, HEAD_DIM * h + 1:HEAD_DIM * h + 2]
                        p = jnp.exp(_dot(qh, kc, NT) + mask - lse_h)
                        ds = (p * (_dot(doh, vc, NT) - delta_h)).astype(BF16)
                        dq_blk = dq_blk + jnp.where(sel, _dot(ds, kc, NN), 0.0)
                        dk_blk = dk_blk + _dot(ds, qh, TN)
                        dv_blk = dv_blk + _dot(p.astype(BF16), doh, TN)
                    dq[cur, :] += dq_blk
                    dk[prv, :] += dk_blk[:BLOCK]
                    dk[cur, :] += dk_blk[BLOCK:]
                    dv[prv, :] += dv_blk[:BLOCK]
                    dv[cur, :] += dv_blk[BLOCK:]
                    return carry

                lax.fori_loop(0, s // (BLOCK * d), block, 0)

        def finish(i, carry):
            acc_q, acc_k = carry
            rows = pl.ds(pl.multiple_of(i * chunk, chunk), chunk)
            zq = fetch_z(12, rows)
            rq = lax.rsqrt(_head_sum(zq * zq, bd_mean) + EPS)
            qh = zq * rq
            dqn = dq[rows, :]
            acc_q = acc_q + jnp.sum(dqn * qh, axis=0, keepdims=True)
            t = dqn * gq
            dzq_ref[rows, :] = (rq * (t - qh * _head_sum(t * qh, bd_mean))).astype(BF16)
            zk = fetch_z(16, rows)
            rk = lax.rsqrt(_head_sum(zk * zk, bd_mean) + EPS)
            kh = zk * rk
            dkn = dk[rows, :]
            acc_k = acc_k + jnp.sum(dkn * kh, axis=0, keepdims=True)
            t = dkn * gk
            dzk_ref[rows, :] = (rk * (t - kh * _head_sum(t * kh, bd_mean))).astype(BF16)
            dzv_ref[rows, :] = dv[rows, :].astype(BF16)
            return acc_q, acc_k

        zero = jnp.zeros((1, 128), F32)
        acc_q, acc_k = lax.fori_loop(0, s // chunk, finish, (zero, zero))
        dgq_ref[...] = acc_q * (HEAD_DIM ** -0.5)
        dgk_ref[...] = acc_k

    hbm = pl.BlockSpec(memory_space=pl.ANY)
    gain = pl.BlockSpec((None, 1, 128), lambda p: (p, 0, 0))
    dz_spec = pl.BlockSpec((s, 128), lambda p: (0, p))
    return pl.pallas_call(
        body, name="attention_backward", grid=(N_PAIRS,),
        in_specs=[hbm, hbm, hbm, hbm, _const_spec((1, 128)), _const_spec((1, 128))],
        out_specs=[dz_spec, dz_spec, dz_spec, gain, gain],
        out_shape=[jax.ShapeDtypeStruct((s, D_ATTN), BF16)] * 3 + [jax.ShapeDtypeStruct((N_PAIRS, 1, 128), F32)] * 2,
        scratch_shapes=[pltpu.VMEM((s, 128), F32)] * 8
        + [pltpu.VMEM((chunk, 128), BF16), pltpu.VMEM((chunk, 128), F32), pltpu.SemaphoreType.DMA],
        compiler_params=_params(("arbitrary",), vmem=VMEM_LIMIT),
    )(z, o, lse, dya, gq2, gk2)


def _in_backward(dcv, dgb, dzq, dzk, dzv, z, x, dx2, g_mix, conv_w, w_in_t, tm):
    s = x.shape[0]
    hb = tm // 8
    last_halo = s // 8 - 1

    def body(dcv_ref, dcn_ref, dgb_ref, dzq_ref, dzk_ref, dzv_ref, z_ref, zp_ref, x_ref, dx2_ref, g_ref, cw_ref, w_ref,
             dz_ref, dx_ref, dg_ref, dcw_ref):
        i = pl.program_id(0)
        first = i == 0
        last = i == pl.num_programs(0) - 1

        @pl.when(first)
        def _():
            dg_ref[...] = jnp.zeros_like(dg_ref)
            dcw_ref[...] = jnp.zeros_like(dcw_ref)

        w = cw_ref[...]
        u, _, gc, cu, cu1, cu2, _ = _conv_forward(z_ref[...].astype(F32), zp_ref[...].astype(F32), w, first)
        dcv_t = dcv_ref[...]
        nxt = jnp.where(last, 0.0, dcn_ref[...])
        row = lax.broadcasted_iota(jnp.int32, dcv_t.shape, 0)
        up1 = jnp.where(row < tm - 1, pltpu.roll(dcv_t, tm - 1, 0), nxt[0:1, :])
        up2 = jnp.where(row < tm - 2, pltpu.roll(dcv_t, tm - 2, 0), jnp.where(row == tm - 2, nxt[0:1, :], nxt[1:2, :]))
        dcu = w[2:3, :] * dcv_t + w[1:2, :] * up1 + w[0:1, :] * up2
        dcw = jnp.concatenate([jnp.sum(dcv_t * cu2, axis=0, keepdims=True), jnp.sum(dcv_t * cu1, axis=0, keepdims=True),
                               jnp.sum(dcv_t * cu, axis=0, keepdims=True), jnp.zeros((5, D_CONV), F32)], axis=0)
        dcw_ref[...] += dcw
        dz_ref[:, :D_CONV] = (dcu * gc).astype(BF16)
        dz_ref[:, D_CONV:2 * D_CONV] = dgb_ref[...]
        dz_ref[:, 2 * D_CONV:3 * D_CONV] = (dcu * u).astype(BF16)
        dz_ref[:, 3 * D_CONV:3 * D_CONV + D_ATTN] = dzq_ref[...]
        dz_ref[:, 3 * D_CONV + D_ATTN:3 * D_CONV + 2 * D_ATTN] = dzk_ref[...]
        dz_ref[:, 3 * D_CONV + 2 * D_ATTN:] = dzv_ref[...]
        dh = _dot(dz_ref[...], w_ref[...], NN)
        xh, r = _rms(x_ref[...])
        dg_ref[...] += jnp.sum(dh * xh, axis=0, keepdims=True)
        dx_ref[...] = dx2_ref[...] + _rms_bwd(dh, xh, r, g_ref[...])

    tile = lambda w: pl.BlockSpec((tm, w), lambda i: (i, 0))
    return pl.pallas_call(
        body, name="in_backward", grid=(s // tm,),
        in_specs=[tile(D_CONV), pl.BlockSpec((8, D_CONV), lambda i: (jnp.minimum((i + 1) * hb, last_halo), 0)),
                  tile(D_CONV), tile(D_ATTN), tile(D_ATTN), tile(D_ATTN),
                  tile(3 * D_CONV), pl.BlockSpec((8, 3 * D_CONV), lambda i: (jnp.maximum(i * hb - 1, 0), 0)),
                  tile(D_MODEL), tile(D_MODEL), _const_spec((1, D_MODEL)), _const_spec((8, D_CONV)),
                  _const_spec((D_IN, D_MODEL))],
        out_specs=[tile(D_IN), tile(D_MODEL), pl.BlockSpec((1, D_MODEL), lambda i: (0, 0)),
                   pl.BlockSpec((8, D_CONV), lambda i: (0, 0))],
        out_shape=[jax.ShapeDtypeStruct((s, D_IN), BF16), jax.ShapeDtypeStruct((s, D_MODEL), F32),
                   jax.ShapeDtypeStruct((1, D_MODEL), F32), jax.ShapeDtypeStruct((8, D_CONV), F32)],
        compiler_params=_params(("arbitrary",), vmem=VMEM_LIMIT),
    )(dcv, dcv, dgb, dzq, dzk, dzv, z, z, x, dx2, g_mix, conv_w, w_in_t)


def _weight_grad(name, a, b, tn, tk):
    s, n = a.shape
    steps = s // tk

    def body(a_ref, b_ref, out_ref, acc):
        k = pl.program_id(1)

        @pl.when(k == 0)
        def _():
            acc[...] = jnp.zeros_like(acc)

        acc[...] += _dot(a_ref[...], b_ref[...].astype(BF16), TN)

        @pl.when(k == steps - 1)
        def _():
            out_ref[...] = acc[...].astype(BF16)

    return pl.pallas_call(
        body, name=name, grid=(n // tn, steps),
        in_specs=[pl.BlockSpec((tk, tn), lambda j, k: (k, j)), pl.BlockSpec((tk, D_MODEL), lambda j, k: (k, 0))],
        out_specs=pl.BlockSpec((tn, D_MODEL), lambda j, k: (j, 0)),
        out_shape=jax.ShapeDtypeStruct((n, D_MODEL), BF16),
        scratch_shapes=[pltpu.VMEM((tn, D_MODEL), F32)],
        compiler_params=_params(("parallel", "arbitrary"), vmem=VMEM_LIMIT),
    )(a, b)


def _adamw(name, w, g, m, v):
    def body(w_ref, g_ref, m_ref, v_ref, d_ref, nm_ref, nv_ref):
        gv = g_ref[...]
        nm = ADAM_B1 * m_ref[...] + (1.0 - ADAM_B1) * gv
        nv = ADAM_B2 * v_ref[...] + (1.0 - ADAM_B2) * (gv * gv)
        m_hat = nm / (1.0 - ADAM_B1 ** ADAM_STEP)
        v_hat = nv / (1.0 - ADAM_B2 ** ADAM_STEP)
        d_ref[...] = -ADAM_LR * (m_hat / (jnp.sqrt(v_hat) + ADAM_EPS) + ADAM_WD * w_ref[...])
        nm_ref[...] = nm
        nv_ref[...] = nv

    rows = w.shape[0]
    tr = 256 if rows % 256 == 0 else rows
    spec = pl.BlockSpec((tr, w.shape[1]), lambda i: (i, 0))
    return pl.pallas_call(
        body, name=name, grid=(rows // tr,),
        in_specs=[spec] * 4, out_specs=[spec] * 3,
        out_shape=[jax.ShapeDtypeStruct(w.shape, F32)] * 3,
        compiler_params=_params(("parallel",)),
    )(w, g, m, v)


def kernel(x, g_mix, w_in, conv_w, g_q, g_k, g_conv_out, g_attn_out, w_out, g_ffn, w_gate, w_up, w_down, loss_target, m_g_mix, m_w_in, m_conv_w, m_g_q, m_g_k, m_g_conv_out, m_g_attn_out, m_w_out, m_g_ffn, m_w_gate, m_w_up, m_w_down, v_g_mix, v_w_in, v_conv_w, v_g_q, v_g_k, v_g_conv_out, v_g_attn_out, v_w_out, v_g_ffn, v_w_gate, v_w_up, v_w_down):
    s = x.shape[1]
    tm = min(512, s)
    xs, target = x[0], loss_target[0]
    px, py, pc = lax.axis_index("x"), lax.axis_index("y"), lax.axis_index("c")
    device = 4 * px + 2 * py + pc

    conv_block = jnp.zeros((8, 128), F32).at[:3, :HEAD_DIM].set(conv_w[0])
    parts = [w_in[0].T, w_out[0], w_gate[0].T, w_up[0].T, w_down[0], conv_block]
    gathered = _all_gather_weights(parts, [BF16] * 5 + [F32])
    w_in_t = gathered[0].reshape(D_IN, D_MODEL)
    w_out_f = gathered[1].reshape(D_MODEL, D_MODEL)
    w_gate_t = gathered[2].reshape(D_FF, D_MODEL)
    w_up_t = gathered[3].reshape(D_FF, D_MODEL)
    w_down_f = gathered[4].reshape(D_FF, D_MODEL)
    conv_full = jnp.transpose(gathered[5][:, :3, :HEAD_DIM], (1, 0, 2)).reshape(3, D_CONV)
    conv_full = jnp.concatenate([conv_full, jnp.zeros((5, D_CONV), F32)], axis=0)
    gq2 = jnp.concatenate([g_q, g_q], axis=-1)
    gk2 = jnp.concatenate([g_k, g_k], axis=-1)

    h1, z = _in_projection(xs, g_mix, w_in_t, tm)
    y_attn, lse = _attention_forward(z, gq2, gk2)
    mix, x2, h2 = _mix_out(z, y_attn, xs, conv_full, g_conv_out, g_attn_out, g_ffn, w_out_f, tm)
    tf = min(256, s)
    a, b, dy, sq_err = _ffn_forward(h2, x2, target, w_gate_t, w_up_t, w_down_f, tf)

    da, db, f, dx2, dg_ffn = _ffn_backward(dy, a, b, x2, g_ffn, w_gate_t, w_up_t, w_down_f, tf)
    dya, dgb, dcv, dg_conv_out, dg_attn_out = _out_backward(dx2, z, y_attn, conv_full, g_conv_out, g_attn_out, w_out_f, tm)
    dzq, dzk, dzv, dgq_pairs, dgk_pairs = _attention_backward(z, y_attn, lse, dya, gq2, gk2)
    dz, grad_x, dg_mix, dconv = _in_backward(dcv, dgb, dzq, dzk, dzv, z, xs, dx2, g_mix, conv_full, w_in_t, tm)

    tk = min(512, s)
    grads = [
        _weight_grad("grad_w_in", dz, h1, D_IN // 2, tk),
        _weight_grad("grad_w_out", mix, dx2, D_MODEL, tk),
        _weight_grad("grad_w_gate", da, h2, D_FF // 2, tk),
        _weight_grad("grad_w_up", db, h2, D_FF // 2, tk),
        _weight_grad("grad_w_down", f, dy, D_FF // 2, tk),
    ]

    grads = [g.reshape(4, 2, g.shape[0] // N_DEV, D_MODEL) for g in grads]
    from_sibling = _reduce_scatter_cores(grads)
    core = jnp.reshape(pc, (1,)).astype(jnp.int32)
    partials = _add_core_partials(core, grads, from_sibling)
    from_chips = _reduce_scatter_chips(partials)
    chip = jnp.reshape(2 * px + py, (1,)).astype(jnp.int32)
    gw_in_t, gw_out, gw_gate_t, gw_up_t, gw_down = _add_chip_partials(chip, partials, from_chips)
    gw_in, gw_gate, gw_up = gw_in_t.T, gw_gate_t.T, gw_up_t.T

    dg_q = jnp.sum(dgq_pairs.reshape(2 * N_PAIRS, HEAD_DIM), axis=0)
    dg_k = jnp.sum(dgk_pairs.reshape(2 * N_PAIRS, HEAD_DIM), axis=0)
    zeros = lambda n: jnp.zeros((n,), F32)
    small = jnp.stack([
        dg_mix[0], dg_ffn[0],
        jnp.concatenate([dg_conv_out[0], dg_attn_out[0]]),
        jnp.concatenate([dg_q, dg_k, zeros(D_MODEL - 2 * HEAD_DIM)]),
        jnp.concatenate([dconv[0], dconv[1]]),
        jnp.concatenate([dconv[2], zeros(D_CONV)]),
        jnp.concatenate([sq_err[0, :1], zeros(D_MODEL - 1)]),
        zeros(D_MODEL),
    ])
    total = _all_reduce_small(small)
    loss = total[6, 0] * (0.5 / D_MODEL)
    gg_mix, gg_ffn = total[0:1], total[1:2]
    gg_conv_out, gg_attn_out = total[2:3, :D_CONV], total[2:3, D_CONV:]
    gg_q, gg_k = total[3:4, :HEAD_DIM], total[3:4, HEAD_DIM:2 * HEAD_DIM]
    conv_total = jnp.stack([total[4, :D_CONV], total[4, D_CONV:], total[5, :D_CONV]])
    g_conv = lax.dynamic_slice(conv_total, (0, device * HEAD_DIM), (3, HEAD_DIM))

    names = ["g_mix", "w_in", "conv_w", "g_q", "g_k", "g_conv_out", "g_attn_out", "w_out", "g_ffn", "w_gate", "w_up", "w_down"]
    weights = [g_mix, w_in[0], conv_w[0], g_q, g_k, g_conv_out, g_attn_out, w_out[0], g_ffn, w_gate[0], w_up[0], w_down[0]]
    grad_list = [gg_mix, gw_in, g_conv, gg_q, gg_k, gg_conv_out, gg_attn_out, gw_out, gg_ffn, gw_gate, gw_up, gw_down]
    m_list = [m_g_mix, m_w_in[0], m_conv_w[0], m_g_q, m_g_k, m_g_conv_out, m_g_attn_out, m_w_out[0], m_g_ffn, m_w_gate[0], m_w_up[0], m_w_down[0]]
    v_list = [v_g_mix, v_w_in[0], v_conv_w[0], v_g_q, v_g_k, v_g_conv_out, v_g_attn_out, v_w_out[0], v_g_ffn, v_w_gate[0], v_w_up[0], v_w_down[0]]
    stacked = {"w_in", "conv_w", "w_out", "w_gate", "w_up", "w_down"}
    out_g, out_d, out_m, out_v = [], [], [], []
    for name, w, g, m, v in zip(names, weights, grad_list, m_list, v_list):
        delta, new_m, new_v = _adamw("adamw_" + name, w, g, m, v)
        lift = (lambda t: t[None]) if name in stacked else (lambda t: t)
        out_g.append(lift(g))
        out_d.append(lift(delta))
        out_m.append(lift(new_m))
        out_v.append(lift(new_v))
    return (loss, grad_x[None], *out_g, *out_d, *out_m, *out_v)
```

```python
import jax
import jax.numpy as jnp
import numpy as np
from jax import lax
from jax.experimental import pallas as pl
from jax.experimental.pallas import tpu as pltpu

F32 = jnp.float32
BF16 = jnp.bfloat16

D_MODEL = 1024
D_CONV = 512
D_ATTN = 512
D_IN = 3072
D_FF = 2816
HEAD_DIM = 64
N_PAIRS = 4
BLOCK = 128
DILATIONS = (1, 4, 16)
N_DEV = 8
EPS = 1e-6
NEG = -1e30

ADAM_LR = 0.001
ADAM_B1 = 0.9
ADAM_B2 = 0.999
ADAM_EPS = 1e-08
ADAM_WD = 0.01
ADAM_STEP = 10

NT = (((1,), (1,)), ((), ()))
NN = (((1,), (0,)), ((), ()))
TN = (((0,), (0,)), ((), ()))
MESH = pl.DeviceIdType.MESH

VMEM_LIMIT = 56 * 1024 * 1024


def _params(semantics=None, vmem=None):
    kw = {}
    if semantics is not None:
        kw["dimension_semantics"] = semantics
    if vmem is not None:
        kw["vmem_limit_bytes"] = vmem
    return pltpu.CompilerParams(**kw)


def _dot(a, b, dims):
    return lax.dot_general(a, b, dims, preferred_element_type=F32)


def _const_spec(shape):
    n = len(shape)
    return pl.BlockSpec(shape, lambda *_: (0,) * n, pipeline_mode=pl.Buffered(1))


def _rms(x):
    r = lax.rsqrt(jnp.mean(x * x, axis=-1, keepdims=True) + EPS)
    return x * r, r


def _rms_bwd(dy, xh, r, g):
    t = dy * g
    return r * (t - xh * jnp.mean(t * xh, axis=-1, keepdims=True))


def _head_blockdiag(scale):
    i = lax.broadcasted_iota(jnp.int32, (128, 128), 0) // HEAD_DIM
    j = lax.broadcasted_iota(jnp.int32, (128, 128), 1) // HEAD_DIM
    return jnp.where(i == j, scale, 0.0).astype(F32)


def _head_sum(x, bd):
    return lax.dot_general(x, bd, NN, precision=lax.Precision.HIGHEST, preferred_element_type=F32)


def _place():
    return lax.axis_index("x"), lax.axis_index("y"), lax.axis_index("c")


def _all_gather_weights(parts, out_dtypes):
    n = len(parts)

    def body(*refs):
        ins, outs = refs[:n], refs[n:2 * n]
        send_sems, recv_sems = refs[2 * n], refs[2 * n + 1]
        x, y, c = _place()
        me, sibling = (x, y, c), (x, y, 1 - c)
        chips = [(x, 1 - y), (1 - x, y), (1 - x, 1 - y)]

        def index(p):
            return 4 * p[0] + 2 * p[1] + p[2]

        for a in range(n):
            outs[a][index(me)] = ins[a][...].astype(outs[a].dtype)

        def copy(a, k, block, to):
            blk = outs[a].at[index(block)]
            return pltpu.make_async_remote_copy(
                src_ref=blk, dst_ref=blk, send_sem=send_sems.at[7 * a + k], recv_sem=recv_sems.at[7 * a + k],
                device_id=to, device_id_type=MESH)

        first = []
        for a in range(n):
            first.append(copy(a, 0, me, sibling))
            first += [copy(a, 1 + j, me, (*chip, c)) for j, chip in enumerate(chips)]
        for cp in first:
            cp.start()
        passed = []
        for a in range(n):
            for j, chip in enumerate(chips):
                copy(a, 1 + j, (*chip, c), me).wait_recv()
                fwd = copy(a, 4 + j, (*chip, c), sibling)
                fwd.start()
                passed.append(fwd)
        for a in range(n):
            copy(a, 0, sibling, me).wait_recv()
            for j, chip in enumerate(chips):
                copy(a, 4 + j, (*chip, 1 - c), me).wait_recv()
        for cp in first + passed:
            cp.wait_send()

    vm = pl.BlockSpec(memory_space=pltpu.VMEM)
    return pl.pallas_call(
        body, name="weight_all_gather",
        out_shape=[jax.ShapeDtypeStruct((N_DEV,) + p.shape, dt) for p, dt in zip(parts, out_dtypes)],
        in_specs=[vm] * n, out_specs=[vm] * n,
        scratch_shapes=[pltpu.SemaphoreType.DMA((7 * n,)), pltpu.SemaphoreType.DMA((7 * n,))],
        compiler_params=_params(vmem=VMEM_LIMIT),
    )(*parts)


def _all_reduce_small(v):
    def body(v_ref, out_ref, gathered, send_sems, recv_sems):
        x, y, c = _place()
        mine = 4 * x + 2 * y + c
        gathered[mine] = v_ref[...]

        def copy(k):
            peer = (1 - x if k & 4 else x, 1 - y if k & 2 else y, 1 - c if k & 1 else c)
            return pltpu.make_async_remote_copy(
                src_ref=gathered.at[mine], dst_ref=gathered.at[mine],
                send_sem=send_sems.at[k - 1], recv_sem=recv_sems.at[k - 1], device_id=peer, device_id_type=MESH)

        copies = [copy(k) for k in range(1, N_DEV)]
        for cp in copies:
            cp.start()
        for cp in copies:
            cp.wait_recv()
        for cp in copies:
            cp.wait_send()
        total = gathered[0]
        for d in range(1, N_DEV):
            total = total + gathered[d]
        out_ref[...] = total

    vm = pl.BlockSpec(memory_space=pltpu.VMEM)
    return pl.pallas_call(
        body, name="small_all_reduce",
        out_shape=jax.ShapeDtypeStruct(v.shape, F32),
        in_specs=[vm], out_specs=vm,
        scratch_shapes=[pltpu.VMEM((N_DEV,) + v.shape, F32),
                        pltpu.SemaphoreType.DMA((N_DEV - 1,)), pltpu.SemaphoreType.DMA((N_DEV - 1,))],
    )(v)


def _reduce_scatter_cores(grads):
    n = len(grads)

    def body(*refs):
        ins, outs = refs[:n], refs[n:2 * n]
        send_sems, recv_sems = refs[2 * n], refs[2 * n + 1]
        x, y, c = _place()
        copies = []
        for a in range(n):
            for k in range(4):
                copies.append(pltpu.make_async_remote_copy(
                    src_ref=ins[a].at[k, 1 - c], dst_ref=outs[a].at[k],
                    send_sem=send_sems.at[4 * a + k], recv_sem=recv_sems.at[4 * a + k],
                    device_id=(x, y, 1 - c), device_id_type=MESH))
        for cp in copies:
            cp.start()
        for cp in copies:
            cp.wait_recv()
        for cp in copies:
            cp.wait_send()

    hbm = pl.BlockSpec(memory_space=pl.ANY)
    return pl.pallas_call(
        body, name="grad_reduce_scatter_cores",
        out_shape=[jax.ShapeDtypeStruct((4,) + g.shape[2:], g.dtype) for g in grads],
        in_specs=[hbm] * n, out_specs=[hbm] * n,
        scratch_shapes=[pltpu.SemaphoreType.DMA((4 * n,)), pltpu.SemaphoreType.DMA((4 * n,))],
    )(*grads)


def _reduce_scatter_chips(partials):
    n = len(partials)

    def body(*refs):
        ins, outs = refs[:n], refs[n:2 * n]
        send_sems, recv_sems = refs[2 * n], refs[2 * n + 1]
        x, y, c = _place()
        copies = []
        for a in range(n):
            for j in (1, 2, 3):
                px = 1 - x if j & 2 else x
                py = 1 - y if j & 1 else y
                copies.append(pltpu.make_async_remote_copy(
                    src_ref=ins[a].at[2 * px + py], dst_ref=outs[a].at[j - 1],
                    send_sem=send_sems.at[3 * a + j - 1], recv_sem=recv_sems.at[3 * a + j - 1],
                    device_id=(px, py, c), device_id_type=MESH))
        for cp in copies:
            cp.start()
        for cp in copies:
            cp.wait_recv()
        for cp in copies:
            cp.wait_send()

    hbm = pl.BlockSpec(memory_space=pl.ANY)
    return pl.pallas_call(
        body, name="grad_reduce_scatter_chips",
        out_shape=[jax.ShapeDtypeStruct((3,) + p.shape[1:], p.dtype) for p in partials],
        in_specs=[hbm] * n, out_specs=[hbm] * n,
        scratch_shapes=[pltpu.SemaphoreType.DMA((3 * n,)), pltpu.SemaphoreType.DMA((3 * n,))],
    )(*partials)


def _add_core_partials(core, grads, received):
    n = len(grads)

    def body(core_ref, *refs):
        del core_ref
        gs, rs, outs = refs[:n], refs[n:2 * n], refs[2 * n:]
        for a in range(n):
            outs[a][...] = (gs[a][...].astype(F32) + rs[a][...].astype(F32)).astype(outs[a].dtype)

    in_specs = [pl.BlockSpec((None, None) + g.shape[2:], lambda k, cref: (k, cref[0], 0, 0)) for g in grads]
    in_specs += [pl.BlockSpec((None,) + r.shape[1:], lambda k, cref: (k, 0, 0)) for r in received]
    out_specs = [pl.BlockSpec((None,) + r.shape[1:], lambda k, cref: (k, 0, 0)) for r in received]
    return pl.pallas_call(
        body, name="grad_add_core_partials",
        grid_spec=pltpu.PrefetchScalarGridSpec(num_scalar_prefetch=1, grid=(4,), in_specs=in_specs, out_specs=out_specs),
        out_shape=[jax.ShapeDtypeStruct(r.shape, r.dtype) for r in received],
        compiler_params=_params(("arbitrary",)),
    )(core, *grads, *received)


def _add_chip_partials(chip, partials, received):
    n = len(partials)

    def body(chip_ref, *refs):
        del chip_ref
        ps, rs, outs = refs[:n], refs[n:2 * n], refs[2 * n:]
        for a in range(n):
            outs[a][...] = ((ps[a][...].astype(F32) + rs[a][0].astype(F32)) + rs[a][1].astype(F32)) + rs[a][2].astype(F32)

    in_specs = [pl.BlockSpec((None,) + p.shape[1:], lambda i, cref: (cref[0], 0, 0)) for p in partials]
    in_specs += [pl.BlockSpec(r.shape, lambda i, cref: (0, 0, 0)) for r in received]
    out_specs = [pl.BlockSpec(p.shape[1:], lambda i, cref: (0, 0)) for p in partials]
    return pl.pallas_call(
        body, name="grad_add_chip_partials",
        grid_spec=pltpu.PrefetchScalarGridSpec(num_scalar_prefetch=1, grid=(1,), in_specs=in_specs, out_specs=out_specs),
        out_shape=[jax.ShapeDtypeStruct(p.shape[1:], F32) for p in partials],
        compiler_params=_params(("arbitrary",), vmem=VMEM_LIMIT),
    )(chip, *partials, *received)


def _in_projection(x, g_mix, w_in_t, tm):
    s = x.shape[0]

    def body(x_ref, g_ref, w_ref, h_ref, z_ref):
        xh, _ = _rms(x_ref[...])
        h = (xh * g_ref[...]).astype(BF16)
        h_ref[...] = h
        for n0 in range(0, D_IN, 512):
            z_ref[:, n0:n0 + 512] = _dot(h, w_ref[n0:n0 + 512, :], NT).astype(BF16)

    return pl.pallas_call(
        body, name="in_projection", grid=(s // tm,),
        in_specs=[pl.BlockSpec((tm, D_MODEL), lambda i: (i, 0)), _const_spec((1, D_MODEL)), _const_spec((D_IN, D_MODEL))],
        out_specs=[pl.BlockSpec((tm, D_MODEL), lambda i: (i, 0)), pl.BlockSpec((tm, D_IN), lambda i: (i, 0))],
        out_shape=[jax.ShapeDtypeStruct((s, D_MODEL), BF16), jax.ShapeDtypeStruct((s, D_IN), BF16)],
        compiler_params=_params(("parallel",), vmem=VMEM_LIMIT),
    )(x, g_mix, w_in_t)


RESIDUES = 16
CHUNK = 512
GROUP = 4


def _branch_geometry(d, seg):
    n_runs = RESIDUES // d
    w = BLOCK // n_runs
    blocks = seg // w
    group = min(GROUP, blocks)
    return n_runs, w, blocks // group, group


def _permuted_masks():
    out = []
    for d in DILATIONS:
        n_runs = RESIDUES // d
        w = BLOCK // n_runs
        p = np.arange(BLOCK)
        pos = (p % w) * n_runs + p // w
        dist = pos[:, None] - np.concatenate([pos - BLOCK, pos])[None, :]
        band = (dist >= 0) & (dist <= BLOCK)
        first = band & (np.arange(2 * BLOCK)[None, :] >= BLOCK)
        both = [np.where(m, 0.0, NEG).astype(np.float32) for m in (band, first)]
        out.append(np.stack([np.concatenate([m, m], axis=0) for m in both]))
    return jnp.asarray(np.stack(out))


def _run_starts(u, grp, d, seg, n_runs, w, group):
    return [(u + d * q) * seg + (w * group) * grp for q in range(n_runs)]


def _tile_rows(start, w, b):
    off = start + w * b
    if b < 0:
        off = jnp.maximum(off, 0)
    return pl.ds(pl.multiple_of(off, w), w)


def _load_tile(ref, starts, w, b):
    parts = [ref[_tile_rows(st, w, b), :] for st in starts]
    return parts[0] if len(parts) == 1 else jnp.concatenate(parts, axis=0)


def _store_tile(ref, starts, w, b, tile):
    for i, st in enumerate(starts):
        ref[_tile_rows(st, w, b), :] = tile[i * w:(i + 1) * w]


def _add_tile(ref, starts, w, b, tile):
    for i, st in enumerate(starts):
        ref[_tile_rows(st, w, b), :] += tile[i * w:(i + 1) * w]


def _to_residue_major(dst, src, c, seg):
    per = CHUNK // RESIDUES
    for r in range(RESIDUES):
        dst[pl.ds(pl.multiple_of(r * seg + c * per, per), per), :] = src[pl.ds(r, per, stride=RESIDUES), :]


def _from_residue_major(dst, dst_start, src, c, seg):
    per = CHUNK // RESIDUES
    for r in range(RESIDUES):
        dst[pl.ds(dst_start + r, per, stride=RESIDUES), :] = src[pl.ds(pl.multiple_of(r * seg + c * per, per), per), :]


def _stack_heads(t, head0):
    return jnp.concatenate([jnp.where(head0, t, 0.0), jnp.where(head0, 0.0, t)], axis=0).astype(BF16)


def _attention_forward(z, gq2, gk2, masks):
    s = z.shape[0]
    seg = s // RESIDUES

    def body(zq_ref, zk_ref, zv_ref, gq_ref, gk_ref, mask_ref, o_ref, lse_ref, qf, kf, vf, o_st, l_st, tq, tk, tv):
        bd = _head_blockdiag(1.0 / HEAD_DIM)
        gq = gq_ref[...] * (HEAD_DIM ** -0.5)
        gk = gk_ref[...]

        def prep(c, carry):
            rows = pl.ds(pl.multiple_of(c * CHUNK, CHUNK), CHUNK)
            zq = zq_ref[rows, :].astype(F32)
            zk = zk_ref[rows, :].astype(F32)
            tq[...] = (zq * lax.rsqrt(_head_sum(zq * zq, bd) + EPS)) * gq
            tk[...] = (zk * lax.rsqrt(_head_sum(zk * zk, bd) + EPS)) * gk
            tv[...] = zv_ref[rows, :].astype(F32)
            _to_residue_major(qf, tq, c, seg)
            _to_residue_major(kf, tk, c, seg)
            _to_residue_major(vf, tv, c, seg)
            return carry

        lax.fori_loop(0, s // CHUNK, prep, 0)

        head0 = lax.broadcasted_iota(jnp.int32, (BLOCK, 128), 1) < HEAD_DIM

        def run_branch(branch, d):
            n_runs, w, steps, group = _branch_geometry(d, seg)

            def step(u, grp):
                starts = _run_starts(u, grp, d, seg, n_runs, w, group)
                kt = [_load_tile(kf, starts, w, b).astype(BF16) for b in range(-1, group)]
                vt = [_load_tile(vf, starts, w, b).astype(BF16) for b in range(-1, group)]
                fresh = []
                for b in range(group):
                    q2 = _stack_heads(_load_tile(qf, starts, w, b), head0)
                    kc = jnp.concatenate([kt[b], kt[b + 1]], axis=0)
                    vc = jnp.concatenate([vt[b], vt[b + 1]], axis=0)
                    mask = mask_ref[branch, 0]
                    if b == 0:
                        mask = jnp.where(grp == 0, mask_ref[branch, 1], mask)
                    sc = _dot(q2, kc, NT) + mask
                    m = jnp.max(sc, axis=-1, keepdims=True)
                    e = jnp.exp(sc - m)
                    den = jnp.sum(e, axis=-1, keepdims=True)
                    o2 = _dot((e * (1.0 / den)).astype(BF16), vc, NN)
                    l2 = m + jnp.log(den)
                    fresh.append((jnp.where(head0, o2[:BLOCK], o2[BLOCK:]), jnp.where(head0, l2[:BLOCK], l2[BLOCK:])))
                for b in range(group):
                    o_new, l_new = fresh[b]
                    if branch > 0:
                        l_old = _load_tile(l_st, starts, w, b)
                        top = jnp.maximum(l_old, l_new)
                        e_old = jnp.exp(l_old - top)
                        e_new = jnp.exp(l_new - top)
                        tot = e_old + e_new
                        inv = 1.0 / tot
                        o_new = _load_tile(o_st, starts, w, b) * (e_old * inv) + o_new * (e_new * inv)
                        l_new = top + jnp.log(tot)
                    _store_tile(o_st, starts, w, b, o_new)
                    _store_tile(l_st, starts, w, b, l_new)

            def unit(u, carry):
                def inner(grp, carry2):
                    step(u, grp)
                    return carry2
                return lax.fori_loop(0, steps, inner, carry)

            lax.fori_loop(0, d, unit, 0)

        for branch, d in enumerate(DILATIONS):
            run_branch(branch, d)

        def finish(c, carry):
            _from_residue_major(o_ref, c * CHUNK, o_st, c, seg)
            _from_residue_major(lse_ref, c * CHUNK, l_st, c, seg)
            return carry

        lax.fori_loop(0, s // CHUNK, finish, 0)

    def col(c0):
        return pl.BlockSpec((s, 128), lambda p: (0, c0 + p), pipeline_mode=pl.Buffered(1))

    return pl.pallas_call(
        body, name="attention_forward", grid=(N_PAIRS,),
        in_specs=[col(12), col(16), col(20), _const_spec((1, 128)), _const_spec((1, 128)), _const_spec(masks.shape)],
        out_specs=[pl.BlockSpec((s, 128), lambda p: (0, p)), pl.BlockSpec((s, 128), lambda p: (0, p))],
        out_shape=[jax.ShapeDtypeStruct((s, D_ATTN), F32), jax.ShapeDtypeStruct((s, D_ATTN), F32)],
        scratch_shapes=[pltpu.VMEM((s, 128), F32)] * 5 + [pltpu.VMEM((CHUNK, 128), F32)] * 3,
        compiler_params=_params(("arbitrary",), vmem=VMEM_LIMIT),
    )(z, z, z, gq2, gk2, masks)


def _conv_forward(zt, zprev, w, first):
    u, gb, gc = zt[:, :D_CONV], zt[:, D_CONV:2 * D_CONV], zt[:, 2 * D_CONV:]
    cu = gc * u
    cu_prev = jnp.where(first, 0.0, zprev[:, 2 * D_CONV:] * zprev[:, :D_CONV])
    row = lax.broadcasted_iota(jnp.int32, cu.shape, 0)
    cu1 = jnp.where(row >= 1, pltpu.roll(cu, 1, 0), cu_prev[7:8, :])
    cu2 = jnp.where(row >= 2, pltpu.roll(cu, 2, 0), jnp.where(row == 1, cu_prev[7:8, :], cu_prev[6:7, :]))
    cv = w[0:1, :] * cu2 + w[1:2, :] * cu1 + w[2:3, :] * cu
    return u, gb, gc, cu, cu1, cu2, cv


def _mix_out(z, y_attn, x, conv_w, g_conv_out, g_attn_out, g_ffn, w_out, tm):
    s = x.shape[0]
    hb = tm // 8

    def body(z_ref, zp_ref, ya_ref, x_ref, cw_ref, gc_ref, ga_ref, gf_ref, w_ref, mix_ref, x2_ref, h2_ref):
        first = pl.program_id(0) == 0
        zt = z_ref[...].astype(F32)
        _, gb, _, _, _, _, cv = _conv_forward(zt, zp_ref[...].astype(F32), cw_ref[...], first)
        nc, _ = _rms(gb * cv)
        na, _ = _rms(ya_ref[...])
        mix = jnp.concatenate([nc * gc_ref[...], na * ga_ref[...]], axis=-1).astype(BF16)
        mix_ref[...] = mix
        x2 = x_ref[...] + _dot(mix, w_ref[...], NN)
        x2_ref[...] = x2
        xh, _ = _rms(x2)
        h2_ref[...] = (xh * gf_ref[...]).astype(BF16)

    tile = lambda w: pl.BlockSpec((tm, w), lambda i: (i, 0))
    return pl.pallas_call(
        body, name="mix_out", grid=(s // tm,),
        in_specs=[tile(3 * D_CONV), pl.BlockSpec((8, 3 * D_CONV), lambda i: (jnp.maximum(i * hb - 1, 0), 0)),
                  tile(D_ATTN), tile(D_MODEL), _const_spec((8, D_CONV)), _const_spec((1, D_CONV)), _const_spec((1, D_ATTN)),
                  _const_spec((1, D_MODEL)), _const_spec((D_MODEL, D_MODEL))],
        out_specs=[tile(D_MODEL)] * 3,
        out_shape=[jax.ShapeDtypeStruct((s, D_MODEL), BF16), jax.ShapeDtypeStruct((s, D_MODEL), F32),
                   jax.ShapeDtypeStruct((s, D_MODEL), BF16)],
        compiler_params=_params(("arbitrary",), vmem=VMEM_LIMIT),
    )(z, z, y_attn, x, conv_w, g_conv_out, g_attn_out, g_ffn, w_out)


FF_CHUNK = 256


def _ffn_forward(h2, x2, target, w_gate_t, w_up_t, w_down, tm):
    s = h2.shape[0]

    def body(h_ref, x2_ref, t_ref, wg_ref, wu_ref, wd_ref, a_ref, b_ref, dy_ref, loss_ref):
        @pl.when(pl.program_id(0) == 0)
        def _():
            loss_ref[...] = jnp.zeros_like(loss_ref)

        h = h_ref[...]
        y = x2_ref[...]
        for c0 in range(0, D_FF, FF_CHUNK):
            rows = slice(c0, c0 + FF_CHUNK)
            a = _dot(h, wg_ref[rows, :], NT)
            b = _dot(h, wu_ref[rows, :], NT)
            a_ref[:, rows] = a.astype(BF16)
            b_ref[:, rows] = b.astype(BF16)
            f = (a * jax.nn.sigmoid(a)) * b
            y = y + _dot(f.astype(BF16), wd_ref[rows, :], NN)
        err = y - t_ref[...]
        dy_ref[...] = err * (1.0 / D_MODEL)
        loss_ref[...] += jnp.sum(err * err)

    tile = lambda w: pl.BlockSpec((tm, w), lambda i: (i, 0))
    return pl.pallas_call(
        body, name="ffn_forward", grid=(s // tm,),
        in_specs=[tile(D_MODEL), tile(D_MODEL), tile(D_MODEL)] + [_const_spec((D_FF, D_MODEL))] * 3,
        out_specs=[tile(D_FF), tile(D_FF), tile(D_MODEL), pl.BlockSpec((8, 128), lambda i: (0, 0))],
        out_shape=[jax.ShapeDtypeStruct((s, D_FF), BF16), jax.ShapeDtypeStruct((s, D_FF), BF16),
                   jax.ShapeDtypeStruct((s, D_MODEL), F32), jax.ShapeDtypeStruct((8, 128), F32)],
        compiler_params=_params(("arbitrary",), vmem=VMEM_LIMIT),
    )(h2, x2, target, w_gate_t, w_up_t, w_down)


def _ffn_backward(dy, a, b, x2, g_ffn, w_gate_t, w_up_t, w_down, tm):
    s = dy.shape[0]

    def body(dy_ref, a_ref, b_ref, x2_ref, g_ref, wg_ref, wu_ref, wd_ref, da_ref, db_ref, f_ref, dx2_ref, dg_ref):
        @pl.when(pl.program_id(0) == 0)
        def _():
            dg_ref[...] = jnp.zeros_like(dg_ref)

        dy_f = dy_ref[...]
        dyb = dy_f.astype(BF16)
        dh = jnp.zeros((tm, D_MODEL), F32)
        for c0 in range(0, D_FF, FF_CHUNK):
            rows = slice(c0, c0 + FF_CHUNK)
            df = _dot(dyb, wd_ref[rows, :], NT)
            av = a_ref[:, rows].astype(F32)
            bv = b_ref[:, rows].astype(F32)
            sig = jax.nn.sigmoid(av)
            silu = av * sig
            f_ref[:, rows] = (silu * bv).astype(BF16)
            da = (df * bv * (sig * (1.0 + av * (1.0 - sig)))).astype(BF16)
            db = (df * silu).astype(BF16)
            da_ref[:, rows] = da
            db_ref[:, rows] = db
            dh = dh + _dot(da, wg_ref[rows, :], NN) + _dot(db, wu_ref[rows, :], NN)
        xh, r = _rms(x2_ref[...])
        dg_ref[...] += jnp.sum(dh * xh, axis=0, keepdims=True)
        dx2_ref[...] = dy_f + _rms_bwd(dh, xh, r, g_ref[...])

    tile = lambda w: pl.BlockSpec((tm, w), lambda i: (i, 0))
    return pl.pallas_call(
        body, name="ffn_backward", grid=(s // tm,),
        in_specs=[tile(D_MODEL), tile(D_FF), tile(D_FF), tile(D_MODEL), _const_spec((1, D_MODEL))]
        + [_const_spec((D_FF, D_MODEL))] * 3,
        out_specs=[tile(D_FF), tile(D_FF), tile(D_FF), tile(D_MODEL), pl.BlockSpec((1, D_MODEL), lambda i: (0, 0))],
        out_shape=[jax.ShapeDtypeStruct((s, D_FF), BF16)] * 3
        + [jax.ShapeDtypeStruct((s, D_MODEL), F32), jax.ShapeDtypeStruct((1, D_MODEL), F32)],
        compiler_params=_params(("arbitrary",), vmem=VMEM_LIMIT),
    )(dy, a, b, x2, g_ffn, w_gate_t, w_up_t, w_down)


def _out_backward(dx2, z, y_attn, conv_w, g_conv_out, g_attn_out, w_out, tm):
    s = dx2.shape[0]
    hb = tm // 8

    def body(dx2_ref, z_ref, zp_ref, ya_ref, cw_ref, gc_ref, ga_ref, w_ref, dya_ref, dgb_ref, dcv_ref, dgc_ref, dga_ref):
        first = pl.program_id(0) == 0

        @pl.when(first)
        def _():
            dgc_ref[...] = jnp.zeros_like(dgc_ref)
            dga_ref[...] = jnp.zeros_like(dga_ref)

        dmix = _dot(dx2_ref[...].astype(BF16), w_ref[...], NT)
        _, gb, _, _, _, _, cv = _conv_forward(z_ref[...].astype(F32), zp_ref[...].astype(F32), cw_ref[...], first)
        ych, rc = _rms(gb * cv)
        dnc = dmix[:, :D_CONV]
        dgc_ref[...] += jnp.sum(dnc * ych, axis=0, keepdims=True)
        dyc = _rms_bwd(dnc, ych, rc, gc_ref[...])
        dgb_ref[...] = (dyc * cv).astype(BF16)
        dcv_ref[...] = dyc * gb
        yah, ra = _rms(ya_ref[...])
        dna = dmix[:, D_CONV:]
        dga_ref[...] += jnp.sum(dna * yah, axis=0, keepdims=True)
        dya_ref[...] = _rms_bwd(dna, yah, ra, ga_ref[...])

    tile = lambda w: pl.BlockSpec((tm, w), lambda i: (i, 0))
    vec = pl.BlockSpec((1, D_CONV), lambda i: (0, 0))
    return pl.pallas_call(
        body, name="out_backward", grid=(s // tm,),
        in_specs=[tile(D_MODEL), tile(3 * D_CONV), pl.BlockSpec((8, 3 * D_CONV), lambda i: (jnp.maximum(i * hb - 1, 0), 0)),
                  tile(D_ATTN), _const_spec((8, D_CONV)), _const_spec((1, D_CONV)), _const_spec((1, D_ATTN)),
                  _const_spec((D_MODEL, D_MODEL))],
        out_specs=[tile(D_ATTN), tile(D_CONV), tile(D_CONV), vec, vec],
        out_shape=[jax.ShapeDtypeStruct((s, D_ATTN), F32), jax.ShapeDtypeStruct((s, D_CONV), BF16),
                   jax.ShapeDtypeStruct((s, D_CONV), F32), jax.ShapeDtypeStruct((1, D_CONV), F32),
                   jax.ShapeDtypeStruct((1, D_ATTN), F32)],
        compiler_params=_params(("arbitrary",), vmem=VMEM_LIMIT),
    )(dx2, z, z, y_attn, conv_w, g_conv_out, g_attn_out, w_out)


def _attention_backward(z, o, lse, dya, gq2, gk2, masks):
    s = z.shape[0]
    seg = s // RESIDUES

    def body(z_hbm, o_hbm, lse_hbm, do_hbm, gq_ref, gk_ref, mask_ref, dzq_ref, dzk_ref, dzv_ref, dgq_ref, dgk_ref,
             qf, kf, vf, dof, stats, dq, dk, dv, t0, t1, t2, zbuf, fbuf, sem):
        pair = pl.program_id(0)
        bd_mean = _head_blockdiag(1.0 / HEAD_DIM)
        bd_sum = _head_blockdiag(1.0)
        gq = gq_ref[...] * (HEAD_DIM ** -0.5)
        gk = gk_ref[...]
        lane = lax.broadcasted_iota(jnp.int32, (CHUNK, 128), 1) % HEAD_DIM

        def fetch_z(col, rows):
            cp = pltpu.make_async_copy(z_hbm.at[rows, pl.ds(pl.multiple_of((col + pair) * 128, 128), 128)], zbuf, sem)
            cp.start()
            cp.wait()
            return zbuf[...].astype(F32)

        def fetch_f(src, rows):
            cp = pltpu.make_async_copy(src.at[rows, pl.ds(pl.multiple_of(pair * 128, 128), 128)], fbuf, sem)
            cp.start()
            cp.wait()
            return fbuf[...]

        def prep(c, carry):
            rows = pl.ds(pl.multiple_of(c * CHUNK, CHUNK), CHUNK)
            zq = fetch_z(12, rows)
            t0[...] = (zq * lax.rsqrt(_head_sum(zq * zq, bd_mean) + EPS)) * gq
            zk = fetch_z(16, rows)
            t1[...] = (zk * lax.rsqrt(_head_sum(zk * zk, bd_mean) + EPS)) * gk
            t2[...] = fetch_z(20, rows)
            _to_residue_major(qf, t0, c, seg)
            _to_residue_major(kf, t1, c, seg)
            _to_residue_major(vf, t2, c, seg)
            do = fetch_f(do_hbm, rows)
            t0[...] = do
            delta = _head_sum(do * fetch_f(o_hbm, rows), bd_sum)
            lse_w = fetch_f(lse_hbm, rows)
            t1[...] = jnp.where(lane == 0, lse_w, jnp.where(lane == 1, delta, 0.0))
            _to_residue_major(dof, t0, c, seg)
            _to_residue_major(stats, t1, c, seg)
            zero = jnp.zeros((CHUNK, 128), F32)
            dq[rows, :] = zero
            dk[rows, :] = zero
            dv[rows, :] = zero
            return carry

        lax.fori_loop(0, s // CHUNK, prep, 0)

        head0 = lax.broadcasted_iota(jnp.int32, (BLOCK, 128), 1) < HEAD_DIM

        def run_branch(branch, d):
            n_runs, w, steps, group = _branch_geometry(d, seg)

            def step(u, grp):
                starts = _run_starts(u, grp, d, seg, n_runs, w, group)
                kt = [_load_tile(kf, starts, w, b).astype(BF16) for b in range(-1, group)]
                vt = [_load_tile(vf, starts, w, b).astype(BF16) for b in range(-1, group)]
                dk_t = [jnp.zeros((BLOCK, 128), F32)] * (group + 1)
                dv_t = [jnp.zeros((BLOCK, 128), F32)] * (group + 1)
                dq_t = []
                for b in range(group):
                    q2 = _stack_heads(_load_tile(qf, starts, w, b), head0)
                    do2 = _stack_heads(_load_tile(dof, starts, w, b), head0)
                    st = _load_tile(stats, starts, w, b)
                    lse2 = jnp.concatenate([st[:, 0:1], st[:, HEAD_DIM:HEAD_DIM + 1]], axis=0)
                    delta2 = jnp.concatenate([st[:, 1:2], st[:, HEAD_DIM + 1:HEAD_DIM + 2]], axis=0)
                    kc = jnp.concatenate([kt[b], kt[b + 1]], axis=0)
                    vc = jnp.concatenate([vt[b], vt[b + 1]], axis=0)
                    mask = mask_ref[branch, 0]
                    if b == 0:
                        mask = jnp.where(grp == 0, mask_ref[branch, 1], mask)
                    p = jnp.exp(_dot(q2, kc, NT) + mask - lse2)
                    ds = (p * (_dot(do2, vc, NT) - delta2)).astype(BF16)
                    dq2 = _dot(ds, kc, NN)
                    dq_t.append(jnp.where(head0, dq2[:BLOCK], dq2[BLOCK:]))
                    dkc = _dot(ds, q2, TN)
                    dvc = _dot(p.astype(BF16), do2, TN)
                    dk_t[b] = dk_t[b] + dkc[:BLOCK]
                    dk_t[b + 1] = dk_t[b + 1] + dkc[BLOCK:]
                    dv_t[b] = dv_t[b] + dvc[:BLOCK]
                    dv_t[b + 1] = dv_t[b + 1] + dvc[BLOCK:]
                for b in range(group):
                    _add_tile(dq, starts, w, b, dq_t[b])
                for b in range(-1, group):
                    _add_tile(dk, starts, w, b, dk_t[b + 1])
                    _add_tile(dv, starts, w, b, dv_t[b + 1])

            def unit(u, carry):
                def inner(grp, carry2):
                    step(u, grp)
                    return carry2
                return lax.fori_loop(0, steps, inner, carry)

            lax.fori_loop(0, d, unit, 0)

        for branch, d in enumerate(DILATIONS):
            run_branch(branch, d)

        def finish(c, carry):
            acc_q, acc_k = carry
            rows = pl.ds(pl.multiple_of(c * CHUNK, CHUNK), CHUNK)
            _from_residue_major(t0, 0, dq, c, seg)
            _from_residue_major(t1, 0, dk, c, seg)
            _from_residue_major(t2, 0, dv, c, seg)
            zq = fetch_z(12, rows)
            rq = lax.rsqrt(_head_sum(zq * zq, bd_mean) + EPS)
            qh = zq * rq
            dqn = t0[...]
            acc_q = acc_q + jnp.sum(dqn * qh, axis=0, keepdims=True)
            t = dqn * gq
            dzq_ref[rows, :] = (rq * (t - qh * _head_sum(t * qh, bd_mean))).astype(BF16)
            zk = fetch_z(16, rows)
            rk = lax.rsqrt(_head_sum(zk * zk, bd_mean) + EPS)
            kh = zk * rk
            dkn = t1[...]
            acc_k = acc_k + jnp.sum(dkn * kh, axis=0, keepdims=True)
            t = dkn * gk
            dzk_ref[rows, :] = (rk * (t - kh * _head_sum(t * kh, bd_mean))).astype(BF16)
            dzv_ref[rows, :] = t2[...].astype(BF16)
            return acc_q, acc_k

        zero = jnp.zeros((1, 128), F32)
        acc_q, acc_k = lax.fori_loop(0, s // CHUNK, finish, (zero, zero))
        dgq_ref[...] = acc_q * (HEAD_DIM ** -0.5)
        dgk_ref[...] = acc_k

    hbm = pl.BlockSpec(memory_space=pl.ANY)
    gain = pl.BlockSpec((None, 1, 128), lambda p: (p, 0, 0))
    dz_spec = pl.BlockSpec((s, 128), lambda p: (0, p))
    return pl.pallas_call(
        body, name="attention_backward", grid=(N_PAIRS,),
        in_specs=[hbm, hbm, hbm, hbm, _const_spec((1, 128)), _const_spec((1, 128)), _const_spec(masks.shape)],
        out_specs=[dz_spec, dz_spec, dz_spec, gain, gain],
        out_shape=[jax.ShapeDtypeStruct((s, D_ATTN), BF16)] * 3 + [jax.ShapeDtypeStruct((N_PAIRS, 1, 128), F32)] * 2,
        scratch_shapes=[pltpu.VMEM((s, 128), F32)] * 8 + [pltpu.VMEM((CHUNK, 128), F32)] * 3
        + [pltpu.VMEM((CHUNK, 128), BF16), pltpu.VMEM((CHUNK, 128), F32), pltpu.SemaphoreType.DMA],
        compiler_params=_params(("arbitrary",), vmem=VMEM_LIMIT),
    )(z, o, lse, dya, gq2, gk2, masks)


def _in_backward(dcv, dgb, dzq, dzk, dzv, z, x, dx2, g_mix, conv_w, w_in_t, tm):
    s = x.shape[0]
    hb = tm // 8
    last_halo = s // 8 - 1

    def body(dcv_ref, dcn_ref, dgb_ref, dzq_ref, dzk_ref, dzv_ref, z_ref, zp_ref, x_ref, dx2_ref, g_ref, cw_ref, w_ref,
             dz_ref, dx_ref, dg_ref, dcw_ref):
        i = pl.program_id(0)
        first = i == 0
        last = i == pl.num_programs(0) - 1

        @pl.when(first)
        def _():
            dg_ref[...] = jnp.zeros_like(dg_ref)
            dcw_ref[...] = jnp.zeros_like(dcw_ref)

        w = cw_ref[...]
        u, _, gc, cu, cu1, cu2, _ = _conv_forward(z_ref[...].astype(F32), zp_ref[...].astype(F32), w, first)
        dcv_t = dcv_ref[...]
        nxt = jnp.where(last, 0.0, dcn_ref[...])
        row = lax.broadcasted_iota(jnp.int32, dcv_t.shape, 0)
        up1 = jnp.where(row < tm - 1, pltpu.roll(dcv_t, tm - 1, 0), nxt[0:1, :])
        up2 = jnp.where(row < tm - 2, pltpu.roll(dcv_t, tm - 2, 0), jnp.where(row == tm - 2, nxt[0:1, :], nxt[1:2, :]))
        dcu = w[2:3, :] * dcv_t + w[1:2, :] * up1 + w[0:1, :] * up2
        dcw = jnp.concatenate([jnp.sum(dcv_t * cu2, axis=0, keepdims=True), jnp.sum(dcv_t * cu1, axis=0, keepdims=True),
                               jnp.sum(dcv_t * cu, axis=0, keepdims=True), jnp.zeros((5, D_CONV), F32)], axis=0)
        dcw_ref[...] += dcw
        dz_ref[:, :D_CONV] = (dcu * gc).astype(BF16)
        dz_ref[:, D_CONV:2 * D_CONV] = dgb_ref[...]
        dz_ref[:, 2 * D_CONV:3 * D_CONV] = (dcu * u).astype(BF16)
        dz_ref[:, 3 * D_CONV:3 * D_CONV + D_ATTN] = dzq_ref[...]
        dz_ref[:, 3 * D_CONV + D_ATTN:3 * D_CONV + 2 * D_ATTN] = dzk_ref[...]
        dz_ref[:, 3 * D_CONV + 2 * D_ATTN:] = dzv_ref[...]
        dh = _dot(dz_ref[...], w_ref[...], NN)
        xh, r = _rms(x_ref[...])
        dg_ref[...] += jnp.sum(dh * xh, axis=0, keepdims=True)
        dx_ref[...] = dx2_ref[...] + _rms_bwd(dh, xh, r, g_ref[...])

    tile = lambda w: pl.BlockSpec((tm, w), lambda i: (i, 0))
    return pl.pallas_call(
        body, name="in_backward", grid=(s // tm,),
        in_specs=[tile(D_CONV), pl.BlockSpec((8, D_CONV), lambda i: (jnp.minimum((i + 1) * hb, last_halo), 0)),
                  tile(D_CONV), tile(D_ATTN), tile(D_ATTN), tile(D_ATTN),
                  tile(3 * D_CONV), pl.BlockSpec((8, 3 * D_CONV), lambda i: (jnp.maximum(i * hb - 1, 0), 0)),
                  tile(D_MODEL), tile(D_MODEL), _const_spec((1, D_MODEL)), _const_spec((8, D_CONV)),
                  _const_spec((D_IN, D_MODEL))],
        out_specs=[tile(D_IN), tile(D_MODEL), pl.BlockSpec((1, D_MODEL), lambda i: (0, 0)),
                   pl.BlockSpec((8, D_CONV), lambda i: (0, 0))],
        out_shape=[jax.ShapeDtypeStruct((s, D_IN), BF16), jax.ShapeDtypeStruct((s, D_MODEL), F32),
                   jax.ShapeDtypeStruct((1, D_MODEL), F32), jax.ShapeDtypeStruct((8, D_CONV), F32)],
        compiler_params=_params(("arbitrary",), vmem=VMEM_LIMIT),
    )(dcv, dcv, dgb, dzq, dzk, dzv, z, z, x, dx2, g_mix, conv_w, w_in_t)


def _weight_grad(name, a, b, tn, tk):
    s, n = a.shape
    steps = s // tk

    def body(a_ref, b_ref, out_ref, acc):
        k = pl.program_id(1)

        @pl.when(k == 0)
        def _():
            acc[...] = jnp.zeros_like(acc)

        acc[...] += _dot(a_ref[...], b_ref[...].astype(BF16), TN)

        @pl.when(k == steps - 1)
        def _():
            out_ref[...] = acc[...].astype(BF16)

    return pl.pallas_call(
        body, name=name, grid=(n // tn, steps),
        in_specs=[pl.BlockSpec((tk, tn), lambda j, k: (k, j)), pl.BlockSpec((tk, D_MODEL), lambda j, k: (k, 0))],
        out_specs=pl.BlockSpec((tn, D_MODEL), lambda j, k: (j, 0)),
        out_shape=jax.ShapeDtypeStruct((n, D_MODEL), BF16),
        scratch_shapes=[pltpu.VMEM((tn, D_MODEL), F32)],
        compiler_params=_params(("parallel", "arbitrary"), vmem=VMEM_LIMIT),
    )(a, b)


def _adamw(name, w, g, m, v):
    def body(w_ref, g_ref, m_ref, v_ref, d_ref, nm_ref, nv_ref):
        gv = g_ref[...]
        nm = ADAM_B1 * m_ref[...] + (1.0 - ADAM_B1) * gv
        nv = ADAM_B2 * v_ref[...] + (1.0 - ADAM_B2) * (gv * gv)
        m_hat = nm / (1.0 - ADAM_B1 ** ADAM_STEP)
        v_hat = nv / (1.0 - ADAM_B2 ** ADAM_STEP)
        d_ref[...] = -ADAM_LR * (m_hat / (jnp.sqrt(v_hat) + ADAM_EPS) + ADAM_WD * w_ref[...])
        nm_ref[...] = nm
        nv_ref[...] = nv

    rows = w.shape[0]
    tr = 256 if rows % 256 == 0 else rows
    spec = pl.BlockSpec((tr, w.shape[1]), lambda i: (i, 0))
    return pl.pallas_call(
        body, name=name, grid=(rows // tr,),
        in_specs=[spec] * 4, out_specs=[spec] * 3,
        out_shape=[jax.ShapeDtypeStruct(w.shape, F32)] * 3,
        compiler_params=_params(("parallel",)),
    )(w, g, m, v)


def kernel(x, g_mix, w_in, conv_w, g_q, g_k, g_conv_out, g_attn_out, w_out, g_ffn, w_gate, w_up, w_down, loss_target, m_g_mix, m_w_in, m_conv_w, m_g_q, m_g_k, m_g_conv_out, m_g_attn_out, m_w_out, m_g_ffn, m_w_gate, m_w_up, m_w_down, v_g_mix, v_w_in, v_conv_w, v_g_q, v_g_k, v_g_conv_out, v_g_attn_out, v_w_out, v_g_ffn, v_w_gate, v_w_up, v_w_down):
    s = x.shape[1]
    tm = min(512, s)
    xs, target = x[0], loss_target[0]
    px, py, pc = lax.axis_index("x"), lax.axis_index("y"), lax.axis_index("c")
    device = 4 * px + 2 * py + pc

    conv_block = jnp.zeros((8, 128), F32).at[:3, :HEAD_DIM].set(conv_w[0])
    parts = [w_in[0].T, w_out[0], w_gate[0].T, w_up[0].T, w_down[0], conv_block]
    gathered = _all_gather_weights(parts, [BF16] * 5 + [F32])
    w_in_t = gathered[0].reshape(D_IN, D_MODEL)
    w_out_f = gathered[1].reshape(D_MODEL, D_MODEL)
    w_gate_t = gathered[2].reshape(D_FF, D_MODEL)
    w_up_t = gathered[3].reshape(D_FF, D_MODEL)
    w_down_f = gathered[4].reshape(D_FF, D_MODEL)
    conv_full = jnp.transpose(gathered[5][:, :3, :HEAD_DIM], (1, 0, 2)).reshape(3, D_CONV)
    conv_full = jnp.concatenate([conv_full, jnp.zeros((5, D_CONV), F32)], axis=0)
    gq2 = jnp.concatenate([g_q, g_q], axis=-1)
    gk2 = jnp.concatenate([g_k, g_k], axis=-1)

    h1, z = _in_projection(xs, g_mix, w_in_t, tm)
    masks = _permuted_masks()
    y_attn, lse = _attention_forward(z, gq2, gk2, masks)
    mix, x2, h2 = _mix_out(z, y_attn, xs, conv_full, g_conv_out, g_attn_out, g_ffn, w_out_f, tm)
    tf = min(256, s)
    a, b, dy, sq_err = _ffn_forward(h2, x2, target, w_gate_t, w_up_t, w_down_f, tf)

    da, db, f, dx2, dg_ffn = _ffn_backward(dy, a, b, x2, g_ffn, w_gate_t, w_up_t, w_down_f, tf)
    dya, dgb, dcv, dg_conv_out, dg_attn_out = _out_backward(dx2, z, y_attn, conv_full, g_conv_out, g_attn_out, w_out_f, tm)
    dzq, dzk, dzv, dgq_pairs, dgk_pairs = _attention_backward(z, y_attn, lse, dya, gq2, gk2, masks)
    dz, grad_x, dg_mix, dconv = _in_backward(dcv, dgb, dzq, dzk, dzv, z, xs, dx2, g_mix, conv_full, w_in_t, tm)

    tk = min(512, s)
    grads = [
        _weight_grad("grad_w_in", dz, h1, D_IN // 2, tk),
        _weight_grad("grad_w_out", mix, dx2, D_MODEL, tk),
        _weight_grad("grad_w_gate", da, h2, D_FF // 2, tk),
        _weight_grad("grad_w_up", db, h2, D_FF // 2, tk),
        _weight_grad("grad_w_down", f, dy, D_FF // 2, tk),
    ]

    grads = [g.reshape(4, 2, g.shape[0] // N_DEV, D_MODEL) for g in grads]
    from_sibling = _reduce_scatter_cores(grads)
    core = jnp.reshape(pc, (1,)).astype(jnp.int32)
    partials = _add_core_partials(core, grads, from_sibling)
    from_chips = _reduce_scatter_chips(partials)
    chip = jnp.reshape(2 * px + py, (1,)).astype(jnp.int32)
    gw_in_t, gw_out, gw_gate_t, gw_up_t, gw_down = _add_chip_partials(chip, partials, from_chips)
    gw_in, gw_gate, gw_up = gw_in_t.T, gw_gate_t.T, gw_up_t.T

    dg_q = jnp.sum(dgq_pairs.reshape(2 * N_PAIRS, HEAD_DIM), axis=0)
    dg_k = jnp.sum(dgk_pairs.reshape(2 * N_PAIRS, HEAD_DIM), axis=0)
    zeros = lambda n: jnp.zeros((n,), F32)
    small = jnp.stack([
        dg_mix[0], dg_ffn[0],
        jnp.concatenate([dg_conv_out[0], dg_attn_out[0]]),
        jnp.concatenate([dg_q, dg_k, zeros(D_MODEL - 2 * HEAD_DIM)]),
        jnp.concatenate([dconv[0], dconv[1]]),
        jnp.concatenate([dconv[2], zeros(D_CONV)]),
        jnp.concatenate([sq_err[0, :1], zeros(D_MODEL - 1)]),
        zeros(D_MODEL),
    ])
    total = _all_reduce_small(small)
    loss = total[6, 0] * (0.5 / D_MODEL)
    gg_mix, gg_ffn = total[0:1], total[1:2]
    gg_conv_out, gg_attn_out = total[2:3, :D_CONV], total[2:3, D_CONV:]
    gg_q, gg_k = total[3:4, :HEAD_DIM], total[3:4, HEAD_DIM:2 * HEAD_DIM]
    conv_total = jnp.stack([total[4, :D_CONV], total[4, D_CONV:], total[5, :D_CONV]])
    g_conv = lax.dynamic_slice(conv_total, (0, device * HEAD_DIM), (3, HEAD_DIM))

    names = ["g_mix", "w_in", "conv_w", "g_q", "g_k", "g_conv_out", "g_attn_out", "w_out", "g_ffn", "w_gate", "w_up", "w_down"]
    weights = [g_mix, w_in[0], conv_w[0], g_q, g_k, g_conv_out, g_attn_out, w_out[0], g_ffn, w_gate[0], w_up[0], w_down[0]]
    grad_list = [gg_mix, gw_in, g_conv, gg_q, gg_k, gg_conv_out, gg_attn_out, gw_out, gg_ffn, gw_gate, gw_up, gw_down]
    m_list = [m_g_mix, m_w_in[0], m_conv_w[0], m_g_q, m_g_k, m_g_conv_out, m_g_attn_out, m_w_out[0], m_g_ffn, m_w_gate[0], m_w_up[0], m_w_down[0]]
    v_list = [v_g_mix, v_w_in[0], v_conv_w[0], v_g_q, v_g_k, v_g_conv_out, v_g_attn_out, v_w_out[0], v_g_ffn, v_w_gate[0], v_w_up[0], v_w_down[0]]
    stacked = {"w_in", "conv_w", "w_out", "w_gate", "w_up", "w_down"}
    out_g, out_d, out_m, out_v = [], [], [], []
    for name, w, g, m, v in zip(names, weights, grad_list, m_list, v_list):
        delta, new_m, new_v = _adamw("adamw_" + name, w, g, m, v)
        lift = (lambda t: t[None]) if name in stacked else (lambda t: t)
        out_g.append(lift(g))
        out_d.append(lift(delta))
        out_m.append(lift(new_m))
        out_v.append(lift(new_v))
    return (loss, grad_x[None], *out_g, *out_d, *out_m, *out_v)
```

```python
import jax
import jax.numpy as jnp
import numpy as np
from jax import lax
from jax.experimental import pallas as pl
from jax.experimental.pallas import tpu as pltpu

F32 = jnp.float32
BF16 = jnp.bfloat16

D_MODEL = 1024
D_CONV = 512
D_ATTN = 512
D_IN = 3072
D_FF = 2816
HEAD_DIM = 64
N_PAIRS = 4
BLOCK = 128
DILATIONS = (1, 4, 16)
N_DEV = 8
EPS = 1e-6
NEG = -1e30

ADAM_LR = 0.001
ADAM_B1 = 0.9
ADAM_B2 = 0.999
ADAM_EPS = 1e-08
ADAM_WD = 0.01
ADAM_STEP = 10

NT = (((1,), (1,)), ((), ()))
NN = (((1,), (0,)), ((), ()))
TN = (((0,), (0,)), ((), ()))
MESH = pl.DeviceIdType.MESH

VMEM_LIMIT = 56 * 1024 * 1024


def _params(semantics=None, vmem=None):
    kw = {}
    if semantics is not None:
        kw["dimension_semantics"] = semantics
    if vmem is not None:
        kw["vmem_limit_bytes"] = vmem
    return pltpu.CompilerParams(**kw)


def _dot(a, b, dims):
    return lax.dot_general(a, b, dims, preferred_element_type=F32)


def _const_spec(shape):
    n = len(shape)
    return pl.BlockSpec(shape, lambda *_: (0,) * n, pipeline_mode=pl.Buffered(1))


def _rms(x):
    r = lax.rsqrt(jnp.mean(x * x, axis=-1, keepdims=True) + EPS)
    return x * r, r


def _rms_bwd(dy, xh, r, g):
    t = dy * g
    return r * (t - xh * jnp.mean(t * xh, axis=-1, keepdims=True))


def _head_blockdiag(scale):
    i = lax.broadcasted_iota(jnp.int32, (128, 128), 0) // HEAD_DIM
    j = lax.broadcasted_iota(jnp.int32, (128, 128), 1) // HEAD_DIM
    return jnp.where(i == j, scale, 0.0).astype(BF16)


def _head_sum(x, bd):
    hi = x.astype(BF16)
    lo = (x - hi.astype(F32)).astype(BF16)
    return _dot(hi, bd, NN) + _dot(lo, bd, NN)


def _place():
    return lax.axis_index("x"), lax.axis_index("y"), lax.axis_index("c")


def _all_gather_weights(parts, out_dtypes):
    n = len(parts)

    def body(*refs):
        ins, outs = refs[:n], refs[n:2 * n]
        send_sems, recv_sems = refs[2 * n], refs[2 * n + 1]
        x, y, c = _place()
        me, sibling = (x, y, c), (x, y, 1 - c)
        chips = [(x, 1 - y), (1 - x, y), (1 - x, 1 - y)]

        def index(p):
            return 4 * p[0] + 2 * p[1] + p[2]

        for a in range(n):
            outs[a][index(me)] = ins[a][...].astype(outs[a].dtype)

        def copy(a, k, block, to):
            blk = outs[a].at[index(block)]
            return pltpu.make_async_remote_copy(
                src_ref=blk, dst_ref=blk, send_sem=send_sems.at[7 * a + k], recv_sem=recv_sems.at[7 * a + k],
                device_id=to, device_id_type=MESH)

        first = []
        for a in range(n):
            first.append(copy(a, 0, me, sibling))
            first += [copy(a, 1 + j, me, (*chip, c)) for j, chip in enumerate(chips)]
        for cp in first:
            cp.start()
        passed = []
        for a in range(n):
            for j, chip in enumerate(chips):
                copy(a, 1 + j, (*chip, c), me).wait_recv()
                fwd = copy(a, 4 + j, (*chip, c), sibling)
                fwd.start()
                passed.append(fwd)
        for a in range(n):
            copy(a, 0, sibling, me).wait_recv()
            for j, chip in enumerate(chips):
                copy(a, 4 + j, (*chip, 1 - c), me).wait_recv()
        for cp in first + passed:
            cp.wait_send()

    vm = pl.BlockSpec(memory_space=pltpu.VMEM)
    return pl.pallas_call(
        body, name="weight_all_gather",
        out_shape=[jax.ShapeDtypeStruct((N_DEV,) + p.shape, dt) for p, dt in zip(parts, out_dtypes)],
        in_specs=[vm] * n, out_specs=[vm] * n,
        scratch_shapes=[pltpu.SemaphoreType.DMA((7 * n,)), pltpu.SemaphoreType.DMA((7 * n,))],
        compiler_params=_params(vmem=VMEM_LIMIT),
    )(*parts)


def _all_reduce_small(v):
    def body(v_ref, out_ref, gathered, send_sems, recv_sems):
        x, y, c = _place()
        mine = 4 * x + 2 * y + c
        gathered[mine] = v_ref[...]

        def copy(k):
            peer = (1 - x if k & 4 else x, 1 - y if k & 2 else y, 1 - c if k & 1 else c)
            return pltpu.make_async_remote_copy(
                src_ref=gathered.at[mine], dst_ref=gathered.at[mine],
                send_sem=send_sems.at[k - 1], recv_sem=recv_sems.at[k - 1], device_id=peer, device_id_type=MESH)

        copies = [copy(k) for k in range(1, N_DEV)]
        for cp in copies:
            cp.start()
        for cp in copies:
            cp.wait_recv()
        for cp in copies:
            cp.wait_send()
        total = gathered[0]
        for d in range(1, N_DEV):
            total = total + gathered[d]
        out_ref[...] = total

    vm = pl.BlockSpec(memory_space=pltpu.VMEM)
    return pl.pallas_call(
        body, name="small_all_reduce",
        out_shape=jax.ShapeDtypeStruct(v.shape, F32),
        in_specs=[vm], out_specs=vm,
        scratch_shapes=[pltpu.VMEM((N_DEV,) + v.shape, F32),
                        pltpu.SemaphoreType.DMA((N_DEV - 1,)), pltpu.SemaphoreType.DMA((N_DEV - 1,))],
    )(v)


def _reduce_scatter_cores(grads):
    n = len(grads)

    def body(*refs):
        ins, outs = refs[:n], refs[n:2 * n]
        send_sems, recv_sems = refs[2 * n], refs[2 * n + 1]
        x, y, c = _place()
        copies = []
        for a in range(n):
            for k in range(4):
                copies.append(pltpu.make_async_remote_copy(
                    src_ref=ins[a].at[k, 1 - c], dst_ref=outs[a].at[k],
                    send_sem=send_sems.at[4 * a + k], recv_sem=recv_sems.at[4 * a + k],
                    device_id=(x, y, 1 - c), device_id_type=MESH))
        for cp in copies:
            cp.start()
        for cp in copies:
            cp.wait_recv()
        for cp in copies:
            cp.wait_send()

    hbm = pl.BlockSpec(memory_space=pl.ANY)
    return pl.pallas_call(
        body, name="grad_reduce_scatter_cores",
        out_shape=[jax.ShapeDtypeStruct((4,) + g.shape[2:], g.dtype) for g in grads],
        in_specs=[hbm] * n, out_specs=[hbm] * n,
        scratch_shapes=[pltpu.SemaphoreType.DMA((4 * n,)), pltpu.SemaphoreType.DMA((4 * n,))],
    )(*grads)


def _reduce_scatter_chips(partials):
    n = len(partials)

    def body(*refs):
        ins, outs = refs[:n], refs[n:2 * n]
        send_sems, recv_sems = refs[2 * n], refs[2 * n + 1]
        x, y, c = _place()
        copies = []
        for a in range(n):
            for j in (1, 2, 3):
                px = 1 - x if j & 2 else x
                py = 1 - y if j & 1 else y
                copies.append(pltpu.make_async_remote_copy(
                    src_ref=ins[a].at[2 * px + py], dst_ref=outs[a].at[j - 1],
                    send_sem=send_sems.at[3 * a + j - 1], recv_sem=recv_sems.at[3 * a + j - 1],
                    device_id=(px, py, c), device_id_type=MESH))
        for cp in copies:
            cp.start()
        for cp in copies:
            cp.wait_recv()
        for cp in copies:
            cp.wait_send()

    hbm = pl.BlockSpec(memory_space=pl.ANY)
    return pl.pallas_call(
        body, name="grad_reduce_scatter_chips",
        out_shape=[jax.ShapeDtypeStruct((3,) + p.shape[1:], p.dtype) for p in partials],
        in_specs=[hbm] * n, out_specs=[hbm] * n,
        scratch_shapes=[pltpu.SemaphoreType.DMA((3 * n,)), pltpu.SemaphoreType.DMA((3 * n,))],
    )(*partials)


def _add_core_partials(core, grads, received):
    n = len(grads)

    def body(core_ref, *refs):
        del core_ref
        gs, rs, outs = refs[:n], refs[n:2 * n], refs[2 * n:]
        for a in range(n):
            outs[a][...] = (gs[a][...].astype(F32) + rs[a][...].astype(F32)).astype(outs[a].dtype)

    in_specs = [pl.BlockSpec((None, None) + g.shape[2:], lambda k, cref: (k, cref[0], 0, 0)) for g in grads]
    in_specs += [pl.BlockSpec((None,) + r.shape[1:], lambda k, cref: (k, 0, 0)) for r in received]
    out_specs = [pl.BlockSpec((None,) + r.shape[1:], lambda k, cref: (k, 0, 0)) for r in received]
    return pl.pallas_call(
        body, name="grad_add_core_partials",
        grid_spec=pltpu.PrefetchScalarGridSpec(num_scalar_prefetch=1, grid=(4,), in_specs=in_specs, out_specs=out_specs),
        out_shape=[jax.ShapeDtypeStruct(r.shape, r.dtype) for r in received],
        compiler_params=_params(("arbitrary",)),
    )(core, *grads, *received)


def _add_chip_partials(chip, partials, received):
    n = len(partials)

    def body(chip_ref, *refs):
        del chip_ref
        ps, rs, outs = refs[:n], refs[n:2 * n], refs[2 * n:]
        for a in range(n):
            outs[a][...] = ((ps[a][...].astype(F32) + rs[a][0].astype(F32)) + rs[a][1].astype(F32)) + rs[a][2].astype(F32)

    in_specs = [pl.BlockSpec((None,) + p.shape[1:], lambda i, cref: (cref[0], 0, 0)) for p in partials]
    in_specs += [pl.BlockSpec(r.shape, lambda i, cref: (0, 0, 0)) for r in received]
    out_specs = [pl.BlockSpec(p.shape[1:], lambda i, cref: (0, 0)) for p in partials]
    return pl.pallas_call(
        body, name="grad_add_chip_partials",
        grid_spec=pltpu.PrefetchScalarGridSpec(num_scalar_prefetch=1, grid=(1,), in_specs=in_specs, out_specs=out_specs),
        out_shape=[jax.ShapeDtypeStruct(p.shape[1:], F32) for p in partials],
        compiler_params=_params(("arbitrary",), vmem=VMEM_LIMIT),
    )(chip, *partials, *received)


def _in_projection(x, g_mix, w_in_t, tm):
    s = x.shape[0]

    def body(x_ref, g_ref, w_ref, h_ref, z_ref):
        xh, _ = _rms(x_ref[...])
        h = (xh * g_ref[...]).astype(BF16)
        h_ref[...] = h
        for n0 in range(0, D_IN, 512):
            z_ref[:, n0:n0 + 512] = _dot(h, w_ref[n0:n0 + 512, :], NT).astype(BF16)

    return pl.pallas_call(
        body, name="in_projection", grid=(s // tm,),
        in_specs=[pl.BlockSpec((tm, D_MODEL), lambda i: (i, 0)), _const_spec((1, D_MODEL)), _const_spec((D_IN, D_MODEL))],
        out_specs=[pl.BlockSpec((tm, D_MODEL), lambda i: (i, 0)), pl.BlockSpec((tm, D_IN), lambda i: (i, 0))],
        out_shape=[jax.ShapeDtypeStruct((s, D_MODEL), BF16), jax.ShapeDtypeStruct((s, D_IN), BF16)],
        compiler_params=_params(("parallel",), vmem=VMEM_LIMIT),
    )(x, g_mix, w_in_t)


RESIDUES = 16
CHUNK = 512
GROUP = 4


def _branch_geometry(d, seg):
    n_runs = RESIDUES // d
    w = BLOCK // n_runs
    blocks = seg // w
    group = min(GROUP, blocks)
    return n_runs, w, blocks // group, group


def _permuted_masks():
    out = []
    for d in DILATIONS:
        n_runs = RESIDUES // d
        w = BLOCK // n_runs
        p = np.arange(BLOCK)
        pos = (p % w) * n_runs + p // w
        dist = pos[:, None] - np.concatenate([pos - BLOCK, pos])[None, :]
        band = (dist >= 0) & (dist <= BLOCK)
        first = band & (np.arange(2 * BLOCK)[None, :] >= BLOCK)
        both = [np.where(m, 0.0, NEG).astype(np.float32) for m in (band, first)]
        out.append(np.stack([np.concatenate([m, m], axis=0) for m in both]))
    return jnp.asarray(np.stack(out))


def _run_starts(u, grp, d, seg, n_runs, w, group):
    return [(u + d * q) * seg + (w * group) * grp for q in range(n_runs)]


def _tile_rows(start, w, b):
    off = start + w * b
    if b < 0:
        off = jnp.maximum(off, 0)
    return pl.ds(pl.multiple_of(off, w), w)


def _load_tile(ref, starts, w, b):
    parts = [ref[_tile_rows(st, w, b), :] for st in starts]
    return parts[0] if len(parts) == 1 else jnp.concatenate(parts, axis=0)


def _store_tile(ref, starts, w, b, tile):
    for i, st in enumerate(starts):
        ref[_tile_rows(st, w, b), :] = tile[i * w:(i + 1) * w]


def _add_tile(ref, starts, w, b, tile):
    for i, st in enumerate(starts):
        ref[_tile_rows(st, w, b), :] += tile[i * w:(i + 1) * w]


def _to_residue_major(dst, src, c, seg):
    per = CHUNK // RESIDUES
    for r in range(RESIDUES):
        dst[pl.ds(pl.multiple_of(r * seg + c * per, per), per), :] = src[pl.ds(r, per, stride=RESIDUES), :]


def _from_residue_major(dst, dst_start, src, c, seg):
    per = CHUNK // RESIDUES
    for r in range(RESIDUES):
        dst[pl.ds(dst_start + r, per, stride=RESIDUES), :] = src[pl.ds(pl.multiple_of(r * seg + c * per, per), per), :]


def _stack_heads(t, head0):
    return jnp.concatenate([jnp.where(head0, t, 0.0), jnp.where(head0, 0.0, t)], axis=0).astype(BF16)


def _attention_forward(z, gq2, gk2, masks):
    s = z.shape[0]
    seg = s // RESIDUES

    def body(zq_ref, zk_ref, zv_ref, gq_ref, gk_ref, mask_ref, o_ref, lse_ref, qf, kf, vf, o_st, l_st, tq, tk, tv):
        bd = _head_blockdiag(1.0 / HEAD_DIM)
        gq = gq_ref[...] * (HEAD_DIM ** -0.5)
        gk = gk_ref[...]

        def prep(c, carry):
            rows = pl.ds(pl.multiple_of(c * CHUNK, CHUNK), CHUNK)
            zq = zq_ref[rows, :].astype(F32)
            zk = zk_ref[rows, :].astype(F32)
            tq[...] = (zq * lax.rsqrt(_head_sum(zq * zq, bd) + EPS)) * gq
            tk[...] = (zk * lax.rsqrt(_head_sum(zk * zk, bd) + EPS)) * gk
            tv[...] = zv_ref[rows, :].astype(F32)
            _to_residue_major(qf, tq, c, seg)
            _to_residue_major(kf, tk, c, seg)
            _to_residue_major(vf, tv, c, seg)
            return carry

        lax.fori_loop(0, s // CHUNK, prep, 0)

        head0 = lax.broadcasted_iota(jnp.int32, (BLOCK, 128), 1) < HEAD_DIM

        def run_branch(branch, d):
            n_runs, w, steps, group = _branch_geometry(d, seg)

            def step(u, grp):
                starts = _run_starts(u, grp, d, seg, n_runs, w, group)
                kt = [_load_tile(kf, starts, w, b).astype(BF16) for b in range(-1, group)]
                vt = [_load_tile(vf, starts, w, b).astype(BF16) for b in range(-1, group)]
                scores = []
                for b in range(group):
                    q2 = _stack_heads(_load_tile(qf, starts, w, b), head0)
                    kc = jnp.concatenate([kt[b], kt[b + 1]], axis=0)
                    mask = mask_ref[branch, 0]
                    if b == 0:
                        mask = jnp.where(grp == 0, mask_ref[branch, 1], mask)
                    scores.append(_dot(q2, kc, NT) + mask)
                sc = jnp.concatenate(scores, axis=0)
                m = jnp.max(sc, axis=-1, keepdims=True)
                e = jnp.exp(sc - m).astype(BF16)
                ones = jnp.ones((BLOCK, 128), BF16)
                fresh = []
                for b in range(group):
                    rows = slice(2 * BLOCK * b, 2 * BLOCK * (b + 1))
                    v_aug = jnp.concatenate([jnp.concatenate([vt[b], ones], axis=1),
                                             jnp.concatenate([vt[b + 1], ones], axis=1)], axis=0)
                    o2 = _dot(e[rows], v_aug, NN)
                    den = o2[:, 128:]
                    o2 = o2[:, :128] * (1.0 / den)
                    l2 = m[rows] + jnp.log(den)
                    fresh.append((jnp.where(head0, o2[:BLOCK], o2[BLOCK:]), jnp.where(head0, l2[:BLOCK], l2[BLOCK:])))
                for b in range(group):
                    o_new, l_new = fresh[b]
                    if branch > 0:
                        l_old = _load_tile(l_st, starts, w, b)
                        top = jnp.maximum(l_old, l_new)
                        e_old = jnp.exp(l_old - top)
                        e_new = jnp.exp(l_new - top)
                        tot = e_old + e_new
                        inv = 1.0 / tot
                        o_new = _load_tile(o_st, starts, w, b) * (e_old * inv) + o_new * (e_new * inv)
                        l_new = top + jnp.log(tot)
                    _store_tile(o_st, starts, w, b, o_new)
                    _store_tile(l_st, starts, w, b, l_new)

            def unit(u, carry):
                def inner(grp, carry2):
                    step(u, grp)
                    return carry2
                return lax.fori_loop(0, steps, inner, carry)

            lax.fori_loop(0, d, unit, 0)

        for branch, d in enumerate(DILATIONS):
            run_branch(branch, d)

        def finish(c, carry):
            _from_residue_major(o_ref, c * CHUNK, o_st, c, seg)
            _from_residue_major(lse_ref, c * CHUNK, l_st, c, seg)
            return carry

        lax.fori_loop(0, s // CHUNK, finish, 0)

    def col(c0):
        return pl.BlockSpec((s, 128), lambda p: (0, c0 + p), pipeline_mode=pl.Buffered(1))

    return pl.pallas_call(
        body, name="attention_forward", grid=(N_PAIRS,),
        in_specs=[col(12), col(16), col(20), _const_spec((1, 128)), _const_spec((1, 128)), _const_spec(masks.shape)],
        out_specs=[pl.BlockSpec((s, 128), lambda p: (0, p)), pl.BlockSpec((s, 128), lambda p: (0, p))],
        out_shape=[jax.ShapeDtypeStruct((s, D_ATTN), F32), jax.ShapeDtypeStruct((s, D_ATTN), F32)],
        scratch_shapes=[pltpu.VMEM((s, 128), F32)] * 5 + [pltpu.VMEM((CHUNK, 128), F32)] * 3,
        compiler_params=_params(("arbitrary",), vmem=VMEM_LIMIT),
    )(z, z, z, gq2, gk2, masks)


def _conv_forward(zt, zprev, w, first):
    u, gb, gc = zt[:, :D_CONV], zt[:, D_CONV:2 * D_CONV], zt[:, 2 * D_CONV:]
    cu = gc * u
    cu_prev = jnp.where(first, 0.0, zprev[:, 2 * D_CONV:] * zprev[:, :D_CONV])
    row = lax.broadcasted_iota(jnp.int32, cu.shape, 0)
    cu1 = jnp.where(row >= 1, pltpu.roll(cu, 1, 0), cu_prev[7:8, :])
    cu2 = jnp.where(row >= 2, pltpu.roll(cu, 2, 0), jnp.where(row == 1, cu_prev[7:8, :], cu_prev[6:7, :]))
    cv = w[0:1, :] * cu2 + w[1:2, :] * cu1 + w[2:3, :] * cu
    return u, gb, gc, cu, cu1, cu2, cv


def _mix_out(z, y_attn, x, conv_w, g_conv_out, g_attn_out, g_ffn, w_out, tm):
    s = x.shape[0]
    hb = tm // 8

    def body(z_ref, zp_ref, ya_ref, x_ref, cw_ref, gc_ref, ga_ref, gf_ref, w_ref, mix_ref, x2_ref, h2_ref):
        first = pl.program_id(0) == 0
        zt = z_ref[...].astype(F32)
        _, gb, _, _, _, _, cv = _conv_forward(zt, zp_ref[...].astype(F32), cw_ref[...], first)
        nc, _ = _rms(gb * cv)
        na, _ = _rms(ya_ref[...])
        mix = jnp.concatenate([nc * gc_ref[...], na * ga_ref[...]], axis=-1).astype(BF16)
        mix_ref[...] = mix
        x2 = x_ref[...] + _dot(mix, w_ref[...], NN)
        x2_ref[...] = x2
        xh, _ = _rms(x2)
        h2_ref[...] = (xh * gf_ref[...]).astype(BF16)

    tile = lambda w: pl.BlockSpec((tm, w), lambda i: (i, 0))
    return pl.pallas_call(
        body, name="mix_out", grid=(s // tm,),
        in_specs=[tile(3 * D_CONV), pl.BlockSpec((8, 3 * D_CONV), lambda i: (jnp.maximum(i * hb - 1, 0), 0)),
                  tile(D_ATTN), tile(D_MODEL), _const_spec((8, D_CONV)), _const_spec((1, D_CONV)), _const_spec((1, D_ATTN)),
                  _const_spec((1, D_MODEL)), _const_spec((D_MODEL, D_MODEL))],
        out_specs=[tile(D_MODEL)] * 3,
        out_shape=[jax.ShapeDtypeStruct((s, D_MODEL), BF16), jax.ShapeDtypeStruct((s, D_MODEL), F32),
                   jax.ShapeDtypeStruct((s, D_MODEL), BF16)],
        compiler_params=_params(("arbitrary",), vmem=VMEM_LIMIT),
    )(z, z, y_attn, x, conv_w, g_conv_out, g_attn_out, g_ffn, w_out)


FF_CHUNK = 256


def _ffn_forward(h2, x2, target, w_gate_t, w_up_t, w_down, tm):
    s = h2.shape[0]

    def body(h_ref, x2_ref, t_ref, wg_ref, wu_ref, wd_ref, a_ref, b_ref, dy_ref, loss_ref):
        @pl.when(pl.program_id(0) == 0)
        def _():
            loss_ref[...] = jnp.zeros_like(loss_ref)

        h = h_ref[...]
        y = x2_ref[...]
        for c0 in range(0, D_FF, FF_CHUNK):
            rows = slice(c0, c0 + FF_CHUNK)
            a = _dot(h, wg_ref[rows, :], NT)
            b = _dot(h, wu_ref[rows, :], NT)
            a_ref[:, rows] = a.astype(BF16)
            b_ref[:, rows] = b.astype(BF16)
            f = (a * jax.nn.sigmoid(a)) * b
            y = y + _dot(f.astype(BF16), wd_ref[rows, :], NN)
        err = y - t_ref[...]
        dy_ref[...] = err * (1.0 / D_MODEL)
        loss_ref[...] += jnp.sum(err * err)

    tile = lambda w: pl.BlockSpec((tm, w), lambda i: (i, 0))
    return pl.pallas_call(
        body, name="ffn_forward", grid=(s // tm,),
        in_specs=[tile(D_MODEL), tile(D_MODEL), tile(D_MODEL)] + [_const_spec((D_FF, D_MODEL))] * 3,
        out_specs=[tile(D_FF), tile(D_FF), tile(D_MODEL), pl.BlockSpec((8, 128), lambda i: (0, 0))],
        out_shape=[jax.ShapeDtypeStruct((s, D_FF), BF16), jax.ShapeDtypeStruct((s, D_FF), BF16),
                   jax.ShapeDtypeStruct((s, D_MODEL), F32), jax.ShapeDtypeStruct((8, 128), F32)],
        compiler_params=_params(("arbitrary",), vmem=VMEM_LIMIT),
    )(h2, x2, target, w_gate_t, w_up_t, w_down)


def _ffn_backward(dy, a, b, x2, g_ffn, w_gate_t, w_up_t, w_down, tm):
    s = dy.shape[0]

    def body(dy_ref, a_ref, b_ref, x2_ref, g_ref, wg_ref, wu_ref, wd_ref, da_ref, db_ref, f_ref, dx2_ref, dg_ref):
        @pl.when(pl.program_id(0) == 0)
        def _():
            dg_ref[...] = jnp.zeros_like(dg_ref)

        dy_f = dy_ref[...]
        dyb = dy_f.astype(BF16)
        dh = jnp.zeros((tm, D_MODEL), F32)
        for c0 in range(0, D_FF, FF_CHUNK):
            rows = slice(c0, c0 + FF_CHUNK)
            df = _dot(dyb, wd_ref[rows, :], NT)
            av = a_ref[:, rows].astype(F32)
            bv = b_ref[:, rows].astype(F32)
            sig = jax.nn.sigmoid(av)
            silu = av * sig
            f_ref[:, rows] = (silu * bv).astype(BF16)
            da = (df * bv * (sig * (1.0 + av * (1.0 - sig)))).astype(BF16)
            db = (df * silu).astype(BF16)
            da_ref[:, rows] = da
            db_ref[:, rows] = db
            dh = dh + _dot(da, wg_ref[rows, :], NN) + _dot(db, wu_ref[rows, :], NN)
        xh, r = _rms(x2_ref[...])
        dg_ref[...] += jnp.sum(dh * xh, axis=0, keepdims=True)
        dx2_ref[...] = dy_f + _rms_bwd(dh, xh, r, g_ref[...])

    tile = lambda w: pl.BlockSpec((tm, w), lambda i: (i, 0))
    return pl.pallas_call(
        body, name="ffn_backward", grid=(s // tm,),
        in_specs=[tile(D_MODEL), tile(D_FF), tile(D_FF), tile(D_MODEL), _const_spec((1, D_MODEL))]
        + [_const_spec((D_FF, D_MODEL))] * 3,
        out_specs=[tile(D_FF), tile(D_FF), tile(D_FF), tile(D_MODEL), pl.BlockSpec((1, D_MODEL), lambda i: (0, 0))],
        out_shape=[jax.ShapeDtypeStruct((s, D_FF), BF16)] * 3
        + [jax.ShapeDtypeStruct((s, D_MODEL), F32), jax.ShapeDtypeStruct((1, D_MODEL), F32)],
        compiler_params=_params(("arbitrary",), vmem=VMEM_LIMIT),
    )(dy, a, b, x2, g_ffn, w_gate_t, w_up_t, w_down)


def _out_backward(dx2, z, y_attn, conv_w, g_conv_out, g_attn_out, w_out, tm):
    s = dx2.shape[0]
    hb = tm // 8

    def body(dx2_ref, z_ref, zp_ref, ya_ref, cw_ref, gc_ref, ga_ref, w_ref, dya_ref, dgb_ref, dcv_ref, dgc_ref, dga_ref):
        first = pl.program_id(0) == 0

        @pl.when(first)
        def _():
            dgc_ref[...] = jnp.zeros_like(dgc_ref)
            dga_ref[...] = jnp.zeros_like(dga_ref)

        dmix = _dot(dx2_ref[...].astype(BF16), w_ref[...], NT)
        _, gb, _, _, _, _, cv = _conv_forward(z_ref[...].astype(F32), zp_ref[...].astype(F32), cw_ref[...], first)
        ych, rc = _rms(gb * cv)
        dnc = dmix[:, :D_CONV]
        dgc_ref[...] += jnp.sum(dnc * ych, axis=0, keepdims=True)
        dyc = _rms_bwd(dnc, ych, rc, gc_ref[...])
        dgb_ref[...] = (dyc * cv).astype(BF16)
        dcv_ref[...] = dyc * gb
        yah, ra = _rms(ya_ref[...])
        dna = dmix[:, D_CONV:]
        dga_ref[...] += jnp.sum(dna * yah, axis=0, keepdims=True)
        dya_ref[...] = _rms_bwd(dna, yah, ra, ga_ref[...])

    tile = lambda w: pl.BlockSpec((tm, w), lambda i: (i, 0))
    vec = pl.BlockSpec((1, D_CONV), lambda i: (0, 0))
    return pl.pallas_call(
        body, name="out_backward", grid=(s // tm,),
        in_specs=[tile(D_MODEL), tile(3 * D_CONV), pl.BlockSpec((8, 3 * D_CONV), lambda i: (jnp.maximum(i * hb - 1, 0), 0)),
                  tile(D_ATTN), _const_spec((8, D_CONV)), _const_spec((1, D_CONV)), _const_spec((1, D_ATTN)),
                  _const_spec((D_MODEL, D_MODEL))],
        out_specs=[tile(D_ATTN), tile(D_CONV), tile(D_CONV), vec, vec],
        out_shape=[jax.ShapeDtypeStruct((s, D_ATTN), F32), jax.ShapeDtypeStruct((s, D_CONV), BF16),
                   jax.ShapeDtypeStruct((s, D_CONV), F32), jax.ShapeDtypeStruct((1, D_CONV), F32),
                   jax.ShapeDtypeStruct((1, D_ATTN), F32)],
        compiler_params=_params(("arbitrary",), vmem=VMEM_LIMIT),
    )(dx2, z, z, y_attn, conv_w, g_conv_out, g_attn_out, w_out)


def _attention_backward(z, o, lse, dya, gq2, gk2, masks):
    s = z.shape[0]
    seg = s // RESIDUES

    def body(z_hbm, o_hbm, lse_hbm, do_hbm, gq_ref, gk_ref, mask_ref, dzq_ref, dzk_ref, dzv_ref, dgq_ref, dgk_ref,
             qf, kf, vf, dof, stats, dq, dk, dv, t0, t1, t2, zbuf, fbuf, sems):
        pair = pl.program_id(0)
        n_chunks = s // CHUNK
        bd_mean = _head_blockdiag(1.0 / HEAD_DIM)
        bd_sum = _head_blockdiag(1.0)
        gq = gq_ref[...] * (HEAD_DIM ** -0.5)
        gk = gk_ref[...]
        lane = lax.broadcasted_iota(jnp.int32, (CHUNK, 128), 1) % HEAD_DIM

        def chunk_copies(c, slot, n_z, n_f):
            rows = pl.ds(pl.multiple_of(c * CHUNK, CHUNK), CHUNK)
            cps = []
            for i, col in enumerate((12, 16, 20)[:n_z]):
                cols = pl.ds(pl.multiple_of((col + pair) * 128, 128), 128)
                cps.append(pltpu.make_async_copy(z_hbm.at[rows, cols], zbuf.at[slot, i], sems.at[slot, i]))
            for i, src in enumerate((do_hbm, o_hbm, lse_hbm)[:n_f]):
                cols = pl.ds(pl.multiple_of(pair * 128, 128), 128)
                cps.append(pltpu.make_async_copy(src.at[rows, cols], fbuf.at[slot, i], sems.at[slot, 3 + i]))
            return cps

        def prefetched(c, n_z, n_f):
            slot = c % 2

            @pl.when(c + 1 < n_chunks)
            def _():
                for cp in chunk_copies(c + 1, 1 - slot, n_z, n_f):
                    cp.start()

            for cp in chunk_copies(c, slot, n_z, n_f):
                cp.wait()
            return slot

        for cp in chunk_copies(0, 0, 3, 3):
            cp.start()

        def prep(c, carry):
            slot = prefetched(c, 3, 3)
            rows = pl.ds(pl.multiple_of(c * CHUNK, CHUNK), CHUNK)
            zq = zbuf[slot, 0].astype(F32)
            t0[...] = (zq * lax.rsqrt(_head_sum(zq * zq, bd_mean) + EPS)) * gq
            zk = zbuf[slot, 1].astype(F32)
            t1[...] = (zk * lax.rsqrt(_head_sum(zk * zk, bd_mean) + EPS)) * gk
            t2[...] = zbuf[slot, 2].astype(F32)
            _to_residue_major(qf, t0, c, seg)
            _to_residue_major(kf, t1, c, seg)
            _to_residue_major(vf, t2, c, seg)
            _to_residue_major(dof, fbuf.at[slot, 0], c, seg)
            delta = _head_sum(fbuf[slot, 0] * fbuf[slot, 1], bd_sum)
            t0[...] = jnp.where(lane == 0, fbuf[slot, 2], jnp.where(lane == 1, delta, 0.0))
            _to_residue_major(stats, t0, c, seg)
            zero = jnp.zeros((CHUNK, 128), F32)
            dq[rows, :] = zero
            dk[rows, :] = zero
            dv[rows, :] = zero
            return carry

        lax.fori_loop(0, n_chunks, prep, 0)

        head0 = lax.broadcasted_iota(jnp.int32, (BLOCK, 128), 1) < HEAD_DIM

        def run_branch(branch, d):
            n_runs, w, steps, group = _branch_geometry(d, seg)

            def step(u, grp):
                starts = _run_starts(u, grp, d, seg, n_runs, w, group)
                kt = [_load_tile(kf, starts, w, b).astype(BF16) for b in range(-1, group)]
                vt = [_load_tile(vf, starts, w, b).astype(BF16) for b in range(-1, group)]
                dk_t = [jnp.zeros((BLOCK, 128), F32)] * (group + 1)
                dv_t = [jnp.zeros((BLOCK, 128), F32)] * (group + 1)
                dq_t = []
                q2s, do2s, kcs, scores, dps, lses, deltas = [], [], [], [], [], [], []
                for b in range(group):
                    q2 = _stack_heads(_load_tile(qf, starts, w, b), head0)
                    do2 = _stack_heads(_load_tile(dof, starts, w, b), head0)
                    st = _load_tile(stats, starts, w, b)
                    lses += [st[:, 0:1], st[:, HEAD_DIM:HEAD_DIM + 1]]
                    deltas += [st[:, 1:2], st[:, HEAD_DIM + 1:HEAD_DIM + 2]]
                    kc = jnp.concatenate([kt[b], kt[b + 1]], axis=0)
                    vc = jnp.concatenate([vt[b], vt[b + 1]], axis=0)
                    mask = mask_ref[branch, 0]
                    if b == 0:
                        mask = jnp.where(grp == 0, mask_ref[branch, 1], mask)
                    scores.append(_dot(q2, kc, NT) + mask)
                    dps.append(_dot(do2, vc, NT))
                    q2s.append(q2)
                    do2s.append(do2)
                    kcs.append(kc)
                p_all = jnp.exp(jnp.concatenate(scores, axis=0) - jnp.concatenate(lses, axis=0))
                ds_all = (p_all * (jnp.concatenate(dps, axis=0) - jnp.concatenate(deltas, axis=0))).astype(BF16)
                p_all = p_all.astype(BF16)
                for b in range(group):
                    rows = slice(2 * BLOCK * b, 2 * BLOCK * (b + 1))
                    ds, q2, do2, kc = ds_all[rows], q2s[b], do2s[b], kcs[b]
                    dq2 = _dot(ds, kc, NN)
                    dq_t.append(jnp.where(head0, dq2[:BLOCK], dq2[BLOCK:]))
                    dkc = _dot(ds, q2, TN)
                    dvc = _dot(p_all[rows], do2, TN)
                    dk_t[b] = dk_t[b] + dkc[:BLOCK]
                    dk_t[b + 1] = dk_t[b + 1] + dkc[BLOCK:]
                    dv_t[b] = dv_t[b] + dvc[:BLOCK]
                    dv_t[b + 1] = dv_t[b + 1] + dvc[BLOCK:]
                for b in range(group):
                    _add_tile(dq, starts, w, b, dq_t[b])
                for b in range(-1, group):
                    _add_tile(dk, starts, w, b, dk_t[b + 1])
                    _add_tile(dv, starts, w, b, dv_t[b + 1])

            def unit(u, carry):
                def inner(grp, carry2):
                    step(u, grp)
                    return carry2
                return lax.fori_loop(0, steps, inner, carry)

            lax.fori_loop(0, d, unit, 0)

        for branch, d in enumerate(DILATIONS):
            run_branch(branch, d)

        for cp in chunk_copies(0, 0, 2, 0):
            cp.start()

        def finish(c, carry):
            acc_q, acc_k = carry
            rows = pl.ds(pl.multiple_of(c * CHUNK, CHUNK), CHUNK)
            _from_residue_major(t0, 0, dq, c, seg)
            _from_residue_major(t1, 0, dk, c, seg)
            _from_residue_major(t2, 0, dv, c, seg)
            slot = prefetched(c, 2, 0)
            zq = zbuf[slot, 0].astype(F32)
            rq = lax.rsqrt(_head_sum(zq * zq, bd_mean) + EPS)
            qh = zq * rq
            dqn = t0[...]
            acc_q = acc_q + jnp.sum(dqn * qh, axis=0, keepdims=True)
            t = dqn * gq
            dzq_ref[rows, :] = (rq * (t - qh * _head_sum(t * qh, bd_mean))).astype(BF16)
            zk = zbuf[slot, 1].astype(F32)
            rk = lax.rsqrt(_head_sum(zk * zk, bd_mean) + EPS)
            kh = zk * rk
            dkn = t1[...]
            acc_k = acc_k + jnp.sum(dkn * kh, axis=0, keepdims=True)
            t = dkn * gk
            dzk_ref[rows, :] = (rk * (t - kh * _head_sum(t * kh, bd_mean))).astype(BF16)
            dzv_ref[rows, :] = t2[...].astype(BF16)
            return acc_q, acc_k

        zero = jnp.zeros((1, 128), F32)
        acc_q, acc_k = lax.fori_loop(0, s // CHUNK, finish, (zero, zero))
        dgq_ref[...] = acc_q * (HEAD_DIM ** -0.5)
        dgk_ref[...] = acc_k

    hbm = pl.BlockSpec(memory_space=pl.ANY)
    gain = pl.BlockSpec((None, 1, 128), lambda p: (p, 0, 0))
    dz_spec = pl.BlockSpec((s, 128), lambda p: (0, p))
    return pl.pallas_call(
        body, name="attention_backward", grid=(N_PAIRS,),
        in_specs=[hbm, hbm, hbm, hbm, _const_spec((1, 128)), _const_spec((1, 128)), _const_spec(masks.shape)],
        out_specs=[dz_spec, dz_spec, dz_spec, gain, gain],
        out_shape=[jax.ShapeDtypeStruct((s, D_ATTN), BF16)] * 3 + [jax.ShapeDtypeStruct((N_PAIRS, 1, 128), F32)] * 2,
        scratch_shapes=[pltpu.VMEM((s, 128), F32)] * 8 + [pltpu.VMEM((CHUNK, 128), F32)] * 3
        + [pltpu.VMEM((2, 3, CHUNK, 128), BF16), pltpu.VMEM((2, 3, CHUNK, 128), F32), pltpu.SemaphoreType.DMA((2, 6))],
        compiler_params=_params(("arbitrary",), vmem=VMEM_LIMIT),
    )(z, o, lse, dya, gq2, gk2, masks)


def _in_backward(dcv, dgb, dzq, dzk, dzv, z, x, dx2, g_mix, conv_w, w_in_t, tm):
    s = x.shape[0]
    hb = tm // 8
    last_halo = s // 8 - 1

    def body(dcv_ref, dcn_ref, dgb_ref, dzq_ref, dzk_ref, dzv_ref, z_ref, zp_ref, x_ref, dx2_ref, g_ref, cw_ref, w_ref,
             dz_ref, dx_ref, dg_ref, dcw_ref):
        i = pl.program_id(0)
        first = i == 0
        last = i == pl.num_programs(0) - 1

        @pl.when(first)
        def _():
            dg_ref[...] = jnp.zeros_like(dg_ref)
            dcw_ref[...] = jnp.zeros_like(dcw_ref)

        w = cw_ref[...]
        u, _, gc, cu, cu1, cu2, _ = _conv_forward(z_ref[...].astype(F32), zp_ref[...].astype(F32), w, first)
        dcv_t = dcv_ref[...]
        nxt = jnp.where(last, 0.0, dcn_ref[...])
        row = lax.broadcasted_iota(jnp.int32, dcv_t.shape, 0)
        up1 = jnp.where(row < tm - 1, pltpu.roll(dcv_t, tm - 1, 0), nxt[0:1, :])
        up2 = jnp.where(row < tm - 2, pltpu.roll(dcv_t, tm - 2, 0), jnp.where(row == tm - 2, nxt[0:1, :], nxt[1:2, :]))
        dcu = w[2:3, :] * dcv_t + w[1:2, :] * up1 + w[0:1, :] * up2
        dcw = jnp.concatenate([jnp.sum(dcv_t * cu2, axis=0, keepdims=True), jnp.sum(dcv_t * cu1, axis=0, keepdims=True),
                               jnp.sum(dcv_t * cu, axis=0, keepdims=True), jnp.zeros((5, D_CONV), F32)], axis=0)
        dcw_ref[...] += dcw
        dz_ref[:, :D_CONV] = (dcu * gc).astype(BF16)
        dz_ref[:, D_CONV:2 * D_CONV] = dgb_ref[...]
        dz_ref[:, 2 * D_CONV:3 * D_CONV] = (dcu * u).astype(BF16)
        dz_ref[:, 3 * D_CONV:3 * D_CONV + D_ATTN] = dzq_ref[...]
        dz_ref[:, 3 * D_CONV + D_ATTN:3 * D_CONV + 2 * D_ATTN] = dzk_ref[...]
        dz_ref[:, 3 * D_CONV + 2 * D_ATTN:] = dzv_ref[...]
        dh = _dot(dz_ref[...], w_ref[...], NN)
        xh, r = _rms(x_ref[...])
        dg_ref[...] += jnp.sum(dh * xh, axis=0, keepdims=True)
        dx_ref[...] = dx2_ref[...] + _rms_bwd(dh, xh, r, g_ref[...])

    tile = lambda w: pl.BlockSpec((tm, w), lambda i: (i, 0))
    return pl.pallas_call(
        body, name="in_backward", grid=(s // tm,),
        in_specs=[tile(D_CONV), pl.BlockSpec((8, D_CONV), lambda i: (jnp.minimum((i + 1) * hb, last_halo), 0)),
                  tile(D_CONV), tile(D_ATTN), tile(D_ATTN), tile(D_ATTN),
                  tile(3 * D_CONV), pl.BlockSpec((8, 3 * D_CONV), lambda i: (jnp.maximum(i * hb - 1, 0), 0)),
                  tile(D_MODEL), tile(D_MODEL), _const_spec((1, D_MODEL)), _const_spec((8, D_CONV)),
                  _const_spec((D_IN, D_MODEL))],
        out_specs=[tile(D_IN), tile(D_MODEL), pl.BlockSpec((1, D_MODEL), lambda i: (0, 0)),
                   pl.BlockSpec((8, D_CONV), lambda i: (0, 0))],
        out_shape=[jax.ShapeDtypeStruct((s, D_IN), BF16), jax.ShapeDtypeStruct((s, D_MODEL), F32),
                   jax.ShapeDtypeStruct((1, D_MODEL), F32), jax.ShapeDtypeStruct((8, D_CONV), F32)],
        compiler_params=_params(("arbitrary",), vmem=VMEM_LIMIT),
    )(dcv, dcv, dgb, dzq, dzk, dzv, z, z, x, dx2, g_mix, conv_w, w_in_t)


def _weight_grad(name, a, b, tn, tk):
    s, n = a.shape
    steps = s // tk

    def body(a_ref, b_ref, out_ref, acc):
        k = pl.program_id(1)

        @pl.when(k == 0)
        def _():
            acc[...] = jnp.zeros_like(acc)

        acc[...] += _dot(a_ref[...], b_ref[...].astype(BF16), TN)

        @pl.when(k == steps - 1)
        def _():
            out_ref[...] = acc[...].astype(BF16)

    return pl.pallas_call(
        body, name=name, grid=(n // tn, steps),
        in_specs=[pl.BlockSpec((tk, tn), lambda j, k: (k, j)), pl.BlockSpec((tk, D_MODEL), lambda j, k: (k, 0))],
        out_specs=pl.BlockSpec((tn, D_MODEL), lambda j, k: (j, 0)),
        out_shape=jax.ShapeDtypeStruct((n, D_MODEL), BF16),
        scratch_shapes=[pltpu.VMEM((tn, D_MODEL), F32)],
        compiler_params=_params(("parallel", "arbitrary"), vmem=VMEM_LIMIT),
    )(a, b)


def _adamw(name, w, g, m, v):
    def body(w_ref, g_ref, m_ref, v_ref, d_ref, nm_ref, nv_ref):
        gv = g_ref[...]
        nm = ADAM_B1 * m_ref[...] + (1.0 - ADAM_B1) * gv
        nv = ADAM_B2 * v_ref[...] + (1.0 - ADAM_B2) * (gv * gv)
        m_hat = nm / (1.0 - ADAM_B1 ** ADAM_STEP)
        v_hat = nv / (1.0 - ADAM_B2 ** ADAM_STEP)
        d_ref[...] = -ADAM_LR * (m_hat / (jnp.sqrt(v_hat) + ADAM_EPS) + ADAM_WD * w_ref[...])
        nm_ref[...] = nm
        nv_ref[...] = nv

    rows = w.shape[0]
    tr = 256 if rows % 256 == 0 else rows
    spec = pl.BlockSpec((tr, w.shape[1]), lambda i: (i, 0))
    return pl.pallas_call(
        body, name=name, grid=(rows // tr,),
        in_specs=[spec] * 4, out_specs=[spec] * 3,
        out_shape=[jax.ShapeDtypeStruct(w.shape, F32)] * 3,
        compiler_params=_params(("parallel",)),
    )(w, g, m, v)


def kernel(x, g_mix, w_in, conv_w, g_q, g_k, g_conv_out, g_attn_out, w_out, g_ffn, w_gate, w_up, w_down, loss_target, m_g_mix, m_w_in, m_conv_w, m_g_q, m_g_k, m_g_conv_out, m_g_attn_out, m_w_out, m_g_ffn, m_w_gate, m_w_up, m_w_down, v_g_mix, v_w_in, v_conv_w, v_g_q, v_g_k, v_g_conv_out, v_g_attn_out, v_w_out, v_g_ffn, v_w_gate, v_w_up, v_w_down):
    s = x.shape[1]
    tm = min(512, s)
    xs, target = x[0], loss_target[0]
    px, py, pc = lax.axis_index("x"), lax.axis_index("y"), lax.axis_index("c")
    device = 4 * px + 2 * py + pc

    conv_block = jnp.zeros((8, 128), F32).at[:3, :HEAD_DIM].set(conv_w[0])
    parts = [w_in[0].T, w_out[0], w_gate[0].T, w_up[0].T, w_down[0], conv_block]
    gathered = _all_gather_weights(parts, [BF16] * 5 + [F32])
    w_in_t = gathered[0].reshape(D_IN, D_MODEL)
    w_out_f = gathered[1].reshape(D_MODEL, D_MODEL)
    w_gate_t = gathered[2].reshape(D_FF, D_MODEL)
    w_up_t = gathered[3].reshape(D_FF, D_MODEL)
    w_down_f = gathered[4].reshape(D_FF, D_MODEL)
    conv_full = jnp.transpose(gathered[5][:, :3, :HEAD_DIM], (1, 0, 2)).reshape(3, D_CONV)
    conv_full = jnp.concatenate([conv_full, jnp.zeros((5, D_CONV), F32)], axis=0)
    gq2 = jnp.concatenate([g_q, g_q], axis=-1)
    gk2 = jnp.concatenate([g_k, g_k], axis=-1)

    h1, z = _in_projection(xs, g_mix, w_in_t, tm)
    masks = _permuted_masks()
    y_attn, lse = _attention_forward(z, gq2, gk2, masks)
    mix, x2, h2 = _mix_out(z, y_attn, xs, conv_full, g_conv_out, g_attn_out, g_ffn, w_out_f, tm)
    tf = min(256, s)
    a, b, dy, sq_err = _ffn_forward(h2, x2, target, w_gate_t, w_up_t, w_down_f, tf)

    da, db, f, dx2, dg_ffn = _ffn_backward(dy, a, b, x2, g_ffn, w_gate_t, w_up_t, w_down_f, tf)
    dya, dgb, dcv, dg_conv_out, dg_attn_out = _out_backward(dx2, z, y_attn, conv_full, g_conv_out, g_attn_out, w_out_f, tm)
    dzq, dzk, dzv, dgq_pairs, dgk_pairs = _attention_backward(z, y_attn, lse, dya, gq2, gk2, masks)
    dz, grad_x, dg_mix, dconv = _in_backward(dcv, dgb, dzq, dzk, dzv, z, xs, dx2, g_mix, conv_full, w_in_t, tm)

    tk = min(512, s)
    grads = [
        _weight_grad("grad_w_in", dz, h1, D_IN // 2, tk),
        _weight_grad("grad_w_out", mix, dx2, D_MODEL, tk),
        _weight_grad("grad_w_gate", da, h2, D_FF // 2, tk),
        _weight_grad("grad_w_up", db, h2, D_FF // 2, tk),
        _weight_grad("grad_w_down", f, dy, D_FF // 2, tk),
    ]

    grads = [g.reshape(4, 2, g.shape[0] // N_DEV, D_MODEL) for g in grads]
    from_sibling = _reduce_scatter_cores(grads)
    core = jnp.reshape(pc, (1,)).astype(jnp.int32)
    partials = _add_core_partials(core, grads, from_sibling)
    from_chips = _reduce_scatter_chips(partials)
    chip = jnp.reshape(2 * px + py, (1,)).astype(jnp.int32)
    gw_in_t, gw_out, gw_gate_t, gw_up_t, gw_down = _add_chip_partials(chip, partials, from_chips)
    gw_in, gw_gate, gw_up = gw_in_t.T, gw_gate_t.T, gw_up_t.T

    dg_q = jnp.sum(dgq_pairs.reshape(2 * N_PAIRS, HEAD_DIM), axis=0)
    dg_k = jnp.sum(dgk_pairs.reshape(2 * N_PAIRS, HEAD_DIM), axis=0)
    zeros = lambda n: jnp.zeros((n,), F32)
    small = jnp.stack([
        dg_mix[0], dg_ffn[0],
        jnp.concatenate([dg_conv_out[0], dg_attn_out[0]]),
        jnp.concatenate([dg_q, dg_k, zeros(D_MODEL - 2 * HEAD_DIM)]),
        jnp.concatenate([dconv[0], dconv[1]]),
        jnp.concatenate([dconv[2], zeros(D_CONV)]),
        jnp.concatenate([sq_err[0, :1], zeros(D_MODEL - 1)]),
        zeros(D_MODEL),
    ])
    total = _all_reduce_small(small)
    loss = total[6, 0] * (0.5 / D_MODEL)
    gg_mix, gg_ffn = total[0:1], total[1:2]
    gg_conv_out, gg_attn_out = total[2:3, :D_CONV], total[2:3, D_CONV:]
    gg_q, gg_k = total[3:4, :HEAD_DIM], total[3:4, HEAD_DIM:2 * HEAD_DIM]
    conv_total = jnp.stack([total[4, :D_CONV], total[4, D_CONV:], total[5, :D_CONV]])
    g_conv = lax.dynamic_slice(conv_total, (0, device * HEAD_DIM), (3, HEAD_DIM))

    names = ["g_mix", "w_in", "conv_w", "g_q", "g_k", "g_conv_out", "g_attn_out", "w_out", "g_ffn", "w_gate", "w_up", "w_down"]
    weights = [g_mix, w_in[0], conv_w[0], g_q, g_k, g_conv_out, g_attn_out, w_out[0], g_ffn, w_gate[0], w_up[0], w_down[0]]
    grad_list = [gg_mix, gw_in, g_conv, gg_q, gg_k, gg_conv_out, gg_attn_out, gw_out, gg_ffn, gw_gate, gw_up, gw_down]
    m_list = [m_g_mix, m_w_in[0], m_conv_w[0], m_g_q, m_g_k, m_g_conv_out, m_g_attn_out, m_w_out[0], m_g_ffn, m_w_gate[0], m_w_up[0], m_w_down[0]]
    v_list = [v_g_mix, v_w_in[0], v_conv_w[0], v_g_q, v_g_k, v_g_conv_out, v_g_attn_out, v_w_out[0], v_g_ffn, v_w_gate[0], v_w_up[0], v_w_down[0]]
    stacked = {"w_in", "conv_w", "w_out", "w_gate", "w_up", "w_down"}
    out_g, out_d, out_m, out_v = [], [], [], []
    for name, w, g, m, v in zip(names, weights, grad_list, m_list, v_list):
        delta, new_m, new_v = _adamw("adamw_" + name, w, g, m, v)
        lift = (lambda t: t[None]) if name in stacked else (lambda t: t)
        out_g.append(lift(g))
        out_d.append(lift(delta))
        out_m.append(lift(new_m))
        out_v.append(lift(new_v))
    return (loss, grad_x[None], *out_g, *out_d, *out_m, *out_v)
```

```python
import jax
import jax.numpy as jnp
import numpy as np
from jax import lax
from jax.experimental import pallas as pl
from jax.experimental.pallas import tpu as pltpu

F32 = jnp.float32
BF16 = jnp.bfloat16

D_MODEL = 1024
D_CONV = 512
D_ATTN = 512
D_IN = 3072
D_FF = 2816
HEAD_DIM = 64
N_PAIRS = 4
BLOCK = 128
DILATIONS = (1, 4, 16)
N_DEV = 8
EPS = 1e-6
NEG = -1e30

ADAM_LR = 0.001
ADAM_B1 = 0.9
ADAM_B2 = 0.999
ADAM_EPS = 1e-08
ADAM_WD = 0.01
ADAM_STEP = 10

NT = (((1,), (1,)), ((), ()))
NN = (((1,), (0,)), ((), ()))
TN = (((0,), (0,)), ((), ()))
MESH = pl.DeviceIdType.MESH

VMEM_LIMIT = 56 * 1024 * 1024


def _params(semantics=None, vmem=None):
    kw = {}
    if semantics is not None:
        kw["dimension_semantics"] = semantics
    if vmem is not None:
        kw["vmem_limit_bytes"] = vmem
    return pltpu.CompilerParams(**kw)


def _dot(a, b, dims):
    return lax.dot_general(a, b, dims, preferred_element_type=F32)


def _const_spec(shape):
    n = len(shape)
    return pl.BlockSpec(shape, lambda *_: (0,) * n, pipeline_mode=pl.Buffered(1))


def _rms(x):
    r = lax.rsqrt(jnp.mean(x * x, axis=-1, keepdims=True) + EPS)
    return x * r, r


def _rms_bwd(dy, xh, r, g):
    t = dy * g
    return r * (t - xh * jnp.mean(t * xh, axis=-1, keepdims=True))


def _head_blockdiag(scale):
    i = lax.broadcasted_iota(jnp.int32, (128, 128), 0) // HEAD_DIM
    j = lax.broadcasted_iota(jnp.int32, (128, 128), 1) // HEAD_DIM
    return jnp.where(i == j, scale, 0.0).astype(BF16)


def _head_sum(x, bd):
    hi = x.astype(BF16)
    lo = (x - hi.astype(F32)).astype(BF16)
    return _dot(hi, bd, NN) + _dot(lo, bd, NN)


def _place():
    return lax.axis_index("x"), lax.axis_index("y"), lax.axis_index("c")


class _TwoLevelGather:
    def __init__(self, stacks, send_sems, recv_sems):
        self.stacks, self.send_sems, self.recv_sems = stacks, send_sems, recv_sems
        x, y, c = _place()
        self.c = c
        self.me, self.sibling = (x, y, c), (x, y, 1 - c)
        self.chips = [(x, 1 - y), (1 - x, y), (1 - x, 1 - y)]

    @staticmethod
    def index(p):
        return 4 * p[0] + 2 * p[1] + p[2]

    def copy(self, a, k, block, to, src=None):
        dst = self.stacks[a].at[self.index(block)]
        return pltpu.make_async_remote_copy(
            src_ref=dst if src is None else src, dst_ref=dst,
            send_sem=self.send_sems.at[7 * a + k], recv_sem=self.recv_sems.at[7 * a + k],
            device_id=to, device_id_type=MESH)

    def first(self, a, src=None):
        cps = [self.copy(a, 0, self.me, self.sibling, src)]
        return cps + [self.copy(a, 1 + j, self.me, (*chip, self.c), src) for j, chip in enumerate(self.chips)]

    def forwards(self, a):
        return [self.copy(a, 4 + j, (*chip, self.c), self.sibling) for j, chip in enumerate(self.chips)]

    def start(self, srcs=None):
        for a in range(len(self.stacks)):
            for cp in self.first(a, None if srcs is None else srcs[a]):
                cp.start()

    def forward(self):
        for a in range(len(self.stacks)):
            for j, (chip, fwd) in enumerate(zip(self.chips, self.forwards(a))):
                self.copy(a, 1 + j, (*chip, self.c), self.me).wait_recv()
                fwd.start()

    def finish(self):
        for a in range(len(self.stacks)):
            self.copy(a, 0, self.sibling, self.me).wait_recv()
            for j, chip in enumerate(self.chips):
                self.copy(a, 4 + j, (*chip, 1 - self.c), self.me).wait_recv()
        for a in range(len(self.stacks)):
            for cp in self.first(a) + self.forwards(a):
                cp.wait_send()


def _gather_first_weights(gathered_parts, gathered_dtypes, cast_parts):
    n, m = len(gathered_parts), len(cast_parts)

    def body(*refs):
        ins, casts_in = refs[:n], refs[n:n + m]
        outs, casts_out = refs[n + m:2 * n + m], refs[2 * n + m:2 * n + 2 * m]
        send_sems, recv_sems = refs[2 * n + 2 * m], refs[2 * n + 2 * m + 1]
        gather = _TwoLevelGather(outs, send_sems, recv_sems)
        for a in range(n):
            outs[a][gather.index(gather.me)] = ins[a][...].astype(outs[a].dtype)
        gather.start()
        for a in range(m):
            casts_out[a][...] = casts_in[a][...].astype(BF16)
        gather.forward()
        gather.finish()

    vm = pl.BlockSpec(memory_space=pltpu.VMEM)
    return pl.pallas_call(
        body, name="weight_all_gather",
        out_shape=[jax.ShapeDtypeStruct((N_DEV,) + p.shape, dt) for p, dt in zip(gathered_parts, gathered_dtypes)]
        + [jax.ShapeDtypeStruct(p.shape, BF16) for p in cast_parts],
        in_specs=[vm] * (n + m), out_specs=[vm] * (n + m),
        scratch_shapes=[pltpu.SemaphoreType.DMA((7 * n,)), pltpu.SemaphoreType.DMA((7 * n,))],
        compiler_params=_params(vmem=VMEM_LIMIT),
    )(*gathered_parts, *cast_parts)


class _DirectReduceScatter:
    def __init__(self, stacks, landing, send_sems, recv_sems):
        self.stacks, self.landing, self.send_sems, self.recv_sems = stacks, landing, send_sems, recv_sems
        self.place = _place()

    def copies(self):
        x, y, c = self.place
        cps = []
        for a in range(len(self.stacks)):
            for k in range(1, N_DEV):
                peer = (1 - x if k & 4 else x, 1 - y if k & 2 else y, 1 - c if k & 1 else c)
                cps.append(pltpu.make_async_remote_copy(
                    src_ref=self.stacks[a].at[4 * peer[0] + 2 * peer[1] + peer[2]], dst_ref=self.landing[a].at[k - 1],
                    send_sem=self.send_sems.at[7 * a + k - 1], recv_sem=self.recv_sems.at[7 * a + k - 1],
                    device_id=peer, device_id_type=MESH))
        return cps

    def start(self):
        for cp in self.copies():
            cp.start()

    def finish(self):
        for cp in self.copies():
            cp.wait_recv()
        for cp in self.copies():
            cp.wait_send()


def _add_received(name, device, stack, received):
    def body(dev_ref, own_ref, recv_ref, out_ref):
        del dev_ref
        total = own_ref[...].astype(F32)
        for k in range(N_DEV - 1):
            total = total + recv_ref[k].astype(F32)
        out_ref[...] = total

    rows = stack.shape[1]
    return pl.pallas_call(
        body, name=name,
        grid_spec=pltpu.PrefetchScalarGridSpec(
            num_scalar_prefetch=1, grid=(1,),
            in_specs=[pl.BlockSpec((None, rows, D_MODEL), lambda i, dref: (dref[0], 0, 0)),
                      pl.BlockSpec(received.shape, lambda i, dref: (0, 0, 0))],
            out_specs=pl.BlockSpec((rows, D_MODEL), lambda i, dref: (0, 0))),
        out_shape=jax.ShapeDtypeStruct((rows, D_MODEL), F32),
        compiler_params=_params(("arbitrary",), vmem=VMEM_LIMIT),
    )(device, stack, received)


def _all_reduce_small(v):
    def body(v_ref, out_ref, gathered, send_sems, recv_sems):
        x, y, c = _place()
        mine = 4 * x + 2 * y + c
        gathered[mine] = v_ref[...]

        def copy(k):
            peer = (1 - x if k & 4 else x, 1 - y if k & 2 else y, 1 - c if k & 1 else c)
            return pltpu.make_async_remote_copy(
                src_ref=gathered.at[mine], dst_ref=gathered.at[mine],
                send_sem=send_sems.at[k - 1], recv_sem=recv_sems.at[k - 1], device_id=peer, device_id_type=MESH)

        copies = [copy(k) for k in range(1, N_DEV)]
        for cp in copies:
            cp.start()
        for cp in copies:
            cp.wait_recv()
        for cp in copies:
            cp.wait_send()
        total = gathered[0]
        for d in range(1, N_DEV):
            total = total + gathered[d]
        out_ref[...] = total

    vm = pl.BlockSpec(memory_space=pltpu.VMEM)
    return pl.pallas_call(
        body, name="small_all_reduce",
        out_shape=jax.ShapeDtypeStruct(v.shape, F32),
        in_specs=[vm], out_specs=vm,
        scratch_shapes=[pltpu.VMEM((N_DEV,) + v.shape, F32),
                        pltpu.SemaphoreType.DMA((N_DEV - 1,)), pltpu.SemaphoreType.DMA((N_DEV - 1,))],
    )(v)


def _reduce_scatter_cores(grads):
    n = len(grads)

    def body(*refs):
        ins, outs = refs[:n], refs[n:2 * n]
        send_sems, recv_sems = refs[2 * n], refs[2 * n + 1]
        x, y, c = _place()
        copies = []
        for a in range(n):
            for k in range(4):
                copies.append(pltpu.make_async_remote_copy(
                    src_ref=ins[a].at[k, 1 - c], dst_ref=outs[a].at[k],
                    send_sem=send_sems.at[4 * a + k], recv_sem=recv_sems.at[4 * a + k],
                    device_id=(x, y, 1 - c), device_id_type=MESH))
        for cp in copies:
            cp.start()
        for cp in copies:
            cp.wait_recv()
        for cp in copies:
            cp.wait_send()

    hbm = pl.BlockSpec(memory_space=pl.ANY)
    return pl.pallas_call(
        body, name="grad_reduce_scatter_cores",
        out_shape=[jax.ShapeDtypeStruct((4,) + g.shape[2:], g.dtype) for g in grads],
        in_specs=[hbm] * n, out_specs=[hbm] * n,
        scratch_shapes=[pltpu.SemaphoreType.DMA((4 * n,)), pltpu.SemaphoreType.DMA((4 * n,))],
    )(*grads)


def _reduce_scatter_chips(partials):
    n = len(partials)

    def body(*refs):
        ins, outs = refs[:n], refs[n:2 * n]
        send_sems, recv_sems = refs[2 * n], refs[2 * n + 1]
        x, y, c = _place()
        copies = []
        for a in range(n):
            for j in (1, 2, 3):
                px = 1 - x if j & 2 else x
                py = 1 - y if j & 1 else y
                copies.append(pltpu.make_async_remote_copy(
                    src_ref=ins[a].at[2 * px + py], dst_ref=outs[a].at[j - 1],
                    send_sem=send_sems.at[3 * a + j - 1], recv_sem=recv_sems.at[3 * a + j - 1],
                    device_id=(px, py, c), device_id_type=MESH))
        for cp in copies:
            cp.start()
        for cp in copies:
            cp.wait_recv()
        for cp in copies:
            cp.wait_send()

    hbm = pl.BlockSpec(memory_space=pl.ANY)
    return pl.pallas_call(
        body, name="grad_reduce_scatter_chips",
        out_shape=[jax.ShapeDtypeStruct((3,) + p.shape[1:], p.dtype) for p in partials],
        in_specs=[hbm] * n, out_specs=[hbm] * n,
        scratch_shapes=[pltpu.SemaphoreType.DMA((3 * n,)), pltpu.SemaphoreType.DMA((3 * n,))],
    )(*partials)


def _add_core_partials(core, grads, received):
    n = len(grads)

    def body(core_ref, *refs):
        del core_ref
        gs, rs, outs = refs[:n], refs[n:2 * n], refs[2 * n:]
        for a in range(n):
            outs[a][...] = (gs[a][...].astype(F32) + rs[a][...].astype(F32)).astype(outs[a].dtype)

    in_specs = [pl.BlockSpec((None, None) + g.shape[2:], lambda k, cref: (k, cref[0], 0, 0)) for g in grads]
    in_specs += [pl.BlockSpec((None,) + r.shape[1:], lambda k, cref: (k, 0, 0)) for r in received]
    out_specs = [pl.BlockSpec((None,) + r.shape[1:], lambda k, cref: (k, 0, 0)) for r in received]
    return pl.pallas_call(
        body, name="grad_add_core_partials",
        grid_spec=pltpu.PrefetchScalarGridSpec(num_scalar_prefetch=1, grid=(4,), in_specs=in_specs, out_specs=out_specs),
        out_shape=[jax.ShapeDtypeStruct(r.shape, r.dtype) for r in received],
        compiler_params=_params(("arbitrary",)),
    )(core, *grads, *received)


def _add_chip_partials(chip, partials, received):
    n = len(partials)

    def body(chip_ref, *refs):
        del chip_ref
        ps, rs, outs = refs[:n], refs[n:2 * n], refs[2 * n:]
        for a in range(n):
            outs[a][...] = ((ps[a][...].astype(F32) + rs[a][0].astype(F32)) + rs[a][1].astype(F32)) + rs[a][2].astype(F32)

    in_specs = [pl.BlockSpec((None,) + p.shape[1:], lambda i, cref: (cref[0], 0, 0)) for p in partials]
    in_specs += [pl.BlockSpec(r.shape, lambda i, cref: (0, 0, 0)) for r in received]
    out_specs = [pl.BlockSpec(p.shape[1:], lambda i, cref: (0, 0)) for p in partials]
    return pl.pallas_call(
        body, name="grad_add_chip_partials",
        grid_spec=pltpu.PrefetchScalarGridSpec(num_scalar_prefetch=1, grid=(1,), in_specs=in_specs, out_specs=out_specs),
        out_shape=[jax.ShapeDtypeStruct(p.shape[1:], F32) for p in partials],
        compiler_params=_params(("arbitrary",), vmem=VMEM_LIMIT),
    )(chip, *partials, *received)


def _in_projection(x, g_mix, w_in_t, tm):
    s = x.shape[0]

    def body(x_ref, g_ref, w_ref, h_ref, z_ref):
        xh, _ = _rms(x_ref[...])
        h = (xh * g_ref[...]).astype(BF16)
        h_ref[...] = h
        for n0 in range(0, D_IN, 512):
            z_ref[:, n0:n0 + 512] = _dot(h, w_ref[n0:n0 + 512, :], NT).astype(BF16)

    return pl.pallas_call(
        body, name="in_projection", grid=(s // tm,),
        in_specs=[pl.BlockSpec((tm, D_MODEL), lambda i: (i, 0)), _const_spec((1, D_MODEL)), _const_spec((D_IN, D_MODEL))],
        out_specs=[pl.BlockSpec((tm, D_MODEL), lambda i: (i, 0)), pl.BlockSpec((tm, D_IN), lambda i: (i, 0))],
        out_shape=[jax.ShapeDtypeStruct((s, D_MODEL), BF16), jax.ShapeDtypeStruct((s, D_IN), BF16)],
        compiler_params=_params(("parallel",), vmem=VMEM_LIMIT),
    )(x, g_mix, w_in_t)


RESIDUES = 16
CHUNK = 512
GROUP = 4


def _branch_geometry(d, seg):
    n_runs = RESIDUES // d
    w = BLOCK // n_runs
    blocks = seg // w
    group = min(GROUP, blocks)
    return n_runs, w, blocks // group, group


def _permuted_masks():
    out = []
    for d in DILATIONS:
        n_runs = RESIDUES // d
        w = BLOCK // n_runs
        p = np.arange(BLOCK)
        pos = (p % w) * n_runs + p // w
        dist = pos[:, None] - np.concatenate([pos - BLOCK, pos])[None, :]
        band = (dist >= 0) & (dist <= BLOCK)
        first = band & (np.arange(2 * BLOCK)[None, :] >= BLOCK)
        both = [np.where(m, 0.0, NEG).astype(np.float32) for m in (band, first)]
        out.append(np.stack([np.concatenate([m, m], axis=0) for m in both]))
    return jnp.asarray(np.stack(out))


def _run_starts(u, grp, d, seg, n_runs, w, group):
    return [(u + d * q) * seg + (w * group) * grp for q in range(n_runs)]


def _tile_rows(start, w, b):
    off = start + w * b
    if b < 0:
        off = jnp.maximum(off, 0)
    return pl.ds(pl.multiple_of(off, w), w)


def _load_tile(ref, starts, w, b):
    parts = [ref[_tile_rows(st, w, b), :] for st in starts]
    return parts[0] if len(parts) == 1 else jnp.concatenate(parts, axis=0)


def _store_tile(ref, starts, w, b, tile):
    for i, st in enumerate(starts):
        ref[_tile_rows(st, w, b), :] = tile[i * w:(i + 1) * w]


def _add_tile(ref, starts, w, b, tile):
    for i, st in enumerate(starts):
        ref[_tile_rows(st, w, b), :] += tile[i * w:(i + 1) * w]


def _to_residue_major(dst, src, c, seg):
    per = CHUNK // RESIDUES
    for r in range(RESIDUES):
        dst[pl.ds(pl.multiple_of(r * seg + c * per, per), per), :] = src[pl.ds(r, per, stride=RESIDUES), :]


def _from_residue_major(dst, dst_start, src, c, seg):
    per = CHUNK // RESIDUES
    for r in range(RESIDUES):
        dst[pl.ds(dst_start + r, per, stride=RESIDUES), :] = src[pl.ds(pl.multiple_of(r * seg + c * per, per), per), :]


def _stack_heads(t, head0):
    return jnp.concatenate([jnp.where(head0, t, 0.0), jnp.where(head0, 0.0, t)], axis=0).astype(BF16)


def _attention_forward(z, gq2, gk2, masks, shards):
    s = z.shape[0]
    seg = s // RESIDUES
    n_w = len(shards)

    def body(zq_ref, zk_ref, zv_ref, gq_ref, gk_ref, mask_ref, *rest):
        shard_refs, (o_ref, lse_ref), stacks = rest[:n_w], rest[n_w:n_w + 2], rest[n_w + 2:2 * n_w + 2]
        qf, kf, vf, o_st, l_st, tq, tk, tv, send_sems, recv_sems, local_sems = rest[2 * n_w + 2:]
        pair = pl.program_id(0)
        gather = _TwoLevelGather(stacks, send_sems, recv_sems)

        def own_copies():
            mine = gather.index(gather.me)
            return [pltpu.make_async_copy(shard_refs[a], stacks[a].at[mine], local_sems.at[a]) for a in range(n_w)]

        @pl.when(pair == 0)
        def _():
            for cp in own_copies():
                cp.start()
            gather.start(shard_refs)

        @pl.when(pair == 2)
        def _():
            gather.forward()

        bd = _head_blockdiag(1.0 / HEAD_DIM)
        gq = gq_ref[...] * (HEAD_DIM ** -0.5)
        gk = gk_ref[...]

        def prep(c, carry):
            rows = pl.ds(pl.multiple_of(c * CHUNK, CHUNK), CHUNK)
            zq = zq_ref[rows, :].astype(F32)
            zk = zk_ref[rows, :].astype(F32)
            tq[...] = (zq * lax.rsqrt(_head_sum(zq * zq, bd) + EPS)) * gq
            tk[...] = (zk * lax.rsqrt(_head_sum(zk * zk, bd) + EPS)) * gk
            tv[...] = zv_ref[rows, :].astype(F32)
            _to_residue_major(qf, tq, c, seg)
            _to_residue_major(kf, tk, c, seg)
            _to_residue_major(vf, tv, c, seg)
            return carry

        lax.fori_loop(0, s // CHUNK, prep, 0)

        head0 = lax.broadcasted_iota(jnp.int32, (BLOCK, 128), 1) < HEAD_DIM

        def run_branch(branch, d):
            n_runs, w, steps, group = _branch_geometry(d, seg)

            def step(u, grp):
                starts = _run_starts(u, grp, d, seg, n_runs, w, group)
                kt = [_load_tile(kf, starts, w, b).astype(BF16) for b in range(-1, group)]
                vt = [_load_tile(vf, starts, w, b).astype(BF16) for b in range(-1, group)]
                scores = []
                for b in range(group):
                    q2 = _stack_heads(_load_tile(qf, starts, w, b), head0)
                    kc = jnp.concatenate([kt[b], kt[b + 1]], axis=0)
                    mask = mask_ref[branch, 0]
                    if b == 0:
                        mask = jnp.where(grp == 0, mask_ref[branch, 1], mask)
                    scores.append(_dot(q2, kc, NT) + mask)
                sc = jnp.concatenate(scores, axis=0)
                m = jnp.max(sc, axis=-1, keepdims=True)
                e = jnp.exp(sc - m).astype(BF16)
                ones = jnp.ones((BLOCK, 128), BF16)
                fresh = []
                for b in range(group):
                    rows = slice(2 * BLOCK * b, 2 * BLOCK * (b + 1))
                    v_aug = jnp.concatenate([jnp.concatenate([vt[b], ones], axis=1),
                                             jnp.concatenate([vt[b + 1], ones], axis=1)], axis=0)
                    o2 = _dot(e[rows], v_aug, NN)
                    den = o2[:, 128:]
                    o2 = o2[:, :128] * (1.0 / den)
                    l2 = m[rows] + jnp.log(den)
                    fresh.append((jnp.where(head0, o2[:BLOCK], o2[BLOCK:]), jnp.where(head0, l2[:BLOCK], l2[BLOCK:])))
                for b in range(group):
                    o_new, l_new = fresh[b]
                    if branch > 0:
                        l_old = _load_tile(l_st, starts, w, b)
                        top = jnp.maximum(l_old, l_new)
                        e_old = jnp.exp(l_old - top)
                        e_new = jnp.exp(l_new - top)
                        tot = e_old + e_new
                        inv = 1.0 / tot
                        o_new = _load_tile(o_st, starts, w, b) * (e_old * inv) + o_new * (e_new * inv)
                        l_new = top + jnp.log(tot)
                    _store_tile(o_st, starts, w, b, o_new)
                    _store_tile(l_st, starts, w, b, l_new)

            def unit(u, carry):
                def inner(grp, carry2):
                    step(u, grp)
                    return carry2
                return lax.fori_loop(0, steps, inner, carry)

            lax.fori_loop(0, d, unit, 0)

        for branch, d in enumerate(DILATIONS):
            run_branch(branch, d)

        def finish(c, carry):
            _from_residue_major(o_ref, c * CHUNK, o_st, c, seg)
            _from_residue_major(lse_ref, c * CHUNK, l_st, c, seg)
            return carry

        lax.fori_loop(0, s // CHUNK, finish, 0)

        @pl.when(pair == N_PAIRS - 1)
        def _():
            gather.finish()
            for cp in own_copies():
                cp.wait()

    def col(c0):
        return pl.BlockSpec((s, 128), lambda p: (0, c0 + p), pipeline_mode=pl.Buffered(1))

    hbm = pl.BlockSpec(memory_space=pl.ANY)
    out = pl.pallas_call(
        body, name="attention_forward", grid=(N_PAIRS,),
        in_specs=[col(12), col(16), col(20), _const_spec((1, 128)), _const_spec((1, 128)), _const_spec(masks.shape)]
        + [hbm] * n_w,
        out_specs=[pl.BlockSpec((s, 128), lambda p: (0, p)), pl.BlockSpec((s, 128), lambda p: (0, p))] + [hbm] * n_w,
        out_shape=[jax.ShapeDtypeStruct((s, D_ATTN), F32), jax.ShapeDtypeStruct((s, D_ATTN), F32)]
        + [jax.ShapeDtypeStruct((N_DEV,) + w.shape, w.dtype) for w in shards],
        scratch_shapes=[pltpu.VMEM((s, 128), F32)] * 5 + [pltpu.VMEM((CHUNK, 128), F32)] * 3
        + [pltpu.SemaphoreType.DMA((7 * n_w,)), pltpu.SemaphoreType.DMA((7 * n_w,)), pltpu.SemaphoreType.DMA((n_w,))],
        compiler_params=_params(("arbitrary",), vmem=VMEM_LIMIT),
    )(z, z, z, gq2, gk2, masks, *shards)
    return out[0], out[1], out[2:]


def _conv_forward(zt, zprev, w, first):
    u, gb, gc = zt[:, :D_CONV], zt[:, D_CONV:2 * D_CONV], zt[:, 2 * D_CONV:]
    cu = gc * u
    cu_prev = jnp.where(first, 0.0, zprev[:, 2 * D_CONV:] * zprev[:, :D_CONV])
    row = lax.broadcasted_iota(jnp.int32, cu.shape, 0)
    cu1 = jnp.where(row >= 1, pltpu.roll(cu, 1, 0), cu_prev[7:8, :])
    cu2 = jnp.where(row >= 2, pltpu.roll(cu, 2, 0), jnp.where(row == 1, cu_prev[7:8, :], cu_prev[6:7, :]))
    cv = w[0:1, :] * cu2 + w[1:2, :] * cu1 + w[2:3, :] * cu
    return u, gb, gc, cu, cu1, cu2, cv


def _mix_out(z, y_attn, x, conv_w, g_conv_out, g_attn_out, g_ffn, w_out, tm):
    s = x.shape[0]
    hb = tm // 8

    def body(z_ref, zp_ref, ya_ref, x_ref, cw_ref, gc_ref, ga_ref, gf_ref, w_ref, mix_ref, x2_ref, h2_ref):
        first = pl.program_id(0) == 0
        zt = z_ref[...].astype(F32)
        _, gb, _, _, _, _, cv = _conv_forward(zt, zp_ref[...].astype(F32), cw_ref[...], first)
        nc, _ = _rms(gb * cv)
        na, _ = _rms(ya_ref[...])
        mix = jnp.concatenate([nc * gc_ref[...], na * ga_ref[...]], axis=-1).astype(BF16)
        mix_ref[...] = mix
        x2 = x_ref[...] + _dot(mix, w_ref[...], NN)
        x2_ref[...] = x2
        xh, _ = _rms(x2)
        h2_ref[...] = (xh * gf_ref[...]).astype(BF16)

    tile = lambda w: pl.BlockSpec((tm, w), lambda i: (i, 0))
    return pl.pallas_call(
        body, name="mix_out", grid=(s // tm,),
        in_specs=[tile(3 * D_CONV), pl.BlockSpec((8, 3 * D_CONV), lambda i: (jnp.maximum(i * hb - 1, 0), 0)),
                  tile(D_ATTN), tile(D_MODEL), _const_spec((8, D_CONV)), _const_spec((1, D_CONV)), _const_spec((1, D_ATTN)),
                  _const_spec((1, D_MODEL)), _const_spec((D_MODEL, D_MODEL))],
        out_specs=[tile(D_MODEL)] * 3,
        out_shape=[jax.ShapeDtypeStruct((s, D_MODEL), BF16), jax.ShapeDtypeStruct((s, D_MODEL), F32),
                   jax.ShapeDtypeStruct((s, D_MODEL), BF16)],
        compiler_params=_params(("arbitrary",), vmem=VMEM_LIMIT),
    )(z, z, y_attn, x, conv_w, g_conv_out, g_attn_out, g_ffn, w_out)


FF_CHUNK = 256


def _ffn_forward(h2, x2, target, w_gate_t, w_up_t, w_down, tm):
    s = h2.shape[0]

    def body(h_ref, x2_ref, t_ref, wg_ref, wu_ref, wd_ref, a_ref, b_ref, dy_ref, loss_ref):
        @pl.when(pl.program_id(0) == 0)
        def _():
            loss_ref[...] = jnp.zeros_like(loss_ref)

        h = h_ref[...]
        y = x2_ref[...]
        for c0 in range(0, D_FF, FF_CHUNK):
            rows = slice(c0, c0 + FF_CHUNK)
            a = _dot(h, wg_ref[rows, :], NT)
            b = _dot(h, wu_ref[rows, :], NT)
            a_ref[:, rows] = a.astype(BF16)
            b_ref[:, rows] = b.astype(BF16)
            f = (a * jax.nn.sigmoid(a)) * b
            y = y + _dot(f.astype(BF16), wd_ref[rows, :], NN)
        err = y - t_ref[...]
        dy_ref[...] = err * (1.0 / D_MODEL)
        loss_ref[...] += jnp.sum(err * err)

    tile = lambda w: pl.BlockSpec((tm, w), lambda i: (i, 0))
    return pl.pallas_call(
        body, name="ffn_forward", grid=(s // tm,),
        in_specs=[tile(D_MODEL), tile(D_MODEL), tile(D_MODEL)] + [_const_spec((D_FF, D_MODEL))] * 3,
        out_specs=[tile(D_FF), tile(D_FF), tile(D_MODEL), pl.BlockSpec((8, 128), lambda i: (0, 0))],
        out_shape=[jax.ShapeDtypeStruct((s, D_FF), BF16), jax.ShapeDtypeStruct((s, D_FF), BF16),
                   jax.ShapeDtypeStruct((s, D_MODEL), F32), jax.ShapeDtypeStruct((8, 128), F32)],
        compiler_params=_params(("arbitrary",), vmem=VMEM_LIMIT),
    )(h2, x2, target, w_gate_t, w_up_t, w_down)


def _ffn_backward(dy, a, b, x2, g_ffn, w_gate_t, w_up_t, w_down, tm):
    s = dy.shape[0]

    def body(dy_ref, a_ref, b_ref, x2_ref, g_ref, wg_ref, wu_ref, wd_ref, da_ref, db_ref, f_ref, dx2_ref, dg_ref):
        @pl.when(pl.program_id(0) == 0)
        def _():
            dg_ref[...] = jnp.zeros_like(dg_ref)

        dy_f = dy_ref[...]
        dyb = dy_f.astype(BF16)
        dh = jnp.zeros((tm, D_MODEL), F32)
        for c0 in range(0, D_FF, FF_CHUNK):
            rows = slice(c0, c0 + FF_CHUNK)
            df = _dot(dyb, wd_ref[rows, :], NT)
            av = a_ref[:, rows].astype(F32)
            bv = b_ref[:, rows].astype(F32)
            sig = jax.nn.sigmoid(av)
            silu = av * sig
            f_ref[:, rows] = (silu * bv).astype(BF16)
            da = (df * bv * (sig * (1.0 + av * (1.0 - sig)))).astype(BF16)
            db = (df * silu).astype(BF16)
            da_ref[:, rows] = da
            db_ref[:, rows] = db
            dh = dh + _dot(da, wg_ref[rows, :], NN) + _dot(db, wu_ref[rows, :], NN)
        xh, r = _rms(x2_ref[...])
        dg_ref[...] += jnp.sum(dh * xh, axis=0, keepdims=True)
        dx2_ref[...] = dy_f + _rms_bwd(dh, xh, r, g_ref[...])

    tile = lambda w: pl.BlockSpec((tm, w), lambda i: (i, 0))
    return pl.pallas_call(
        body, name="ffn_backward", grid=(s // tm,),
        in_specs=[tile(D_MODEL), tile(D_FF), tile(D_FF), tile(D_MODEL), _const_spec((1, D_MODEL))]
        + [_const_spec((D_FF, D_MODEL))] * 3,
        out_specs=[tile(D_FF), tile(D_FF), tile(D_FF), tile(D_MODEL), pl.BlockSpec((1, D_MODEL), lambda i: (0, 0))],
        out_shape=[jax.ShapeDtypeStruct((s, D_FF), BF16)] * 3
        + [jax.ShapeDtypeStruct((s, D_MODEL), F32), jax.ShapeDtypeStruct((1, D_MODEL), F32)],
        compiler_params=_params(("arbitrary",), vmem=VMEM_LIMIT),
    )(dy, a, b, x2, g_ffn, w_gate_t, w_up_t, w_down)


def _out_backward(dx2, z, y_attn, conv_w, g_conv_out, g_attn_out, w_out, tm):
    s = dx2.shape[0]
    hb = tm // 8

    def body(dx2_ref, z_ref, zp_ref, ya_ref, cw_ref, gc_ref, ga_ref, w_ref, dya_ref, dgb_ref, dcv_ref, dgc_ref, dga_ref):
        first = pl.program_id(0) == 0

        @pl.when(first)
        def _():
            dgc_ref[...] = jnp.zeros_like(dgc_ref)
            dga_ref[...] = jnp.zeros_like(dga_ref)

        dmix = _dot(dx2_ref[...].astype(BF16), w_ref[...], NT)
        _, gb, _, _, _, _, cv = _conv_forward(z_ref[...].astype(F32), zp_ref[...].astype(F32), cw_ref[...], first)
        ych, rc = _rms(gb * cv)
        dnc = dmix[:, :D_CONV]
        dgc_ref[...] += jnp.sum(dnc * ych, axis=0, keepdims=True)
        dyc = _rms_bwd(dnc, ych, rc, gc_ref[...])
        dgb_ref[...] = (dyc * cv).astype(BF16)
        dcv_ref[...] = dyc * gb
        yah, ra = _rms(ya_ref[...])
        dna = dmix[:, D_CONV:]
        dga_ref[...] += jnp.sum(dna * yah, axis=0, keepdims=True)
        dya_ref[...] = _rms_bwd(dna, yah, ra, ga_ref[...])

    tile = lambda w: pl.BlockSpec((tm, w), lambda i: (i, 0))
    vec = pl.BlockSpec((1, D_CONV), lambda i: (0, 0))
    return pl.pallas_call(
        body, name="out_backward", grid=(s // tm,),
        in_specs=[tile(D_MODEL), tile(3 * D_CONV), pl.BlockSpec((8, 3 * D_CONV), lambda i: (jnp.maximum(i * hb - 1, 0), 0)),
                  tile(D_ATTN), _const_spec((8, D_CONV)), _const_spec((1, D_CONV)), _const_spec((1, D_ATTN)),
                  _const_spec((D_MODEL, D_MODEL))],
        out_specs=[tile(D_ATTN), tile(D_CONV), tile(D_CONV), vec, vec],
        out_shape=[jax.ShapeDtypeStruct((s, D_ATTN), F32), jax.ShapeDtypeStruct((s, D_CONV), BF16),
                   jax.ShapeDtypeStruct((s, D_CONV), F32), jax.ShapeDtypeStruct((1, D_CONV), F32),
                   jax.ShapeDtypeStruct((1, D_ATTN), F32)],
        compiler_params=_params(("arbitrary",), vmem=VMEM_LIMIT),
    )(dx2, z, z, y_attn, conv_w, g_conv_out, g_attn_out, w_out)


def _attention_backward(z, o, lse, dya, gq2, gk2, masks, grad_stacks):
    s = z.shape[0]
    seg = s // RESIDUES
    n_g = len(grad_stacks)

    def body(z_hbm, o_hbm, lse_hbm, do_hbm, gq_ref, gk_ref, mask_ref, *rest):
        stack_refs = rest[:n_g]
        dzq_ref, dzk_ref, dzv_ref, dgq_ref, dgk_ref = rest[n_g:n_g + 5]
        landing = rest[n_g + 5:2 * n_g + 5]
        (qf, kf, vf, dof, stats, dq, dk, dv, t0, t1, t2, zbuf, fbuf, sems, send_sems, recv_sems) = rest[2 * n_g + 5:]
        pair = pl.program_id(0)
        scatter = _DirectReduceScatter(stack_refs, landing, send_sems, recv_sems)

        @pl.when(pair == 0)
        def _():
            scatter.start()

        n_chunks = s // CHUNK
        bd_mean = _head_blockdiag(1.0 / HEAD_DIM)
        bd_sum = _head_blockdiag(1.0)
        gq = gq_ref[...] * (HEAD_DIM ** -0.5)
        gk = gk_ref[...]
        lane = lax.broadcasted_iota(jnp.int32, (CHUNK, 128), 1) % HEAD_DIM

        def chunk_copies(c, slot, n_z, n_f):
            rows = pl.ds(pl.multiple_of(c * CHUNK, CHUNK), CHUNK)
            cps = []
            for i, col in enumerate((12, 16, 20)[:n_z]):
                cols = pl.ds(pl.multiple_of((col + pair) * 128, 128), 128)
                cps.append(pltpu.make_async_copy(z_hbm.at[rows, cols], zbuf.at[slot, i], sems.at[slot, i]))
            for i, src in enumerate((do_hbm, o_hbm, lse_hbm)[:n_f]):
                cols = pl.ds(pl.multiple_of(pair * 128, 128), 128)
                cps.append(pltpu.make_async_copy(src.at[rows, cols], fbuf.at[slot, i], sems.at[slot, 3 + i]))
            return cps

        def prefetched(c, n_z, n_f):
            slot = c % 2

            @pl.when(c + 1 < n_chunks)
            def _():
                for cp in chunk_copies(c + 1, 1 - slot, n_z, n_f):
                    cp.start()

            for cp in chunk_copies(c, slot, n_z, n_f):
                cp.wait()
            return slot

        for cp in chunk_copies(0, 0, 3, 3):
            cp.start()

        def prep(c, carry):
            slot = prefetched(c, 3, 3)
            rows = pl.ds(pl.multiple_of(c * CHUNK, CHUNK), CHUNK)
            zq = zbuf[slot, 0].astype(F32)
            t0[...] = (zq * lax.rsqrt(_head_sum(zq * zq, bd_mean) + EPS)) * gq
            zk = zbuf[slot, 1].astype(F32)
            t1[...] = (zk * lax.rsqrt(_head_sum(zk * zk, bd_mean) + EPS)) * gk
            t2[...] = zbuf[slot, 2].astype(F32)
            _to_residue_major(qf, t0, c, seg)
            _to_residue_major(kf, t1, c, seg)
            _to_residue_major(vf, t2, c, seg)
            _to_residue_major(dof, fbuf.at[slot, 0], c, seg)
            delta = _head_sum(fbuf[slot, 0] * fbuf[slot, 1], bd_sum)
            t0[...] = jnp.where(lane == 0, fbuf[slot, 2], jnp.where(lane == 1, delta, 0.0))
            _to_residue_major(stats, t0, c, seg)
            zero = jnp.zeros((CHUNK, 128), F32)
            dq[rows, :] = zero
            dk[rows, :] = zero
            dv[rows, :] = zero
            return carry

        lax.fori_loop(0, n_chunks, prep, 0)

        head0 = lax.broadcasted_iota(jnp.int32, (BLOCK, 128), 1) < HEAD_DIM

        def run_branch(branch, d):
            n_runs, w, steps, group = _branch_geometry(d, seg)

            def step(u, grp):
                starts = _run_starts(u, grp, d, seg, n_runs, w, group)
                kt = [_load_tile(kf, starts, w, b).astype(BF16) for b in range(-1, group)]
                vt = [_load_tile(vf, starts, w, b).astype(BF16) for b in range(-1, group)]
                dk_t = [jnp.zeros((BLOCK, 128), F32)] * (group + 1)
                dv_t = [jnp.zeros((BLOCK, 128), F32)] * (group + 1)
                dq_t = []
                q2s, do2s, kcs, scores, dps, lses, deltas = [], [], [], [], [], [], []
                for b in range(group):
                    q2 = _stack_heads(_load_tile(qf, starts, w, b), head0)
                    do2 = _stack_heads(_load_tile(dof, starts, w, b), head0)
                    st = _load_tile(stats, starts, w, b)
                    lses += [st[:, 0:1], st[:, HEAD_DIM:HEAD_DIM + 1]]
                    deltas += [st[:, 1:2], st[:, HEAD_DIM + 1:HEAD_DIM + 2]]
                    kc = jnp.concatenate([kt[b], kt[b + 1]], axis=0)
                    vc = jnp.concatenate([vt[b], vt[b + 1]], axis=0)
                    mask = mask_ref[branch, 0]
                    if b == 0:
                        mask = jnp.where(grp == 0, mask_ref[branch, 1], mask)
                    scores.append(_dot(q2, kc, NT) + mask)
                    dps.append(_dot(do2, vc, NT))
                    q2s.append(q2)
                    do2s.append(do2)
                    kcs.append(kc)
                p_all = jnp.exp(jnp.concatenate(scores, axis=0) - jnp.concatenate(lses, axis=0))
                ds_all = (p_all * (jnp.concatenate(dps, axis=0) - jnp.concatenate(deltas, axis=0))).astype(BF16)
                p_all = p_all.astype(BF16)
                for b in range(group):
                    rows = slice(2 * BLOCK * b, 2 * BLOCK * (b + 1))
                    ds, q2, do2, kc = ds_all[rows], q2s[b], do2s[b], kcs[b]
                    dq2 = _dot(ds, kc, NN)
                    dq_t.append(jnp.where(head0, dq2[:BLOCK], dq2[BLOCK:]))
                    dkc = _dot(ds, q2, TN)
                    dvc = _dot(p_all[rows], do2, TN)
                    dk_t[b] = dk_t[b] + dkc[:BLOCK]
                    dk_t[b + 1] = dk_t[b + 1] + dkc[BLOCK:]
                    dv_t[b] = dv_t[b] + dvc[:BLOCK]
                    dv_t[b + 1] = dv_t[b + 1] + dvc[BLOCK:]
                for b in range(group):
                    _add_tile(dq, starts, w, b, dq_t[b])
                for b in range(-1, group):
                    _add_tile(dk, starts, w, b, dk_t[b + 1])
                    _add_tile(dv, starts, w, b, dv_t[b + 1])

            def unit(u, carry):
                def inner(grp, carry2):
                    step(u, grp)
                    return carry2
                return lax.fori_loop(0, steps, inner, carry)

            lax.fori_loop(0, d, unit, 0)

        for branch, d in enumerate(DILATIONS):
            run_branch(branch, d)

        for cp in chunk_copies(0, 0, 2, 0):
            cp.start()

        def finish(c, carry):
            acc_q, acc_k = carry
            rows = pl.ds(pl.multiple_of(c * CHUNK, CHUNK), CHUNK)
            _from_residue_major(t0, 0, dq, c, seg)
            _from_residue_major(t1, 0, dk, c, seg)
            _from_residue_major(t2, 0, dv, c, seg)
            slot = prefetched(c, 2, 0)
            zq = zbuf[slot, 0].astype(F32)
            rq = lax.rsqrt(_head_sum(zq * zq, bd_mean) + EPS)
            qh = zq * rq
            dqn = t0[...]
            acc_q = acc_q + jnp.sum(dqn * qh, axis=0, keepdims=True)
            t = dqn * gq
            dzq_ref[rows, :] = (rq * (t - qh * _head_sum(t * qh, bd_mean))).astype(BF16)
            zk = zbuf[slot, 1].astype(F32)
            rk = lax.rsqrt(_head_sum(zk * zk, bd_mean) + EPS)
            kh = zk * rk
            dkn = t1[...]
            acc_k = acc_k + jnp.sum(dkn * kh, axis=0, keepdims=True)
            t = dkn * gk
            dzk_ref[rows, :] = (rk * (t - kh * _head_sum(t * kh, bd_mean))).astype(BF16)
            dzv_ref[rows, :] = t2[...].astype(BF16)
            return acc_q, acc_k

        zero = jnp.zeros((1, 128), F32)
        acc_q, acc_k = lax.fori_loop(0, s // CHUNK, finish, (zero, zero))
        dgq_ref[...] = acc_q * (HEAD_DIM ** -0.5)
        dgk_ref[...] = acc_k

        @pl.when(pair == N_PAIRS - 1)
        def _():
            scatter.finish()

    hbm = pl.BlockSpec(memory_space=pl.ANY)
    gain = pl.BlockSpec((None, 1, 128), lambda p: (p, 0, 0))
    dz_spec = pl.BlockSpec((s, 128), lambda p: (0, p))
    out = pl.pallas_call(
        body, name="attention_backward", grid=(N_PAIRS,),
        in_specs=[hbm, hbm, hbm, hbm, _const_spec((1, 128)), _const_spec((1, 128)), _const_spec(masks.shape)] + [hbm] * n_g,
        out_specs=[dz_spec, dz_spec, dz_spec, gain, gain] + [hbm] * n_g,
        out_shape=[jax.ShapeDtypeStruct((s, D_ATTN), BF16)] * 3 + [jax.ShapeDtypeStruct((N_PAIRS, 1, 128), F32)] * 2
        + [jax.ShapeDtypeStruct((N_DEV - 1,) + g.shape[1:], g.dtype) for g in grad_stacks],
        scratch_shapes=[pltpu.VMEM((s, 128), F32)] * 8 + [pltpu.VMEM((CHUNK, 128), F32)] * 3
        + [pltpu.VMEM((2, 3, CHUNK, 128), BF16), pltpu.VMEM((2, 3, CHUNK, 128), F32), pltpu.SemaphoreType.DMA((2, 6)),
           pltpu.SemaphoreType.DMA((7 * n_g,)), pltpu.SemaphoreType.DMA((7 * n_g,))],
        compiler_params=_params(("arbitrary",), vmem=VMEM_LIMIT),
    )(z, o, lse, dya, gq2, gk2, masks, *grad_stacks)
    return out[:5], out[5:]


def _in_backward(dcv, dgb, dzq, dzk, dzv, z, x, dx2, g_mix, conv_w, w_in_t, tm):
    s = x.shape[0]
    hb = tm // 8
    last_halo = s // 8 - 1

    def body(dcv_ref, dcn_ref, dgb_ref, dzq_ref, dzk_ref, dzv_ref, z_ref, zp_ref, x_ref, dx2_ref, g_ref, cw_ref, w_ref,
             dz_ref, dx_ref, dg_ref, dcw_ref):
        i = pl.program_id(0)
        first = i == 0
        last = i == pl.num_programs(0) - 1

        @pl.when(first)
        def _():
            dg_ref[...] = jnp.zeros_like(dg_ref)
            dcw_ref[...] = jnp.zeros_like(dcw_ref)

        w = cw_ref[...]
        u, _, gc, cu, cu1, cu2, _ = _conv_forward(z_ref[...].astype(F32), zp_ref[...].astype(F32), w, first)
        dcv_t = dcv_ref[...]
        nxt = jnp.where(last, 0.0, dcn_ref[...])
        row = lax.broadcasted_iota(jnp.int32, dcv_t.shape, 0)
        up1 = jnp.where(row < tm - 1, pltpu.roll(dcv_t, tm - 1, 0), nxt[0:1, :])
        up2 = jnp.where(row < tm - 2, pltpu.roll(dcv_t, tm - 2, 0), jnp.where(row == tm - 2, nxt[0:1, :], nxt[1:2, :]))
        dcu = w[2:3, :] * dcv_t + w[1:2, :] * up1 + w[0:1, :] * up2
        dcw = jnp.concatenate([jnp.sum(dcv_t * cu2, axis=0, keepdims=True), jnp.sum(dcv_t * cu1, axis=0, keepdims=True),
                               jnp.sum(dcv_t * cu, axis=0, keepdims=True), jnp.zeros((5, D_CONV), F32)], axis=0)
        dcw_ref[...] += dcw
        dz_ref[:, :D_CONV] = (dcu * gc).astype(BF16)
        dz_ref[:, D_CONV:2 * D_CONV] = dgb_ref[...]
        dz_ref[:, 2 * D_CONV:3 * D_CONV] = (dcu * u).astype(BF16)
        dz_ref[:, 3 * D_CONV:3 * D_CONV + D_ATTN] = dzq_ref[...]
        dz_ref[:, 3 * D_CONV + D_ATTN:3 * D_CONV + 2 * D_ATTN] = dzk_ref[...]
        dz_ref[:, 3 * D_CONV + 2 * D_ATTN:] = dzv_ref[...]
        dh = _dot(dz_ref[...], w_ref[...], NN)
        xh, r = _rms(x_ref[...])
        dg_ref[...] += jnp.sum(dh * xh, axis=0, keepdims=True)
        dx_ref[...] = dx2_ref[...] + _rms_bwd(dh, xh, r, g_ref[...])

    tile = lambda w: pl.BlockSpec((tm, w), lambda i: (i, 0))
    return pl.pallas_call(
        body, name="in_backward", grid=(s // tm,),
        in_specs=[tile(D_CONV), pl.BlockSpec((8, D_CONV), lambda i: (jnp.minimum((i + 1) * hb, last_halo), 0)),
                  tile(D_CONV), tile(D_ATTN), tile(D_ATTN), tile(D_ATTN),
                  tile(3 * D_CONV), pl.BlockSpec((8, 3 * D_CONV), lambda i: (jnp.maximum(i * hb - 1, 0), 0)),
                  tile(D_MODEL), tile(D_MODEL), _const_spec((1, D_MODEL)), _const_spec((8, D_CONV)),
                  _const_spec((D_IN, D_MODEL))],
        out_specs=[tile(D_IN), tile(D_MODEL), pl.BlockSpec((1, D_MODEL), lambda i: (0, 0)),
                   pl.BlockSpec((8, D_CONV), lambda i: (0, 0))],
        out_shape=[jax.ShapeDtypeStruct((s, D_IN), BF16), jax.ShapeDtypeStruct((s, D_MODEL), F32),
                   jax.ShapeDtypeStruct((1, D_MODEL), F32), jax.ShapeDtypeStruct((8, D_CONV), F32)],
        compiler_params=_params(("arbitrary",), vmem=VMEM_LIMIT),
    )(dcv, dcv, dgb, dzq, dzk, dzv, z, z, x, dx2, g_mix, conv_w, w_in_t)


def _weight_grad(name, a, b, tn, tk):
    s, n = a.shape
    steps = s // tk

    def body(a_ref, b_ref, out_ref, acc):
        k = pl.program_id(1)

        @pl.when(k == 0)
        def _():
            acc[...] = jnp.zeros_like(acc)

        acc[...] += _dot(a_ref[...], b_ref[...].astype(BF16), TN)

        @pl.when(k == steps - 1)
        def _():
            out_ref[...] = acc[...].astype(BF16)

    return pl.pallas_call(
        body, name=name, grid=(n // tn, steps),
        in_specs=[pl.BlockSpec((tk, tn), lambda j, k: (k, j)), pl.BlockSpec((tk, D_MODEL), lambda j, k: (k, 0))],
        out_specs=pl.BlockSpec((tn, D_MODEL), lambda j, k: (j, 0)),
        out_shape=jax.ShapeDtypeStruct((n, D_MODEL), BF16),
        scratch_shapes=[pltpu.VMEM((tn, D_MODEL), F32)],
        compiler_params=_params(("parallel", "arbitrary"), vmem=VMEM_LIMIT),
    )(a, b)


def _adamw(name, w, g, m, v):
    def body(w_ref, g_ref, m_ref, v_ref, d_ref, nm_ref, nv_ref):
        gv = g_ref[...]
        nm = ADAM_B1 * m_ref[...] + (1.0 - ADAM_B1) * gv
        nv = ADAM_B2 * v_ref[...] + (1.0 - ADAM_B2) * (gv * gv)
        m_hat = nm / (1.0 - ADAM_B1 ** ADAM_STEP)
        v_hat = nv / (1.0 - ADAM_B2 ** ADAM_STEP)
        d_ref[...] = -ADAM_LR * (m_hat / (jnp.sqrt(v_hat) + ADAM_EPS) + ADAM_WD * w_ref[...])
        nm_ref[...] = nm
        nv_ref[...] = nv

    rows = w.shape[0]
    tr = 256 if rows % 256 == 0 else rows
    spec = pl.BlockSpec((tr, w.shape[1]), lambda i: (i, 0))
    return pl.pallas_call(
        body, name=name, grid=(rows // tr,),
        in_specs=[spec] * 4, out_specs=[spec] * 3,
        out_shape=[jax.ShapeDtypeStruct(w.shape, F32)] * 3,
        compiler_params=_params(("parallel",)),
    )(w, g, m, v)


def kernel(x, g_mix, w_in, conv_w, g_q, g_k, g_conv_out, g_attn_out, w_out, g_ffn, w_gate, w_up, w_down, loss_target, m_g_mix, m_w_in, m_conv_w, m_g_q, m_g_k, m_g_conv_out, m_g_attn_out, m_w_out, m_g_ffn, m_w_gate, m_w_up, m_w_down, v_g_mix, v_w_in, v_conv_w, v_g_q, v_g_k, v_g_conv_out, v_g_attn_out, v_w_out, v_g_ffn, v_w_gate, v_w_up, v_w_down):
    s = x.shape[1]
    tm = min(512, s)
    xs, target = x[0], loss_target[0]
    px, py, pc = lax.axis_index("x"), lax.axis_index("y"), lax.axis_index("c")
    device = 4 * px + 2 * py + pc

    conv_block = jnp.zeros((8, 128), F32).at[:3, :HEAD_DIM].set(conv_w[0])
    first, shards = [w_in[0].T, conv_block], [w_out[0], w_gate[0].T, w_up[0].T, w_down[0]]
    w_in_g, conv_g, *shards = _gather_first_weights(first, [BF16, F32], shards)
    w_in_t = w_in_g.reshape(D_IN, D_MODEL)
    conv_full = jnp.transpose(conv_g[:, :3, :HEAD_DIM], (1, 0, 2)).reshape(3, D_CONV)
    conv_full = jnp.concatenate([conv_full, jnp.zeros((5, D_CONV), F32)], axis=0)
    gq2 = jnp.concatenate([g_q, g_q], axis=-1)
    gk2 = jnp.concatenate([g_k, g_k], axis=-1)

    h1, z = _in_projection(xs, g_mix, w_in_t, tm)
    masks = _permuted_masks()
    y_attn, lse, gathered = _attention_forward(z, gq2, gk2, masks, shards)
    w_out_f = gathered[0].reshape(D_MODEL, D_MODEL)
    w_gate_t, w_up_t, w_down_f = [g.reshape(D_FF, D_MODEL) for g in gathered[1:]]
    mix, x2, h2 = _mix_out(z, y_attn, xs, conv_full, g_conv_out, g_attn_out, g_ffn, w_out_f, tm)
    tf = min(256, s)
    a, b, dy, sq_err = _ffn_forward(h2, x2, target, w_gate_t, w_up_t, w_down_f, tf)

    tk = min(512, s)
    da, db, f, dx2, dg_ffn = _ffn_backward(dy, a, b, x2, g_ffn, w_gate_t, w_up_t, w_down_f, tf)
    dya, dgb, dcv, dg_conv_out, dg_attn_out = _out_backward(dx2, z, y_attn, conv_full, g_conv_out, g_attn_out, w_out_f, tm)
    early = [
        _weight_grad("grad_w_out", mix, dx2, D_MODEL, tk),
        _weight_grad("grad_w_gate", da, h2, D_FF // 2, tk),
        _weight_grad("grad_w_up", db, h2, D_FF // 2, tk),
        _weight_grad("grad_w_down", f, dy, D_FF // 2, tk),
    ]
    early = [g.reshape(N_DEV, g.shape[0] // N_DEV, D_MODEL) for g in early]
    (dzq, dzk, dzv, dgq_pairs, dgk_pairs), landed = _attention_backward(z, y_attn, lse, dya, gq2, gk2, masks, early)
    dz, grad_x, dg_mix, dconv = _in_backward(dcv, dgb, dzq, dzk, dzv, z, xs, dx2, g_mix, conv_full, w_in_t, tm)
    dev = jnp.reshape(device, (1,)).astype(jnp.int32)
    gw_out, gw_gate_t, gw_up_t, gw_down = [
        _add_received("grad_sum_" + n, dev, g, r) for n, g, r in zip(("w_out", "w_gate", "w_up", "w_down"), early, landed)]

    late = [_weight_grad("grad_w_in", dz, h1, D_IN // 2, tk).reshape(4, 2, D_IN // N_DEV, D_MODEL)]
    from_sibling = _reduce_scatter_cores(late)
    core = jnp.reshape(pc, (1,)).astype(jnp.int32)
    partials = _add_core_partials(core, late, from_sibling)
    from_chips = _reduce_scatter_chips(partials)
    chip = jnp.reshape(2 * px + py, (1,)).astype(jnp.int32)
    (gw_in_t,) = _add_chip_partials(chip, partials, from_chips)
    gw_in, gw_gate, gw_up = gw_in_t.T, gw_gate_t.T, gw_up_t.T

    dg_q = jnp.sum(dgq_pairs.reshape(2 * N_PAIRS, HEAD_DIM), axis=0)
    dg_k = jnp.sum(dgk_pairs.reshape(2 * N_PAIRS, HEAD_DIM), axis=0)
    zeros = lambda n: jnp.zeros((n,), F32)
    small = jnp.stack([
        dg_mix[0], dg_ffn[0],
        jnp.concatenate([dg_conv_out[0], dg_attn_out[0]]),
        jnp.concatenate([dg_q, dg_k, zeros(D_MODEL - 2 * HEAD_DIM)]),
        jnp.concatenate([dconv[0], dconv[1]]),
        jnp.concatenate([dconv[2], zeros(D_CONV)]),
        jnp.concatenate([sq_err[0, :1], zeros(D_MODEL - 1)]),
        zeros(D_MODEL),
    ])
    total = _all_reduce_small(small)
    loss = total[6, 0] * (0.5 / D_MODEL)
    gg_mix, gg_ffn = total[0:1], total[1:2]
    gg_conv_out, gg_attn_out = total[2:3, :D_CONV], total[2:3, D_CONV:]
    gg_q, gg_k = total[3:4, :HEAD_DIM], total[3:4, HEAD_DIM:2 * HEAD_DIM]
    conv_total = jnp.stack([total[4, :D_CONV], total[4, D_CONV:], total[5, :D_CONV]])
    g_conv = lax.dynamic_slice(conv_total, (0, device * HEAD_DIM), (3, HEAD_DIM))

    names = ["g_mix", "w_in", "conv_w", "g_q", "g_k", "g_conv_out", "g_attn_out", "w_out", "g_ffn", "w_gate", "w_up", "w_down"]
    weights = [g_mix, w_in[0], conv_w[0], g_q, g_k, g_conv_out, g_attn_out, w_out[0], g_ffn, w_gate[0], w_up[0], w_down[0]]
    grad_list = [gg_mix, gw_in, g_conv, gg_q, gg_k, gg_conv_out, gg_attn_out, gw_out, gg_ffn, gw_gate, gw_up, gw_down]
    m_list = [m_g_mix, m_w_in[0], m_conv_w[0], m_g_q, m_g_k, m_g_conv_out, m_g_attn_out, m_w_out[0], m_g_ffn, m_w_gate[0], m_w_up[0], m_w_down[0]]
    v_list = [v_g_mix, v_w_in[0], v_conv_w[0], v_g_q, v_g_k, v_g_conv_out, v_g_attn_out, v_w_out[0], v_g_ffn, v_w_gate[0], v_w_up[0], v_w_down[0]]
    stacked = {"w_in", "conv_w", "w_out", "w_gate", "w_up", "w_down"}
    out_g, out_d, out_m, out_v = [], [], [], []
    for name, w, g, m, v in zip(names, weights, grad_list, m_list, v_list):
        delta, new_m, new_v = _adamw("adamw_" + name, w, g, m, v)
        lift = (lambda t: t[None]) if name in stacked else (lambda t: t)
        out_g.append(lift(g))
        out_d.append(lift(delta))
        out_m.append(lift(new_m))
        out_v.append(lift(new_v))
    return (loss, grad_x[None], *out_g, *out_d, *out_m, *out_v)
```

```python
import jax
import jax.numpy as jnp
import numpy as np
from jax import lax
from jax.experimental import pallas as pl
from jax.experimental.pallas import tpu as pltpu

F32 = jnp.float32
BF16 = jnp.bfloat16

D_MODEL = 1024
D_CONV = 512
D_ATTN = 512
D_IN = 3072
D_FF = 2816
HEAD_DIM = 64
N_PAIRS = 4
BLOCK = 128
DILATIONS = (1, 4, 16)
N_DEV = 8
EPS = 1e-6
NEG = -1e30

ADAM_LR = 0.001
ADAM_B1 = 0.9
ADAM_B2 = 0.999
ADAM_EPS = 1e-08
ADAM_WD = 0.01
ADAM_STEP = 10

NT = (((1,), (1,)), ((), ()))
NN = (((1,), (0,)), ((), ()))
TN = (((0,), (0,)), ((), ()))
MESH = pl.DeviceIdType.MESH

VMEM_LIMIT = 56 * 1024 * 1024


def _params(semantics=None, vmem=None):
    kw = {}
    if semantics is not None:
        kw["dimension_semantics"] = semantics
    if vmem is not None:
        kw["vmem_limit_bytes"] = vmem
    return pltpu.CompilerParams(**kw)


def _dot(a, b, dims):
    return lax.dot_general(a, b, dims, preferred_element_type=F32)


def _const_spec(shape):
    n = len(shape)
    return pl.BlockSpec(shape, lambda *_: (0,) * n, pipeline_mode=pl.Buffered(1))


def _rms(x):
    r = lax.rsqrt(jnp.mean(x * x, axis=-1, keepdims=True) + EPS)
    return x * r, r


def _rms_bwd(dy, xh, r, g):
    t = dy * g
    return r * (t - xh * jnp.mean(t * xh, axis=-1, keepdims=True))


def _head_blockdiag(scale):
    i = lax.broadcasted_iota(jnp.int32, (128, 128), 0) // HEAD_DIM
    j = lax.broadcasted_iota(jnp.int32, (128, 128), 1) // HEAD_DIM
    return jnp.where(i == j, scale, 0.0).astype(BF16)


def _head_sum(x, bd):
    hi = x.astype(BF16)
    lo = (x - hi.astype(F32)).astype(BF16)
    return _dot(hi, bd, NN) + _dot(lo, bd, NN)


def _place():
    return lax.axis_index("x"), lax.axis_index("y"), lax.axis_index("c")


class _TwoLevelGather:
    def __init__(self, stacks, send_sems, recv_sems):
        self.stacks, self.send_sems, self.recv_sems = stacks, send_sems, recv_sems
        x, y, c = _place()
        self.c = c
        self.me, self.sibling = (x, y, c), (x, y, 1 - c)
        self.chips = [(x, 1 - y), (1 - x, y), (1 - x, 1 - y)]

    @staticmethod
    def index(p):
        return 4 * p[0] + 2 * p[1] + p[2]

    def copy(self, a, k, block, to, src=None):
        dst = self.stacks[a].at[self.index(block)]
        return pltpu.make_async_remote_copy(
            src_ref=dst if src is None else src, dst_ref=dst,
            send_sem=self.send_sems.at[7 * a + k], recv_sem=self.recv_sems.at[7 * a + k],
            device_id=to, device_id_type=MESH)

    def first(self, a, src=None):
        cps = [self.copy(a, 0, self.me, self.sibling, src)]
        return cps + [self.copy(a, 1 + j, self.me, (*chip, self.c), src) for j, chip in enumerate(self.chips)]

    def forwards(self, a):
        return [self.copy(a, 4 + j, (*chip, self.c), self.sibling) for j, chip in enumerate(self.chips)]

    def start(self, srcs=None):
        for a in range(len(self.stacks)):
            for cp in self.first(a, None if srcs is None else srcs[a]):
                cp.start()

    def forward(self):
        for a in range(len(self.stacks)):
            for j, (chip, fwd) in enumerate(zip(self.chips, self.forwards(a))):
                self.copy(a, 1 + j, (*chip, self.c), self.me).wait_recv()
                fwd.start()

    def finish(self):
        for a in range(len(self.stacks)):
            self.copy(a, 0, self.sibling, self.me).wait_recv()
            for j, chip in enumerate(self.chips):
                self.copy(a, 4 + j, (*chip, 1 - self.c), self.me).wait_recv()
        for a in range(len(self.stacks)):
            for cp in self.first(a) + self.forwards(a):
                cp.wait_send()


def _gather_first_weights(gathered_parts, gathered_dtypes, cast_parts):
    n, m = len(gathered_parts), len(cast_parts)

    def body(*refs):
        ins, casts_in = refs[:n], refs[n:n + m]
        outs, casts_out = refs[n + m:2 * n + m], refs[2 * n + m:2 * n + 2 * m]
        send_sems, recv_sems = refs[2 * n + 2 * m], refs[2 * n + 2 * m + 1]
        gather = _TwoLevelGather(outs, send_sems, recv_sems)
        for a in range(n):
            outs[a][gather.index(gather.me)] = ins[a][...].astype(outs[a].dtype)
        gather.start()
        for a in range(m):
            casts_out[a][...] = casts_in[a][...].astype(BF16)
        gather.forward()
        gather.finish()

    vm = pl.BlockSpec(memory_space=pltpu.VMEM)
    return pl.pallas_call(
        body, name="weight_all_gather",
        out_shape=[jax.ShapeDtypeStruct((N_DEV,) + p.shape, dt) for p, dt in zip(gathered_parts, gathered_dtypes)]
        + [jax.ShapeDtypeStruct(p.shape, BF16) for p in cast_parts],
        in_specs=[vm] * (n + m), out_specs=[vm] * (n + m),
        scratch_shapes=[pltpu.SemaphoreType.DMA((7 * n,)), pltpu.SemaphoreType.DMA((7 * n,))],
        compiler_params=_params(vmem=VMEM_LIMIT),
    )(*gathered_parts, *cast_parts)


class _DirectReduceScatter:
    def __init__(self, stacks, landing, send_sems, recv_sems):
        self.stacks, self.landing, self.send_sems, self.recv_sems = stacks, landing, send_sems, recv_sems
        self.place = _place()

    def copies(self):
        x, y, c = self.place
        cps = []
        for a in range(len(self.stacks)):
            for k in range(1, N_DEV):
                peer = (1 - x if k & 4 else x, 1 - y if k & 2 else y, 1 - c if k & 1 else c)
                cps.append(pltpu.make_async_remote_copy(
                    src_ref=self.stacks[a].at[4 * peer[0] + 2 * peer[1] + peer[2]], dst_ref=self.landing[a].at[k - 1],
                    send_sem=self.send_sems.at[7 * a + k - 1], recv_sem=self.recv_sems.at[7 * a + k - 1],
                    device_id=peer, device_id_type=MESH))
        return cps

    def start(self):
        for cp in self.copies():
            cp.start()

    def finish(self):
        for cp in self.copies():
            cp.wait_recv()
        for cp in self.copies():
            cp.wait_send()


def _add_received(name, device, stack, received):
    def body(dev_ref, own_ref, recv_ref, out_ref):
        del dev_ref
        total = own_ref[...].astype(F32)
        for k in range(N_DEV - 1):
            total = total + recv_ref[k].astype(F32)
        out_ref[...] = total

    rows = stack.shape[1]
    return pl.pallas_call(
        body, name=name,
        grid_spec=pltpu.PrefetchScalarGridSpec(
            num_scalar_prefetch=1, grid=(1,),
            in_specs=[pl.BlockSpec((None, rows, D_MODEL), lambda i, dref: (dref[0], 0, 0)),
                      pl.BlockSpec(received.shape, lambda i, dref: (0, 0, 0))],
            out_specs=pl.BlockSpec((rows, D_MODEL), lambda i, dref: (0, 0))),
        out_shape=jax.ShapeDtypeStruct((rows, D_MODEL), F32),
        compiler_params=_params(("arbitrary",), vmem=VMEM_LIMIT),
    )(device, stack, received)


def _all_reduce_small(v):
    def body(v_ref, out_ref, gathered, send_sems, recv_sems):
        x, y, c = _place()
        mine = 4 * x + 2 * y + c
        gathered[mine] = v_ref[...]

        def copy(k):
            peer = (1 - x if k & 4 else x, 1 - y if k & 2 else y, 1 - c if k & 1 else c)
            return pltpu.make_async_remote_copy(
                src_ref=gathered.at[mine], dst_ref=gathered.at[mine],
                send_sem=send_sems.at[k - 1], recv_sem=recv_sems.at[k - 1], device_id=peer, device_id_type=MESH)

        copies = [copy(k) for k in range(1, N_DEV)]
        for cp in copies:
            cp.start()
        for cp in copies:
            cp.wait_recv()
        for cp in copies:
            cp.wait_send()
        total = gathered[0]
        for d in range(1, N_DEV):
            total = total + gathered[d]
        out_ref[...] = total

    vm = pl.BlockSpec(memory_space=pltpu.VMEM)
    return pl.pallas_call(
        body, name="small_all_reduce",
        out_shape=jax.ShapeDtypeStruct(v.shape, F32),
        in_specs=[vm], out_specs=vm,
        scratch_shapes=[pltpu.VMEM((N_DEV,) + v.shape, F32),
                        pltpu.SemaphoreType.DMA((N_DEV - 1,)), pltpu.SemaphoreType.DMA((N_DEV - 1,))],
    )(v)


def _reduce_scatter_cores(grads):
    n = len(grads)

    def body(*refs):
        ins, outs = refs[:n], refs[n:2 * n]
        send_sems, recv_sems = refs[2 * n], refs[2 * n + 1]
        x, y, c = _place()
        copies = []
        for a in range(n):
            for k in range(4):
                copies.append(pltpu.make_async_remote_copy(
                    src_ref=ins[a].at[k, 1 - c], dst_ref=outs[a].at[k],
                    send_sem=send_sems.at[4 * a + k], recv_sem=recv_sems.at[4 * a + k],
                    device_id=(x, y, 1 - c), device_id_type=MESH))
        for cp in copies:
            cp.start()
        for cp in copies:
            cp.wait_recv()
        for cp in copies:
            cp.wait_send()

    hbm = pl.BlockSpec(memory_space=pl.ANY)
    return pl.pallas_call(
        body, name="grad_reduce_scatter_cores",
        out_shape=[jax.ShapeDtypeStruct((4,) + g.shape[2:], g.dtype) for g in grads],
        in_specs=[hbm] * n, out_specs=[hbm] * n,
        scratch_shapes=[pltpu.SemaphoreType.DMA((4 * n,)), pltpu.SemaphoreType.DMA((4 * n,))],
    )(*grads)


def _reduce_scatter_chips(partials):
    n = len(partials)

    def body(*refs):
        ins, outs = refs[:n], refs[n:2 * n]
        send_sems, recv_sems = refs[2 * n], refs[2 * n + 1]
        x, y, c = _place()
        copies = []
        for a in range(n):
            for j in (1, 2, 3):
                px = 1 - x if j & 2 else x
                py = 1 - y if j & 1 else y
                copies.append(pltpu.make_async_remote_copy(
                    src_ref=ins[a].at[2 * px + py], dst_ref=outs[a].at[j - 1],
                    send_sem=send_sems.at[3 * a + j - 1], recv_sem=recv_sems.at[3 * a + j - 1],
                    device_id=(px, py, c), device_id_type=MESH))
        for cp in copies:
            cp.start()
        for cp in copies:
            cp.wait_recv()
        for cp in copies:
            cp.wait_send()

    hbm = pl.BlockSpec(memory_space=pl.ANY)
    return pl.pallas_call(
        body, name="grad_reduce_scatter_chips",
        out_shape=[jax.ShapeDtypeStruct((3,) + p.shape[1:], p.dtype) for p in partials],
        in_specs=[hbm] * n, out_specs=[hbm] * n,
        scratch_shapes=[pltpu.SemaphoreType.DMA((3 * n,)), pltpu.SemaphoreType.DMA((3 * n,))],
    )(*partials)


def _add_core_partials(core, grads, received):
    n = len(grads)

    def body(core_ref, *refs):
        del core_ref
        gs, rs, outs = refs[:n], refs[n:2 * n], refs[2 * n:]
        for a in range(n):
            outs[a][...] = (gs[a][...].astype(F32) + rs[a][...].astype(F32)).astype(outs[a].dtype)

    in_specs = [pl.BlockSpec((None, None) + g.shape[2:], lambda k, cref: (k, cref[0], 0, 0)) for g in grads]
    in_specs += [pl.BlockSpec((None,) + r.shape[1:], lambda k, cref: (k, 0, 0)) for r in received]
    out_specs = [pl.BlockSpec((None,) + r.shape[1:], lambda k, cref: (k, 0, 0)) for r in received]
    return pl.pallas_call(
        body, name="grad_add_core_partials",
        grid_spec=pltpu.PrefetchScalarGridSpec(num_scalar_prefetch=1, grid=(4,), in_specs=in_specs, out_specs=out_specs),
        out_shape=[jax.ShapeDtypeStruct(r.shape, r.dtype) for r in received],
        compiler_params=_params(("arbitrary",)),
    )(core, *grads, *received)


def _add_chip_partials(chip, partials, received):
    n = len(partials)

    def body(chip_ref, *refs):
        del chip_ref
        ps, rs, outs = refs[:n], refs[n:2 * n], refs[2 * n:]
        for a in range(n):
            outs[a][...] = ((ps[a][...].astype(F32) + rs[a][0].astype(F32)) + rs[a][1].astype(F32)) + rs[a][2].astype(F32)

    in_specs = [pl.BlockSpec((None,) + p.shape[1:], lambda i, cref: (cref[0], 0, 0)) for p in partials]
    in_specs += [pl.BlockSpec(r.shape, lambda i, cref: (0, 0, 0)) for r in received]
    out_specs = [pl.BlockSpec(p.shape[1:], lambda i, cref: (0, 0)) for p in partials]
    return pl.pallas_call(
        body, name="grad_add_chip_partials",
        grid_spec=pltpu.PrefetchScalarGridSpec(num_scalar_prefetch=1, grid=(1,), in_specs=in_specs, out_specs=out_specs),
        out_shape=[jax.ShapeDtypeStruct(p.shape[1:], F32) for p in partials],
        compiler_params=_params(("arbitrary",), vmem=VMEM_LIMIT),
    )(chip, *partials, *received)


def _in_projection(x, g_mix, w_in_t, tm):
    s = x.shape[0]

    def body(x_ref, g_ref, w_ref, h_ref, z_ref):
        xh, _ = _rms(x_ref[...])
        h = (xh * g_ref[...]).astype(BF16)
        h_ref[...] = h
        for n0 in range(0, D_IN, 512):
            z_ref[:, n0:n0 + 512] = _dot(h, w_ref[n0:n0 + 512, :], NT).astype(BF16)

    return pl.pallas_call(
        body, name="in_projection", grid=(s // tm,),
        in_specs=[pl.BlockSpec((tm, D_MODEL), lambda i: (i, 0)), _const_spec((1, D_MODEL)), _const_spec((D_IN, D_MODEL))],
        out_specs=[pl.BlockSpec((tm, D_MODEL), lambda i: (i, 0)), pl.BlockSpec((tm, D_IN), lambda i: (i, 0))],
        out_shape=[jax.ShapeDtypeStruct((s, D_MODEL), BF16), jax.ShapeDtypeStruct((s, D_IN), BF16)],
        compiler_params=_params(("parallel",), vmem=VMEM_LIMIT),
    )(x, g_mix, w_in_t)


RESIDUES = 16
CHUNK = 512
GROUP = 4


def _branch_geometry(d, seg):
    n_runs = RESIDUES // d
    w = BLOCK // n_runs
    blocks = seg // w
    group = min(GROUP, blocks)
    return n_runs, w, blocks // group, group


def _permuted_masks():
    out = []
    for d in DILATIONS:
        n_runs = RESIDUES // d
        w = BLOCK // n_runs
        p = np.arange(BLOCK)
        pos = (p % w) * n_runs + p // w
        dist = pos[:, None] - np.concatenate([pos - BLOCK, pos])[None, :]
        band = (dist >= 0) & (dist <= BLOCK)
        first = band & (np.arange(2 * BLOCK)[None, :] >= BLOCK)
        both = [np.where(m, 0.0, NEG).astype(np.float32) for m in (band, first)]
        out.append(np.stack([np.concatenate([m, m], axis=0) for m in both]))
    return jnp.asarray(np.stack(out))


def _run_starts(u, grp, d, seg, n_runs, w, group):
    return [(u + d * q) * seg + (w * group) * grp for q in range(n_runs)]


def _tile_rows(start, w, b):
    off = start + w * b
    if b < 0:
        off = jnp.maximum(off, 0)
    return pl.ds(pl.multiple_of(off, w), w)


def _load_tile(ref, starts, w, b):
    parts = [ref[_tile_rows(st, w, b), :] for st in starts]
    return parts[0] if len(parts) == 1 else jnp.concatenate(parts, axis=0)


def _store_tile(ref, starts, w, b, tile):
    for i, st in enumerate(starts):
        ref[_tile_rows(st, w, b), :] = tile[i * w:(i + 1) * w]


def _add_tile(ref, starts, w, b, tile):
    for i, st in enumerate(starts):
        ref[_tile_rows(st, w, b), :] += tile[i * w:(i + 1) * w]


def _to_residue_major(dst, src, c, seg):
    per = CHUNK // RESIDUES
    for r in range(RESIDUES):
        dst[pl.ds(pl.multiple_of(r * seg + c * per, per), per), :] = src[pl.ds(r, per, stride=RESIDUES), :]


def _from_residue_major(dst, dst_start, src, c, seg):
    per = CHUNK // RESIDUES
    for r in range(RESIDUES):
        dst[pl.ds(dst_start + r, per, stride=RESIDUES), :] = src[pl.ds(pl.multiple_of(r * seg + c * per, per), per), :]


def _stack_heads(t, head0):
    return jnp.concatenate([jnp.where(head0, t, 0.0), jnp.where(head0, 0.0, t)], axis=0).astype(BF16)


def _attention_forward(z, gq2, gk2, masks, shards):
    s = z.shape[0]
    seg = s // RESIDUES
    n_w = len(shards)

    def body(zq_ref, zk_ref, zv_ref, gq_ref, gk_ref, mask_ref, *rest):
        shard_refs, (o_ref, lse_ref), stacks = rest[:n_w], rest[n_w:n_w + 2], rest[n_w + 2:2 * n_w + 2]
        qf, kf, vf, o_st, l_st, tq, tk, tv, send_sems, recv_sems, local_sems = rest[2 * n_w + 2:]
        pair = pl.program_id(0)
        gather = _TwoLevelGather(stacks, send_sems, recv_sems)

        def own_copies():
            mine = gather.index(gather.me)
            return [pltpu.make_async_copy(shard_refs[a], stacks[a].at[mine], local_sems.at[a]) for a in range(n_w)]

        @pl.when(pair == 0)
        def _():
            for cp in own_copies():
                cp.start()
            gather.start(shard_refs)

        @pl.when(pair == 2)
        def _():
            gather.forward()

        bd = _head_blockdiag(1.0 / HEAD_DIM)
        gq = gq_ref[...] * (HEAD_DIM ** -0.5)
        gk = gk_ref[...]

        def prep(c, carry):
            rows = pl.ds(pl.multiple_of(c * CHUNK, CHUNK), CHUNK)
            zq = zq_ref[rows, :].astype(F32)
            zk = zk_ref[rows, :].astype(F32)
            tq[...] = (zq * lax.rsqrt(_head_sum(zq * zq, bd) + EPS)) * gq
            tk[...] = (zk * lax.rsqrt(_head_sum(zk * zk, bd) + EPS)) * gk
            tv[...] = zv_ref[rows, :].astype(F32)
            _to_residue_major(qf, tq, c, seg)
            _to_residue_major(kf, tk, c, seg)
            _to_residue_major(vf, tv, c, seg)
            return carry

        lax.fori_loop(0, s // CHUNK, prep, 0)

        head0 = lax.broadcasted_iota(jnp.int32, (BLOCK, 128), 1) < HEAD_DIM

        def run_branch(branch, d):
            n_runs, w, steps, group = _branch_geometry(d, seg)

            def step(u, grp):
                starts = _run_starts(u, grp, d, seg, n_runs, w, group)
                kt = [_load_tile(kf, starts, w, b).astype(BF16) for b in range(-1, group)]
                vt = [_load_tile(vf, starts, w, b).astype(BF16) for b in range(-1, group)]
                scores = []
                for b in range(group):
                    q2 = _stack_heads(_load_tile(qf, starts, w, b), head0)
                    kc = jnp.concatenate([kt[b], kt[b + 1]], axis=0)
                    mask = mask_ref[branch, 0]
                    if b == 0:
                        mask = jnp.where(grp == 0, mask_ref[branch, 1], mask)
                    scores.append(_dot(q2, kc, NT) + mask)
                sc = jnp.concatenate(scores, axis=0)
                m = jnp.max(sc, axis=-1, keepdims=True)
                e = jnp.exp(sc - m).astype(BF16)
                ones = jnp.ones((BLOCK, 128), BF16)
                fresh = []
                for b in range(group):
                    rows = slice(2 * BLOCK * b, 2 * BLOCK * (b + 1))
                    v_aug = jnp.concatenate([jnp.concatenate([vt[b], ones], axis=1),
                                             jnp.concatenate([vt[b + 1], ones], axis=1)], axis=0)
                    o2 = _dot(e[rows], v_aug, NN)
                    den = o2[:, 128:]
                    o2 = o2[:, :128] * (1.0 / den)
                    l2 = m[rows] + jnp.log(den)
                    fresh.append((jnp.where(head0, o2[:BLOCK], o2[BLOCK:]), jnp.where(head0, l2[:BLOCK], l2[BLOCK:])))
                for b in range(group):
                    o_new, l_new = fresh[b]
                    if branch > 0:
                        l_old = _load_tile(l_st, starts, w, b)
                        top = jnp.maximum(l_old, l_new)
                        e_old = jnp.exp(l_old - top)
                        e_new = jnp.exp(l_new - top)
                        tot = e_old + e_new
                        inv = 1.0 / tot
                        o_new = _load_tile(o_st, starts, w, b) * (e_old * inv) + o_new * (e_new * inv)
                        l_new = top + jnp.log(tot)
                    _store_tile(o_st, starts, w, b, o_new)
                    _store_tile(l_st, starts, w, b, l_new)

            def unit(u, carry):
                def inner(grp, carry2):
                    step(u, grp)
                    return carry2
                return lax.fori_loop(0, steps, inner, carry)

            lax.fori_loop(0, d, unit, 0)

        for branch, d in enumerate(DILATIONS):
            run_branch(branch, d)

        def finish(c, carry):
            _from_residue_major(o_ref, c * CHUNK, o_st, c, seg)
            _from_residue_major(lse_ref, c * CHUNK, l_st, c, seg)
            return carry

        lax.fori_loop(0, s // CHUNK, finish, 0)

        @pl.when(pair == N_PAIRS - 1)
        def _():
            gather.finish()
            for cp in own_copies():
                cp.wait()

    def col(c0):
        return pl.BlockSpec((s, 128), lambda p: (0, c0 + p), pipeline_mode=pl.Buffered(1))

    hbm = pl.BlockSpec(memory_space=pl.ANY)
    out = pl.pallas_call(
        body, name="attention_forward", grid=(N_PAIRS,),
        in_specs=[col(12), col(16), col(20), _const_spec((1, 128)), _const_spec((1, 128)), _const_spec(masks.shape)]
        + [hbm] * n_w,
        out_specs=[pl.BlockSpec((s, 128), lambda p: (0, p)), pl.BlockSpec((s, 128), lambda p: (0, p))] + [hbm] * n_w,
        out_shape=[jax.ShapeDtypeStruct((s, D_ATTN), F32), jax.ShapeDtypeStruct((s, D_ATTN), F32)]
        + [jax.ShapeDtypeStruct((N_DEV,) + w.shape, w.dtype) for w in shards],
        scratch_shapes=[pltpu.VMEM((s, 128), F32)] * 5 + [pltpu.VMEM((CHUNK, 128), F32)] * 3
        + [pltpu.SemaphoreType.DMA((7 * n_w,)), pltpu.SemaphoreType.DMA((7 * n_w,)), pltpu.SemaphoreType.DMA((n_w,))],
        compiler_params=_params(("arbitrary",), vmem=VMEM_LIMIT),
    )(z, z, z, gq2, gk2, masks, *shards)
    return out[0], out[1], out[2:]


def _conv_forward(zt, zprev, w, first):
    u, gb, gc = zt[:, :D_CONV], zt[:, D_CONV:2 * D_CONV], zt[:, 2 * D_CONV:]
    cu = gc * u
    cu_prev = jnp.where(first, 0.0, zprev[:, 2 * D_CONV:] * zprev[:, :D_CONV])
    row = lax.broadcasted_iota(jnp.int32, cu.shape, 0)
    cu1 = jnp.where(row >= 1, pltpu.roll(cu, 1, 0), cu_prev[7:8, :])
    cu2 = jnp.where(row >= 2, pltpu.roll(cu, 2, 0), jnp.where(row == 1, cu_prev[7:8, :], cu_prev[6:7, :]))
    cv = w[0:1, :] * cu2 + w[1:2, :] * cu1 + w[2:3, :] * cu
    return u, gb, gc, cu, cu1, cu2, cv


def _mix_out(z, y_attn, x, conv_w, g_conv_out, g_attn_out, g_ffn, w_out, tm):
    s = x.shape[0]
    hb = tm // 8

    def body(z_ref, zp_ref, ya_ref, x_ref, cw_ref, gc_ref, ga_ref, gf_ref, w_ref, mix_ref, x2_ref, h2_ref):
        first = pl.program_id(0) == 0
        zt = z_ref[...].astype(F32)
        _, gb, _, _, _, _, cv = _conv_forward(zt, zp_ref[...].astype(F32), cw_ref[...], first)
        nc, _ = _rms(gb * cv)
        na, _ = _rms(ya_ref[...])
        mix = jnp.concatenate([nc * gc_ref[...], na * ga_ref[...]], axis=-1).astype(BF16)
        mix_ref[...] = mix
        x2 = x_ref[...] + _dot(mix, w_ref[...], NN)
        x2_ref[...] = x2
        xh, _ = _rms(x2)
        h2_ref[...] = (xh * gf_ref[...]).astype(BF16)

    tile = lambda w: pl.BlockSpec((tm, w), lambda i: (i, 0))
    return pl.pallas_call(
        body, name="mix_out", grid=(s // tm,),
        in_specs=[tile(3 * D_CONV), pl.BlockSpec((8, 3 * D_CONV), lambda i: (jnp.maximum(i * hb - 1, 0), 0)),
                  tile(D_ATTN), tile(D_MODEL), _const_spec((8, D_CONV)), _const_spec((1, D_CONV)), _const_spec((1, D_ATTN)),
                  _const_spec((1, D_MODEL)), _const_spec((D_MODEL, D_MODEL))],
        out_specs=[tile(D_MODEL)] * 3,
        out_shape=[jax.ShapeDtypeStruct((s, D_MODEL), BF16), jax.ShapeDtypeStruct((s, D_MODEL), F32),
                   jax.ShapeDtypeStruct((s, D_MODEL), BF16)],
        compiler_params=_params(("arbitrary",), vmem=VMEM_LIMIT),
    )(z, z, y_attn, x, conv_w, g_conv_out, g_attn_out, g_ffn, w_out)


FF_CHUNK = 256


def _ffn_forward(h2, x2, target, w_gate_t, w_up_t, w_down, tm):
    s = h2.shape[0]

    def body(h_ref, x2_ref, t_ref, wg_ref, wu_ref, wd_ref, a_ref, b_ref, dy_ref, loss_ref):
        @pl.when(pl.program_id(0) == 0)
        def _():
            loss_ref[...] = jnp.zeros_like(loss_ref)

        h = h_ref[...]
        y = x2_ref[...]
        for c0 in range(0, D_FF, FF_CHUNK):
            rows = slice(c0, c0 + FF_CHUNK)
            a = _dot(h, wg_ref[rows, :], NT)
            b = _dot(h, wu_ref[rows, :], NT)
            a_ref[:, rows] = a.astype(BF16)
            b_ref[:, rows] = b.astype(BF16)
            f = (a * jax.nn.sigmoid(a)) * b
            y = y + _dot(f.astype(BF16), wd_ref[rows, :], NN)
        err = y - t_ref[...]
        dy_ref[...] = err * (1.0 / D_MODEL)
        loss_ref[...] += jnp.sum(err * err)

    tile = lambda w: pl.BlockSpec((tm, w), lambda i: (i, 0))
    return pl.pallas_call(
        body, name="ffn_forward", grid=(s // tm,),
        in_specs=[tile(D_MODEL), tile(D_MODEL), tile(D_MODEL)] + [_const_spec((D_FF, D_MODEL))] * 3,
        out_specs=[tile(D_FF), tile(D_FF), tile(D_MODEL), pl.BlockSpec((8, 128), lambda i: (0, 0))],
        out_shape=[jax.ShapeDtypeStruct((s, D_FF), BF16), jax.ShapeDtypeStruct((s, D_FF), BF16),
                   jax.ShapeDtypeStruct((s, D_MODEL), F32), jax.ShapeDtypeStruct((8, 128), F32)],
        compiler_params=_params(("arbitrary",), vmem=VMEM_LIMIT),
    )(h2, x2, target, w_gate_t, w_up_t, w_down)


def _ffn_backward(dy, a, b, x2, g_ffn, w_gate_t, w_up_t, w_down, tm):
    s = dy.shape[0]
    half = D_FF // 2
    pieces = [(p0, min(FF_CHUNK, half - p0)) for p0 in range(0, half, FF_CHUNK)]

    def body(dy_ref, a_ref, b_ref, x2_ref, g_ref, wg_ref, wu_ref, wd_ref, da_ref, db_ref, dx2_ref, dg_ref, acc):
        i, c = pl.program_id(0), pl.program_id(1)

        @pl.when((i == 0) & (c == 0))
        def _():
            dg_ref[...] = jnp.zeros_like(dg_ref)

        dyb = dy_ref[...].astype(BF16)
        dh = jnp.zeros((tm, D_MODEL), F32)
        for p0, width in pieces:
            rows = slice(p0, p0 + width)
            df = _dot(dyb, wd_ref[rows, :], NT)
            av = a_ref[:, rows].astype(F32)
            bv = b_ref[:, rows].astype(F32)
            sig = jax.nn.sigmoid(av)
            silu = av * sig
            da = (df * bv * (sig * (1.0 + av * (1.0 - sig)))).astype(BF16)
            db = (df * silu).astype(BF16)
            da_ref[:, rows] = da
            db_ref[:, rows] = db
            dh = dh + _dot(da, wg_ref[rows, :], NN) + _dot(db, wu_ref[rows, :], NN)

        @pl.when(c == 0)
        def _():
            acc[...] = dh

        @pl.when(c == 1)
        def _():
            dh_all = acc[...] + dh
            xh, r = _rms(x2_ref[...])
            dg_ref[...] += jnp.sum(dh_all * xh, axis=0, keepdims=True)
            dx2_ref[...] = dy_ref[...] + _rms_bwd(dh_all, xh, r, g_ref[...])

    tile = lambda w: pl.BlockSpec((tm, w), lambda i, c: (i, 0))
    part = pl.BlockSpec((tm, half), lambda i, c: (i, c))
    weight = pl.BlockSpec((half, D_MODEL), lambda i, c: (c, 0))
    return pl.pallas_call(
        body, name="ffn_backward", grid=(s // tm, 2),
        in_specs=[tile(D_MODEL), part, part, tile(D_MODEL), _const_spec((1, D_MODEL)), weight, weight, weight],
        out_specs=[part, part, tile(D_MODEL), pl.BlockSpec((1, D_MODEL), lambda i, c: (0, 0))],
        out_shape=[jax.ShapeDtypeStruct((s, D_FF), BF16)] * 2
        + [jax.ShapeDtypeStruct((s, D_MODEL), F32), jax.ShapeDtypeStruct((1, D_MODEL), F32)],
        scratch_shapes=[pltpu.VMEM((tm, D_MODEL), F32)],
        compiler_params=_params(("arbitrary", "arbitrary"), vmem=VMEM_LIMIT),
    )(dy, a, b, x2, g_ffn, w_gate_t, w_up_t, w_down)


def _out_backward(dx2, z, y_attn, conv_w, g_conv_out, g_attn_out, w_out, tm):
    s = dx2.shape[0]
    hb = tm // 8

    def body(dx2_ref, z_ref, zp_ref, ya_ref, cw_ref, gc_ref, ga_ref, w_ref, dya_ref, dgb_ref, dcv_ref, dgc_ref, dga_ref):
        first = pl.program_id(0) == 0

        @pl.when(first)
        def _():
            dgc_ref[...] = jnp.zeros_like(dgc_ref)
            dga_ref[...] = jnp.zeros_like(dga_ref)

        dmix = _dot(dx2_ref[...].astype(BF16), w_ref[...], NT)
        _, gb, _, _, _, _, cv = _conv_forward(z_ref[...].astype(F32), zp_ref[...].astype(F32), cw_ref[...], first)
        ych, rc = _rms(gb * cv)
        dnc = dmix[:, :D_CONV]
        dgc_ref[...] += jnp.sum(dnc * ych, axis=0, keepdims=True)
        dyc = _rms_bwd(dnc, ych, rc, gc_ref[...])
        dgb_ref[...] = (dyc * cv).astype(BF16)
        dcv_ref[...] = dyc * gb
        yah, ra = _rms(ya_ref[...])
        dna = dmix[:, D_CONV:]
        dga_ref[...] += jnp.sum(dna * yah, axis=0, keepdims=True)
        dya_ref[...] = _rms_bwd(dna, yah, ra, ga_ref[...])

    tile = lambda w: pl.BlockSpec((tm, w), lambda i: (i, 0))
    vec = pl.BlockSpec((1, D_CONV), lambda i: (0, 0))
    return pl.pallas_call(
        body, name="out_backward", grid=(s // tm,),
        in_specs=[tile(D_MODEL), tile(3 * D_CONV), pl.BlockSpec((8, 3 * D_CONV), lambda i: (jnp.maximum(i * hb - 1, 0), 0)),
                  tile(D_ATTN), _const_spec((8, D_CONV)), _const_spec((1, D_CONV)), _const_spec((1, D_ATTN)),
                  _const_spec((D_MODEL, D_MODEL))],
        out_specs=[tile(D_ATTN), tile(D_CONV), tile(D_CONV), vec, vec],
        out_shape=[jax.ShapeDtypeStruct((s, D_ATTN), F32), jax.ShapeDtypeStruct((s, D_CONV), BF16),
                   jax.ShapeDtypeStruct((s, D_CONV), F32), jax.ShapeDtypeStruct((1, D_CONV), F32),
                   jax.ShapeDtypeStruct((1, D_ATTN), F32)],
        compiler_params=_params(("arbitrary",), vmem=VMEM_LIMIT),
    )(dx2, z, z, y_attn, conv_w, g_conv_out, g_attn_out, w_out)


def _attention_backward(z, o, lse, dya, gq2, gk2, masks, grad_stacks):
    s = z.shape[0]
    seg = s // RESIDUES
    n_g = len(grad_stacks)

    def body(z_hbm, o_hbm, lse_hbm, do_hbm, gq_ref, gk_ref, mask_ref, *rest):
        stack_refs = rest[:n_g]
        dzq_ref, dzk_ref, dzv_ref, dgq_ref, dgk_ref = rest[n_g:n_g + 5]
        landing = rest[n_g + 5:2 * n_g + 5]
        (qf, kf, vf, dof, stats, dq, dk, dv, t0, t1, t2, zbuf, fbuf, sems, send_sems, recv_sems) = rest[2 * n_g + 5:]
        pair = pl.program_id(0)
        scatter = _DirectReduceScatter(stack_refs, landing, send_sems, recv_sems)

        @pl.when(pair == 0)
        def _():
            scatter.start()

        n_chunks = s // CHUNK
        bd_mean = _head_blockdiag(1.0 / HEAD_DIM)
        bd_sum = _head_blockdiag(1.0)
        gq = gq_ref[...] * (HEAD_DIM ** -0.5)
        gk = gk_ref[...]
        lane = lax.broadcasted_iota(jnp.int32, (CHUNK, 128), 1) % HEAD_DIM

        def chunk_copies(c, slot, n_z, n_f):
            rows = pl.ds(pl.multiple_of(c * CHUNK, CHUNK), CHUNK)
            cps = []
            for i, col in enumerate((12, 16, 20)[:n_z]):
                cols = pl.ds(pl.multiple_of((col + pair) * 128, 128), 128)
                cps.append(pltpu.make_async_copy(z_hbm.at[rows, cols], zbuf.at[slot, i], sems.at[slot, i]))
            for i, src in enumerate((do_hbm, o_hbm, lse_hbm)[:n_f]):
                cols = pl.ds(pl.multiple_of(pair * 128, 128), 128)
                cps.append(pltpu.make_async_copy(src.at[rows, cols], fbuf.at[slot, i], sems.at[slot, 3 + i]))
            return cps

        def prefetched(c, n_z, n_f):
            slot = c % 2

            @pl.when(c + 1 < n_chunks)
            def _():
                for cp in chunk_copies(c + 1, 1 - slot, n_z, n_f):
                    cp.start()

            for cp in chunk_copies(c, slot, n_z, n_f):
                cp.wait()
            return slot

        for cp in chunk_copies(0, 0, 3, 3):
            cp.start()

        def prep(c, carry):
            slot = prefetched(c, 3, 3)
            rows = pl.ds(pl.multiple_of(c * CHUNK, CHUNK), CHUNK)
            zq = zbuf[slot, 0].astype(F32)
            t0[...] = (zq * lax.rsqrt(_head_sum(zq * zq, bd_mean) + EPS)) * gq
            zk = zbuf[slot, 1].astype(F32)
            t1[...] = (zk * lax.rsqrt(_head_sum(zk * zk, bd_mean) + EPS)) * gk
            t2[...] = zbuf[slot, 2].astype(F32)
            _to_residue_major(qf, t0, c, seg)
            _to_residue_major(kf, t1, c, seg)
            _to_residue_major(vf, t2, c, seg)
            _to_residue_major(dof, fbuf.at[slot, 0], c, seg)
            delta = _head_sum(fbuf[slot, 0] * fbuf[slot, 1], bd_sum)
            t0[...] = jnp.where(lane == 0, fbuf[slot, 2], jnp.where(lane == 1, delta, 0.0))
            _to_residue_major(stats, t0, c, seg)
            zero = jnp.zeros((CHUNK, 128), F32)
            dq[rows, :] = zero
            dk[rows, :] = zero
            dv[rows, :] = zero
            return carry

        lax.fori_loop(0, n_chunks, prep, 0)

        head0 = lax.broadcasted_iota(jnp.int32, (BLOCK, 128), 1) < HEAD_DIM

        def run_branch(branch, d):
            n_runs, w, steps, group = _branch_geometry(d, seg)

            def step(u, grp):
                starts = _run_starts(u, grp, d, seg, n_runs, w, group)
                kt = [_load_tile(kf, starts, w, b).astype(BF16) for b in range(-1, group)]
                vt = [_load_tile(vf, starts, w, b).astype(BF16) for b in range(-1, group)]
                dk_t = [jnp.zeros((BLOCK, 128), F32)] * (group + 1)
                dv_t = [jnp.zeros((BLOCK, 128), F32)] * (group + 1)
                dq_t = []
                q2s, do2s, kcs, scores, dps, lses, deltas = [], [], [], [], [], [], []
                for b in range(group):
                    q2 = _stack_heads(_load_tile(qf, starts, w, b), head0)
                    do2 = _stack_heads(_load_tile(dof, starts, w, b), head0)
                    st = _load_tile(stats, starts, w, b)
                    lses += [st[:, 0:1], st[:, HEAD_DIM:HEAD_DIM + 1]]
                    deltas += [st[:, 1:2], st[:, HEAD_DIM + 1:HEAD_DIM + 2]]
                    kc = jnp.concatenate([kt[b], kt[b + 1]], axis=0)
                    vc = jnp.concatenate([vt[b], vt[b + 1]], axis=0)
                    mask = mask_ref[branch, 0]
                    if b == 0:
                        mask = jnp.where(grp == 0, mask_ref[branch, 1], mask)
                    scores.append(_dot(q2, kc, NT) + mask)
                    dps.append(_dot(do2, vc, NT))
                    q2s.append(q2)
                    do2s.append(do2)
                    kcs.append(kc)
                p_all = jnp.exp(jnp.concatenate(scores, axis=0) - jnp.concatenate(lses, axis=0))
                ds_all = (p_all * (jnp.concatenate(dps, axis=0) - jnp.concatenate(deltas, axis=0))).astype(BF16)
                p_all = p_all.astype(BF16)
                for b in range(group):
                    rows = slice(2 * BLOCK * b, 2 * BLOCK * (b + 1))
                    ds, q2, do2, kc = ds_all[rows], q2s[b], do2s[b], kcs[b]
                    dq2 = _dot(ds, kc, NN)
                    dq_t.append(jnp.where(head0, dq2[:BLOCK], dq2[BLOCK:]))
                    dkc = _dot(ds, q2, TN)
                    dvc = _dot(p_all[rows], do2, TN)
                    dk_t[b] = dk_t[b] + dkc[:BLOCK]
                    dk_t[b + 1] = dk_t[b + 1] + dkc[BLOCK:]
                    dv_t[b] = dv_t[b] + dvc[:BLOCK]
                    dv_t[b + 1] = dv_t[b + 1] + dvc[BLOCK:]
                for b in range(group):
                    _add_tile(dq, starts, w, b, dq_t[b])
                for b in range(-1, group):
                    _add_tile(dk, starts, w, b, dk_t[b + 1])
                    _add_tile(dv, starts, w, b, dv_t[b + 1])

            def unit(u, carry):
                def inner(grp, carry2):
                    step(u, grp)
                    return carry2
                return lax.fori_loop(0, steps, inner, carry)

            lax.fori_loop(0, d, unit, 0)

        for branch, d in enumerate(DILATIONS):
            run_branch(branch, d)

        for cp in chunk_copies(0, 0, 2, 0):
            cp.start()

        def finish(c, carry):
            acc_q, acc_k = carry
            rows = pl.ds(pl.multiple_of(c * CHUNK, CHUNK), CHUNK)
            _from_residue_major(t0, 0, dq, c, seg)
            _from_residue_major(t1, 0, dk, c, seg)
            _from_residue_major(t2, 0, dv, c, seg)
            slot = prefetched(c, 2, 0)
            zq = zbuf[slot, 0].astype(F32)
            rq = lax.rsqrt(_head_sum(zq * zq, bd_mean) + EPS)
            qh = zq * rq
            dqn = t0[...]
            acc_q = acc_q + jnp.sum(dqn * qh, axis=0, keepdims=True)
            t = dqn * gq
            dzq_ref[rows, :] = (rq * (t - qh * _head_sum(t * qh, bd_mean))).astype(BF16)
            zk = zbuf[slot, 1].astype(F32)
            rk = lax.rsqrt(_head_sum(zk * zk, bd_mean) + EPS)
            kh = zk * rk
            dkn = t1[...]
            acc_k = acc_k + jnp.sum(dkn * kh, axis=0, keepdims=True)
            t = dkn * gk
            dzk_ref[rows, :] = (rk * (t - kh * _head_sum(t * kh, bd_mean))).astype(BF16)
            dzv_ref[rows, :] = t2[...].astype(BF16)
            return acc_q, acc_k

        zero = jnp.zeros((1, 128), F32)
        acc_q, acc_k = lax.fori_loop(0, s // CHUNK, finish, (zero, zero))
        dgq_ref[...] = acc_q * (HEAD_DIM ** -0.5)
        dgk_ref[...] = acc_k

        @pl.when(pair == N_PAIRS - 1)
        def _():
            scatter.finish()

    hbm = pl.BlockSpec(memory_space=pl.ANY)
    gain = pl.BlockSpec((None, 1, 128), lambda p: (p, 0, 0))
    dz_spec = pl.BlockSpec((s, 128), lambda p: (0, p))
    out = pl.pallas_call(
        body, name="attention_backward", grid=(N_PAIRS,),
        in_specs=[hbm, hbm, hbm, hbm, _const_spec((1, 128)), _const_spec((1, 128)), _const_spec(masks.shape)] + [hbm] * n_g,
        out_specs=[dz_spec, dz_spec, dz_spec, gain, gain] + [hbm] * n_g,
        out_shape=[jax.ShapeDtypeStruct((s, D_ATTN), BF16)] * 3 + [jax.ShapeDtypeStruct((N_PAIRS, 1, 128), F32)] * 2
        + [jax.ShapeDtypeStruct((N_DEV - 1,) + g.shape[1:], g.dtype) for g in grad_stacks],
        scratch_shapes=[pltpu.VMEM((s, 128), F32)] * 8 + [pltpu.VMEM((CHUNK, 128), F32)] * 3
        + [pltpu.VMEM((2, 3, CHUNK, 128), BF16), pltpu.VMEM((2, 3, CHUNK, 128), F32), pltpu.SemaphoreType.DMA((2, 6)),
           pltpu.SemaphoreType.DMA((7 * n_g,)), pltpu.SemaphoreType.DMA((7 * n_g,))],
        compiler_params=_params(("arbitrary",), vmem=VMEM_LIMIT),
    )(z, o, lse, dya, gq2, gk2, masks, *grad_stacks)
    return out[:5], out[5:]


def _in_backward(dcv, dgb, dzq, dzk, dzv, z, x, dx2, g_mix, conv_w, w_in_t, tm):
    s = x.shape[0]
    hb = tm // 8
    last_halo = s // 8 - 1

    def body(dcv_ref, dcn_ref, dgb_ref, dzq_ref, dzk_ref, dzv_ref, z_ref, zp_ref, x_ref, dx2_ref, g_ref, cw_ref, w_ref,
             dz_ref, dx_ref, dg_ref, dcw_ref):
        i = pl.program_id(0)
        first = i == 0
        last = i == pl.num_programs(0) - 1

        @pl.when(first)
        def _():
            dg_ref[...] = jnp.zeros_like(dg_ref)
            dcw_ref[...] = jnp.zeros_like(dcw_ref)

        w = cw_ref[...]
        u, _, gc, cu, cu1, cu2, _ = _conv_forward(z_ref[...].astype(F32), zp_ref[...].astype(F32), w, first)
        dcv_t = dcv_ref[...]
        nxt = jnp.where(last, 0.0, dcn_ref[...])
        row = lax.broadcasted_iota(jnp.int32, dcv_t.shape, 0)
        up1 = jnp.where(row < tm - 1, pltpu.roll(dcv_t, tm - 1, 0), nxt[0:1, :])
        up2 = jnp.where(row < tm - 2, pltpu.roll(dcv_t, tm - 2, 0), jnp.where(row == tm - 2, nxt[0:1, :], nxt[1:2, :]))
        dcu = w[2:3, :] * dcv_t + w[1:2, :] * up1 + w[0:1, :] * up2
        dcw = jnp.concatenate([jnp.sum(dcv_t * cu2, axis=0, keepdims=True), jnp.sum(dcv_t * cu1, axis=0, keepdims=True),
                               jnp.sum(dcv_t * cu, axis=0, keepdims=True), jnp.zeros((5, D_CONV), F32)], axis=0)
        dcw_ref[...] += dcw
        dz_ref[:, :D_CONV] = (dcu * gc).astype(BF16)
        dz_ref[:, D_CONV:2 * D_CONV] = dgb_ref[...]
        dz_ref[:, 2 * D_CONV:3 * D_CONV] = (dcu * u).astype(BF16)
        dz_ref[:, 3 * D_CONV:3 * D_CONV + D_ATTN] = dzq_ref[...]
        dz_ref[:, 3 * D_CONV + D_ATTN:3 * D_CONV + 2 * D_ATTN] = dzk_ref[...]
        dz_ref[:, 3 * D_CONV + 2 * D_ATTN:] = dzv_ref[...]
        dh = _dot(dz_ref[...], w_ref[...], NN)
        xh, r = _rms(x_ref[...])
        dg_ref[...] += jnp.sum(dh * xh, axis=0, keepdims=True)
        dx_ref[...] = dx2_ref[...] + _rms_bwd(dh, xh, r, g_ref[...])

    tile = lambda w: pl.BlockSpec((tm, w), lambda i: (i, 0))
    return pl.pallas_call(
        body, name="in_backward", grid=(s // tm,),
        in_specs=[tile(D_CONV), pl.BlockSpec((8, D_CONV), lambda i: (jnp.minimum((i + 1) * hb, last_halo), 0)),
                  tile(D_CONV), tile(D_ATTN), tile(D_ATTN), tile(D_ATTN),
                  tile(3 * D_CONV), pl.BlockSpec((8, 3 * D_CONV), lambda i: (jnp.maximum(i * hb - 1, 0), 0)),
                  tile(D_MODEL), tile(D_MODEL), _const_spec((1, D_MODEL)), _const_spec((8, D_CONV)),
                  _const_spec((D_IN, D_MODEL))],
        out_specs=[tile(D_IN), tile(D_MODEL), pl.BlockSpec((1, D_MODEL), lambda i: (0, 0)),
                   pl.BlockSpec((8, D_CONV), lambda i: (0, 0))],
        out_shape=[jax.ShapeDtypeStruct((s, D_IN), BF16), jax.ShapeDtypeStruct((s, D_MODEL), F32),
                   jax.ShapeDtypeStruct((1, D_MODEL), F32), jax.ShapeDtypeStruct((8, D_CONV), F32)],
        compiler_params=_params(("arbitrary",), vmem=VMEM_LIMIT),
    )(dcv, dcv, dgb, dzq, dzk, dzv, z, z, x, dx2, g_mix, conv_w, w_in_t)


def _weight_grad(name, a, b, tn, tk, gate=None):
    s, n = a.shape
    steps = s // tk

    def body(*refs):
        gate_ref = refs[0] if gate is not None else None
        a_ref, b_ref, out_ref, acc = refs[1:] if gate is not None else refs
        k = pl.program_id(1)

        @pl.when(k == 0)
        def _():
            acc[...] = jnp.zeros_like(acc)

        lhs = a_ref[...]
        if gate is not None:
            gv = gate_ref[...].astype(F32)
            lhs = ((gv * jax.nn.sigmoid(gv)) * lhs.astype(F32)).astype(BF16)
        acc[...] += _dot(lhs, b_ref[...].astype(BF16), TN)

        @pl.when(k == steps - 1)
        def _():
            out_ref[...] = acc[...].astype(BF16)

    lhs_spec = pl.BlockSpec((tk, tn), lambda j, k: (k, j))
    return pl.pallas_call(
        body, name=name, grid=(n // tn, steps),
        in_specs=([lhs_spec] if gate is not None else []) + [lhs_spec, pl.BlockSpec((tk, D_MODEL), lambda j, k: (k, 0))],
        out_specs=pl.BlockSpec((tn, D_MODEL), lambda j, k: (j, 0)),
        out_shape=jax.ShapeDtypeStruct((n, D_MODEL), BF16),
        scratch_shapes=[pltpu.VMEM((tn, D_MODEL), F32)],
        compiler_params=_params(("parallel", "arbitrary"), vmem=VMEM_LIMIT),
    )(*([gate] if gate is not None else []), a, b)


def _adamw(name, w, g, m, v):
    def body(w_ref, g_ref, m_ref, v_ref, d_ref, nm_ref, nv_ref):
        gv = g_ref[...]
        nm = ADAM_B1 * m_ref[...] + (1.0 - ADAM_B1) * gv
        nv = ADAM_B2 * v_ref[...] + (1.0 - ADAM_B2) * (gv * gv)
        m_hat = nm / (1.0 - ADAM_B1 ** ADAM_STEP)
        v_hat = nv / (1.0 - ADAM_B2 ** ADAM_STEP)
        d_ref[...] = -ADAM_LR * (m_hat / (jnp.sqrt(v_hat) + ADAM_EPS) + ADAM_WD * w_ref[...])
        nm_ref[...] = nm
        nv_ref[...] = nv

    rows = w.shape[0]
    tr = 256 if rows % 256 == 0 else rows
    spec = pl.BlockSpec((tr, w.shape[1]), lambda i: (i, 0))
    return pl.pallas_call(
        body, name=name, grid=(rows // tr,),
        in_specs=[spec] * 4, out_specs=[spec] * 3,
        out_shape=[jax.ShapeDtypeStruct(w.shape, F32)] * 3,
        compiler_params=_params(("parallel",)),
    )(w, g, m, v)


def kernel(x, g_mix, w_in, conv_w, g_q, g_k, g_conv_out, g_attn_out, w_out, g_ffn, w_gate, w_up, w_down, loss_target, m_g_mix, m_w_in, m_conv_w, m_g_q, m_g_k, m_g_conv_out, m_g_attn_out, m_w_out, m_g_ffn, m_w_gate, m_w_up, m_w_down, v_g_mix, v_w_in, v_conv_w, v_g_q, v_g_k, v_g_conv_out, v_g_attn_out, v_w_out, v_g_ffn, v_w_gate, v_w_up, v_w_down):
    s = x.shape[1]
    tm = min(512, s)
    xs, target = x[0], loss_target[0]
    px, py, pc = lax.axis_index("x"), lax.axis_index("y"), lax.axis_index("c")
    device = 4 * px + 2 * py + pc

    conv_block = jnp.zeros((8, 128), F32).at[:3, :HEAD_DIM].set(conv_w[0])
    first, shards = [w_in[0].T, conv_block], [w_out[0], w_gate[0].T, w_up[0].T, w_down[0]]
    w_in_g, conv_g, *shards = _gather_first_weights(first, [BF16, F32], shards)
    w_in_t = w_in_g.reshape(D_IN, D_MODEL)
    conv_full = jnp.transpose(conv_g[:, :3, :HEAD_DIM], (1, 0, 2)).reshape(3, D_CONV)
    conv_full = jnp.concatenate([conv_full, jnp.zeros((5, D_CONV), F32)], axis=0)
    gq2 = jnp.concatenate([g_q, g_q], axis=-1)
    gk2 = jnp.concatenate([g_k, g_k], axis=-1)

    h1, z = _in_projection(xs, g_mix, w_in_t, tm)
    masks = _permuted_masks()
    y_attn, lse, gathered = _attention_forward(z, gq2, gk2, masks, shards)
    w_out_f = gathered[0].reshape(D_MODEL, D_MODEL)
    w_gate_t, w_up_t, w_down_f = [g.reshape(D_FF, D_MODEL) for g in gathered[1:]]
    mix, x2, h2 = _mix_out(z, y_attn, xs, conv_full, g_conv_out, g_attn_out, g_ffn, w_out_f, tm)
    a, b, dy, sq_err = _ffn_forward(h2, x2, target, w_gate_t, w_up_t, w_down_f, tm)

    tk = min(512, s)
    da, db, dx2, dg_ffn = _ffn_backward(dy, a, b, x2, g_ffn, w_gate_t, w_up_t, w_down_f, tm)
    dya, dgb, dcv, dg_conv_out, dg_attn_out = _out_backward(dx2, z, y_attn, conv_full, g_conv_out, g_attn_out, w_out_f, tm)
    early = [
        _weight_grad("grad_w_out", mix, dx2, D_MODEL, tk),
        _weight_grad("grad_w_gate", da, h2, D_FF // 2, tk),
        _weight_grad("grad_w_up", db, h2, D_FF // 2, tk),
        _weight_grad("grad_w_down", b, dy, D_FF // 2, tk, gate=a),
    ]
    early = [g.reshape(N_DEV, g.shape[0] // N_DEV, D_MODEL) for g in early]
    (dzq, dzk, dzv, dgq_pairs, dgk_pairs), landed = _attention_backward(z, y_attn, lse, dya, gq2, gk2, masks, early)
    dz, grad_x, dg_mix, dconv = _in_backward(dcv, dgb, dzq, dzk, dzv, z, xs, dx2, g_mix, conv_full, w_in_t, tm)
    dev = jnp.reshape(device, (1,)).astype(jnp.int32)
    gw_out, gw_gate_t, gw_up_t, gw_down = [
        _add_received("grad_sum_" + n, dev, g, r) for n, g, r in zip(("w_out", "w_gate", "w_up", "w_down"), early, landed)]

    late = [_weight_grad("grad_w_in", dz, h1, D_IN // 2, tk).reshape(4, 2, D_IN // N_DEV, D_MODEL)]
    from_sibling = _reduce_scatter_cores(late)
    core = jnp.reshape(pc, (1,)).astype(jnp.int32)
    partials = _add_core_partials(core, late, from_sibling)
    from_chips = _reduce_scatter_chips(partials)
    chip = jnp.reshape(2 * px + py, (1,)).astype(jnp.int32)
    (gw_in_t,) = _add_chip_partials(chip, partials, from_chips)
    gw_in = gw_in_t.T

    dg_q = jnp.sum(dgq_pairs.reshape(2 * N_PAIRS, HEAD_DIM), axis=0)
    dg_k = jnp.sum(dgk_pairs.reshape(2 * N_PAIRS, HEAD_DIM), axis=0)
    zeros = lambda n: jnp.zeros((n,), F32)
    small = jnp.stack([
        dg_mix[0], dg_ffn[0],
        jnp.concatenate([dg_conv_out[0], dg_attn_out[0]]),
        jnp.concatenate([dg_q, dg_k, zeros(D_MODEL - 2 * HEAD_DIM)]),
        jnp.concatenate([dconv[0], dconv[1]]),
        jnp.concatenate([dconv[2], zeros(D_CONV)]),
        jnp.concatenate([sq_err[0, :1], zeros(D_MODEL - 1)]),
        zeros(D_MODEL),
    ])
    total = _all_reduce_small(small)
    loss = total[6, 0] * (0.5 / D_MODEL)
    gg_mix, gg_ffn = total[0:1], total[1:2]
    gg_conv_out, gg_attn_out = total[2:3, :D_CONV], total[2:3, D_CONV:]
    gg_q, gg_k = total[3:4, :HEAD_DIM], total[3:4, HEAD_DIM:2 * HEAD_DIM]
    conv_total = jnp.stack([total[4, :D_CONV], total[4, D_CONV:], total[5, :D_CONV]])
    g_conv = lax.dynamic_slice(conv_total, (0, device * HEAD_DIM), (3, HEAD_DIM))

    names = ["g_mix", "w_in", "conv_w", "g_q", "g_k", "g_conv_out", "g_attn_out", "w_out", "g_ffn", "w_gate", "w_up", "w_down"]
    weights = [g_mix, w_in[0], conv_w[0], g_q, g_k, g_conv_out, g_attn_out, w_out[0], g_ffn, w_gate[0], w_up[0], w_down[0]]
    grad_list = [gg_mix, gw_in, g_conv, gg_q, gg_k, gg_conv_out, gg_attn_out, gw_out, gg_ffn, gw_gate_t, gw_up_t, gw_down]
    m_list = [m_g_mix, m_w_in[0], m_conv_w[0], m_g_q, m_g_k, m_g_conv_out, m_g_attn_out, m_w_out[0], m_g_ffn, m_w_gate[0], m_w_up[0], m_w_down[0]]
    v_list = [v_g_mix, v_w_in[0], v_conv_w[0], v_g_q, v_g_k, v_g_conv_out, v_g_attn_out, v_w_out[0], v_g_ffn, v_w_gate[0], v_w_up[0], v_w_down[0]]
    stacked = {"w_in", "conv_w", "w_out", "w_gate", "w_up", "w_down"}
    transposed = {"w_gate", "w_up"}
    out_g, out_d, out_m, out_v = [], [], [], []
    for name, w, g, m, v in zip(names, weights, grad_list, m_list, v_list):
        if name in transposed:
            w, m, v = w.T, m.T, v.T
        delta, new_m, new_v = _adamw("adamw_" + name, w, g, m, v)
        if name in transposed:
            g, delta, new_m, new_v = g.T, delta.T, new_m.T, new_v.T
        lift = (lambda t: t[None]) if name in stacked else (lambda t: t)
        out_g.append(lift(g))
        out_d.append(lift(delta))
        out_m.append(lift(new_m))
        out_v.append(lift(new_v))
    return (loss, grad_x[None], *out_g, *out_d, *out_m, *out_v)
```

```python
import jax
import jax.numpy as jnp
import numpy as np
from jax import lax
from jax.experimental import pallas as pl
from jax.experimental.pallas import tpu as pltpu

F32 = jnp.float32
BF16 = jnp.bfloat16

D_MODEL = 1024
D_CONV = 512
D_ATTN = 512
D_IN = 3072
D_FF = 2816
HEAD_DIM = 64
N_PAIRS = 4
BLOCK = 128
DILATIONS = (1, 4, 16)
N_DEV = 8
EPS = 1e-6
NEG = -1e30

ADAM_LR = 0.001
ADAM_B1 = 0.9
ADAM_B2 = 0.999
ADAM_EPS = 1e-08
ADAM_WD = 0.01
ADAM_STEP = 10

NT = (((1,), (1,)), ((), ()))
NN = (((1,), (0,)), ((), ()))
TN = (((0,), (0,)), ((), ()))
MESH = pl.DeviceIdType.MESH

VMEM_LIMIT = 56 * 1024 * 1024


def _params(semantics=None, vmem=None):
    kw = {}
    if semantics is not None:
        kw["dimension_semantics"] = semantics
    if vmem is not None:
        kw["vmem_limit_bytes"] = vmem
    return pltpu.CompilerParams(**kw)


def _dot(a, b, dims):
    return lax.dot_general(a, b, dims, preferred_element_type=F32)


def _const_spec(shape):
    n = len(shape)
    return pl.BlockSpec(shape, lambda *_: (0,) * n, pipeline_mode=pl.Buffered(1))


def _rms(x):
    r = lax.rsqrt(jnp.mean(x * x, axis=-1, keepdims=True) + EPS)
    return x * r, r


def _rms_bwd(dy, xh, r, g):
    t = dy * g
    return r * (t - xh * jnp.mean(t * xh, axis=-1, keepdims=True))


def _head_blockdiag(scale):
    i = lax.broadcasted_iota(jnp.int32, (128, 128), 0) // HEAD_DIM
    j = lax.broadcasted_iota(jnp.int32, (128, 128), 1) // HEAD_DIM
    return jnp.where(i == j, scale, 0.0).astype(BF16)


def _head_sum(x, bd):
    hi = x.astype(BF16)
    lo = (x - hi.astype(F32)).astype(BF16)
    return _dot(hi, bd, NN) + _dot(lo, bd, NN)


def _place():
    return lax.axis_index("x"), lax.axis_index("y"), lax.axis_index("c")


class _TwoLevelGather:
    def __init__(self, stacks, send_sems, recv_sems):
        self.stacks, self.send_sems, self.recv_sems = stacks, send_sems, recv_sems
        x, y, c = _place()
        self.c = c
        self.me, self.sibling = (x, y, c), (x, y, 1 - c)
        self.chips = [(x, 1 - y), (1 - x, y), (1 - x, 1 - y)]

    @staticmethod
    def index(p):
        return 4 * p[0] + 2 * p[1] + p[2]

    def copy(self, a, k, block, to, src=None):
        dst = self.stacks[a].at[self.index(block)]
        return pltpu.make_async_remote_copy(
            src_ref=dst if src is None else src, dst_ref=dst,
            send_sem=self.send_sems.at[7 * a + k], recv_sem=self.recv_sems.at[7 * a + k],
            device_id=to, device_id_type=MESH)

    def first(self, a, src=None):
        cps = [self.copy(a, 0, self.me, self.sibling, src)]
        return cps + [self.copy(a, 1 + j, self.me, (*chip, self.c), src) for j, chip in enumerate(self.chips)]

    def forwards(self, a):
        return [self.copy(a, 4 + j, (*chip, self.c), self.sibling) for j, chip in enumerate(self.chips)]

    def start(self, srcs=None):
        for a in range(len(self.stacks)):
            for cp in self.first(a, None if srcs is None else srcs[a]):
                cp.start()

    def forward(self):
        for a in range(len(self.stacks)):
            for j, (chip, fwd) in enumerate(zip(self.chips, self.forwards(a))):
                self.copy(a, 1 + j, (*chip, self.c), self.me).wait_recv()
                fwd.start()

    def finish(self):
        for a in range(len(self.stacks)):
            self.copy(a, 0, self.sibling, self.me).wait_recv()
            for j, chip in enumerate(self.chips):
                self.copy(a, 4 + j, (*chip, 1 - self.c), self.me).wait_recv()
        for a in range(len(self.stacks)):
            for cp in self.first(a) + self.forwards(a):
                cp.wait_send()


def _gather_first_weights(gathered_parts, gathered_dtypes, cast_parts):
    n, m = len(gathered_parts), len(cast_parts)

    def body(*refs):
        ins, casts_in = refs[:n], refs[n:n + m]
        outs, casts_out = refs[n + m:2 * n + m], refs[2 * n + m:2 * n + 2 * m]
        send_sems, recv_sems = refs[2 * n + 2 * m], refs[2 * n + 2 * m + 1]
        gather = _TwoLevelGather(outs, send_sems, recv_sems)
        for a in range(n):
            outs[a][gather.index(gather.me)] = ins[a][...].astype(outs[a].dtype)
        gather.start()
        for a in range(m):
            casts_out[a][...] = casts_in[a][...].astype(BF16)
        gather.forward()
        gather.finish()

    vm = pl.BlockSpec(memory_space=pltpu.VMEM)
    return pl.pallas_call(
        body, name="weight_all_gather",
        out_shape=[jax.ShapeDtypeStruct((N_DEV,) + p.shape, dt) for p, dt in zip(gathered_parts, gathered_dtypes)]
        + [jax.ShapeDtypeStruct(p.shape, BF16) for p in cast_parts],
        in_specs=[vm] * (n + m), out_specs=[vm] * (n + m),
        scratch_shapes=[pltpu.SemaphoreType.DMA((7 * n,)), pltpu.SemaphoreType.DMA((7 * n,))],
        compiler_params=_params(vmem=VMEM_LIMIT),
    )(*gathered_parts, *cast_parts)


class _DirectReduceScatter:
    def __init__(self, stacks, landing, send_sems, recv_sems):
        self.stacks, self.landing, self.send_sems, self.recv_sems = stacks, landing, send_sems, recv_sems
        self.place = _place()

    def copies(self):
        x, y, c = self.place
        cps = []
        for a in range(len(self.stacks)):
            for k in range(1, N_DEV):
                peer = (1 - x if k & 4 else x, 1 - y if k & 2 else y, 1 - c if k & 1 else c)
                cps.append(pltpu.make_async_remote_copy(
                    src_ref=self.stacks[a].at[4 * peer[0] + 2 * peer[1] + peer[2]], dst_ref=self.landing[a].at[k - 1],
                    send_sem=self.send_sems.at[7 * a + k - 1], recv_sem=self.recv_sems.at[7 * a + k - 1],
                    device_id=peer, device_id_type=MESH))
        return cps

    def start(self):
        for cp in self.copies():
            cp.start()

    def finish(self):
        for cp in self.copies():
            cp.wait_recv()
        for cp in self.copies():
            cp.wait_send()


def _add_received(name, device, stack, received):
    def body(dev_ref, own_ref, recv_ref, out_ref):
        del dev_ref
        total = own_ref[...].astype(F32)
        for k in range(N_DEV - 1):
            total = total + recv_ref[k].astype(F32)
        out_ref[...] = total

    rows = stack.shape[1]
    return pl.pallas_call(
        body, name=name,
        grid_spec=pltpu.PrefetchScalarGridSpec(
            num_scalar_prefetch=1, grid=(1,),
            in_specs=[pl.BlockSpec((None, rows, D_MODEL), lambda i, dref: (dref[0], 0, 0)),
                      pl.BlockSpec(received.shape, lambda i, dref: (0, 0, 0))],
            out_specs=pl.BlockSpec((rows, D_MODEL), lambda i, dref: (0, 0))),
        out_shape=jax.ShapeDtypeStruct((rows, D_MODEL), F32),
        compiler_params=_params(("arbitrary",), vmem=VMEM_LIMIT),
    )(device, stack, received)


def _all_reduce_small(v):
    def body(v_ref, out_ref, gathered, send_sems, recv_sems):
        x, y, c = _place()
        mine = 4 * x + 2 * y + c
        gathered[mine] = v_ref[...]

        def copy(k):
            peer = (1 - x if k & 4 else x, 1 - y if k & 2 else y, 1 - c if k & 1 else c)
            return pltpu.make_async_remote_copy(
                src_ref=gathered.at[mine], dst_ref=gathered.at[mine],
                send_sem=send_sems.at[k - 1], recv_sem=recv_sems.at[k - 1], device_id=peer, device_id_type=MESH)

        copies = [copy(k) for k in range(1, N_DEV)]
        for cp in copies:
            cp.start()
        for cp in copies:
            cp.wait_recv()
        for cp in copies:
            cp.wait_send()
        total = gathered[0]
        for d in range(1, N_DEV):
            total = total + gathered[d]
        out_ref[...] = total

    vm = pl.BlockSpec(memory_space=pltpu.VMEM)
    return pl.pallas_call(
        body, name="small_all_reduce",
        out_shape=jax.ShapeDtypeStruct(v.shape, F32),
        in_specs=[vm], out_specs=vm,
        scratch_shapes=[pltpu.VMEM((N_DEV,) + v.shape, F32),
                        pltpu.SemaphoreType.DMA((N_DEV - 1,)), pltpu.SemaphoreType.DMA((N_DEV - 1,))],
    )(v)


def _reduce_scatter_cores(grads):
    n = len(grads)

    def body(*refs):
        ins, outs = refs[:n], refs[n:2 * n]
        send_sems, recv_sems = refs[2 * n], refs[2 * n + 1]
        x, y, c = _place()
        copies = []
        for a in range(n):
            for k in range(4):
                copies.append(pltpu.make_async_remote_copy(
                    src_ref=ins[a].at[k, 1 - c], dst_ref=outs[a].at[k],
                    send_sem=send_sems.at[4 * a + k], recv_sem=recv_sems.at[4 * a + k],
                    device_id=(x, y, 1 - c), device_id_type=MESH))
        for cp in copies:
            cp.start()
        for cp in copies:
            cp.wait_recv()
        for cp in copies:
            cp.wait_send()

    hbm = pl.BlockSpec(memory_space=pl.ANY)
    return pl.pallas_call(
        body, name="grad_reduce_scatter_cores",
        out_shape=[jax.ShapeDtypeStruct((4,) + g.shape[2:], g.dtype) for g in grads],
        in_specs=[hbm] * n, out_specs=[hbm] * n,
        scratch_shapes=[pltpu.SemaphoreType.DMA((4 * n,)), pltpu.SemaphoreType.DMA((4 * n,))],
    )(*grads)


def _reduce_scatter_chips(partials):
    n = len(partials)

    def body(*refs):
        ins, outs = refs[:n], refs[n:2 * n]
        send_sems, recv_sems = refs[2 * n], refs[2 * n + 1]
        x, y, c = _place()
        copies = []
        for a in range(n):
            for j in (1, 2, 3):
                px = 1 - x if j & 2 else x
                py = 1 - y if j & 1 else y
                copies.append(pltpu.make_async_remote_copy(
                    src_ref=ins[a].at[2 * px + py], dst_ref=outs[a].at[j - 1],
                    send_sem=send_sems.at[3 * a + j - 1], recv_sem=recv_sems.at[3 * a + j - 1],
                    device_id=(px, py, c), device_id_type=MESH))
        for cp in copies:
            cp.start()
        for cp in copies:
            cp.wait_recv()
        for cp in copies:
            cp.wait_send()

    hbm = pl.BlockSpec(memory_space=pl.ANY)
    return pl.pallas_call(
        body, name="grad_reduce_scatter_chips",
        out_shape=[jax.ShapeDtypeStruct((3,) + p.shape[1:], p.dtype) for p in partials],
        in_specs=[hbm] * n, out_specs=[hbm] * n,
        scratch_shapes=[pltpu.SemaphoreType.DMA((3 * n,)), pltpu.SemaphoreType.DMA((3 * n,))],
    )(*partials)


def _add_core_partials(core, grads, received):
    n = len(grads)

    def body(core_ref, *refs):
        del core_ref
        gs, rs, outs = refs[:n], refs[n:2 * n], refs[2 * n:]
        for a in range(n):
            outs[a][...] = (gs[a][...].astype(F32) + rs[a][...].astype(F32)).astype(outs[a].dtype)

    in_specs = [pl.BlockSpec((None, None) + g.shape[2:], lambda k, cref: (k, cref[0], 0, 0)) for g in grads]
    in_specs += [pl.BlockSpec((None,) + r.shape[1:], lambda k, cref: (k, 0, 0)) for r in received]
    out_specs = [pl.BlockSpec((None,) + r.shape[1:], lambda k, cref: (k, 0, 0)) for r in received]
    return pl.pallas_call(
        body, name="grad_add_core_partials",
        grid_spec=pltpu.PrefetchScalarGridSpec(num_scalar_prefetch=1, grid=(4,), in_specs=in_specs, out_specs=out_specs),
        out_shape=[jax.ShapeDtypeStruct(r.shape, r.dtype) for r in received],
        compiler_params=_params(("arbitrary",)),
    )(core, *grads, *received)


def _add_chip_partials(chip, partials, received):
    n = len(partials)

    def body(chip_ref, *refs):
        del chip_ref
        ps, rs, outs = refs[:n], refs[n:2 * n], refs[2 * n:]
        for a in range(n):
            outs[a][...] = ((ps[a][...].astype(F32) + rs[a][0].astype(F32)) + rs[a][1].astype(F32)) + rs[a][2].astype(F32)

    in_specs = [pl.BlockSpec((None,) + p.shape[1:], lambda i, cref: (cref[0], 0, 0)) for p in partials]
    in_specs += [pl.BlockSpec(r.shape, lambda i, cref: (0, 0, 0)) for r in received]
    out_specs = [pl.BlockSpec(p.shape[1:], lambda i, cref: (0, 0)) for p in partials]
    return pl.pallas_call(
        body, name="grad_add_chip_partials",
        grid_spec=pltpu.PrefetchScalarGridSpec(num_scalar_prefetch=1, grid=(1,), in_specs=in_specs, out_specs=out_specs),
        out_shape=[jax.ShapeDtypeStruct(p.shape[1:], F32) for p in partials],
        compiler_params=_params(("arbitrary",), vmem=VMEM_LIMIT),
    )(chip, *partials, *received)


def _in_projection(x, g_mix, w_in_t, tm):
    s = x.shape[0]

    def body(x_ref, g_ref, w_ref, h_ref, z_ref):
        xh, _ = _rms(x_ref[...])
        h = (xh * g_ref[...]).astype(BF16)
        h_ref[...] = h
        for n0 in range(0, D_IN, 512):
            z_ref[:, n0:n0 + 512] = _dot(h, w_ref[n0:n0 + 512, :], NT).astype(BF16)

    return pl.pallas_call(
        body, name="in_projection", grid=(s // tm,),
        in_specs=[pl.BlockSpec((tm, D_MODEL), lambda i: (i, 0)), _const_spec((1, D_MODEL)), _const_spec((D_IN, D_MODEL))],
        out_specs=[pl.BlockSpec((tm, D_MODEL), lambda i: (i, 0)), pl.BlockSpec((tm, D_IN), lambda i: (i, 0))],
        out_shape=[jax.ShapeDtypeStruct((s, D_MODEL), BF16), jax.ShapeDtypeStruct((s, D_IN), BF16)],
        compiler_params=_params(("parallel",), vmem=VMEM_LIMIT),
    )(x, g_mix, w_in_t)


RESIDUES = 16
CHUNK = 512
GROUP = 4


def _branch_geometry(d, seg):
    n_runs = RESIDUES // d
    w = BLOCK // n_runs
    blocks = seg // w
    group = min(GROUP, blocks)
    return n_runs, w, blocks // group, group


def _permuted_masks():
    out = []
    for d in DILATIONS:
        n_runs = RESIDUES // d
        w = BLOCK // n_runs
        p = np.arange(BLOCK)
        pos = (p % w) * n_runs + p // w
        dist = pos[:, None] - np.concatenate([pos - BLOCK, pos])[None, :]
        band = (dist >= 0) & (dist <= BLOCK)
        first = band & (np.arange(2 * BLOCK)[None, :] >= BLOCK)
        both = [np.where(m, 0.0, NEG).astype(np.float32) for m in (band, first)]
        out.append(np.stack([np.concatenate([m, m], axis=0) for m in both]))
    return jnp.asarray(np.stack(out))


def _run_starts(u, grp, d, seg, n_runs, w, group):
    return [(u + d * q) * seg + (w * group) * grp for q in range(n_runs)]


def _tile_rows(start, w, b):
    off = start + w * b
    if b < 0:
        off = jnp.maximum(off, 0)
    return pl.ds(pl.multiple_of(off, w), w)


def _load_tile(ref, starts, w, b):
    parts = [ref[_tile_rows(st, w, b), :] for st in starts]
    return parts[0] if len(parts) == 1 else jnp.concatenate(parts, axis=0)


def _store_tile(ref, starts, w, b, tile):
    for i, st in enumerate(starts):
        ref[_tile_rows(st, w, b), :] = tile[i * w:(i + 1) * w]


def _add_tile(ref, starts, w, b, tile):
    for i, st in enumerate(starts):
        ref[_tile_rows(st, w, b), :] += tile[i * w:(i + 1) * w]


def _to_residue_major(dst, src, c, seg):
    per = CHUNK // RESIDUES
    for r in range(RESIDUES):
        dst[pl.ds(pl.multiple_of(r * seg + c * per, per), per), :] = src[pl.ds(r, per, stride=RESIDUES), :]


def _from_residue_major(dst, dst_start, src, c, seg):
    per = CHUNK // RESIDUES
    for r in range(RESIDUES):
        dst[pl.ds(dst_start + r, per, stride=RESIDUES), :] = src[pl.ds(pl.multiple_of(r * seg + c * per, per), per), :]


def _stack_heads(t, head0):
    return jnp.concatenate([jnp.where(head0, t, 0.0), jnp.where(head0, 0.0, t)], axis=0).astype(BF16)


def _attention_forward(z, gq2, gk2, masks, shards):
    s = z.shape[0]
    seg = s // RESIDUES
    n_w = len(shards)

    def body(zq_ref, zk_ref, zv_ref, gq_ref, gk_ref, mask_ref, *rest):
        shard_refs, (o_ref, lse_ref), stacks = rest[:n_w], rest[n_w:n_w + 2], rest[n_w + 2:2 * n_w + 2]
        qf, kf, vf, o_st, l_st, tq, tk, tv, send_sems, recv_sems, local_sems = rest[2 * n_w + 2:]
        pair = pl.program_id(0)
        gather = _TwoLevelGather(stacks, send_sems, recv_sems)

        def own_copies():
            mine = gather.index(gather.me)
            return [pltpu.make_async_copy(shard_refs[a], stacks[a].at[mine], local_sems.at[a]) for a in range(n_w)]

        @pl.when(pair == 0)
        def _():
            for cp in own_copies():
                cp.start()
            gather.start(shard_refs)

        @pl.when(pair == 2)
        def _():
            gather.forward()

        bd = _head_blockdiag(1.0 / HEAD_DIM)
        gq = gq_ref[...] * (HEAD_DIM ** -0.5)
        gk = gk_ref[...]

        def prep(c, carry):
            rows = pl.ds(pl.multiple_of(c * CHUNK, CHUNK), CHUNK)
            zq = zq_ref[rows, :].astype(F32)
            zk = zk_ref[rows, :].astype(F32)
            tq[...] = (zq * lax.rsqrt(_head_sum(zq * zq, bd) + EPS)) * gq
            tk[...] = (zk * lax.rsqrt(_head_sum(zk * zk, bd) + EPS)) * gk
            tv[...] = zv_ref[rows, :].astype(F32)
            _to_residue_major(qf, tq, c, seg)
            _to_residue_major(kf, tk, c, seg)
            _to_residue_major(vf, tv, c, seg)
            return carry

        lax.fori_loop(0, s // CHUNK, prep, 0)

        head0 = lax.broadcasted_iota(jnp.int32, (BLOCK, 128), 1) < HEAD_DIM

        def run_branch(branch, d):
            n_runs, w, steps, group = _branch_geometry(d, seg)

            def step(u, grp):
                starts = _run_starts(u, grp, d, seg, n_runs, w, group)
                kt = [_load_tile(kf, starts, w, b).astype(BF16) for b in range(-1, group)]
                vt = [_load_tile(vf, starts, w, b).astype(BF16) for b in range(-1, group)]
                scores = []
                for b in range(group):
                    q2 = _stack_heads(_load_tile(qf, starts, w, b), head0)
                    kc = jnp.concatenate([kt[b], kt[b + 1]], axis=0)
                    mask = mask_ref[branch, 0]
                    if b == 0:
                        mask = jnp.where(grp == 0, mask_ref[branch, 1], mask)
                    scores.append(_dot(q2, kc, NT) + mask)
                sc = jnp.concatenate(scores, axis=0)
                m = jnp.max(sc, axis=-1, keepdims=True)
                e = jnp.exp(sc - m).astype(BF16)
                ones = jnp.ones((BLOCK, 128), BF16)
                fresh = []
                for b in range(group):
                    rows = slice(2 * BLOCK * b, 2 * BLOCK * (b + 1))
                    v_aug = jnp.concatenate([jnp.concatenate([vt[b], ones], axis=1),
                                             jnp.concatenate([vt[b + 1], ones], axis=1)], axis=0)
                    o2 = _dot(e[rows], v_aug, NN)
                    den = o2[:, 128:]
                    o2 = o2[:, :128] * (1.0 / den)
                    l2 = m[rows] + jnp.log(den)
                    fresh.append((jnp.where(head0, o2[:BLOCK], o2[BLOCK:]), jnp.where(head0, l2[:BLOCK], l2[BLOCK:])))
                for b in range(group):
                    o_new, l_new = fresh[b]
                    if branch > 0:
                        l_old = _load_tile(l_st, starts, w, b)
                        top = jnp.maximum(l_old, l_new)
                        e_old = jnp.exp(l_old - top)
                        e_new = jnp.exp(l_new - top)
                        tot = e_old + e_new
                        inv = 1.0 / tot
                        o_new = _load_tile(o_st, starts, w, b) * (e_old * inv) + o_new * (e_new * inv)
                        l_new = top + jnp.log(tot)
                    _store_tile(o_st, starts, w, b, o_new)
                    _store_tile(l_st, starts, w, b, l_new)

            def unit(u, carry):
                def inner(grp, carry2):
                    step(u, grp)
                    return carry2
                return lax.fori_loop(0, steps, inner, carry)

            lax.fori_loop(0, d, unit, 0)

        for branch, d in enumerate(DILATIONS):
            run_branch(branch, d)

        def finish(c, carry):
            _from_residue_major(o_ref, c * CHUNK, o_st, c, seg)
            _from_residue_major(lse_ref, c * CHUNK, l_st, c, seg)
            return carry

        lax.fori_loop(0, s // CHUNK, finish, 0)

        @pl.when(pair == N_PAIRS - 1)
        def _():
            gather.finish()
            for cp in own_copies():
                cp.wait()

    def col(c0):
        return pl.BlockSpec((s, 128), lambda p: (0, c0 + p), pipeline_mode=pl.Buffered(1))

    hbm = pl.BlockSpec(memory_space=pl.ANY)
    out = pl.pallas_call(
        body, name="attention_forward", grid=(N_PAIRS,),
        in_specs=[col(12), col(16), col(20), _const_spec((1, 128)), _const_spec((1, 128)), _const_spec(masks.shape)]
        + [hbm] * n_w,
        out_specs=[pl.BlockSpec((s, 128), lambda p: (0, p)), pl.BlockSpec((s, 128), lambda p: (0, p))] + [hbm] * n_w,
        out_shape=[jax.ShapeDtypeStruct((s, D_ATTN), F32), jax.ShapeDtypeStruct((s, D_ATTN), F32)]
        + [jax.ShapeDtypeStruct((N_DEV,) + w.shape, w.dtype) for w in shards],
        scratch_shapes=[pltpu.VMEM((s, 128), F32)] * 5 + [pltpu.VMEM((CHUNK, 128), F32)] * 3
        + [pltpu.SemaphoreType.DMA((7 * n_w,)), pltpu.SemaphoreType.DMA((7 * n_w,)), pltpu.SemaphoreType.DMA((n_w,))],
        compiler_params=_params(("arbitrary",), vmem=VMEM_LIMIT),
    )(z, z, z, gq2, gk2, masks, *shards)
    return out[0], out[1], out[2:]


def _conv_forward(zt, zprev, w, first):
    u, gb, gc = zt[:, :D_CONV], zt[:, D_CONV:2 * D_CONV], zt[:, 2 * D_CONV:]
    cu = gc * u
    cu_prev = jnp.where(first, 0.0, zprev[:, 2 * D_CONV:] * zprev[:, :D_CONV])
    row = lax.broadcasted_iota(jnp.int32, cu.shape, 0)
    cu1 = jnp.where(row >= 1, pltpu.roll(cu, 1, 0), cu_prev[7:8, :])
    cu2 = jnp.where(row >= 2, pltpu.roll(cu, 2, 0), jnp.where(row == 1, cu_prev[7:8, :], cu_prev[6:7, :]))
    cv = w[0:1, :] * cu2 + w[1:2, :] * cu1 + w[2:3, :] * cu
    return u, gb, gc, cu, cu1, cu2, cv


def _mix_out(z, y_attn, x, conv_w, g_conv_out, g_attn_out, g_ffn, w_out, tm):
    s = x.shape[0]
    hb = tm // 8

    def body(z_ref, zp_ref, ya_ref, x_ref, cw_ref, gc_ref, ga_ref, gf_ref, w_ref, mix_ref, x2_ref, h2_ref):
        first = pl.program_id(0) == 0
        zt = z_ref[...].astype(F32)
        _, gb, _, _, _, _, cv = _conv_forward(zt, zp_ref[...].astype(F32), cw_ref[...], first)
        nc, _ = _rms(gb * cv)
        na, _ = _rms(ya_ref[...])
        mix = jnp.concatenate([nc * gc_ref[...], na * ga_ref[...]], axis=-1).astype(BF16)
        mix_ref[...] = mix
        x2 = x_ref[...] + _dot(mix, w_ref[...], NN)
        x2_ref[...] = x2
        xh, _ = _rms(x2)
        h2_ref[...] = (xh * gf_ref[...]).astype(BF16)

    tile = lambda w: pl.BlockSpec((tm, w), lambda i: (i, 0))
    return pl.pallas_call(
        body, name="mix_out", grid=(s // tm,),
        in_specs=[tile(3 * D_CONV), pl.BlockSpec((8, 3 * D_CONV), lambda i: (jnp.maximum(i * hb - 1, 0), 0)),
                  tile(D_ATTN), tile(D_MODEL), _const_spec((8, D_CONV)), _const_spec((1, D_CONV)), _const_spec((1, D_ATTN)),
                  _const_spec((1, D_MODEL)), _const_spec((D_MODEL, D_MODEL))],
        out_specs=[tile(D_MODEL)] * 3,
        out_shape=[jax.ShapeDtypeStruct((s, D_MODEL), BF16), jax.ShapeDtypeStruct((s, D_MODEL), F32),
                   jax.ShapeDtypeStruct((s, D_MODEL), BF16)],
        compiler_params=_params(("arbitrary",), vmem=VMEM_LIMIT),
    )(z, z, y_attn, x, conv_w, g_conv_out, g_attn_out, g_ffn, w_out)


FF_CHUNK = 256


def _ffn_forward(h2, x2, target, w_gate_t, w_up_t, w_down, tm):
    s = h2.shape[0]

    def body(h_ref, x2_ref, t_ref, wg_ref, wu_ref, wd_ref, a_ref, b_ref, dy_ref, loss_ref, f_buf):
        @pl.when(pl.program_id(0) == 0)
        def _():
            loss_ref[...] = jnp.zeros_like(loss_ref)

        h = h_ref[...]
        for c0 in range(0, D_FF, FF_CHUNK):
            rows = slice(c0, c0 + FF_CHUNK)
            a = _dot(h, wg_ref[rows, :], NT)
            b = _dot(h, wu_ref[rows, :], NT)
            a_ref[:, rows] = a.astype(BF16)
            b_ref[:, rows] = b.astype(BF16)
            f_buf[:, rows] = ((a * jax.nn.sigmoid(a)) * b).astype(BF16)
        y = x2_ref[...] + _dot(f_buf[...], wd_ref[...], NN)
        err = y - t_ref[...]
        dy_ref[...] = err * (1.0 / D_MODEL)
        loss_ref[...] += jnp.sum(err * err)

    tile = lambda w: pl.BlockSpec((tm, w), lambda i: (i, 0))
    return pl.pallas_call(
        body, name="ffn_forward", grid=(s // tm,),
        in_specs=[tile(D_MODEL), tile(D_MODEL), tile(D_MODEL)] + [_const_spec((D_FF, D_MODEL))] * 3,
        out_specs=[tile(D_FF), tile(D_FF), tile(D_MODEL), pl.BlockSpec((8, 128), lambda i: (0, 0))],
        out_shape=[jax.ShapeDtypeStruct((s, D_FF), BF16), jax.ShapeDtypeStruct((s, D_FF), BF16),
                   jax.ShapeDtypeStruct((s, D_MODEL), F32), jax.ShapeDtypeStruct((8, 128), F32)],
        scratch_shapes=[pltpu.VMEM((tm, D_FF), BF16)],
        compiler_params=_params(("arbitrary",), vmem=VMEM_LIMIT),
    )(h2, x2, target, w_gate_t, w_up_t, w_down)


def _ffn_backward(dy, a, b, x2, g_ffn, w_gate_t, w_up_t, w_down, tm):
    s = dy.shape[0]
    half = D_FF // 2
    pieces = [(p0, min(FF_CHUNK, half - p0)) for p0 in range(0, half, FF_CHUNK)]

    def body(dy_ref, a_ref, b_ref, x2_ref, g_ref, wg_ref, wu_ref, wd_ref, da_ref, db_ref, dx2_ref, dg_ref, acc):
        i, c = pl.program_id(0), pl.program_id(1)

        @pl.when((i == 0) & (c == 0))
        def _():
            dg_ref[...] = jnp.zeros_like(dg_ref)

        dyb = dy_ref[...].astype(BF16)
        for p0, width in pieces:
            rows = slice(p0, p0 + width)
            df = _dot(dyb, wd_ref[rows, :], NT)
            av = a_ref[:, rows].astype(F32)
            bv = b_ref[:, rows].astype(F32)
            sig = jax.nn.sigmoid(av)
            silu = av * sig
            da_ref[:, rows] = (df * bv * (sig * (1.0 + av * (1.0 - sig)))).astype(BF16)
            db_ref[:, rows] = (df * silu).astype(BF16)
        dh = _dot(da_ref[...], wg_ref[...], NN) + _dot(db_ref[...], wu_ref[...], NN)

        @pl.when(c == 0)
        def _():
            acc[...] = dh

        @pl.when(c == 1)
        def _():
            dh_all = acc[...] + dh
            xh, r = _rms(x2_ref[...])
            dg_ref[...] += jnp.sum(dh_all * xh, axis=0, keepdims=True)
            dx2_ref[...] = dy_ref[...] + _rms_bwd(dh_all, xh, r, g_ref[...])

    tile = lambda w: pl.BlockSpec((tm, w), lambda i, c: (i, 0))
    part = pl.BlockSpec((tm, half), lambda i, c: (i, c))
    weight = pl.BlockSpec((half, D_MODEL), lambda i, c: (c, 0))
    return pl.pallas_call(
        body, name="ffn_backward", grid=(s // tm, 2),
        in_specs=[tile(D_MODEL), part, part, tile(D_MODEL), _const_spec((1, D_MODEL)), weight, weight, weight],
        out_specs=[part, part, tile(D_MODEL), pl.BlockSpec((1, D_MODEL), lambda i, c: (0, 0))],
        out_shape=[jax.ShapeDtypeStruct((s, D_FF), BF16)] * 2
        + [jax.ShapeDtypeStruct((s, D_MODEL), F32), jax.ShapeDtypeStruct((1, D_MODEL), F32)],
        scratch_shapes=[pltpu.VMEM((tm, D_MODEL), F32)],
        compiler_params=_params(("arbitrary", "arbitrary"), vmem=VMEM_LIMIT),
    )(dy, a, b, x2, g_ffn, w_gate_t, w_up_t, w_down)


def _out_backward(dx2, z, y_attn, conv_w, g_conv_out, g_attn_out, w_out, tm):
    s = dx2.shape[0]
    hb = tm // 8

    def body(dx2_ref, z_ref, zp_ref, ya_ref, cw_ref, gc_ref, ga_ref, w_ref, dya_ref, dgb_ref, dcv_ref, dgc_ref, dga_ref):
        first = pl.program_id(0) == 0

        @pl.when(first)
        def _():
            dgc_ref[...] = jnp.zeros_like(dgc_ref)
            dga_ref[...] = jnp.zeros_like(dga_ref)

        dmix = _dot(dx2_ref[...].astype(BF16), w_ref[...], NT)
        _, gb, _, _, _, _, cv = _conv_forward(z_ref[...].astype(F32), zp_ref[...].astype(F32), cw_ref[...], first)
        ych, rc = _rms(gb * cv)
        dnc = dmix[:, :D_CONV]
        dgc_ref[...] += jnp.sum(dnc * ych, axis=0, keepdims=True)
        dyc = _rms_bwd(dnc, ych, rc, gc_ref[...])
        dgb_ref[...] = (dyc * cv).astype(BF16)
        dcv_ref[...] = dyc * gb
        yah, ra = _rms(ya_ref[...])
        dna = dmix[:, D_CONV:]
        dga_ref[...] += jnp.sum(dna * yah, axis=0, keepdims=True)
        dya_ref[...] = _rms_bwd(dna, yah, ra, ga_ref[...])

    tile = lambda w: pl.BlockSpec((tm, w), lambda i: (i, 0))
    vec = pl.BlockSpec((1, D_CONV), lambda i: (0, 0))
    return pl.pallas_call(
        body, name="out_backward", grid=(s // tm,),
        in_specs=[tile(D_MODEL), tile(3 * D_CONV), pl.BlockSpec((8, 3 * D_CONV), lambda i: (jnp.maximum(i * hb - 1, 0), 0)),
                  tile(D_ATTN), _const_spec((8, D_CONV)), _const_spec((1, D_CONV)), _const_spec((1, D_ATTN)),
                  _const_spec((D_MODEL, D_MODEL))],
        out_specs=[tile(D_ATTN), tile(D_CONV), tile(D_CONV), vec, vec],
        out_shape=[jax.ShapeDtypeStruct((s, D_ATTN), F32), jax.ShapeDtypeStruct((s, D_CONV), BF16),
                   jax.ShapeDtypeStruct((s, D_CONV), F32), jax.ShapeDtypeStruct((1, D_CONV), F32),
                   jax.ShapeDtypeStruct((1, D_ATTN), F32)],
        compiler_params=_params(("arbitrary",), vmem=VMEM_LIMIT),
    )(dx2, z, z, y_attn, conv_w, g_conv_out, g_attn_out, w_out)


def _attention_backward(z, o, lse, dya, gq2, gk2, masks, grad_stacks):
    s = z.shape[0]
    seg = s // RESIDUES
    n_g = len(grad_stacks)

    def body(z_hbm, o_hbm, lse_hbm, do_hbm, gq_ref, gk_ref, mask_ref, *rest):
        stack_refs = rest[:n_g]
        dzq_ref, dzk_ref, dzv_ref, dgq_ref, dgk_ref = rest[n_g:n_g + 5]
        landing = rest[n_g + 5:2 * n_g + 5]
        (qf, kf, vf, dof, stats, dq, dk, dv, t0, t1, t2, zbuf, fbuf, sems, send_sems, recv_sems) = rest[2 * n_g + 5:]
        pair = pl.program_id(0)
        scatter = _DirectReduceScatter(stack_refs, landing, send_sems, recv_sems)

        @pl.when(pair == 0)
        def _():
            scatter.start()

        n_chunks = s // CHUNK
        bd_mean = _head_blockdiag(1.0 / HEAD_DIM)
        bd_sum = _head_blockdiag(1.0)
        gq = gq_ref[...] * (HEAD_DIM ** -0.5)
        gk = gk_ref[...]
        lane = lax.broadcasted_iota(jnp.int32, (CHUNK, 128), 1) % HEAD_DIM

        def chunk_copies(c, slot, n_z, n_f):
            rows = pl.ds(pl.multiple_of(c * CHUNK, CHUNK), CHUNK)
            cps = []
            for i, col in enumerate((12, 16, 20)[:n_z]):
                cols = pl.ds(pl.multiple_of((col + pair) * 128, 128), 128)
                cps.append(pltpu.make_async_copy(z_hbm.at[rows, cols], zbuf.at[slot, i], sems.at[slot, i]))
            for i, src in enumerate((do_hbm, o_hbm, lse_hbm)[:n_f]):
                cols = pl.ds(pl.multiple_of(pair * 128, 128), 128)
                cps.append(pltpu.make_async_copy(src.at[rows, cols], fbuf.at[slot, i], sems.at[slot, 3 + i]))
            return cps

        def prefetched(c, n_z, n_f):
            slot = c % 2

            @pl.when(c + 1 < n_chunks)
            def _():
                for cp in chunk_copies(c + 1, 1 - slot, n_z, n_f):
                    cp.start()

            for cp in chunk_copies(c, slot, n_z, n_f):
                cp.wait()
            return slot

        for cp in chunk_copies(0, 0, 3, 3):
            cp.start()

        def prep(c, carry):
            slot = prefetched(c, 3, 3)
            rows = pl.ds(pl.multiple_of(c * CHUNK, CHUNK), CHUNK)
            zq = zbuf[slot, 0].astype(F32)
            t0[...] = (zq * lax.rsqrt(_head_sum(zq * zq, bd_mean) + EPS)) * gq
            zk = zbuf[slot, 1].astype(F32)
            t1[...] = (zk * lax.rsqrt(_head_sum(zk * zk, bd_mean) + EPS)) * gk
            t2[...] = zbuf[slot, 2].astype(F32)
            _to_residue_major(qf, t0, c, seg)
            _to_residue_major(kf, t1, c, seg)
            _to_residue_major(vf, t2, c, seg)
            _to_residue_major(dof, fbuf.at[slot, 0], c, seg)
            delta = _head_sum(fbuf[slot, 0] * fbuf[slot, 1], bd_sum)
            t0[...] = jnp.where(lane == 0, fbuf[slot, 2], jnp.where(lane == 1, delta, 0.0))
            _to_residue_major(stats, t0, c, seg)
            zero = jnp.zeros((CHUNK, 128), F32)
            dq[rows, :] = zero
            dk[rows, :] = zero
            dv[rows, :] = zero
            return carry

        lax.fori_loop(0, n_chunks, prep, 0)

        head0 = lax.broadcasted_iota(jnp.int32, (BLOCK, 128), 1) < HEAD_DIM

        def run_branch(branch, d):
            n_runs, w, steps, group = _branch_geometry(d, seg)

            def step(u, grp):
                starts = _run_starts(u, grp, d, seg, n_runs, w, group)
                kt = [_load_tile(kf, starts, w, b).astype(BF16) for b in range(-1, group)]
                vt = [_load_tile(vf, starts, w, b).astype(BF16) for b in range(-1, group)]
                dk_t = [jnp.zeros((BLOCK, 128), F32)] * (group + 1)
                dv_t = [jnp.zeros((BLOCK, 128), F32)] * (group + 1)
                dq_t = []
                q2s, do2s, kcs, scores, dps, lses, deltas = [], [], [], [], [], [], []
                for b in range(group):
                    q2 = _stack_heads(_load_tile(qf, starts, w, b), head0)
                    do2 = _stack_heads(_load_tile(dof, starts, w, b), head0)
                    st = _load_tile(stats, starts, w, b)
                    lses += [st[:, 0:1], st[:, HEAD_DIM:HEAD_DIM + 1]]
                    deltas += [st[:, 1:2], st[:, HEAD_DIM + 1:HEAD_DIM + 2]]
                    kc = jnp.concatenate([kt[b], kt[b + 1]], axis=0)
                    vc = jnp.concatenate([vt[b], vt[b + 1]], axis=0)
                    mask = mask_ref[branch, 0]
                    if b == 0:
                        mask = jnp.where(grp == 0, mask_ref[branch, 1], mask)
                    scores.append(_dot(q2, kc, NT) + mask)
                    dps.append(_dot(do2, vc, NT))
                    q2s.append(q2)
                    do2s.append(do2)
                    kcs.append(kc)
                p_all = jnp.exp(jnp.concatenate(scores, axis=0) - jnp.concatenate(lses, axis=0))
                ds_all = (p_all * (jnp.concatenate(dps, axis=0) - jnp.concatenate(deltas, axis=0))).astype(BF16)
                p_all = p_all.astype(BF16)
                for b in range(group):
                    rows = slice(2 * BLOCK * b, 2 * BLOCK * (b + 1))
                    ds, q2, do2, kc = ds_all[rows], q2s[b], do2s[b], kcs[b]
                    dq2 = _dot(ds, kc, NN)
                    dq_t.append(jnp.where(head0, dq2[:BLOCK], dq2[BLOCK:]))
                    dkc = _dot(ds, q2, TN)
                    dvc = _dot(p_all[rows], do2, TN)
                    dk_t[b] = dk_t[b] + dkc[:BLOCK]
                    dk_t[b + 1] = dk_t[b + 1] + dkc[BLOCK:]
                    dv_t[b] = dv_t[b] + dvc[:BLOCK]
                    dv_t[b + 1] = dv_t[b + 1] + dvc[BLOCK:]
                for b in range(group):
                    _add_tile(dq, starts, w, b, dq_t[b])
                for b in range(-1, group):
                    _add_tile(dk, starts, w, b, dk_t[b + 1])
                    _add_tile(dv, starts, w, b, dv_t[b + 1])

            def unit(u, carry):
                def inner(grp, carry2):
                    step(u, grp)
                    return carry2
                return lax.fori_loop(0, steps, inner, carry)

            lax.fori_loop(0, d, unit, 0)

        for branch, d in enumerate(DILATIONS):
            run_branch(branch, d)

        for cp in chunk_copies(0, 0, 2, 0):
            cp.start()

        def finish(c, carry):
            acc_q, acc_k = carry
            rows = pl.ds(pl.multiple_of(c * CHUNK, CHUNK), CHUNK)
            _from_residue_major(t0, 0, dq, c, seg)
            _from_residue_major(t1, 0, dk, c, seg)
            _from_residue_major(t2, 0, dv, c, seg)
            slot = prefetched(c, 2, 0)
            zq = zbuf[slot, 0].astype(F32)
            rq = lax.rsqrt(_head_sum(zq * zq, bd_mean) + EPS)
            qh = zq * rq
            dqn = t0[...]
            acc_q = acc_q + jnp.sum(dqn * qh, axis=0, keepdims=True)
            t = dqn * gq
            dzq_ref[rows, :] = (rq * (t - qh * _head_sum(t * qh, bd_mean))).astype(BF16)
            zk = zbuf[slot, 1].astype(F32)
            rk = lax.rsqrt(_head_sum(zk * zk, bd_mean) + EPS)
            kh = zk * rk
            dkn = t1[...]
            acc_k = acc_k + jnp.sum(dkn * kh, axis=0, keepdims=True)
            t = dkn * gk
            dzk_ref[rows, :] = (rk * (t - kh * _head_sum(t * kh, bd_mean))).astype(BF16)
            dzv_ref[rows, :] = t2[...].astype(BF16)
            return acc_q, acc_k

        zero = jnp.zeros((1, 128), F32)
        acc_q, acc_k = lax.fori_loop(0, s // CHUNK, finish, (zero, zero))
        dgq_ref[...] = acc_q * (HEAD_DIM ** -0.5)
        dgk_ref[...] = acc_k

        @pl.when(pair == N_PAIRS - 1)
        def _():
            scatter.finish()

    hbm = pl.BlockSpec(memory_space=pl.ANY)
    gain = pl.BlockSpec((None, 1, 128), lambda p: (p, 0, 0))
    dz_spec = pl.BlockSpec((s, 128), lambda p: (0, p))
    out = pl.pallas_call(
        body, name="attention_backward", grid=(N_PAIRS,),
        in_specs=[hbm, hbm, hbm, hbm, _const_spec((1, 128)), _const_spec((1, 128)), _const_spec(masks.shape)] + [hbm] * n_g,
        out_specs=[dz_spec, dz_spec, dz_spec, gain, gain] + [hbm] * n_g,
        out_shape=[jax.ShapeDtypeStruct((s, D_ATTN), BF16)] * 3 + [jax.ShapeDtypeStruct((N_PAIRS, 1, 128), F32)] * 2
        + [jax.ShapeDtypeStruct((N_DEV - 1,) + g.shape[1:], g.dtype) for g in grad_stacks],
        scratch_shapes=[pltpu.VMEM((s, 128), F32)] * 8 + [pltpu.VMEM((CHUNK, 128), F32)] * 3
        + [pltpu.VMEM((2, 3, CHUNK, 128), BF16), pltpu.VMEM((2, 3, CHUNK, 128), F32), pltpu.SemaphoreType.DMA((2, 6)),
           pltpu.SemaphoreType.DMA((7 * n_g,)), pltpu.SemaphoreType.DMA((7 * n_g,))],
        compiler_params=_params(("arbitrary",), vmem=VMEM_LIMIT),
    )(z, o, lse, dya, gq2, gk2, masks, *grad_stacks)
    return out[:5], out[5:]


def _in_backward(dcv, dgb, dzq, dzk, dzv, z, x, dx2, g_mix, conv_w, w_in_t, tm):
    s = x.shape[0]
    hb = tm // 8
    last_halo = s // 8 - 1

    def body(dcv_ref, dcn_ref, dgb_ref, dzq_ref, dzk_ref, dzv_ref, z_ref, zp_ref, x_ref, dx2_ref, g_ref, cw_ref, w_ref,
             dz_ref, dx_ref, dg_ref, dcw_ref):
        i = pl.program_id(0)
        first = i == 0
        last = i == pl.num_programs(0) - 1

        @pl.when(first)
        def _():
            dg_ref[...] = jnp.zeros_like(dg_ref)
            dcw_ref[...] = jnp.zeros_like(dcw_ref)

        w = cw_ref[...]
        u, _, gc, cu, cu1, cu2, _ = _conv_forward(z_ref[...].astype(F32), zp_ref[...].astype(F32), w, first)
        dcv_t = dcv_ref[...]
        nxt = jnp.where(last, 0.0, dcn_ref[...])
        row = lax.broadcasted_iota(jnp.int32, dcv_t.shape, 0)
        up1 = jnp.where(row < tm - 1, pltpu.roll(dcv_t, tm - 1, 0), nxt[0:1, :])
        up2 = jnp.where(row < tm - 2, pltpu.roll(dcv_t, tm - 2, 0), jnp.where(row == tm - 2, nxt[0:1, :], nxt[1:2, :]))
        dcu = w[2:3, :] * dcv_t + w[1:2, :] * up1 + w[0:1, :] * up2
        dcw = jnp.concatenate([jnp.sum(dcv_t * cu2, axis=0, keepdims=True), jnp.sum(dcv_t * cu1, axis=0, keepdims=True),
                               jnp.sum(dcv_t * cu, axis=0, keepdims=True), jnp.zeros((5, D_CONV), F32)], axis=0)
        dcw_ref[...] += dcw
        dz_ref[:, :D_CONV] = (dcu * gc).astype(BF16)
        dz_ref[:, D_CONV:2 * D_CONV] = dgb_ref[...]
        dz_ref[:, 2 * D_CONV:3 * D_CONV] = (dcu * u).astype(BF16)
        dz_ref[:, 3 * D_CONV:3 * D_CONV + D_ATTN] = dzq_ref[...]
        dz_ref[:, 3 * D_CONV + D_ATTN:3 * D_CONV + 2 * D_ATTN] = dzk_ref[...]
        dz_ref[:, 3 * D_CONV + 2 * D_ATTN:] = dzv_ref[...]
        dh = _dot(dz_ref[...], w_ref[...], NN)
        xh, r = _rms(x_ref[...])
        dg_ref[...] += jnp.sum(dh * xh, axis=0, keepdims=True)
        dx_ref[...] = dx2_ref[...] + _rms_bwd(dh, xh, r, g_ref[...])

    tile = lambda w: pl.BlockSpec((tm, w), lambda i: (i, 0))
    return pl.pallas_call(
        body, name="in_backward", grid=(s // tm,),
        in_specs=[tile(D_CONV), pl.BlockSpec((8, D_CONV), lambda i: (jnp.minimum((i + 1) * hb, last_halo), 0)),
                  tile(D_CONV), tile(D_ATTN), tile(D_ATTN), tile(D_ATTN),
                  tile(3 * D_CONV), pl.BlockSpec((8, 3 * D_CONV), lambda i: (jnp.maximum(i * hb - 1, 0), 0)),
                  tile(D_MODEL), tile(D_MODEL), _const_spec((1, D_MODEL)), _const_spec((8, D_CONV)),
                  _const_spec((D_IN, D_MODEL))],
        out_specs=[tile(D_IN), tile(D_MODEL), pl.BlockSpec((1, D_MODEL), lambda i: (0, 0)),
                   pl.BlockSpec((8, D_CONV), lambda i: (0, 0))],
        out_shape=[jax.ShapeDtypeStruct((s, D_IN), BF16), jax.ShapeDtypeStruct((s, D_MODEL), F32),
                   jax.ShapeDtypeStruct((1, D_MODEL), F32), jax.ShapeDtypeStruct((8, D_CONV), F32)],
        compiler_params=_params(("arbitrary",), vmem=VMEM_LIMIT),
    )(dcv, dcv, dgb, dzq, dzk, dzv, z, z, x, dx2, g_mix, conv_w, w_in_t)


def _weight_grad(name, a, b, tn, tk, gate=None):
    s, n = a.shape
    steps = s // tk

    def body(*refs):
        gate_ref = refs[0] if gate is not None else None
        a_ref, b_ref, out_ref, acc = refs[1:] if gate is not None else refs
        k = pl.program_id(1)

        @pl.when(k == 0)
        def _():
            acc[...] = jnp.zeros_like(acc)

        lhs = a_ref[...]
        if gate is not None:
            gv = gate_ref[...].astype(F32)
            lhs = ((gv * jax.nn.sigmoid(gv)) * lhs.astype(F32)).astype(BF16)
        acc[...] += _dot(lhs, b_ref[...].astype(BF16), TN)

        @pl.when(k == steps - 1)
        def _():
            out_ref[...] = acc[...].astype(BF16)

    lhs_spec = pl.BlockSpec((tk, tn), lambda j, k: (k, j))
    return pl.pallas_call(
        body, name=name, grid=(n // tn, steps),
        in_specs=([lhs_spec] if gate is not None else []) + [lhs_spec, pl.BlockSpec((tk, D_MODEL), lambda j, k: (k, 0))],
        out_specs=pl.BlockSpec((tn, D_MODEL), lambda j, k: (j, 0)),
        out_shape=jax.ShapeDtypeStruct((n, D_MODEL), BF16),
        scratch_shapes=[pltpu.VMEM((tn, D_MODEL), F32)],
        compiler_params=_params(("parallel", "arbitrary"), vmem=VMEM_LIMIT),
    )(*([gate] if gate is not None else []), a, b)


def _adamw(name, w, g, m, v):
    def body(w_ref, g_ref, m_ref, v_ref, d_ref, nm_ref, nv_ref):
        gv = g_ref[...]
        nm = ADAM_B1 * m_ref[...] + (1.0 - ADAM_B1) * gv
        nv = ADAM_B2 * v_ref[...] + (1.0 - ADAM_B2) * (gv * gv)
        m_hat = nm / (1.0 - ADAM_B1 ** ADAM_STEP)
        v_hat = nv / (1.0 - ADAM_B2 ** ADAM_STEP)
        d_ref[...] = -ADAM_LR * (m_hat / (jnp.sqrt(v_hat) + ADAM_EPS) + ADAM_WD * w_ref[...])
        nm_ref[...] = nm
        nv_ref[...] = nv

    rows = w.shape[0]
    tr = 256 if rows % 256 == 0 else rows
    spec = pl.BlockSpec((tr, w.shape[1]), lambda i: (i, 0))
    return pl.pallas_call(
        body, name=name, grid=(rows // tr,),
        in_specs=[spec] * 4, out_specs=[spec] * 3,
        out_shape=[jax.ShapeDtypeStruct(w.shape, F32)] * 3,
        compiler_params=_params(("parallel",)),
    )(w, g, m, v)


def kernel(x, g_mix, w_in, conv_w, g_q, g_k, g_conv_out, g_attn_out, w_out, g_ffn, w_gate, w_up, w_down, loss_target, m_g_mix, m_w_in, m_conv_w, m_g_q, m_g_k, m_g_conv_out, m_g_attn_out, m_w_out, m_g_ffn, m_w_gate, m_w_up, m_w_down, v_g_mix, v_w_in, v_conv_w, v_g_q, v_g_k, v_g_conv_out, v_g_attn_out, v_w_out, v_g_ffn, v_w_gate, v_w_up, v_w_down):
    s = x.shape[1]
    tm = min(512, s)
    xs, target = x[0], loss_target[0]
    px, py, pc = lax.axis_index("x"), lax.axis_index("y"), lax.axis_index("c")
    device = 4 * px + 2 * py + pc

    conv_block = jnp.zeros((8, 128), F32).at[:3, :HEAD_DIM].set(conv_w[0])
    first, shards = [w_in[0].T, conv_block], [w_out[0], w_gate[0].T, w_up[0].T, w_down[0]]
    w_in_g, conv_g, *shards = _gather_first_weights(first, [BF16, F32], shards)
    w_in_t = w_in_g.reshape(D_IN, D_MODEL)
    conv_full = jnp.transpose(conv_g[:, :3, :HEAD_DIM], (1, 0, 2)).reshape(3, D_CONV)
    conv_full = jnp.concatenate([conv_full, jnp.zeros((5, D_CONV), F32)], axis=0)
    gq2 = jnp.concatenate([g_q, g_q], axis=-1)
    gk2 = jnp.concatenate([g_k, g_k], axis=-1)

    h1, z = _in_projection(xs, g_mix, w_in_t, tm)
    masks = _permuted_masks()
    y_attn, lse, gathered = _attention_forward(z, gq2, gk2, masks, shards)
    w_out_f = gathered[0].reshape(D_MODEL, D_MODEL)
    w_gate_t, w_up_t, w_down_f = [g.reshape(D_FF, D_MODEL) for g in gathered[1:]]
    mix, x2, h2 = _mix_out(z, y_attn, xs, conv_full, g_conv_out, g_attn_out, g_ffn, w_out_f, tm)
    a, b, dy, sq_err = _ffn_forward(h2, x2, target, w_gate_t, w_up_t, w_down_f, tm)

    tk = min(512, s)
    da, db, dx2, dg_ffn = _ffn_backward(dy, a, b, x2, g_ffn, w_gate_t, w_up_t, w_down_f, tm)
    dya, dgb, dcv, dg_conv_out, dg_attn_out = _out_backward(dx2, z, y_attn, conv_full, g_conv_out, g_attn_out, w_out_f, tm)
    early = [
        _weight_grad("grad_w_out", mix, dx2, D_MODEL, tk),
        _weight_grad("grad_w_gate", da, h2, D_FF // 2, tk),
        _weight_grad("grad_w_up", db, h2, D_FF // 2, tk),
        _weight_grad("grad_w_down", b, dy, D_FF // 2, tk, gate=a),
    ]
    early = [g.reshape(N_DEV, g.shape[0] // N_DEV, D_MODEL) for g in early]
    (dzq, dzk, dzv, dgq_pairs, dgk_pairs), landed = _attention_backward(z, y_attn, lse, dya, gq2, gk2, masks, early)
    dz, grad_x, dg_mix, dconv = _in_backward(dcv, dgb, dzq, dzk, dzv, z, xs, dx2, g_mix, conv_full, w_in_t, tm)
    dev = jnp.reshape(device, (1,)).astype(jnp.int32)
    gw_out, gw_gate_t, gw_up_t, gw_down = [
        _add_received("grad_sum_" + n, dev, g, r) for n, g, r in zip(("w_out", "w_gate", "w_up", "w_down"), early, landed)]

    late = [_weight_grad("grad_w_in", dz, h1, D_IN // 2, tk).reshape(4, 2, D_IN // N_DEV, D_MODEL)]
    from_sibling = _reduce_scatter_cores(late)
    core = jnp.reshape(pc, (1,)).astype(jnp.int32)
    partials = _add_core_partials(core, late, from_sibling)
    from_chips = _reduce_scatter_chips(partials)
    chip = jnp.reshape(2 * px + py, (1,)).astype(jnp.int32)
    (gw_in_t,) = _add_chip_partials(chip, partials, from_chips)
    gw_in = gw_in_t.T

    dg_q = jnp.sum(dgq_pairs.reshape(2 * N_PAIRS, HEAD_DIM), axis=0)
    dg_k = jnp.sum(dgk_pairs.reshape(2 * N_PAIRS, HEAD_DIM), axis=0)
    zeros = lambda n: jnp.zeros((n,), F32)
    small = jnp.stack([
        dg_mix[0], dg_ffn[0],
        jnp.concatenate([dg_conv_out[0], dg_attn_out[0]]),
        jnp.concatenate([dg_q, dg_k, zeros(D_MODEL - 2 * HEAD_DIM)]),
        jnp.concatenate([dconv[0], dconv[1]]),
        jnp.concatenate([dconv[2], zeros(D_CONV)]),
        jnp.concatenate([sq_err[0, :1], zeros(D_MODEL - 1)]),
        zeros(D_MODEL),
    ])
    total = _all_reduce_small(small)
    loss = total[6, 0] * (0.5 / D_MODEL)
    gg_mix, gg_ffn = total[0:1], total[1:2]
    gg_conv_out, gg_attn_out = total[2:3, :D_CONV], total[2:3, D_CONV:]
    gg_q, gg_k = total[3:4, :HEAD_DIM], total[3:4, HEAD_DIM:2 * HEAD_DIM]
    conv_total = jnp.stack([total[4, :D_CONV], total[4, D_CONV:], total[5, :D_CONV]])
    g_conv = lax.dynamic_slice(conv_total, (0, device * HEAD_DIM), (3, HEAD_DIM))

    names = ["g_mix", "w_in", "conv_w", "g_q", "g_k", "g_conv_out", "g_attn_out", "w_out", "g_ffn", "w_gate", "w_up", "w_down"]
    weights = [g_mix, w_in[0], conv_w[0], g_q, g_k, g_conv_out, g_attn_out, w_out[0], g_ffn, w_gate[0], w_up[0], w_down[0]]
    grad_list = [gg_mix, gw_in, g_conv, gg_q, gg_k, gg_conv_out, gg_attn_out, gw_out, gg_ffn, gw_gate_t, gw_up_t, gw_down]
    m_list = [m_g_mix, m_w_in[0], m_conv_w[0], m_g_q, m_g_k, m_g_conv_out, m_g_attn_out, m_w_out[0], m_g_ffn, m_w_gate[0], m_w_up[0], m_w_down[0]]
    v_list = [v_g_mix, v_w_in[0], v_conv_w[0], v_g_q, v_g_k, v_g_conv_out, v_g_attn_out, v_w_out[0], v_g_ffn, v_w_gate[0], v_w_up[0], v_w_down[0]]
    stacked = {"w_in", "conv_w", "w_out", "w_gate", "w_up", "w_down"}
    transposed = {"w_gate", "w_up"}
    out_g, out_d, out_m, out_v = [], [], [], []
    for name, w, g, m, v in zip(names, weights, grad_list, m_list, v_list):
        if name in transposed:
            w, m, v = w.T, m.T, v.T
        delta, new_m, new_v = _adamw("adamw_" + name, w, g, m, v)
        if name in transposed:
            g, delta, new_m, new_v = g.T, delta.T, new_m.T, new_v.T
        lift = (lambda t: t[None]) if name in stacked else (lambda t: t)
        out_g.append(lift(g))
        out_d.append(lift(delta))
        out_m.append(lift(new_m))
        out_v.append(lift(new_v))
    return (loss, grad_x[None], *out_g, *out_d, *out_m, *out_v)
```

```python
import jax
import jax.numpy as jnp
import numpy as np
from jax import lax
from jax.experimental import pallas as pl
from jax.experimental.pallas import tpu as pltpu

F32 = jnp.float32
BF16 = jnp.bfloat16

D_MODEL = 1024
D_CONV = 512
D_ATTN = 512
D_IN = 3072
D_FF = 2816
HEAD_DIM = 64
N_PAIRS = 4
BLOCK = 128
DILATIONS = (1, 4, 16)
N_DEV = 8
EPS = 1e-6
NEG = -1e30

ADAM_LR = 0.001
ADAM_B1 = 0.9
ADAM_B2 = 0.999
ADAM_EPS = 1e-08
ADAM_WD = 0.01
ADAM_STEP = 10

NT = (((1,), (1,)), ((), ()))
NN = (((1,), (0,)), ((), ()))
TN = (((0,), (0,)), ((), ()))
MESH = pl.DeviceIdType.MESH

VMEM_LIMIT = 56 * 1024 * 1024


def _params(semantics=None, vmem=None):
    kw = {}
    if semantics is not None:
        kw["dimension_semantics"] = semantics
    if vmem is not None:
        kw["vmem_limit_bytes"] = vmem
    return pltpu.CompilerParams(**kw)


def _dot(a, b, dims):
    return lax.dot_general(a, b, dims, preferred_element_type=F32)


def _const_spec(shape):
    n = len(shape)
    return pl.BlockSpec(shape, lambda *_: (0,) * n, pipeline_mode=pl.Buffered(1))


def _rms(x):
    r = lax.rsqrt(jnp.mean(x * x, axis=-1, keepdims=True) + EPS)
    return x * r, r


def _rms_bwd(dy, xh, r, g):
    t = dy * g
    return r * (t - xh * jnp.mean(t * xh, axis=-1, keepdims=True))


def _head_blockdiag(scale):
    i = lax.broadcasted_iota(jnp.int32, (128, 128), 0) // HEAD_DIM
    j = lax.broadcasted_iota(jnp.int32, (128, 128), 1) // HEAD_DIM
    return jnp.where(i == j, scale, 0.0).astype(BF16)


def _head_sum(x, bd):
    hi = x.astype(BF16)
    lo = (x - hi.astype(F32)).astype(BF16)
    return _dot(hi, bd, NN) + _dot(lo, bd, NN)


def _place():
    return lax.axis_index("x"), lax.axis_index("y"), lax.axis_index("c")


class _TwoLevelGather:
    def __init__(self, stacks, send_sems, recv_sems):
        self.stacks, self.send_sems, self.recv_sems = stacks, send_sems, recv_sems
        x, y, c = _place()
        self.c = c
        self.me, self.sibling = (x, y, c), (x, y, 1 - c)
        self.chips = [(x, 1 - y), (1 - x, y), (1 - x, 1 - y)]

    @staticmethod
    def index(p):
        return 4 * p[0] + 2 * p[1] + p[2]

    def copy(self, a, k, block, to, src=None):
        dst = self.stacks[a].at[self.index(block)]
        return pltpu.make_async_remote_copy(
            src_ref=dst if src is None else src, dst_ref=dst,
            send_sem=self.send_sems.at[7 * a + k], recv_sem=self.recv_sems.at[7 * a + k],
            device_id=to, device_id_type=MESH)

    def first(self, a, src=None):
        cps = [self.copy(a, 0, self.me, self.sibling, src)]
        return cps + [self.copy(a, 1 + j, self.me, (*chip, self.c), src) for j, chip in enumerate(self.chips)]

    def forwards(self, a):
        return [self.copy(a, 4 + j, (*chip, self.c), self.sibling) for j, chip in enumerate(self.chips)]

    def start(self, srcs=None):
        for a in range(len(self.stacks)):
            for cp in self.first(a, None if srcs is None else srcs[a]):
                cp.start()

    def forward(self):
        for a in range(len(self.stacks)):
            for j, (chip, fwd) in enumerate(zip(self.chips, self.forwards(a))):
                self.copy(a, 1 + j, (*chip, self.c), self.me).wait_recv()
                fwd.start()

    def finish(self):
        for a in range(len(self.stacks)):
            self.copy(a, 0, self.sibling, self.me).wait_recv()
            for j, chip in enumerate(self.chips):
                self.copy(a, 4 + j, (*chip, 1 - self.c), self.me).wait_recv()
        for a in range(len(self.stacks)):
            for cp in self.first(a) + self.forwards(a):
                cp.wait_send()


def _gather_first_weights(gathered_parts, gathered_dtypes, cast_parts):
    n, m = len(gathered_parts), len(cast_parts)

    def body(*refs):
        ins, casts_in = refs[:n], refs[n:n + m]
        outs, casts_out = refs[n + m:2 * n + m], refs[2 * n + m:2 * n + 2 * m]
        send_sems, recv_sems = refs[2 * n + 2 * m], refs[2 * n + 2 * m + 1]
        gather = _TwoLevelGather(outs, send_sems, recv_sems)
        for a in range(n):
            outs[a][gather.index(gather.me)] = ins[a][...].astype(outs[a].dtype)
        gather.start()
        for a in range(m):
            casts_out[a][...] = casts_in[a][...].astype(BF16)
        gather.forward()
        gather.finish()

    vm = pl.BlockSpec(memory_space=pltpu.VMEM)
    return pl.pallas_call(
        body, name="weight_all_gather",
        out_shape=[jax.ShapeDtypeStruct((N_DEV,) + p.shape, dt) for p, dt in zip(gathered_parts, gathered_dtypes)]
        + [jax.ShapeDtypeStruct(p.shape, BF16) for p in cast_parts],
        in_specs=[vm] * (n + m), out_specs=[vm] * (n + m),
        scratch_shapes=[pltpu.SemaphoreType.DMA((7 * n,)), pltpu.SemaphoreType.DMA((7 * n,))],
        compiler_params=_params(vmem=VMEM_LIMIT),
    )(*gathered_parts, *cast_parts)


class _DirectReduceScatter:
    def __init__(self, stacks, landing, send_sems, recv_sems):
        self.stacks, self.landing, self.send_sems, self.recv_sems = stacks, landing, send_sems, recv_sems
        self.place = _place()

    def copies(self):
        x, y, c = self.place
        cps = []
        for a in range(len(self.stacks)):
            for k in range(1, N_DEV):
                peer = (1 - x if k & 4 else x, 1 - y if k & 2 else y, 1 - c if k & 1 else c)
                cps.append(pltpu.make_async_remote_copy(
                    src_ref=self.stacks[a].at[4 * peer[0] + 2 * peer[1] + peer[2]], dst_ref=self.landing[a].at[k - 1],
                    send_sem=self.send_sems.at[7 * a + k - 1], recv_sem=self.recv_sems.at[7 * a + k - 1],
                    device_id=peer, device_id_type=MESH))
        return cps

    def start(self):
        for cp in self.copies():
            cp.start()

    def finish(self):
        for cp in self.copies():
            cp.wait_recv()
        for cp in self.copies():
            cp.wait_send()


def _add_received(name, device, stack, received):
    def body(dev_ref, own_ref, recv_ref, out_ref):
        del dev_ref
        total = own_ref[...].astype(F32)
        for k in range(N_DEV - 1):
            total = total + recv_ref[k].astype(F32)
        out_ref[...] = total

    rows = stack.shape[1]
    return pl.pallas_call(
        body, name=name,
        grid_spec=pltpu.PrefetchScalarGridSpec(
            num_scalar_prefetch=1, grid=(1,),
            in_specs=[pl.BlockSpec((None, rows, D_MODEL), lambda i, dref: (dref[0], 0, 0)),
                      pl.BlockSpec(received.shape, lambda i, dref: (0, 0, 0))],
            out_specs=pl.BlockSpec((rows, D_MODEL), lambda i, dref: (0, 0))),
        out_shape=jax.ShapeDtypeStruct((rows, D_MODEL), F32),
        compiler_params=_params(("arbitrary",), vmem=VMEM_LIMIT),
    )(device, stack, received)


def _all_reduce_small(v):
    def body(v_ref, out_ref, gathered, send_sems, recv_sems):
        x, y, c = _place()
        mine = 4 * x + 2 * y + c
        gathered[mine] = v_ref[...]

        def copy(k):
            peer = (1 - x if k & 4 else x, 1 - y if k & 2 else y, 1 - c if k & 1 else c)
            return pltpu.make_async_remote_copy(
                src_ref=gathered.at[mine], dst_ref=gathered.at[mine],
                send_sem=send_sems.at[k - 1], recv_sem=recv_sems.at[k - 1], device_id=peer, device_id_type=MESH)

        copies = [copy(k) for k in range(1, N_DEV)]
        for cp in copies:
            cp.start()
        for cp in copies:
            cp.wait_recv()
        for cp in copies:
            cp.wait_send()
        total = gathered[0]
        for d in range(1, N_DEV):
            total = total + gathered[d]
        out_ref[...] = total

    vm = pl.BlockSpec(memory_space=pltpu.VMEM)
    return pl.pallas_call(
        body, name="small_all_reduce",
        out_shape=jax.ShapeDtypeStruct(v.shape, F32),
        in_specs=[vm], out_specs=vm,
        scratch_shapes=[pltpu.VMEM((N_DEV,) + v.shape, F32),
                        pltpu.SemaphoreType.DMA((N_DEV - 1,)), pltpu.SemaphoreType.DMA((N_DEV - 1,))],
    )(v)


def _reduce_scatter_cores(grads):
    n = len(grads)

    def body(*refs):
        ins, outs = refs[:n], refs[n:2 * n]
        send_sems, recv_sems = refs[2 * n], refs[2 * n + 1]
        x, y, c = _place()
        copies = []
        for a in range(n):
            for k in range(4):
                copies.append(pltpu.make_async_remote_copy(
                    src_ref=ins[a].at[k, 1 - c], dst_ref=outs[a].at[k],
                    send_sem=send_sems.at[4 * a + k], recv_sem=recv_sems.at[4 * a + k],
                    device_id=(x, y, 1 - c), device_id_type=MESH))
        for cp in copies:
            cp.start()
        for cp in copies:
            cp.wait_recv()
        for cp in copies:
            cp.wait_send()

    hbm = pl.BlockSpec(memory_space=pl.ANY)
    return pl.pallas_call(
        body, name="grad_reduce_scatter_cores",
        out_shape=[jax.ShapeDtypeStruct((4,) + g.shape[2:], g.dtype) for g in grads],
        in_specs=[hbm] * n, out_specs=[hbm] * n,
        scratch_shapes=[pltpu.SemaphoreType.DMA((4 * n,)), pltpu.SemaphoreType.DMA((4 * n,))],
    )(*grads)


def _reduce_scatter_chips(partials):
    n = len(partials)

    def body(*refs):
        ins, outs = refs[:n], refs[n:2 * n]
        send_sems, recv_sems = refs[2 * n], refs[2 * n + 1]
        x, y, c = _place()
        copies = []
        for a in range(n):
            for j in (1, 2, 3):
                px = 1 - x if j & 2 else x
                py = 1 - y if j & 1 else y
                copies.append(pltpu.make_async_remote_copy(
                    src_ref=ins[a].at[2 * px + py], dst_ref=outs[a].at[j - 1],
                    send_sem=send_sems.at[3 * a + j - 1], recv_sem=recv_sems.at[3 * a + j - 1],
                    device_id=(px, py, c), device_id_type=MESH))
        for cp in copies:
            cp.start()
        for cp in copies:
            cp.wait_recv()
        for cp in copies:
            cp.wait_send()

    hbm = pl.BlockSpec(memory_space=pl.ANY)
    return pl.pallas_call(
        body, name="grad_reduce_scatter_chips",
        out_shape=[jax.ShapeDtypeStruct((3,) + p.shape[1:], p.dtype) for p in partials],
        in_specs=[hbm] * n, out_specs=[hbm] * n,
        scratch_shapes=[pltpu.SemaphoreType.DMA((3 * n,)), pltpu.SemaphoreType.DMA((3 * n,))],
    )(*partials)


def _add_core_partials(core, grads, received):
    n = len(grads)

    def body(core_ref, *refs):
        del core_ref
        gs, rs, outs = refs[:n], refs[n:2 * n], refs[2 * n:]
        for a in range(n):
            outs[a][...] = (gs[a][...].astype(F32) + rs[a][...].astype(F32)).astype(outs[a].dtype)

    in_specs = [pl.BlockSpec((None, None) + g.shape[2:], lambda k, cref: (k, cref[0], 0, 0)) for g in grads]
    in_specs += [pl.BlockSpec((None,) + r.shape[1:], lambda k, cref: (k, 0, 0)) for r in received]
    out_specs = [pl.BlockSpec((None,) + r.shape[1:], lambda k, cref: (k, 0, 0)) for r in received]
    return pl.pallas_call(
        body, name="grad_add_core_partials",
        grid_spec=pltpu.PrefetchScalarGridSpec(num_scalar_prefetch=1, grid=(4,), in_specs=in_specs, out_specs=out_specs),
        out_shape=[jax.ShapeDtypeStruct(r.shape, r.dtype) for r in received],
        compiler_params=_params(("arbitrary",)),
    )(core, *grads, *received)


def _add_chip_partials(chip, partials, received):
    n = len(partials)

    def body(chip_ref, *refs):
        del chip_ref
        ps, rs, outs = refs[:n], refs[n:2 * n], refs[2 * n:]
        for a in range(n):
            outs[a][...] = ((ps[a][...].astype(F32) + rs[a][0].astype(F32)) + rs[a][1].astype(F32)) + rs[a][2].astype(F32)

    in_specs = [pl.BlockSpec((None,) + p.shape[1:], lambda i, cref: (cref[0], 0, 0)) for p in partials]
    in_specs += [pl.BlockSpec(r.shape, lambda i, cref: (0, 0, 0)) for r in received]
    out_specs = [pl.BlockSpec(p.shape[1:], lambda i, cref: (0, 0)) for p in partials]
    return pl.pallas_call(
        body, name="grad_add_chip_partials",
        grid_spec=pltpu.PrefetchScalarGridSpec(num_scalar_prefetch=1, grid=(1,), in_specs=in_specs, out_specs=out_specs),
        out_shape=[jax.ShapeDtypeStruct(p.shape[1:], F32) for p in partials],
        compiler_params=_params(("arbitrary",), vmem=VMEM_LIMIT),
    )(chip, *partials, *received)


def _in_projection(x, g_mix, w_in_t, tm):
    s = x.shape[0]

    def body(x_ref, g_ref, w_ref, h_ref, z_ref):
        xh, _ = _rms(x_ref[...])
        h = (xh * g_ref[...]).astype(BF16)
        h_ref[...] = h
        for n0 in range(0, D_IN, 512):
            z_ref[:, n0:n0 + 512] = _dot(h, w_ref[n0:n0 + 512, :], NT).astype(BF16)

    return pl.pallas_call(
        body, name="in_projection", grid=(s // tm,),
        in_specs=[pl.BlockSpec((tm, D_MODEL), lambda i: (i, 0)), _const_spec((1, D_MODEL)), _const_spec((D_IN, D_MODEL))],
        out_specs=[pl.BlockSpec((tm, D_MODEL), lambda i: (i, 0)), pl.BlockSpec((tm, D_IN), lambda i: (i, 0))],
        out_shape=[jax.ShapeDtypeStruct((s, D_MODEL), BF16), jax.ShapeDtypeStruct((s, D_IN), BF16)],
        compiler_params=_params(("parallel",), vmem=VMEM_LIMIT),
    )(x, g_mix, w_in_t)


RESIDUES = 16
CHUNK = 512
GROUP = 8


def _branch_geometry(d, seg):
    n_runs = RESIDUES // d
    w = BLOCK // n_runs
    blocks = seg // w
    group = min(GROUP, blocks)
    chains = min(GROUP // group, d)
    return n_runs, w, blocks // group, group, chains


def _permuted_masks():
    out = []
    for d in DILATIONS:
        n_runs = RESIDUES // d
        w = BLOCK // n_runs
        p = np.arange(BLOCK)
        pos = (p % w) * n_runs + p // w
        dist = pos[:, None] - np.concatenate([pos - BLOCK, pos])[None, :]
        band = (dist >= 0) & (dist <= BLOCK)
        first = band & (np.arange(2 * BLOCK)[None, :] >= BLOCK)
        both = [np.where(m, 0.0, NEG).astype(np.float32) for m in (band, first)]
        out.append(np.stack([np.concatenate([m, m], axis=0) for m in both]))
    return jnp.asarray(np.stack(out))


def _run_starts(u, grp, d, seg, n_runs, w, group):
    return [(u + d * q) * seg + (w * group) * grp for q in range(n_runs)]


def _tile_rows(start, w, b):
    off = start + w * b
    if b < 0:
        off = jnp.maximum(off, 0)
    return pl.ds(pl.multiple_of(off, w), w)


def _load_tile(ref, starts, w, b):
    parts = [ref[_tile_rows(st, w, b), :] for st in starts]
    return parts[0] if len(parts) == 1 else jnp.concatenate(parts, axis=0)


def _store_tile(ref, starts, w, b, tile):
    for i, st in enumerate(starts):
        ref[_tile_rows(st, w, b), :] = tile[i * w:(i + 1) * w]


def _add_tile(ref, starts, w, b, tile):
    for i, st in enumerate(starts):
        ref[_tile_rows(st, w, b), :] += tile[i * w:(i + 1) * w]


def _to_residue_major(dst, src, c, seg):
    per = CHUNK // RESIDUES
    for r in range(RESIDUES):
        dst[pl.ds(pl.multiple_of(r * seg + c * per, per), per), :] = src[pl.ds(r, per, stride=RESIDUES), :]


def _from_residue_major(dst, dst_start, src, c, seg):
    per = CHUNK // RESIDUES
    for r in range(RESIDUES):
        dst[pl.ds(dst_start + r, per, stride=RESIDUES), :] = src[pl.ds(pl.multiple_of(r * seg + c * per, per), per), :]


def _stack_heads(t, head0):
    return jnp.concatenate([jnp.where(head0, t, 0.0), jnp.where(head0, 0.0, t)], axis=0).astype(BF16)


def _attention_forward(z, gq2, gk2, masks, shards):
    s = z.shape[0]
    seg = s // RESIDUES
    n_w = len(shards)

    def body(zq_ref, zk_ref, zv_ref, gq_ref, gk_ref, mask_ref, *rest):
        shard_refs, (o_ref, lse_ref), stacks = rest[:n_w], rest[n_w:n_w + 2], rest[n_w + 2:2 * n_w + 2]
        qf, kf, vf, o_st, l_st, tq, tk, tv, send_sems, recv_sems, local_sems = rest[2 * n_w + 2:]
        pair = pl.program_id(0)
        gather = _TwoLevelGather(stacks, send_sems, recv_sems)

        def own_copies():
            mine = gather.index(gather.me)
            return [pltpu.make_async_copy(shard_refs[a], stacks[a].at[mine], local_sems.at[a]) for a in range(n_w)]

        @pl.when(pair == 0)
        def _():
            for cp in own_copies():
                cp.start()
            gather.start(shard_refs)

        @pl.when(pair == 2)
        def _():
            gather.forward()

        bd = _head_blockdiag(1.0 / HEAD_DIM)
        gq = gq_ref[...] * (HEAD_DIM ** -0.5)
        gk = gk_ref[...]

        def prep(c, carry):
            rows = pl.ds(pl.multiple_of(c * CHUNK, CHUNK), CHUNK)
            zq = zq_ref[rows, :].astype(F32)
            zk = zk_ref[rows, :].astype(F32)
            tq[...] = (zq * lax.rsqrt(_head_sum(zq * zq, bd) + EPS)) * gq
            tk[...] = (zk * lax.rsqrt(_head_sum(zk * zk, bd) + EPS)) * gk
            tv[...] = zv_ref[rows, :].astype(F32)
            _to_residue_major(qf, tq, c, seg)
            _to_residue_major(kf, tk, c, seg)
            _to_residue_major(vf, tv, c, seg)
            return carry

        lax.fori_loop(0, s // CHUNK, prep, 0)

        head0 = lax.broadcasted_iota(jnp.int32, (BLOCK, 128), 1) < HEAD_DIM

        def run_branch(branch, d):
            n_runs, w, steps, group, chains = _branch_geometry(d, seg)
            slots = [(ci, b) for ci in range(chains) for b in range(group)]

            def step(u, grp):
                starts = [_run_starts(u + ci, grp, d, seg, n_runs, w, group) for ci in range(chains)]
                kt = [[_load_tile(kf, st, w, b).astype(BF16) for b in range(-1, group)] for st in starts]
                vt = [[_load_tile(vf, st, w, b).astype(BF16) for b in range(-1, group)] for st in starts]
                scores = []
                for ci, b in slots:
                    q2 = _stack_heads(_load_tile(qf, starts[ci], w, b), head0)
                    kc = jnp.concatenate([kt[ci][b], kt[ci][b + 1]], axis=0)
                    mask = mask_ref[branch, 0]
                    if b == 0:
                        mask = jnp.where(grp == 0, mask_ref[branch, 1], mask)
                    scores.append(_dot(q2, kc, NT) + mask)
                sc = jnp.concatenate(scores, axis=0)
                m = jnp.max(sc, axis=-1, keepdims=True)
                e = jnp.exp(sc - m).astype(BF16)
                ones = jnp.ones((BLOCK, 128), BF16)
                fresh = []
                for n, (ci, b) in enumerate(slots):
                    rows = slice(2 * BLOCK * n, 2 * BLOCK * (n + 1))
                    v_aug = jnp.concatenate([jnp.concatenate([vt[ci][b], ones], axis=1),
                                             jnp.concatenate([vt[ci][b + 1], ones], axis=1)], axis=0)
                    o2 = _dot(e[rows], v_aug, NN)
                    den = o2[:, 128:]
                    o2 = o2[:, :128] * (1.0 / den)
                    l2 = m[rows] + jnp.log(den)
                    fresh.append((jnp.where(head0, o2[:BLOCK], o2[BLOCK:]), jnp.where(head0, l2[:BLOCK], l2[BLOCK:])))
                for n, (ci, b) in enumerate(slots):
                    o_new, l_new = fresh[n]
                    if branch > 0:
                        l_old = _load_tile(l_st, starts[ci], w, b)
                        top = jnp.maximum(l_old, l_new)
                        e_old = jnp.exp(l_old - top)
                        e_new = jnp.exp(l_new - top)
                        tot = e_old + e_new
                        inv = 1.0 / tot
                        o_new = _load_tile(o_st, starts[ci], w, b) * (e_old * inv) + o_new * (e_new * inv)
                        l_new = top + jnp.log(tot)
                    _store_tile(o_st, starts[ci], w, b, o_new)
                    _store_tile(l_st, starts[ci], w, b, l_new)

            def unit(it, carry):
                def inner(grp, carry2):
                    step(it * chains, grp)
                    return carry2
                return lax.fori_loop(0, steps, inner, carry)

            lax.fori_loop(0, d // chains, unit, 0)

        for branch, d in enumerate(DILATIONS):
            run_branch(branch, d)

        def finish(c, carry):
            _from_residue_major(o_ref, c * CHUNK, o_st, c, seg)
            _from_residue_major(lse_ref, c * CHUNK, l_st, c, seg)
            return carry

        lax.fori_loop(0, s // CHUNK, finish, 0)

        @pl.when(pair == N_PAIRS - 1)
        def _():
            gather.finish()
            for cp in own_copies():
                cp.wait()

    def col(c0):
        return pl.BlockSpec((s, 128), lambda p: (0, c0 + p), pipeline_mode=pl.Buffered(1))

    hbm = pl.BlockSpec(memory_space=pl.ANY)
    out = pl.pallas_call(
        body, name="attention_forward", grid=(N_PAIRS,),
        in_specs=[col(12), col(16), col(20), _const_spec((1, 128)), _const_spec((1, 128)), _const_spec(masks.shape)]
        + [hbm] * n_w,
        out_specs=[pl.BlockSpec((s, 128), lambda p: (0, p)), pl.BlockSpec((s, 128), lambda p: (0, p))] + [hbm] * n_w,
        out_shape=[jax.ShapeDtypeStruct((s, D_ATTN), F32), jax.ShapeDtypeStruct((s, D_ATTN), F32)]
        + [jax.ShapeDtypeStruct((N_DEV,) + w.shape, w.dtype) for w in shards],
        scratch_shapes=[pltpu.VMEM((s, 128), F32)] * 5 + [pltpu.VMEM((CHUNK, 128), F32)] * 3
        + [pltpu.SemaphoreType.DMA((7 * n_w,)), pltpu.SemaphoreType.DMA((7 * n_w,)), pltpu.SemaphoreType.DMA((n_w,))],
        compiler_params=_params(("arbitrary",), vmem=VMEM_LIMIT),
    )(z, z, z, gq2, gk2, masks, *shards)
    return out[0], out[1], out[2:]


def _conv_forward(zt, zprev, w, first):
    u, gb, gc = zt[:, :D_CONV], zt[:, D_CONV:2 * D_CONV], zt[:, 2 * D_CONV:]
    cu = gc * u
    cu_prev = jnp.where(first, 0.0, zprev[:, 2 * D_CONV:] * zprev[:, :D_CONV])
    row = lax.broadcasted_iota(jnp.int32, cu.shape, 0)
    cu1 = jnp.where(row >= 1, pltpu.roll(cu, 1, 0), cu_prev[7:8, :])
    cu2 = jnp.where(row >= 2, pltpu.roll(cu, 2, 0), jnp.where(row == 1, cu_prev[7:8, :], cu_prev[6:7, :]))
    cv = w[0:1, :] * cu2 + w[1:2, :] * cu1 + w[2:3, :] * cu
    return u, gb, gc, cu, cu1, cu2, cv


def _mix_out(z, y_attn, x, conv_w, g_conv_out, g_attn_out, g_ffn, w_out, tm):
    s = x.shape[0]
    hb = tm // 8

    def body(z_ref, zp_ref, ya_ref, x_ref, cw_ref, gc_ref, ga_ref, gf_ref, w_ref, mix_ref, x2_ref, h2_ref):
        first = pl.program_id(0) == 0
        zt = z_ref[...].astype(F32)
        _, gb, _, _, _, _, cv = _conv_forward(zt, zp_ref[...].astype(F32), cw_ref[...], first)
        nc, _ = _rms(gb * cv)
        na, _ = _rms(ya_ref[...])
        mix = jnp.concatenate([nc * gc_ref[...], na * ga_ref[...]], axis=-1).astype(BF16)
        mix_ref[...] = mix
        x2 = x_ref[...] + _dot(mix, w_ref[...], NN)
        x2_ref[...] = x2
        xh, _ = _rms(x2)
        h2_ref[...] = (xh * gf_ref[...]).astype(BF16)

    tile = lambda w: pl.BlockSpec((tm, w), lambda i: (i, 0))
    return pl.pallas_call(
        body, name="mix_out", grid=(s // tm,),
        in_specs=[tile(3 * D_CONV), pl.BlockSpec((8, 3 * D_CONV), lambda i: (jnp.maximum(i * hb - 1, 0), 0)),
                  tile(D_ATTN), tile(D_MODEL), _const_spec((8, D_CONV)), _const_spec((1, D_CONV)), _const_spec((1, D_ATTN)),
                  _const_spec((1, D_MODEL)), _const_spec((D_MODEL, D_MODEL))],
        out_specs=[tile(D_MODEL)] * 3,
        out_shape=[jax.ShapeDtypeStruct((s, D_MODEL), BF16), jax.ShapeDtypeStruct((s, D_MODEL), F32),
                   jax.ShapeDtypeStruct((s, D_MODEL), BF16)],
        compiler_params=_params(("arbitrary",), vmem=VMEM_LIMIT),
    )(z, z, y_attn, x, conv_w, g_conv_out, g_attn_out, g_ffn, w_out)


FF_CHUNK = 256


def _ffn_forward(h2, x2, target, w_gate_t, w_up_t, w_down, tm):
    s = h2.shape[0]

    def body(h_ref, x2_ref, t_ref, wg_ref, wu_ref, wd_ref, a_ref, b_ref, dy_ref, loss_ref, f_buf):
        @pl.when(pl.program_id(0) == 0)
        def _():
            loss_ref[...] = jnp.zeros_like(loss_ref)

        h = h_ref[...]
        for c0 in range(0, D_FF, FF_CHUNK):
            rows = slice(c0, c0 + FF_CHUNK)
            a = _dot(h, wg_ref[rows, :], NT)
            b = _dot(h, wu_ref[rows, :], NT)
            a_ref[:, rows] = a.astype(BF16)
            b_ref[:, rows] = b.astype(BF16)
            f_buf[:, rows] = ((a * jax.nn.sigmoid(a)) * b).astype(BF16)
        y = x2_ref[...] + _dot(f_buf[...], wd_ref[...], NN)
        err = y - t_ref[...]
        dy_ref[...] = err * (1.0 / D_MODEL)
        loss_ref[...] += jnp.sum(err * err)

    tile = lambda w: pl.BlockSpec((tm, w), lambda i: (i, 0))
    return pl.pallas_call(
        body, name="ffn_forward", grid=(s // tm,),
        in_specs=[tile(D_MODEL), tile(D_MODEL), tile(D_MODEL)] + [_const_spec((D_FF, D_MODEL))] * 3,
        out_specs=[tile(D_FF), tile(D_FF), tile(D_MODEL), pl.BlockSpec((8, 128), lambda i: (0, 0))],
        out_shape=[jax.ShapeDtypeStruct((s, D_FF), BF16), jax.ShapeDtypeStruct((s, D_FF), BF16),
                   jax.ShapeDtypeStruct((s, D_MODEL), F32), jax.ShapeDtypeStruct((8, 128), F32)],
        scratch_shapes=[pltpu.VMEM((tm, D_FF), BF16)],
        compiler_params=_params(("arbitrary",), vmem=VMEM_LIMIT),
    )(h2, x2, target, w_gate_t, w_up_t, w_down)


def _ffn_backward(dy, a, b, x2, g_ffn, w_gate_t, w_up_t, w_down, tm):
    s = dy.shape[0]
    half = D_FF // 2
    pieces = [(p0, min(FF_CHUNK, half - p0)) for p0 in range(0, half, FF_CHUNK)]

    def body(dy_ref, a_ref, b_ref, x2_ref, g_ref, wg_ref, wu_ref, wd_ref, da_ref, db_ref, dx2_ref, dg_ref, acc):
        i, c = pl.program_id(0), pl.program_id(1)

        @pl.when((i == 0) & (c == 0))
        def _():
            dg_ref[...] = jnp.zeros_like(dg_ref)

        dyb = dy_ref[...].astype(BF16)
        for p0, width in pieces:
            rows = slice(p0, p0 + width)
            df = _dot(dyb, wd_ref[rows, :], NT)
            av = a_ref[:, rows].astype(F32)
            bv = b_ref[:, rows].astype(F32)
            sig = jax.nn.sigmoid(av)
            silu = av * sig
            da_ref[:, rows] = (df * bv * (sig * (1.0 + av * (1.0 - sig)))).astype(BF16)
            db_ref[:, rows] = (df * silu).astype(BF16)
        dh = _dot(da_ref[...], wg_ref[...], NN) + _dot(db_ref[...], wu_ref[...], NN)

        @pl.when(c == 0)
        def _():
            acc[...] = dh

        @pl.when(c == 1)
        def _():
            dh_all = acc[...] + dh
            xh, r = _rms(x2_ref[...])
            dg_ref[...] += jnp.sum(dh_all * xh, axis=0, keepdims=True)
            dx2_ref[...] = dy_ref[...] + _rms_bwd(dh_all, xh, r, g_ref[...])

    tile = lambda w: pl.BlockSpec((tm, w), lambda i, c: (i, 0))
    part = pl.BlockSpec((tm, half), lambda i, c: (i, c))
    weight = pl.BlockSpec((half, D_MODEL), lambda i, c: (c, 0))
    return pl.pallas_call(
        body, name="ffn_backward", grid=(s // tm, 2),
        in_specs=[tile(D_MODEL), part, part, tile(D_MODEL), _const_spec((1, D_MODEL)), weight, weight, weight],
        out_specs=[part, part, tile(D_MODEL), pl.BlockSpec((1, D_MODEL), lambda i, c: (0, 0))],
        out_shape=[jax.ShapeDtypeStruct((s, D_FF), BF16)] * 2
        + [jax.ShapeDtypeStruct((s, D_MODEL), F32), jax.ShapeDtypeStruct((1, D_MODEL), F32)],
        scratch_shapes=[pltpu.VMEM((tm, D_MODEL), F32)],
        compiler_params=_params(("arbitrary", "arbitrary"), vmem=VMEM_LIMIT),
    )(dy, a, b, x2, g_ffn, w_gate_t, w_up_t, w_down)


def _out_backward(dx2, z, y_attn, conv_w, g_conv_out, g_attn_out, w_out, tm):
    s = dx2.shape[0]
    hb = tm // 8

    def body(dx2_ref, z_ref, zp_ref, ya_ref, cw_ref, gc_ref, ga_ref, w_ref, dya_ref, dgb_ref, dcv_ref, dgc_ref, dga_ref):
        first = pl.program_id(0) == 0

        @pl.when(first)
        def _():
            dgc_ref[...] = jnp.zeros_like(dgc_ref)
            dga_ref[...] = jnp.zeros_like(dga_ref)

        dmix = _dot(dx2_ref[...].astype(BF16), w_ref[...], NT)
        _, gb, _, _, _, _, cv = _conv_forward(z_ref[...].astype(F32), zp_ref[...].astype(F32), cw_ref[...], first)
        ych, rc = _rms(gb * cv)
        dnc = dmix[:, :D_CONV]
        dgc_ref[...] += jnp.sum(dnc * ych, axis=0, keepdims=True)
        dyc = _rms_bwd(dnc, ych, rc, gc_ref[...])
        dgb_ref[...] = (dyc * cv).astype(BF16)
        dcv_ref[...] = dyc * gb
        yah, ra = _rms(ya_ref[...])
        dna = dmix[:, D_CONV:]
        dga_ref[...] += jnp.sum(dna * yah, axis=0, keepdims=True)
        dya_ref[...] = _rms_bwd(dna, yah, ra, ga_ref[...])

    tile = lambda w: pl.BlockSpec((tm, w), lambda i: (i, 0))
    vec = pl.BlockSpec((1, D_CONV), lambda i: (0, 0))
    return pl.pallas_call(
        body, name="out_backward", grid=(s // tm,),
        in_specs=[tile(D_MODEL), tile(3 * D_CONV), pl.BlockSpec((8, 3 * D_CONV), lambda i: (jnp.maximum(i * hb - 1, 0), 0)),
                  tile(D_ATTN), _const_spec((8, D_CONV)), _const_spec((1, D_CONV)), _const_spec((1, D_ATTN)),
                  _const_spec((D_MODEL, D_MODEL))],
        out_specs=[tile(D_ATTN), tile(D_CONV), tile(D_CONV), vec, vec],
        out_shape=[jax.ShapeDtypeStruct((s, D_ATTN), F32), jax.ShapeDtypeStruct((s, D_CONV), BF16),
                   jax.ShapeDtypeStruct((s, D_CONV), F32), jax.ShapeDtypeStruct((1, D_CONV), F32),
                   jax.ShapeDtypeStruct((1, D_ATTN), F32)],
        compiler_params=_params(("arbitrary",), vmem=VMEM_LIMIT),
    )(dx2, z, z, y_attn, conv_w, g_conv_out, g_attn_out, w_out)


def _attention_backward(z, o, lse, dya, gq2, gk2, masks, grad_stacks):
    s = z.shape[0]
    seg = s // RESIDUES
    n_g = len(grad_stacks)

    def body(z_hbm, o_hbm, lse_hbm, do_hbm, gq_ref, gk_ref, mask_ref, *rest):
        stack_refs = rest[:n_g]
        dzq_ref, dzk_ref, dzv_ref, dgq_ref, dgk_ref = rest[n_g:n_g + 5]
        landing = rest[n_g + 5:2 * n_g + 5]
        (qf, kf, vf, dof, stats, dq, dk, dv, t0, t1, t2, zbuf, fbuf, sems, send_sems, recv_sems) = rest[2 * n_g + 5:]
        pair = pl.program_id(0)
        scatter = _DirectReduceScatter(stack_refs, landing, send_sems, recv_sems)

        @pl.when(pair == 0)
        def _():
            scatter.start()

        n_chunks = s // CHUNK
        bd_mean = _head_blockdiag(1.0 / HEAD_DIM)
        bd_sum = _head_blockdiag(1.0)
        gq = gq_ref[...] * (HEAD_DIM ** -0.5)
        gk = gk_ref[...]
        lane = lax.broadcasted_iota(jnp.int32, (CHUNK, 128), 1) % HEAD_DIM

        def chunk_copies(c, slot, n_z, n_f):
            rows = pl.ds(pl.multiple_of(c * CHUNK, CHUNK), CHUNK)
            cps = []
            for i, col in enumerate((12, 16, 20)[:n_z]):
                cols = pl.ds(pl.multiple_of((col + pair) * 128, 128), 128)
                cps.append(pltpu.make_async_copy(z_hbm.at[rows, cols], zbuf.at[slot, i], sems.at[slot, i]))
            for i, src in enumerate((do_hbm, o_hbm, lse_hbm)[:n_f]):
                cols = pl.ds(pl.multiple_of(pair * 128, 128), 128)
                cps.append(pltpu.make_async_copy(src.at[rows, cols], fbuf.at[slot, i], sems.at[slot, 3 + i]))
            return cps

        def prefetched(c, n_z, n_f):
            slot = c % 2

            @pl.when(c + 1 < n_chunks)
            def _():
                for cp in chunk_copies(c + 1, 1 - slot, n_z, n_f):
                    cp.start()

            for cp in chunk_copies(c, slot, n_z, n_f):
                cp.wait()
            return slot

        for cp in chunk_copies(0, 0, 3, 3):
            cp.start()

        def prep(c, carry):
            slot = prefetched(c, 3, 3)
            rows = pl.ds(pl.multiple_of(c * CHUNK, CHUNK), CHUNK)
            zq = zbuf[slot, 0].astype(F32)
            t0[...] = (zq * lax.rsqrt(_head_sum(zq * zq, bd_mean) + EPS)) * gq
            zk = zbuf[slot, 1].astype(F32)
            t1[...] = (zk * lax.rsqrt(_head_sum(zk * zk, bd_mean) + EPS)) * gk
            t2[...] = zbuf[slot, 2].astype(F32)
            _to_residue_major(qf, t0, c, seg)
            _to_residue_major(kf, t1, c, seg)
            _to_residue_major(vf, t2, c, seg)
            _to_residue_major(dof, fbuf.at[slot, 0], c, seg)
            delta = _head_sum(fbuf[slot, 0] * fbuf[slot, 1], bd_sum)
            t0[...] = jnp.where(lane == 0, fbuf[slot, 2], jnp.where(lane == 1, delta, 0.0))
            _to_residue_major(stats, t0, c, seg)
            zero = jnp.zeros((CHUNK, 128), F32)
            dq[rows, :] = zero
            dk[rows, :] = zero
            dv[rows, :] = zero
            return carry

        lax.fori_loop(0, n_chunks, prep, 0)

        head0 = lax.broadcasted_iota(jnp.int32, (BLOCK, 128), 1) < HEAD_DIM

        def run_branch(branch, d):
            n_runs, w, steps, group, chains = _branch_geometry(d, seg)
            slots = [(ci, b) for ci in range(chains) for b in range(group)]

            def step(u, grp):
                starts = [_run_starts(u + ci, grp, d, seg, n_runs, w, group) for ci in range(chains)]
                kt = [[_load_tile(kf, st, w, b).astype(BF16) for b in range(-1, group)] for st in starts]
                vt = [[_load_tile(vf, st, w, b).astype(BF16) for b in range(-1, group)] for st in starts]
                dk_t = [[jnp.zeros((BLOCK, 128), F32)] * (group + 1) for _ in range(chains)]
                dv_t = [[jnp.zeros((BLOCK, 128), F32)] * (group + 1) for _ in range(chains)]
                dq_t = []
                q2s, do2s, kcs, scores, dps, lses, deltas = [], [], [], [], [], [], []
                for ci, b in slots:
                    q2 = _stack_heads(_load_tile(qf, starts[ci], w, b), head0)
                    do2 = _stack_heads(_load_tile(dof, starts[ci], w, b), head0)
                    st = _load_tile(stats, starts[ci], w, b)
                    lses += [st[:, 0:1], st[:, HEAD_DIM:HEAD_DIM + 1]]
                    deltas += [st[:, 1:2], st[:, HEAD_DIM + 1:HEAD_DIM + 2]]
                    kc = jnp.concatenate([kt[ci][b], kt[ci][b + 1]], axis=0)
                    vc = jnp.concatenate([vt[ci][b], vt[ci][b + 1]], axis=0)
                    mask = mask_ref[branch, 0]
                    if b == 0:
                        mask = jnp.where(grp == 0, mask_ref[branch, 1], mask)
                    scores.append(_dot(q2, kc, NT) + mask)
                    dps.append(_dot(do2, vc, NT))
                    q2s.append(q2)
                    do2s.append(do2)
                    kcs.append(kc)
                p_all = jnp.exp(jnp.concatenate(scores, axis=0) - jnp.concatenate(lses, axis=0))
                ds_all = (p_all * (jnp.concatenate(dps, axis=0) - jnp.concatenate(deltas, axis=0))).astype(BF16)
                p_all = p_all.astype(BF16)
                for n, (ci, b) in enumerate(slots):
                    rows = slice(2 * BLOCK * n, 2 * BLOCK * (n + 1))
                    ds, q2, do2, kc = ds_all[rows], q2s[n], do2s[n], kcs[n]
                    dq2 = _dot(ds, kc, NN)
                    dq_t.append(jnp.where(head0, dq2[:BLOCK], dq2[BLOCK:]))
                    dkc = _dot(ds, q2, TN)
                    dvc = _dot(p_all[rows], do2, TN)
                    dk_t[ci][b] = dk_t[ci][b] + dkc[:BLOCK]
                    dk_t[ci][b + 1] = dk_t[ci][b + 1] + dkc[BLOCK:]
                    dv_t[ci][b] = dv_t[ci][b] + dvc[:BLOCK]
                    dv_t[ci][b + 1] = dv_t[ci][b + 1] + dvc[BLOCK:]
                for n, (ci, b) in enumerate(slots):
                    _add_tile(dq, starts[ci], w, b, dq_t[n])
                for ci in range(chains):
                    for b in range(-1, group):
                        _add_tile(dk, starts[ci], w, b, dk_t[ci][b + 1])
                        _add_tile(dv, starts[ci], w, b, dv_t[ci][b + 1])

            def unit(it, carry):
                def inner(grp, carry2):
                    step(it * chains, grp)
                    return carry2
                return lax.fori_loop(0, steps, inner, carry)

            lax.fori_loop(0, d // chains, unit, 0)

        for branch, d in enumerate(DILATIONS):
            run_branch(branch, d)

        for cp in chunk_copies(0, 0, 2, 0):
            cp.start()

        def finish(c, carry):
            acc_q, acc_k = carry
            rows = pl.ds(pl.multiple_of(c * CHUNK, CHUNK), CHUNK)
            _from_residue_major(t0, 0, dq, c, seg)
            _from_residue_major(t1, 0, dk, c, seg)
            _from_residue_major(t2, 0, dv, c, seg)
            slot = prefetched(c, 2, 0)
            zq = zbuf[slot, 0].astype(F32)
            rq = lax.rsqrt(_head_sum(zq * zq, bd_mean) + EPS)
            qh = zq * rq
            dqn = t0[...]
            acc_q = acc_q + jnp.sum(dqn * qh, axis=0, keepdims=True)
            t = dqn * gq
            dzq_ref[rows, :] = (rq * (t - qh * _head_sum(t * qh, bd_mean))).astype(BF16)
            zk = zbuf[slot, 1].astype(F32)
            rk = lax.rsqrt(_head_sum(zk * zk, bd_mean) + EPS)
            kh = zk * rk
            dkn = t1[...]
            acc_k = acc_k + jnp.sum(dkn * kh, axis=0, keepdims=True)
            t = dkn * gk
            dzk_ref[rows, :] = (rk * (t - kh * _head_sum(t * kh, bd_mean))).astype(BF16)
            dzv_ref[rows, :] = t2[...].astype(BF16)
            return acc_q, acc_k

        zero = jnp.zeros((1, 128), F32)
        acc_q, acc_k = lax.fori_loop(0, s // CHUNK, finish, (zero, zero))
        dgq_ref[...] = acc_q * (HEAD_DIM ** -0.5)
        dgk_ref[...] = acc_k

        @pl.when(pair == N_PAIRS - 1)
        def _():
            scatter.finish()

    hbm = pl.BlockSpec(memory_space=pl.ANY)
    gain = pl.BlockSpec((None, 1, 128), lambda p: (p, 0, 0))
    dz_spec = pl.BlockSpec((s, 128), lambda p: (0, p))
    out = pl.pallas_call(
        body, name="attention_backward", grid=(N_PAIRS,),
        in_specs=[hbm, hbm, hbm, hbm, _const_spec((1, 128)), _const_spec((1, 128)), _const_spec(masks.shape)] + [hbm] * n_g,
        out_specs=[dz_spec, dz_spec, dz_spec, gain, gain] + [hbm] * n_g,
        out_shape=[jax.ShapeDtypeStruct((s, D_ATTN), BF16)] * 3 + [jax.ShapeDtypeStruct((N_PAIRS, 1, 128), F32)] * 2
        + [jax.ShapeDtypeStruct((N_DEV - 1,) + g.shape[1:], g.dtype) for g in grad_stacks],
        scratch_shapes=[pltpu.VMEM((s, 128), F32)] * 8 + [pltpu.VMEM((CHUNK, 128), F32)] * 3
        + [pltpu.VMEM((2, 3, CHUNK, 128), BF16), pltpu.VMEM((2, 3, CHUNK, 128), F32), pltpu.SemaphoreType.DMA((2, 6)),
           pltpu.SemaphoreType.DMA((7 * n_g,)), pltpu.SemaphoreType.DMA((7 * n_g,))],
        compiler_params=_params(("arbitrary",), vmem=VMEM_LIMIT),
    )(z, o, lse, dya, gq2, gk2, masks, *grad_stacks)
    return out[:5], out[5:]


def _in_backward(dcv, dgb, dzq, dzk, dzv, z, x, dx2, g_mix, conv_w, w_in_t, tm):
    s = x.shape[0]
    hb = tm // 8
    last_halo = s // 8 - 1

    def body(dcv_ref, dcn_ref, dgb_ref, dzq_ref, dzk_ref, dzv_ref, z_ref, zp_ref, x_ref, dx2_ref, g_ref, cw_ref, w_ref,
             dz_ref, dx_ref, dg_ref, dcw_ref):
        i = pl.program_id(0)
        first = i == 0
        last = i == pl.num_programs(0) - 1

        @pl.when(first)
        def _():
            dg_ref[...] = jnp.zeros_like(dg_ref)
            dcw_ref[...] = jnp.zeros_like(dcw_ref)

        w = cw_ref[...]
        u, _, gc, cu, cu1, cu2, _ = _conv_forward(z_ref[...].astype(F32), zp_ref[...].astype(F32), w, first)
        dcv_t = dcv_ref[...]
        nxt = jnp.where(last, 0.0, dcn_ref[...])
        row = lax.broadcasted_iota(jnp.int32, dcv_t.shape, 0)
        up1 = jnp.where(row < tm - 1, pltpu.roll(dcv_t, tm - 1, 0), nxt[0:1, :])
        up2 = jnp.where(row < tm - 2, pltpu.roll(dcv_t, tm - 2, 0), jnp.where(row == tm - 2, nxt[0:1, :], nxt[1:2, :]))
        dcu = w[2:3, :] * dcv_t + w[1:2, :] * up1 + w[0:1, :] * up2
        dcw = jnp.concatenate([jnp.sum(dcv_t * cu2, axis=0, keepdims=True), jnp.sum(dcv_t * cu1, axis=0, keepdims=True),
                               jnp.sum(dcv_t * cu, axis=0, keepdims=True), jnp.zeros((5, D_CONV), F32)], axis=0)
        dcw_ref[...] += dcw
        dz_ref[:, :D_CONV] = (dcu * gc).astype(BF16)
        dz_ref[:, D_CONV:2 * D_CONV] = dgb_ref[...]
        dz_ref[:, 2 * D_CONV:3 * D_CONV] = (dcu * u).astype(BF16)
        dz_ref[:, 3 * D_CONV:3 * D_CONV + D_ATTN] = dzq_ref[...]
        dz_ref[:, 3 * D_CONV + D_ATTN:3 * D_CONV + 2 * D_ATTN] = dzk_ref[...]
        dz_ref[:, 3 * D_CONV + 2 * D_ATTN:] = dzv_ref[...]
        dh = _dot(dz_ref[...], w_ref[...], NN)
        xh, r = _rms(x_ref[...])
        dg_ref[...] += jnp.sum(dh * xh, axis=0, keepdims=True)
        dx_ref[...] = dx2_ref[...] + _rms_bwd(dh, xh, r, g_ref[...])

    tile = lambda w: pl.BlockSpec((tm, w), lambda i: (i, 0))
    return pl.pallas_call(
        body, name="in_backward", grid=(s // tm,),
        in_specs=[tile(D_CONV), pl.BlockSpec((8, D_CONV), lambda i: (jnp.minimum((i + 1) * hb, last_halo), 0)),
                  tile(D_CONV), tile(D_ATTN), tile(D_ATTN), tile(D_ATTN),
                  tile(3 * D_CONV), pl.BlockSpec((8, 3 * D_CONV), lambda i: (jnp.maximum(i * hb - 1, 0), 0)),
                  tile(D_MODEL), tile(D_MODEL), _const_spec((1, D_MODEL)), _const_spec((8, D_CONV)),
                  _const_spec((D_IN, D_MODEL))],
        out_specs=[tile(D_IN), tile(D_MODEL), pl.BlockSpec((1, D_MODEL), lambda i: (0, 0)),
                   pl.BlockSpec((8, D_CONV), lambda i: (0, 0))],
        out_shape=[jax.ShapeDtypeStruct((s, D_IN), BF16), jax.ShapeDtypeStruct((s, D_MODEL), F32),
                   jax.ShapeDtypeStruct((1, D_MODEL), F32), jax.ShapeDtypeStruct((8, D_CONV), F32)],
        compiler_params=_params(("arbitrary",), vmem=VMEM_LIMIT),
    )(dcv, dcv, dgb, dzq, dzk, dzv, z, z, x, dx2, g_mix, conv_w, w_in_t)


def _weight_grad(name, a, b, tn, tk, gate=None):
    s, n = a.shape
    steps = s // tk

    def body(*refs):
        gate_ref = refs[0] if gate is not None else None
        a_ref, b_ref, out_ref, acc = refs[1:] if gate is not None else refs
        k = pl.program_id(1)

        @pl.when(k == 0)
        def _():
            acc[...] = jnp.zeros_like(acc)

        lhs = a_ref[...]
        if gate is not None:
            gv = gate_ref[...].astype(F32)
            lhs = ((gv * jax.nn.sigmoid(gv)) * lhs.astype(F32)).astype(BF16)
        acc[...] += _dot(lhs, b_ref[...].astype(BF16), TN)

        @pl.when(k == steps - 1)
        def _():
            out_ref[...] = acc[...].astype(BF16)

    lhs_spec = pl.BlockSpec((tk, tn), lambda j, k: (k, j))
    return pl.pallas_call(
        body, name=name, grid=(n // tn, steps),
        in_specs=([lhs_spec] if gate is not None else []) + [lhs_spec, pl.BlockSpec((tk, D_MODEL), lambda j, k: (k, 0))],
        out_specs=pl.BlockSpec((tn, D_MODEL), lambda j, k: (j, 0)),
        out_shape=jax.ShapeDtypeStruct((n, D_MODEL), BF16),
        scratch_shapes=[pltpu.VMEM((tn, D_MODEL), F32)],
        compiler_params=_params(("parallel", "arbitrary"), vmem=VMEM_LIMIT),
    )(*([gate] if gate is not None else []), a, b)


def _adamw(name, w, g, m, v):
    def body(w_ref, g_ref, m_ref, v_ref, d_ref, nm_ref, nv_ref):
        gv = g_ref[...]
        nm = ADAM_B1 * m_ref[...] + (1.0 - ADAM_B1) * gv
        nv = ADAM_B2 * v_ref[...] + (1.0 - ADAM_B2) * (gv * gv)
        m_hat = nm / (1.0 - ADAM_B1 ** ADAM_STEP)
        v_hat = nv / (1.0 - ADAM_B2 ** ADAM_STEP)
        d_ref[...] = -ADAM_LR * (m_hat / (jnp.sqrt(v_hat) + ADAM_EPS) + ADAM_WD * w_ref[...])
        nm_ref[...] = nm
        nv_ref[...] = nv

    rows = w.shape[0]
    tr = 256 if rows % 256 == 0 else rows
    spec = pl.BlockSpec((tr, w.shape[1]), lambda i: (i, 0))
    return pl.pallas_call(
        body, name=name, grid=(rows // tr,),
        in_specs=[spec] * 4, out_specs=[spec] * 3,
        out_shape=[jax.ShapeDtypeStruct(w.shape, F32)] * 3,
        compiler_params=_params(("parallel",)),
    )(w, g, m, v)


def kernel(x, g_mix, w_in, conv_w, g_q, g_k, g_conv_out, g_attn_out, w_out, g_ffn, w_gate, w_up, w_down, loss_target, m_g_mix, m_w_in, m_conv_w, m_g_q, m_g_k, m_g_conv_out, m_g_attn_out, m_w_out, m_g_ffn, m_w_gate, m_w_up, m_w_down, v_g_mix, v_w_in, v_conv_w, v_g_q, v_g_k, v_g_conv_out, v_g_attn_out, v_w_out, v_g_ffn, v_w_gate, v_w_up, v_w_down):
    s = x.shape[1]
    tm = min(512, s)
    xs, target = x[0], loss_target[0]
    px, py, pc = lax.axis_index("x"), lax.axis_index("y"), lax.axis_index("c")
    device = 4 * px + 2 * py + pc

    conv_block = jnp.zeros((8, 128), F32).at[:3, :HEAD_DIM].set(conv_w[0])
    first, shards = [w_in[0].T, conv_block], [w_out[0], w_gate[0].T, w_up[0].T, w_down[0]]
    w_in_g, conv_g, *shards = _gather_first_weights(first, [BF16, F32], shards)
    w_in_t = w_in_g.reshape(D_IN, D_MODEL)
    conv_full = jnp.transpose(conv_g[:, :3, :HEAD_DIM], (1, 0, 2)).reshape(3, D_CONV)
    conv_full = jnp.concatenate([conv_full, jnp.zeros((5, D_CONV), F32)], axis=0)
    gq2 = jnp.concatenate([g_q, g_q], axis=-1)
    gk2 = jnp.concatenate([g_k, g_k], axis=-1)

    h1, z = _in_projection(xs, g_mix, w_in_t, tm)
    masks = _permuted_masks()
    y_attn, lse, gathered = _attention_forward(z, gq2, gk2, masks, shards)
    w_out_f = gathered[0].reshape(D_MODEL, D_MODEL)
    w_gate_t, w_up_t, w_down_f = [g.reshape(D_FF, D_MODEL) for g in gathered[1:]]
    mix, x2, h2 = _mix_out(z, y_attn, xs, conv_full, g_conv_out, g_attn_out, g_ffn, w_out_f, tm)
    a, b, dy, sq_err = _ffn_forward(h2, x2, target, w_gate_t, w_up_t, w_down_f, tm)

    tk = min(512, s)
    da, db, dx2, dg_ffn = _ffn_backward(dy, a, b, x2, g_ffn, w_gate_t, w_up_t, w_down_f, tm)
    dya, dgb, dcv, dg_conv_out, dg_attn_out = _out_backward(dx2, z, y_attn, conv_full, g_conv_out, g_attn_out, w_out_f, tm)
    early = [
        _weight_grad("grad_w_out", mix, dx2, D_MODEL, tk),
        _weight_grad("grad_w_gate", da, h2, D_FF // 2, tk),
        _weight_grad("grad_w_up", db, h2, D_FF // 2, tk),
        _weight_grad("grad_w_down", b, dy, D_FF // 2, tk, gate=a),
    ]
    early = [g.reshape(N_DEV, g.shape[0] // N_DEV, D_MODEL) for g in early]
    (dzq, dzk, dzv, dgq_pairs, dgk_pairs), landed = _attention_backward(z, y_attn, lse, dya, gq2, gk2, masks, early)
    dz, grad_x, dg_mix, dconv = _in_backward(dcv, dgb, dzq, dzk, dzv, z, xs, dx2, g_mix, conv_full, w_in_t, tm)
    dev = jnp.reshape(device, (1,)).astype(jnp.int32)
    gw_out, gw_gate_t, gw_up_t, gw_down = [
        _add_received("grad_sum_" + n, dev, g, r) for n, g, r in zip(("w_out", "w_gate", "w_up", "w_down"), early, landed)]

    late = [_weight_grad("grad_w_in", dz, h1, D_IN // 2, tk).reshape(4, 2, D_IN // N_DEV, D_MODEL)]
    from_sibling = _reduce_scatter_cores(late)
    core = jnp.reshape(pc, (1,)).astype(jnp.int32)
    partials = _add_core_partials(core, late, from_sibling)
    from_chips = _reduce_scatter_chips(partials)
    chip = jnp.reshape(2 * px + py, (1,)).astype(jnp.int32)
    (gw_in_t,) = _add_chip_partials(chip, partials, from_chips)
    gw_in = gw_in_t.T

    dg_q = jnp.sum(dgq_pairs.reshape(2 * N_PAIRS, HEAD_DIM), axis=0)
    dg_k = jnp.sum(dgk_pairs.reshape(2 * N_PAIRS, HEAD_DIM), axis=0)
    zeros = lambda n: jnp.zeros((n,), F32)
    small = jnp.stack([
        dg_mix[0], dg_ffn[0],
        jnp.concatenate([dg_conv_out[0], dg_attn_out[0]]),
        jnp.concatenate([dg_q, dg_k, zeros(D_MODEL - 2 * HEAD_DIM)]),
        jnp.concatenate([dconv[0], dconv[1]]),
        jnp.concatenate([dconv[2], zeros(D_CONV)]),
        jnp.concatenate([sq_err[0, :1], zeros(D_MODEL - 1)]),
        zeros(D_MODEL),
    ])
    total = _all_reduce_small(small)
    loss = total[6, 0] * (0.5 / D_MODEL)
    gg_mix, gg_ffn = total[0:1], total[1:2]
    gg_conv_out, gg_attn_out = total[2:3, :D_CONV], total[2:3, D_CONV:]
    gg_q, gg_k = total[3:4, :HEAD_DIM], total[3:4, HEAD_DIM:2 * HEAD_DIM]
    conv_total = jnp.stack([total[4, :D_CONV], total[4, D_CONV:], total[5, :D_CONV]])
    g_conv = lax.dynamic_slice(conv_total, (0, device * HEAD_DIM), (3, HEAD_DIM))

    names = ["g_mix", "w_in", "conv_w", "g_q", "g_k", "g_conv_out", "g_attn_out", "w_out", "g_ffn", "w_gate", "w_up", "w_down"]
    weights = [g_mix, w_in[0], conv_w[0], g_q, g_k, g_conv_out, g_attn_out, w_out[0], g_ffn, w_gate[0], w_up[0], w_down[0]]
    grad_list = [gg_mix, gw_in, g_conv, gg_q, gg_k, gg_conv_out, gg_attn_out, gw_out, gg_ffn, gw_gate_t, gw_up_t, gw_down]
    m_list = [m_g_mix, m_w_in[0], m_conv_w[0], m_g_q, m_g_k, m_g_conv_out, m_g_attn_out, m_w_out[0], m_g_ffn, m_w_gate[0], m_w_up[0], m_w_down[0]]
    v_list = [v_g_mix, v_w_in[0], v_conv_w[0], v_g_q, v_g_k, v_g_conv_out, v_g_attn_out, v_w_out[0], v_g_ffn, v_w_gate[0], v_w_up[0], v_w_down[0]]
    stacked = {"w_in", "conv_w", "w_out", "w_gate", "w_up", "w_down"}
    transposed = {"w_gate", "w_up"}
    out_g, out_d, out_m, out_v = [], [], [], []
    for name, w, g, m, v in zip(names, weights, grad_list, m_list, v_list):
        if name in transposed:
            w, m, v = w.T, m.T, v.T
        delta, new_m, new_v = _adamw("adamw_" + name, w, g, m, v)
        if name in transposed:
            g, delta, new_m, new_v = g.T, delta.T, new_m.T, new_v.T
        lift = (lambda t: t[None]) if name in stacked else (lambda t: t)
        out_g.append(lift(g))
        out_d.append(lift(delta))
        out_m.append(lift(new_m))
        out_v.append(lift(new_v))
    return (loss, grad_x[None], *out_g, *out_d, *out_m, *out_v)
```

```python
import jax
import jax.numpy as jnp
import numpy as np
from jax import lax
from jax.experimental import pallas as pl
from jax.experimental.pallas import tpu as pltpu

F32 = jnp.float32
BF16 = jnp.bfloat16

D_MODEL = 1024
D_CONV = 512
D_ATTN = 512
D_IN = 3072
D_FF = 2816
HEAD_DIM = 64
N_PAIRS = 4
BLOCK = 128
DILATIONS = (1, 4, 16)
N_DEV = 8
EPS = 1e-6
NEG = -1e30

ADAM_LR = 0.001
ADAM_B1 = 0.9
ADAM_B2 = 0.999
ADAM_EPS = 1e-08
ADAM_WD = 0.01
ADAM_STEP = 10

NT = (((1,), (1,)), ((), ()))
NN = (((1,), (0,)), ((), ()))
TN = (((0,), (0,)), ((), ()))
MESH = pl.DeviceIdType.MESH

VMEM_LIMIT = 56 * 1024 * 1024


def _params(semantics=None, vmem=None):
    kw = {}
    if semantics is not None:
        kw["dimension_semantics"] = semantics
    if vmem is not None:
        kw["vmem_limit_bytes"] = vmem
    return pltpu.CompilerParams(**kw)


def _dot(a, b, dims):
    return lax.dot_general(a, b, dims, preferred_element_type=F32)


def _const_spec(shape):
    n = len(shape)
    return pl.BlockSpec(shape, lambda *_: (0,) * n, pipeline_mode=pl.Buffered(1))


def _rms(x):
    r = lax.rsqrt(jnp.mean(x * x, axis=-1, keepdims=True) + EPS)
    return x * r, r


def _rms_bwd(dy, xh, r, g):
    t = dy * g
    return r * (t - xh * jnp.mean(t * xh, axis=-1, keepdims=True))


def _head_blockdiag(scale):
    i = lax.broadcasted_iota(jnp.int32, (128, 128), 0) // HEAD_DIM
    j = lax.broadcasted_iota(jnp.int32, (128, 128), 1) // HEAD_DIM
    return jnp.where(i == j, scale, 0.0).astype(BF16)


def _head_sum(x, bd):
    hi = x.astype(BF16)
    lo = (x - hi.astype(F32)).astype(BF16)
    return _dot(hi, bd, NN) + _dot(lo, bd, NN)


def _place():
    return lax.axis_index("x"), lax.axis_index("y"), lax.axis_index("c")


class _TwoLevelGather:
    def __init__(self, stacks, send_sems, recv_sems):
        self.stacks, self.send_sems, self.recv_sems = stacks, send_sems, recv_sems
        x, y, c = _place()
        self.c = c
        self.me, self.sibling = (x, y, c), (x, y, 1 - c)
        self.chips = [(x, 1 - y), (1 - x, y), (1 - x, 1 - y)]

    @staticmethod
    def index(p):
        return 4 * p[0] + 2 * p[1] + p[2]

    def copy(self, a, k, block, to, src=None):
        dst = self.stacks[a].at[self.index(block)]
        return pltpu.make_async_remote_copy(
            src_ref=dst if src is None else src, dst_ref=dst,
            send_sem=self.send_sems.at[7 * a + k], recv_sem=self.recv_sems.at[7 * a + k],
            device_id=to, device_id_type=MESH)

    def first(self, a, src=None):
        cps = [self.copy(a, 0, self.me, self.sibling, src)]
        return cps + [self.copy(a, 1 + j, self.me, (*chip, self.c), src) for j, chip in enumerate(self.chips)]

    def forwards(self, a):
        return [self.copy(a, 4 + j, (*chip, self.c), self.sibling) for j, chip in enumerate(self.chips)]

    def start(self, srcs=None):
        for a in range(len(self.stacks)):
            for cp in self.first(a, None if srcs is None else srcs[a]):
                cp.start()

    def forward(self):
        for a in range(len(self.stacks)):
            for j, (chip, fwd) in enumerate(zip(self.chips, self.forwards(a))):
                self.copy(a, 1 + j, (*chip, self.c), self.me).wait_recv()
                fwd.start()

    def finish(self):
        for a in range(len(self.stacks)):
            self.copy(a, 0, self.sibling, self.me).wait_recv()
            for j, chip in enumerate(self.chips):
                self.copy(a, 4 + j, (*chip, 1 - self.c), self.me).wait_recv()
        for a in range(len(self.stacks)):
            for cp in self.first(a) + self.forwards(a):
                cp.wait_send()


def _gather_first_weights(gathered_parts, gathered_dtypes, cast_parts):
    n, m = len(gathered_parts), len(cast_parts)

    def body(*refs):
        ins, casts_in = refs[:n], refs[n:n + m]
        outs, casts_out = refs[n + m:2 * n + m], refs[2 * n + m:2 * n + 2 * m]
        send_sems, recv_sems = refs[2 * n + 2 * m], refs[2 * n + 2 * m + 1]
        gather = _TwoLevelGather(outs, send_sems, recv_sems)
        for a in range(n):
            outs[a][gather.index(gather.me)] = ins[a][...].astype(outs[a].dtype)
        gather.start()
        for a in range(m):
            casts_out[a][...] = casts_in[a][...].astype(BF16)
        gather.forward()
        gather.finish()

    vm = pl.BlockSpec(memory_space=pltpu.VMEM)
    return pl.pallas_call(
        body, name="weight_all_gather",
        out_shape=[jax.ShapeDtypeStruct((N_DEV,) + p.shape, dt) for p, dt in zip(gathered_parts, gathered_dtypes)]
        + [jax.ShapeDtypeStruct(p.shape, BF16) for p in cast_parts],
        in_specs=[vm] * (n + m), out_specs=[vm] * (n + m),
        scratch_shapes=[pltpu.SemaphoreType.DMA((7 * n,)), pltpu.SemaphoreType.DMA((7 * n,))],
        compiler_params=_params(vmem=VMEM_LIMIT),
    )(*gathered_parts, *cast_parts)


class _DirectReduceScatter:
    def __init__(self, stacks, landing, send_sems, recv_sems):
        self.stacks, self.landing, self.send_sems, self.recv_sems = stacks, landing, send_sems, recv_sems
        self.place = _place()

    def copies(self):
        x, y, c = self.place
        cps = []
        for a in range(len(self.stacks)):
            for k in range(1, N_DEV):
                peer = (1 - x if k & 4 else x, 1 - y if k & 2 else y, 1 - c if k & 1 else c)
                cps.append(pltpu.make_async_remote_copy(
                    src_ref=self.stacks[a].at[4 * peer[0] + 2 * peer[1] + peer[2]], dst_ref=self.landing[a].at[k - 1],
                    send_sem=self.send_sems.at[7 * a + k - 1], recv_sem=self.recv_sems.at[7 * a + k - 1],
                    device_id=peer, device_id_type=MESH))
        return cps

    def start(self):
        for cp in self.copies():
            cp.start()

    def finish(self):
        for cp in self.copies():
            cp.wait_recv()
        for cp in self.copies():
            cp.wait_send()


def _all_reduce_small(v):
    def body(v_ref, out_ref, gathered, send_sems, recv_sems):
        x, y, c = _place()
        mine = 4 * x + 2 * y + c
        gathered[mine] = v_ref[...]

        def copy(k):
            peer = (1 - x if k & 4 else x, 1 - y if k & 2 else y, 1 - c if k & 1 else c)
            return pltpu.make_async_remote_copy(
                src_ref=gathered.at[mine], dst_ref=gathered.at[mine],
                send_sem=send_sems.at[k - 1], recv_sem=recv_sems.at[k - 1], device_id=peer, device_id_type=MESH)

        copies = [copy(k) for k in range(1, N_DEV)]
        for cp in copies:
            cp.start()
        for cp in copies:
            cp.wait_recv()
        for cp in copies:
            cp.wait_send()
        total = gathered[0]
        for d in range(1, N_DEV):
            total = total + gathered[d]
        out_ref[...] = total

    vm = pl.BlockSpec(memory_space=pltpu.VMEM)
    return pl.pallas_call(
        body, name="small_all_reduce",
        out_shape=jax.ShapeDtypeStruct(v.shape, F32),
        in_specs=[vm], out_specs=vm,
        scratch_shapes=[pltpu.VMEM((N_DEV,) + v.shape, F32),
                        pltpu.SemaphoreType.DMA((N_DEV - 1,)), pltpu.SemaphoreType.DMA((N_DEV - 1,))],
    )(v)


def _reduce_scatter_cores(grads):
    n = len(grads)

    def body(*refs):
        ins, outs = refs[:n], refs[n:2 * n]
        send_sems, recv_sems = refs[2 * n], refs[2 * n + 1]
        x, y, c = _place()
        copies = []
        for a in range(n):
            for k in range(4):
                copies.append(pltpu.make_async_remote_copy(
                    src_ref=ins[a].at[k, 1 - c], dst_ref=outs[a].at[k],
                    send_sem=send_sems.at[4 * a + k], recv_sem=recv_sems.at[4 * a + k],
                    device_id=(x, y, 1 - c), device_id_type=MESH))
        for cp in copies:
            cp.start()
        for cp in copies:
            cp.wait_recv()
        for cp in copies:
            cp.wait_send()

    hbm = pl.BlockSpec(memory_space=pl.ANY)
    return pl.pallas_call(
        body, name="grad_reduce_scatter_cores",
        out_shape=[jax.ShapeDtypeStruct((4,) + g.shape[2:], g.dtype) for g in grads],
        in_specs=[hbm] * n, out_specs=[hbm] * n,
        scratch_shapes=[pltpu.SemaphoreType.DMA((4 * n,)), pltpu.SemaphoreType.DMA((4 * n,))],
    )(*grads)


def _reduce_scatter_chips(partials):
    n = len(partials)

    def body(*refs):
        ins, outs = refs[:n], refs[n:2 * n]
        send_sems, recv_sems = refs[2 * n], refs[2 * n + 1]
        x, y, c = _place()
        copies = []
        for a in range(n):
            for j in (1, 2, 3):
                px = 1 - x if j & 2 else x
                py = 1 - y if j & 1 else y
                copies.append(pltpu.make_async_remote_copy(
                    src_ref=ins[a].at[2 * px + py], dst_ref=outs[a].at[j - 1],
                    send_sem=send_sems.at[3 * a + j - 1], recv_sem=recv_sems.at[3 * a + j - 1],
                    device_id=(px, py, c), device_id_type=MESH))
        for cp in copies:
            cp.start()
        for cp in copies:
            cp.wait_recv()
        for cp in copies:
            cp.wait_send()

    hbm = pl.BlockSpec(memory_space=pl.ANY)
    return pl.pallas_call(
        body, name="grad_reduce_scatter_chips",
        out_shape=[jax.ShapeDtypeStruct((3,) + p.shape[1:], p.dtype) for p in partials],
        in_specs=[hbm] * n, out_specs=[hbm] * n,
        scratch_shapes=[pltpu.SemaphoreType.DMA((3 * n,)), pltpu.SemaphoreType.DMA((3 * n,))],
    )(*partials)


def _add_core_partials(core, grads, received):
    n = len(grads)

    def body(core_ref, *refs):
        del core_ref
        gs, rs, outs = refs[:n], refs[n:2 * n], refs[2 * n:]
        for a in range(n):
            outs[a][...] = (gs[a][...].astype(F32) + rs[a][...].astype(F32)).astype(outs[a].dtype)

    in_specs = [pl.BlockSpec((None, None) + g.shape[2:], lambda k, cref: (k, cref[0], 0, 0)) for g in grads]
    in_specs += [pl.BlockSpec((None,) + r.shape[1:], lambda k, cref: (k, 0, 0)) for r in received]
    out_specs = [pl.BlockSpec((None,) + r.shape[1:], lambda k, cref: (k, 0, 0)) for r in received]
    return pl.pallas_call(
        body, name="grad_add_core_partials",
        grid_spec=pltpu.PrefetchScalarGridSpec(num_scalar_prefetch=1, grid=(4,), in_specs=in_specs, out_specs=out_specs),
        out_shape=[jax.ShapeDtypeStruct(r.shape, r.dtype) for r in received],
        compiler_params=_params(("arbitrary",)),
    )(core, *grads, *received)


def _add_chip_partials(chip, partials, received):
    n = len(partials)

    def body(chip_ref, *refs):
        del chip_ref
        ps, rs, outs = refs[:n], refs[n:2 * n], refs[2 * n:]
        for a in range(n):
            outs[a][...] = ((ps[a][...].astype(F32) + rs[a][0].astype(F32)) + rs[a][1].astype(F32)) + rs[a][2].astype(F32)

    in_specs = [pl.BlockSpec((None,) + p.shape[1:], lambda i, cref: (cref[0], 0, 0)) for p in partials]
    in_specs += [pl.BlockSpec(r.shape, lambda i, cref: (0, 0, 0)) for r in received]
    out_specs = [pl.BlockSpec(p.shape[1:], lambda i, cref: (0, 0)) for p in partials]
    return pl.pallas_call(
        body, name="grad_add_chip_partials",
        grid_spec=pltpu.PrefetchScalarGridSpec(num_scalar_prefetch=1, grid=(1,), in_specs=in_specs, out_specs=out_specs),
        out_shape=[jax.ShapeDtypeStruct(p.shape[1:], F32) for p in partials],
        compiler_params=_params(("arbitrary",), vmem=VMEM_LIMIT),
    )(chip, *partials, *received)


def _in_projection(x, g_mix, w_in_t, tm):
    s = x.shape[0]

    def body(x_ref, g_ref, w_ref, h_ref, z_ref):
        xh, _ = _rms(x_ref[...])
        h = (xh * g_ref[...]).astype(BF16)
        h_ref[...] = h
        for n0 in range(0, D_IN, 512):
            z_ref[:, n0:n0 + 512] = _dot(h, w_ref[n0:n0 + 512, :], NT).astype(BF16)

    return pl.pallas_call(
        body, name="in_projection", grid=(s // tm,),
        in_specs=[pl.BlockSpec((tm, D_MODEL), lambda i: (i, 0)), _const_spec((1, D_MODEL)), _const_spec((D_IN, D_MODEL))],
        out_specs=[pl.BlockSpec((tm, D_MODEL), lambda i: (i, 0)), pl.BlockSpec((tm, D_IN), lambda i: (i, 0))],
        out_shape=[jax.ShapeDtypeStruct((s, D_MODEL), BF16), jax.ShapeDtypeStruct((s, D_IN), BF16)],
        compiler_params=_params(("parallel",), vmem=VMEM_LIMIT),
    )(x, g_mix, w_in_t)


RESIDUES = 16
CHUNK = 512
GROUP = 8


def _branch_geometry(d, seg):
    n_runs = RESIDUES // d
    w = BLOCK // n_runs
    blocks = seg // w
    group = min(GROUP, blocks)
    chains = min(GROUP // group, d)
    return n_runs, w, blocks // group, group, chains


def _permuted_masks():
    out = []
    for d in DILATIONS:
        n_runs = RESIDUES // d
        w = BLOCK // n_runs
        p = np.arange(BLOCK)
        pos = (p % w) * n_runs + p // w
        dist = pos[:, None] - np.concatenate([pos - BLOCK, pos])[None, :]
        band = (dist >= 0) & (dist <= BLOCK)
        first = band & (np.arange(2 * BLOCK)[None, :] >= BLOCK)
        both = [np.where(m, 0.0, NEG).astype(np.float32) for m in (band, first)]
        out.append(np.stack([np.concatenate([m, m], axis=0) for m in both]))
    return jnp.asarray(np.stack(out))


def _run_starts(u, grp, d, seg, n_runs, w, group):
    return [(u + d * q) * seg + (w * group) * grp for q in range(n_runs)]


def _tile_rows(start, w, b):
    off = start + w * b
    if b < 0:
        off = jnp.maximum(off, 0)
    return pl.ds(pl.multiple_of(off, w), w)


def _load_tile(ref, starts, w, b):
    parts = [ref[_tile_rows(st, w, b), :] for st in starts]
    return parts[0] if len(parts) == 1 else jnp.concatenate(parts, axis=0)


def _store_tile(ref, starts, w, b, tile):
    for i, st in enumerate(starts):
        ref[_tile_rows(st, w, b), :] = tile[i * w:(i + 1) * w]


def _add_tile(ref, starts, w, b, tile):
    for i, st in enumerate(starts):
        ref[_tile_rows(st, w, b), :] += tile[i * w:(i + 1) * w]


def _to_residue_major(dst, src, c, seg):
    per = CHUNK // RESIDUES
    for r in range(RESIDUES):
        dst[pl.ds(pl.multiple_of(r * seg + c * per, per), per), :] = src[pl.ds(r, per, stride=RESIDUES), :]


def _from_residue_major(dst, dst_start, src, c, seg):
    per = CHUNK // RESIDUES
    for r in range(RESIDUES):
        dst[pl.ds(dst_start + r, per, stride=RESIDUES), :] = src[pl.ds(pl.multiple_of(r * seg + c * per, per), per), :]


def _stack_heads(t, head0):
    return jnp.concatenate([jnp.where(head0, t, 0.0), jnp.where(head0, 0.0, t)], axis=0).astype(BF16)


def _attention_forward(z, gq2, gk2, masks, shards):
    s = z.shape[0]
    seg = s // RESIDUES
    n_w = len(shards)

    def body(zq_ref, zk_ref, zv_ref, gq_ref, gk_ref, mask_ref, *rest):
        shard_refs, (o_ref, lse_ref), stacks = rest[:n_w], rest[n_w:n_w + 2], rest[n_w + 2:2 * n_w + 2]
        qf, kf, vf, o_st, l_st, tq, tk, tv, send_sems, recv_sems, local_sems = rest[2 * n_w + 2:]
        pair = pl.program_id(0)
        gather = _TwoLevelGather(stacks, send_sems, recv_sems)

        def own_copies():
            mine = gather.index(gather.me)
            return [pltpu.make_async_copy(shard_refs[a], stacks[a].at[mine], local_sems.at[a]) for a in range(n_w)]

        @pl.when(pair == 0)
        def _():
            for cp in own_copies():
                cp.start()
            gather.start(shard_refs)

        @pl.when(pair == 2)
        def _():
            gather.forward()

        bd = _head_blockdiag(1.0 / HEAD_DIM)
        gq = gq_ref[...] * (HEAD_DIM ** -0.5)
        gk = gk_ref[...]

        def prep(c, carry):
            rows = pl.ds(pl.multiple_of(c * CHUNK, CHUNK), CHUNK)
            zq = zq_ref[rows, :].astype(F32)
            zk = zk_ref[rows, :].astype(F32)
            tq[...] = (zq * lax.rsqrt(_head_sum(zq * zq, bd) + EPS)) * gq
            tk[...] = (zk * lax.rsqrt(_head_sum(zk * zk, bd) + EPS)) * gk
            tv[...] = zv_ref[rows, :].astype(F32)
            _to_residue_major(qf, tq, c, seg)
            _to_residue_major(kf, tk, c, seg)
            _to_residue_major(vf, tv, c, seg)
            return carry

        lax.fori_loop(0, s // CHUNK, prep, 0)

        head0 = lax.broadcasted_iota(jnp.int32, (BLOCK, 128), 1) < HEAD_DIM

        def run_branch(branch, d):
            n_runs, w, steps, group, chains = _branch_geometry(d, seg)
            slots = [(ci, b) for ci in range(chains) for b in range(group)]

            def step(u, grp):
                starts = [_run_starts(u + ci, grp, d, seg, n_runs, w, group) for ci in range(chains)]
                kt = [[_load_tile(kf, st, w, b).astype(BF16) for b in range(-1, group)] for st in starts]
                vt = [[_load_tile(vf, st, w, b).astype(BF16) for b in range(-1, group)] for st in starts]
                scores = []
                for ci, b in slots:
                    q2 = _stack_heads(_load_tile(qf, starts[ci], w, b), head0)
                    kc = jnp.concatenate([kt[ci][b], kt[ci][b + 1]], axis=0)
                    mask = mask_ref[branch, 0]
                    if b == 0:
                        mask = jnp.where(grp == 0, mask_ref[branch, 1], mask)
                    scores.append(_dot(q2, kc, NT) + mask)
                sc = jnp.concatenate(scores, axis=0)
                m = jnp.max(sc, axis=-1, keepdims=True)
                e = jnp.exp(sc - m).astype(BF16)
                ones = jnp.ones((BLOCK, 128), BF16)
                fresh = []
                for n, (ci, b) in enumerate(slots):
                    rows = slice(2 * BLOCK * n, 2 * BLOCK * (n + 1))
                    v_aug = jnp.concatenate([jnp.concatenate([vt[ci][b], ones], axis=1),
                                             jnp.concatenate([vt[ci][b + 1], ones], axis=1)], axis=0)
                    o2 = _dot(e[rows], v_aug, NN)
                    den = o2[:, 128:]
                    o2 = o2[:, :128] * (1.0 / den)
                    l2 = m[rows] + jnp.log(den)
                    fresh.append((jnp.where(head0, o2[:BLOCK], o2[BLOCK:]), jnp.where(head0, l2[:BLOCK], l2[BLOCK:])))
                for n, (ci, b) in enumerate(slots):
                    o_new, l_new = fresh[n]
                    if branch > 0:
                        l_old = _load_tile(l_st, starts[ci], w, b)
                        top = jnp.maximum(l_old, l_new)
                        e_old = jnp.exp(l_old - top)
                        e_new = jnp.exp(l_new - top)
                        tot = e_old + e_new
                        inv = 1.0 / tot
                        o_new = _load_tile(o_st, starts[ci], w, b) * (e_old * inv) + o_new * (e_new * inv)
                        l_new = top + jnp.log(tot)
                    _store_tile(o_st, starts[ci], w, b, o_new)
                    _store_tile(l_st, starts[ci], w, b, l_new)

            def unit(it, carry):
                def inner(grp, carry2):
                    step(it * chains, grp)
                    return carry2
                return lax.fori_loop(0, steps, inner, carry)

            lax.fori_loop(0, d // chains, unit, 0)

        for branch, d in enumerate(DILATIONS):
            run_branch(branch, d)

        def finish(c, carry):
            _from_residue_major(o_ref, c * CHUNK, o_st, c, seg)
            _from_residue_major(lse_ref, c * CHUNK, l_st, c, seg)
            return carry

        lax.fori_loop(0, s // CHUNK, finish, 0)

        @pl.when(pair == N_PAIRS - 1)
        def _():
            gather.finish()
            for cp in own_copies():
                cp.wait()

    def col(c0):
        return pl.BlockSpec((s, 128), lambda p: (0, c0 + p), pipeline_mode=pl.Buffered(1))

    hbm = pl.BlockSpec(memory_space=pl.ANY)
    out = pl.pallas_call(
        body, name="attention_forward", grid=(N_PAIRS,),
        in_specs=[col(12), col(16), col(20), _const_spec((1, 128)), _const_spec((1, 128)), _const_spec(masks.shape)]
        + [hbm] * n_w,
        out_specs=[pl.BlockSpec((s, 128), lambda p: (0, p)), pl.BlockSpec((s, 128), lambda p: (0, p))] + [hbm] * n_w,
        out_shape=[jax.ShapeDtypeStruct((s, D_ATTN), F32), jax.ShapeDtypeStruct((s, D_ATTN), F32)]
        + [jax.ShapeDtypeStruct((N_DEV,) + w.shape, w.dtype) for w in shards],
        scratch_shapes=[pltpu.VMEM((s, 128), F32)] * 5 + [pltpu.VMEM((CHUNK, 128), F32)] * 3
        + [pltpu.SemaphoreType.DMA((7 * n_w,)), pltpu.SemaphoreType.DMA((7 * n_w,)), pltpu.SemaphoreType.DMA((n_w,))],
        compiler_params=_params(("arbitrary",), vmem=VMEM_LIMIT),
    )(z, z, z, gq2, gk2, masks, *shards)
    return out[0], out[1], out[2:]


def _conv_forward(zt, zprev, w, first):
    u, gb, gc = zt[:, :D_CONV], zt[:, D_CONV:2 * D_CONV], zt[:, 2 * D_CONV:]
    cu = gc * u
    cu_prev = jnp.where(first, 0.0, zprev[:, 2 * D_CONV:] * zprev[:, :D_CONV])
    row = lax.broadcasted_iota(jnp.int32, cu.shape, 0)
    cu1 = jnp.where(row >= 1, pltpu.roll(cu, 1, 0), cu_prev[7:8, :])
    cu2 = jnp.where(row >= 2, pltpu.roll(cu, 2, 0), jnp.where(row == 1, cu_prev[7:8, :], cu_prev[6:7, :]))
    cv = w[0:1, :] * cu2 + w[1:2, :] * cu1 + w[2:3, :] * cu
    return u, gb, gc, cu, cu1, cu2, cv


def _mix_out(z, y_attn, x, conv_w, g_conv_out, g_attn_out, g_ffn, w_out, tm):
    s = x.shape[0]
    hb = tm // 8

    def body(z_ref, zp_ref, ya_ref, x_ref, cw_ref, gc_ref, ga_ref, gf_ref, w_ref, mix_ref, x2_ref, h2_ref):
        first = pl.program_id(0) == 0
        zt = z_ref[...].astype(F32)
        _, gb, _, _, _, _, cv = _conv_forward(zt, zp_ref[...].astype(F32), cw_ref[...], first)
        nc, _ = _rms(gb * cv)
        na, _ = _rms(ya_ref[...])
        mix = jnp.concatenate([nc * gc_ref[...], na * ga_ref[...]], axis=-1).astype(BF16)
        mix_ref[...] = mix
        x2 = x_ref[...] + _dot(mix, w_ref[...], NN)
        x2_ref[...] = x2
        xh, _ = _rms(x2)
        h2_ref[...] = (xh * gf_ref[...]).astype(BF16)

    tile = lambda w: pl.BlockSpec((tm, w), lambda i: (i, 0))
    return pl.pallas_call(
        body, name="mix_out", grid=(s // tm,),
        in_specs=[tile(3 * D_CONV), pl.BlockSpec((8, 3 * D_CONV), lambda i: (jnp.maximum(i * hb - 1, 0), 0)),
                  tile(D_ATTN), tile(D_MODEL), _const_spec((8, D_CONV)), _const_spec((1, D_CONV)), _const_spec((1, D_ATTN)),
                  _const_spec((1, D_MODEL)), _const_spec((D_MODEL, D_MODEL))],
        out_specs=[tile(D_MODEL)] * 3,
        out_shape=[jax.ShapeDtypeStruct((s, D_MODEL), BF16), jax.ShapeDtypeStruct((s, D_MODEL), F32),
                   jax.ShapeDtypeStruct((s, D_MODEL), BF16)],
        compiler_params=_params(("arbitrary",), vmem=VMEM_LIMIT),
    )(z, z, y_attn, x, conv_w, g_conv_out, g_attn_out, g_ffn, w_out)


FF_CHUNK = 256


def _ffn_forward(h2, x2, target, w_gate_t, w_up_t, w_down, tm):
    s = h2.shape[0]

    def body(h_ref, x2_ref, t_ref, wg_ref, wu_ref, wd_ref, a_ref, b_ref, dy_ref, loss_ref, f_buf):
        @pl.when(pl.program_id(0) == 0)
        def _():
            loss_ref[...] = jnp.zeros_like(loss_ref)

        h = h_ref[...]
        for c0 in range(0, D_FF, FF_CHUNK):
            rows = slice(c0, c0 + FF_CHUNK)
            a = _dot(h, wg_ref[rows, :], NT)
            b = _dot(h, wu_ref[rows, :], NT)
            a_ref[:, rows] = a.astype(BF16)
            b_ref[:, rows] = b.astype(BF16)
            f_buf[:, rows] = ((a * jax.nn.sigmoid(a)) * b).astype(BF16)
        y = x2_ref[...] + _dot(f_buf[...], wd_ref[...], NN)
        err = y - t_ref[...]
        dy_ref[...] = err * (1.0 / D_MODEL)
        loss_ref[...] += jnp.sum(err * err)

    tile = lambda w: pl.BlockSpec((tm, w), lambda i: (i, 0))
    return pl.pallas_call(
        body, name="ffn_forward", grid=(s // tm,),
        in_specs=[tile(D_MODEL), tile(D_MODEL), tile(D_MODEL)] + [_const_spec((D_FF, D_MODEL))] * 3,
        out_specs=[tile(D_FF), tile(D_FF), tile(D_MODEL), pl.BlockSpec((8, 128), lambda i: (0, 0))],
        out_shape=[jax.ShapeDtypeStruct((s, D_FF), BF16), jax.ShapeDtypeStruct((s, D_FF), BF16),
                   jax.ShapeDtypeStruct((s, D_MODEL), F32), jax.ShapeDtypeStruct((8, 128), F32)],
        scratch_shapes=[pltpu.VMEM((tm, D_FF), BF16)],
        compiler_params=_params(("arbitrary",), vmem=VMEM_LIMIT),
    )(h2, x2, target, w_gate_t, w_up_t, w_down)


def _ffn_backward(dy, a, b, x2, g_ffn, w_gate_t, w_up_t, w_down, tm):
    s = dy.shape[0]
    half = D_FF // 2
    pieces = [(p0, min(FF_CHUNK, half - p0)) for p0 in range(0, half, FF_CHUNK)]

    def body(dy_ref, a_ref, b_ref, x2_ref, g_ref, wg_ref, wu_ref, wd_ref, da_ref, db_ref, dx2_ref, dg_ref, acc):
        i, c = pl.program_id(0), pl.program_id(1)

        @pl.when((i == 0) & (c == 0))
        def _():
            dg_ref[...] = jnp.zeros_like(dg_ref)

        dyb = dy_ref[...].astype(BF16)
        for p0, width in pieces:
            rows = slice(p0, p0 + width)
            df = _dot(dyb, wd_ref[rows, :], NT)
            av = a_ref[:, rows].astype(F32)
            bv = b_ref[:, rows].astype(F32)
            sig = jax.nn.sigmoid(av)
            silu = av * sig
            da_ref[:, rows] = (df * bv * (sig * (1.0 + av * (1.0 - sig)))).astype(BF16)
            db_ref[:, rows] = (df * silu).astype(BF16)
        dh = _dot(da_ref[...], wg_ref[...], NN) + _dot(db_ref[...], wu_ref[...], NN)

        @pl.when(c == 0)
        def _():
            acc[...] = dh

        @pl.when(c == 1)
        def _():
            dh_all = acc[...] + dh
            xh, r = _rms(x2_ref[...])
            dg_ref[...] += jnp.sum(dh_all * xh, axis=0, keepdims=True)
            dx2_ref[...] = dy_ref[...] + _rms_bwd(dh_all, xh, r, g_ref[...])

    tile = lambda w: pl.BlockSpec((tm, w), lambda i, c: (i, 0))
    part = pl.BlockSpec((tm, half), lambda i, c: (i, c))
    weight = pl.BlockSpec((half, D_MODEL), lambda i, c: (c, 0))
    return pl.pallas_call(
        body, name="ffn_backward", grid=(s // tm, 2),
        in_specs=[tile(D_MODEL), part, part, tile(D_MODEL), _const_spec((1, D_MODEL)), weight, weight, weight],
        out_specs=[part, part, tile(D_MODEL), pl.BlockSpec((1, D_MODEL), lambda i, c: (0, 0))],
        out_shape=[jax.ShapeDtypeStruct((s, D_FF), BF16)] * 2
        + [jax.ShapeDtypeStruct((s, D_MODEL), F32), jax.ShapeDtypeStruct((1, D_MODEL), F32)],
        scratch_shapes=[pltpu.VMEM((tm, D_MODEL), F32)],
        compiler_params=_params(("arbitrary", "arbitrary"), vmem=VMEM_LIMIT),
    )(dy, a, b, x2, g_ffn, w_gate_t, w_up_t, w_down)


def _out_backward(dx2, z, y_attn, conv_w, g_conv_out, g_attn_out, w_out, tm):
    s = dx2.shape[0]
    hb = tm // 8

    def body(dx2_ref, z_ref, zp_ref, ya_ref, cw_ref, gc_ref, ga_ref, w_ref, dya_ref, dgb_ref, dcv_ref, dgc_ref, dga_ref):
        first = pl.program_id(0) == 0

        @pl.when(first)
        def _():
            dgc_ref[...] = jnp.zeros_like(dgc_ref)
            dga_ref[...] = jnp.zeros_like(dga_ref)

        dmix = _dot(dx2_ref[...].astype(BF16), w_ref[...], NT)
        _, gb, _, _, _, _, cv = _conv_forward(z_ref[...].astype(F32), zp_ref[...].astype(F32), cw_ref[...], first)
        ych, rc = _rms(gb * cv)
        dnc = dmix[:, :D_CONV]
        dgc_ref[...] += jnp.sum(dnc * ych, axis=0, keepdims=True)
        dyc = _rms_bwd(dnc, ych, rc, gc_ref[...])
        dgb_ref[...] = (dyc * cv).astype(BF16)
        dcv_ref[...] = dyc * gb
        yah, ra = _rms(ya_ref[...])
        dna = dmix[:, D_CONV:]
        dga_ref[...] += jnp.sum(dna * yah, axis=0, keepdims=True)
        dya_ref[...] = _rms_bwd(dna, yah, ra, ga_ref[...])

    tile = lambda w: pl.BlockSpec((tm, w), lambda i: (i, 0))
    vec = pl.BlockSpec((1, D_CONV), lambda i: (0, 0))
    return pl.pallas_call(
        body, name="out_backward", grid=(s // tm,),
        in_specs=[tile(D_MODEL), tile(3 * D_CONV), pl.BlockSpec((8, 3 * D_CONV), lambda i: (jnp.maximum(i * hb - 1, 0), 0)),
                  tile(D_ATTN), _const_spec((8, D_CONV)), _const_spec((1, D_CONV)), _const_spec((1, D_ATTN)),
                  _const_spec((D_MODEL, D_MODEL))],
        out_specs=[tile(D_ATTN), tile(D_CONV), tile(D_CONV), vec, vec],
        out_shape=[jax.ShapeDtypeStruct((s, D_ATTN), F32), jax.ShapeDtypeStruct((s, D_CONV), BF16),
                   jax.ShapeDtypeStruct((s, D_CONV), F32), jax.ShapeDtypeStruct((1, D_CONV), F32),
                   jax.ShapeDtypeStruct((1, D_ATTN), F32)],
        compiler_params=_params(("arbitrary",), vmem=VMEM_LIMIT),
    )(dx2, z, z, y_attn, conv_w, g_conv_out, g_attn_out, w_out)


def _attention_backward(z, o, lse, dya, gq2, gk2, masks, grad_stacks):
    s = z.shape[0]
    seg = s // RESIDUES
    n_g = len(grad_stacks)

    def body(z_hbm, o_hbm, lse_hbm, do_hbm, gq_ref, gk_ref, mask_ref, *rest):
        stack_refs = rest[:n_g]
        dzq_ref, dzk_ref, dzv_ref, dgq_ref, dgk_ref = rest[n_g:n_g + 5]
        landing = rest[n_g + 5:2 * n_g + 5]
        (qf, kf, vf, dof, stats, dq, dk, dv, t0, t1, t2, zbuf, fbuf, sems, send_sems, recv_sems) = rest[2 * n_g + 5:]
        pair = pl.program_id(0)
        scatter = _DirectReduceScatter(stack_refs, landing, send_sems, recv_sems)

        @pl.when(pair == 0)
        def _():
            scatter.start()

        n_chunks = s // CHUNK
        bd_mean = _head_blockdiag(1.0 / HEAD_DIM)
        bd_sum = _head_blockdiag(1.0)
        gq = gq_ref[...] * (HEAD_DIM ** -0.5)
        gk = gk_ref[...]
        lane = lax.broadcasted_iota(jnp.int32, (CHUNK, 128), 1) % HEAD_DIM

        def chunk_copies(c, slot, n_z, n_f):
            rows = pl.ds(pl.multiple_of(c * CHUNK, CHUNK), CHUNK)
            cps = []
            for i, col in enumerate((12, 16, 20)[:n_z]):
                cols = pl.ds(pl.multiple_of((col + pair) * 128, 128), 128)
                cps.append(pltpu.make_async_copy(z_hbm.at[rows, cols], zbuf.at[slot, i], sems.at[slot, i]))
            for i, src in enumerate((do_hbm, o_hbm, lse_hbm)[:n_f]):
                cols = pl.ds(pl.multiple_of(pair * 128, 128), 128)
                cps.append(pltpu.make_async_copy(src.at[rows, cols], fbuf.at[slot, i], sems.at[slot, 3 + i]))
            return cps

        def prefetched(c, n_z, n_f):
            slot = c % 2

            @pl.when(c + 1 < n_chunks)
            def _():
                for cp in chunk_copies(c + 1, 1 - slot, n_z, n_f):
                    cp.start()

            for cp in chunk_copies(c, slot, n_z, n_f):
                cp.wait()
            return slot

        for cp in chunk_copies(0, 0, 3, 3):
            cp.start()

        def prep(c, carry):
            slot = prefetched(c, 3, 3)
            rows = pl.ds(pl.multiple_of(c * CHUNK, CHUNK), CHUNK)
            zq = zbuf[slot, 0].astype(F32)
            t0[...] = (zq * lax.rsqrt(_head_sum(zq * zq, bd_mean) + EPS)) * gq
            zk = zbuf[slot, 1].astype(F32)
            t1[...] = (zk * lax.rsqrt(_head_sum(zk * zk, bd_mean) + EPS)) * gk
            t2[...] = zbuf[slot, 2].astype(F32)
            _to_residue_major(qf, t0, c, seg)
            _to_residue_major(kf, t1, c, seg)
            _to_residue_major(vf, t2, c, seg)
            _to_residue_major(dof, fbuf.at[slot, 0], c, seg)
            delta = _head_sum(fbuf[slot, 0] * fbuf[slot, 1], bd_sum)
            t0[...] = jnp.where(lane == 0, fbuf[slot, 2], jnp.where(lane == 1, delta, 0.0))
            _to_residue_major(stats, t0, c, seg)
            zero = jnp.zeros((CHUNK, 128), F32)
            dq[rows, :] = zero
            dk[rows, :] = zero
            dv[rows, :] = zero
            return carry

        lax.fori_loop(0, n_chunks, prep, 0)

        head0 = lax.broadcasted_iota(jnp.int32, (BLOCK, 128), 1) < HEAD_DIM

        def run_branch(branch, d):
            n_runs, w, steps, group, chains = _branch_geometry(d, seg)
            slots = [(ci, b) for ci in range(chains) for b in range(group)]

            def step(u, grp):
                starts = [_run_starts(u + ci, grp, d, seg, n_runs, w, group) for ci in range(chains)]
                kt = [[_load_tile(kf, st, w, b).astype(BF16) for b in range(-1, group)] for st in starts]
                vt = [[_load_tile(vf, st, w, b).astype(BF16) for b in range(-1, group)] for st in starts]
                dk_t = [[jnp.zeros((BLOCK, 128), F32)] * (group + 1) for _ in range(chains)]
                dv_t = [[jnp.zeros((BLOCK, 128), F32)] * (group + 1) for _ in range(chains)]
                dq_t = []
                q2s, do2s, kcs, scores, dps, lses, deltas = [], [], [], [], [], [], []
                for ci, b in slots:
                    q2 = _stack_heads(_load_tile(qf, starts[ci], w, b), head0)
                    do2 = _stack_heads(_load_tile(dof, starts[ci], w, b), head0)
                    st = _load_tile(stats, starts[ci], w, b)
                    lses += [st[:, 0:1], st[:, HEAD_DIM:HEAD_DIM + 1]]
                    deltas += [st[:, 1:2], st[:, HEAD_DIM + 1:HEAD_DIM + 2]]
                    kc = jnp.concatenate([kt[ci][b], kt[ci][b + 1]], axis=0)
                    vc = jnp.concatenate([vt[ci][b], vt[ci][b + 1]], axis=0)
                    mask = mask_ref[branch, 0]
                    if b == 0:
                        mask = jnp.where(grp == 0, mask_ref[branch, 1], mask)
                    scores.append(_dot(q2, kc, NT) + mask)
                    dps.append(_dot(do2, vc, NT))
                    q2s.append(q2)
                    do2s.append(do2)
                    kcs.append(kc)
                p_all = jnp.exp(jnp.concatenate(scores, axis=0) - jnp.concatenate(lses, axis=0))
                ds_all = (p_all * (jnp.concatenate(dps, axis=0) - jnp.concatenate(deltas, axis=0))).astype(BF16)
                p_all = p_all.astype(BF16)
                for n, (ci, b) in enumerate(slots):
                    rows = slice(2 * BLOCK * n, 2 * BLOCK * (n + 1))
                    ds, q2, do2, kc = ds_all[rows], q2s[n], do2s[n], kcs[n]
                    dq2 = _dot(ds, kc, NN)
                    dq_t.append(jnp.where(head0, dq2[:BLOCK], dq2[BLOCK:]))
                    dkc = _dot(ds, q2, TN)
                    dvc = _dot(p_all[rows], do2, TN)
                    dk_t[ci][b] = dk_t[ci][b] + dkc[:BLOCK]
                    dk_t[ci][b + 1] = dk_t[ci][b + 1] + dkc[BLOCK:]
                    dv_t[ci][b] = dv_t[ci][b] + dvc[:BLOCK]
                    dv_t[ci][b + 1] = dv_t[ci][b + 1] + dvc[BLOCK:]
                for n, (ci, b) in enumerate(slots):
                    _add_tile(dq, starts[ci], w, b, dq_t[n])
                for ci in range(chains):
                    for b in range(-1, group):
                        _add_tile(dk, starts[ci], w, b, dk_t[ci][b + 1])
                        _add_tile(dv, starts[ci], w, b, dv_t[ci][b + 1])

            def unit(it, carry):
                def inner(grp, carry2):
                    step(it * chains, grp)
                    return carry2
                return lax.fori_loop(0, steps, inner, carry)

            lax.fori_loop(0, d // chains, unit, 0)

        for branch, d in enumerate(DILATIONS):
            run_branch(branch, d)

        for cp in chunk_copies(0, 0, 2, 0):
            cp.start()

        def finish(c, carry):
            acc_q, acc_k = carry
            rows = pl.ds(pl.multiple_of(c * CHUNK, CHUNK), CHUNK)
            _from_residue_major(t0, 0, dq, c, seg)
            _from_residue_major(t1, 0, dk, c, seg)
            _from_residue_major(t2, 0, dv, c, seg)
            slot = prefetched(c, 2, 0)
            zq = zbuf[slot, 0].astype(F32)
            rq = lax.rsqrt(_head_sum(zq * zq, bd_mean) + EPS)
            qh = zq * rq
            dqn = t0[...]
            acc_q = acc_q + jnp.sum(dqn * qh, axis=0, keepdims=True)
            t = dqn * gq
            dzq_ref[rows, :] = (rq * (t - qh * _head_sum(t * qh, bd_mean))).astype(BF16)
            zk = zbuf[slot, 1].astype(F32)
            rk = lax.rsqrt(_head_sum(zk * zk, bd_mean) + EPS)
            kh = zk * rk
            dkn = t1[...]
            acc_k = acc_k + jnp.sum(dkn * kh, axis=0, keepdims=True)
            t = dkn * gk
            dzk_ref[rows, :] = (rk * (t - kh * _head_sum(t * kh, bd_mean))).astype(BF16)
            dzv_ref[rows, :] = t2[...].astype(BF16)
            return acc_q, acc_k

        zero = jnp.zeros((1, 128), F32)
        acc_q, acc_k = lax.fori_loop(0, s // CHUNK, finish, (zero, zero))
        dgq_ref[...] = acc_q * (HEAD_DIM ** -0.5)
        dgk_ref[...] = acc_k

        @pl.when(pair == N_PAIRS - 1)
        def _():
            scatter.finish()

    hbm = pl.BlockSpec(memory_space=pl.ANY)
    gain = pl.BlockSpec((None, 1, 128), lambda p: (p, 0, 0))
    dz_spec = pl.BlockSpec((s, 128), lambda p: (0, p))
    out = pl.pallas_call(
        body, name="attention_backward", grid=(N_PAIRS,),
        in_specs=[hbm, hbm, hbm, hbm, _const_spec((1, 128)), _const_spec((1, 128)), _const_spec(masks.shape)] + [hbm] * n_g,
        out_specs=[dz_spec, dz_spec, dz_spec, gain, gain] + [hbm] * n_g,
        out_shape=[jax.ShapeDtypeStruct((s, D_ATTN), BF16)] * 3 + [jax.ShapeDtypeStruct((N_PAIRS, 1, 128), F32)] * 2
        + [jax.ShapeDtypeStruct((N_DEV - 1,) + g.shape[1:], g.dtype) for g in grad_stacks],
        scratch_shapes=[pltpu.VMEM((s, 128), F32)] * 8 + [pltpu.VMEM((CHUNK, 128), F32)] * 3
        + [pltpu.VMEM((2, 3, CHUNK, 128), BF16), pltpu.VMEM((2, 3, CHUNK, 128), F32), pltpu.SemaphoreType.DMA((2, 6)),
           pltpu.SemaphoreType.DMA((7 * n_g,)), pltpu.SemaphoreType.DMA((7 * n_g,))],
        compiler_params=_params(("arbitrary",), vmem=VMEM_LIMIT),
    )(z, o, lse, dya, gq2, gk2, masks, *grad_stacks)
    return out[:5], out[5:]


def _in_backward(dcv, dgb, dzq, dzk, dzv, z, x, dx2, g_mix, conv_w, w_in_t, tm):
    s = x.shape[0]
    hb = tm // 8
    last_halo = s // 8 - 1

    def body(dcv_ref, dcn_ref, dgb_ref, dzq_ref, dzk_ref, dzv_ref, z_ref, zp_ref, x_ref, dx2_ref, g_ref, cw_ref, w_ref,
             dz_ref, dx_ref, dg_ref, dcw_ref):
        i = pl.program_id(0)
        first = i == 0
        last = i == pl.num_programs(0) - 1

        @pl.when(first)
        def _():
            dg_ref[...] = jnp.zeros_like(dg_ref)
            dcw_ref[...] = jnp.zeros_like(dcw_ref)

        w = cw_ref[...]
        u, _, gc, cu, cu1, cu2, _ = _conv_forward(z_ref[...].astype(F32), zp_ref[...].astype(F32), w, first)
        dcv_t = dcv_ref[...]
        nxt = jnp.where(last, 0.0, dcn_ref[...])
        row = lax.broadcasted_iota(jnp.int32, dcv_t.shape, 0)
        up1 = jnp.where(row < tm - 1, pltpu.roll(dcv_t, tm - 1, 0), nxt[0:1, :])
        up2 = jnp.where(row < tm - 2, pltpu.roll(dcv_t, tm - 2, 0), jnp.where(row == tm - 2, nxt[0:1, :], nxt[1:2, :]))
        dcu = w[2:3, :] * dcv_t + w[1:2, :] * up1 + w[0:1, :] * up2
        dcw = jnp.concatenate([jnp.sum(dcv_t * cu2, axis=0, keepdims=True), jnp.sum(dcv_t * cu1, axis=0, keepdims=True),
                               jnp.sum(dcv_t * cu, axis=0, keepdims=True), jnp.zeros((5, D_CONV), F32)], axis=0)
        dcw_ref[...] += dcw
        dz_ref[:, :D_CONV] = (dcu * gc).astype(BF16)
        dz_ref[:, D_CONV:2 * D_CONV] = dgb_ref[...]
        dz_ref[:, 2 * D_CONV:3 * D_CONV] = (dcu * u).astype(BF16)
        dz_ref[:, 3 * D_CONV:3 * D_CONV + D_ATTN] = dzq_ref[...]
        dz_ref[:, 3 * D_CONV + D_ATTN:3 * D_CONV + 2 * D_ATTN] = dzk_ref[...]
        dz_ref[:, 3 * D_CONV + 2 * D_ATTN:] = dzv_ref[...]
        dh = _dot(dz_ref[...], w_ref[...], NN)
        xh, r = _rms(x_ref[...])
        dg_ref[...] += jnp.sum(dh * xh, axis=0, keepdims=True)
        dx_ref[...] = dx2_ref[...] + _rms_bwd(dh, xh, r, g_ref[...])

    tile = lambda w: pl.BlockSpec((tm, w), lambda i: (i, 0))
    return pl.pallas_call(
        body, name="in_backward", grid=(s // tm,),
        in_specs=[tile(D_CONV), pl.BlockSpec((8, D_CONV), lambda i: (jnp.minimum((i + 1) * hb, last_halo), 0)),
                  tile(D_CONV), tile(D_ATTN), tile(D_ATTN), tile(D_ATTN),
                  tile(3 * D_CONV), pl.BlockSpec((8, 3 * D_CONV), lambda i: (jnp.maximum(i * hb - 1, 0), 0)),
                  tile(D_MODEL), tile(D_MODEL), _const_spec((1, D_MODEL)), _const_spec((8, D_CONV)),
                  _const_spec((D_IN, D_MODEL))],
        out_specs=[tile(D_IN), tile(D_MODEL), pl.BlockSpec((1, D_MODEL), lambda i: (0, 0)),
                   pl.BlockSpec((8, D_CONV), lambda i: (0, 0))],
        out_shape=[jax.ShapeDtypeStruct((s, D_IN), BF16), jax.ShapeDtypeStruct((s, D_MODEL), F32),
                   jax.ShapeDtypeStruct((1, D_MODEL), F32), jax.ShapeDtypeStruct((8, D_CONV), F32)],
        compiler_params=_params(("arbitrary",), vmem=VMEM_LIMIT),
    )(dcv, dcv, dgb, dzq, dzk, dzv, z, z, x, dx2, g_mix, conv_w, w_in_t)


def _weight_grad(name, a, b, tn, tk, gate=None):
    s, n = a.shape
    steps = s // tk

    def body(*refs):
        gate_ref = refs[0] if gate is not None else None
        a_ref, b_ref, out_ref, acc = refs[1:] if gate is not None else refs
        k = pl.program_id(1)

        @pl.when(k == 0)
        def _():
            acc[...] = jnp.zeros_like(acc)

        lhs = a_ref[...]
        if gate is not None:
            gv = gate_ref[...].astype(F32)
            lhs = ((gv * jax.nn.sigmoid(gv)) * lhs.astype(F32)).astype(BF16)
        acc[...] += _dot(lhs, b_ref[...].astype(BF16), TN)

        @pl.when(k == steps - 1)
        def _():
            out_ref[...] = acc[...].astype(BF16)

    lhs_spec = pl.BlockSpec((tk, tn), lambda j, k: (k, j))
    return pl.pallas_call(
        body, name=name, grid=(n // tn, steps),
        in_specs=([lhs_spec] if gate is not None else []) + [lhs_spec, pl.BlockSpec((tk, D_MODEL), lambda j, k: (k, 0))],
        out_specs=pl.BlockSpec((tn, D_MODEL), lambda j, k: (j, 0)),
        out_shape=jax.ShapeDtypeStruct((n, D_MODEL), BF16),
        scratch_shapes=[pltpu.VMEM((tn, D_MODEL), F32)],
        compiler_params=_params(("parallel", "arbitrary"), vmem=VMEM_LIMIT),
    )(*([gate] if gate is not None else []), a, b)


def _adamw_math(w, g, m, v):
    nm = ADAM_B1 * m + (1.0 - ADAM_B1) * g
    nv = ADAM_B2 * v + (1.0 - ADAM_B2) * (g * g)
    m_hat = nm / (1.0 - ADAM_B1 ** ADAM_STEP)
    v_hat = nv / (1.0 - ADAM_B2 ** ADAM_STEP)
    return -ADAM_LR * (m_hat / (jnp.sqrt(v_hat) + ADAM_EPS) + ADAM_WD * w), nm, nv


def _adamw_received(name, device, w, stack, received, m, v):
    def body(dev_ref, w_ref, own_ref, recv_ref, m_ref, v_ref, g_ref, d_ref, nm_ref, nv_ref):
        del dev_ref
        g = own_ref[...].astype(F32)
        for k in range(N_DEV - 1):
            g = g + recv_ref[k].astype(F32)
        g_ref[...] = g
        d_ref[...], nm_ref[...], nv_ref[...] = _adamw_math(w_ref[...], g, m_ref[...], v_ref[...])

    rows = stack.shape[1]
    full = pl.BlockSpec((rows, D_MODEL), lambda i, dref: (0, 0))
    return pl.pallas_call(
        body, name=name,
        grid_spec=pltpu.PrefetchScalarGridSpec(
            num_scalar_prefetch=1, grid=(1,),
            in_specs=[full, pl.BlockSpec((None, rows, D_MODEL), lambda i, dref: (dref[0], 0, 0)),
                      pl.BlockSpec(received.shape, lambda i, dref: (0, 0, 0)), full, full],
            out_specs=[full] * 4),
        out_shape=[jax.ShapeDtypeStruct((rows, D_MODEL), F32)] * 4,
        compiler_params=_params(("arbitrary",), vmem=VMEM_LIMIT),
    )(device, w, stack, received, m, v)


def _adamw(name, w, g, m, v):
    def body(w_ref, g_ref, m_ref, v_ref, d_ref, nm_ref, nv_ref):
        d_ref[...], nm_ref[...], nv_ref[...] = _adamw_math(w_ref[...], g_ref[...], m_ref[...], v_ref[...])

    rows = w.shape[0]
    tr = 256 if rows % 256 == 0 else rows
    spec = pl.BlockSpec((tr, w.shape[1]), lambda i: (i, 0))
    return pl.pallas_call(
        body, name=name, grid=(rows // tr,),
        in_specs=[spec] * 4, out_specs=[spec] * 3,
        out_shape=[jax.ShapeDtypeStruct(w.shape, F32)] * 3,
        compiler_params=_params(("parallel",)),
    )(w, g, m, v)


def kernel(x, g_mix, w_in, conv_w, g_q, g_k, g_conv_out, g_attn_out, w_out, g_ffn, w_gate, w_up, w_down, loss_target, m_g_mix, m_w_in, m_conv_w, m_g_q, m_g_k, m_g_conv_out, m_g_attn_out, m_w_out, m_g_ffn, m_w_gate, m_w_up, m_w_down, v_g_mix, v_w_in, v_conv_w, v_g_q, v_g_k, v_g_conv_out, v_g_attn_out, v_w_out, v_g_ffn, v_w_gate, v_w_up, v_w_down):
    s = x.shape[1]
    tm = min(512, s)
    xs, target = x[0], loss_target[0]
    px, py, pc = lax.axis_index("x"), lax.axis_index("y"), lax.axis_index("c")
    device = 4 * px + 2 * py + pc

    conv_block = jnp.zeros((8, 128), F32).at[:3, :HEAD_DIM].set(conv_w[0])
    first, shards = [w_in[0].T, conv_block], [w_out[0], w_gate[0].T, w_up[0].T, w_down[0]]
    w_in_g, conv_g, *shards = _gather_first_weights(first, [BF16, F32], shards)
    w_in_t = w_in_g.reshape(D_IN, D_MODEL)
    conv_full = jnp.transpose(conv_g[:, :3, :HEAD_DIM], (1, 0, 2)).reshape(3, D_CONV)
    conv_full = jnp.concatenate([conv_full, jnp.zeros((5, D_CONV), F32)], axis=0)
    gq2 = jnp.concatenate([g_q, g_q], axis=-1)
    gk2 = jnp.concatenate([g_k, g_k], axis=-1)

    h1, z = _in_projection(xs, g_mix, w_in_t, tm)
    masks = _permuted_masks()
    y_attn, lse, gathered = _attention_forward(z, gq2, gk2, masks, shards)
    w_out_f = gathered[0].reshape(D_MODEL, D_MODEL)
    w_gate_t, w_up_t, w_down_f = [g.reshape(D_FF, D_MODEL) for g in gathered[1:]]
    mix, x2, h2 = _mix_out(z, y_attn, xs, conv_full, g_conv_out, g_attn_out, g_ffn, w_out_f, tm)
    a, b, dy, sq_err = _ffn_forward(h2, x2, target, w_gate_t, w_up_t, w_down_f, tm)

    tk = min(1024, s)
    da, db, dx2, dg_ffn = _ffn_backward(dy, a, b, x2, g_ffn, w_gate_t, w_up_t, w_down_f, tm)
    dya, dgb, dcv, dg_conv_out, dg_attn_out = _out_backward(dx2, z, y_attn, conv_full, g_conv_out, g_attn_out, w_out_f, tm)
    early = [
        _weight_grad("grad_w_out", mix, dx2, D_MODEL, tk),
        _weight_grad("grad_w_gate", da, h2, D_FF // 2, tk),
        _weight_grad("grad_w_up", db, h2, D_FF // 2, tk),
        _weight_grad("grad_w_down", b, dy, D_FF // 2, tk, gate=a),
    ]
    early = [g.reshape(N_DEV, g.shape[0] // N_DEV, D_MODEL) for g in early]
    (dzq, dzk, dzv, dgq_pairs, dgk_pairs), landed = _attention_backward(z, y_attn, lse, dya, gq2, gk2, masks, early)
    dz, grad_x, dg_mix, dconv = _in_backward(dcv, dgb, dzq, dzk, dzv, z, xs, dx2, g_mix, conv_full, w_in_t, tm)
    dev = jnp.reshape(device, (1,)).astype(jnp.int32)
    scattered = dict(zip(("w_out", "w_gate", "w_up", "w_down"), zip(early, landed)))

    late = [_weight_grad("grad_w_in", dz, h1, D_IN // 2, tk).reshape(4, 2, D_IN // N_DEV, D_MODEL)]
    from_sibling = _reduce_scatter_cores(late)
    core = jnp.reshape(pc, (1,)).astype(jnp.int32)
    partials = _add_core_partials(core, late, from_sibling)
    from_chips = _reduce_scatter_chips(partials)
    chip = jnp.reshape(2 * px + py, (1,)).astype(jnp.int32)
    (gw_in_t,) = _add_chip_partials(chip, partials, from_chips)
    gw_in = gw_in_t.T

    dg_q = jnp.sum(dgq_pairs.reshape(2 * N_PAIRS, HEAD_DIM), axis=0)
    dg_k = jnp.sum(dgk_pairs.reshape(2 * N_PAIRS, HEAD_DIM), axis=0)
    zeros = lambda n: jnp.zeros((n,), F32)
    small = jnp.stack([
        dg_mix[0], dg_ffn[0],
        jnp.concatenate([dg_conv_out[0], dg_attn_out[0]]),
        jnp.concatenate([dg_q, dg_k, zeros(D_MODEL - 2 * HEAD_DIM)]),
        jnp.concatenate([dconv[0], dconv[1]]),
        jnp.concatenate([dconv[2], zeros(D_CONV)]),
        jnp.concatenate([sq_err[0, :1], zeros(D_MODEL - 1)]),
        zeros(D_MODEL),
    ])
    total = _all_reduce_small(small)
    loss = total[6, 0] * (0.5 / D_MODEL)
    gg_mix, gg_ffn = total[0:1], total[1:2]
    gg_conv_out, gg_attn_out = total[2:3, :D_CONV], total[2:3, D_CONV:]
    gg_q, gg_k = total[3:4, :HEAD_DIM], total[3:4, HEAD_DIM:2 * HEAD_DIM]
    conv_total = jnp.stack([total[4, :D_CONV], total[4, D_CONV:], total[5, :D_CONV]])
    g_conv = lax.dynamic_slice(conv_total, (0, device * HEAD_DIM), (3, HEAD_DIM))

    names = ["g_mix", "w_in", "conv_w", "g_q", "g_k", "g_conv_out", "g_attn_out", "w_out", "g_ffn", "w_gate", "w_up", "w_down"]
    weights = [g_mix, w_in[0], conv_w[0], g_q, g_k, g_conv_out, g_attn_out, w_out[0], g_ffn, w_gate[0], w_up[0], w_down[0]]
    grad_list = [gg_mix, gw_in, g_conv, gg_q, gg_k, gg_conv_out, gg_attn_out, None, gg_ffn, None, None, None]
    m_list = [m_g_mix, m_w_in[0], m_conv_w[0], m_g_q, m_g_k, m_g_conv_out, m_g_attn_out, m_w_out[0], m_g_ffn, m_w_gate[0], m_w_up[0], m_w_down[0]]
    v_list = [v_g_mix, v_w_in[0], v_conv_w[0], v_g_q, v_g_k, v_g_conv_out, v_g_attn_out, v_w_out[0], v_g_ffn, v_w_gate[0], v_w_up[0], v_w_down[0]]
    stacked = {"w_in", "conv_w", "w_out", "w_gate", "w_up", "w_down"}
    transposed = {"w_gate", "w_up"}
    out_g, out_d, out_m, out_v = [], [], [], []
    for name, w, g, m, v in zip(names, weights, grad_list, m_list, v_list):
        if name in transposed:
            w, m, v = w.T, m.T, v.T
        if name in scattered:
            g, delta, new_m, new_v = _adamw_received("adamw_" + name, dev, w, *scattered[name], m, v)
        else:
            delta, new_m, new_v = _adamw("adamw_" + name, w, g, m, v)
        if name in transposed:
            g, delta, new_m, new_v = g.T, delta.T, new_m.T, new_v.T
        lift = (lambda t: t[None]) if name in stacked else (lambda t: t)
        out_g.append(lift(g))
        out_d.append(lift(delta))
        out_m.append(lift(new_m))
        out_v.append(lift(new_v))
    return (loss, grad_x[None], *out_g, *out_d, *out_m, *out_v)
```

```python
import jax
import jax.numpy as jnp
import numpy as np
from jax import lax
from jax.experimental import pallas as pl
from jax.experimental.pallas import tpu as pltpu

F32 = jnp.float32
BF16 = jnp.bfloat16

D_MODEL = 1024
D_CONV = 512
D_ATTN = 512
D_IN = 3072
D_FF = 2816
HEAD_DIM = 64
N_PAIRS = 4
BLOCK = 128
DILATIONS = (1, 4, 16)
N_DEV = 8
EPS = 1e-6
NEG = -1e30

ADAM_LR = 0.001
ADAM_B1 = 0.9
ADAM_B2 = 0.999
ADAM_EPS = 1e-08
ADAM_WD = 0.01
ADAM_STEP = 10

NT = (((1,), (1,)), ((), ()))
NN = (((1,), (0,)), ((), ()))
TN = (((0,), (0,)), ((), ()))
MESH = pl.DeviceIdType.MESH

VMEM_LIMIT = 56 * 1024 * 1024


def _params(semantics=None, vmem=None):
    kw = {}
    if semantics is not None:
        kw["dimension_semantics"] = semantics
    if vmem is not None:
        kw["vmem_limit_bytes"] = vmem
    return pltpu.CompilerParams(**kw)


def _dot(a, b, dims):
    return lax.dot_general(a, b, dims, preferred_element_type=F32)


def _const_spec(shape):
    n = len(shape)
    return pl.BlockSpec(shape, lambda *_: (0,) * n, pipeline_mode=pl.Buffered(1))


def _rms(x):
    r = lax.rsqrt(jnp.mean(x * x, axis=-1, keepdims=True) + EPS)
    return x * r, r


def _rms_bwd(dy, xh, r, g):
    t = dy * g
    return r * (t - xh * jnp.mean(t * xh, axis=-1, keepdims=True))


def _head_blockdiag(scale):
    i = lax.broadcasted_iota(jnp.int32, (128, 128), 0) // HEAD_DIM
    j = lax.broadcasted_iota(jnp.int32, (128, 128), 1) // HEAD_DIM
    return jnp.where(i == j, scale, 0.0).astype(BF16)


def _head_sum(x, bd):
    hi = x.astype(BF16)
    lo = (x - hi.astype(F32)).astype(BF16)
    return _dot(hi, bd, NN) + _dot(lo, bd, NN)


def _place():
    return lax.axis_index("x"), lax.axis_index("y"), lax.axis_index("c")


class _TwoLevelGather:
    def __init__(self, stacks, send_sems, recv_sems):
        self.stacks, self.send_sems, self.recv_sems = stacks, send_sems, recv_sems
        x, y, c = _place()
        self.c = c
        self.me, self.sibling = (x, y, c), (x, y, 1 - c)
        self.chips = [(x, 1 - y), (1 - x, y), (1 - x, 1 - y)]

    @staticmethod
    def index(p):
        return 4 * p[0] + 2 * p[1] + p[2]

    def copy(self, a, k, block, to, src=None):
        dst = self.stacks[a].at[self.index(block)]
        return pltpu.make_async_remote_copy(
            src_ref=dst if src is None else src, dst_ref=dst,
            send_sem=self.send_sems.at[7 * a + k], recv_sem=self.recv_sems.at[7 * a + k],
            device_id=to, device_id_type=MESH)

    def first(self, a, src=None):
        cps = [self.copy(a, 0, self.me, self.sibling, src)]
        return cps + [self.copy(a, 1 + j, self.me, (*chip, self.c), src) for j, chip in enumerate(self.chips)]

    def forwards(self, a):
        return [self.copy(a, 4 + j, (*chip, self.c), self.sibling) for j, chip in enumerate(self.chips)]

    def start(self, srcs=None):
        for a in range(len(self.stacks)):
            for cp in self.first(a, None if srcs is None else srcs[a]):
                cp.start()

    def forward(self):
        for a in range(len(self.stacks)):
            for j, (chip, fwd) in enumerate(zip(self.chips, self.forwards(a))):
                self.copy(a, 1 + j, (*chip, self.c), self.me).wait_recv()
                fwd.start()

    def finish(self):
        for a in range(len(self.stacks)):
            self.copy(a, 0, self.sibling, self.me).wait_recv()
            for j, chip in enumerate(self.chips):
                self.copy(a, 4 + j, (*chip, 1 - self.c), self.me).wait_recv()
        for a in range(len(self.stacks)):
            for cp in self.first(a) + self.forwards(a):
                cp.wait_send()


def _gather_first_weights(gathered_parts, gathered_dtypes, cast_parts):
    n, m = len(gathered_parts), len(cast_parts)

    def body(*refs):
        ins, casts_in = refs[:n], refs[n:n + m]
        outs, casts_out = refs[n + m:2 * n + m], refs[2 * n + m:2 * n + 2 * m]
        send_sems, recv_sems = refs[2 * n + 2 * m], refs[2 * n + 2 * m + 1]
        gather = _TwoLevelGather(outs, send_sems, recv_sems)
        for a in range(n):
            outs[a][gather.index(gather.me)] = ins[a][...].astype(outs[a].dtype)
        gather.start()
        for a in range(m):
            casts_out[a][...] = casts_in[a][...].astype(BF16)
        gather.forward()
        gather.finish()

    vm = pl.BlockSpec(memory_space=pltpu.VMEM)
    return pl.pallas_call(
        body, name="weight_all_gather",
        out_shape=[jax.ShapeDtypeStruct((N_DEV,) + p.shape, dt) for p, dt in zip(gathered_parts, gathered_dtypes)]
        + [jax.ShapeDtypeStruct(p.shape, BF16) for p in cast_parts],
        in_specs=[vm] * (n + m), out_specs=[vm] * (n + m),
        scratch_shapes=[pltpu.SemaphoreType.DMA((7 * n,)), pltpu.SemaphoreType.DMA((7 * n,))],
        compiler_params=_params(vmem=VMEM_LIMIT),
    )(*gathered_parts, *cast_parts)


class _DirectReduceScatter:
    def __init__(self, stacks, landing, send_sems, recv_sems):
        self.stacks, self.landing, self.send_sems, self.recv_sems = stacks, landing, send_sems, recv_sems
        self.place = _place()

    def copies(self):
        x, y, c = self.place
        cps = []
        for a in range(len(self.stacks)):
            for k in range(1, N_DEV):
                peer = (1 - x if k & 4 else x, 1 - y if k & 2 else y, 1 - c if k & 1 else c)
                cps.append(pltpu.make_async_remote_copy(
                    src_ref=self.stacks[a].at[4 * peer[0] + 2 * peer[1] + peer[2]], dst_ref=self.landing[a].at[k - 1],
                    send_sem=self.send_sems.at[7 * a + k - 1], recv_sem=self.recv_sems.at[7 * a + k - 1],
                    device_id=peer, device_id_type=MESH))
        return cps

    def start(self):
        for cp in self.copies():
            cp.start()

    def finish(self):
        for cp in self.copies():
            cp.wait_recv()
        for cp in self.copies():
            cp.wait_send()


def _all_reduce_small(v):
    def body(v_ref, out_ref, gathered, send_sems, recv_sems):
        x, y, c = _place()
        mine = 4 * x + 2 * y + c
        gathered[mine] = v_ref[...]

        def copy(k):
            peer = (1 - x if k & 4 else x, 1 - y if k & 2 else y, 1 - c if k & 1 else c)
            return pltpu.make_async_remote_copy(
                src_ref=gathered.at[mine], dst_ref=gathered.at[mine],
                send_sem=send_sems.at[k - 1], recv_sem=recv_sems.at[k - 1], device_id=peer, device_id_type=MESH)

        copies = [copy(k) for k in range(1, N_DEV)]
        for cp in copies:
            cp.start()
        for cp in copies:
            cp.wait_recv()
        for cp in copies:
            cp.wait_send()
        total = gathered[0]
        for d in range(1, N_DEV):
            total = total + gathered[d]
        out_ref[...] = total

    vm = pl.BlockSpec(memory_space=pltpu.VMEM)
    return pl.pallas_call(
        body, name="small_all_reduce",
        out_shape=jax.ShapeDtypeStruct(v.shape, F32),
        in_specs=[vm], out_specs=vm,
        scratch_shapes=[pltpu.VMEM((N_DEV,) + v.shape, F32),
                        pltpu.SemaphoreType.DMA((N_DEV - 1,)), pltpu.SemaphoreType.DMA((N_DEV - 1,))],
    )(v)


def _reduce_scatter_cores(grads):
    n = len(grads)

    def body(*refs):
        ins, outs = refs[:n], refs[n:2 * n]
        send_sems, recv_sems = refs[2 * n], refs[2 * n + 1]
        x, y, c = _place()
        copies = []
        for a in range(n):
            for k in range(4):
                copies.append(pltpu.make_async_remote_copy(
                    src_ref=ins[a].at[k, 1 - c], dst_ref=outs[a].at[k],
                    send_sem=send_sems.at[4 * a + k], recv_sem=recv_sems.at[4 * a + k],
                    device_id=(x, y, 1 - c), device_id_type=MESH))
        for cp in copies:
            cp.start()
        for cp in copies:
            cp.wait_recv()
        for cp in copies:
            cp.wait_send()

    hbm = pl.BlockSpec(memory_space=pl.ANY)
    return pl.pallas_call(
        body, name="grad_reduce_scatter_cores",
        out_shape=[jax.ShapeDtypeStruct((4,) + g.shape[2:], g.dtype) for g in grads],
        in_specs=[hbm] * n, out_specs=[hbm] * n,
        scratch_shapes=[pltpu.SemaphoreType.DMA((4 * n,)), pltpu.SemaphoreType.DMA((4 * n,))],
    )(*grads)


def _reduce_scatter_chips(partials):
    n = len(partials)

    def body(*refs):
        ins, outs = refs[:n], refs[n:2 * n]
        send_sems, recv_sems = refs[2 * n], refs[2 * n + 1]
        x, y, c = _place()
        copies = []
        for a in range(n):
            for j in (1, 2, 3):
                px = 1 - x if j & 2 else x
                py = 1 - y if j & 1 else y
                copies.append(pltpu.make_async_remote_copy(
                    src_ref=ins[a].at[2 * px + py], dst_ref=outs[a].at[j - 1],
                    send_sem=send_sems.at[3 * a + j - 1], recv_sem=recv_sems.at[3 * a + j - 1],
                    device_id=(px, py, c), device_id_type=MESH))
        for cp in copies:
            cp.start()
        for cp in copies:
            cp.wait_recv()
        for cp in copies:
            cp.wait_send()

    hbm = pl.BlockSpec(memory_space=pl.ANY)
    return pl.pallas_call(
        body, name="grad_reduce_scatter_chips",
        out_shape=[jax.ShapeDtypeStruct((3,) + p.shape[1:], p.dtype) for p in partials],
        in_specs=[hbm] * n, out_specs=[hbm] * n,
        scratch_shapes=[pltpu.SemaphoreType.DMA((3 * n,)), pltpu.SemaphoreType.DMA((3 * n,))],
    )(*partials)


def _add_core_partials(core, grads, received):
    n = len(grads)

    def body(core_ref, *refs):
        del core_ref
        gs, rs, outs = refs[:n], refs[n:2 * n], refs[2 * n:]
        for a in range(n):
            outs[a][...] = (gs[a][...].astype(F32) + rs[a][...].astype(F32)).astype(outs[a].dtype)

    in_specs = [pl.BlockSpec((None, None) + g.shape[2:], lambda k, cref: (k, cref[0], 0, 0)) for g in grads]
    in_specs += [pl.BlockSpec((None,) + r.shape[1:], lambda k, cref: (k, 0, 0)) for r in received]
    out_specs = [pl.BlockSpec((None,) + r.shape[1:], lambda k, cref: (k, 0, 0)) for r in received]
    return pl.pallas_call(
        body, name="grad_add_core_partials",
        grid_spec=pltpu.PrefetchScalarGridSpec(num_scalar_prefetch=1, grid=(4,), in_specs=in_specs, out_specs=out_specs),
        out_shape=[jax.ShapeDtypeStruct(r.shape, r.dtype) for r in received],
        compiler_params=_params(("arbitrary",)),
    )(core, *grads, *received)


def _add_chip_partials(chip, partials, received):
    n = len(partials)

    def body(chip_ref, *refs):
        del chip_ref
        ps, rs, outs = refs[:n], refs[n:2 * n], refs[2 * n:]
        for a in range(n):
            outs[a][...] = ((ps[a][...].astype(F32) + rs[a][0].astype(F32)) + rs[a][1].astype(F32)) + rs[a][2].astype(F32)

    in_specs = [pl.BlockSpec((None,) + p.shape[1:], lambda i, cref: (cref[0], 0, 0)) for p in partials]
    in_specs += [pl.BlockSpec(r.shape, lambda i, cref: (0, 0, 0)) for r in received]
    out_specs = [pl.BlockSpec(p.shape[1:], lambda i, cref: (0, 0)) for p in partials]
    return pl.pallas_call(
        body, name="grad_add_chip_partials",
        grid_spec=pltpu.PrefetchScalarGridSpec(num_scalar_prefetch=1, grid=(1,), in_specs=in_specs, out_specs=out_specs),
        out_shape=[jax.ShapeDtypeStruct(p.shape[1:], F32) for p in partials],
        compiler_params=_params(("arbitrary",), vmem=VMEM_LIMIT),
    )(chip, *partials, *received)


def _in_projection(x, g_mix, w_in_t, tm):
    s = x.shape[0]

    def body(x_ref, g_ref, w_ref, h_ref, z_ref):
        xh, _ = _rms(x_ref[...])
        h = (xh * g_ref[...]).astype(BF16)
        h_ref[...] = h
        for n0 in range(0, D_IN, 512):
            z_ref[:, n0:n0 + 512] = _dot(h, w_ref[n0:n0 + 512, :], NT).astype(BF16)

    return pl.pallas_call(
        body, name="in_projection", grid=(s // tm,),
        in_specs=[pl.BlockSpec((tm, D_MODEL), lambda i: (i, 0)), _const_spec((1, D_MODEL)), _const_spec((D_IN, D_MODEL))],
        out_specs=[pl.BlockSpec((tm, D_MODEL), lambda i: (i, 0)), pl.BlockSpec((tm, D_IN), lambda i: (i, 0))],
        out_shape=[jax.ShapeDtypeStruct((s, D_MODEL), BF16), jax.ShapeDtypeStruct((s, D_IN), BF16)],
        compiler_params=_params(("parallel",), vmem=VMEM_LIMIT),
    )(x, g_mix, w_in_t)


RESIDUES = 16
CHUNK = 512
GROUP = 8


def _branch_geometry(d, seg):
    n_runs = RESIDUES // d
    w = BLOCK // n_runs
    blocks = seg // w
    group = min(GROUP, blocks)
    chains = min(GROUP // group, d)
    return n_runs, w, blocks // group, group, chains


def _permuted_masks():
    out = []
    for d in DILATIONS:
        n_runs = RESIDUES // d
        w = BLOCK // n_runs
        p = np.arange(BLOCK)
        pos = (p % w) * n_runs + p // w
        dist = pos[:, None] - np.concatenate([pos - BLOCK, pos])[None, :]
        band = (dist >= 0) & (dist <= BLOCK)
        first = band & (np.arange(2 * BLOCK)[None, :] >= BLOCK)
        both = [np.where(m, 0.0, NEG).astype(np.float32) for m in (band, first)]
        out.append(np.stack([np.concatenate([m, m], axis=0) for m in both]))
    return jnp.asarray(np.stack(out))


def _run_starts(u, grp, d, seg, n_runs, w, group):
    return [(u + d * q) * seg + (w * group) * grp for q in range(n_runs)]


def _tile_rows(start, w, b):
    off = start + w * b
    if b < 0:
        off = jnp.maximum(off, 0)
    return pl.ds(pl.multiple_of(off, w), w)


def _load_tile(ref, starts, w, b):
    parts = [ref[_tile_rows(st, w, b), :] for st in starts]
    return parts[0] if len(parts) == 1 else jnp.concatenate(parts, axis=0)


def _store_tile(ref, starts, w, b, tile):
    for i, st in enumerate(starts):
        ref[_tile_rows(st, w, b), :] = tile[i * w:(i + 1) * w]


def _add_tile(ref, starts, w, b, tile):
    for i, st in enumerate(starts):
        ref[_tile_rows(st, w, b), :] += tile[i * w:(i + 1) * w]


def _to_residue_major(dst, src, c, seg):
    per = CHUNK // RESIDUES
    for r in range(RESIDUES):
        dst[pl.ds(pl.multiple_of(r * seg + c * per, per), per), :] = src[pl.ds(r, per, stride=RESIDUES), :]


def _from_residue_major(dst, dst_start, src, c, seg):
    per = CHUNK // RESIDUES
    for r in range(RESIDUES):
        dst[pl.ds(dst_start + r, per, stride=RESIDUES), :] = src[pl.ds(pl.multiple_of(r * seg + c * per, per), per), :]


def _stack_heads(t, head0):
    return jnp.concatenate([jnp.where(head0, t, 0.0), jnp.where(head0, 0.0, t)], axis=0).astype(BF16)


def _attention_forward(z, gq2, gk2, masks, shards):
    s = z.shape[0]
    seg = s // RESIDUES
    n_w = len(shards)

    def body(zq_ref, zk_ref, zv_ref, gq_ref, gk_ref, mask_ref, *rest):
        shard_refs, (o_ref, lse_ref), stacks = rest[:n_w], rest[n_w:n_w + 2], rest[n_w + 2:2 * n_w + 2]
        qf, kf, vf, o_st, l_st, tq, tk, tv, send_sems, recv_sems, local_sems = rest[2 * n_w + 2:]
        pair = pl.program_id(0)
        gather = _TwoLevelGather(stacks, send_sems, recv_sems)

        def own_copies():
            mine = gather.index(gather.me)
            return [pltpu.make_async_copy(shard_refs[a], stacks[a].at[mine], local_sems.at[a]) for a in range(n_w)]

        @pl.when(pair == 0)
        def _():
            for cp in own_copies():
                cp.start()
            gather.start(shard_refs)

        @pl.when(pair == 2)
        def _():
            gather.forward()

        bd = _head_blockdiag(1.0 / HEAD_DIM)
        gq = gq_ref[...] * (HEAD_DIM ** -0.5)
        gk = gk_ref[...]

        def prep(c, carry):
            rows = pl.ds(pl.multiple_of(c * CHUNK, CHUNK), CHUNK)
            zq = zq_ref[rows, :].astype(F32)
            zk = zk_ref[rows, :].astype(F32)
            tq[...] = (zq * lax.rsqrt(_head_sum(zq * zq, bd) + EPS)) * gq
            tk[...] = (zk * lax.rsqrt(_head_sum(zk * zk, bd) + EPS)) * gk
            tv[...] = zv_ref[rows, :].astype(F32)
            _to_residue_major(qf, tq, c, seg)
            _to_residue_major(kf, tk, c, seg)
            _to_residue_major(vf, tv, c, seg)
            return carry

        lax.fori_loop(0, s // CHUNK, prep, 0)

        head0 = lax.broadcasted_iota(jnp.int32, (BLOCK, 128), 1) < HEAD_DIM

        def run_branch(branch, d):
            n_runs, w, steps, group, chains = _branch_geometry(d, seg)
            slots = [(ci, b) for ci in range(chains) for b in range(group)]

            def step(u, grp):
                starts = [_run_starts(u + ci, grp, d, seg, n_runs, w, group) for ci in range(chains)]
                kt = [[_load_tile(kf, st, w, b).astype(BF16) for b in range(-1, group)] for st in starts]
                vt = [[_load_tile(vf, st, w, b).astype(BF16) for b in range(-1, group)] for st in starts]
                scores = []
                for ci, b in slots:
                    q2 = _stack_heads(_load_tile(qf, starts[ci], w, b), head0)
                    kc = jnp.concatenate([kt[ci][b], kt[ci][b + 1]], axis=0)
                    mask = mask_ref[branch, 0]
                    if b == 0:
                        mask = jnp.where(grp == 0, mask_ref[branch, 1], mask)
                    scores.append(_dot(q2, kc, NT) + mask)
                sc = jnp.concatenate(scores, axis=0)
                m = jnp.max(sc, axis=-1, keepdims=True)
                e = jnp.exp(sc - m).astype(BF16)
                ones = jnp.ones((BLOCK, 128), BF16)
                fresh = []
                for n, (ci, b) in enumerate(slots):
                    rows = slice(2 * BLOCK * n, 2 * BLOCK * (n + 1))
                    v_aug = jnp.concatenate([jnp.concatenate([vt[ci][b], ones], axis=1),
                                             jnp.concatenate([vt[ci][b + 1], ones], axis=1)], axis=0)
                    o2 = _dot(e[rows], v_aug, NN)
                    den = o2[:, 128:]
                    o2 = o2[:, :128] * (1.0 / den)
                    l2 = m[rows] + jnp.log(den)
                    fresh.append((jnp.where(head0, o2[:BLOCK], o2[BLOCK:]), jnp.where(head0, l2[:BLOCK], l2[BLOCK:])))
                for n, (ci, b) in enumerate(slots):
                    o_new, l_new = fresh[n]
                    if branch > 0:
                        l_old = _load_tile(l_st, starts[ci], w, b)
                        top = jnp.maximum(l_old, l_new)
                        e_old = jnp.exp(l_old - top)
                        e_new = jnp.exp(l_new - top)
                        tot = e_old + e_new
                        inv = 1.0 / tot
                        o_new = _load_tile(o_st, starts[ci], w, b) * (e_old * inv) + o_new * (e_new * inv)
                        l_new = top + jnp.log(tot)
                    _store_tile(o_st, starts[ci], w, b, o_new)
                    _store_tile(l_st, starts[ci], w, b, l_new)

            def unit(it, carry):
                def inner(grp, carry2):
                    step(it * chains, grp)
                    return carry2
                return lax.fori_loop(0, steps, inner, carry)

            lax.fori_loop(0, d // chains, unit, 0)

        for branch, d in enumerate(DILATIONS):
            run_branch(branch, d)

        def finish(c, carry):
            _from_residue_major(o_ref, c * CHUNK, o_st, c, seg)
            _from_residue_major(lse_ref, c * CHUNK, l_st, c, seg)
            return carry

        lax.fori_loop(0, s // CHUNK, finish, 0)

        @pl.when(pair == N_PAIRS - 1)
        def _():
            gather.finish()
            for cp in own_copies():
                cp.wait()

    def col(c0):
        return pl.BlockSpec((s, 128), lambda p: (0, c0 + p), pipeline_mode=pl.Buffered(1))

    hbm = pl.BlockSpec(memory_space=pl.ANY)
    out = pl.pallas_call(
        body, name="attention_forward", grid=(N_PAIRS,),
        in_specs=[col(12), col(16), col(20), _const_spec((1, 128)), _const_spec((1, 128)), _const_spec(masks.shape)]
        + [hbm] * n_w,
        out_specs=[pl.BlockSpec((s, 128), lambda p: (0, p)), pl.BlockSpec((s, 128), lambda p: (0, p))] + [hbm] * n_w,
        out_shape=[jax.ShapeDtypeStruct((s, D_ATTN), F32), jax.ShapeDtypeStruct((s, D_ATTN), F32)]
        + [jax.ShapeDtypeStruct((N_DEV,) + w.shape, w.dtype) for w in shards],
        scratch_shapes=[pltpu.VMEM((s, 128), F32)] * 5 + [pltpu.VMEM((CHUNK, 128), F32)] * 3
        + [pltpu.SemaphoreType.DMA((7 * n_w,)), pltpu.SemaphoreType.DMA((7 * n_w,)), pltpu.SemaphoreType.DMA((n_w,))],
        compiler_params=_params(("arbitrary",), vmem=VMEM_LIMIT),
    )(z, z, z, gq2, gk2, masks, *shards)
    return out[0], out[1], out[2:]


def _conv_forward(zt, zprev, w, first):
    u, gb, gc = zt[:, :D_CONV], zt[:, D_CONV:2 * D_CONV], zt[:, 2 * D_CONV:]
    cu = gc * u
    cu_prev = jnp.where(first, 0.0, zprev[:, 2 * D_CONV:] * zprev[:, :D_CONV])
    row = lax.broadcasted_iota(jnp.int32, cu.shape, 0)
    cu1 = jnp.where(row >= 1, pltpu.roll(cu, 1, 0), cu_prev[7:8, :])
    cu2 = jnp.where(row >= 2, pltpu.roll(cu, 2, 0), jnp.where(row == 1, cu_prev[7:8, :], cu_prev[6:7, :]))
    cv = w[0:1, :] * cu2 + w[1:2, :] * cu1 + w[2:3, :] * cu
    return u, gb, gc, cu, cu1, cu2, cv


def _mix_out(z, y_attn, x, conv_w, g_conv_out, g_attn_out, g_ffn, w_out, tm):
    s = x.shape[0]
    hb = tm // 8

    def body(z_ref, zp_ref, ya_ref, x_ref, cw_ref, gc_ref, ga_ref, gf_ref, w_ref, mix_ref, x2_ref, h2_ref):
        first = pl.program_id(0) == 0
        zt = z_ref[...].astype(F32)
        _, gb, _, _, _, _, cv = _conv_forward(zt, zp_ref[...].astype(F32), cw_ref[...], first)
        nc, _ = _rms(gb * cv)
        na, _ = _rms(ya_ref[...])
        mix = jnp.concatenate([nc * gc_ref[...], na * ga_ref[...]], axis=-1).astype(BF16)
        mix_ref[...] = mix
        x2 = x_ref[...] + _dot(mix, w_ref[...], NN)
        x2_ref[...] = x2
        xh, _ = _rms(x2)
        h2_ref[...] = (xh * gf_ref[...]).astype(BF16)

    tile = lambda w: pl.BlockSpec((tm, w), lambda i: (i, 0))
    return pl.pallas_call(
        body, name="mix_out", grid=(s // tm,),
        in_specs=[tile(3 * D_CONV), pl.BlockSpec((8, 3 * D_CONV), lambda i: (jnp.maximum(i * hb - 1, 0), 0)),
                  tile(D_ATTN), tile(D_MODEL), _const_spec((8, D_CONV)), _const_spec((1, D_CONV)), _const_spec((1, D_ATTN)),
                  _const_spec((1, D_MODEL)), _const_spec((D_MODEL, D_MODEL))],
        out_specs=[tile(D_MODEL)] * 3,
        out_shape=[jax.ShapeDtypeStruct((s, D_MODEL), BF16), jax.ShapeDtypeStruct((s, D_MODEL), F32),
                   jax.ShapeDtypeStruct((s, D_MODEL), BF16)],
        compiler_params=_params(("arbitrary",), vmem=VMEM_LIMIT),
    )(z, z, y_attn, x, conv_w, g_conv_out, g_attn_out, g_ffn, w_out)


FF_CHUNK = 256


def _ffn_forward(h2, x2, target, w_gate_t, w_up_t, w_down, tm):
    s = h2.shape[0]

    def body(h_ref, x2_ref, t_ref, wg_ref, wu_ref, wd_ref, a_ref, b_ref, dy_ref, loss_ref, f_buf):
        @pl.when(pl.program_id(0) == 0)
        def _():
            loss_ref[...] = jnp.zeros_like(loss_ref)

        h = h_ref[...]
        for c0 in range(0, D_FF, FF_CHUNK):
            rows = slice(c0, c0 + FF_CHUNK)
            a = _dot(h, wg_ref[rows, :], NT)
            b = _dot(h, wu_ref[rows, :], NT)
            a_ref[:, rows] = a.astype(BF16)
            b_ref[:, rows] = b.astype(BF16)
            f_buf[:, rows] = ((a * jax.nn.sigmoid(a)) * b).astype(BF16)
        y = x2_ref[...] + _dot(f_buf[...], wd_ref[...], NN)
        err = y - t_ref[...]
        dy_ref[...] = err * (1.0 / D_MODEL)
        loss_ref[...] += jnp.sum(err * err)

    tile = lambda w: pl.BlockSpec((tm, w), lambda i: (i, 0))
    return pl.pallas_call(
        body, name="ffn_forward", grid=(s // tm,),
        in_specs=[tile(D_MODEL), tile(D_MODEL), tile(D_MODEL)] + [_const_spec((D_FF, D_MODEL))] * 3,
        out_specs=[tile(D_FF), tile(D_FF), tile(D_MODEL), pl.BlockSpec((8, 128), lambda i: (0, 0))],
        out_shape=[jax.ShapeDtypeStruct((s, D_FF), BF16), jax.ShapeDtypeStruct((s, D_FF), BF16),
                   jax.ShapeDtypeStruct((s, D_MODEL), F32), jax.ShapeDtypeStruct((8, 128), F32)],
        scratch_shapes=[pltpu.VMEM((tm, D_FF), BF16)],
        compiler_params=_params(("arbitrary",), vmem=VMEM_LIMIT),
    )(h2, x2, target, w_gate_t, w_up_t, w_down)


def _ffn_backward(dy, a, b, x2, g_ffn, w_gate_t, w_up_t, w_down, tm):
    s = dy.shape[0]
    half = D_FF // 2
    pieces = [(p0, min(FF_CHUNK, half - p0)) for p0 in range(0, half, FF_CHUNK)]

    def body(dy_ref, a_ref, b_ref, x2_ref, g_ref, wg_ref, wu_ref, wd_ref, da_ref, db_ref, dx2_ref, dg_ref, acc):
        i, c = pl.program_id(0), pl.program_id(1)

        @pl.when((i == 0) & (c == 0))
        def _():
            dg_ref[...] = jnp.zeros_like(dg_ref)

        dyb = dy_ref[...].astype(BF16)
        for p0, width in pieces:
            rows = slice(p0, p0 + width)
            df = _dot(dyb, wd_ref[rows, :], NT)
            av = a_ref[:, rows].astype(F32)
            bv = b_ref[:, rows].astype(F32)
            sig = jax.nn.sigmoid(av)
            silu = av * sig
            da_ref[:, rows] = (df * bv * (sig * (1.0 + av * (1.0 - sig)))).astype(BF16)
            db_ref[:, rows] = (df * silu).astype(BF16)
        dh = _dot(da_ref[...], wg_ref[...], NN) + _dot(db_ref[...], wu_ref[...], NN)

        @pl.when(c == 0)
        def _():
            acc[...] = dh

        @pl.when(c == 1)
        def _():
            dh_all = acc[...] + dh
            xh, r = _rms(x2_ref[...])
            dg_ref[...] += jnp.sum(dh_all * xh, axis=0, keepdims=True)
            dx2_ref[...] = dy_ref[...] + _rms_bwd(dh_all, xh, r, g_ref[...])

    tile = lambda w: pl.BlockSpec((tm, w), lambda i, c: (i, 0))
    part = pl.BlockSpec((tm, half), lambda i, c: (i, c))
    weight = pl.BlockSpec((half, D_MODEL), lambda i, c: (c, 0))
    return pl.pallas_call(
        body, name="ffn_backward", grid=(s // tm, 2),
        in_specs=[tile(D_MODEL), part, part, tile(D_MODEL), _const_spec((1, D_MODEL)), weight, weight, weight],
        out_specs=[part, part, tile(D_MODEL), pl.BlockSpec((1, D_MODEL), lambda i, c: (0, 0))],
        out_shape=[jax.ShapeDtypeStruct((s, D_FF), BF16)] * 2
        + [jax.ShapeDtypeStruct((s, D_MODEL), F32), jax.ShapeDtypeStruct((1, D_MODEL), F32)],
        scratch_shapes=[pltpu.VMEM((tm, D_MODEL), F32)],
        compiler_params=_params(("arbitrary", "arbitrary"), vmem=VMEM_LIMIT),
    )(dy, a, b, x2, g_ffn, w_gate_t, w_up_t, w_down)


def _out_backward(dx2, z, y_attn, conv_w, g_conv_out, g_attn_out, w_out, tm):
    s = dx2.shape[0]
    hb = tm // 8

    def body(dx2_ref, z_ref, zp_ref, ya_ref, cw_ref, gc_ref, ga_ref, w_ref, dya_ref, dgb_ref, dcv_ref, dgc_ref, dga_ref):
        first = pl.program_id(0) == 0

        @pl.when(first)
        def _():
            dgc_ref[...] = jnp.zeros_like(dgc_ref)
            dga_ref[...] = jnp.zeros_like(dga_ref)

        dmix = _dot(dx2_ref[...].astype(BF16), w_ref[...], NT)
        _, gb, _, _, _, _, cv = _conv_forward(z_ref[...].astype(F32), zp_ref[...].astype(F32), cw_ref[...], first)
        ych, rc = _rms(gb * cv)
        dnc = dmix[:, :D_CONV]
        dgc_ref[...] += jnp.sum(dnc * ych, axis=0, keepdims=True)
        dyc = _rms_bwd(dnc, ych, rc, gc_ref[...])
        dgb_ref[...] = (dyc * cv).astype(BF16)
        dcv_ref[...] = dyc * gb
        yah, ra = _rms(ya_ref[...])
        dna = dmix[:, D_CONV:]
        dga_ref[...] += jnp.sum(dna * yah, axis=0, keepdims=True)
        dya_ref[...] = _rms_bwd(dna, yah, ra, ga_ref[...])

    tile = lambda w: pl.BlockSpec((tm, w), lambda i: (i, 0))
    vec = pl.BlockSpec((1, D_CONV), lambda i: (0, 0))
    return pl.pallas_call(
        body, name="out_backward", grid=(s // tm,),
        in_specs=[tile(D_MODEL), tile(3 * D_CONV), pl.BlockSpec((8, 3 * D_CONV), lambda i: (jnp.maximum(i * hb - 1, 0), 0)),
                  tile(D_ATTN), _const_spec((8, D_CONV)), _const_spec((1, D_CONV)), _const_spec((1, D_ATTN)),
                  _const_spec((D_MODEL, D_MODEL))],
        out_specs=[tile(D_ATTN), tile(D_CONV), tile(D_CONV), vec, vec],
        out_shape=[jax.ShapeDtypeStruct((s, D_ATTN), F32), jax.ShapeDtypeStruct((s, D_CONV), BF16),
                   jax.ShapeDtypeStruct((s, D_CONV), F32), jax.ShapeDtypeStruct((1, D_CONV), F32),
                   jax.ShapeDtypeStruct((1, D_ATTN), F32)],
        compiler_params=_params(("arbitrary",), vmem=VMEM_LIMIT),
    )(dx2, z, z, y_attn, conv_w, g_conv_out, g_attn_out, w_out)


def _attention_backward(z, o, lse, dya, gq2, gk2, masks, grad_stacks):
    s = z.shape[0]
    seg = s // RESIDUES
    n_g = len(grad_stacks)

    def body(z_hbm, o_hbm, lse_hbm, do_hbm, gq_ref, gk_ref, mask_ref, *rest):
        stack_refs = rest[:n_g]
        dzq_ref, dzk_ref, dzv_ref, dgq_ref, dgk_ref = rest[n_g:n_g + 5]
        landing = rest[n_g + 5:2 * n_g + 5]
        (qf, kf, vf, dof, stats, dq, dk, dv, t0, t1, t2, zbuf, fbuf, sems, send_sems, recv_sems) = rest[2 * n_g + 5:]
        pair = pl.program_id(0)
        scatter = _DirectReduceScatter(stack_refs, landing, send_sems, recv_sems)

        @pl.when(pair == 0)
        def _():
            scatter.start()

        n_chunks = s // CHUNK
        bd_mean = _head_blockdiag(1.0 / HEAD_DIM)
        bd_sum = _head_blockdiag(1.0)
        gq = gq_ref[...] * (HEAD_DIM ** -0.5)
        gk = gk_ref[...]
        lane = lax.broadcasted_iota(jnp.int32, (CHUNK, 128), 1) % HEAD_DIM

        def chunk_copies(c, slot, n_z, n_f):
            rows = pl.ds(pl.multiple_of(c * CHUNK, CHUNK), CHUNK)
            cps = []
            for i, col in enumerate((12, 16, 20)[:n_z]):
                cols = pl.ds(pl.multiple_of((col + pair) * 128, 128), 128)
                cps.append(pltpu.make_async_copy(z_hbm.at[rows, cols], zbuf.at[slot, i], sems.at[slot, i]))
            for i, src in enumerate((do_hbm, o_hbm, lse_hbm)[:n_f]):
                cols = pl.ds(pl.multiple_of(pair * 128, 128), 128)
                cps.append(pltpu.make_async_copy(src.at[rows, cols], fbuf.at[slot, i], sems.at[slot, 3 + i]))
            return cps

        def prefetched(c, n_z, n_f):
            slot = c % 2

            @pl.when(c + 1 < n_chunks)
            def _():
                for cp in chunk_copies(c + 1, 1 - slot, n_z, n_f):
                    cp.start()

            for cp in chunk_copies(c, slot, n_z, n_f):
                cp.wait()
            return slot

        for cp in chunk_copies(0, 0, 3, 3):
            cp.start()

        def prep(c, carry):
            slot = prefetched(c, 3, 3)
            rows = pl.ds(pl.multiple_of(c * CHUNK, CHUNK), CHUNK)
            zq = zbuf[slot, 0].astype(F32)
            t0[...] = (zq * lax.rsqrt(_head_sum(zq * zq, bd_mean) + EPS)) * gq
            zk = zbuf[slot, 1].astype(F32)
            t1[...] = (zk * lax.rsqrt(_head_sum(zk * zk, bd_mean) + EPS)) * gk
            t2[...] = zbuf[slot, 2].astype(F32)
            _to_residue_major(qf, t0, c, seg)
            _to_residue_major(kf, t1, c, seg)
            _to_residue_major(vf, t2, c, seg)
            _to_residue_major(dof, fbuf.at[slot, 0], c, seg)
            delta = _head_sum(fbuf[slot, 0] * fbuf[slot, 1], bd_sum)
            t0[...] = jnp.where(lane == 0, fbuf[slot, 2], jnp.where(lane == 1, delta, 0.0))
            _to_residue_major(stats, t0, c, seg)
            zero = jnp.zeros((CHUNK, 128), F32)
            dq[rows, :] = zero
            dk[rows, :] = zero
            dv[rows, :] = zero
            return carry

        lax.fori_loop(0, n_chunks, prep, 0)

        head0 = lax.broadcasted_iota(jnp.int32, (BLOCK, 128), 1) < HEAD_DIM

        def run_branch(branch, d):
            n_runs, w, steps, group, chains = _branch_geometry(d, seg)
            slots = [(ci, b) for ci in range(chains) for b in range(group)]

            def step(u, grp):
                starts = [_run_starts(u + ci, grp, d, seg, n_runs, w, group) for ci in range(chains)]
                kt = [[_load_tile(kf, st, w, b).astype(BF16) for b in range(-1, group)] for st in starts]
                vt = [[_load_tile(vf, st, w, b).astype(BF16) for b in range(-1, group)] for st in starts]
                dk_t = [[jnp.zeros((BLOCK, 128), F32)] * (group + 1) for _ in range(chains)]
                dv_t = [[jnp.zeros((BLOCK, 128), F32)] * (group + 1) for _ in range(chains)]
                dq_t = []
                q2s, do2s, kcs, scores, dps, lses, deltas = [], [], [], [], [], [], []
                for ci, b in slots:
                    q2 = _stack_heads(_load_tile(qf, starts[ci], w, b), head0)
                    do2 = _stack_heads(_load_tile(dof, starts[ci], w, b), head0)
                    st = _load_tile(stats, starts[ci], w, b)
                    lses += [st[:, 0:1], st[:, HEAD_DIM:HEAD_DIM + 1]]
                    deltas += [st[:, 1:2], st[:, HEAD_DIM + 1:HEAD_DIM + 2]]
                    kc = jnp.concatenate([kt[ci][b], kt[ci][b + 1]], axis=0)
                    vc = jnp.concatenate([vt[ci][b], vt[ci][b + 1]], axis=0)
                    mask = mask_ref[branch, 0]
                    if b == 0:
                        mask = jnp.where(grp == 0, mask_ref[branch, 1], mask)
                    scores.append(_dot(q2, kc, NT) + mask)
                    dps.append(_dot(do2, vc, NT))
                    q2s.append(q2)
                    do2s.append(do2)
                    kcs.append(kc)
                p_all = jnp.exp(jnp.concatenate(scores, axis=0) - jnp.concatenate(lses, axis=0))
                ds_all = (p_all * (jnp.concatenate(dps, axis=0) - jnp.concatenate(deltas, axis=0))).astype(BF16)
                p_all = p_all.astype(BF16)
                for n, (ci, b) in enumerate(slots):
                    rows = slice(2 * BLOCK * n, 2 * BLOCK * (n + 1))
                    ds, q2, do2, kc = ds_all[rows], q2s[n], do2s[n], kcs[n]
                    dq2 = _dot(ds, kc, NN)
                    dq_t.append(jnp.where(head0, dq2[:BLOCK], dq2[BLOCK:]))
                    dkc = _dot(ds, q2, TN)
                    dvc = _dot(p_all[rows], do2, TN)
                    dk_t[ci][b] = dk_t[ci][b] + dkc[:BLOCK]
                    dk_t[ci][b + 1] = dk_t[ci][b + 1] + dkc[BLOCK:]
                    dv_t[ci][b] = dv_t[ci][b] + dvc[:BLOCK]
                    dv_t[ci][b + 1] = dv_t[ci][b + 1] + dvc[BLOCK:]
                for n, (ci, b) in enumerate(slots):
                    _add_tile(dq, starts[ci], w, b, dq_t[n])
                for ci in range(chains):
                    for b in range(-1, group):
                        _add_tile(dk, starts[ci], w, b, dk_t[ci][b + 1])
                        _add_tile(dv, starts[ci], w, b, dv_t[ci][b + 1])

            def unit(it, carry):
                def inner(grp, carry2):
                    step(it * chains, grp)
                    return carry2
                return lax.fori_loop(0, steps, inner, carry)

            lax.fori_loop(0, d // chains, unit, 0)

        for branch, d in enumerate(DILATIONS):
            run_branch(branch, d)

        for cp in chunk_copies(0, 0, 2, 0):
            cp.start()

        def finish(c, carry):
            acc_q, acc_k = carry
            rows = pl.ds(pl.multiple_of(c * CHUNK, CHUNK), CHUNK)
            _from_residue_major(t0, 0, dq, c, seg)
            _from_residue_major(t1, 0, dk, c, seg)
            _from_residue_major(t2, 0, dv, c, seg)
            slot = prefetched(c, 2, 0)
            zq = zbuf[slot, 0].astype(F32)
            rq = lax.rsqrt(_head_sum(zq * zq, bd_mean) + EPS)
            qh = zq * rq
            dqn = t0[...]
            acc_q = acc_q + jnp.sum(dqn * qh, axis=0, keepdims=True)
            t = dqn * gq
            dzq_ref[rows, :] = (rq * (t - qh * _head_sum(t * qh, bd_mean))).astype(BF16)
            zk = zbuf[slot, 1].astype(F32)
            rk = lax.rsqrt(_head_sum(zk * zk, bd_mean) + EPS)
            kh = zk * rk
            dkn = t1[...]
            acc_k = acc_k + jnp.sum(dkn * kh, axis=0, keepdims=True)
            t = dkn * gk
            dzk_ref[rows, :] = (rk * (t - kh * _head_sum(t * kh, bd_mean))).astype(BF16)
            dzv_ref[rows, :] = t2[...].astype(BF16)
            return acc_q, acc_k

        zero = jnp.zeros((1, 128), F32)
        acc_q, acc_k = lax.fori_loop(0, s // CHUNK, finish, (zero, zero))
        dgq_ref[...] = acc_q * (HEAD_DIM ** -0.5)
        dgk_ref[...] = acc_k

        @pl.when(pair == N_PAIRS - 1)
        def _():
            scatter.finish()

    hbm = pl.BlockSpec(memory_space=pl.ANY)
    gain = pl.BlockSpec((None, 1, 128), lambda p: (p, 0, 0))
    dz_spec = pl.BlockSpec((s, 128), lambda p: (0, p))
    out = pl.pallas_call(
        body, name="attention_backward", grid=(N_PAIRS,),
        in_specs=[hbm, hbm, hbm, hbm, _const_spec((1, 128)), _const_spec((1, 128)), _const_spec(masks.shape)] + [hbm] * n_g,
        out_specs=[dz_spec, dz_spec, dz_spec, gain, gain] + [hbm] * n_g,
        out_shape=[jax.ShapeDtypeStruct((s, D_ATTN), BF16)] * 3 + [jax.ShapeDtypeStruct((N_PAIRS, 1, 128), F32)] * 2
        + [jax.ShapeDtypeStruct((N_DEV - 1,) + g.shape[1:], g.dtype) for g in grad_stacks],
        scratch_shapes=[pltpu.VMEM((s, 128), F32)] * 8 + [pltpu.VMEM((CHUNK, 128), F32)] * 3
        + [pltpu.VMEM((2, 3, CHUNK, 128), BF16), pltpu.VMEM((2, 3, CHUNK, 128), F32), pltpu.SemaphoreType.DMA((2, 6)),
           pltpu.SemaphoreType.DMA((7 * n_g,)), pltpu.SemaphoreType.DMA((7 * n_g,))],
        compiler_params=_params(("arbitrary",), vmem=VMEM_LIMIT),
    )(z, o, lse, dya, gq2, gk2, masks, *grad_stacks)
    return out[:5], out[5:]


def _in_backward(dcv, dgb, dzq, dzk, dzv, z, x, dx2, g_mix, conv_w, w_in_t, tm):
    s = x.shape[0]
    hb = tm // 8
    last_halo = s // 8 - 1

    def body(dcv_ref, dcn_ref, dgb_ref, dzq_ref, dzk_ref, dzv_ref, z_ref, zp_ref, x_ref, dx2_ref, g_ref, cw_ref, w_ref,
             dz_ref, dx_ref, dg_ref, dcw_ref):
        i = pl.program_id(0)
        first = i == 0
        last = i == pl.num_programs(0) - 1

        @pl.when(first)
        def _():
            dg_ref[...] = jnp.zeros_like(dg_ref)
            dcw_ref[...] = jnp.zeros_like(dcw_ref)

        w = cw_ref[...]
        u, _, gc, cu, cu1, cu2, _ = _conv_forward(z_ref[...].astype(F32), zp_ref[...].astype(F32), w, first)
        dcv_t = dcv_ref[...]
        nxt = jnp.where(last, 0.0, dcn_ref[...])
        row = lax.broadcasted_iota(jnp.int32, dcv_t.shape, 0)
        up1 = jnp.where(row < tm - 1, pltpu.roll(dcv_t, tm - 1, 0), nxt[0:1, :])
        up2 = jnp.where(row < tm - 2, pltpu.roll(dcv_t, tm - 2, 0), jnp.where(row == tm - 2, nxt[0:1, :], nxt[1:2, :]))
        dcu = w[2:3, :] * dcv_t + w[1:2, :] * up1 + w[0:1, :] * up2
        dcw = jnp.concatenate([jnp.sum(dcv_t * cu2, axis=0, keepdims=True), jnp.sum(dcv_t * cu1, axis=0, keepdims=True),
                               jnp.sum(dcv_t * cu, axis=0, keepdims=True), jnp.zeros((5, D_CONV), F32)], axis=0)
        dcw_ref[...] += dcw
        dz_ref[:, :D_CONV] = (dcu * gc).astype(BF16)
        dz_ref[:, D_CONV:2 * D_CONV] = dgb_ref[...]
        dz_ref[:, 2 * D_CONV:3 * D_CONV] = (dcu * u).astype(BF16)
        dz_ref[:, 3 * D_CONV:3 * D_CONV + D_ATTN] = dzq_ref[...]
        dz_ref[:, 3 * D_CONV + D_ATTN:3 * D_CONV + 2 * D_ATTN] = dzk_ref[...]
        dz_ref[:, 3 * D_CONV + 2 * D_ATTN:] = dzv_ref[...]
        dh = _dot(dz_ref[...], w_ref[...], NN)
        xh, r = _rms(x_ref[...])
        dg_ref[...] += jnp.sum(dh * xh, axis=0, keepdims=True)
        dx_ref[...] = dx2_ref[...] + _rms_bwd(dh, xh, r, g_ref[...])

    tile = lambda w: pl.BlockSpec((tm, w), lambda i: (i, 0))
    return pl.pallas_call(
        body, name="in_backward", grid=(s // tm,),
        in_specs=[tile(D_CONV), pl.BlockSpec((8, D_CONV), lambda i: (jnp.minimum((i + 1) * hb, last_halo), 0)),
                  tile(D_CONV), tile(D_ATTN), tile(D_ATTN), tile(D_ATTN),
                  tile(3 * D_CONV), pl.BlockSpec((8, 3 * D_CONV), lambda i: (jnp.maximum(i * hb - 1, 0), 0)),
                  tile(D_MODEL), tile(D_MODEL), _const_spec((1, D_MODEL)), _const_spec((8, D_CONV)),
                  _const_spec((D_IN, D_MODEL))],
        out_specs=[tile(D_IN), tile(D_MODEL), pl.BlockSpec((1, D_MODEL), lambda i: (0, 0)),
                   pl.BlockSpec((8, D_CONV), lambda i: (0, 0))],
        out_shape=[jax.ShapeDtypeStruct((s, D_IN), BF16), jax.ShapeDtypeStruct((s, D_MODEL), F32),
                   jax.ShapeDtypeStruct((1, D_MODEL), F32), jax.ShapeDtypeStruct((8, D_CONV), F32)],
        compiler_params=_params(("arbitrary",), vmem=VMEM_LIMIT),
    )(dcv, dcv, dgb, dzq, dzk, dzv, z, z, x, dx2, g_mix, conv_w, w_in_t)


def _weight_grad(name, a, b, tn, tk, gate=None):
    s, n = a.shape
    steps = s // tk

    def body(*refs):
        gate_ref = refs[0] if gate is not None else None
        a_ref, b_ref, out_ref, acc = refs[1:] if gate is not None else refs
        k = pl.program_id(1)

        @pl.when(k == 0)
        def _():
            acc[...] = jnp.zeros_like(acc)

        lhs = a_ref[...]
        if gate is not None:
            gv = gate_ref[...].astype(F32)
            lhs = ((gv * jax.nn.sigmoid(gv)) * lhs.astype(F32)).astype(BF16)
        acc[...] += _dot(lhs, b_ref[...].astype(BF16), TN)

        @pl.when(k == steps - 1)
        def _():
            out_ref[...] = acc[...].astype(BF16)

    lhs_spec = pl.BlockSpec((tk, tn), lambda j, k: (k, j))
    return pl.pallas_call(
        body, name=name, grid=(n // tn, steps),
        in_specs=([lhs_spec] if gate is not None else []) + [lhs_spec, pl.BlockSpec((tk, D_MODEL), lambda j, k: (k, 0))],
        out_specs=pl.BlockSpec((tn, D_MODEL), lambda j, k: (j, 0)),
        out_shape=jax.ShapeDtypeStruct((n, D_MODEL), BF16),
        scratch_shapes=[pltpu.VMEM((tn, D_MODEL), F32)],
        compiler_params=_params(("parallel", "arbitrary"), vmem=VMEM_LIMIT),
    )(*([gate] if gate is not None else []), a, b)


def _adamw_math(w, g, m, v):
    nm = ADAM_B1 * m + (1.0 - ADAM_B1) * g
    nv = ADAM_B2 * v + (1.0 - ADAM_B2) * (g * g)
    m_hat = nm / (1.0 - ADAM_B1 ** ADAM_STEP)
    v_hat = nv / (1.0 - ADAM_B2 ** ADAM_STEP)
    return -ADAM_LR * (m_hat / (jnp.sqrt(v_hat) + ADAM_EPS) + ADAM_WD * w), nm, nv


def _adamw_received(name, device, w, stack, received, m, v, after):
    def body(dev_ref, w_ref, own_ref, recv_ref, m_ref, v_ref, after_ref, g_ref, d_ref, nm_ref, nv_ref):
        del dev_ref, after_ref
        g = own_ref[...].astype(F32)
        for k in range(N_DEV - 1):
            g = g + recv_ref[k].astype(F32)
        g_ref[...] = g
        d_ref[...], nm_ref[...], nv_ref[...] = _adamw_math(w_ref[...], g, m_ref[...], v_ref[...])

    rows = stack.shape[1]
    full = pl.BlockSpec((rows, D_MODEL), lambda i, dref: (0, 0))
    return pl.pallas_call(
        body, name=name,
        grid_spec=pltpu.PrefetchScalarGridSpec(
            num_scalar_prefetch=1, grid=(1,),
            in_specs=[full, pl.BlockSpec((None, rows, D_MODEL), lambda i, dref: (dref[0], 0, 0)),
                      pl.BlockSpec(received.shape, lambda i, dref: (0, 0, 0)), full, full,
                      pl.BlockSpec(after.shape, lambda i, dref: (0, 0))],
            out_specs=[full] * 4),
        out_shape=[jax.ShapeDtypeStruct((rows, D_MODEL), F32)] * 4,
        compiler_params=_params(("arbitrary",), vmem=VMEM_LIMIT),
    )(device, w, stack, received, m, v, after)


def _scatter_start(stack):
    landing = lax.empty((N_DEV - 1,) + stack.shape[1:], stack.dtype)

    def body(stack_ref, land_ref, send_sems, recv_sems, stack_thru, land_thru, token):
        del stack_thru, land_thru
        _DirectReduceScatter([stack_ref], [land_ref], send_sems, recv_sems).start()
        token[...] = jnp.zeros_like(token)

    hbm = pl.BlockSpec(memory_space=pltpu.HBM)
    sem = pl.BlockSpec(memory_space=pltpu.SEMAPHORE)
    return pl.pallas_call(
        body, name="w_in_grad_scatter_start",
        out_shape=(pltpu.SemaphoreType.DMA((N_DEV - 1,)), pltpu.SemaphoreType.DMA((N_DEV - 1,)),
                   pltpu.HBM(stack.shape, stack.dtype), pltpu.HBM(landing.shape, landing.dtype),
                   jax.ShapeDtypeStruct((8, 128), F32)),
        in_specs=(hbm, hbm), out_specs=(sem, sem, hbm, hbm, pl.BlockSpec(memory_space=pltpu.VMEM)),
        input_output_aliases={0: 2, 1: 3},
        compiler_params=pltpu.CompilerParams(has_side_effects=pltpu.SideEffectType.DATAFLOW_SIDE_EFFECTING),
    )(pltpu.with_memory_space_constraint(stack, pltpu.HBM), pltpu.with_memory_space_constraint(landing, pltpu.HBM))


def _scatter_wait(send_sems, recv_sems, stack_thru, land_thru, after):
    def body(stack_ref, land_ref, send_sems, recv_sems, *rest):
        del rest
        _DirectReduceScatter([stack_ref], [land_ref], send_sems, recv_sems).finish()

    hbm = pl.BlockSpec(memory_space=pltpu.HBM)
    sem = pl.BlockSpec(memory_space=pltpu.SEMAPHORE)
    return pl.pallas_call(
        body, name="w_in_grad_scatter_wait",
        out_shape=(pltpu.HBM(stack_thru.shape, stack_thru.dtype), pltpu.HBM(land_thru.shape, land_thru.dtype)),
        in_specs=(hbm, hbm, sem, sem) + (pl.BlockSpec(memory_space=pl.ANY),) * len(after), out_specs=(hbm, hbm),
        input_output_aliases={0: 0, 1: 1},
        compiler_params=pltpu.CompilerParams(has_side_effects=pltpu.SideEffectType.DATAFLOW_SIDE_EFFECTING),
    )(stack_thru, land_thru, send_sems, recv_sems, *after)


def _sum_received(device, stack, received):
    def body(dev_ref, own_ref, recv_ref, out_ref):
        del dev_ref
        total = own_ref[...].astype(F32)
        for k in range(N_DEV - 1):
            total = total + recv_ref[k].astype(F32)
        out_ref[...] = total

    rows = stack.shape[1]
    return pl.pallas_call(
        body, name="w_in_grad_sum",
        grid_spec=pltpu.PrefetchScalarGridSpec(
            num_scalar_prefetch=1, grid=(1,),
            in_specs=[pl.BlockSpec((None, rows, D_MODEL), lambda i, dref: (dref[0], 0, 0)),
                      pl.BlockSpec(received.shape, lambda i, dref: (0, 0, 0))],
            out_specs=pl.BlockSpec((rows, D_MODEL), lambda i, dref: (0, 0))),
        out_shape=jax.ShapeDtypeStruct((rows, D_MODEL), F32),
        compiler_params=_params(("arbitrary",), vmem=VMEM_LIMIT),
    )(device, stack, received)


def _adamw(name, w, g, m, v):
    def body(w_ref, g_ref, m_ref, v_ref, d_ref, nm_ref, nv_ref):
        d_ref[...], nm_ref[...], nv_ref[...] = _adamw_math(w_ref[...], g_ref[...], m_ref[...], v_ref[...])

    rows = w.shape[0]
    tr = 256 if rows % 256 == 0 else rows
    spec = pl.BlockSpec((tr, w.shape[1]), lambda i: (i, 0))
    return pl.pallas_call(
        body, name=name, grid=(rows // tr,),
        in_specs=[spec] * 4, out_specs=[spec] * 3,
        out_shape=[jax.ShapeDtypeStruct(w.shape, F32)] * 3,
        compiler_params=_params(("parallel",)),
    )(w, g, m, v)


def kernel(x, g_mix, w_in, conv_w, g_q, g_k, g_conv_out, g_attn_out, w_out, g_ffn, w_gate, w_up, w_down, loss_target, m_g_mix, m_w_in, m_conv_w, m_g_q, m_g_k, m_g_conv_out, m_g_attn_out, m_w_out, m_g_ffn, m_w_gate, m_w_up, m_w_down, v_g_mix, v_w_in, v_conv_w, v_g_q, v_g_k, v_g_conv_out, v_g_attn_out, v_w_out, v_g_ffn, v_w_gate, v_w_up, v_w_down):
    s = x.shape[1]
    tm = min(512, s)
    xs, target = x[0], loss_target[0]
    px, py, pc = lax.axis_index("x"), lax.axis_index("y"), lax.axis_index("c")
    device = 4 * px + 2 * py + pc

    conv_block = jnp.zeros((8, 128), F32).at[:3, :HEAD_DIM].set(conv_w[0])
    first, shards = [w_in[0].T, conv_block], [w_out[0], w_gate[0].T, w_up[0].T, w_down[0]]
    w_in_g, conv_g, *shards = _gather_first_weights(first, [BF16, F32], shards)
    w_in_t = w_in_g.reshape(D_IN, D_MODEL)
    conv_full = jnp.transpose(conv_g[:, :3, :HEAD_DIM], (1, 0, 2)).reshape(3, D_CONV)
    conv_full = jnp.concatenate([conv_full, jnp.zeros((5, D_CONV), F32)], axis=0)
    gq2 = jnp.concatenate([g_q, g_q], axis=-1)
    gk2 = jnp.concatenate([g_k, g_k], axis=-1)

    h1, z = _in_projection(xs, g_mix, w_in_t, tm)
    masks = _permuted_masks()
    y_attn, lse, gathered = _attention_forward(z, gq2, gk2, masks, shards)
    w_out_f = gathered[0].reshape(D_MODEL, D_MODEL)
    w_gate_t, w_up_t, w_down_f = [g.reshape(D_FF, D_MODEL) for g in gathered[1:]]
    mix, x2, h2 = _mix_out(z, y_attn, xs, conv_full, g_conv_out, g_attn_out, g_ffn, w_out_f, tm)
    a, b, dy, sq_err = _ffn_forward(h2, x2, target, w_gate_t, w_up_t, w_down_f, tm)

    tk = min(1024, s)
    da, db, dx2, dg_ffn = _ffn_backward(dy, a, b, x2, g_ffn, w_gate_t, w_up_t, w_down_f, tm)
    dya, dgb, dcv, dg_conv_out, dg_attn_out = _out_backward(dx2, z, y_attn, conv_full, g_conv_out, g_attn_out, w_out_f, tm)
    early = [
        _weight_grad("grad_w_out", mix, dx2, D_MODEL, tk),
        _weight_grad("grad_w_gate", da, h2, D_FF // 2, tk),
        _weight_grad("grad_w_up", db, h2, D_FF // 2, tk),
        _weight_grad("grad_w_down", b, dy, D_FF // 2, tk, gate=a),
    ]
    early = [g.reshape(N_DEV, g.shape[0] // N_DEV, D_MODEL) for g in early]
    (dzq, dzk, dzv, dgq_pairs, dgk_pairs), landed = _attention_backward(z, y_attn, lse, dya, gq2, gk2, masks, early)
    dz, grad_x, dg_mix, dconv = _in_backward(dcv, dgb, dzq, dzk, dzv, z, xs, dx2, g_mix, conv_full, w_in_t, tm)
    dev = jnp.reshape(device, (1,)).astype(jnp.int32)
    scattered = dict(zip(("w_out", "w_gate", "w_up", "w_down"), zip(early, landed)))

    late = _weight_grad("grad_w_in", dz, h1, D_IN // 2, tk).reshape(N_DEV, D_IN // N_DEV, D_MODEL)
    send_sems, recv_sems, late_thru, landing, token = _scatter_start(late)

    dg_q = jnp.sum(dgq_pairs.reshape(2 * N_PAIRS, HEAD_DIM), axis=0)
    dg_k = jnp.sum(dgk_pairs.reshape(2 * N_PAIRS, HEAD_DIM), axis=0)
    zeros = lambda n: jnp.zeros((n,), F32)
    small = jnp.stack([
        dg_mix[0], dg_ffn[0],
        jnp.concatenate([dg_conv_out[0], dg_attn_out[0]]),
        jnp.concatenate([dg_q, dg_k, zeros(D_MODEL - 2 * HEAD_DIM)]),
        jnp.concatenate([dconv[0], dconv[1]]),
        jnp.concatenate([dconv[2], zeros(D_CONV)]),
        jnp.concatenate([sq_err[0, :1], zeros(D_MODEL - 1)]),
        zeros(D_MODEL),
    ])
    total = _all_reduce_small(small + token[0, 0])
    loss = total[6, 0] * (0.5 / D_MODEL)
    gg_mix, gg_ffn = total[0:1], total[1:2]
    gg_conv_out, gg_attn_out = total[2:3, :D_CONV], total[2:3, D_CONV:]
    gg_q, gg_k = total[3:4, :HEAD_DIM], total[3:4, HEAD_DIM:2 * HEAD_DIM]
    conv_total = jnp.stack([total[4, :D_CONV], total[4, D_CONV:], total[5, :D_CONV]])
    g_conv = lax.dynamic_slice(conv_total, (0, device * HEAD_DIM), (3, HEAD_DIM))

    names = ["g_mix", "w_in", "conv_w", "g_q", "g_k", "g_conv_out", "g_attn_out", "w_out", "g_ffn", "w_gate", "w_up", "w_down"]
    weights = [g_mix, w_in[0], conv_w[0], g_q, g_k, g_conv_out, g_attn_out, w_out[0], g_ffn, w_gate[0], w_up[0], w_down[0]]
    grad_list = [gg_mix, None, g_conv, gg_q, gg_k, gg_conv_out, gg_attn_out, None, gg_ffn, None, None, None]
    m_list = [m_g_mix, m_w_in[0], m_conv_w[0], m_g_q, m_g_k, m_g_conv_out, m_g_attn_out, m_w_out[0], m_g_ffn, m_w_gate[0], m_w_up[0], m_w_down[0]]
    v_list = [v_g_mix, v_w_in[0], v_conv_w[0], v_g_q, v_g_k, v_g_conv_out, v_g_attn_out, v_w_out[0], v_g_ffn, v_w_gate[0], v_w_up[0], v_w_down[0]]
    stacked = {"w_in", "conv_w", "w_out", "w_gate", "w_up", "w_down"}
    transposed = {"w_gate", "w_up"}
    results = {}
    operands = {n: (w, g, m, v) for n, w, g, m, v in zip(names, weights, grad_list, m_list, v_list)}
    for name in scattered:
        w, _, m, v = operands[name]
        if name in transposed:
            w, m, v = w.T, m.T, v.T
        results[name] = _adamw_received("adamw_" + name, dev, w, *scattered[name], m, v, token)
        if name in transposed:
            results[name] = [t.T for t in results[name]]
    late_done, landed = _scatter_wait(send_sems, recv_sems, late_thru, landing, [results[n][3] for n in scattered] + [total])
    gw_in = _sum_received(dev, late_done, landed).T
    for name in names:
        if name not in results:
            w, g, m, v = operands[name]
            g = gw_in if name == "w_in" else g
            results[name] = [g, *_adamw("adamw_" + name, w, g, m, v)]
    outs = []
    for part in range(4):
        outs += [results[n][part][None] if n in stacked else results[n][part] for n in names]
    return (loss, grad_x[None], *outs)
```

```python
import jax
import jax.numpy as jnp
import numpy as np
from jax import lax
from jax.experimental import pallas as pl
from jax.experimental.pallas import tpu as pltpu

F32 = jnp.float32
BF16 = jnp.bfloat16

D_MODEL = 1024
D_CONV = 512
D_ATTN = 512
D_IN = 3072
D_FF = 2816
HEAD_DIM = 64
N_PAIRS = 4
BLOCK = 128
DILATIONS = (1, 4, 16)
N_DEV = 8
EPS = 1e-6
NEG = -1e30

ADAM_LR = 0.001
ADAM_B1 = 0.9
ADAM_B2 = 0.999
ADAM_EPS = 1e-08
ADAM_WD = 0.01
ADAM_STEP = 10

NT = (((1,), (1,)), ((), ()))
NN = (((1,), (0,)), ((), ()))
TN = (((0,), (0,)), ((), ()))
MESH = pl.DeviceIdType.MESH

VMEM_LIMIT = 56 * 1024 * 1024


def _params(semantics=None, vmem=None):
    kw = {}
    if semantics is not None:
        kw["dimension_semantics"] = semantics
    if vmem is not None:
        kw["vmem_limit_bytes"] = vmem
    return pltpu.CompilerParams(**kw)


def _dot(a, b, dims):
    return lax.dot_general(a, b, dims, preferred_element_type=F32)


def _const_spec(shape):
    n = len(shape)
    return pl.BlockSpec(shape, lambda *_: (0,) * n, pipeline_mode=pl.Buffered(1))


def _rms(x):
    r = lax.rsqrt(jnp.mean(x * x, axis=-1, keepdims=True) + EPS)
    return x * r, r


def _rms_bwd(dy, xh, r, g):
    t = dy * g
    return r * (t - xh * jnp.mean(t * xh, axis=-1, keepdims=True))


def _head_blockdiag(scale):
    i = lax.broadcasted_iota(jnp.int32, (128, 128), 0) // HEAD_DIM
    j = lax.broadcasted_iota(jnp.int32, (128, 128), 1) // HEAD_DIM
    return jnp.where(i == j, scale, 0.0).astype(BF16)


def _head_sum(x, bd):
    hi = x.astype(BF16)
    lo = (x - hi.astype(F32)).astype(BF16)
    return _dot(hi, bd, NN) + _dot(lo, bd, NN)


def _place():
    return lax.axis_index("x"), lax.axis_index("y"), lax.axis_index("c")


class _TwoLevelGather:
    def __init__(self, stacks, send_sems, recv_sems):
        self.stacks, self.send_sems, self.recv_sems = stacks, send_sems, recv_sems
        x, y, c = _place()
        self.c = c
        self.me, self.sibling = (x, y, c), (x, y, 1 - c)
        self.chips = [(x, 1 - y), (1 - x, y), (1 - x, 1 - y)]

    @staticmethod
    def index(p):
        return 4 * p[0] + 2 * p[1] + p[2]

    def copy(self, a, k, block, to, src=None):
        dst = self.stacks[a].at[self.index(block)]
        return pltpu.make_async_remote_copy(
            src_ref=dst if src is None else src, dst_ref=dst,
            send_sem=self.send_sems.at[7 * a + k], recv_sem=self.recv_sems.at[7 * a + k],
            device_id=to, device_id_type=MESH)

    def first(self, a, src=None):
        cps = [self.copy(a, 0, self.me, self.sibling, src)]
        return cps + [self.copy(a, 1 + j, self.me, (*chip, self.c), src) for j, chip in enumerate(self.chips)]

    def forwards(self, a):
        return [self.copy(a, 4 + j, (*chip, self.c), self.sibling) for j, chip in enumerate(self.chips)]

    def start(self, srcs=None):
        for a in range(len(self.stacks)):
            for cp in self.first(a, None if srcs is None else srcs[a]):
                cp.start()

    def forward(self):
        for a in range(len(self.stacks)):
            for j, (chip, fwd) in enumerate(zip(self.chips, self.forwards(a))):
                self.copy(a, 1 + j, (*chip, self.c), self.me).wait_recv()
                fwd.start()

    def finish(self):
        for a in range(len(self.stacks)):
            self.copy(a, 0, self.sibling, self.me).wait_recv()
            for j, chip in enumerate(self.chips):
                self.copy(a, 4 + j, (*chip, 1 - self.c), self.me).wait_recv()
        for a in range(len(self.stacks)):
            for cp in self.first(a) + self.forwards(a):
                cp.wait_send()


def _gather_first_weights(gathered_parts, gathered_dtypes, cast_parts):
    n, m = len(gathered_parts), len(cast_parts)

    def body(*refs):
        ins, casts_in = refs[:n], refs[n:n + m]
        outs, casts_out = refs[n + m:2 * n + m], refs[2 * n + m:2 * n + 2 * m]
        send_sems, recv_sems = refs[2 * n + 2 * m], refs[2 * n + 2 * m + 1]
        gather = _TwoLevelGather(outs, send_sems, recv_sems)
        for a in range(n):
            outs[a][gather.index(gather.me)] = ins[a][...].astype(outs[a].dtype)
        gather.start()
        for a in range(m):
            casts_out[a][...] = casts_in[a][...].astype(BF16)
        gather.forward()
        gather.finish()

    vm = pl.BlockSpec(memory_space=pltpu.VMEM)
    return pl.pallas_call(
        body, name="weight_all_gather",
        out_shape=[jax.ShapeDtypeStruct((N_DEV,) + p.shape, dt) for p, dt in zip(gathered_parts, gathered_dtypes)]
        + [jax.ShapeDtypeStruct(p.shape, BF16) for p in cast_parts],
        in_specs=[vm] * (n + m), out_specs=[vm] * (n + m),
        scratch_shapes=[pltpu.SemaphoreType.DMA((7 * n,)), pltpu.SemaphoreType.DMA((7 * n,))],
        compiler_params=_params(vmem=VMEM_LIMIT),
    )(*gathered_parts, *cast_parts)


class _DirectReduceScatter:
    def __init__(self, stacks, landing, send_sems, recv_sems):
        self.stacks, self.landing, self.send_sems, self.recv_sems = stacks, landing, send_sems, recv_sems
        self.place = _place()

    def copies(self):
        x, y, c = self.place
        cps = []
        for a in range(len(self.stacks)):
            for k in range(1, N_DEV):
                peer = (1 - x if k & 4 else x, 1 - y if k & 2 else y, 1 - c if k & 1 else c)
                cps.append(pltpu.make_async_remote_copy(
                    src_ref=self.stacks[a].at[4 * peer[0] + 2 * peer[1] + peer[2]], dst_ref=self.landing[a].at[k - 1],
                    send_sem=self.send_sems.at[7 * a + k - 1], recv_sem=self.recv_sems.at[7 * a + k - 1],
                    device_id=peer, device_id_type=MESH))
        return cps

    def start(self):
        for cp in self.copies():
            cp.start()

    def finish(self):
        for cp in self.copies():
            cp.wait_recv()
        for cp in self.copies():
            cp.wait_send()


def _all_reduce_small(v):
    def body(v_ref, out_ref, gathered, send_sems, recv_sems):
        x, y, c = _place()
        mine = 4 * x + 2 * y + c
        gathered[mine] = v_ref[...]

        def copy(k):
            peer = (1 - x if k & 4 else x, 1 - y if k & 2 else y, 1 - c if k & 1 else c)
            return pltpu.make_async_remote_copy(
                src_ref=gathered.at[mine], dst_ref=gathered.at[mine],
                send_sem=send_sems.at[k - 1], recv_sem=recv_sems.at[k - 1], device_id=peer, device_id_type=MESH)

        copies = [copy(k) for k in range(1, N_DEV)]
        for cp in copies:
            cp.start()
        for cp in copies:
            cp.wait_recv()
        for cp in copies:
            cp.wait_send()
        total = gathered[0]
        for d in range(1, N_DEV):
            total = total + gathered[d]
        out_ref[...] = total

    vm = pl.BlockSpec(memory_space=pltpu.VMEM)
    return pl.pallas_call(
        body, name="small_all_reduce",
        out_shape=jax.ShapeDtypeStruct(v.shape, F32),
        in_specs=[vm], out_specs=vm,
        scratch_shapes=[pltpu.VMEM((N_DEV,) + v.shape, F32),
                        pltpu.SemaphoreType.DMA((N_DEV - 1,)), pltpu.SemaphoreType.DMA((N_DEV - 1,))],
    )(v)


def _reduce_scatter_cores(grads):
    n = len(grads)

    def body(*refs):
        ins, outs = refs[:n], refs[n:2 * n]
        send_sems, recv_sems = refs[2 * n], refs[2 * n + 1]
        x, y, c = _place()
        copies = []
        for a in range(n):
            for k in range(4):
                copies.append(pltpu.make_async_remote_copy(
                    src_ref=ins[a].at[k, 1 - c], dst_ref=outs[a].at[k],
                    send_sem=send_sems.at[4 * a + k], recv_sem=recv_sems.at[4 * a + k],
                    device_id=(x, y, 1 - c), device_id_type=MESH))
        for cp in copies:
            cp.start()
        for cp in copies:
            cp.wait_recv()
        for cp in copies:
            cp.wait_send()

    hbm = pl.BlockSpec(memory_space=pl.ANY)
    return pl.pallas_call(
        body, name="grad_reduce_scatter_cores",
        out_shape=[jax.ShapeDtypeStruct((4,) + g.shape[2:], g.dtype) for g in grads],
        in_specs=[hbm] * n, out_specs=[hbm] * n,
        scratch_shapes=[pltpu.SemaphoreType.DMA((4 * n,)), pltpu.SemaphoreType.DMA((4 * n,))],
    )(*grads)


def _reduce_scatter_chips(partials):
    n = len(partials)

    def body(*refs):
        ins, outs = refs[:n], refs[n:2 * n]
        send_sems, recv_sems = refs[2 * n], refs[2 * n + 1]
        x, y, c = _place()
        copies = []
        for a in range(n):
            for j in (1, 2, 3):
                px = 1 - x if j & 2 else x
                py = 1 - y if j & 1 else y
                copies.append(pltpu.make_async_remote_copy(
                    src_ref=ins[a].at[2 * px + py], dst_ref=outs[a].at[j - 1],
                    send_sem=send_sems.at[3 * a + j - 1], recv_sem=recv_sems.at[3 * a + j - 1],
                    device_id=(px, py, c), device_id_type=MESH))
        for cp in copies:
            cp.start()
        for cp in copies:
            cp.wait_recv()
        for cp in copies:
            cp.wait_send()

    hbm = pl.BlockSpec(memory_space=pl.ANY)
    return pl.pallas_call(
        body, name="grad_reduce_scatter_chips",
        out_shape=[jax.ShapeDtypeStruct((3,) + p.shape[1:], p.dtype) for p in partials],
        in_specs=[hbm] * n, out_specs=[hbm] * n,
        scratch_shapes=[pltpu.SemaphoreType.DMA((3 * n,)), pltpu.SemaphoreType.DMA((3 * n,))],
    )(*partials)


def _add_core_partials(core, grads, received):
    n = len(grads)

    def body(core_ref, *refs):
        del core_ref
        gs, rs, outs = refs[:n], refs[n:2 * n], refs[2 * n:]
        for a in range(n):
            outs[a][...] = (gs[a][...].astype(F32) + rs[a][...].astype(F32)).astype(outs[a].dtype)

    in_specs = [pl.BlockSpec((None, None) + g.shape[2:], lambda k, cref: (k, cref[0], 0, 0)) for g in grads]
    in_specs += [pl.BlockSpec((None,) + r.shape[1:], lambda k, cref: (k, 0, 0)) for r in received]
    out_specs = [pl.BlockSpec((None,) + r.shape[1:], lambda k, cref: (k, 0, 0)) for r in received]
    return pl.pallas_call(
        body, name="grad_add_core_partials",
        grid_spec=pltpu.PrefetchScalarGridSpec(num_scalar_prefetch=1, grid=(4,), in_specs=in_specs, out_specs=out_specs),
        out_shape=[jax.ShapeDtypeStruct(r.shape, r.dtype) for r in received],
        compiler_params=_params(("arbitrary",)),
    )(core, *grads, *received)


def _add_chip_partials(chip, partials, received):
    n = len(partials)

    def body(chip_ref, *refs):
        del chip_ref
        ps, rs, outs = refs[:n], refs[n:2 * n], refs[2 * n:]
        for a in range(n):
            outs[a][...] = ((ps[a][...].astype(F32) + rs[a][0].astype(F32)) + rs[a][1].astype(F32)) + rs[a][2].astype(F32)

    in_specs = [pl.BlockSpec((None,) + p.shape[1:], lambda i, cref: (cref[0], 0, 0)) for p in partials]
    in_specs += [pl.BlockSpec(r.shape, lambda i, cref: (0, 0, 0)) for r in received]
    out_specs = [pl.BlockSpec(p.shape[1:], lambda i, cref: (0, 0)) for p in partials]
    return pl.pallas_call(
        body, name="grad_add_chip_partials",
        grid_spec=pltpu.PrefetchScalarGridSpec(num_scalar_prefetch=1, grid=(1,), in_specs=in_specs, out_specs=out_specs),
        out_shape=[jax.ShapeDtypeStruct(p.shape[1:], F32) for p in partials],
        compiler_params=_params(("arbitrary",), vmem=VMEM_LIMIT),
    )(chip, *partials, *received)


def _in_projection(x, g_mix, w_in_t, tm):
    s = x.shape[0]

    def body(x_ref, g_ref, w_ref, h_ref, z_ref):
        xh, _ = _rms(x_ref[...])
        h = (xh * g_ref[...]).astype(BF16)
        h_ref[...] = h
        for n0 in range(0, D_IN, 512):
            z_ref[:, n0:n0 + 512] = _dot(h, w_ref[n0:n0 + 512, :], NT).astype(BF16)

    return pl.pallas_call(
        body, name="in_projection", grid=(s // tm,),
        in_specs=[pl.BlockSpec((tm, D_MODEL), lambda i: (i, 0)), _const_spec((1, D_MODEL)), _const_spec((D_IN, D_MODEL))],
        out_specs=[pl.BlockSpec((tm, D_MODEL), lambda i: (i, 0)), pl.BlockSpec((tm, D_IN), lambda i: (i, 0))],
        out_shape=[jax.ShapeDtypeStruct((s, D_MODEL), BF16), jax.ShapeDtypeStruct((s, D_IN), BF16)],
        compiler_params=_params(("parallel",), vmem=VMEM_LIMIT),
    )(x, g_mix, w_in_t)


RESIDUES = 16
CHUNK = 512
GROUP = 8


def _branch_geometry(d, seg):
    n_runs = RESIDUES // d
    w = BLOCK // n_runs
    blocks = seg // w
    group = min(GROUP, blocks)
    chains = min(GROUP // group, d)
    return n_runs, w, blocks // group, group, chains


def _permuted_masks():
    out = []
    for d in DILATIONS:
        n_runs = RESIDUES // d
        w = BLOCK // n_runs
        p = np.arange(BLOCK)
        pos = (p % w) * n_runs + p // w
        dist = pos[:, None] - np.concatenate([pos - BLOCK, pos])[None, :]
        band = (dist >= 0) & (dist <= BLOCK)
        first = band & (np.arange(2 * BLOCK)[None, :] >= BLOCK)
        both = [np.where(m, 0.0, NEG).astype(np.float32) for m in (band, first)]
        out.append(np.stack([np.concatenate([m, m], axis=0) for m in both]))
    return jnp.asarray(np.stack(out))


def _run_starts(u, grp, d, seg, n_runs, w, group):
    return [(u + d * q) * seg + (w * group) * grp for q in range(n_runs)]


def _tile_rows(start, w, b):
    off = start + w * b
    if b < 0:
        off = jnp.maximum(off, 0)
    return pl.ds(pl.multiple_of(off, w), w)


def _load_tile(ref, starts, w, b):
    parts = [ref[_tile_rows(st, w, b), :] for st in starts]
    return parts[0] if len(parts) == 1 else jnp.concatenate(parts, axis=0)


def _store_tile(ref, starts, w, b, tile):
    for i, st in enumerate(starts):
        ref[_tile_rows(st, w, b), :] = tile[i * w:(i + 1) * w]


def _add_tile(ref, starts, w, b, tile):
    for i, st in enumerate(starts):
        ref[_tile_rows(st, w, b), :] += tile[i * w:(i + 1) * w]


def _to_residue_major(dst, src, c, seg):
    per = CHUNK // RESIDUES
    for r in range(RESIDUES):
        dst[pl.ds(pl.multiple_of(r * seg + c * per, per), per), :] = src[pl.ds(r, per, stride=RESIDUES), :]


def _from_residue_major(dst, dst_start, src, c, seg):
    per = CHUNK // RESIDUES
    for r in range(RESIDUES):
        dst[pl.ds(dst_start + r, per, stride=RESIDUES), :] = src[pl.ds(pl.multiple_of(r * seg + c * per, per), per), :]


def _stack_heads(t, head0):
    return jnp.concatenate([jnp.where(head0, t, 0.0), jnp.where(head0, 0.0, t)], axis=0).astype(BF16)


def _attention_forward(z, gq2, gk2, masks, shards):
    s = z.shape[0]
    seg = s // RESIDUES
    n_w = len(shards)

    def body(zq_ref, zk_ref, zv_ref, gq_ref, gk_ref, mask_ref, *rest):
        shard_refs, (o_ref, lse_ref), stacks = rest[:n_w], rest[n_w:n_w + 2], rest[n_w + 2:2 * n_w + 2]
        qf, kf, vf, o_st, l_st, tq, tk, tv, send_sems, recv_sems, local_sems = rest[2 * n_w + 2:]
        pair = pl.program_id(0)
        gather = _TwoLevelGather(stacks, send_sems, recv_sems)

        def own_copies():
            mine = gather.index(gather.me)
            return [pltpu.make_async_copy(shard_refs[a], stacks[a].at[mine], local_sems.at[a]) for a in range(n_w)]

        @pl.when(pair == 0)
        def _():
            for cp in own_copies():
                cp.start()
            gather.start(shard_refs)

        @pl.when(pair == 2)
        def _():
            gather.forward()

        bd = _head_blockdiag(1.0 / HEAD_DIM)
        gq = gq_ref[...] * (HEAD_DIM ** -0.5)
        gk = gk_ref[...]

        def prep(c, carry):
            rows = pl.ds(pl.multiple_of(c * CHUNK, CHUNK), CHUNK)
            zq = zq_ref[rows, :].astype(F32)
            zk = zk_ref[rows, :].astype(F32)
            tq[...] = (zq * lax.rsqrt(_head_sum(zq * zq, bd) + EPS)) * gq
            tk[...] = (zk * lax.rsqrt(_head_sum(zk * zk, bd) + EPS)) * gk
            tv[...] = zv_ref[rows, :].astype(F32)
            _to_residue_major(qf, tq, c, seg)
            _to_residue_major(kf, tk, c, seg)
            _to_residue_major(vf, tv, c, seg)
            return carry

        lax.fori_loop(0, s // CHUNK, prep, 0)

        head0 = lax.broadcasted_iota(jnp.int32, (BLOCK, 128), 1) < HEAD_DIM

        def run_branch(branch, d):
            n_runs, w, steps, group, chains = _branch_geometry(d, seg)
            slots = [(ci, b) for ci in range(chains) for b in range(group)]

            def step(u, grp):
                starts = [_run_starts(u + ci, grp, d, seg, n_runs, w, group) for ci in range(chains)]
                kt = [[_load_tile(kf, st, w, b).astype(BF16) for b in range(-1, group)] for st in starts]
                vt = [[_load_tile(vf, st, w, b).astype(BF16) for b in range(-1, group)] for st in starts]
                scores = []
                for ci, b in slots:
                    q2 = _stack_heads(_load_tile(qf, starts[ci], w, b), head0)
                    kc = jnp.concatenate([kt[ci][b], kt[ci][b + 1]], axis=0)
                    mask = mask_ref[branch, 0]
                    if b == 0:
                        mask = jnp.where(grp == 0, mask_ref[branch, 1], mask)
                    scores.append(_dot(q2, kc, NT) + mask)
                sc = jnp.concatenate(scores, axis=0)
                m = jnp.max(sc, axis=-1, keepdims=True)
                e = jnp.exp(sc - m).astype(BF16)
                ones = jnp.ones((BLOCK, 128), BF16)
                fresh = []
                for n, (ci, b) in enumerate(slots):
                    rows = slice(2 * BLOCK * n, 2 * BLOCK * (n + 1))
                    v_aug = jnp.concatenate([jnp.concatenate([vt[ci][b], ones], axis=1),
                                             jnp.concatenate([vt[ci][b + 1], ones], axis=1)], axis=0)
                    o2 = _dot(e[rows], v_aug, NN)
                    den = o2[:, 128:]
                    o2 = o2[:, :128] * (1.0 / den)
                    l2 = m[rows] + jnp.log(den)
                    fresh.append((jnp.where(head0, o2[:BLOCK], o2[BLOCK:]), jnp.where(head0, l2[:BLOCK], l2[BLOCK:])))
                for n, (ci, b) in enumerate(slots):
                    o_new, l_new = fresh[n]
                    if branch > 0:
                        l_old = _load_tile(l_st, starts[ci], w, b)
                        top = jnp.maximum(l_old, l_new)
                        e_old = jnp.exp(l_old - top)
                        e_new = jnp.exp(l_new - top)
                        tot = e_old + e_new
                        inv = 1.0 / tot
                        o_new = _load_tile(o_st, starts[ci], w, b) * (e_old * inv) + o_new * (e_new * inv)
                        l_new = top + jnp.log(tot)
                    _store_tile(o_st, starts[ci], w, b, o_new)
                    _store_tile(l_st, starts[ci], w, b, l_new)

            def unit(it, carry):
                def inner(grp, carry2):
                    step(it * chains, grp)
                    return carry2
                return lax.fori_loop(0, steps, inner, carry)

            lax.fori_loop(0, d // chains, unit, 0)

        for branch, d in enumerate(DILATIONS):
            run_branch(branch, d)

        def finish(c, carry):
            _from_residue_major(o_ref, c * CHUNK, o_st, c, seg)
            _from_residue_major(lse_ref, c * CHUNK, l_st, c, seg)
            return carry

        lax.fori_loop(0, s // CHUNK, finish, 0)

        @pl.when(pair == N_PAIRS - 1)
        def _():
            gather.finish()
            for cp in own_copies():
                cp.wait()

    def col(c0):
        return pl.BlockSpec((s, 128), lambda p: (0, c0 + p), pipeline_mode=pl.Buffered(1))

    hbm = pl.BlockSpec(memory_space=pl.ANY)
    out = pl.pallas_call(
        body, name="attention_forward", grid=(N_PAIRS,),
        in_specs=[col(12), col(16), col(20), _const_spec((1, 128)), _const_spec((1, 128)), _const_spec(masks.shape)]
        + [hbm] * n_w,
        out_specs=[pl.BlockSpec((s, 128), lambda p: (0, p)), pl.BlockSpec((s, 128), lambda p: (0, p))] + [hbm] * n_w,
        out_shape=[jax.ShapeDtypeStruct((s, D_ATTN), F32), jax.ShapeDtypeStruct((s, D_ATTN), F32)]
        + [jax.ShapeDtypeStruct((N_DEV,) + w.shape, w.dtype) for w in shards],
        scratch_shapes=[pltpu.VMEM((s, 128), F32)] * 5 + [pltpu.VMEM((CHUNK, 128), F32)] * 3
        + [pltpu.SemaphoreType.DMA((7 * n_w,)), pltpu.SemaphoreType.DMA((7 * n_w,)), pltpu.SemaphoreType.DMA((n_w,))],
        compiler_params=_params(("arbitrary",), vmem=VMEM_LIMIT),
    )(z, z, z, gq2, gk2, masks, *shards)
    return out[0], out[1], out[2:]


def _conv_forward(zt, zprev, w, first):
    u, gb, gc = zt[:, :D_CONV], zt[:, D_CONV:2 * D_CONV], zt[:, 2 * D_CONV:]
    cu = gc * u
    cu_prev = jnp.where(first, 0.0, zprev[:, 2 * D_CONV:] * zprev[:, :D_CONV])
    row = lax.broadcasted_iota(jnp.int32, cu.shape, 0)
    cu1 = jnp.where(row >= 1, pltpu.roll(cu, 1, 0), cu_prev[7:8, :])
    cu2 = jnp.where(row >= 2, pltpu.roll(cu, 2, 0), jnp.where(row == 1, cu_prev[7:8, :], cu_prev[6:7, :]))
    cv = w[0:1, :] * cu2 + w[1:2, :] * cu1 + w[2:3, :] * cu
    return u, gb, gc, cu, cu1, cu2, cv


def _mix_out(z, y_attn, x, conv_w, g_conv_out, g_attn_out, g_ffn, w_out, tm):
    s = x.shape[0]
    hb = tm // 8

    def body(z_ref, zp_ref, ya_ref, x_ref, cw_ref, gc_ref, ga_ref, gf_ref, w_ref, mix_ref, x2_ref, h2_ref):
        first = pl.program_id(0) == 0
        zt = z_ref[...].astype(F32)
        _, gb, _, _, _, _, cv = _conv_forward(zt, zp_ref[...].astype(F32), cw_ref[...], first)
        nc, _ = _rms(gb * cv)
        na, _ = _rms(ya_ref[...])
        mix = jnp.concatenate([nc * gc_ref[...], na * ga_ref[...]], axis=-1).astype(BF16)
        mix_ref[...] = mix
        x2 = x_ref[...] + _dot(mix, w_ref[...], NN)
        x2_ref[...] = x2
        xh, _ = _rms(x2)
        h2_ref[...] = (xh * gf_ref[...]).astype(BF16)

    tile = lambda w: pl.BlockSpec((tm, w), lambda i: (i, 0))
    return pl.pallas_call(
        body, name="mix_out", grid=(s // tm,),
        in_specs=[tile(3 * D_CONV), pl.BlockSpec((8, 3 * D_CONV), lambda i: (jnp.maximum(i * hb - 1, 0), 0)),
                  tile(D_ATTN), tile(D_MODEL), _const_spec((8, D_CONV)), _const_spec((1, D_CONV)), _const_spec((1, D_ATTN)),
                  _const_spec((1, D_MODEL)), _const_spec((D_MODEL, D_MODEL))],
        out_specs=[tile(D_MODEL)] * 3,
        out_shape=[jax.ShapeDtypeStruct((s, D_MODEL), BF16), jax.ShapeDtypeStruct((s, D_MODEL), F32),
                   jax.ShapeDtypeStruct((s, D_MODEL), BF16)],
        compiler_params=_params(("arbitrary",), vmem=VMEM_LIMIT),
    )(z, z, y_attn, x, conv_w, g_conv_out, g_attn_out, g_ffn, w_out)


FF_CHUNK = 256


def _ffn_forward(h2, x2, target, w_gate_t, w_up_t, w_down, tm):
    s = h2.shape[0]

    def body(h_ref, x2_ref, t_ref, wg_ref, wu_ref, wd_ref, a_ref, b_ref, dy_ref, dyb_ref, loss_ref, f_buf):
        @pl.when(pl.program_id(0) == 0)
        def _():
            loss_ref[...] = jnp.zeros_like(loss_ref)

        h = h_ref[...]
        for c0 in range(0, D_FF, FF_CHUNK):
            rows = slice(c0, c0 + FF_CHUNK)
            a = _dot(h, wg_ref[rows, :], NT)
            b = _dot(h, wu_ref[rows, :], NT)
            a_ref[:, rows] = a.astype(BF16)
            b_ref[:, rows] = b.astype(BF16)
            f_buf[:, rows] = ((a * jax.nn.sigmoid(a)) * b).astype(BF16)
        y = x2_ref[...] + _dot(f_buf[...], wd_ref[...], NN)
        err = y - t_ref[...]
        dy = err * (1.0 / D_MODEL)
        dy_ref[...] = dy
        dyb_ref[...] = dy.astype(BF16)
        loss_ref[...] += jnp.sum(err * err)

    tile = lambda w: pl.BlockSpec((tm, w), lambda i: (i, 0))
    return pl.pallas_call(
        body, name="ffn_forward", grid=(s // tm,),
        in_specs=[tile(D_MODEL), tile(D_MODEL), tile(D_MODEL)] + [_const_spec((D_FF, D_MODEL))] * 3,
        out_specs=[tile(D_FF), tile(D_FF), tile(D_MODEL), tile(D_MODEL), pl.BlockSpec((8, 128), lambda i: (0, 0))],
        out_shape=[jax.ShapeDtypeStruct((s, D_FF), BF16), jax.ShapeDtypeStruct((s, D_FF), BF16),
                   jax.ShapeDtypeStruct((s, D_MODEL), F32), jax.ShapeDtypeStruct((s, D_MODEL), BF16),
                   jax.ShapeDtypeStruct((8, 128), F32)],
        scratch_shapes=[pltpu.VMEM((tm, D_FF), BF16)],
        compiler_params=_params(("arbitrary",), vmem=VMEM_LIMIT),
    )(h2, x2, target, w_gate_t, w_up_t, w_down)


def _ffn_backward(dy, a, b, x2, g_ffn, w_gate_t, w_up_t, w_down, tm):
    s = dy.shape[0]
    half = D_FF // 2
    pieces = [(p0, min(FF_CHUNK, half - p0)) for p0 in range(0, half, FF_CHUNK)]

    def body(dy_ref, a_ref, b_ref, x2_ref, g_ref, wg_ref, wu_ref, wd_ref, da_ref, db_ref, dx2_ref, dx2b_ref, dg_ref, acc):
        i, c = pl.program_id(0), pl.program_id(1)

        @pl.when((i == 0) & (c == 0))
        def _():
            dg_ref[...] = jnp.zeros_like(dg_ref)

        dyb = dy_ref[...].astype(BF16)
        for p0, width in pieces:
            rows = slice(p0, p0 + width)
            df = _dot(dyb, wd_ref[rows, :], NT)
            av = a_ref[:, rows].astype(F32)
            bv = b_ref[:, rows].astype(F32)
            sig = jax.nn.sigmoid(av)
            silu = av * sig
            da_ref[:, rows] = (df * bv * (sig * (1.0 + av * (1.0 - sig)))).astype(BF16)
            db_ref[:, rows] = (df * silu).astype(BF16)
        dh = _dot(da_ref[...], wg_ref[...], NN) + _dot(db_ref[...], wu_ref[...], NN)

        @pl.when(c == 0)
        def _():
            acc[...] = dh

        @pl.when(c == 1)
        def _():
            dh_all = acc[...] + dh
            xh, r = _rms(x2_ref[...])
            dg_ref[...] += jnp.sum(dh_all * xh, axis=0, keepdims=True)
            dx2 = dy_ref[...] + _rms_bwd(dh_all, xh, r, g_ref[...])
            dx2_ref[...] = dx2
            dx2b_ref[...] = dx2.astype(BF16)

    tile = lambda w: pl.BlockSpec((tm, w), lambda i, c: (i, 0))
    part = pl.BlockSpec((tm, half), lambda i, c: (i, c))
    weight = pl.BlockSpec((half, D_MODEL), lambda i, c: (c, 0))
    return pl.pallas_call(
        body, name="ffn_backward", grid=(s // tm, 2),
        in_specs=[tile(D_MODEL), part, part, tile(D_MODEL), _const_spec((1, D_MODEL)), weight, weight, weight],
        out_specs=[part, part, tile(D_MODEL), tile(D_MODEL), pl.BlockSpec((1, D_MODEL), lambda i, c: (0, 0))],
        out_shape=[jax.ShapeDtypeStruct((s, D_FF), BF16)] * 2
        + [jax.ShapeDtypeStruct((s, D_MODEL), F32), jax.ShapeDtypeStruct((s, D_MODEL), BF16),
           jax.ShapeDtypeStruct((1, D_MODEL), F32)],
        scratch_shapes=[pltpu.VMEM((tm, D_MODEL), F32)],
        compiler_params=_params(("arbitrary", "arbitrary"), vmem=VMEM_LIMIT),
    )(dy, a, b, x2, g_ffn, w_gate_t, w_up_t, w_down)


def _out_backward(dx2, z, y_attn, conv_w, g_conv_out, g_attn_out, w_out, tm):
    s = dx2.shape[0]
    hb = tm // 8

    def body(dx2_ref, z_ref, zp_ref, ya_ref, cw_ref, gc_ref, ga_ref, w_ref, dya_ref, dgb_ref, dcv_ref, dgc_ref, dga_ref):
        first = pl.program_id(0) == 0

        @pl.when(first)
        def _():
            dgc_ref[...] = jnp.zeros_like(dgc_ref)
            dga_ref[...] = jnp.zeros_like(dga_ref)

        dmix = _dot(dx2_ref[...], w_ref[...], NT)
        _, gb, _, _, _, _, cv = _conv_forward(z_ref[...].astype(F32), zp_ref[...].astype(F32), cw_ref[...], first)
        ych, rc = _rms(gb * cv)
        dnc = dmix[:, :D_CONV]
        dgc_ref[...] += jnp.sum(dnc * ych, axis=0, keepdims=True)
        dyc = _rms_bwd(dnc, ych, rc, gc_ref[...])
        dgb_ref[...] = (dyc * cv).astype(BF16)
        dcv_ref[...] = dyc * gb
        yah, ra = _rms(ya_ref[...])
        dna = dmix[:, D_CONV:]
        dga_ref[...] += jnp.sum(dna * yah, axis=0, keepdims=True)
        dya_ref[...] = _rms_bwd(dna, yah, ra, ga_ref[...])

    tile = lambda w: pl.BlockSpec((tm, w), lambda i: (i, 0))
    vec = pl.BlockSpec((1, D_CONV), lambda i: (0, 0))
    return pl.pallas_call(
        body, name="out_backward", grid=(s // tm,),
        in_specs=[tile(D_MODEL), tile(3 * D_CONV), pl.BlockSpec((8, 3 * D_CONV), lambda i: (jnp.maximum(i * hb - 1, 0), 0)),
                  tile(D_ATTN), _const_spec((8, D_CONV)), _const_spec((1, D_CONV)), _const_spec((1, D_ATTN)),
                  _const_spec((D_MODEL, D_MODEL))],
        out_specs=[tile(D_ATTN), tile(D_CONV), tile(D_CONV), vec, vec],
        out_shape=[jax.ShapeDtypeStruct((s, D_ATTN), F32), jax.ShapeDtypeStruct((s, D_CONV), BF16),
                   jax.ShapeDtypeStruct((s, D_CONV), F32), jax.ShapeDtypeStruct((1, D_CONV), F32),
                   jax.ShapeDtypeStruct((1, D_ATTN), F32)],
        compiler_params=_params(("arbitrary",), vmem=VMEM_LIMIT),
    )(dx2, z, z, y_attn, conv_w, g_conv_out, g_attn_out, w_out)


def _attention_backward(z, o, lse, dya, gq2, gk2, masks, grad_stacks):
    s = z.shape[0]
    seg = s // RESIDUES
    n_g = len(grad_stacks)

    def body(z_hbm, o_hbm, lse_hbm, do_hbm, gq_ref, gk_ref, mask_ref, *rest):
        stack_refs = rest[:n_g]
        dzq_ref, dzk_ref, dzv_ref, dgq_ref, dgk_ref = rest[n_g:n_g + 5]
        landing = rest[n_g + 5:2 * n_g + 5]
        (qf, kf, vf, dof, stats, dq, dk, dv, t0, t1, t2, zbuf, fbuf, sems, send_sems, recv_sems) = rest[2 * n_g + 5:]
        pair = pl.program_id(0)
        scatter = _DirectReduceScatter(stack_refs, landing, send_sems, recv_sems)

        @pl.when(pair == 0)
        def _():
            scatter.start()

        n_chunks = s // CHUNK
        bd_mean = _head_blockdiag(1.0 / HEAD_DIM)
        bd_sum = _head_blockdiag(1.0)
        gq = gq_ref[...] * (HEAD_DIM ** -0.5)
        gk = gk_ref[...]
        lane = lax.broadcasted_iota(jnp.int32, (CHUNK, 128), 1) % HEAD_DIM

        def chunk_copies(c, slot, n_z, n_f):
            rows = pl.ds(pl.multiple_of(c * CHUNK, CHUNK), CHUNK)
            cps = []
            for i, col in enumerate((12, 16, 20)[:n_z]):
                cols = pl.ds(pl.multiple_of((col + pair) * 128, 128), 128)
                cps.append(pltpu.make_async_copy(z_hbm.at[rows, cols], zbuf.at[slot, i], sems.at[slot, i]))
            for i, src in enumerate((do_hbm, o_hbm, lse_hbm)[:n_f]):
                cols = pl.ds(pl.multiple_of(pair * 128, 128), 128)
                cps.append(pltpu.make_async_copy(src.at[rows, cols], fbuf.at[slot, i], sems.at[slot, 3 + i]))
            return cps

        def prefetched(c, n_z, n_f):
            slot = c % 2

            @pl.when(c + 1 < n_chunks)
            def _():
                for cp in chunk_copies(c + 1, 1 - slot, n_z, n_f):
                    cp.start()

            for cp in chunk_copies(c, slot, n_z, n_f):
                cp.wait()
            return slot

        for cp in chunk_copies(0, 0, 3, 3):
            cp.start()

        def prep(c, carry):
            slot = prefetched(c, 3, 3)
            rows = pl.ds(pl.multiple_of(c * CHUNK, CHUNK), CHUNK)
            zq = zbuf[slot, 0].astype(F32)
            t0[...] = (zq * lax.rsqrt(_head_sum(zq * zq, bd_mean) + EPS)) * gq
            zk = zbuf[slot, 1].astype(F32)
            t1[...] = (zk * lax.rsqrt(_head_sum(zk * zk, bd_mean) + EPS)) * gk
            t2[...] = zbuf[slot, 2].astype(F32)
            _to_residue_major(qf, t0, c, seg)
            _to_residue_major(kf, t1, c, seg)
            _to_residue_major(vf, t2, c, seg)
            _to_residue_major(dof, fbuf.at[slot, 0], c, seg)
            delta = _head_sum(fbuf[slot, 0] * fbuf[slot, 1], bd_sum)
            t0[...] = jnp.where(lane == 0, fbuf[slot, 2], jnp.where(lane == 1, delta, 0.0))
            _to_residue_major(stats, t0, c, seg)
            zero = jnp.zeros((CHUNK, 128), F32)
            dq[rows, :] = zero
            dk[rows, :] = zero
            dv[rows, :] = zero
            return carry

        lax.fori_loop(0, n_chunks, prep, 0)

        head0 = lax.broadcasted_iota(jnp.int32, (BLOCK, 128), 1) < HEAD_DIM

        def run_branch(branch, d):
            n_runs, w, steps, group, chains = _branch_geometry(d, seg)
            slots = [(ci, b) for ci in range(chains) for b in range(group)]

            def step(u, grp):
                starts = [_run_starts(u + ci, grp, d, seg, n_runs, w, group) for ci in range(chains)]
                kt = [[_load_tile(kf, st, w, b).astype(BF16) for b in range(-1, group)] for st in starts]
                vt = [[_load_tile(vf, st, w, b).astype(BF16) for b in range(-1, group)] for st in starts]
                dk_t = [[jnp.zeros((BLOCK, 128), F32)] * (group + 1) for _ in range(chains)]
                dv_t = [[jnp.zeros((BLOCK, 128), F32)] * (group + 1) for _ in range(chains)]
                dq_t = []
                q2s, do2s, kcs, scores, dps, lses, deltas = [], [], [], [], [], [], []
                for ci, b in slots:
                    q2 = _stack_heads(_load_tile(qf, starts[ci], w, b), head0)
                    do2 = _stack_heads(_load_tile(dof, starts[ci], w, b), head0)
                    st = _load_tile(stats, starts[ci], w, b)
                    lses += [st[:, 0:1], st[:, HEAD_DIM:HEAD_DIM + 1]]
                    deltas += [st[:, 1:2], st[:, HEAD_DIM + 1:HEAD_DIM + 2]]
                    kc = jnp.concatenate([kt[ci][b], kt[ci][b + 1]], axis=0)
                    vc = jnp.concatenate([vt[ci][b], vt[ci][b + 1]], axis=0)
                    mask = mask_ref[branch, 0]
                    if b == 0:
                        mask = jnp.where(grp == 0, mask_ref[branch, 1], mask)
                    scores.append(_dot(q2, kc, NT) + mask)
                    dps.append(_dot(do2, vc, NT))
                    q2s.append(q2)
                    do2s.append(do2)
                    kcs.append(kc)
                p_all = jnp.exp(jnp.concatenate(scores, axis=0) - jnp.concatenate(lses, axis=0))
                ds_all = (p_all * (jnp.concatenate(dps, axis=0) - jnp.concatenate(deltas, axis=0))).astype(BF16)
                p_all = p_all.astype(BF16)
                for n, (ci, b) in enumerate(slots):
                    rows = slice(2 * BLOCK * n, 2 * BLOCK * (n + 1))
                    ds, q2, do2, kc = ds_all[rows], q2s[n], do2s[n], kcs[n]
                    dq2 = _dot(ds, kc, NN)
                    dq_t.append(jnp.where(head0, dq2[:BLOCK], dq2[BLOCK:]))
                    dkc = _dot(ds, q2, TN)
                    dvc = _dot(p_all[rows], do2, TN)
                    dk_t[ci][b] = dk_t[ci][b] + dkc[:BLOCK]
                    dk_t[ci][b + 1] = dk_t[ci][b + 1] + dkc[BLOCK:]
                    dv_t[ci][b] = dv_t[ci][b] + dvc[:BLOCK]
                    dv_t[ci][b + 1] = dv_t[ci][b + 1] + dvc[BLOCK:]
                for n, (ci, b) in enumerate(slots):
                    _add_tile(dq, starts[ci], w, b, dq_t[n])
                for ci in range(chains):
                    for b in range(-1, group):
                        _add_tile(dk, starts[ci], w, b, dk_t[ci][b + 1])
                        _add_tile(dv, starts[ci], w, b, dv_t[ci][b + 1])

            def unit(it, carry):
                def inner(grp, carry2):
                    step(it * chains, grp)
                    return carry2
                return lax.fori_loop(0, steps, inner, carry)

            lax.fori_loop(0, d // chains, unit, 0)

        for branch, d in enumerate(DILATIONS):
            run_branch(branch, d)

        for cp in chunk_copies(0, 0, 2, 0):
            cp.start()

        def finish(c, carry):
            acc_q, acc_k = carry
            rows = pl.ds(pl.multiple_of(c * CHUNK, CHUNK), CHUNK)
            _from_residue_major(t0, 0, dq, c, seg)
            _from_residue_major(t1, 0, dk, c, seg)
            _from_residue_major(t2, 0, dv, c, seg)
            slot = prefetched(c, 2, 0)
            zq = zbuf[slot, 0].astype(F32)
            rq = lax.rsqrt(_head_sum(zq * zq, bd_mean) + EPS)
            qh = zq * rq
            dqn = t0[...]
            acc_q = acc_q + jnp.sum(dqn * qh, axis=0, keepdims=True)
            t = dqn * gq
            dzq_ref[rows, :] = (rq * (t - qh * _head_sum(t * qh, bd_mean))).astype(BF16)
            zk = zbuf[slot, 1].astype(F32)
            rk = lax.rsqrt(_head_sum(zk * zk, bd_mean) + EPS)
            kh = zk * rk
            dkn = t1[...]
            acc_k = acc_k + jnp.sum(dkn * kh, axis=0, keepdims=True)
            t = dkn * gk
            dzk_ref[rows, :] = (rk * (t - kh * _head_sum(t * kh, bd_mean))).astype(BF16)
            dzv_ref[rows, :] = t2[...].astype(BF16)
            return acc_q, acc_k

        zero = jnp.zeros((1, 128), F32)
        acc_q, acc_k = lax.fori_loop(0, s // CHUNK, finish, (zero, zero))
        dgq_ref[...] = acc_q * (HEAD_DIM ** -0.5)
        dgk_ref[...] = acc_k

        @pl.when(pair == N_PAIRS - 1)
        def _():
            scatter.finish()

    hbm = pl.BlockSpec(memory_space=pl.ANY)
    gain = pl.BlockSpec((None, 1, 128), lambda p: (p, 0, 0))
    dz_spec = pl.BlockSpec((s, 128), lambda p: (0, p))
    out = pl.pallas_call(
        body, name="attention_backward", grid=(N_PAIRS,),
        in_specs=[hbm, hbm, hbm, hbm, _const_spec((1, 128)), _const_spec((1, 128)), _const_spec(masks.shape)] + [hbm] * n_g,
        out_specs=[dz_spec, dz_spec, dz_spec, gain, gain] + [hbm] * n_g,
        out_shape=[jax.ShapeDtypeStruct((s, D_ATTN), BF16)] * 3 + [jax.ShapeDtypeStruct((N_PAIRS, 1, 128), F32)] * 2
        + [jax.ShapeDtypeStruct((N_DEV - 1,) + g.shape[1:], g.dtype) for g in grad_stacks],
        scratch_shapes=[pltpu.VMEM((s, 128), F32)] * 8 + [pltpu.VMEM((CHUNK, 128), F32)] * 3
        + [pltpu.VMEM((2, 3, CHUNK, 128), BF16), pltpu.VMEM((2, 3, CHUNK, 128), F32), pltpu.SemaphoreType.DMA((2, 6)),
           pltpu.SemaphoreType.DMA((7 * n_g,)), pltpu.SemaphoreType.DMA((7 * n_g,))],
        compiler_params=_params(("arbitrary",), vmem=VMEM_LIMIT),
    )(z, o, lse, dya, gq2, gk2, masks, *grad_stacks)
    return out[:5], out[5:]


def _in_backward(dcv, dgb, dzq, dzk, dzv, z, x, dx2, g_mix, conv_w, w_in_t, tm):
    s = x.shape[0]
    hb = tm // 8
    last_halo = s // 8 - 1

    def body(dcv_ref, dcn_ref, dgb_ref, dzq_ref, dzk_ref, dzv_ref, z_ref, zp_ref, x_ref, dx2_ref, g_ref, cw_ref, w_ref,
             dz_ref, dx_ref, dg_ref, dcw_ref):
        i = pl.program_id(0)
        first = i == 0
        last = i == pl.num_programs(0) - 1

        @pl.when(first)
        def _():
            dg_ref[...] = jnp.zeros_like(dg_ref)
            dcw_ref[...] = jnp.zeros_like(dcw_ref)

        w = cw_ref[...]
        u, _, gc, cu, cu1, cu2, _ = _conv_forward(z_ref[...].astype(F32), zp_ref[...].astype(F32), w, first)
        dcv_t = dcv_ref[...]
        nxt = jnp.where(last, 0.0, dcn_ref[...])
        row = lax.broadcasted_iota(jnp.int32, dcv_t.shape, 0)
        up1 = jnp.where(row < tm - 1, pltpu.roll(dcv_t, tm - 1, 0), nxt[0:1, :])
        up2 = jnp.where(row < tm - 2, pltpu.roll(dcv_t, tm - 2, 0), jnp.where(row == tm - 2, nxt[0:1, :], nxt[1:2, :]))
        dcu = w[2:3, :] * dcv_t + w[1:2, :] * up1 + w[0:1, :] * up2
        dcw = jnp.concatenate([jnp.sum(dcv_t * cu2, axis=0, keepdims=True), jnp.sum(dcv_t * cu1, axis=0, keepdims=True),
                               jnp.sum(dcv_t * cu, axis=0, keepdims=True), jnp.zeros((5, D_CONV), F32)], axis=0)
        dcw_ref[...] += dcw
        dz_ref[:, :D_CONV] = (dcu * gc).astype(BF16)
        dz_ref[:, D_CONV:2 * D_CONV] = dgb_ref[...]
        dz_ref[:, 2 * D_CONV:3 * D_CONV] = (dcu * u).astype(BF16)
        dz_ref[:, 3 * D_CONV:3 * D_CONV + D_ATTN] = dzq_ref[...]
        dz_ref[:, 3 * D_CONV + D_ATTN:3 * D_CONV + 2 * D_ATTN] = dzk_ref[...]
        dz_ref[:, 3 * D_CONV + 2 * D_ATTN:] = dzv_ref[...]
        dh = _dot(dz_ref[...], w_ref[...], NN)
        xh, r = _rms(x_ref[...])
        dg_ref[...] += jnp.sum(dh * xh, axis=0, keepdims=True)
        dx_ref[...] = dx2_ref[...] + _rms_bwd(dh, xh, r, g_ref[...])

    tile = lambda w: pl.BlockSpec((tm, w), lambda i: (i, 0))
    return pl.pallas_call(
        body, name="in_backward", grid=(s // tm,),
        in_specs=[tile(D_CONV), pl.BlockSpec((8, D_CONV), lambda i: (jnp.minimum((i + 1) * hb, last_halo), 0)),
                  tile(D_CONV), tile(D_ATTN), tile(D_ATTN), tile(D_ATTN),
                  tile(3 * D_CONV), pl.BlockSpec((8, 3 * D_CONV), lambda i: (jnp.maximum(i * hb - 1, 0), 0)),
                  tile(D_MODEL), tile(D_MODEL), _const_spec((1, D_MODEL)), _const_spec((8, D_CONV)),
                  _const_spec((D_IN, D_MODEL))],
        out_specs=[tile(D_IN), tile(D_MODEL), pl.BlockSpec((1, D_MODEL), lambda i: (0, 0)),
                   pl.BlockSpec((8, D_CONV), lambda i: (0, 0))],
        out_shape=[jax.ShapeDtypeStruct((s, D_IN), BF16), jax.ShapeDtypeStruct((s, D_MODEL), F32),
                   jax.ShapeDtypeStruct((1, D_MODEL), F32), jax.ShapeDtypeStruct((8, D_CONV), F32)],
        compiler_params=_params(("arbitrary",), vmem=VMEM_LIMIT),
    )(dcv, dcv, dgb, dzq, dzk, dzv, z, z, x, dx2, g_mix, conv_w, w_in_t)


def _weight_grad(name, a, b, tn, tk, gate=None):
    s, n = a.shape
    steps = s // tk

    def body(*refs):
        gate_ref = refs[0] if gate is not None else None
        a_ref, b_ref, out_ref, acc = refs[1:] if gate is not None else refs
        k = pl.program_id(1)

        @pl.when(k == 0)
        def _():
            acc[...] = jnp.zeros_like(acc)

        lhs = a_ref[...]
        if gate is not None:
            gv = gate_ref[...].astype(F32)
            lhs = ((gv * jax.nn.sigmoid(gv)) * lhs.astype(F32)).astype(BF16)
        acc[...] += _dot(lhs, b_ref[...].astype(BF16), TN)

        @pl.when(k == steps - 1)
        def _():
            out_ref[...] = acc[...].astype(BF16)

    lhs_spec = pl.BlockSpec((tk, tn), lambda j, k: (k, j))
    return pl.pallas_call(
        body, name=name, grid=(n // tn, steps),
        in_specs=([lhs_spec] if gate is not None else []) + [lhs_spec, pl.BlockSpec((tk, D_MODEL), lambda j, k: (k, 0))],
        out_specs=pl.BlockSpec((tn, D_MODEL), lambda j, k: (j, 0)),
        out_shape=jax.ShapeDtypeStruct((n, D_MODEL), BF16),
        scratch_shapes=[pltpu.VMEM((tn, D_MODEL), F32)],
        compiler_params=_params(("parallel", "arbitrary"), vmem=VMEM_LIMIT),
    )(*([gate] if gate is not None else []), a, b)


def _adamw_math(w, g, m, v):
    nm = ADAM_B1 * m + (1.0 - ADAM_B1) * g
    nv = ADAM_B2 * v + (1.0 - ADAM_B2) * (g * g)
    m_hat = nm / (1.0 - ADAM_B1 ** ADAM_STEP)
    v_hat = nv / (1.0 - ADAM_B2 ** ADAM_STEP)
    return -ADAM_LR * (m_hat / (jnp.sqrt(v_hat) + ADAM_EPS) + ADAM_WD * w), nm, nv


def _adamw_received(name, device, w, stack, received, m, v):
    def body(dev_ref, w_ref, own_ref, recv_ref, m_ref, v_ref, g_ref, d_ref, nm_ref, nv_ref):
        del dev_ref
        g = own_ref[...].astype(F32)
        for k in range(N_DEV - 1):
            g = g + recv_ref[k].astype(F32)
        g_ref[...] = g
        d_ref[...], nm_ref[...], nv_ref[...] = _adamw_math(w_ref[...], g, m_ref[...], v_ref[...])

    rows = stack.shape[1]
    full = pl.BlockSpec((rows, D_MODEL), lambda i, dref: (0, 0))
    return pl.pallas_call(
        body, name=name,
        grid_spec=pltpu.PrefetchScalarGridSpec(
            num_scalar_prefetch=1, grid=(1,),
            in_specs=[full, pl.BlockSpec((None, rows, D_MODEL), lambda i, dref: (dref[0], 0, 0)),
                      pl.BlockSpec(received.shape, lambda i, dref: (0, 0, 0)), full, full],
            out_specs=[full] * 4),
        out_shape=[jax.ShapeDtypeStruct((rows, D_MODEL), F32)] * 4,
        compiler_params=_params(("arbitrary",), vmem=VMEM_LIMIT),
    )(device, w, stack, received, m, v)


def _adamw(name, w, g, m, v):
    def body(w_ref, g_ref, m_ref, v_ref, d_ref, nm_ref, nv_ref):
        d_ref[...], nm_ref[...], nv_ref[...] = _adamw_math(w_ref[...], g_ref[...], m_ref[...], v_ref[...])

    rows = w.shape[0]
    tr = 256 if rows % 256 == 0 else rows
    spec = pl.BlockSpec((tr, w.shape[1]), lambda i: (i, 0))
    return pl.pallas_call(
        body, name=name, grid=(rows // tr,),
        in_specs=[spec] * 4, out_specs=[spec] * 3,
        out_shape=[jax.ShapeDtypeStruct(w.shape, F32)] * 3,
        compiler_params=_params(("parallel",)),
    )(w, g, m, v)


def kernel(x, g_mix, w_in, conv_w, g_q, g_k, g_conv_out, g_attn_out, w_out, g_ffn, w_gate, w_up, w_down, loss_target, m_g_mix, m_w_in, m_conv_w, m_g_q, m_g_k, m_g_conv_out, m_g_attn_out, m_w_out, m_g_ffn, m_w_gate, m_w_up, m_w_down, v_g_mix, v_w_in, v_conv_w, v_g_q, v_g_k, v_g_conv_out, v_g_attn_out, v_w_out, v_g_ffn, v_w_gate, v_w_up, v_w_down):
    s = x.shape[1]
    tm = min(512, s)
    xs, target = x[0], loss_target[0]
    px, py, pc = lax.axis_index("x"), lax.axis_index("y"), lax.axis_index("c")
    device = 4 * px + 2 * py + pc

    conv_block = jnp.zeros((8, 128), F32).at[:3, :HEAD_DIM].set(conv_w[0])
    first, shards = [w_in[0].T, conv_block], [w_out[0], w_gate[0].T, w_up[0].T, w_down[0]]
    w_in_g, conv_g, *shards = _gather_first_weights(first, [BF16, F32], shards)
    w_in_t = w_in_g.reshape(D_IN, D_MODEL)
    conv_full = jnp.transpose(conv_g[:, :3, :HEAD_DIM], (1, 0, 2)).reshape(3, D_CONV)
    conv_full = jnp.concatenate([conv_full, jnp.zeros((5, D_CONV), F32)], axis=0)
    gq2 = jnp.concatenate([g_q, g_q], axis=-1)
    gk2 = jnp.concatenate([g_k, g_k], axis=-1)

    h1, z = _in_projection(xs, g_mix, w_in_t, tm)
    masks = _permuted_masks()
    y_attn, lse, gathered = _attention_forward(z, gq2, gk2, masks, shards)
    w_out_f = gathered[0].reshape(D_MODEL, D_MODEL)
    w_gate_t, w_up_t, w_down_f = [g.reshape(D_FF, D_MODEL) for g in gathered[1:]]
    mix, x2, h2 = _mix_out(z, y_attn, xs, conv_full, g_conv_out, g_attn_out, g_ffn, w_out_f, tm)
    a, b, dy, dy_bf, sq_err = _ffn_forward(h2, x2, target, w_gate_t, w_up_t, w_down_f, tm)

    tk = min(2048, s)
    da, db, dx2, dx2_bf, dg_ffn = _ffn_backward(dy, a, b, x2, g_ffn, w_gate_t, w_up_t, w_down_f, tm)
    dya, dgb, dcv, dg_conv_out, dg_attn_out = _out_backward(dx2_bf, z, y_attn, conv_full, g_conv_out, g_attn_out, w_out_f, tm)
    early = [
        _weight_grad("grad_w_out", mix, dx2_bf, D_MODEL, tk),
        _weight_grad("grad_w_gate", da, h2, D_FF // 2, tk),
        _weight_grad("grad_w_up", db, h2, D_FF // 2, tk),
        _weight_grad("grad_w_down", b, dy_bf, D_FF // 2, tk, gate=a),
    ]
    early = [g.reshape(N_DEV, g.shape[0] // N_DEV, D_MODEL) for g in early]
    (dzq, dzk, dzv, dgq_pairs, dgk_pairs), landed = _attention_backward(z, y_attn, lse, dya, gq2, gk2, masks, early)
    dz, grad_x, dg_mix, dconv = _in_backward(dcv, dgb, dzq, dzk, dzv, z, xs, dx2, g_mix, conv_full, w_in_t, tm)
    dev = jnp.reshape(device, (1,)).astype(jnp.int32)
    scattered = dict(zip(("w_out", "w_gate", "w_up", "w_down"), zip(early, landed)))

    late = [_weight_grad("grad_w_in", dz, h1, D_IN // 2, tk).reshape(4, 2, D_IN // N_DEV, D_MODEL)]
    from_sibling = _reduce_scatter_cores(late)
    core = jnp.reshape(pc, (1,)).astype(jnp.int32)
    partials = _add_core_partials(core, late, from_sibling)
    from_chips = _reduce_scatter_chips(partials)
    chip = jnp.reshape(2 * px + py, (1,)).astype(jnp.int32)
    (gw_in_t,) = _add_chip_partials(chip, partials, from_chips)
    gw_in = gw_in_t.T

    dg_q = jnp.sum(dgq_pairs.reshape(2 * N_PAIRS, HEAD_DIM), axis=0)
    dg_k = jnp.sum(dgk_pairs.reshape(2 * N_PAIRS, HEAD_DIM), axis=0)
    zeros = lambda n: jnp.zeros((n,), F32)
    small = jnp.stack([
        dg_mix[0], dg_ffn[0],
        jnp.concatenate([dg_conv_out[0], dg_attn_out[0]]),
        jnp.concatenate([dg_q, dg_k, zeros(D_MODEL - 2 * HEAD_DIM)]),
        jnp.concatenate([dconv[0], dconv[1]]),
        jnp.concatenate([dconv[2], zeros(D_CONV)]),
        jnp.concatenate([sq_err[0, :1], zeros(D_MODEL - 1)]),
        zeros(D_MODEL),
    ])
    total = _all_reduce_small(small)
    loss = total[6, 0] * (0.5 / D_MODEL)
    gg_mix, gg_ffn = total[0:1], total[1:2]
    gg_conv_out, gg_attn_out = total[2:3, :D_CONV], total[2:3, D_CONV:]
    gg_q, gg_k = total[3:4, :HEAD_DIM], total[3:4, HEAD_DIM:2 * HEAD_DIM]
    conv_total = jnp.stack([total[4, :D_CONV], total[4, D_CONV:], total[5, :D_CONV]])
    g_conv = lax.dynamic_slice(conv_total, (0, device * HEAD_DIM), (3, HEAD_DIM))

    names = ["g_mix", "w_in", "conv_w", "g_q", "g_k", "g_conv_out", "g_attn_out", "w_out", "g_ffn", "w_gate", "w_up", "w_down"]
    weights = [g_mix, w_in[0], conv_w[0], g_q, g_k, g_conv_out, g_attn_out, w_out[0], g_ffn, w_gate[0], w_up[0], w_down[0]]
    grad_list = [gg_mix, None, g_conv, gg_q, gg_k, gg_conv_out, gg_attn_out, None, gg_ffn, None, None, None]
    m_list = [m_g_mix, m_w_in[0], m_conv_w[0], m_g_q, m_g_k, m_g_conv_out, m_g_attn_out, m_w_out[0], m_g_ffn, m_w_gate[0], m_w_up[0], m_w_down[0]]
    v_list = [v_g_mix, v_w_in[0], v_conv_w[0], v_g_q, v_g_k, v_g_conv_out, v_g_attn_out, v_w_out[0], v_g_ffn, v_w_gate[0], v_w_up[0], v_w_down[0]]
    stacked = {"w_in", "conv_w", "w_out", "w_gate", "w_up", "w_down"}
    transposed = {"w_gate", "w_up"}
    results = {}
    operands = {n: (w, g, m, v) for n, w, g, m, v in zip(names, weights, grad_list, m_list, v_list)}
    for name in scattered:
        w, _, m, v = operands[name]
        if name in transposed:
            w, m, v = w.T, m.T, v.T
        results[name] = _adamw_received("adamw_" + name, dev, w, *scattered[name], m, v)
        if name in transposed:
            results[name] = [t.T for t in results[name]]
    for name in names:
        if name not in results:
            w, g, m, v = operands[name]
            g = gw_in if name == "w_in" else g
            results[name] = [g, *_adamw("adamw_" + name, w, g, m, v)]
    outs = []
    for part in range(4):
        outs += [results[n][part][None] if n in stacked else results[n][part] for n in names]
    return (loss, grad_x[None], *outs)
```

```python
import jax
import jax.numpy as jnp
import numpy as np
from jax import lax
from jax.experimental import pallas as pl
from jax.experimental.pallas import tpu as pltpu

F32 = jnp.float32
BF16 = jnp.bfloat16

D_MODEL = 1024
D_CONV = 512
D_ATTN = 512
D_IN = 3072
D_FF = 2816
HEAD_DIM = 64
N_PAIRS = 4
BLOCK = 128
DILATIONS = (1, 4, 16)
N_DEV = 8
EPS = 1e-6
NEG = -1e30

ADAM_LR = 0.001
ADAM_B1 = 0.9
ADAM_B2 = 0.999
ADAM_EPS = 1e-08
ADAM_WD = 0.01
ADAM_STEP = 10

NT = (((1,), (1,)), ((), ()))
NN = (((1,), (0,)), ((), ()))
TN = (((0,), (0,)), ((), ()))
MESH = pl.DeviceIdType.MESH

VMEM_LIMIT = 56 * 1024 * 1024


def _params(semantics=None, vmem=None):
    kw = {}
    if semantics is not None:
        kw["dimension_semantics"] = semantics
    if vmem is not None:
        kw["vmem_limit_bytes"] = vmem
    return pltpu.CompilerParams(**kw)


def _dot(a, b, dims):
    return lax.dot_general(a, b, dims, preferred_element_type=F32)


def _const_spec(shape):
    n = len(shape)
    return pl.BlockSpec(shape, lambda *_: (0,) * n, pipeline_mode=pl.Buffered(1))


def _rms(x):
    r = lax.rsqrt(jnp.mean(x * x, axis=-1, keepdims=True) + EPS)
    return x * r, r


def _rms_bwd(dy, xh, r, g):
    t = dy * g
    return r * (t - xh * jnp.mean(t * xh, axis=-1, keepdims=True))


def _head_blockdiag(scale):
    i = lax.broadcasted_iota(jnp.int32, (128, 128), 0) // HEAD_DIM
    j = lax.broadcasted_iota(jnp.int32, (128, 128), 1) // HEAD_DIM
    return jnp.where(i == j, scale, 0.0).astype(BF16)


def _head_sum(x, bd):
    hi = x.astype(BF16)
    lo = (x - hi.astype(F32)).astype(BF16)
    return _dot(hi, bd, NN) + _dot(lo, bd, NN)


def _place():
    return lax.axis_index("x"), lax.axis_index("y"), lax.axis_index("c")


class _TwoLevelGather:
    def __init__(self, stacks, send_sems, recv_sems):
        self.stacks, self.send_sems, self.recv_sems = stacks, send_sems, recv_sems
        x, y, c = _place()
        self.c = c
        self.me, self.sibling = (x, y, c), (x, y, 1 - c)
        self.chips = [(x, 1 - y), (1 - x, y), (1 - x, 1 - y)]

    @staticmethod
    def index(p):
        return 4 * p[0] + 2 * p[1] + p[2]

    def copy(self, a, k, block, to, src=None):
        dst = self.stacks[a].at[self.index(block)]
        return pltpu.make_async_remote_copy(
            src_ref=dst if src is None else src, dst_ref=dst,
            send_sem=self.send_sems.at[7 * a + k], recv_sem=self.recv_sems.at[7 * a + k],
            device_id=to, device_id_type=MESH)

    def first(self, a, src=None):
        cps = [self.copy(a, 0, self.me, self.sibling, src)]
        return cps + [self.copy(a, 1 + j, self.me, (*chip, self.c), src) for j, chip in enumerate(self.chips)]

    def forwards(self, a):
        return [self.copy(a, 4 + j, (*chip, self.c), self.sibling) for j, chip in enumerate(self.chips)]

    def start(self, srcs=None):
        for a in range(len(self.stacks)):
            for cp in self.first(a, None if srcs is None else srcs[a]):
                cp.start()

    def forward(self):
        for a in range(len(self.stacks)):
            for j, (chip, fwd) in enumerate(zip(self.chips, self.forwards(a))):
                self.copy(a, 1 + j, (*chip, self.c), self.me).wait_recv()
                fwd.start()

    def finish(self):
        for a in range(len(self.stacks)):
            self.copy(a, 0, self.sibling, self.me).wait_recv()
            for j, chip in enumerate(self.chips):
                self.copy(a, 4 + j, (*chip, 1 - self.c), self.me).wait_recv()
        for a in range(len(self.stacks)):
            for cp in self.first(a) + self.forwards(a):
                cp.wait_send()


def _gather_first_weights(gathered_parts, gathered_dtypes, cast_parts):
    n, m = len(gathered_parts), len(cast_parts)

    def body(*refs):
        ins, casts_in = refs[:n], refs[n:n + m]
        outs, casts_out = refs[n + m:2 * n + m], refs[2 * n + m:2 * n + 2 * m]
        send_sems, recv_sems = refs[2 * n + 2 * m], refs[2 * n + 2 * m + 1]
        gather = _TwoLevelGather(outs, send_sems, recv_sems)
        for a in range(n):
            outs[a][gather.index(gather.me)] = ins[a][...].astype(outs[a].dtype)
        gather.start()
        for a in range(m):
            casts_out[a][...] = casts_in[a][...].astype(BF16)
        gather.forward()
        gather.finish()

    vm = pl.BlockSpec(memory_space=pltpu.VMEM)
    return pl.pallas_call(
        body, name="weight_all_gather",
        out_shape=[jax.ShapeDtypeStruct((N_DEV,) + p.shape, dt) for p, dt in zip(gathered_parts, gathered_dtypes)]
        + [jax.ShapeDtypeStruct(p.shape, BF16) for p in cast_parts],
        in_specs=[vm] * (n + m), out_specs=[vm] * (n + m),
        scratch_shapes=[pltpu.SemaphoreType.DMA((7 * n,)), pltpu.SemaphoreType.DMA((7 * n,))],
        compiler_params=_params(vmem=VMEM_LIMIT),
    )(*gathered_parts, *cast_parts)


class _DirectReduceScatter:
    def __init__(self, stacks, landing, send_sems, recv_sems):
        self.stacks, self.landing, self.send_sems, self.recv_sems = stacks, landing, send_sems, recv_sems
        self.place = _place()

    def copies(self):
        x, y, c = self.place
        cps = []
        for a in range(len(self.stacks)):
            for k in range(1, N_DEV):
                peer = (1 - x if k & 4 else x, 1 - y if k & 2 else y, 1 - c if k & 1 else c)
                cps.append(pltpu.make_async_remote_copy(
                    src_ref=self.stacks[a].at[4 * peer[0] + 2 * peer[1] + peer[2]], dst_ref=self.landing[a].at[k - 1],
                    send_sem=self.send_sems.at[7 * a + k - 1], recv_sem=self.recv_sems.at[7 * a + k - 1],
                    device_id=peer, device_id_type=MESH))
        return cps

    def start(self):
        for cp in self.copies():
            cp.start()

    def finish(self):
        for cp in self.copies():
            cp.wait_recv()
        for cp in self.copies():
            cp.wait_send()


def _all_reduce_small(v):
    def body(v_ref, out_ref, gathered, send_sems, recv_sems):
        x, y, c = _place()
        mine = 4 * x + 2 * y + c
        gathered[mine] = v_ref[...]

        def copy(k):
            peer = (1 - x if k & 4 else x, 1 - y if k & 2 else y, 1 - c if k & 1 else c)
            return pltpu.make_async_remote_copy(
                src_ref=gathered.at[mine], dst_ref=gathered.at[mine],
                send_sem=send_sems.at[k - 1], recv_sem=recv_sems.at[k - 1], device_id=peer, device_id_type=MESH)

        copies = [copy(k) for k in range(1, N_DEV)]
        for cp in copies:
            cp.start()
        for cp in copies:
            cp.wait_recv()
        for cp in copies:
            cp.wait_send()
        total = gathered[0]
        for d in range(1, N_DEV):
            total = total + gathered[d]
        out_ref[...] = total

    vm = pl.BlockSpec(memory_space=pltpu.VMEM)
    return pl.pallas_call(
        body, name="small_all_reduce",
        out_shape=jax.ShapeDtypeStruct(v.shape, F32),
        in_specs=[vm], out_specs=vm,
        scratch_shapes=[pltpu.VMEM((N_DEV,) + v.shape, F32),
                        pltpu.SemaphoreType.DMA((N_DEV - 1,)), pltpu.SemaphoreType.DMA((N_DEV - 1,))],
    )(v)


def _reduce_scatter_cores(grads):
    n = len(grads)

    def body(*refs):
        ins, outs = refs[:n], refs[n:2 * n]
        send_sems, recv_sems = refs[2 * n], refs[2 * n + 1]
        x, y, c = _place()
        copies = []
        for a in range(n):
            for k in range(4):
                copies.append(pltpu.make_async_remote_copy(
                    src_ref=ins[a].at[k, 1 - c], dst_ref=outs[a].at[k],
                    send_sem=send_sems.at[4 * a + k], recv_sem=recv_sems.at[4 * a + k],
                    device_id=(x, y, 1 - c), device_id_type=MESH))
        for cp in copies:
            cp.start()
        for cp in copies:
            cp.wait_recv()
        for cp in copies:
            cp.wait_send()

    hbm = pl.BlockSpec(memory_space=pl.ANY)
    return pl.pallas_call(
        body, name="grad_reduce_scatter_cores",
        out_shape=[jax.ShapeDtypeStruct((4,) + g.shape[2:], g.dtype) for g in grads],
        in_specs=[hbm] * n, out_specs=[hbm] * n,
        scratch_shapes=[pltpu.SemaphoreType.DMA((4 * n,)), pltpu.SemaphoreType.DMA((4 * n,))],
    )(*grads)


def _reduce_scatter_chips(partials):
    n = len(partials)

    def body(*refs):
        ins, outs = refs[:n], refs[n:2 * n]
        send_sems, recv_sems = refs[2 * n], refs[2 * n + 1]
        x, y, c = _place()
        copies = []
        for a in range(n):
            for j in (1, 2, 3):
                px = 1 - x if j & 2 else x
                py = 1 - y if j & 1 else y
                copies.append(pltpu.make_async_remote_copy(
                    src_ref=ins[a].at[2 * px + py], dst_ref=outs[a].at[j - 1],
                    send_sem=send_sems.at[3 * a + j - 1], recv_sem=recv_sems.at[3 * a + j - 1],
                    device_id=(px, py, c), device_id_type=MESH))
        for cp in copies:
            cp.start()
        for cp in copies:
            cp.wait_recv()
        for cp in copies:
            cp.wait_send()

    hbm = pl.BlockSpec(memory_space=pl.ANY)
    return pl.pallas_call(
        body, name="grad_reduce_scatter_chips",
        out_shape=[jax.ShapeDtypeStruct((3,) + p.shape[1:], p.dtype) for p in partials],
        in_specs=[hbm] * n, out_specs=[hbm] * n,
        scratch_shapes=[pltpu.SemaphoreType.DMA((3 * n,)), pltpu.SemaphoreType.DMA((3 * n,))],
    )(*partials)


def _add_core_partials(core, grads, received):
    n = len(grads)

    def body(core_ref, *refs):
        del core_ref
        gs, rs, outs = refs[:n], refs[n:2 * n], refs[2 * n:]
        for a in range(n):
            outs[a][...] = (gs[a][...].astype(F32) + rs[a][...].astype(F32)).astype(outs[a].dtype)

    in_specs = [pl.BlockSpec((None, None) + g.shape[2:], lambda k, cref: (k, cref[0], 0, 0)) for g in grads]
    in_specs += [pl.BlockSpec((None,) + r.shape[1:], lambda k, cref: (k, 0, 0)) for r in received]
    out_specs = [pl.BlockSpec((None,) + r.shape[1:], lambda k, cref: (k, 0, 0)) for r in received]
    return pl.pallas_call(
        body, name="grad_add_core_partials",
        grid_spec=pltpu.PrefetchScalarGridSpec(num_scalar_prefetch=1, grid=(4,), in_specs=in_specs, out_specs=out_specs),
        out_shape=[jax.ShapeDtypeStruct(r.shape, r.dtype) for r in received],
        compiler_params=_params(("arbitrary",)),
    )(core, *grads, *received)


def _add_chip_partials(chip, partials, received):
    n = len(partials)

    def body(chip_ref, *refs):
        del chip_ref
        ps, rs, outs = refs[:n], refs[n:2 * n], refs[2 * n:]
        for a in range(n):
            outs[a][...] = ((ps[a][...].astype(F32) + rs[a][0].astype(F32)) + rs[a][1].astype(F32)) + rs[a][2].astype(F32)

    in_specs = [pl.BlockSpec((None,) + p.shape[1:], lambda i, cref: (cref[0], 0, 0)) for p in partials]
    in_specs += [pl.BlockSpec(r.shape, lambda i, cref: (0, 0, 0)) for r in received]
    out_specs = [pl.BlockSpec(p.shape[1:], lambda i, cref: (0, 0)) for p in partials]
    return pl.pallas_call(
        body, name="grad_add_chip_partials",
        grid_spec=pltpu.PrefetchScalarGridSpec(num_scalar_prefetch=1, grid=(1,), in_specs=in_specs, out_specs=out_specs),
        out_shape=[jax.ShapeDtypeStruct(p.shape[1:], F32) for p in partials],
        compiler_params=_params(("arbitrary",), vmem=VMEM_LIMIT),
    )(chip, *partials, *received)


def _in_projection(x, g_mix, w_in_t, tm):
    s = x.shape[0]

    def body(x_ref, g_ref, w_ref, h_ref, z_ref):
        xh, _ = _rms(x_ref[...])
        h = (xh * g_ref[...]).astype(BF16)
        h_ref[...] = h
        for n0 in range(0, D_IN, 512):
            z_ref[:, n0:n0 + 512] = _dot(h, w_ref[n0:n0 + 512, :], NT).astype(BF16)

    return pl.pallas_call(
        body, name="in_projection", grid=(s // tm,),
        in_specs=[pl.BlockSpec((tm, D_MODEL), lambda i: (i, 0)), _const_spec((1, D_MODEL)), _const_spec((D_IN, D_MODEL))],
        out_specs=[pl.BlockSpec((tm, D_MODEL), lambda i: (i, 0)), pl.BlockSpec((tm, D_IN), lambda i: (i, 0))],
        out_shape=[jax.ShapeDtypeStruct((s, D_MODEL), BF16), jax.ShapeDtypeStruct((s, D_IN), BF16)],
        compiler_params=_params(("parallel",), vmem=VMEM_LIMIT),
    )(x, g_mix, w_in_t)


RESIDUES = 16
CHUNK = 512
GROUP = 8


def _branch_geometry(d, seg):
    n_runs = RESIDUES // d
    w = BLOCK // n_runs
    blocks = seg // w
    group = min(GROUP, blocks)
    chains = min(GROUP // group, d)
    return n_runs, w, blocks // group, group, chains


def _permuted_masks():
    out = []
    for d in DILATIONS:
        n_runs = RESIDUES // d
        w = BLOCK // n_runs
        p = np.arange(BLOCK)
        pos = (p % w) * n_runs + p // w
        dist = pos[:, None] - np.concatenate([pos - BLOCK, pos])[None, :]
        band = (dist >= 0) & (dist <= BLOCK)
        first = band & (np.arange(2 * BLOCK)[None, :] >= BLOCK)
        both = [np.where(m, 0.0, NEG).astype(np.float32) for m in (band, first)]
        out.append(np.stack([np.concatenate([m, m], axis=0) for m in both]))
    return jnp.asarray(np.stack(out))


def _run_starts(u, grp, d, seg, n_runs, w, group):
    return [(u + d * q) * seg + (w * group) * grp for q in range(n_runs)]


def _tile_rows(start, w, b):
    off = start + w * b
    if b < 0:
        off = jnp.maximum(off, 0)
    return pl.ds(pl.multiple_of(off, w), w)


def _load_tile(ref, starts, w, b):
    parts = [ref[_tile_rows(st, w, b), :] for st in starts]
    return parts[0] if len(parts) == 1 else jnp.concatenate(parts, axis=0)


def _store_tile(ref, starts, w, b, tile):
    for i, st in enumerate(starts):
        ref[_tile_rows(st, w, b), :] = tile[i * w:(i + 1) * w]


def _add_tile(ref, starts, w, b, tile):
    for i, st in enumerate(starts):
        ref[_tile_rows(st, w, b), :] += tile[i * w:(i + 1) * w]


def _to_residue_major(dst, src, c, seg):
    per = CHUNK // RESIDUES
    for r in range(RESIDUES):
        dst[pl.ds(pl.multiple_of(r * seg + c * per, per), per), :] = src[pl.ds(r, per, stride=RESIDUES), :]


def _from_residue_major(dst, dst_start, src, c, seg):
    per = CHUNK // RESIDUES
    for r in range(RESIDUES):
        dst[pl.ds(dst_start + r, per, stride=RESIDUES), :] = src[pl.ds(pl.multiple_of(r * seg + c * per, per), per), :]


def _stack_heads(t, head0):
    return jnp.concatenate([jnp.where(head0, t, 0.0), jnp.where(head0, 0.0, t)], axis=0).astype(BF16)


def _attention_forward(z, gq2, gk2, masks, shards):
    s = z.shape[0]
    seg = s // RESIDUES
    n_w = len(shards)

    def body(zq_ref, zk_ref, zv_ref, gq_ref, gk_ref, mask_ref, *rest):
        shard_refs, (o_ref, lse_ref), stacks = rest[:n_w], rest[n_w:n_w + 2], rest[n_w + 2:2 * n_w + 2]
        qf, kf, vf, o_st, l_st, tq, tk, tv, send_sems, recv_sems, local_sems = rest[2 * n_w + 2:]
        pair = pl.program_id(0)
        gather = _TwoLevelGather(stacks, send_sems, recv_sems)

        def own_copies():
            mine = gather.index(gather.me)
            return [pltpu.make_async_copy(shard_refs[a], stacks[a].at[mine], local_sems.at[a]) for a in range(n_w)]

        @pl.when(pair == 0)
        def _():
            for cp in own_copies():
                cp.start()
            gather.start(shard_refs)

        @pl.when(pair == 2)
        def _():
            gather.forward()

        bd = _head_blockdiag(1.0 / HEAD_DIM)
        gq = gq_ref[...] * (HEAD_DIM ** -0.5)
        gk = gk_ref[...]

        def prep(c, carry):
            rows = pl.ds(pl.multiple_of(c * CHUNK, CHUNK), CHUNK)
            zq = zq_ref[rows, :].astype(F32)
            zk = zk_ref[rows, :].astype(F32)
            tq[...] = (zq * lax.rsqrt(_head_sum(zq * zq, bd) + EPS)) * gq
            tk[...] = (zk * lax.rsqrt(_head_sum(zk * zk, bd) + EPS)) * gk
            tv[...] = zv_ref[rows, :].astype(F32)
            _to_residue_major(qf, tq, c, seg)
            _to_residue_major(kf, tk, c, seg)
            _to_residue_major(vf, tv, c, seg)
            return carry

        lax.fori_loop(0, s // CHUNK, prep, 0)

        head0 = lax.broadcasted_iota(jnp.int32, (BLOCK, 128), 1) < HEAD_DIM

        def run_branch(branch, d):
            n_runs, w, steps, group, chains = _branch_geometry(d, seg)
            slots = [(ci, b) for ci in range(chains) for b in range(group)]

            def step(u, grp):
                starts = [_run_starts(u + ci, grp, d, seg, n_runs, w, group) for ci in range(chains)]
                kt = [[_load_tile(kf, st, w, b).astype(BF16) for b in range(-1, group)] for st in starts]
                vt = [[_load_tile(vf, st, w, b).astype(BF16) for b in range(-1, group)] for st in starts]
                scores = []
                for ci, b in slots:
                    q2 = _stack_heads(_load_tile(qf, starts[ci], w, b), head0)
                    kc = jnp.concatenate([kt[ci][b], kt[ci][b + 1]], axis=0)
                    mask = mask_ref[branch, 0]
                    if b == 0:
                        mask = jnp.where(grp == 0, mask_ref[branch, 1], mask)
                    scores.append(_dot(q2, kc, NT) + mask)
                sc = jnp.concatenate(scores, axis=0)
                m = jnp.max(sc, axis=-1, keepdims=True)
                e = jnp.exp((sc - m).astype(BF16))
                ones = jnp.ones((BLOCK, 128), BF16)
                fresh = []
                for n, (ci, b) in enumerate(slots):
                    rows = slice(2 * BLOCK * n, 2 * BLOCK * (n + 1))
                    v_aug = jnp.concatenate([jnp.concatenate([vt[ci][b], ones], axis=1),
                                             jnp.concatenate([vt[ci][b + 1], ones], axis=1)], axis=0)
                    o2 = _dot(e[rows], v_aug, NN)
                    den = o2[:, 128:]
                    o2 = o2[:, :128] * (1.0 / den)
                    l2 = m[rows] + jnp.log(den)
                    fresh.append((jnp.where(head0, o2[:BLOCK], o2[BLOCK:]), jnp.where(head0, l2[:BLOCK], l2[BLOCK:])))
                for n, (ci, b) in enumerate(slots):
                    o_new, l_new = fresh[n]
                    if branch > 0:
                        l_old = _load_tile(l_st, starts[ci], w, b)
                        diff = l_new - l_old
                        t = jnp.exp(-jnp.abs(diff))
                        big = 1.0 / (1.0 + t)
                        small = t * big
                        newer = diff >= 0.0
                        o_old = _load_tile(o_st, starts[ci], w, b)
                        o_new = o_old * jnp.where(newer, small, big) + o_new * jnp.where(newer, big, small)
                        l_new = jnp.maximum(l_old, l_new) + jnp.log(1.0 + t)
                    _store_tile(o_st, starts[ci], w, b, o_new)
                    _store_tile(l_st, starts[ci], w, b, l_new)

            def unit(it, carry):
                def inner(grp, carry2):
                    step(it * chains, grp)
                    return carry2
                return lax.fori_loop(0, steps, inner, carry)

            lax.fori_loop(0, d // chains, unit, 0)

        for branch, d in enumerate(DILATIONS):
            run_branch(branch, d)

        def finish(c, carry):
            _from_residue_major(o_ref, c * CHUNK, o_st, c, seg)
            _from_residue_major(lse_ref, c * CHUNK, l_st, c, seg)
            return carry

        lax.fori_loop(0, s // CHUNK, finish, 0)

        @pl.when(pair == N_PAIRS - 1)
        def _():
            gather.finish()
            for cp in own_copies():
                cp.wait()

    def col(c0):
        return pl.BlockSpec((s, 128), lambda p: (0, c0 + p), pipeline_mode=pl.Buffered(1))

    hbm = pl.BlockSpec(memory_space=pl.ANY)
    out = pl.pallas_call(
        body, name="attention_forward", grid=(N_PAIRS,),
        in_specs=[col(12), col(16), col(20), _const_spec((1, 128)), _const_spec((1, 128)), _const_spec(masks.shape)]
        + [hbm] * n_w,
        out_specs=[pl.BlockSpec((s, 128), lambda p: (0, p)), pl.BlockSpec((s, 128), lambda p: (0, p))] + [hbm] * n_w,
        out_shape=[jax.ShapeDtypeStruct((s, D_ATTN), F32), jax.ShapeDtypeStruct((s, D_ATTN), F32)]
        + [jax.ShapeDtypeStruct((N_DEV,) + w.shape, w.dtype) for w in shards],
        scratch_shapes=[pltpu.VMEM((s, 128), F32)] * 5 + [pltpu.VMEM((CHUNK, 128), F32)] * 3
        + [pltpu.SemaphoreType.DMA((7 * n_w,)), pltpu.SemaphoreType.DMA((7 * n_w,)), pltpu.SemaphoreType.DMA((n_w,))],
        compiler_params=_params(("arbitrary",), vmem=VMEM_LIMIT),
    )(z, z, z, gq2, gk2, masks, *shards)
    return out[0], out[1], out[2:]


def _conv_forward(zt, zprev, w, first):
    u, gb, gc = zt[:, :D_CONV], zt[:, D_CONV:2 * D_CONV], zt[:, 2 * D_CONV:]
    cu = gc * u
    cu_prev = jnp.where(first, 0.0, zprev[:, 2 * D_CONV:] * zprev[:, :D_CONV])
    row = lax.broadcasted_iota(jnp.int32, cu.shape, 0)
    cu1 = jnp.where(row >= 1, pltpu.roll(cu, 1, 0), cu_prev[7:8, :])
    cu2 = jnp.where(row >= 2, pltpu.roll(cu, 2, 0), jnp.where(row == 1, cu_prev[7:8, :], cu_prev[6:7, :]))
    cv = w[0:1, :] * cu2 + w[1:2, :] * cu1 + w[2:3, :] * cu
    return u, gb, gc, cu, cu1, cu2, cv


def _mix_out(z, y_attn, x, conv_w, g_conv_out, g_attn_out, g_ffn, w_out, tm):
    s = x.shape[0]
    hb = tm // 8

    def body(z_ref, zp_ref, ya_ref, x_ref, cw_ref, gc_ref, ga_ref, gf_ref, w_ref, mix_ref, x2_ref, h2_ref):
        first = pl.program_id(0) == 0
        zt = z_ref[...].astype(F32)
        _, gb, _, _, _, _, cv = _conv_forward(zt, zp_ref[...].astype(F32), cw_ref[...], first)
        nc, _ = _rms(gb * cv)
        na, _ = _rms(ya_ref[...])
        mix = jnp.concatenate([nc * gc_ref[...], na * ga_ref[...]], axis=-1).astype(BF16)
        mix_ref[...] = mix
        x2 = x_ref[...] + _dot(mix, w_ref[...], NN)
        x2_ref[...] = x2
        xh, _ = _rms(x2)
        h2_ref[...] = (xh * gf_ref[...]).astype(BF16)

    tile = lambda w: pl.BlockSpec((tm, w), lambda i: (i, 0))
    return pl.pallas_call(
        body, name="mix_out", grid=(s // tm,),
        in_specs=[tile(3 * D_CONV), pl.BlockSpec((8, 3 * D_CONV), lambda i: (jnp.maximum(i * hb - 1, 0), 0)),
                  tile(D_ATTN), tile(D_MODEL), _const_spec((8, D_CONV)), _const_spec((1, D_CONV)), _const_spec((1, D_ATTN)),
                  _const_spec((1, D_MODEL)), _const_spec((D_MODEL, D_MODEL))],
        out_specs=[tile(D_MODEL)] * 3,
        out_shape=[jax.ShapeDtypeStruct((s, D_MODEL), BF16), jax.ShapeDtypeStruct((s, D_MODEL), F32),
                   jax.ShapeDtypeStruct((s, D_MODEL), BF16)],
        compiler_params=_params(("arbitrary",), vmem=VMEM_LIMIT),
    )(z, z, y_attn, x, conv_w, g_conv_out, g_attn_out, g_ffn, w_out)


FF_CHUNK = 256


def _ffn_forward(h2, x2, target, w_gate_t, w_up_t, w_down, tm):
    s = h2.shape[0]

    def body(h_ref, x2_ref, t_ref, wg_ref, wu_ref, wd_ref, a_ref, b_ref, dy_ref, dyb_ref, loss_ref, f_buf):
        @pl.when(pl.program_id(0) == 0)
        def _():
            loss_ref[...] = jnp.zeros_like(loss_ref)

        h = h_ref[...]
        for c0 in range(0, D_FF, FF_CHUNK):
            rows = slice(c0, c0 + FF_CHUNK)
            a = _dot(h, wg_ref[rows, :], NT)
            b = _dot(h, wu_ref[rows, :], NT)
            a_ref[:, rows] = a.astype(BF16)
            b_ref[:, rows] = b.astype(BF16)
            f_buf[:, rows] = ((a * jax.nn.sigmoid(a)) * b).astype(BF16)
        y = x2_ref[...] + _dot(f_buf[...], wd_ref[...], NN)
        err = y - t_ref[...]
        dy = err * (1.0 / D_MODEL)
        dy_ref[...] = dy
        dyb_ref[...] = dy.astype(BF16)
        loss_ref[...] += jnp.sum(err * err)

    tile = lambda w: pl.BlockSpec((tm, w), lambda i: (i, 0))
    return pl.pallas_call(
        body, name="ffn_forward", grid=(s // tm,),
        in_specs=[tile(D_MODEL), tile(D_MODEL), tile(D_MODEL)] + [_const_spec((D_FF, D_MODEL))] * 3,
        out_specs=[tile(D_FF), tile(D_FF), tile(D_MODEL), tile(D_MODEL), pl.BlockSpec((8, 128), lambda i: (0, 0))],
        out_shape=[jax.ShapeDtypeStruct((s, D_FF), BF16), jax.ShapeDtypeStruct((s, D_FF), BF16),
                   jax.ShapeDtypeStruct((s, D_MODEL), F32), jax.ShapeDtypeStruct((s, D_MODEL), BF16),
                   jax.ShapeDtypeStruct((8, 128), F32)],
        scratch_shapes=[pltpu.VMEM((tm, D_FF), BF16)],
        compiler_params=_params(("arbitrary",), vmem=VMEM_LIMIT),
    )(h2, x2, target, w_gate_t, w_up_t, w_down)


def _ffn_backward(dy, a, b, x2, g_ffn, w_gate_t, w_up_t, w_down, tm):
    s = dy.shape[0]
    half = D_FF // 2
    pieces = [(p0, min(FF_CHUNK, half - p0)) for p0 in range(0, half, FF_CHUNK)]

    def body(dy_ref, a_ref, b_ref, x2_ref, g_ref, wg_ref, wu_ref, wd_ref, da_ref, db_ref, dx2_ref, dx2b_ref, dg_ref, acc):
        i, c = pl.program_id(0), pl.program_id(1)

        @pl.when((i == 0) & (c == 0))
        def _():
            dg_ref[...] = jnp.zeros_like(dg_ref)

        dyb = dy_ref[...].astype(BF16)
        for p0, width in pieces:
            rows = slice(p0, p0 + width)
            df = _dot(dyb, wd_ref[rows, :], NT)
            av = a_ref[:, rows].astype(F32)
            bv = b_ref[:, rows].astype(F32)
            sig = jax.nn.sigmoid(av)
            silu = av * sig
            da_ref[:, rows] = (df * bv * (sig * (1.0 + av * (1.0 - sig)))).astype(BF16)
            db_ref[:, rows] = (df * silu).astype(BF16)
        dh = _dot(da_ref[...], wg_ref[...], NN) + _dot(db_ref[...], wu_ref[...], NN)

        @pl.when(c == 0)
        def _():
            acc[...] = dh

        @pl.when(c == 1)
        def _():
            dh_all = acc[...] + dh
            xh, r = _rms(x2_ref[...])
            dg_ref[...] += jnp.sum(dh_all * xh, axis=0, keepdims=True)
            dx2 = dy_ref[...] + _rms_bwd(dh_all, xh, r, g_ref[...])
            dx2_ref[...] = dx2
            dx2b_ref[...] = dx2.astype(BF16)

    tile = lambda w: pl.BlockSpec((tm, w), lambda i, c: (i, 0))
    part = pl.BlockSpec((tm, half), lambda i, c: (i, c))
    weight = pl.BlockSpec((half, D_MODEL), lambda i, c: (c, 0))
    return pl.pallas_call(
        body, name="ffn_backward", grid=(s // tm, 2),
        in_specs=[tile(D_MODEL), part, part, tile(D_MODEL), _const_spec((1, D_MODEL)), weight, weight, weight],
        out_specs=[part, part, tile(D_MODEL), tile(D_MODEL), pl.BlockSpec((1, D_MODEL), lambda i, c: (0, 0))],
        out_shape=[jax.ShapeDtypeStruct((s, D_FF), BF16)] * 2
        + [jax.ShapeDtypeStruct((s, D_MODEL), F32), jax.ShapeDtypeStruct((s, D_MODEL), BF16),
           jax.ShapeDtypeStruct((1, D_MODEL), F32)],
        scratch_shapes=[pltpu.VMEM((tm, D_MODEL), F32)],
        compiler_params=_params(("arbitrary", "arbitrary"), vmem=VMEM_LIMIT),
    )(dy, a, b, x2, g_ffn, w_gate_t, w_up_t, w_down)


def _out_backward(dx2, z, y_attn, conv_w, g_conv_out, g_attn_out, w_out, tm):
    s = dx2.shape[0]
    hb = tm // 8

    def body(dx2_ref, z_ref, zp_ref, ya_ref, cw_ref, gc_ref, ga_ref, w_ref, dya_ref, dgb_ref, dcv_ref, dgc_ref, dga_ref):
        first = pl.program_id(0) == 0

        @pl.when(first)
        def _():
            dgc_ref[...] = jnp.zeros_like(dgc_ref)
            dga_ref[...] = jnp.zeros_like(dga_ref)

        dmix = _dot(dx2_ref[...], w_ref[...], NT)
        _, gb, _, _, _, _, cv = _conv_forward(z_ref[...].astype(F32), zp_ref[...].astype(F32), cw_ref[...], first)
        ych, rc = _rms(gb * cv)
        dnc = dmix[:, :D_CONV]
        dgc_ref[...] += jnp.sum(dnc * ych, axis=0, keepdims=True)
        dyc = _rms_bwd(dnc, ych, rc, gc_ref[...])
        dgb_ref[...] = (dyc * cv).astype(BF16)
        dcv_ref[...] = dyc * gb
        yah, ra = _rms(ya_ref[...])
        dna = dmix[:, D_CONV:]
        dga_ref[...] += jnp.sum(dna * yah, axis=0, keepdims=True)
        dya_ref[...] = _rms_bwd(dna, yah, ra, ga_ref[...])

    tile = lambda w: pl.BlockSpec((tm, w), lambda i: (i, 0))
    vec = pl.BlockSpec((1, D_CONV), lambda i: (0, 0))
    return pl.pallas_call(
        body, name="out_backward", grid=(s // tm,),
        in_specs=[tile(D_MODEL), tile(3 * D_CONV), pl.BlockSpec((8, 3 * D_CONV), lambda i: (jnp.maximum(i * hb - 1, 0), 0)),
                  tile(D_ATTN), _const_spec((8, D_CONV)), _const_spec((1, D_CONV)), _const_spec((1, D_ATTN)),
                  _const_spec((D_MODEL, D_MODEL))],
        out_specs=[tile(D_ATTN), tile(D_CONV), tile(D_CONV), vec, vec],
        out_shape=[jax.ShapeDtypeStruct((s, D_ATTN), F32), jax.ShapeDtypeStruct((s, D_CONV), BF16),
                   jax.ShapeDtypeStruct((s, D_CONV), F32), jax.ShapeDtypeStruct((1, D_CONV), F32),
                   jax.ShapeDtypeStruct((1, D_ATTN), F32)],
        compiler_params=_params(("arbitrary",), vmem=VMEM_LIMIT),
    )(dx2, z, z, y_attn, conv_w, g_conv_out, g_attn_out, w_out)


def _attention_backward(z, o, lse, dya, gq2, gk2, masks, grad_stacks):
    s = z.shape[0]
    seg = s // RESIDUES
    n_g = len(grad_stacks)

    def body(z_hbm, o_hbm, lse_hbm, do_hbm, gq_ref, gk_ref, mask_ref, *rest):
        stack_refs = rest[:n_g]
        dzq_ref, dzk_ref, dzv_ref, dgq_ref, dgk_ref = rest[n_g:n_g + 5]
        landing = rest[n_g + 5:2 * n_g + 5]
        (qf, kf, vf, dof, stats, dq, dk, dv, t0, t1, t2, zbuf, fbuf, sems, send_sems, recv_sems) = rest[2 * n_g + 5:]
        pair = pl.program_id(0)
        scatter = _DirectReduceScatter(stack_refs, landing, send_sems, recv_sems)

        @pl.when(pair == 0)
        def _():
            scatter.start()

        n_chunks = s // CHUNK
        bd_mean = _head_blockdiag(1.0 / HEAD_DIM)
        bd_sum = _head_blockdiag(1.0)
        gq = gq_ref[...] * (HEAD_DIM ** -0.5)
        gk = gk_ref[...]
        lane = lax.broadcasted_iota(jnp.int32, (CHUNK, 128), 1) % HEAD_DIM

        def chunk_copies(c, slot, n_z, n_f):
            rows = pl.ds(pl.multiple_of(c * CHUNK, CHUNK), CHUNK)
            cps = []
            for i, col in enumerate((12, 16, 20)[:n_z]):
                cols = pl.ds(pl.multiple_of((col + pair) * 128, 128), 128)
                cps.append(pltpu.make_async_copy(z_hbm.at[rows, cols], zbuf.at[slot, i], sems.at[slot, i]))
            for i, src in enumerate((do_hbm, o_hbm, lse_hbm)[:n_f]):
                cols = pl.ds(pl.multiple_of(pair * 128, 128), 128)
                cps.append(pltpu.make_async_copy(src.at[rows, cols], fbuf.at[slot, i], sems.at[slot, 3 + i]))
            return cps

        def prefetched(c, n_z, n_f):
            slot = c % 2

            @pl.when(c + 1 < n_chunks)
            def _():
                for cp in chunk_copies(c + 1, 1 - slot, n_z, n_f):
                    cp.start()

            for cp in chunk_copies(c, slot, n_z, n_f):
                cp.wait()
            return slot

        for cp in chunk_copies(0, 0, 3, 3):
            cp.start()

        def prep(c, carry):
            slot = prefetched(c, 3, 3)
            rows = pl.ds(pl.multiple_of(c * CHUNK, CHUNK), CHUNK)
            zq = zbuf[slot, 0].astype(F32)
            t0[...] = (zq * lax.rsqrt(_head_sum(zq * zq, bd_mean) + EPS)) * gq
            zk = zbuf[slot, 1].astype(F32)
            t1[...] = (zk * lax.rsqrt(_head_sum(zk * zk, bd_mean) + EPS)) * gk
            t2[...] = zbuf[slot, 2].astype(F32)
            _to_residue_major(qf, t0, c, seg)
            _to_residue_major(kf, t1, c, seg)
            _to_residue_major(vf, t2, c, seg)
            _to_residue_major(dof, fbuf.at[slot, 0], c, seg)
            delta = _head_sum(fbuf[slot, 0] * fbuf[slot, 1], bd_sum)
            t0[...] = jnp.where(lane == 0, fbuf[slot, 2], jnp.where(lane == 1, delta, 0.0))
            _to_residue_major(stats, t0, c, seg)
            zero = jnp.zeros((CHUNK, 128), F32)
            dq[rows, :] = zero
            dk[rows, :] = zero
            dv[rows, :] = zero
            return carry

        lax.fori_loop(0, n_chunks, prep, 0)

        head0 = lax.broadcasted_iota(jnp.int32, (BLOCK, 128), 1) < HEAD_DIM

        def run_branch(branch, d):
            n_runs, w, steps, group, chains = _branch_geometry(d, seg)
            slots = [(ci, b) for ci in range(chains) for b in range(group)]

            def step(u, grp):
                starts = [_run_starts(u + ci, grp, d, seg, n_runs, w, group) for ci in range(chains)]
                kt = [[_load_tile(kf, st, w, b).astype(BF16) for b in range(-1, group)] for st in starts]
                vt = [[_load_tile(vf, st, w, b).astype(BF16) for b in range(-1, group)] for st in starts]
                dk_t = [[jnp.zeros((BLOCK, 128), F32)] * (group + 1) for _ in range(chains)]
                dv_t = [[jnp.zeros((BLOCK, 128), F32)] * (group + 1) for _ in range(chains)]
                dq_t = []
                q2s, do2s, kcs, scores, dps, lses, deltas = [], [], [], [], [], [], []
                for ci, b in slots:
                    q2 = _stack_heads(_load_tile(qf, starts[ci], w, b), head0)
                    do2 = _stack_heads(_load_tile(dof, starts[ci], w, b), head0)
                    st = _load_tile(stats, starts[ci], w, b)
                    lses += [st[:, 0:1], st[:, HEAD_DIM:HEAD_DIM + 1]]
                    deltas += [st[:, 1:2], st[:, HEAD_DIM + 1:HEAD_DIM + 2]]
                    kc = jnp.concatenate([kt[ci][b], kt[ci][b + 1]], axis=0)
                    vc = jnp.concatenate([vt[ci][b], vt[ci][b + 1]], axis=0)
                    mask = mask_ref[branch, 0]
                    if b == 0:
                        mask = jnp.where(grp == 0, mask_ref[branch, 1], mask)
                    scores.append(_dot(q2, kc, NT) + mask)
                    dps.append(_dot(do2, vc, NT))
                    q2s.append(q2)
                    do2s.append(do2)
                    kcs.append(kc)
                p_all = jnp.exp(jnp.concatenate(scores, axis=0) - jnp.concatenate(lses, axis=0))
                ds_all = (p_all * (jnp.concatenate(dps, axis=0) - jnp.concatenate(deltas, axis=0))).astype(BF16)
                p_all = p_all.astype(BF16)
                for n, (ci, b) in enumerate(slots):
                    rows = slice(2 * BLOCK * n, 2 * BLOCK * (n + 1))
                    ds, q2, do2, kc = ds_all[rows], q2s[n], do2s[n], kcs[n]
                    dq2 = _dot(ds, kc, NN)
                    dq_t.append(jnp.where(head0, dq2[:BLOCK], dq2[BLOCK:]))
                    dkc = _dot(ds, q2, TN)
                    dvc = _dot(p_all[rows], do2, TN)
                    dk_t[ci][b] = dk_t[ci][b] + dkc[:BLOCK]
                    dk_t[ci][b + 1] = dk_t[ci][b + 1] + dkc[BLOCK:]
                    dv_t[ci][b] = dv_t[ci][b] + dvc[:BLOCK]
                    dv_t[ci][b + 1] = dv_t[ci][b + 1] + dvc[BLOCK:]
                for n, (ci, b) in enumerate(slots):
                    _add_tile(dq, starts[ci], w, b, dq_t[n])
                for ci in range(chains):
                    for b in range(-1, group):
                        _add_tile(dk, starts[ci], w, b, dk_t[ci][b + 1])
                        _add_tile(dv, starts[ci], w, b, dv_t[ci][b + 1])

            def unit(it, carry):
                def inner(grp, carry2):
                    step(it * chains, grp)
                    return carry2
                return lax.fori_loop(0, steps, inner, carry)

            lax.fori_loop(0, d // chains, unit, 0)

        for branch, d in enumerate(DILATIONS):
            run_branch(branch, d)

        for cp in chunk_copies(0, 0, 2, 0):
            cp.start()

        def finish(c, carry):
            acc_q, acc_k = carry
            rows = pl.ds(pl.multiple_of(c * CHUNK, CHUNK), CHUNK)
            _from_residue_major(t0, 0, dq, c, seg)
            _from_residue_major(t1, 0, dk, c, seg)
            _from_residue_major(t2, 0, dv, c, seg)
            slot = prefetched(c, 2, 0)
            zq = zbuf[slot, 0].astype(F32)
            rq = lax.rsqrt(_head_sum(zq * zq, bd_mean) + EPS)
            qh = zq * rq
            dqn = t0[...]
            acc_q = acc_q + jnp.sum(dqn * qh, axis=0, keepdims=True)
            t = dqn * gq
            dzq_ref[rows, :] = (rq * (t - qh * _head_sum(t * qh, bd_mean))).astype(BF16)
            zk = zbuf[slot, 1].astype(F32)
            rk = lax.rsqrt(_head_sum(zk * zk, bd_mean) + EPS)
            kh = zk * rk
            dkn = t1[...]
            acc_k = acc_k + jnp.sum(dkn * kh, axis=0, keepdims=True)
            t = dkn * gk
            dzk_ref[rows, :] = (rk * (t - kh * _head_sum(t * kh, bd_mean))).astype(BF16)
            dzv_ref[rows, :] = t2[...].astype(BF16)
            return acc_q, acc_k

        zero = jnp.zeros((1, 128), F32)
        acc_q, acc_k = lax.fori_loop(0, s // CHUNK, finish, (zero, zero))
        dgq_ref[...] = acc_q * (HEAD_DIM ** -0.5)
        dgk_ref[...] = acc_k

        @pl.when(pair == N_PAIRS - 1)
        def _():
            scatter.finish()

    hbm = pl.BlockSpec(memory_space=pl.ANY)
    gain = pl.BlockSpec((None, 1, 128), lambda p: (p, 0, 0))
    dz_spec = pl.BlockSpec((s, 128), lambda p: (0, p))
    out = pl.pallas_call(
        body, name="attention_backward", grid=(N_PAIRS,),
        in_specs=[hbm, hbm, hbm, hbm, _const_spec((1, 128)), _const_spec((1, 128)), _const_spec(masks.shape)] + [hbm] * n_g,
        out_specs=[dz_spec, dz_spec, dz_spec, gain, gain] + [hbm] * n_g,
        out_shape=[jax.ShapeDtypeStruct((s, D_ATTN), BF16)] * 3 + [jax.ShapeDtypeStruct((N_PAIRS, 1, 128), F32)] * 2
        + [jax.ShapeDtypeStruct((N_DEV - 1,) + g.shape[1:], g.dtype) for g in grad_stacks],
        scratch_shapes=[pltpu.VMEM((s, 128), F32)] * 8 + [pltpu.VMEM((CHUNK, 128), F32)] * 3
        + [pltpu.VMEM((2, 3, CHUNK, 128), BF16), pltpu.VMEM((2, 3, CHUNK, 128), F32), pltpu.SemaphoreType.DMA((2, 6)),
           pltpu.SemaphoreType.DMA((7 * n_g,)), pltpu.SemaphoreType.DMA((7 * n_g,))],
        compiler_params=_params(("arbitrary",), vmem=VMEM_LIMIT),
    )(z, o, lse, dya, gq2, gk2, masks, *grad_stacks)
    return out[:5], out[5:]


def _in_backward(dcv, dgb, dzq, dzk, dzv, z, x, dx2, g_mix, conv_w, w_in_t, tm):
    s = x.shape[0]
    hb = tm // 8
    last_halo = s // 8 - 1

    def body(dcv_ref, dcn_ref, dgb_ref, dzq_ref, dzk_ref, dzv_ref, z_ref, zp_ref, x_ref, dx2_ref, g_ref, cw_ref, w_ref,
             dz_ref, dx_ref, dg_ref, dcw_ref):
        i = pl.program_id(0)
        first = i == 0
        last = i == pl.num_programs(0) - 1

        @pl.when(first)
        def _():
            dg_ref[...] = jnp.zeros_like(dg_ref)
            dcw_ref[...] = jnp.zeros_like(dcw_ref)

        w = cw_ref[...]
        u, _, gc, cu, cu1, cu2, _ = _conv_forward(z_ref[...].astype(F32), zp_ref[...].astype(F32), w, first)
        dcv_t = dcv_ref[...]
        nxt = jnp.where(last, 0.0, dcn_ref[...])
        row = lax.broadcasted_iota(jnp.int32, dcv_t.shape, 0)
        up1 = jnp.where(row < tm - 1, pltpu.roll(dcv_t, tm - 1, 0), nxt[0:1, :])
        up2 = jnp.where(row < tm - 2, pltpu.roll(dcv_t, tm - 2, 0), jnp.where(row == tm - 2, nxt[0:1, :], nxt[1:2, :]))
        dcu = w[2:3, :] * dcv_t + w[1:2, :] * up1 + w[0:1, :] * up2
        dcw = jnp.concatenate([jnp.sum(dcv_t * cu2, axis=0, keepdims=True), jnp.sum(dcv_t * cu1, axis=0, keepdims=True),
                               jnp.sum(dcv_t * cu, axis=0, keepdims=True), jnp.zeros((5, D_CONV), F32)], axis=0)
        dcw_ref[...] += dcw
        dz_ref[:, :D_CONV] = (dcu * gc).astype(BF16)
        dz_ref[:, D_CONV:2 * D_CONV] = dgb_ref[...]
        dz_ref[:, 2 * D_CONV:3 * D_CONV] = (dcu * u).astype(BF16)
        dz_ref[:, 3 * D_CONV:3 * D_CONV + D_ATTN] = dzq_ref[...]
        dz_ref[:, 3 * D_CONV + D_ATTN:3 * D_CONV + 2 * D_ATTN] = dzk_ref[...]
        dz_ref[:, 3 * D_CONV + 2 * D_ATTN:] = dzv_ref[...]
        dh = _dot(dz_ref[...], w_ref[...], NN)
        xh, r = _rms(x_ref[...])
        dg_ref[...] += jnp.sum(dh * xh, axis=0, keepdims=True)
        dx_ref[...] = dx2_ref[...] + _rms_bwd(dh, xh, r, g_ref[...])

    tile = lambda w: pl.BlockSpec((tm, w), lambda i: (i, 0))
    return pl.pallas_call(
        body, name="in_backward", grid=(s // tm,),
        in_specs=[tile(D_CONV), pl.BlockSpec((8, D_CONV), lambda i: (jnp.minimum((i + 1) * hb, last_halo), 0)),
                  tile(D_CONV), tile(D_ATTN), tile(D_ATTN), tile(D_ATTN),
                  tile(3 * D_CONV), pl.BlockSpec((8, 3 * D_CONV), lambda i: (jnp.maximum(i * hb - 1, 0), 0)),
                  tile(D_MODEL), tile(D_MODEL), _const_spec((1, D_MODEL)), _const_spec((8, D_CONV)),
                  _const_spec((D_IN, D_MODEL))],
        out_specs=[tile(D_IN), tile(D_MODEL), pl.BlockSpec((1, D_MODEL), lambda i: (0, 0)),
                   pl.BlockSpec((8, D_CONV), lambda i: (0, 0))],
        out_shape=[jax.ShapeDtypeStruct((s, D_IN), BF16), jax.ShapeDtypeStruct((s, D_MODEL), F32),
                   jax.ShapeDtypeStruct((1, D_MODEL), F32), jax.ShapeDtypeStruct((8, D_CONV), F32)],
        compiler_params=_params(("arbitrary",), vmem=VMEM_LIMIT),
    )(dcv, dcv, dgb, dzq, dzk, dzv, z, z, x, dx2, g_mix, conv_w, w_in_t)


def _weight_grad(name, a, b, tn, tk, gate=None):
    s, n = a.shape
    steps = s // tk

    def body(*refs):
        gate_ref = refs[0] if gate is not None else None
        a_ref, b_ref, out_ref, acc = refs[1:] if gate is not None else refs
        k = pl.program_id(1)

        @pl.when(k == 0)
        def _():
            acc[...] = jnp.zeros_like(acc)

        lhs = a_ref[...]
        if gate is not None:
            gv = gate_ref[...].astype(F32)
            lhs = ((gv * jax.nn.sigmoid(gv)) * lhs.astype(F32)).astype(BF16)
        acc[...] += _dot(lhs, b_ref[...].astype(BF16), TN)

        @pl.when(k == steps - 1)
        def _():
            out_ref[...] = acc[...].astype(BF16)

    lhs_spec = pl.BlockSpec((tk, tn), lambda j, k: (k, j))
    return pl.pallas_call(
        body, name=name, grid=(n // tn, steps),
        in_specs=([lhs_spec] if gate is not None else []) + [lhs_spec, pl.BlockSpec((tk, D_MODEL), lambda j, k: (k, 0))],
        out_specs=pl.BlockSpec((tn, D_MODEL), lambda j, k: (j, 0)),
        out_shape=jax.ShapeDtypeStruct((n, D_MODEL), BF16),
        scratch_shapes=[pltpu.VMEM((tn, D_MODEL), F32)],
        compiler_params=_params(("parallel", "arbitrary"), vmem=VMEM_LIMIT),
    )(*([gate] if gate is not None else []), a, b)


def _adamw_math(w, g, m, v):
    nm = ADAM_B1 * m + (1.0 - ADAM_B1) * g
    nv = ADAM_B2 * v + (1.0 - ADAM_B2) * (g * g)
    m_hat = nm / (1.0 - ADAM_B1 ** ADAM_STEP)
    v_hat = nv / (1.0 - ADAM_B2 ** ADAM_STEP)
    return -ADAM_LR * (m_hat / (jnp.sqrt(v_hat) + ADAM_EPS) + ADAM_WD * w), nm, nv


def _adamw_received(name, device, w, stack, received, m, v):
    def body(dev_ref, w_ref, own_ref, recv_ref, m_ref, v_ref, g_ref, d_ref, nm_ref, nv_ref):
        del dev_ref
        g = own_ref[...].astype(F32)
        for k in range(N_DEV - 1):
            g = g + recv_ref[k].astype(F32)
        g_ref[...] = g
        d_ref[...], nm_ref[...], nv_ref[...] = _adamw_math(w_ref[...], g, m_ref[...], v_ref[...])

    rows = stack.shape[1]
    tr = rows // 4 if rows % 64 == 0 else rows // 2
    tile = pl.BlockSpec((tr, D_MODEL), lambda i, dref: (i, 0))
    return pl.pallas_call(
        body, name=name,
        grid_spec=pltpu.PrefetchScalarGridSpec(
            num_scalar_prefetch=1, grid=(rows // tr,),
            in_specs=[tile, pl.BlockSpec((None, tr, D_MODEL), lambda i, dref: (dref[0], i, 0)),
                      pl.BlockSpec((N_DEV - 1, tr, D_MODEL), lambda i, dref: (0, i, 0)), tile, tile],
            out_specs=[tile] * 4),
        out_shape=[jax.ShapeDtypeStruct((rows, D_MODEL), F32)] * 4,
        compiler_params=_params(("arbitrary",), vmem=VMEM_LIMIT),
    )(device, w, stack, received, m, v)


def _adamw(name, w, g, m, v):
    def body(w_ref, g_ref, m_ref, v_ref, d_ref, nm_ref, nv_ref):
        d_ref[...], nm_ref[...], nv_ref[...] = _adamw_math(w_ref[...], g_ref[...], m_ref[...], v_ref[...])

    rows = w.shape[0]
    tr = 256 if rows % 256 == 0 else rows
    spec = pl.BlockSpec((tr, w.shape[1]), lambda i: (i, 0))
    return pl.pallas_call(
        body, name=name, grid=(rows // tr,),
        in_specs=[spec] * 4, out_specs=[spec] * 3,
        out_shape=[jax.ShapeDtypeStruct(w.shape, F32)] * 3,
        compiler_params=_params(("parallel",)),
    )(w, g, m, v)


def kernel(x, g_mix, w_in, conv_w, g_q, g_k, g_conv_out, g_attn_out, w_out, g_ffn, w_gate, w_up, w_down, loss_target, m_g_mix, m_w_in, m_conv_w, m_g_q, m_g_k, m_g_conv_out, m_g_attn_out, m_w_out, m_g_ffn, m_w_gate, m_w_up, m_w_down, v_g_mix, v_w_in, v_conv_w, v_g_q, v_g_k, v_g_conv_out, v_g_attn_out, v_w_out, v_g_ffn, v_w_gate, v_w_up, v_w_down):
    s = x.shape[1]
    tm = min(512, s)
    xs, target = x[0], loss_target[0]
    px, py, pc = lax.axis_index("x"), lax.axis_index("y"), lax.axis_index("c")
    device = 4 * px + 2 * py + pc

    conv_block = jnp.zeros((8, 128), F32).at[:3, :HEAD_DIM].set(conv_w[0])
    first, shards = [w_in[0].T, conv_block], [w_out[0], w_gate[0].T, w_up[0].T, w_down[0]]
    w_in_g, conv_g, *shards = _gather_first_weights(first, [BF16, F32], shards)
    w_in_t = w_in_g.reshape(D_IN, D_MODEL)
    conv_full = jnp.transpose(conv_g[:, :3, :HEAD_DIM], (1, 0, 2)).reshape(3, D_CONV)
    conv_full = jnp.concatenate([conv_full, jnp.zeros((5, D_CONV), F32)], axis=0)
    gq2 = jnp.concatenate([g_q, g_q], axis=-1)
    gk2 = jnp.concatenate([g_k, g_k], axis=-1)

    h1, z = _in_projection(xs, g_mix, w_in_t, tm)
    masks = _permuted_masks()
    y_attn, lse, gathered = _attention_forward(z, gq2, gk2, masks, shards)
    w_out_f = gathered[0].reshape(D_MODEL, D_MODEL)
    w_gate_t, w_up_t, w_down_f = [g.reshape(D_FF, D_MODEL) for g in gathered[1:]]
    mix, x2, h2 = _mix_out(z, y_attn, xs, conv_full, g_conv_out, g_attn_out, g_ffn, w_out_f, tm)
    a, b, dy, dy_bf, sq_err = _ffn_forward(h2, x2, target, w_gate_t, w_up_t, w_down_f, tm)

    tk = min(2048, s)
    da, db, dx2, dx2_bf, dg_ffn = _ffn_backward(dy, a, b, x2, g_ffn, w_gate_t, w_up_t, w_down_f, tm)
    dya, dgb, dcv, dg_conv_out, dg_attn_out = _out_backward(dx2_bf, z, y_attn, conv_full, g_conv_out, g_attn_out, w_out_f, tm)
    early = [
        _weight_grad("grad_w_out", mix, dx2_bf, D_MODEL, tk),
        _weight_grad("grad_w_gate", da, h2, D_FF // 2, tk),
        _weight_grad("grad_w_up", db, h2, D_FF // 2, tk),
        _weight_grad("grad_w_down", b, dy_bf, D_FF // 2, tk, gate=a),
    ]
    early = [g.reshape(N_DEV, g.shape[0] // N_DEV, D_MODEL) for g in early]
    (dzq, dzk, dzv, dgq_pairs, dgk_pairs), landed = _attention_backward(z, y_attn, lse, dya, gq2, gk2, masks, early)
    dz, grad_x, dg_mix, dconv = _in_backward(dcv, dgb, dzq, dzk, dzv, z, xs, dx2, g_mix, conv_full, w_in_t, tm)
    dev = jnp.reshape(device, (1,)).astype(jnp.int32)
    scattered = dict(zip(("w_out", "w_gate", "w_up", "w_down"), zip(early, landed)))

    late = [_weight_grad("grad_w_in", dz, h1, D_IN // 2, tk).reshape(4, 2, D_IN // N_DEV, D_MODEL)]
    from_sibling = _reduce_scatter_cores(late)
    core = jnp.reshape(pc, (1,)).astype(jnp.int32)
    partials = _add_core_partials(core, late, from_sibling)
    from_chips = _reduce_scatter_chips(partials)
    chip = jnp.reshape(2 * px + py, (1,)).astype(jnp.int32)
    (gw_in_t,) = _add_chip_partials(chip, partials, from_chips)
    gw_in = gw_in_t.T

    dg_q = jnp.sum(dgq_pairs.reshape(2 * N_PAIRS, HEAD_DIM), axis=0)
    dg_k = jnp.sum(dgk_pairs.reshape(2 * N_PAIRS, HEAD_DIM), axis=0)
    zeros = lambda n: jnp.zeros((n,), F32)
    small = jnp.stack([
        dg_mix[0], dg_ffn[0],
        jnp.concatenate([dg_conv_out[0], dg_attn_out[0]]),
        jnp.concatenate([dg_q, dg_k, zeros(D_MODEL - 2 * HEAD_DIM)]),
        jnp.concatenate([dconv[0], dconv[1]]),
        jnp.concatenate([dconv[2], zeros(D_CONV)]),
        jnp.concatenate([sq_err[0, :1], zeros(D_MODEL - 1)]),
        zeros(D_MODEL),
    ])
    total = _all_reduce_small(small)
    loss = total[6, 0] * (0.5 / D_MODEL)
    gg_mix, gg_ffn = total[0:1], total[1:2]
    gg_conv_out, gg_attn_out = total[2:3, :D_CONV], total[2:3, D_CONV:]
    gg_q, gg_k = total[3:4, :HEAD_DIM], total[3:4, HEAD_DIM:2 * HEAD_DIM]
    conv_total = jnp.stack([total[4, :D_CONV], total[4, D_CONV:], total[5, :D_CONV]])
    g_conv = lax.dynamic_slice(conv_total, (0, device * HEAD_DIM), (3, HEAD_DIM))

    names = ["g_mix", "w_in", "conv_w", "g_q", "g_k", "g_conv_out", "g_attn_out", "w_out", "g_ffn", "w_gate", "w_up", "w_down"]
    weights = [g_mix, w_in[0], conv_w[0], g_q, g_k, g_conv_out, g_attn_out, w_out[0], g_ffn, w_gate[0], w_up[0], w_down[0]]
    grad_list = [gg_mix, None, g_conv, gg_q, gg_k, gg_conv_out, gg_attn_out, None, gg_ffn, None, None, None]
    m_list = [m_g_mix, m_w_in[0], m_conv_w[0], m_g_q, m_g_k, m_g_conv_out, m_g_attn_out, m_w_out[0], m_g_ffn, m_w_gate[0], m_w_up[0], m_w_down[0]]
    v_list = [v_g_mix, v_w_in[0], v_conv_w[0], v_g_q, v_g_k, v_g_conv_out, v_g_attn_out, v_w_out[0], v_g_ffn, v_w_gate[0], v_w_up[0], v_w_down[0]]
    stacked = {"w_in", "conv_w", "w_out", "w_gate", "w_up", "w_down"}
    transposed = {"w_gate", "w_up"}
    results = {}
    operands = {n: (w, g, m, v) for n, w, g, m, v in zip(names, weights, grad_list, m_list, v_list)}
    for name in scattered:
        w, _, m, v = operands[name]
        if name in transposed:
            w, m, v = w.T, m.T, v.T
        results[name] = _adamw_received("adamw_" + name, dev, w, *scattered[name], m, v)
        if name in transposed:
            results[name] = [t.T for t in results[name]]
    for name in names:
        if name not in results:
            w, g, m, v = operands[name]
            g = gw_in if name == "w_in" else g
            results[name] = [g, *_adamw("adamw_" + name, w, g, m, v)]
    outs = []
    for part in range(4):
        outs += [results[n][part][None] if n in stacked else results[n][part] for n in names]
    return (loss, grad_x[None], *outs)
```

```python
import jax
import jax.numpy as jnp
import numpy as np
from jax import lax
from jax.experimental import pallas as pl
from jax.experimental.pallas import tpu as pltpu

F32 = jnp.float32
BF16 = jnp.bfloat16

D_MODEL = 1024
D_CONV = 512
D_ATTN = 512
D_IN = 3072
D_FF = 2816
HEAD_DIM = 64
N_PAIRS = 4
BLOCK = 128
DILATIONS = (1, 4, 16)
N_DEV = 8
EPS = 1e-6
NEG = -1e30

ADAM_LR = 0.001
ADAM_B1 = 0.9
ADAM_B2 = 0.999
ADAM_EPS = 1e-08
ADAM_WD = 0.01
ADAM_STEP = 10

NT = (((1,), (1,)), ((), ()))
NN = (((1,), (0,)), ((), ()))
TN = (((0,), (0,)), ((), ()))
MESH = pl.DeviceIdType.MESH

VMEM_LIMIT = 56 * 1024 * 1024


def _params(semantics=None, vmem=None):
    kw = {}
    if semantics is not None:
        kw["dimension_semantics"] = semantics
    if vmem is not None:
        kw["vmem_limit_bytes"] = vmem
    return pltpu.CompilerParams(**kw)


def _dot(a, b, dims):
    return lax.dot_general(a, b, dims, preferred_element_type=F32)


def _const_spec(shape):
    n = len(shape)
    return pl.BlockSpec(shape, lambda *_: (0,) * n, pipeline_mode=pl.Buffered(1))


def _rms(x):
    r = lax.rsqrt(jnp.mean(x * x, axis=-1, keepdims=True) + EPS)
    return x * r, r


def _rms_bwd(dy, xh, r, g):
    t = dy * g
    return r * (t - xh * jnp.mean(t * xh, axis=-1, keepdims=True))


def _head_blockdiag(scale):
    i = lax.broadcasted_iota(jnp.int32, (128, 128), 0) // HEAD_DIM
    j = lax.broadcasted_iota(jnp.int32, (128, 128), 1) // HEAD_DIM
    return jnp.where(i == j, scale, 0.0).astype(BF16)


def _head_sum(x, bd):
    hi = x.astype(BF16)
    lo = (x - hi.astype(F32)).astype(BF16)
    return _dot(hi, bd, NN) + _dot(lo, bd, NN)


def _place():
    return lax.axis_index("x"), lax.axis_index("y"), lax.axis_index("c")


class _TwoLevelGather:
    def __init__(self, stacks, send_sems, recv_sems):
        self.stacks, self.send_sems, self.recv_sems = stacks, send_sems, recv_sems
        x, y, c = _place()
        self.c = c
        self.me, self.sibling = (x, y, c), (x, y, 1 - c)
        self.chips = [(x, 1 - y), (1 - x, y), (1 - x, 1 - y)]

    @staticmethod
    def index(p):
        return 4 * p[0] + 2 * p[1] + p[2]

    def copy(self, a, k, block, to, src=None):
        dst = self.stacks[a].at[self.index(block)]
        return pltpu.make_async_remote_copy(
            src_ref=dst if src is None else src, dst_ref=dst,
            send_sem=self.send_sems.at[7 * a + k], recv_sem=self.recv_sems.at[7 * a + k],
            device_id=to, device_id_type=MESH)

    def first(self, a, src=None):
        cps = [self.copy(a, 0, self.me, self.sibling, src)]
        return cps + [self.copy(a, 1 + j, self.me, (*chip, self.c), src) for j, chip in enumerate(self.chips)]

    def forwards(self, a):
        return [self.copy(a, 4 + j, (*chip, self.c), self.sibling) for j, chip in enumerate(self.chips)]

    def start(self, srcs=None):
        for a in range(len(self.stacks)):
            for cp in self.first(a, None if srcs is None else srcs[a]):
                cp.start()

    def forward(self):
        for a in range(len(self.stacks)):
            for j, (chip, fwd) in enumerate(zip(self.chips, self.forwards(a))):
                self.copy(a, 1 + j, (*chip, self.c), self.me).wait_recv()
                fwd.start()

    def finish(self):
        for a in range(len(self.stacks)):
            self.copy(a, 0, self.sibling, self.me).wait_recv()
            for j, chip in enumerate(self.chips):
                self.copy(a, 4 + j, (*chip, 1 - self.c), self.me).wait_recv()
        for a in range(len(self.stacks)):
            for cp in self.first(a) + self.forwards(a):
                cp.wait_send()


def _gather_first_weights(gathered_parts, gathered_dtypes, cast_parts):
    n, m = len(gathered_parts), len(cast_parts)

    def body(*refs):
        ins, casts_in = refs[:n], refs[n:n + m]
        outs, casts_out = refs[n + m:2 * n + m], refs[2 * n + m:2 * n + 2 * m]
        send_sems, recv_sems = refs[2 * n + 2 * m], refs[2 * n + 2 * m + 1]
        gather = _TwoLevelGather(outs, send_sems, recv_sems)
        for a in range(n):
            outs[a][gather.index(gather.me)] = ins[a][...].astype(outs[a].dtype)
        gather.start()
        for a in range(m):
            casts_out[a][...] = casts_in[a][...].astype(BF16)
        gather.forward()
        gather.finish()

    vm = pl.BlockSpec(memory_space=pltpu.VMEM)
    return pl.pallas_call(
        body, name="weight_all_gather",
        out_shape=[jax.ShapeDtypeStruct((N_DEV,) + p.shape, dt) for p, dt in zip(gathered_parts, gathered_dtypes)]
        + [jax.ShapeDtypeStruct(p.shape, BF16) for p in cast_parts],
        in_specs=[vm] * (n + m), out_specs=[vm] * (n + m),
        scratch_shapes=[pltpu.SemaphoreType.DMA((7 * n,)), pltpu.SemaphoreType.DMA((7 * n,))],
        compiler_params=_params(vmem=VMEM_LIMIT),
    )(*gathered_parts, *cast_parts)


class _DirectReduceScatter:
    def __init__(self, stacks, landing, send_sems, recv_sems):
        self.stacks, self.landing, self.send_sems, self.recv_sems = stacks, landing, send_sems, recv_sems
        self.place = _place()

    def copies(self):
        x, y, c = self.place
        cps = []
        for a in range(len(self.stacks)):
            for k in range(1, N_DEV):
                peer = (1 - x if k & 4 else x, 1 - y if k & 2 else y, 1 - c if k & 1 else c)
                cps.append(pltpu.make_async_remote_copy(
                    src_ref=self.stacks[a].at[4 * peer[0] + 2 * peer[1] + peer[2]], dst_ref=self.landing[a].at[k - 1],
                    send_sem=self.send_sems.at[7 * a + k - 1], recv_sem=self.recv_sems.at[7 * a + k - 1],
                    device_id=peer, device_id_type=MESH))
        return cps

    def start(self):
        for cp in self.copies():
            cp.start()

    def finish(self):
        for cp in self.copies():
            cp.wait_recv()
        for cp in self.copies():
            cp.wait_send()


def _all_reduce_small(v):
    def body(v_ref, out_ref, gathered, send_sems, recv_sems):
        x, y, c = _place()
        mine = 4 * x + 2 * y + c
        gathered[mine] = v_ref[...]

        def copy(k):
            peer = (1 - x if k & 4 else x, 1 - y if k & 2 else y, 1 - c if k & 1 else c)
            return pltpu.make_async_remote_copy(
                src_ref=gathered.at[mine], dst_ref=gathered.at[mine],
                send_sem=send_sems.at[k - 1], recv_sem=recv_sems.at[k - 1], device_id=peer, device_id_type=MESH)

        copies = [copy(k) for k in range(1, N_DEV)]
        for cp in copies:
            cp.start()
        for cp in copies:
            cp.wait_recv()
        for cp in copies:
            cp.wait_send()
        total = gathered[0]
        for d in range(1, N_DEV):
            total = total + gathered[d]
        out_ref[...] = total

    vm = pl.BlockSpec(memory_space=pltpu.VMEM)
    return pl.pallas_call(
        body, name="small_all_reduce",
        out_shape=jax.ShapeDtypeStruct(v.shape, F32),
        in_specs=[vm], out_specs=vm,
        scratch_shapes=[pltpu.VMEM((N_DEV,) + v.shape, F32),
                        pltpu.SemaphoreType.DMA((N_DEV - 1,)), pltpu.SemaphoreType.DMA((N_DEV - 1,))],
    )(v)


def _reduce_scatter_cores(grads):
    n = len(grads)

    def body(*refs):
        ins, outs = refs[:n], refs[n:2 * n]
        send_sems, recv_sems = refs[2 * n], refs[2 * n + 1]
        x, y, c = _place()
        copies = []
        for a in range(n):
            for k in range(4):
                copies.append(pltpu.make_async_remote_copy(
                    src_ref=ins[a].at[k, 1 - c], dst_ref=outs[a].at[k],
                    send_sem=send_sems.at[4 * a + k], recv_sem=recv_sems.at[4 * a + k],
                    device_id=(x, y, 1 - c), device_id_type=MESH))
        for cp in copies:
            cp.start()
        for cp in copies:
            cp.wait_recv()
        for cp in copies:
            cp.wait_send()

    hbm = pl.BlockSpec(memory_space=pl.ANY)
    return pl.pallas_call(
        body, name="grad_reduce_scatter_cores",
        out_shape=[jax.ShapeDtypeStruct((4,) + g.shape[2:], g.dtype) for g in grads],
        in_specs=[hbm] * n, out_specs=[hbm] * n,
        scratch_shapes=[pltpu.SemaphoreType.DMA((4 * n,)), pltpu.SemaphoreType.DMA((4 * n,))],
    )(*grads)


def _reduce_scatter_chips(partials):
    n = len(partials)

    def body(*refs):
        ins, outs = refs[:n], refs[n:2 * n]
        send_sems, recv_sems = refs[2 * n], refs[2 * n + 1]
        x, y, c = _place()
        copies = []
        for a in range(n):
            for j in (1, 2, 3):
                px = 1 - x if j & 2 else x
                py = 1 - y if j & 1 else y
                copies.append(pltpu.make_async_remote_copy(
                    src_ref=ins[a].at[2 * px + py], dst_ref=outs[a].at[j - 1],
                    send_sem=send_sems.at[3 * a + j - 1], recv_sem=recv_sems.at[3 * a + j - 1],
                    device_id=(px, py, c), device_id_type=MESH))
        for cp in copies:
            cp.start()
        for cp in copies:
            cp.wait_recv()
        for cp in copies:
            cp.wait_send()

    hbm = pl.BlockSpec(memory_space=pl.ANY)
    return pl.pallas_call(
        body, name="grad_reduce_scatter_chips",
        out_shape=[jax.ShapeDtypeStruct((3,) + p.shape[1:], p.dtype) for p in partials],
        in_specs=[hbm] * n, out_specs=[hbm] * n,
        scratch_shapes=[pltpu.SemaphoreType.DMA((3 * n,)), pltpu.SemaphoreType.DMA((3 * n,))],
    )(*partials)


def _add_core_partials(core, grads, received):
    n = len(grads)

    def body(core_ref, *refs):
        del core_ref
        gs, rs, outs = refs[:n], refs[n:2 * n], refs[2 * n:]
        for a in range(n):
            outs[a][...] = (gs[a][...].astype(F32) + rs[a][...].astype(F32)).astype(outs[a].dtype)

    in_specs = [pl.BlockSpec((None, None) + g.shape[2:], lambda k, cref: (k, cref[0], 0, 0)) for g in grads]
    in_specs += [pl.BlockSpec((None,) + r.shape[1:], lambda k, cref: (k, 0, 0)) for r in received]
    out_specs = [pl.BlockSpec((None,) + r.shape[1:], lambda k, cref: (k, 0, 0)) for r in received]
    return pl.pallas_call(
        body, name="grad_add_core_partials",
        grid_spec=pltpu.PrefetchScalarGridSpec(num_scalar_prefetch=1, grid=(4,), in_specs=in_specs, out_specs=out_specs),
        out_shape=[jax.ShapeDtypeStruct(r.shape, r.dtype) for r in received],
        compiler_params=_params(("arbitrary",)),
    )(core, *grads, *received)


def _add_chip_partials(chip, partials, received):
    n = len(partials)

    def body(chip_ref, *refs):
        del chip_ref
        ps, rs, outs = refs[:n], refs[n:2 * n], refs[2 * n:]
        for a in range(n):
            outs[a][...] = ((ps[a][...].astype(F32) + rs[a][0].astype(F32)) + rs[a][1].astype(F32)) + rs[a][2].astype(F32)

    in_specs = [pl.BlockSpec((None,) + p.shape[1:], lambda i, cref: (cref[0], 0, 0)) for p in partials]
    in_specs += [pl.BlockSpec(r.shape, lambda i, cref: (0, 0, 0)) for r in received]
    out_specs = [pl.BlockSpec(p.shape[1:], lambda i, cref: (0, 0)) for p in partials]
    return pl.pallas_call(
        body, name="grad_add_chip_partials",
        grid_spec=pltpu.PrefetchScalarGridSpec(num_scalar_prefetch=1, grid=(1,), in_specs=in_specs, out_specs=out_specs),
        out_shape=[jax.ShapeDtypeStruct(p.shape[1:], F32) for p in partials],
        compiler_params=_params(("arbitrary",), vmem=VMEM_LIMIT),
    )(chip, *partials, *received)


def _in_projection(x, g_mix, w_in_t, tm):
    s = x.shape[0]

    def body(x_ref, g_ref, w_ref, h_ref, z_ref):
        xh, _ = _rms(x_ref[...])
        h = (xh * g_ref[...]).astype(BF16)
        h_ref[...] = h
        for n0 in range(0, D_IN, 512):
            z_ref[:, n0:n0 + 512] = _dot(h, w_ref[n0:n0 + 512, :], NT).astype(BF16)

    return pl.pallas_call(
        body, name="in_projection", grid=(s // tm,),
        in_specs=[pl.BlockSpec((tm, D_MODEL), lambda i: (i, 0)), _const_spec((1, D_MODEL)), _const_spec((D_IN, D_MODEL))],
        out_specs=[pl.BlockSpec((tm, D_MODEL), lambda i: (i, 0)), pl.BlockSpec((tm, D_IN), lambda i: (i, 0))],
        out_shape=[jax.ShapeDtypeStruct((s, D_MODEL), BF16), jax.ShapeDtypeStruct((s, D_IN), BF16)],
        compiler_params=_params(("parallel",), vmem=VMEM_LIMIT),
    )(x, g_mix, w_in_t)


RESIDUES = 16
CHUNK = 512
GROUP = 8


def _branch_geometry(d, seg):
    n_runs = RESIDUES // d
    w = BLOCK // n_runs
    blocks = seg // w
    group = min(GROUP, blocks)
    chains = min(GROUP // group, d)
    return n_runs, w, blocks // group, group, chains


def _permuted_masks():
    out = []
    for d in DILATIONS:
        n_runs = RESIDUES // d
        w = BLOCK // n_runs
        p = np.arange(BLOCK)
        pos = (p % w) * n_runs + p // w
        dist = pos[:, None] - np.concatenate([pos - BLOCK, pos])[None, :]
        band = (dist >= 0) & (dist <= BLOCK)
        first = band & (np.arange(2 * BLOCK)[None, :] >= BLOCK)
        both = [np.where(m, 0.0, NEG).astype(np.float32) for m in (band, first)]
        out.append(np.stack([np.concatenate([m, m], axis=0) for m in both]))
    return jnp.asarray(np.stack(out))


def _run_starts(u, grp, d, seg, n_runs, w, group):
    return [(u + d * q) * seg + (w * group) * grp for q in range(n_runs)]


def _tile_rows(start, w, b):
    off = start + w * b
    if b < 0:
        off = jnp.maximum(off, 0)
    return pl.ds(pl.multiple_of(off, w), w)


def _load_tile(ref, starts, w, b):
    parts = [ref[_tile_rows(st, w, b), :] for st in starts]
    return parts[0] if len(parts) == 1 else jnp.concatenate(parts, axis=0)


def _store_tile(ref, starts, w, b, tile):
    for i, st in enumerate(starts):
        ref[_tile_rows(st, w, b), :] = tile[i * w:(i + 1) * w]


def _add_tile(ref, starts, w, b, tile):
    for i, st in enumerate(starts):
        ref[_tile_rows(st, w, b), :] += tile[i * w:(i + 1) * w]


def _to_residue_major(dst, src, c, seg):
    per = CHUNK // RESIDUES
    for r in range(RESIDUES):
        dst[pl.ds(pl.multiple_of(r * seg + c * per, per), per), :] = src[pl.ds(r, per, stride=RESIDUES), :]


def _from_residue_major(dst, dst_start, src, c, seg):
    per = CHUNK // RESIDUES
    for r in range(RESIDUES):
        dst[pl.ds(dst_start + r, per, stride=RESIDUES), :] = src[pl.ds(pl.multiple_of(r * seg + c * per, per), per), :]


def _stack_heads(t, head0):
    return jnp.concatenate([jnp.where(head0, t, 0.0), jnp.where(head0, 0.0, t)], axis=0).astype(BF16)


def _attention_forward(z, gq2, gk2, masks, shards):
    s = z.shape[0]
    seg = s // RESIDUES
    n_w = len(shards)

    def body(zq_ref, zk_ref, zv_ref, gq_ref, gk_ref, mask_ref, *rest):
        shard_refs, (o_ref, lse_ref), stacks = rest[:n_w], rest[n_w:n_w + 2], rest[n_w + 2:2 * n_w + 2]
        saved = rest[2 * n_w + 2]
        qf, kf, vf, o_st, l_st, tq, tk, tv, send_sems, recv_sems, local_sems, save_sems = rest[2 * n_w + 3:]
        pair = pl.program_id(0)
        gather = _TwoLevelGather(stacks, send_sems, recv_sems)

        def own_copies():
            mine = gather.index(gather.me)
            return [pltpu.make_async_copy(shard_refs[a], stacks[a].at[mine], local_sems.at[a]) for a in range(n_w)]

        @pl.when(pair == 0)
        def _():
            for cp in own_copies():
                cp.start()
            gather.start(shard_refs)

        @pl.when(pair == 2)
        def _():
            gather.forward()

        bd = _head_blockdiag(1.0 / HEAD_DIM)
        gq = gq_ref[...] * (HEAD_DIM ** -0.5)
        gk = gk_ref[...]

        def prep(c, carry):
            rows = pl.ds(pl.multiple_of(c * CHUNK, CHUNK), CHUNK)
            zq = zq_ref[rows, :].astype(F32)
            zk = zk_ref[rows, :].astype(F32)
            tq[...] = (zq * lax.rsqrt(_head_sum(zq * zq, bd) + EPS)) * gq
            tk[...] = (zk * lax.rsqrt(_head_sum(zk * zk, bd) + EPS)) * gk
            tv[...] = zv_ref[rows, :].astype(F32)
            _to_residue_major(qf, tq, c, seg)
            _to_residue_major(kf, tk, c, seg)
            _to_residue_major(vf, tv, c, seg)
            return carry

        lax.fori_loop(0, s // CHUNK, prep, 0)

        save_copies = [pltpu.make_async_copy(src, saved.at[pair, i], save_sems.at[i]) for i, src in enumerate((qf, kf, vf))]
        for cp in save_copies:
            cp.start()

        head0 = lax.broadcasted_iota(jnp.int32, (BLOCK, 128), 1) < HEAD_DIM

        def run_branch(branch, d):
            n_runs, w, steps, group, chains = _branch_geometry(d, seg)
            slots = [(ci, b) for ci in range(chains) for b in range(group)]

            def step(u, grp):
                starts = [_run_starts(u + ci, grp, d, seg, n_runs, w, group) for ci in range(chains)]
                kt = [[_load_tile(kf, st, w, b).astype(BF16) for b in range(-1, group)] for st in starts]
                vt = [[_load_tile(vf, st, w, b).astype(BF16) for b in range(-1, group)] for st in starts]
                scores = []
                for ci, b in slots:
                    q2 = _stack_heads(_load_tile(qf, starts[ci], w, b), head0)
                    kc = jnp.concatenate([kt[ci][b], kt[ci][b + 1]], axis=0)
                    mask = mask_ref[branch, 0]
                    if b == 0:
                        mask = jnp.where(grp == 0, mask_ref[branch, 1], mask)
                    scores.append(_dot(q2, kc, NT) + mask)
                sc = jnp.concatenate(scores, axis=0)
                m = jnp.max(sc, axis=-1, keepdims=True)
                e = jnp.exp((sc - m).astype(BF16))
                ones = jnp.ones((BLOCK, 128), BF16)
                fresh = []
                for n, (ci, b) in enumerate(slots):
                    rows = slice(2 * BLOCK * n, 2 * BLOCK * (n + 1))
                    v_aug = jnp.concatenate([jnp.concatenate([vt[ci][b], ones], axis=1),
                                             jnp.concatenate([vt[ci][b + 1], ones], axis=1)], axis=0)
                    o2 = _dot(e[rows], v_aug, NN)
                    den = o2[:, 128:]
                    o2 = o2[:, :128] * (1.0 / den)
                    l2 = m[rows] + jnp.log(den)
                    fresh.append((jnp.where(head0, o2[:BLOCK], o2[BLOCK:]), jnp.where(head0, l2[:BLOCK], l2[BLOCK:])))
                for n, (ci, b) in enumerate(slots):
                    o_new, l_new = fresh[n]
                    if branch > 0:
                        l_old = _load_tile(l_st, starts[ci], w, b)
                        diff = l_new - l_old
                        t = jnp.exp(-jnp.abs(diff))
                        big = 1.0 / (1.0 + t)
                        small = t * big
                        newer = diff >= 0.0
                        o_old = _load_tile(o_st, starts[ci], w, b)
                        o_new = o_old * jnp.where(newer, small, big) + o_new * jnp.where(newer, big, small)
                        l_new = jnp.maximum(l_old, l_new) + jnp.log(1.0 + t)
                    _store_tile(o_st, starts[ci], w, b, o_new)
                    _store_tile(l_st, starts[ci], w, b, l_new)

            def unit(it, carry):
                def inner(grp, carry2):
                    step(it * chains, grp)
                    return carry2
                return lax.fori_loop(0, steps, inner, carry)

            lax.fori_loop(0, d // chains, unit, 0)

        for branch, d in enumerate(DILATIONS):
            run_branch(branch, d)

        def finish(c, carry):
            _from_residue_major(o_ref, c * CHUNK, o_st, c, seg)
            _from_residue_major(lse_ref, c * CHUNK, l_st, c, seg)
            return carry

        lax.fori_loop(0, s // CHUNK, finish, 0)
        for cp in save_copies:
            cp.wait()

        @pl.when(pair == N_PAIRS - 1)
        def _():
            gather.finish()
            for cp in own_copies():
                cp.wait()

    def col(c0):
        return pl.BlockSpec((s, 128), lambda p: (0, c0 + p), pipeline_mode=pl.Buffered(1))

    hbm = pl.BlockSpec(memory_space=pl.ANY)
    out = pl.pallas_call(
        body, name="attention_forward", grid=(N_PAIRS,),
        in_specs=[col(12), col(16), col(20), _const_spec((1, 128)), _const_spec((1, 128)), _const_spec(masks.shape)]
        + [hbm] * n_w,
        out_specs=[pl.BlockSpec((s, 128), lambda p: (0, p)), pl.BlockSpec((s, 128), lambda p: (0, p))] + [hbm] * (n_w + 1),
        out_shape=[jax.ShapeDtypeStruct((s, D_ATTN), F32), jax.ShapeDtypeStruct((s, D_ATTN), F32)]
        + [jax.ShapeDtypeStruct((N_DEV,) + w.shape, w.dtype) for w in shards]
        + [jax.ShapeDtypeStruct((N_PAIRS, 3, s, 128), F32)],
        scratch_shapes=[pltpu.VMEM((s, 128), F32)] * 5 + [pltpu.VMEM((CHUNK, 128), F32)] * 3
        + [pltpu.SemaphoreType.DMA((7 * n_w,)), pltpu.SemaphoreType.DMA((7 * n_w,)), pltpu.SemaphoreType.DMA((n_w,)),
           pltpu.SemaphoreType.DMA((3,))],
        compiler_params=_params(("arbitrary",), vmem=VMEM_LIMIT),
    )(z, z, z, gq2, gk2, masks, *shards)
    return out[0], out[1], out[2:2 + n_w], out[2 + n_w]


def _conv_forward(zt, zprev, w, first):
    u, gb, gc = zt[:, :D_CONV], zt[:, D_CONV:2 * D_CONV], zt[:, 2 * D_CONV:]
    cu = gc * u
    cu_prev = jnp.where(first, 0.0, zprev[:, 2 * D_CONV:] * zprev[:, :D_CONV])
    row = lax.broadcasted_iota(jnp.int32, cu.shape, 0)
    cu1 = jnp.where(row >= 1, pltpu.roll(cu, 1, 0), cu_prev[7:8, :])
    cu2 = jnp.where(row >= 2, pltpu.roll(cu, 2, 0), jnp.where(row == 1, cu_prev[7:8, :], cu_prev[6:7, :]))
    cv = w[0:1, :] * cu2 + w[1:2, :] * cu1 + w[2:3, :] * cu
    return u, gb, gc, cu, cu1, cu2, cv


def _mix_out(z, y_attn, x, conv_w, g_conv_out, g_attn_out, g_ffn, w_out, tm):
    s = x.shape[0]
    hb = tm // 8

    def body(z_ref, zp_ref, ya_ref, x_ref, cw_ref, gc_ref, ga_ref, gf_ref, w_ref, mix_ref, x2_ref, h2_ref):
        first = pl.program_id(0) == 0
        zt = z_ref[...].astype(F32)
        _, gb, _, _, _, _, cv = _conv_forward(zt, zp_ref[...].astype(F32), cw_ref[...], first)
        nc, _ = _rms(gb * cv)
        na, _ = _rms(ya_ref[...])
        mix = jnp.concatenate([nc * gc_ref[...], na * ga_ref[...]], axis=-1).astype(BF16)
        mix_ref[...] = mix
        x2 = x_ref[...] + _dot(mix, w_ref[...], NN)
        x2_ref[...] = x2
        xh, _ = _rms(x2)
        h2_ref[...] = (xh * gf_ref[...]).astype(BF16)

    tile = lambda w: pl.BlockSpec((tm, w), lambda i: (i, 0))
    return pl.pallas_call(
        body, name="mix_out", grid=(s // tm,),
        in_specs=[tile(3 * D_CONV), pl.BlockSpec((8, 3 * D_CONV), lambda i: (jnp.maximum(i * hb - 1, 0), 0)),
                  tile(D_ATTN), tile(D_MODEL), _const_spec((8, D_CONV)), _const_spec((1, D_CONV)), _const_spec((1, D_ATTN)),
                  _const_spec((1, D_MODEL)), _const_spec((D_MODEL, D_MODEL))],
        out_specs=[tile(D_MODEL)] * 3,
        out_shape=[jax.ShapeDtypeStruct((s, D_MODEL), BF16), jax.ShapeDtypeStruct((s, D_MODEL), F32),
                   jax.ShapeDtypeStruct((s, D_MODEL), BF16)],
        compiler_params=_params(("arbitrary",), vmem=VMEM_LIMIT),
    )(z, z, y_attn, x, conv_w, g_conv_out, g_attn_out, g_ffn, w_out)


FF_CHUNK = 256


def _ffn_forward(h2, x2, target, w_gate_t, w_up_t, w_down, tm):
    s = h2.shape[0]

    def body(h_ref, x2_ref, t_ref, wg_ref, wu_ref, wd_ref, a_ref, b_ref, dy_ref, dyb_ref, loss_ref, f_buf):
        @pl.when(pl.program_id(0) == 0)
        def _():
            loss_ref[...] = jnp.zeros_like(loss_ref)

        h = h_ref[...]
        for c0 in range(0, D_FF, FF_CHUNK):
            rows = slice(c0, c0 + FF_CHUNK)
            a = _dot(h, wg_ref[rows, :], NT)
            b = _dot(h, wu_ref[rows, :], NT)
            a_ref[:, rows] = a.astype(BF16)
            b_ref[:, rows] = b.astype(BF16)
            f_buf[:, rows] = ((a * jax.nn.sigmoid(a)) * b).astype(BF16)
        y = x2_ref[...] + _dot(f_buf[...], wd_ref[...], NN)
        err = y - t_ref[...]
        dy = err * (1.0 / D_MODEL)
        dy_ref[...] = dy
        dyb_ref[...] = dy.astype(BF16)
        loss_ref[...] += jnp.sum(err * err)

    tile = lambda w: pl.BlockSpec((tm, w), lambda i: (i, 0))
    return pl.pallas_call(
        body, name="ffn_forward", grid=(s // tm,),
        in_specs=[tile(D_MODEL), tile(D_MODEL), tile(D_MODEL)] + [_const_spec((D_FF, D_MODEL))] * 3,
        out_specs=[tile(D_FF), tile(D_FF), tile(D_MODEL), tile(D_MODEL), pl.BlockSpec((8, 128), lambda i: (0, 0))],
        out_shape=[jax.ShapeDtypeStruct((s, D_FF), BF16), jax.ShapeDtypeStruct((s, D_FF), BF16),
                   jax.ShapeDtypeStruct((s, D_MODEL), F32), jax.ShapeDtypeStruct((s, D_MODEL), BF16),
                   jax.ShapeDtypeStruct((8, 128), F32)],
        scratch_shapes=[pltpu.VMEM((tm, D_FF), BF16)],
        compiler_params=_params(("arbitrary",), vmem=VMEM_LIMIT),
    )(h2, x2, target, w_gate_t, w_up_t, w_down)


def _ffn_backward(dy, a, b, x2, g_ffn, w_gate_t, w_up_t, w_down, tm):
    s = dy.shape[0]
    half = D_FF // 2
    pieces = [(p0, min(FF_CHUNK, half - p0)) for p0 in range(0, half, FF_CHUNK)]

    def body(dy_ref, a_ref, b_ref, x2_ref, g_ref, wg_ref, wu_ref, wd_ref, da_ref, db_ref, dx2_ref, dx2b_ref, dg_ref, acc):
        i, c = pl.program_id(0), pl.program_id(1)

        @pl.when((i == 0) & (c == 0))
        def _():
            dg_ref[...] = jnp.zeros_like(dg_ref)

        dyb = dy_ref[...].astype(BF16)
        for p0, width in pieces:
            rows = slice(p0, p0 + width)
            df = _dot(dyb, wd_ref[rows, :], NT)
            av = a_ref[:, rows].astype(F32)
            bv = b_ref[:, rows].astype(F32)
            sig = jax.nn.sigmoid(av)
            silu = av * sig
            da_ref[:, rows] = (df * bv * (sig * (1.0 + av * (1.0 - sig)))).astype(BF16)
            db_ref[:, rows] = (df * silu).astype(BF16)
        dh = _dot(da_ref[...], wg_ref[...], NN) + _dot(db_ref[...], wu_ref[...], NN)

        @pl.when(c == 0)
        def _():
            acc[...] = dh

        @pl.when(c == 1)
        def _():
            dh_all = acc[...] + dh
            xh, r = _rms(x2_ref[...])
            dg_ref[...] += jnp.sum(dh_all * xh, axis=0, keepdims=True)
            dx2 = dy_ref[...] + _rms_bwd(dh_all, xh, r, g_ref[...])
            dx2_ref[...] = dx2
            dx2b_ref[...] = dx2.astype(BF16)

    tile = lambda w: pl.BlockSpec((tm, w), lambda i, c: (i, 0))
    part = pl.BlockSpec((tm, half), lambda i, c: (i, c))
    weight = pl.BlockSpec((half, D_MODEL), lambda i, c: (c, 0))
    return pl.pallas_call(
        body, name="ffn_backward", grid=(s // tm, 2),
        in_specs=[tile(D_MODEL), part, part, tile(D_MODEL), _const_spec((1, D_MODEL)), weight, weight, weight],
        out_specs=[part, part, tile(D_MODEL), tile(D_MODEL), pl.BlockSpec((1, D_MODEL), lambda i, c: (0, 0))],
        out_shape=[jax.ShapeDtypeStruct((s, D_FF), BF16)] * 2
        + [jax.ShapeDtypeStruct((s, D_MODEL), F32), jax.ShapeDtypeStruct((s, D_MODEL), BF16),
           jax.ShapeDtypeStruct((1, D_MODEL), F32)],
        scratch_shapes=[pltpu.VMEM((tm, D_MODEL), F32)],
        compiler_params=_params(("arbitrary", "arbitrary"), vmem=VMEM_LIMIT),
    )(dy, a, b, x2, g_ffn, w_gate_t, w_up_t, w_down)


def _out_backward(dx2, z, y_attn, conv_w, g_conv_out, g_attn_out, w_out, tm):
    s = dx2.shape[0]
    hb = tm // 8

    def body(dx2_ref, z_ref, zp_ref, ya_ref, cw_ref, gc_ref, ga_ref, w_ref, dya_ref, dgb_ref, dcv_ref, dgc_ref, dga_ref):
        first = pl.program_id(0) == 0

        @pl.when(first)
        def _():
            dgc_ref[...] = jnp.zeros_like(dgc_ref)
            dga_ref[...] = jnp.zeros_like(dga_ref)

        dmix = _dot(dx2_ref[...], w_ref[...], NT)
        _, gb, _, _, _, _, cv = _conv_forward(z_ref[...].astype(F32), zp_ref[...].astype(F32), cw_ref[...], first)
        ych, rc = _rms(gb * cv)
        dnc = dmix[:, :D_CONV]
        dgc_ref[...] += jnp.sum(dnc * ych, axis=0, keepdims=True)
        dyc = _rms_bwd(dnc, ych, rc, gc_ref[...])
        dgb_ref[...] = (dyc * cv).astype(BF16)
        dcv_ref[...] = dyc * gb
        yah, ra = _rms(ya_ref[...])
        dna = dmix[:, D_CONV:]
        dga_ref[...] += jnp.sum(dna * yah, axis=0, keepdims=True)
        dya_ref[...] = _rms_bwd(dna, yah, ra, ga_ref[...])

    tile = lambda w: pl.BlockSpec((tm, w), lambda i: (i, 0))
    vec = pl.BlockSpec((1, D_CONV), lambda i: (0, 0))
    return pl.pallas_call(
        body, name="out_backward", grid=(s // tm,),
        in_specs=[tile(D_MODEL), tile(3 * D_CONV), pl.BlockSpec((8, 3 * D_CONV), lambda i: (jnp.maximum(i * hb - 1, 0), 0)),
                  tile(D_ATTN), _const_spec((8, D_CONV)), _const_spec((1, D_CONV)), _const_spec((1, D_ATTN)),
                  _const_spec((D_MODEL, D_MODEL))],
        out_specs=[tile(D_ATTN), tile(D_CONV), tile(D_CONV), vec, vec],
        out_shape=[jax.ShapeDtypeStruct((s, D_ATTN), F32), jax.ShapeDtypeStruct((s, D_CONV), BF16),
                   jax.ShapeDtypeStruct((s, D_CONV), F32), jax.ShapeDtypeStruct((1, D_CONV), F32),
                   jax.ShapeDtypeStruct((1, D_ATTN), F32)],
        compiler_params=_params(("arbitrary",), vmem=VMEM_LIMIT),
    )(dx2, z, z, y_attn, conv_w, g_conv_out, g_attn_out, w_out)


def _attention_backward(z, o, lse, dya, gq2, gk2, masks, saved, grad_stacks):
    s = z.shape[0]
    seg = s // RESIDUES
    n_g = len(grad_stacks)

    def body(z_hbm, o_hbm, lse_hbm, do_hbm, gq_ref, gk_ref, mask_ref, saved_hbm, *rest):
        stack_refs = rest[:n_g]
        dzq_ref, dzk_ref, dzv_ref, dgq_ref, dgk_ref = rest[n_g:n_g + 5]
        landing = rest[n_g + 5:2 * n_g + 5]
        (qf, kf, vf, dof, stats, dq, dk, dv, t0, t1, t2, zbuf, fbuf, sems, send_sems, recv_sems,
         load_sems) = rest[2 * n_g + 5:]
        pair = pl.program_id(0)
        scatter = _DirectReduceScatter(stack_refs, landing, send_sems, recv_sems)

        @pl.when(pair == 0)
        def _():
            scatter.start()

        saved_copies = [pltpu.make_async_copy(saved_hbm.at[pair, i], dst, load_sems.at[i]) for i, dst in enumerate((qf, kf, vf))]
        for cp in saved_copies:
            cp.start()

        n_chunks = s // CHUNK
        bd_mean = _head_blockdiag(1.0 / HEAD_DIM)
        bd_sum = _head_blockdiag(1.0)
        gq = gq_ref[...] * (HEAD_DIM ** -0.5)
        gk = gk_ref[...]
        lane = lax.broadcasted_iota(jnp.int32, (CHUNK, 128), 1) % HEAD_DIM

        def chunk_copies(c, slot, n_z, n_f):
            rows = pl.ds(pl.multiple_of(c * CHUNK, CHUNK), CHUNK)
            cps = []
            for i, col in enumerate((12, 16, 20)[:n_z]):
                cols = pl.ds(pl.multiple_of((col + pair) * 128, 128), 128)
                cps.append(pltpu.make_async_copy(z_hbm.at[rows, cols], zbuf.at[slot, i], sems.at[slot, i]))
            for i, src in enumerate((do_hbm, o_hbm, lse_hbm)[:n_f]):
                cols = pl.ds(pl.multiple_of(pair * 128, 128), 128)
                cps.append(pltpu.make_async_copy(src.at[rows, cols], fbuf.at[slot, i], sems.at[slot, 3 + i]))
            return cps

        def prefetched(c, n_z, n_f):
            slot = c % 2

            @pl.when(c + 1 < n_chunks)
            def _():
                for cp in chunk_copies(c + 1, 1 - slot, n_z, n_f):
                    cp.start()

            for cp in chunk_copies(c, slot, n_z, n_f):
                cp.wait()
            return slot

        for cp in chunk_copies(0, 0, 0, 3):
            cp.start()

        def prep(c, carry):
            slot = prefetched(c, 0, 3)
            rows = pl.ds(pl.multiple_of(c * CHUNK, CHUNK), CHUNK)
            _to_residue_major(dof, fbuf.at[slot, 0], c, seg)
            delta = _head_sum(fbuf[slot, 0] * fbuf[slot, 1], bd_sum)
            t0[...] = jnp.where(lane == 0, fbuf[slot, 2], jnp.where(lane == 1, delta, 0.0))
            _to_residue_major(stats, t0, c, seg)
            zero = jnp.zeros((CHUNK, 128), F32)
            dq[rows, :] = zero
            dk[rows, :] = zero
            dv[rows, :] = zero
            return carry

        lax.fori_loop(0, n_chunks, prep, 0)
        for cp in saved_copies:
            cp.wait()

        head0 = lax.broadcasted_iota(jnp.int32, (BLOCK, 128), 1) < HEAD_DIM

        def run_branch(branch, d):
            n_runs, w, steps, group, chains = _branch_geometry(d, seg)
            slots = [(ci, b) for ci in range(chains) for b in range(group)]

            def step(u, grp):
                starts = [_run_starts(u + ci, grp, d, seg, n_runs, w, group) for ci in range(chains)]
                kt = [[_load_tile(kf, st, w, b).astype(BF16) for b in range(-1, group)] for st in starts]
                vt = [[_load_tile(vf, st, w, b).astype(BF16) for b in range(-1, group)] for st in starts]
                dk_t = [[jnp.zeros((BLOCK, 128), F32)] * (group + 1) for _ in range(chains)]
                dv_t = [[jnp.zeros((BLOCK, 128), F32)] * (group + 1) for _ in range(chains)]
                dq_t = []
                q2s, do2s, kcs, scores, dps, lses, deltas = [], [], [], [], [], [], []
                for ci, b in slots:
                    q2 = _stack_heads(_load_tile(qf, starts[ci], w, b), head0)
                    do2 = _stack_heads(_load_tile(dof, starts[ci], w, b), head0)
                    st = _load_tile(stats, starts[ci], w, b)
                    lses += [st[:, 0:1], st[:, HEAD_DIM:HEAD_DIM + 1]]
                    deltas += [st[:, 1:2], st[:, HEAD_DIM + 1:HEAD_DIM + 2]]
                    kc = jnp.concatenate([kt[ci][b], kt[ci][b + 1]], axis=0)
                    vc = jnp.concatenate([vt[ci][b], vt[ci][b + 1]], axis=0)
                    mask = mask_ref[branch, 0]
                    if b == 0:
                        mask = jnp.where(grp == 0, mask_ref[branch, 1], mask)
                    scores.append(_dot(q2, kc, NT) + mask)
                    dps.append(_dot(do2, vc, NT))
                    q2s.append(q2)
                    do2s.append(do2)
                    kcs.append(kc)
                p_all = jnp.exp(jnp.concatenate(scores, axis=0) - jnp.concatenate(lses, axis=0))
                ds_all = (p_all * (jnp.concatenate(dps, axis=0) - jnp.concatenate(deltas, axis=0))).astype(BF16)
                p_all = p_all.astype(BF16)
                for n, (ci, b) in enumerate(slots):
                    rows = slice(2 * BLOCK * n, 2 * BLOCK * (n + 1))
                    ds, q2, do2, kc = ds_all[rows], q2s[n], do2s[n], kcs[n]
                    dq2 = _dot(ds, kc, NN)
                    dq_t.append(jnp.where(head0, dq2[:BLOCK], dq2[BLOCK:]))
                    dkc = _dot(ds, q2, TN)
                    dvc = _dot(p_all[rows], do2, TN)
                    dk_t[ci][b] = dk_t[ci][b] + dkc[:BLOCK]
                    dk_t[ci][b + 1] = dk_t[ci][b + 1] + dkc[BLOCK:]
                    dv_t[ci][b] = dv_t[ci][b] + dvc[:BLOCK]
                    dv_t[ci][b + 1] = dv_t[ci][b + 1] + dvc[BLOCK:]
                for n, (ci, b) in enumerate(slots):
                    _add_tile(dq, starts[ci], w, b, dq_t[n])
                for ci in range(chains):
                    for b in range(-1, group):
                        _add_tile(dk, starts[ci], w, b, dk_t[ci][b + 1])
                        _add_tile(dv, starts[ci], w, b, dv_t[ci][b + 1])

            def unit(it, carry):
                def inner(grp, carry2):
                    step(it * chains, grp)
                    return carry2
                return lax.fori_loop(0, steps, inner, carry)

            lax.fori_loop(0, d // chains, unit, 0)

        for branch, d in enumerate(DILATIONS):
            run_branch(branch, d)

        for cp in chunk_copies(0, 0, 2, 0):
            cp.start()

        def finish(c, carry):
            acc_q, acc_k = carry
            rows = pl.ds(pl.multiple_of(c * CHUNK, CHUNK), CHUNK)
            _from_residue_major(t0, 0, dq, c, seg)
            _from_residue_major(t1, 0, dk, c, seg)
            _from_residue_major(t2, 0, dv, c, seg)
            slot = prefetched(c, 2, 0)
            zq = zbuf[slot, 0].astype(F32)
            rq = lax.rsqrt(_head_sum(zq * zq, bd_mean) + EPS)
            qh = zq * rq
            dqn = t0[...]
            acc_q = acc_q + jnp.sum(dqn * qh, axis=0, keepdims=True)
            t = dqn * gq
            dzq_ref[rows, :] = (rq * (t - qh * _head_sum(t * qh, bd_mean))).astype(BF16)
            zk = zbuf[slot, 1].astype(F32)
            rk = lax.rsqrt(_head_sum(zk * zk, bd_mean) + EPS)
            kh = zk * rk
            dkn = t1[...]
            acc_k = acc_k + jnp.sum(dkn * kh, axis=0, keepdims=True)
            t = dkn * gk
            dzk_ref[rows, :] = (rk * (t - kh * _head_sum(t * kh, bd_mean))).astype(BF16)
            dzv_ref[rows, :] = t2[...].astype(BF16)
            return acc_q, acc_k

        zero = jnp.zeros((1, 128), F32)
        acc_q, acc_k = lax.fori_loop(0, s // CHUNK, finish, (zero, zero))
        dgq_ref[...] = acc_q * (HEAD_DIM ** -0.5)
        dgk_ref[...] = acc_k

        @pl.when(pair == N_PAIRS - 1)
        def _():
            scatter.finish()

    hbm = pl.BlockSpec(memory_space=pl.ANY)
    gain = pl.BlockSpec((None, 1, 128), lambda p: (p, 0, 0))
    dz_spec = pl.BlockSpec((s, 128), lambda p: (0, p))
    out = pl.pallas_call(
        body, name="attention_backward", grid=(N_PAIRS,),
        in_specs=[hbm, hbm, hbm, hbm, _const_spec((1, 128)), _const_spec((1, 128)), _const_spec(masks.shape), hbm]
        + [hbm] * n_g,
        out_specs=[dz_spec, dz_spec, dz_spec, gain, gain] + [hbm] * n_g,
        out_shape=[jax.ShapeDtypeStruct((s, D_ATTN), BF16)] * 3 + [jax.ShapeDtypeStruct((N_PAIRS, 1, 128), F32)] * 2
        + [jax.ShapeDtypeStruct((N_DEV - 1,) + g.shape[1:], g.dtype) for g in grad_stacks],
        scratch_shapes=[pltpu.VMEM((s, 128), F32)] * 8 + [pltpu.VMEM((CHUNK, 128), F32)] * 3
        + [pltpu.VMEM((2, 3, CHUNK, 128), BF16), pltpu.VMEM((2, 3, CHUNK, 128), F32), pltpu.SemaphoreType.DMA((2, 6)),
           pltpu.SemaphoreType.DMA((7 * n_g,)), pltpu.SemaphoreType.DMA((7 * n_g,)), pltpu.SemaphoreType.DMA((3,))],
        compiler_params=_params(("arbitrary",), vmem=VMEM_LIMIT),
    )(z, o, lse, dya, gq2, gk2, masks, saved, *grad_stacks)
    return out[:5], out[5:]


def _in_backward(dcv, dgb, dzq, dzk, dzv, z, x, dx2, g_mix, conv_w, w_in_t, tm):
    s = x.shape[0]
    hb = tm // 8
    last_halo = s // 8 - 1

    def body(dcv_ref, dcn_ref, dgb_ref, dzq_ref, dzk_ref, dzv_ref, z_ref, zp_ref, x_ref, dx2_ref, g_ref, cw_ref, w_ref,
             dz_ref, dx_ref, dg_ref, dcw_ref):
        i = pl.program_id(0)
        first = i == 0
        last = i == pl.num_programs(0) - 1

        @pl.when(first)
        def _():
            dg_ref[...] = jnp.zeros_like(dg_ref)
            dcw_ref[...] = jnp.zeros_like(dcw_ref)

        w = cw_ref[...]
        u, _, gc, cu, cu1, cu2, _ = _conv_forward(z_ref[...].astype(F32), zp_ref[...].astype(F32), w, first)
        dcv_t = dcv_ref[...]
        nxt = jnp.where(last, 0.0, dcn_ref[...])
        row = lax.broadcasted_iota(jnp.int32, dcv_t.shape, 0)
        up1 = jnp.where(row < tm - 1, pltpu.roll(dcv_t, tm - 1, 0), nxt[0:1, :])
        up2 = jnp.where(row < tm - 2, pltpu.roll(dcv_t, tm - 2, 0), jnp.where(row == tm - 2, nxt[0:1, :], nxt[1:2, :]))
        dcu = w[2:3, :] * dcv_t + w[1:2, :] * up1 + w[0:1, :] * up2
        dcw = jnp.concatenate([jnp.sum(dcv_t * cu2, axis=0, keepdims=True), jnp.sum(dcv_t * cu1, axis=0, keepdims=True),
                               jnp.sum(dcv_t * cu, axis=0, keepdims=True), jnp.zeros((5, D_CONV), F32)], axis=0)
        dcw_ref[...] += dcw
        dz_ref[:, :D_CONV] = (dcu * gc).astype(BF16)
        dz_ref[:, D_CONV:2 * D_CONV] = dgb_ref[...]
        dz_ref[:, 2 * D_CONV:3 * D_CONV] = (dcu * u).astype(BF16)
        dz_ref[:, 3 * D_CONV:3 * D_CONV + D_ATTN] = dzq_ref[...]
        dz_ref[:, 3 * D_CONV + D_ATTN:3 * D_CONV + 2 * D_ATTN] = dzk_ref[...]
        dz_ref[:, 3 * D_CONV + 2 * D_ATTN:] = dzv_ref[...]
        dh = _dot(dz_ref[...], w_ref[...], NN)
        xh, r = _rms(x_ref[...])
        dg_ref[...] += jnp.sum(dh * xh, axis=0, keepdims=True)
        dx_ref[...] = dx2_ref[...] + _rms_bwd(dh, xh, r, g_ref[...])

    tile = lambda w: pl.BlockSpec((tm, w), lambda i: (i, 0))
    return pl.pallas_call(
        body, name="in_backward", grid=(s // tm,),
        in_specs=[tile(D_CONV), pl.BlockSpec((8, D_CONV), lambda i: (jnp.minimum((i + 1) * hb, last_halo), 0)),
                  tile(D_CONV), tile(D_ATTN), tile(D_ATTN), tile(D_ATTN),
                  tile(3 * D_CONV), pl.BlockSpec((8, 3 * D_CONV), lambda i: (jnp.maximum(i * hb - 1, 0), 0)),
                  tile(D_MODEL), tile(D_MODEL), _const_spec((1, D_MODEL)), _const_spec((8, D_CONV)),
                  _const_spec((D_IN, D_MODEL))],
        out_specs=[tile(D_IN), tile(D_MODEL), pl.BlockSpec((1, D_MODEL), lambda i: (0, 0)),
                   pl.BlockSpec((8, D_CONV), lambda i: (0, 0))],
        out_shape=[jax.ShapeDtypeStruct((s, D_IN), BF16), jax.ShapeDtypeStruct((s, D_MODEL), F32),
                   jax.ShapeDtypeStruct((1, D_MODEL), F32), jax.ShapeDtypeStruct((8, D_CONV), F32)],
        compiler_params=_params(("arbitrary",), vmem=VMEM_LIMIT),
    )(dcv, dcv, dgb, dzq, dzk, dzv, z, z, x, dx2, g_mix, conv_w, w_in_t)


def _weight_grad(name, a, b, tn, tk, gate=None):
    s, n = a.shape
    steps = s // tk

    def body(*refs):
        gate_ref = refs[0] if gate is not None else None
        a_ref, b_ref, out_ref, acc = refs[1:] if gate is not None else refs
        k = pl.program_id(1)

        @pl.when(k == 0)
        def _():
            acc[...] = jnp.zeros_like(acc)

        lhs = a_ref[...]
        if gate is not None:
            gv = gate_ref[...].astype(F32)
            lhs = ((gv * jax.nn.sigmoid(gv)) * lhs.astype(F32)).astype(BF16)
        acc[...] += _dot(lhs, b_ref[...].astype(BF16), TN)

        @pl.when(k == steps - 1)
        def _():
            out_ref[...] = acc[...].astype(BF16)

    lhs_spec = pl.BlockSpec((tk, tn), lambda j, k: (k, j))
    return pl.pallas_call(
        body, name=name, grid=(n // tn, steps),
        in_specs=([lhs_spec] if gate is not None else []) + [lhs_spec, pl.BlockSpec((tk, D_MODEL), lambda j, k: (k, 0))],
        out_specs=pl.BlockSpec((tn, D_MODEL), lambda j, k: (j, 0)),
        out_shape=jax.ShapeDtypeStruct((n, D_MODEL), BF16),
        scratch_shapes=[pltpu.VMEM((tn, D_MODEL), F32)],
        compiler_params=_params(("parallel", "arbitrary"), vmem=VMEM_LIMIT),
    )(*([gate] if gate is not None else []), a, b)


def _adamw_math(w, g, m, v):
    nm = ADAM_B1 * m + (1.0 - ADAM_B1) * g
    nv = ADAM_B2 * v + (1.0 - ADAM_B2) * (g * g)
    m_hat = nm / (1.0 - ADAM_B1 ** ADAM_STEP)
    v_hat = nv / (1.0 - ADAM_B2 ** ADAM_STEP)
    return -ADAM_LR * (m_hat / (jnp.sqrt(v_hat) + ADAM_EPS) + ADAM_WD * w), nm, nv


def _adamw_received(name, device, w, stack, received, m, v):
    def body(dev_ref, w_ref, own_ref, recv_ref, m_ref, v_ref, g_ref, d_ref, nm_ref, nv_ref):
        del dev_ref
        g = own_ref[...].astype(F32)
        for k in range(N_DEV - 1):
            g = g + recv_ref[k].astype(F32)
        g_ref[...] = g
        d_ref[...], nm_ref[...], nv_ref[...] = _adamw_math(w_ref[...], g, m_ref[...], v_ref[...])

    rows = stack.shape[1]
    tr = rows // 4 if rows % 64 == 0 else rows // 2
    tile = pl.BlockSpec((tr, D_MODEL), lambda i, dref: (i, 0))
    return pl.pallas_call(
        body, name=name,
        grid_spec=pltpu.PrefetchScalarGridSpec(
            num_scalar_prefetch=1, grid=(rows // tr,),
            in_specs=[tile, pl.BlockSpec((None, tr, D_MODEL), lambda i, dref: (dref[0], i, 0)),
                      pl.BlockSpec((N_DEV - 1, tr, D_MODEL), lambda i, dref: (0, i, 0)), tile, tile],
            out_specs=[tile] * 4),
        out_shape=[jax.ShapeDtypeStruct((rows, D_MODEL), F32)] * 4,
        compiler_params=_params(("arbitrary",), vmem=VMEM_LIMIT),
    )(device, w, stack, received, m, v)


def _adamw(name, w, g, m, v):
    def body(w_ref, g_ref, m_ref, v_ref, d_ref, nm_ref, nv_ref):
        d_ref[...], nm_ref[...], nv_ref[...] = _adamw_math(w_ref[...], g_ref[...], m_ref[...], v_ref[...])

    rows = w.shape[0]
    tr = 256 if rows % 256 == 0 else rows
    spec = pl.BlockSpec((tr, w.shape[1]), lambda i: (i, 0))
    return pl.pallas_call(
        body, name=name, grid=(rows // tr,),
        in_specs=[spec] * 4, out_specs=[spec] * 3,
        out_shape=[jax.ShapeDtypeStruct(w.shape, F32)] * 3,
        compiler_params=_params(("parallel",)),
    )(w, g, m, v)


def kernel(x, g_mix, w_in, conv_w, g_q, g_k, g_conv_out, g_attn_out, w_out, g_ffn, w_gate, w_up, w_down, loss_target, m_g_mix, m_w_in, m_conv_w, m_g_q, m_g_k, m_g_conv_out, m_g_attn_out, m_w_out, m_g_ffn, m_w_gate, m_w_up, m_w_down, v_g_mix, v_w_in, v_conv_w, v_g_q, v_g_k, v_g_conv_out, v_g_attn_out, v_w_out, v_g_ffn, v_w_gate, v_w_up, v_w_down):
    s = x.shape[1]
    tm = min(512, s)
    xs, target = x[0], loss_target[0]
    px, py, pc = lax.axis_index("x"), lax.axis_index("y"), lax.axis_index("c")
    device = 4 * px + 2 * py + pc

    conv_block = jnp.zeros((8, 128), F32).at[:3, :HEAD_DIM].set(conv_w[0])
    first, shards = [w_in[0].T, conv_block], [w_out[0], w_gate[0].T, w_up[0].T, w_down[0]]
    w_in_g, conv_g, *shards = _gather_first_weights(first, [BF16, F32], shards)
    w_in_t = w_in_g.reshape(D_IN, D_MODEL)
    conv_full = jnp.transpose(conv_g[:, :3, :HEAD_DIM], (1, 0, 2)).reshape(3, D_CONV)
    conv_full = jnp.concatenate([conv_full, jnp.zeros((5, D_CONV), F32)], axis=0)
    gq2 = jnp.concatenate([g_q, g_q], axis=-1)
    gk2 = jnp.concatenate([g_k, g_k], axis=-1)

    h1, z = _in_projection(xs, g_mix, w_in_t, tm)
    masks = _permuted_masks()
    y_attn, lse, gathered, qkv_saved = _attention_forward(z, gq2, gk2, masks, shards)
    w_out_f = gathered[0].reshape(D_MODEL, D_MODEL)
    w_gate_t, w_up_t, w_down_f = [g.reshape(D_FF, D_MODEL) for g in gathered[1:]]
    mix, x2, h2 = _mix_out(z, y_attn, xs, conv_full, g_conv_out, g_attn_out, g_ffn, w_out_f, tm)
    a, b, dy, dy_bf, sq_err = _ffn_forward(h2, x2, target, w_gate_t, w_up_t, w_down_f, tm)

    tk = min(2048, s)
    da, db, dx2, dx2_bf, dg_ffn = _ffn_backward(dy, a, b, x2, g_ffn, w_gate_t, w_up_t, w_down_f, tm)
    dya, dgb, dcv, dg_conv_out, dg_attn_out = _out_backward(dx2_bf, z, y_attn, conv_full, g_conv_out, g_attn_out, w_out_f, tm)
    early = [
        _weight_grad("grad_w_out", mix, dx2_bf, D_MODEL, tk),
        _weight_grad("grad_w_gate", da, h2, D_FF // 2, tk),
        _weight_grad("grad_w_up", db, h2, D_FF // 2, tk),
        _weight_grad("grad_w_down", b, dy_bf, D_FF // 2, tk, gate=a),
    ]
    early = [g.reshape(N_DEV, g.shape[0] // N_DEV, D_MODEL) for g in early]
    (dzq, dzk, dzv, dgq_pairs, dgk_pairs), landed = _attention_backward(z, y_attn, lse, dya, gq2, gk2, masks, qkv_saved, early)
    dz, grad_x, dg_mix, dconv = _in_backward(dcv, dgb, dzq, dzk, dzv, z, xs, dx2, g_mix, conv_full, w_in_t, tm)
    dev = jnp.reshape(device, (1,)).astype(jnp.int32)
    scattered = dict(zip(("w_out", "w_gate", "w_up", "w_down"), zip(early, landed)))

    late = [_weight_grad("grad_w_in", dz, h1, D_IN // 2, tk).reshape(4, 2, D_IN // N_DEV, D_MODEL)]
    from_sibling = _reduce_scatter_cores(late)
    core = jnp.reshape(pc, (1,)).astype(jnp.int32)
    partials = _add_core_partials(core, late, from_sibling)
    from_chips = _reduce_scatter_chips(partials)
    chip = jnp.reshape(2 * px + py, (1,)).astype(jnp.int32)
    (gw_in_t,) = _add_chip_partials(chip, partials, from_chips)
    gw_in = gw_in_t.T

    dg_q = jnp.sum(dgq_pairs.reshape(2 * N_PAIRS, HEAD_DIM), axis=0)
    dg_k = jnp.sum(dgk_pairs.reshape(2 * N_PAIRS, HEAD_DIM), axis=0)
    zeros = lambda n: jnp.zeros((n,), F32)
    small = jnp.stack([
        dg_mix[0], dg_ffn[0],
        jnp.concatenate([dg_conv_out[0], dg_attn_out[0]]),
        jnp.concatenate([dg_q, dg_k, zeros(D_MODEL - 2 * HEAD_DIM)]),
        jnp.concatenate([dconv[0], dconv[1]]),
        jnp.concatenate([dconv[2], zeros(D_CONV)]),
        jnp.concatenate([sq_err[0, :1], zeros(D_MODEL - 1)]),
        zeros(D_MODEL),
    ])
    total = _all_reduce_small(small)
    loss = total[6, 0] * (0.5 / D_MODEL)
    gg_mix, gg_ffn = total[0:1], total[1:2]
    gg_conv_out, gg_attn_out = total[2:3, :D_CONV], total[2:3, D_CONV:]
    gg_q, gg_k = total[3:4, :HEAD_DIM], total[3:4, HEAD_DIM:2 * HEAD_DIM]
    conv_total = jnp.stack([total[4, :D_CONV], total[4, D_CONV:], total[5, :D_CONV]])
    g_conv = lax.dynamic_slice(conv_total, (0, device * HEAD_DIM), (3, HEAD_DIM))

    names = ["g_mix", "w_in", "conv_w", "g_q", "g_k", "g_conv_out", "g_attn_out", "w_out", "g_ffn", "w_gate", "w_up", "w_down"]
    weights = [g_mix, w_in[0], conv_w[0], g_q, g_k, g_conv_out, g_attn_out, w_out[0], g_ffn, w_gate[0], w_up[0], w_down[0]]
    grad_list = [gg_mix, None, g_conv, gg_q, gg_k, gg_conv_out, gg_attn_out, None, gg_ffn, None, None, None]
    m_list = [m_g_mix, m_w_in[0], m_conv_w[0], m_g_q, m_g_k, m_g_conv_out, m_g_attn_out, m_w_out[0], m_g_ffn, m_w_gate[0], m_w_up[0], m_w_down[0]]
    v_list = [v_g_mix, v_w_in[0], v_conv_w[0], v_g_q, v_g_k, v_g_conv_out, v_g_attn_out, v_w_out[0], v_g_ffn, v_w_gate[0], v_w_up[0], v_w_down[0]]
    stacked = {"w_in", "conv_w", "w_out", "w_gate", "w_up", "w_down"}
    transposed = {"w_gate", "w_up"}
    results = {}
    operands = {n: (w, g, m, v) for n, w, g, m, v in zip(names, weights, grad_list, m_list, v_list)}
    for name in scattered:
        w, _, m, v = operands[name]
        if name in transposed:
            w, m, v = w.T, m.T, v.T
        results[name] = _adamw_received("adamw_" + name, dev, w, *scattered[name], m, v)
        if name in transposed:
            results[name] = [t.T for t in results[name]]
    for name in names:
        if name not in results:
            w, g, m, v = operands[name]
            g = gw_in if name == "w_in" else g
            results[name] = [g, *_adamw("adamw_" + name, w, g, m, v)]
    outs = []
    for part in range(4):
        outs += [results[n][part][None] if n in stacked else results[n][part] for n in names]
    return (loss, grad_x[None], *outs)
```

```python
import jax
import jax.numpy as jnp
import numpy as np
from jax import lax
from jax.experimental import pallas as pl
from jax.experimental.pallas import tpu as pltpu

F32 = jnp.float32
BF16 = jnp.bfloat16

D_MODEL = 1024
D_CONV = 512
D_ATTN = 512
D_IN = 3072
D_FF = 2816
HEAD_DIM = 64
N_PAIRS = 4
BLOCK = 128
DILATIONS = (1, 4, 16)
N_DEV = 8
EPS = 1e-6
NEG = -1e30

ADAM_LR = 0.001
ADAM_B1 = 0.9
ADAM_B2 = 0.999
ADAM_EPS = 1e-08
ADAM_WD = 0.01
ADAM_STEP = 10

NT = (((1,), (1,)), ((), ()))
NN = (((1,), (0,)), ((), ()))
TN = (((0,), (0,)), ((), ()))
MESH = pl.DeviceIdType.MESH

VMEM_LIMIT = 56 * 1024 * 1024


def _params(semantics=None, vmem=None):
    kw = {}
    if semantics is not None:
        kw["dimension_semantics"] = semantics
    if vmem is not None:
        kw["vmem_limit_bytes"] = vmem
    return pltpu.CompilerParams(**kw)


def _dot(a, b, dims):
    return lax.dot_general(a, b, dims, preferred_element_type=F32)


def _const_spec(shape):
    n = len(shape)
    return pl.BlockSpec(shape, lambda *_: (0,) * n, pipeline_mode=pl.Buffered(1))


def _rms(x):
    r = lax.rsqrt(jnp.mean(x * x, axis=-1, keepdims=True) + EPS)
    return x * r, r


def _rms_bwd(dy, xh, r, g):
    t = dy * g
    return r * (t - xh * jnp.mean(t * xh, axis=-1, keepdims=True))


def _head_blockdiag(scale):
    i = lax.broadcasted_iota(jnp.int32, (128, 128), 0) // HEAD_DIM
    j = lax.broadcasted_iota(jnp.int32, (128, 128), 1) // HEAD_DIM
    return jnp.where(i == j, scale, 0.0).astype(BF16)


def _head_sum(x, bd):
    hi = x.astype(BF16)
    lo = (x - hi.astype(F32)).astype(BF16)
    return _dot(hi, bd, NN) + _dot(lo, bd, NN)


def _place():
    return lax.axis_index("x"), lax.axis_index("y"), lax.axis_index("c")


class _TwoLevelGather:
    def __init__(self, stacks, send_sems, recv_sems):
        self.stacks, self.send_sems, self.recv_sems = stacks, send_sems, recv_sems
        x, y, c = _place()
        self.c = c
        self.me, self.sibling = (x, y, c), (x, y, 1 - c)
        self.chips = [(x, 1 - y), (1 - x, y), (1 - x, 1 - y)]

    @staticmethod
    def index(p):
        return 4 * p[0] + 2 * p[1] + p[2]

    def copy(self, a, k, block, to, src=None):
        dst = self.stacks[a].at[self.index(block)]
        return pltpu.make_async_remote_copy(
            src_ref=dst if src is None else src, dst_ref=dst,
            send_sem=self.send_sems.at[7 * a + k], recv_sem=self.recv_sems.at[7 * a + k],
            device_id=to, device_id_type=MESH)

    def first(self, a, src=None):
        cps = [self.copy(a, 0, self.me, self.sibling, src)]
        return cps + [self.copy(a, 1 + j, self.me, (*chip, self.c), src) for j, chip in enumerate(self.chips)]

    def forwards(self, a):
        return [self.copy(a, 4 + j, (*chip, self.c), self.sibling) for j, chip in enumerate(self.chips)]

    def start(self, srcs=None):
        for a in range(len(self.stacks)):
            for cp in self.first(a, None if srcs is None else srcs[a]):
                cp.start()

    def forward(self):
        for a in range(len(self.stacks)):
            for j, (chip, fwd) in enumerate(zip(self.chips, self.forwards(a))):
                self.copy(a, 1 + j, (*chip, self.c), self.me).wait_recv()
                fwd.start()

    def finish(self):
        for a in range(len(self.stacks)):
            self.copy(a, 0, self.sibling, self.me).wait_recv()
            for j, chip in enumerate(self.chips):
                self.copy(a, 4 + j, (*chip, 1 - self.c), self.me).wait_recv()
        for a in range(len(self.stacks)):
            for cp in self.first(a) + self.forwards(a):
                cp.wait_send()


def _gather_first_weights(gathered_parts, gathered_dtypes, cast_parts):
    n, m = len(gathered_parts), len(cast_parts)

    def body(*refs):
        ins, casts_in = refs[:n], refs[n:n + m]
        outs, casts_out = refs[n + m:2 * n + m], refs[2 * n + m:2 * n + 2 * m]
        send_sems, recv_sems = refs[2 * n + 2 * m], refs[2 * n + 2 * m + 1]
        gather = _TwoLevelGather(outs, send_sems, recv_sems)
        for a in range(n):
            outs[a][gather.index(gather.me)] = ins[a][...].astype(outs[a].dtype)
        gather.start()
        for a in range(m):
            casts_out[a][...] = casts_in[a][...].astype(BF16)
        gather.forward()
        gather.finish()

    vm = pl.BlockSpec(memory_space=pltpu.VMEM)
    return pl.pallas_call(
        body, name="weight_all_gather",
        out_shape=[jax.ShapeDtypeStruct((N_DEV,) + p.shape, dt) for p, dt in zip(gathered_parts, gathered_dtypes)]
        + [jax.ShapeDtypeStruct(p.shape, BF16) for p in cast_parts],
        in_specs=[vm] * (n + m), out_specs=[vm] * (n + m),
        scratch_shapes=[pltpu.SemaphoreType.DMA((7 * n,)), pltpu.SemaphoreType.DMA((7 * n,))],
        compiler_params=_params(vmem=VMEM_LIMIT),
    )(*gathered_parts, *cast_parts)


class _DirectReduceScatter:
    def __init__(self, stacks, landing, send_sems, recv_sems):
        self.stacks, self.landing, self.send_sems, self.recv_sems = stacks, landing, send_sems, recv_sems
        self.place = _place()

    def copies(self):
        x, y, c = self.place
        cps = []
        for a in range(len(self.stacks)):
            for k in range(1, N_DEV):
                peer = (1 - x if k & 4 else x, 1 - y if k & 2 else y, 1 - c if k & 1 else c)
                cps.append(pltpu.make_async_remote_copy(
                    src_ref=self.stacks[a].at[4 * peer[0] + 2 * peer[1] + peer[2]], dst_ref=self.landing[a].at[k - 1],
                    send_sem=self.send_sems.at[7 * a + k - 1], recv_sem=self.recv_sems.at[7 * a + k - 1],
                    device_id=peer, device_id_type=MESH))
        return cps

    def start(self):
        for cp in self.copies():
            cp.start()

    def finish(self):
        for cp in self.copies():
            cp.wait_recv()
        for cp in self.copies():
            cp.wait_send()


def _all_reduce_small(v):
    def body(v_ref, out_ref, gathered, send_sems, recv_sems):
        x, y, c = _place()
        mine = 4 * x + 2 * y + c
        gathered[mine] = v_ref[...]

        def copy(k):
            peer = (1 - x if k & 4 else x, 1 - y if k & 2 else y, 1 - c if k & 1 else c)
            return pltpu.make_async_remote_copy(
                src_ref=gathered.at[mine], dst_ref=gathered.at[mine],
                send_sem=send_sems.at[k - 1], recv_sem=recv_sems.at[k - 1], device_id=peer, device_id_type=MESH)

        copies = [copy(k) for k in range(1, N_DEV)]
        for cp in copies:
            cp.start()
        for cp in copies:
            cp.wait_recv()
        for cp in copies:
            cp.wait_send()
        total = gathered[0]
        for d in range(1, N_DEV):
            total = total + gathered[d]
        out_ref[...] = total

    vm = pl.BlockSpec(memory_space=pltpu.VMEM)
    return pl.pallas_call(
        body, name="small_all_reduce",
        out_shape=jax.ShapeDtypeStruct(v.shape, F32),
        in_specs=[vm], out_specs=vm,
        scratch_shapes=[pltpu.VMEM((N_DEV,) + v.shape, F32),
                        pltpu.SemaphoreType.DMA((N_DEV - 1,)), pltpu.SemaphoreType.DMA((N_DEV - 1,))],
    )(v)


def _reduce_scatter_cores(grads):
    n = len(grads)

    def body(*refs):
        ins, outs = refs[:n], refs[n:2 * n]
        send_sems, recv_sems = refs[2 * n], refs[2 * n + 1]
        x, y, c = _place()
        copies = []
        for a in range(n):
            for k in range(4):
                copies.append(pltpu.make_async_remote_copy(
                    src_ref=ins[a].at[k, 1 - c], dst_ref=outs[a].at[k],
                    send_sem=send_sems.at[4 * a + k], recv_sem=recv_sems.at[4 * a + k],
                    device_id=(x, y, 1 - c), device_id_type=MESH))
        for cp in copies:
            cp.start()
        for cp in copies:
            cp.wait_recv()
        for cp in copies:
            cp.wait_send()

    hbm = pl.BlockSpec(memory_space=pl.ANY)
    return pl.pallas_call(
        body, name="grad_reduce_scatter_cores",
        out_shape=[jax.ShapeDtypeStruct((4,) + g.shape[2:], g.dtype) for g in grads],
        in_specs=[hbm] * n, out_specs=[hbm] * n,
        scratch_shapes=[pltpu.SemaphoreType.DMA((4 * n,)), pltpu.SemaphoreType.DMA((4 * n,))],
    )(*grads)


def _reduce_scatter_chips(partials):
    n = len(partials)

    def body(*refs):
        ins, outs = refs[:n], refs[n:2 * n]
        send_sems, recv_sems = refs[2 * n], refs[2 * n + 1]
        x, y, c = _place()
        copies = []
        for a in range(n):
            for j in (1, 2, 3):
                px = 1 - x if j & 2 else x
                py = 1 - y if j & 1 else y
                copies.append(pltpu.make_async_remote_copy(
                    src_ref=ins[a].at[2 * px + py], dst_ref=outs[a].at[j - 1],
                    send_sem=send_sems.at[3 * a + j - 1], recv_sem=recv_sems.at[3 * a + j - 1],
                    device_id=(px, py, c), device_id_type=MESH))
        for cp in copies:
            cp.start()
        for cp in copies:
            cp.wait_recv()
        for cp in copies:
            cp.wait_send()

    hbm = pl.BlockSpec(memory_space=pl.ANY)
    return pl.pallas_call(
        body, name="grad_reduce_scatter_chips",
        out_shape=[jax.ShapeDtypeStruct((3,) + p.shape[1:], p.dtype) for p in partials],
        in_specs=[hbm] * n, out_specs=[hbm] * n,
        scratch_shapes=[pltpu.SemaphoreType.DMA((3 * n,)), pltpu.SemaphoreType.DMA((3 * n,))],
    )(*partials)


def _add_core_partials(core, grads, received):
    n = len(grads)

    def body(core_ref, *refs):
        del core_ref
        gs, rs, outs = refs[:n], refs[n:2 * n], refs[2 * n:]
        for a in range(n):
            outs[a][...] = (gs[a][...].astype(F32) + rs[a][...].astype(F32)).astype(outs[a].dtype)

    in_specs = [pl.BlockSpec((None, None) + g.shape[2:], lambda k, cref: (k, cref[0], 0, 0)) for g in grads]
    in_specs += [pl.BlockSpec((None,) + r.shape[1:], lambda k, cref: (k, 0, 0)) for r in received]
    out_specs = [pl.BlockSpec((None,) + r.shape[1:], lambda k, cref: (k, 0, 0)) for r in received]
    return pl.pallas_call(
        body, name="grad_add_core_partials",
        grid_spec=pltpu.PrefetchScalarGridSpec(num_scalar_prefetch=1, grid=(4,), in_specs=in_specs, out_specs=out_specs),
        out_shape=[jax.ShapeDtypeStruct(r.shape, r.dtype) for r in received],
        compiler_params=_params(("arbitrary",)),
    )(core, *grads, *received)


def _add_chip_partials(chip, partials, received):
    n = len(partials)

    def body(chip_ref, *refs):
        del chip_ref
        ps, rs, outs = refs[:n], refs[n:2 * n], refs[2 * n:]
        for a in range(n):
            outs[a][...] = ((ps[a][...].astype(F32) + rs[a][0].astype(F32)) + rs[a][1].astype(F32)) + rs[a][2].astype(F32)

    in_specs = [pl.BlockSpec((None,) + p.shape[1:], lambda i, cref: (cref[0], 0, 0)) for p in partials]
    in_specs += [pl.BlockSpec(r.shape, lambda i, cref: (0, 0, 0)) for r in received]
    out_specs = [pl.BlockSpec(p.shape[1:], lambda i, cref: (0, 0)) for p in partials]
    return pl.pallas_call(
        body, name="grad_add_chip_partials",
        grid_spec=pltpu.PrefetchScalarGridSpec(num_scalar_prefetch=1, grid=(1,), in_specs=in_specs, out_specs=out_specs),
        out_shape=[jax.ShapeDtypeStruct(p.shape[1:], F32) for p in partials],
        compiler_params=_params(("arbitrary",), vmem=VMEM_LIMIT),
    )(chip, *partials, *received)


def _in_projection(x, g_mix, w_in_t, tm):
    s = x.shape[0]

    def body(x_ref, g_ref, w_ref, h_ref, z_ref):
        xh, _ = _rms(x_ref[...])
        h = (xh * g_ref[...]).astype(BF16)
        h_ref[...] = h
        for n0 in range(0, D_IN, 512):
            z_ref[:, n0:n0 + 512] = _dot(h, w_ref[n0:n0 + 512, :], NT).astype(BF16)

    return pl.pallas_call(
        body, name="in_projection", grid=(s // tm,),
        in_specs=[pl.BlockSpec((tm, D_MODEL), lambda i: (i, 0)), _const_spec((1, D_MODEL)), _const_spec((D_IN, D_MODEL))],
        out_specs=[pl.BlockSpec((tm, D_MODEL), lambda i: (i, 0)), pl.BlockSpec((tm, D_IN), lambda i: (i, 0))],
        out_shape=[jax.ShapeDtypeStruct((s, D_MODEL), BF16), jax.ShapeDtypeStruct((s, D_IN), BF16)],
        compiler_params=_params(("parallel",), vmem=VMEM_LIMIT),
    )(x, g_mix, w_in_t)


RESIDUES = 16
CHUNK = 512
GROUP = 8


def _branch_geometry(d, seg):
    n_runs = RESIDUES // d
    w = BLOCK // n_runs
    blocks = seg // w
    group = min(GROUP, blocks)
    chains = min(GROUP // group, d)
    return n_runs, w, blocks // group, group, chains


def _permuted_masks():
    out = []
    for d in DILATIONS:
        n_runs = RESIDUES // d
        w = BLOCK // n_runs
        p = np.arange(BLOCK)
        pos = (p % w) * n_runs + p // w
        dist = pos[:, None] - np.concatenate([pos - BLOCK, pos])[None, :]
        band = (dist >= 0) & (dist <= BLOCK)
        first = band & (np.arange(2 * BLOCK)[None, :] >= BLOCK)
        both = [np.where(m, 0.0, NEG).astype(np.float32) for m in (band, first)]
        out.append(np.stack([np.concatenate([m, m], axis=0) for m in both]))
    return jnp.asarray(np.stack(out))


def _run_starts(u, grp, d, seg, n_runs, w, group):
    return [(u + d * q) * seg + (w * group) * grp for q in range(n_runs)]


def _tile_rows(start, w, b):
    off = start + w * b
    if b < 0:
        off = jnp.maximum(off, 0)
    return pl.ds(pl.multiple_of(off, w), w)


def _load_tile(ref, starts, w, b):
    parts = [ref[_tile_rows(st, w, b), :] for st in starts]
    return parts[0] if len(parts) == 1 else jnp.concatenate(parts, axis=0)


def _store_tile(ref, starts, w, b, tile):
    for i, st in enumerate(starts):
        ref[_tile_rows(st, w, b), :] = tile[i * w:(i + 1) * w]


def _add_tile(ref, starts, w, b, tile):
    for i, st in enumerate(starts):
        ref[_tile_rows(st, w, b), :] += tile[i * w:(i + 1) * w]


def _to_residue_major(dst, src, c, seg):
    per = CHUNK // RESIDUES
    for r in range(RESIDUES):
        dst[pl.ds(pl.multiple_of(r * seg + c * per, per), per), :] = src[pl.ds(r, per, stride=RESIDUES), :]


def _from_residue_major(dst, dst_start, src, c, seg):
    per = CHUNK // RESIDUES
    for r in range(RESIDUES):
        dst[pl.ds(dst_start + r, per, stride=RESIDUES), :] = src[pl.ds(pl.multiple_of(r * seg + c * per, per), per), :]


def _stack_heads(t, head0):
    return jnp.concatenate([jnp.where(head0, t, 0.0), jnp.where(head0, 0.0, t)], axis=0).astype(BF16)


def _attention_forward(z, gq2, gk2, masks, shards):
    s = z.shape[0]
    seg = s // RESIDUES
    n_w = len(shards)

    def body(zq_ref, zk_ref, zv_ref, gq_ref, gk_ref, mask_ref, *rest):
        shard_refs, (o_ref, lse_ref), stacks = rest[:n_w], rest[n_w:n_w + 2], rest[n_w + 2:2 * n_w + 2]
        saved = rest[2 * n_w + 2]
        qf, kf, vf, o_st, l_st, tq, tk, tv, send_sems, recv_sems, local_sems, save_sems = rest[2 * n_w + 3:]
        pair = pl.program_id(0)
        gather = _TwoLevelGather(stacks, send_sems, recv_sems)

        def own_copies():
            mine = gather.index(gather.me)
            return [pltpu.make_async_copy(shard_refs[a], stacks[a].at[mine], local_sems.at[a]) for a in range(n_w)]

        @pl.when(pair == 0)
        def _():
            for cp in own_copies():
                cp.start()
            gather.start(shard_refs)

        @pl.when(pair == 2)
        def _():
            gather.forward()

        bd = _head_blockdiag(1.0 / HEAD_DIM)
        gq = gq_ref[...] * (HEAD_DIM ** -0.5)
        gk = gk_ref[...]

        def prep(c, carry):
            rows = pl.ds(pl.multiple_of(c * CHUNK, CHUNK), CHUNK)
            zq = zq_ref[rows, :].astype(F32)
            zk = zk_ref[rows, :].astype(F32)
            tq[...] = (zq * lax.rsqrt(_head_sum(zq * zq, bd) + EPS)) * gq
            tk[...] = (zk * lax.rsqrt(_head_sum(zk * zk, bd) + EPS)) * gk
            tv[...] = zv_ref[rows, :].astype(F32)
            _to_residue_major(qf, tq, c, seg)
            _to_residue_major(kf, tk, c, seg)
            _to_residue_major(vf, tv, c, seg)
            return carry

        lax.fori_loop(0, s // CHUNK, prep, 0)

        save_copies = [pltpu.make_async_copy(src, saved.at[pair, i], save_sems.at[i]) for i, src in enumerate((qf, kf, vf))]
        for cp in save_copies:
            cp.start()

        head0 = lax.broadcasted_iota(jnp.int32, (BLOCK, 128), 1) < HEAD_DIM

        def run_branch(branch, d):
            n_runs, w, steps, group, chains = _branch_geometry(d, seg)
            slots = [(ci, b) for ci in range(chains) for b in range(group)]

            def step(u, grp):
                starts = [_run_starts(u + ci, grp, d, seg, n_runs, w, group) for ci in range(chains)]
                kt = [[_load_tile(kf, st, w, b).astype(BF16) for b in range(-1, group)] for st in starts]
                vt = [[_load_tile(vf, st, w, b).astype(BF16) for b in range(-1, group)] for st in starts]
                scores = []
                for ci, b in slots:
                    q2 = _stack_heads(_load_tile(qf, starts[ci], w, b), head0)
                    kc = jnp.concatenate([kt[ci][b], kt[ci][b + 1]], axis=0)
                    mask = mask_ref[branch, 0]
                    if b == 0:
                        mask = jnp.where(grp == 0, mask_ref[branch, 1], mask)
                    scores.append(_dot(q2, kc, NT) + mask)
                sc = jnp.concatenate(scores, axis=0)
                m = jnp.max(sc, axis=-1, keepdims=True)
                e = jnp.exp((sc - m).astype(BF16))
                ones = jnp.ones((BLOCK, 128), BF16)
                fresh = []
                for n, (ci, b) in enumerate(slots):
                    rows = slice(2 * BLOCK * n, 2 * BLOCK * (n + 1))
                    v_aug = jnp.concatenate([jnp.concatenate([vt[ci][b], ones], axis=1),
                                             jnp.concatenate([vt[ci][b + 1], ones], axis=1)], axis=0)
                    o2 = _dot(e[rows], v_aug, NN)
                    den = o2[:, 128:]
                    o2 = o2[:, :128] * (1.0 / den)
                    l2 = m[rows] + jnp.log(den)
                    fresh.append((jnp.where(head0, o2[:BLOCK], o2[BLOCK:]), jnp.where(head0, l2[:BLOCK], l2[BLOCK:])))
                for n, (ci, b) in enumerate(slots):
                    o_new, l_new = fresh[n]
                    if branch > 0:
                        l_old = _load_tile(l_st, starts[ci], w, b)
                        diff = l_new - l_old
                        t = jnp.exp(-jnp.abs(diff))
                        big = 1.0 / (1.0 + t)
                        small = t * big
                        newer = diff >= 0.0
                        o_old = _load_tile(o_st, starts[ci], w, b)
                        o_new = o_old * jnp.where(newer, small, big) + o_new * jnp.where(newer, big, small)
                        l_new = jnp.maximum(l_old, l_new) + jnp.log(1.0 + t)
                    _store_tile(o_st, starts[ci], w, b, o_new)
                    _store_tile(l_st, starts[ci], w, b, l_new)

            def unit(it, carry):
                def inner(grp, carry2):
                    step(it * chains, grp)
                    return carry2
                return lax.fori_loop(0, steps, inner, carry)

            lax.fori_loop(0, d // chains, unit, 0)

        for branch, d in enumerate(DILATIONS):
            run_branch(branch, d)

        def finish(c, carry):
            _from_residue_major(o_ref, c * CHUNK, o_st, c, seg)
            _from_residue_major(lse_ref, c * CHUNK, l_st, c, seg)
            return carry

        lax.fori_loop(0, s // CHUNK, finish, 0)
        for cp in save_copies:
            cp.wait()

        @pl.when(pair == N_PAIRS - 1)
        def _():
            gather.finish()
            for cp in own_copies():
                cp.wait()

    def col(c0):
        return pl.BlockSpec((s, 128), lambda p: (0, c0 + p), pipeline_mode=pl.Buffered(1))

    hbm = pl.BlockSpec(memory_space=pl.ANY)
    out = pl.pallas_call(
        body, name="attention_forward", grid=(N_PAIRS,),
        in_specs=[col(12), col(16), col(20), _const_spec((1, 128)), _const_spec((1, 128)), _const_spec(masks.shape)]
        + [hbm] * n_w,
        out_specs=[pl.BlockSpec((s, 128), lambda p: (0, p)), pl.BlockSpec((s, 128), lambda p: (0, p))] + [hbm] * (n_w + 1),
        out_shape=[jax.ShapeDtypeStruct((s, D_ATTN), F32), jax.ShapeDtypeStruct((s, D_ATTN), F32)]
        + [jax.ShapeDtypeStruct((N_DEV,) + w.shape, w.dtype) for w in shards]
        + [jax.ShapeDtypeStruct((N_PAIRS, 3, s, 128), F32)],
        scratch_shapes=[pltpu.VMEM((s, 128), F32)] * 5 + [pltpu.VMEM((CHUNK, 128), F32)] * 3
        + [pltpu.SemaphoreType.DMA((7 * n_w,)), pltpu.SemaphoreType.DMA((7 * n_w,)), pltpu.SemaphoreType.DMA((n_w,)),
           pltpu.SemaphoreType.DMA((3,))],
        compiler_params=_params(("arbitrary",), vmem=VMEM_LIMIT),
    )(z, z, z, gq2, gk2, masks, *shards)
    return out[0], out[1], out[2:2 + n_w], out[2 + n_w]


def _conv_forward(zt, zprev, w, first):
    u, gb, gc = zt[:, :D_CONV], zt[:, D_CONV:2 * D_CONV], zt[:, 2 * D_CONV:]
    cu = gc * u
    cu_prev = jnp.where(first, 0.0, zprev[:, 2 * D_CONV:] * zprev[:, :D_CONV])
    row = lax.broadcasted_iota(jnp.int32, cu.shape, 0)
    cu1 = jnp.where(row >= 1, pltpu.roll(cu, 1, 0), cu_prev[7:8, :])
    cu2 = jnp.where(row >= 2, pltpu.roll(cu, 2, 0), jnp.where(row == 1, cu_prev[7:8, :], cu_prev[6:7, :]))
    cv = w[0:1, :] * cu2 + w[1:2, :] * cu1 + w[2:3, :] * cu
    return u, gb, gc, cu, cu1, cu2, cv


def _mix_out(z, y_attn, x, conv_w, g_conv_out, g_attn_out, g_ffn, w_out, tm):
    s = x.shape[0]
    hb = tm // 8

    def body(z_ref, zp_ref, ya_ref, x_ref, cw_ref, gc_ref, ga_ref, gf_ref, w_ref, mix_ref, x2_ref, h2_ref):
        first = pl.program_id(0) == 0
        zt = z_ref[...].astype(F32)
        _, gb, _, _, _, _, cv = _conv_forward(zt, zp_ref[...].astype(F32), cw_ref[...], first)
        nc, _ = _rms(gb * cv)
        na, _ = _rms(ya_ref[...])
        mix = jnp.concatenate([nc * gc_ref[...], na * ga_ref[...]], axis=-1).astype(BF16)
        mix_ref[...] = mix
        x2 = x_ref[...] + _dot(mix, w_ref[...], NN)
        x2_ref[...] = x2
        xh, _ = _rms(x2)
        h2_ref[...] = (xh * gf_ref[...]).astype(BF16)

    tile = lambda w: pl.BlockSpec((tm, w), lambda i: (i, 0))
    return pl.pallas_call(
        body, name="mix_out", grid=(s // tm,),
        in_specs=[tile(3 * D_CONV), pl.BlockSpec((8, 3 * D_CONV), lambda i: (jnp.maximum(i * hb - 1, 0), 0)),
                  tile(D_ATTN), tile(D_MODEL), _const_spec((8, D_CONV)), _const_spec((1, D_CONV)), _const_spec((1, D_ATTN)),
                  _const_spec((1, D_MODEL)), _const_spec((D_MODEL, D_MODEL))],
        out_specs=[tile(D_MODEL)] * 3,
        out_shape=[jax.ShapeDtypeStruct((s, D_MODEL), BF16), jax.ShapeDtypeStruct((s, D_MODEL), F32),
                   jax.ShapeDtypeStruct((s, D_MODEL), BF16)],
        compiler_params=_params(("arbitrary",), vmem=VMEM_LIMIT),
    )(z, z, y_attn, x, conv_w, g_conv_out, g_attn_out, g_ffn, w_out)


FF_CHUNK = 256


def _ffn_forward(h2, x2, target, w_gate_t, w_up_t, w_down, tm):
    s = h2.shape[0]

    def body(h_ref, x2_ref, t_ref, wg_ref, wu_ref, wd_ref, a_ref, b_ref, dy_ref, dyb_ref, loss_ref, f_buf):
        @pl.when(pl.program_id(0) == 0)
        def _():
            loss_ref[...] = jnp.zeros_like(loss_ref)

        h = h_ref[...]
        for c0 in range(0, D_FF, FF_CHUNK):
            rows = slice(c0, c0 + FF_CHUNK)
            a = _dot(h, wg_ref[rows, :], NT)
            b = _dot(h, wu_ref[rows, :], NT)
            a_ref[:, rows] = a.astype(BF16)
            b_ref[:, rows] = b.astype(BF16)
            f_buf[:, rows] = ((a * jax.nn.sigmoid(a)) * b).astype(BF16)
        y = x2_ref[...] + _dot(f_buf[...], wd_ref[...], NN)
        err = y - t_ref[...]
        dy = err * (1.0 / D_MODEL)
        dy_ref[...] = dy
        dyb_ref[...] = dy.astype(BF16)
        loss_ref[...] += jnp.sum(err * err)

    tile = lambda w: pl.BlockSpec((tm, w), lambda i: (i, 0))
    return pl.pallas_call(
        body, name="ffn_forward", grid=(s // tm,),
        in_specs=[tile(D_MODEL), tile(D_MODEL), tile(D_MODEL)] + [_const_spec((D_FF, D_MODEL))] * 3,
        out_specs=[tile(D_FF), tile(D_FF), tile(D_MODEL), tile(D_MODEL), pl.BlockSpec((8, 128), lambda i: (0, 0))],
        out_shape=[jax.ShapeDtypeStruct((s, D_FF), BF16), jax.ShapeDtypeStruct((s, D_FF), BF16),
                   jax.ShapeDtypeStruct((s, D_MODEL), F32), jax.ShapeDtypeStruct((s, D_MODEL), BF16),
                   jax.ShapeDtypeStruct((8, 128), F32)],
        scratch_shapes=[pltpu.VMEM((tm, D_FF), BF16)],
        compiler_params=_params(("arbitrary",), vmem=VMEM_LIMIT),
    )(h2, x2, target, w_gate_t, w_up_t, w_down)


def _ffn_backward(dy, a, b, x2, g_ffn, w_gate_t, w_up_t, w_down, tm):
    s = dy.shape[0]
    half = D_FF // 2
    pieces = [(p0, min(FF_CHUNK, half - p0)) for p0 in range(0, half, FF_CHUNK)]

    def body(dy_ref, a_ref, b_ref, x2_ref, g_ref, wg_ref, wu_ref, wd_ref, da_ref, db_ref, dx2_ref, dx2b_ref, dg_ref, acc):
        i, c = pl.program_id(0), pl.program_id(1)

        @pl.when((i == 0) & (c == 0))
        def _():
            dg_ref[...] = jnp.zeros_like(dg_ref)

        dyb = dy_ref[...].astype(BF16)
        for p0, width in pieces:
            rows = slice(p0, p0 + width)
            df = _dot(dyb, wd_ref[rows, :], NT)
            av = a_ref[:, rows].astype(F32)
            bv = b_ref[:, rows].astype(F32)
            sig = jax.nn.sigmoid(av)
            silu = av * sig
            da_ref[:, rows] = (df * bv * (sig * (1.0 + av * (1.0 - sig)))).astype(BF16)
            db_ref[:, rows] = (df * silu).astype(BF16)
        dh = _dot(da_ref[...], wg_ref[...], NN) + _dot(db_ref[...], wu_ref[...], NN)

        @pl.when(c == 0)
        def _():
            acc[...] = dh

        @pl.when(c == 1)
        def _():
            dh_all = acc[...] + dh
            xh, r = _rms(x2_ref[...])
            dg_ref[...] += jnp.sum(dh_all * xh, axis=0, keepdims=True)
            dx2 = dy_ref[...] + _rms_bwd(dh_all, xh, r, g_ref[...])
            dx2_ref[...] = dx2
            dx2b_ref[...] = dx2.astype(BF16)

    tile = lambda w: pl.BlockSpec((tm, w), lambda i, c: (i, 0))
    part = pl.BlockSpec((tm, half), lambda i, c: (i, c))
    weight = pl.BlockSpec((half, D_MODEL), lambda i, c: (c, 0))
    return pl.pallas_call(
        body, name="ffn_backward", grid=(s // tm, 2),
        in_specs=[tile(D_MODEL), part, part, tile(D_MODEL), _const_spec((1, D_MODEL)), weight, weight, weight],
        out_specs=[part, part, tile(D_MODEL), tile(D_MODEL), pl.BlockSpec((1, D_MODEL), lambda i, c: (0, 0))],
        out_shape=[jax.ShapeDtypeStruct((s, D_FF), BF16)] * 2
        + [jax.ShapeDtypeStruct((s, D_MODEL), F32), jax.ShapeDtypeStruct((s, D_MODEL), BF16),
           jax.ShapeDtypeStruct((1, D_MODEL), F32)],
        scratch_shapes=[pltpu.VMEM((tm, D_MODEL), F32)],
        compiler_params=_params(("arbitrary", "arbitrary"), vmem=VMEM_LIMIT),
    )(dy, a, b, x2, g_ffn, w_gate_t, w_up_t, w_down)


def _out_backward(dx2, z, y_attn, conv_w, g_conv_out, g_attn_out, w_out, tm):
    s = dx2.shape[0]
    hb = tm // 8

    def body(dx2_ref, z_ref, zp_ref, ya_ref, cw_ref, gc_ref, ga_ref, w_ref, dya_ref, dgb_ref, dcv_ref, dgc_ref, dga_ref):
        first = pl.program_id(0) == 0

        @pl.when(first)
        def _():
            dgc_ref[...] = jnp.zeros_like(dgc_ref)
            dga_ref[...] = jnp.zeros_like(dga_ref)

        dmix = _dot(dx2_ref[...], w_ref[...], NT)
        _, gb, _, _, _, _, cv = _conv_forward(z_ref[...].astype(F32), zp_ref[...].astype(F32), cw_ref[...], first)
        ych, rc = _rms(gb * cv)
        dnc = dmix[:, :D_CONV]
        dgc_ref[...] += jnp.sum(dnc * ych, axis=0, keepdims=True)
        dyc = _rms_bwd(dnc, ych, rc, gc_ref[...])
        dgb_ref[...] = (dyc * cv).astype(BF16)
        dcv_ref[...] = dyc * gb
        yah, ra = _rms(ya_ref[...])
        dna = dmix[:, D_CONV:]
        dga_ref[...] += jnp.sum(dna * yah, axis=0, keepdims=True)
        dya_ref[...] = _rms_bwd(dna, yah, ra, ga_ref[...])

    tile = lambda w: pl.BlockSpec((tm, w), lambda i: (i, 0))
    vec = pl.BlockSpec((1, D_CONV), lambda i: (0, 0))
    return pl.pallas_call(
        body, name="out_backward", grid=(s // tm,),
        in_specs=[tile(D_MODEL), tile(3 * D_CONV), pl.BlockSpec((8, 3 * D_CONV), lambda i: (jnp.maximum(i * hb - 1, 0), 0)),
                  tile(D_ATTN), _const_spec((8, D_CONV)), _const_spec((1, D_CONV)), _const_spec((1, D_ATTN)),
                  _const_spec((D_MODEL, D_MODEL))],
        out_specs=[tile(D_ATTN), tile(D_CONV), tile(D_CONV), vec, vec],
        out_shape=[jax.ShapeDtypeStruct((s, D_ATTN), F32), jax.ShapeDtypeStruct((s, D_CONV), BF16),
                   jax.ShapeDtypeStruct((s, D_CONV), F32), jax.ShapeDtypeStruct((1, D_CONV), F32),
                   jax.ShapeDtypeStruct((1, D_ATTN), F32)],
        compiler_params=_params(("arbitrary",), vmem=VMEM_LIMIT),
    )(dx2, z, z, y_attn, conv_w, g_conv_out, g_attn_out, w_out)


def _attention_backward(z, o, lse, dya, gq2, gk2, masks, saved, grad_stacks):
    s = z.shape[0]
    seg = s // RESIDUES
    n_g = len(grad_stacks)

    def body(z_hbm, o_hbm, lse_hbm, do_hbm, gq_ref, gk_ref, mask_ref, saved_hbm, *rest):
        stack_refs = rest[:n_g]
        dzq_ref, dzk_ref, dzv_ref, dgq_ref, dgk_ref = rest[n_g:n_g + 5]
        landing = rest[n_g + 5:2 * n_g + 5]
        (qf, kf, vf, dof, stats, dq, dk, dv, t0, t1, t2, zbuf, fbuf, sems, send_sems, recv_sems,
         load_sems) = rest[2 * n_g + 5:]
        pair = pl.program_id(0)
        scatter = _DirectReduceScatter(stack_refs, landing, send_sems, recv_sems)

        @pl.when(pair == 0)
        def _():
            scatter.start()

        saved_copies = [pltpu.make_async_copy(saved_hbm.at[pair, i], dst, load_sems.at[i]) for i, dst in enumerate((qf, kf, vf))]
        for cp in saved_copies:
            cp.start()

        n_chunks = s // CHUNK
        bd_mean = _head_blockdiag(1.0 / HEAD_DIM)
        bd_sum = _head_blockdiag(1.0)
        gq = gq_ref[...] * (HEAD_DIM ** -0.5)
        gk = gk_ref[...]
        lane = lax.broadcasted_iota(jnp.int32, (CHUNK, 128), 1) % HEAD_DIM

        def chunk_copies(c, slot, n_z, n_f):
            rows = pl.ds(pl.multiple_of(c * CHUNK, CHUNK), CHUNK)
            cps = []
            for i, col in enumerate((12, 16, 20)[:n_z]):
                cols = pl.ds(pl.multiple_of((col + pair) * 128, 128), 128)
                cps.append(pltpu.make_async_copy(z_hbm.at[rows, cols], zbuf.at[slot, i], sems.at[slot, i]))
            for i, src in enumerate((do_hbm, o_hbm, lse_hbm)[:n_f]):
                cols = pl.ds(pl.multiple_of(pair * 128, 128), 128)
                cps.append(pltpu.make_async_copy(src.at[rows, cols], fbuf.at[slot, i], sems.at[slot, 3 + i]))
            return cps

        def prefetched(c, n_z, n_f):
            slot = c % 2

            @pl.when(c + 1 < n_chunks)
            def _():
                for cp in chunk_copies(c + 1, 1 - slot, n_z, n_f):
                    cp.start()

            for cp in chunk_copies(c, slot, n_z, n_f):
                cp.wait()
            return slot

        for cp in chunk_copies(0, 0, 0, 3):
            cp.start()

        def prep(c, carry):
            slot = prefetched(c, 0, 3)
            rows = pl.ds(pl.multiple_of(c * CHUNK, CHUNK), CHUNK)
            _to_residue_major(dof, fbuf.at[slot, 0], c, seg)
            delta = _head_sum(fbuf[slot, 0] * fbuf[slot, 1], bd_sum)
            t0[...] = jnp.where(lane == 0, fbuf[slot, 2], jnp.where(lane == 1, delta, 0.0))
            _to_residue_major(stats, t0, c, seg)
            zero = jnp.zeros((CHUNK, 128), F32)
            dq[rows, :] = zero
            dk[rows, :] = zero
            dv[rows, :] = zero
            return carry

        lax.fori_loop(0, n_chunks, prep, 0)
        for cp in saved_copies:
            cp.wait()

        head0 = lax.broadcasted_iota(jnp.int32, (BLOCK, 128), 1) < HEAD_DIM

        def run_branch(branch, d):
            n_runs, w, steps, group, chains = _branch_geometry(d, seg)
            slots = [(ci, b) for ci in range(chains) for b in range(group)]

            def step(u, grp):
                starts = [_run_starts(u + ci, grp, d, seg, n_runs, w, group) for ci in range(chains)]
                kt = [[_load_tile(kf, st, w, b).astype(BF16) for b in range(-1, group)] for st in starts]
                vt = [[_load_tile(vf, st, w, b).astype(BF16) for b in range(-1, group)] for st in starts]
                dk_t = [[jnp.zeros((BLOCK, 128), F32)] * (group + 1) for _ in range(chains)]
                dv_t = [[jnp.zeros((BLOCK, 128), F32)] * (group + 1) for _ in range(chains)]
                dq_t = []
                q2s, do2s, kcs, scores, dps, lses, deltas = [], [], [], [], [], [], []
                for ci, b in slots:
                    q2 = _stack_heads(_load_tile(qf, starts[ci], w, b), head0)
                    do2 = _stack_heads(_load_tile(dof, starts[ci], w, b), head0)
                    st = _load_tile(stats, starts[ci], w, b)
                    lses += [st[:, 0:1], st[:, HEAD_DIM:HEAD_DIM + 1]]
                    deltas += [st[:, 1:2], st[:, HEAD_DIM + 1:HEAD_DIM + 2]]
                    kc = jnp.concatenate([kt[ci][b], kt[ci][b + 1]], axis=0)
                    vc = jnp.concatenate([vt[ci][b], vt[ci][b + 1]], axis=0)
                    mask = mask_ref[branch, 0]
                    if b == 0:
                        mask = jnp.where(grp == 0, mask_ref[branch, 1], mask)
                    scores.append(_dot(q2, kc, NT) + mask)
                    dps.append(_dot(do2, vc, NT))
                    q2s.append(q2)
                    do2s.append(do2)
                    kcs.append(kc)
                p_all = jnp.exp(jnp.concatenate(scores, axis=0) - jnp.concatenate(lses, axis=0))
                ds_all = (p_all * (jnp.concatenate(dps, axis=0) - jnp.concatenate(deltas, axis=0))).astype(BF16)
                p_all = p_all.astype(BF16)
                for n, (ci, b) in enumerate(slots):
                    rows = slice(2 * BLOCK * n, 2 * BLOCK * (n + 1))
                    ds, q2, do2, kc = ds_all[rows], q2s[n], do2s[n], kcs[n]
                    dq2 = _dot(ds, kc, NN)
                    dq_t.append(jnp.where(head0, dq2[:BLOCK], dq2[BLOCK:]))
                    dkc = _dot(ds, q2, TN)
                    dvc = _dot(p_all[rows], do2, TN)
                    dk_t[ci][b] = dk_t[ci][b] + dkc[:BLOCK]
                    dk_t[ci][b + 1] = dk_t[ci][b + 1] + dkc[BLOCK:]
                    dv_t[ci][b] = dv_t[ci][b] + dvc[:BLOCK]
                    dv_t[ci][b + 1] = dv_t[ci][b + 1] + dvc[BLOCK:]
                for n, (ci, b) in enumerate(slots):
                    _add_tile(dq, starts[ci], w, b, dq_t[n])
                for ci in range(chains):
                    for b in range(-1, group):
                        _add_tile(dk, starts[ci], w, b, dk_t[ci][b + 1])
                        _add_tile(dv, starts[ci], w, b, dv_t[ci][b + 1])

            def unit(it, carry):
                def inner(grp, carry2):
                    step(it * chains, grp)
                    return carry2
                return lax.fori_loop(0, steps, inner, carry)

            lax.fori_loop(0, d // chains, unit, 0)

        for branch, d in enumerate(DILATIONS):
            run_branch(branch, d)

        for cp in chunk_copies(0, 0, 2, 0):
            cp.start()

        def finish(c, carry):
            acc_q, acc_k = carry
            rows = pl.ds(pl.multiple_of(c * CHUNK, CHUNK), CHUNK)
            _from_residue_major(t0, 0, dq, c, seg)
            _from_residue_major(t1, 0, dk, c, seg)
            _from_residue_major(t2, 0, dv, c, seg)
            slot = prefetched(c, 2, 0)
            zq = zbuf[slot, 0].astype(F32)
            rq = lax.rsqrt(_head_sum(zq * zq, bd_mean) + EPS)
            qh = zq * rq
            dqn = t0[...]
            acc_q = acc_q + jnp.sum(dqn * qh, axis=0, keepdims=True)
            t = dqn * gq
            dzq_ref[rows, :] = (rq * (t - qh * _head_sum(t * qh, bd_mean))).astype(BF16)
            zk = zbuf[slot, 1].astype(F32)
            rk = lax.rsqrt(_head_sum(zk * zk, bd_mean) + EPS)
            kh = zk * rk
            dkn = t1[...]
            acc_k = acc_k + jnp.sum(dkn * kh, axis=0, keepdims=True)
            t = dkn * gk
            dzk_ref[rows, :] = (rk * (t - kh * _head_sum(t * kh, bd_mean))).astype(BF16)
            dzv_ref[rows, :] = t2[...].astype(BF16)
            return acc_q, acc_k

        zero = jnp.zeros((1, 128), F32)
        acc_q, acc_k = lax.fori_loop(0, s // CHUNK, finish, (zero, zero))
        dgq_ref[...] = acc_q * (HEAD_DIM ** -0.5)
        dgk_ref[...] = acc_k

        @pl.when(pair == N_PAIRS - 1)
        def _():
            scatter.finish()

    hbm = pl.BlockSpec(memory_space=pl.ANY)
    gain = pl.BlockSpec((None, 1, 128), lambda p: (p, 0, 0))
    dz_spec = pl.BlockSpec((s, 128), lambda p: (0, p))
    out = pl.pallas_call(
        body, name="attention_backward", grid=(N_PAIRS,),
        in_specs=[hbm, hbm, hbm, hbm, _const_spec((1, 128)), _const_spec((1, 128)), _const_spec(masks.shape), hbm]
        + [hbm] * n_g,
        out_specs=[dz_spec, dz_spec, dz_spec, gain, gain] + [hbm] * n_g,
        out_shape=[jax.ShapeDtypeStruct((s, D_ATTN), BF16)] * 3 + [jax.ShapeDtypeStruct((N_PAIRS, 1, 128), F32)] * 2
        + [jax.ShapeDtypeStruct((N_DEV - 1,) + g.shape[1:], g.dtype) for g in grad_stacks],
        scratch_shapes=[pltpu.VMEM((s, 128), F32)] * 8 + [pltpu.VMEM((CHUNK, 128), F32)] * 3
        + [pltpu.VMEM((2, 3, CHUNK, 128), BF16), pltpu.VMEM((2, 3, CHUNK, 128), F32), pltpu.SemaphoreType.DMA((2, 6)),
           pltpu.SemaphoreType.DMA((7 * n_g,)), pltpu.SemaphoreType.DMA((7 * n_g,)), pltpu.SemaphoreType.DMA((3,))],
        compiler_params=_params(("arbitrary",), vmem=VMEM_LIMIT),
    )(z, o, lse, dya, gq2, gk2, masks, saved, *grad_stacks)
    return out[:5], out[5:]


def _in_backward(dcv, dgb, dzq, dzk, dzv, z, x, dx2, g_mix, conv_w, w_in_t, tm, grad_stacks):
    s = x.shape[0]
    hb = tm // 8
    last_halo = s // 8 - 1

    n_g = len(grad_stacks)

    def body(dcv_ref, dcn_ref, dgb_ref, dzq_ref, dzk_ref, dzv_ref, z_ref, zp_ref, x_ref, dx2_ref, g_ref, cw_ref, w_ref,
             *rest):
        stack_refs, (dz_ref, dx_ref, dg_ref, dcw_ref) = rest[:n_g], rest[n_g:n_g + 4]
        landing, (send_sems, recv_sems) = rest[n_g + 4:2 * n_g + 4], rest[2 * n_g + 4:]
        scatter = _DirectReduceScatter(stack_refs, landing, send_sems, recv_sems)
        i = pl.program_id(0)

        @pl.when(i == 0)
        def _():
            scatter.start()

        first = i == 0
        last = i == pl.num_programs(0) - 1

        @pl.when(first)
        def _():
            dg_ref[...] = jnp.zeros_like(dg_ref)
            dcw_ref[...] = jnp.zeros_like(dcw_ref)

        w = cw_ref[...]
        u, _, gc, cu, cu1, cu2, _ = _conv_forward(z_ref[...].astype(F32), zp_ref[...].astype(F32), w, first)
        dcv_t = dcv_ref[...]
        nxt = jnp.where(last, 0.0, dcn_ref[...])
        row = lax.broadcasted_iota(jnp.int32, dcv_t.shape, 0)
        up1 = jnp.where(row < tm - 1, pltpu.roll(dcv_t, tm - 1, 0), nxt[0:1, :])
        up2 = jnp.where(row < tm - 2, pltpu.roll(dcv_t, tm - 2, 0), jnp.where(row == tm - 2, nxt[0:1, :], nxt[1:2, :]))
        dcu = w[2:3, :] * dcv_t + w[1:2, :] * up1 + w[0:1, :] * up2
        dcw = jnp.concatenate([jnp.sum(dcv_t * cu2, axis=0, keepdims=True), jnp.sum(dcv_t * cu1, axis=0, keepdims=True),
                               jnp.sum(dcv_t * cu, axis=0, keepdims=True), jnp.zeros((5, D_CONV), F32)], axis=0)
        dcw_ref[...] += dcw
        dz_ref[:, :D_CONV] = (dcu * gc).astype(BF16)
        dz_ref[:, D_CONV:2 * D_CONV] = dgb_ref[...]
        dz_ref[:, 2 * D_CONV:3 * D_CONV] = (dcu * u).astype(BF16)
        dz_ref[:, 3 * D_CONV:3 * D_CONV + D_ATTN] = dzq_ref[...]
        dz_ref[:, 3 * D_CONV + D_ATTN:3 * D_CONV + 2 * D_ATTN] = dzk_ref[...]
        dz_ref[:, 3 * D_CONV + 2 * D_ATTN:] = dzv_ref[...]
        dh = _dot(dz_ref[...], w_ref[...], NN)
        xh, r = _rms(x_ref[...])
        dg_ref[...] += jnp.sum(dh * xh, axis=0, keepdims=True)
        dx_ref[...] = dx2_ref[...] + _rms_bwd(dh, xh, r, g_ref[...])

        @pl.when(last)
        def _():
            scatter.finish()

    tile = lambda w: pl.BlockSpec((tm, w), lambda i: (i, 0))
    hbm = pl.BlockSpec(memory_space=pl.ANY)
    out = pl.pallas_call(
        body, name="in_backward", grid=(s // tm,),
        in_specs=[tile(D_CONV), pl.BlockSpec((8, D_CONV), lambda i: (jnp.minimum((i + 1) * hb, last_halo), 0)),
                  tile(D_CONV), tile(D_ATTN), tile(D_ATTN), tile(D_ATTN),
                  tile(3 * D_CONV), pl.BlockSpec((8, 3 * D_CONV), lambda i: (jnp.maximum(i * hb - 1, 0), 0)),
                  tile(D_MODEL), tile(D_MODEL), _const_spec((1, D_MODEL)), _const_spec((8, D_CONV)),
                  _const_spec((D_IN, D_MODEL))] + [hbm] * n_g,
        out_specs=[tile(D_IN), tile(D_MODEL), pl.BlockSpec((1, D_MODEL), lambda i: (0, 0)),
                   pl.BlockSpec((8, D_CONV), lambda i: (0, 0))] + [hbm] * n_g,
        out_shape=[jax.ShapeDtypeStruct((s, D_IN), BF16), jax.ShapeDtypeStruct((s, D_MODEL), F32),
                   jax.ShapeDtypeStruct((1, D_MODEL), F32), jax.ShapeDtypeStruct((8, D_CONV), F32)]
        + [jax.ShapeDtypeStruct((N_DEV - 1,) + g.shape[1:], g.dtype) for g in grad_stacks],
        scratch_shapes=[pltpu.SemaphoreType.DMA((7 * n_g,)), pltpu.SemaphoreType.DMA((7 * n_g,))],
        compiler_params=_params(("arbitrary",), vmem=VMEM_LIMIT),
    )(dcv, dcv, dgb, dzq, dzk, dzv, z, z, x, dx2, g_mix, conv_w, w_in_t, *grad_stacks)
    return out[:4], out[4:]


def _weight_grad(name, a, b, tn, tk, gate=None):
    s, n = a.shape
    steps = s // tk

    def body(*refs):
        gate_ref = refs[0] if gate is not None else None
        a_ref, b_ref, out_ref, acc = refs[1:] if gate is not None else refs
        k = pl.program_id(1)

        @pl.when(k == 0)
        def _():
            acc[...] = jnp.zeros_like(acc)

        lhs = a_ref[...]
        if gate is not None:
            gv = gate_ref[...].astype(F32)
            lhs = ((gv * jax.nn.sigmoid(gv)) * lhs.astype(F32)).astype(BF16)
        acc[...] += _dot(lhs, b_ref[...].astype(BF16), TN)

        @pl.when(k == steps - 1)
        def _():
            out_ref[...] = acc[...].astype(BF16)

    lhs_spec = pl.BlockSpec((tk, tn), lambda j, k: (k, j))
    return pl.pallas_call(
        body, name=name, grid=(n // tn, steps),
        in_specs=([lhs_spec] if gate is not None else []) + [lhs_spec, pl.BlockSpec((tk, D_MODEL), lambda j, k: (k, 0))],
        out_specs=pl.BlockSpec((tn, D_MODEL), lambda j, k: (j, 0)),
        out_shape=jax.ShapeDtypeStruct((n, D_MODEL), BF16),
        scratch_shapes=[pltpu.VMEM((tn, D_MODEL), F32)],
        compiler_params=_params(("parallel", "arbitrary"), vmem=VMEM_LIMIT),
    )(*([gate] if gate is not None else []), a, b)


def _adamw_math(w, g, m, v):
    nm = ADAM_B1 * m + (1.0 - ADAM_B1) * g
    nv = ADAM_B2 * v + (1.0 - ADAM_B2) * (g * g)
    m_hat = nm / (1.0 - ADAM_B1 ** ADAM_STEP)
    v_hat = nv / (1.0 - ADAM_B2 ** ADAM_STEP)
    return -ADAM_LR * (m_hat / (jnp.sqrt(v_hat) + ADAM_EPS) + ADAM_WD * w), nm, nv


def _adamw_received(name, device, w, stack, received, m, v):
    def body(dev_ref, w_ref, own_ref, recv_ref, m_ref, v_ref, g_ref, d_ref, nm_ref, nv_ref):
        del dev_ref
        g = own_ref[...].astype(F32)
        for k in range(N_DEV - 1):
            g = g + recv_ref[k].astype(F32)
        g_ref[...] = g
        d_ref[...], nm_ref[...], nv_ref[...] = _adamw_math(w_ref[...], g, m_ref[...], v_ref[...])

    rows = stack.shape[1]
    tr = rows // 4 if rows % 64 == 0 else rows // 2
    tile = pl.BlockSpec((tr, D_MODEL), lambda i, dref: (i, 0))
    return pl.pallas_call(
        body, name=name,
        grid_spec=pltpu.PrefetchScalarGridSpec(
            num_scalar_prefetch=1, grid=(rows // tr,),
            in_specs=[tile, pl.BlockSpec((None, tr, D_MODEL), lambda i, dref: (dref[0], i, 0)),
                      pl.BlockSpec((N_DEV - 1, tr, D_MODEL), lambda i, dref: (0, i, 0)), tile, tile],
            out_specs=[tile] * 4),
        out_shape=[jax.ShapeDtypeStruct((rows, D_MODEL), F32)] * 4,
        compiler_params=_params(("arbitrary",), vmem=VMEM_LIMIT),
    )(device, w, stack, received, m, v)


def _adamw(name, w, g, m, v):
    def body(w_ref, g_ref, m_ref, v_ref, d_ref, nm_ref, nv_ref):
        d_ref[...], nm_ref[...], nv_ref[...] = _adamw_math(w_ref[...], g_ref[...], m_ref[...], v_ref[...])

    rows = w.shape[0]
    tr = 256 if rows % 256 == 0 else rows
    spec = pl.BlockSpec((tr, w.shape[1]), lambda i: (i, 0))
    return pl.pallas_call(
        body, name=name, grid=(rows // tr,),
        in_specs=[spec] * 4, out_specs=[spec] * 3,
        out_shape=[jax.ShapeDtypeStruct(w.shape, F32)] * 3,
        compiler_params=_params(("parallel",)),
    )(w, g, m, v)


def kernel(x, g_mix, w_in, conv_w, g_q, g_k, g_conv_out, g_attn_out, w_out, g_ffn, w_gate, w_up, w_down, loss_target, m_g_mix, m_w_in, m_conv_w, m_g_q, m_g_k, m_g_conv_out, m_g_attn_out, m_w_out, m_g_ffn, m_w_gate, m_w_up, m_w_down, v_g_mix, v_w_in, v_conv_w, v_g_q, v_g_k, v_g_conv_out, v_g_attn_out, v_w_out, v_g_ffn, v_w_gate, v_w_up, v_w_down):
    s = x.shape[1]
    tm = min(512, s)
    xs, target = x[0], loss_target[0]
    px, py, pc = lax.axis_index("x"), lax.axis_index("y"), lax.axis_index("c")
    device = 4 * px + 2 * py + pc

    conv_block = jnp.zeros((8, 128), F32).at[:3, :HEAD_DIM].set(conv_w[0])
    first, shards = [w_in[0].T, conv_block], [w_out[0], w_gate[0].T, w_up[0].T, w_down[0]]
    w_in_g, conv_g, *shards = _gather_first_weights(first, [BF16, F32], shards)
    w_in_t = w_in_g.reshape(D_IN, D_MODEL)
    conv_full = jnp.transpose(conv_g[:, :3, :HEAD_DIM], (1, 0, 2)).reshape(3, D_CONV)
    conv_full = jnp.concatenate([conv_full, jnp.zeros((5, D_CONV), F32)], axis=0)
    gq2 = jnp.concatenate([g_q, g_q], axis=-1)
    gk2 = jnp.concatenate([g_k, g_k], axis=-1)

    h1, z = _in_projection(xs, g_mix, w_in_t, tm)
    masks = _permuted_masks()
    y_attn, lse, gathered, qkv_saved = _attention_forward(z, gq2, gk2, masks, shards)
    w_out_f = gathered[0].reshape(D_MODEL, D_MODEL)
    w_gate_t, w_up_t, w_down_f = [g.reshape(D_FF, D_MODEL) for g in gathered[1:]]
    mix, x2, h2 = _mix_out(z, y_attn, xs, conv_full, g_conv_out, g_attn_out, g_ffn, w_out_f, tm)
    a, b, dy, dy_bf, sq_err = _ffn_forward(h2, x2, target, w_gate_t, w_up_t, w_down_f, tm)

    tk = min(2048, s)
    da, db, dx2, dx2_bf, dg_ffn = _ffn_backward(dy, a, b, x2, g_ffn, w_gate_t, w_up_t, w_down_f, tm)
    dya, dgb, dcv, dg_conv_out, dg_attn_out = _out_backward(dx2_bf, z, y_attn, conv_full, g_conv_out, g_attn_out, w_out_f, tm)
    early = [
        _weight_grad("grad_w_out", mix, dx2_bf, D_MODEL, tk),
        _weight_grad("grad_w_gate", da, h2, D_FF // 2, tk),
        _weight_grad("grad_w_up", db, h2, D_FF // 2, tk),
        _weight_grad("grad_w_down", b, dy_bf, D_FF // 2, tk, gate=a),
    ]
    early = [g.reshape(N_DEV, g.shape[0] // N_DEV, D_MODEL) for g in early]
    (dzq, dzk, dzv, dgq_pairs, dgk_pairs), landed = _attention_backward(
        z, y_attn, lse, dya, gq2, gk2, masks, qkv_saved, early[:3])
    (dz, grad_x, dg_mix, dconv), landed_down = _in_backward(
        dcv, dgb, dzq, dzk, dzv, z, xs, dx2, g_mix, conv_full, w_in_t, tm, early[3:])
    dev = jnp.reshape(device, (1,)).astype(jnp.int32)
    scattered = dict(zip(("w_out", "w_gate", "w_up", "w_down"), zip(early, list(landed) + list(landed_down))))

    late = [_weight_grad("grad_w_in", dz, h1, D_IN // 2, tk).reshape(4, 2, D_IN // N_DEV, D_MODEL)]
    from_sibling = _reduce_scatter_cores(late)
    core = jnp.reshape(pc, (1,)).astype(jnp.int32)
    partials = _add_core_partials(core, late, from_sibling)
    from_chips = _reduce_scatter_chips(partials)
    chip = jnp.reshape(2 * px + py, (1,)).astype(jnp.int32)
    (gw_in_t,) = _add_chip_partials(chip, partials, from_chips)
    gw_in = gw_in_t.T

    dg_q = jnp.sum(dgq_pairs.reshape(2 * N_PAIRS, HEAD_DIM), axis=0)
    dg_k = jnp.sum(dgk_pairs.reshape(2 * N_PAIRS, HEAD_DIM), axis=0)
    zeros = lambda n: jnp.zeros((n,), F32)
    small = jnp.stack([
        dg_mix[0], dg_ffn[0],
        jnp.concatenate([dg_conv_out[0], dg_attn_out[0]]),
        jnp.concatenate([dg_q, dg_k, zeros(D_MODEL - 2 * HEAD_DIM)]),
        jnp.concatenate([dconv[0], dconv[1]]),
        jnp.concatenate([dconv[2], zeros(D_CONV)]),
        jnp.concatenate([sq_err[0, :1], zeros(D_MODEL - 1)]),
        zeros(D_MODEL),
    ])
    total = _all_reduce_small(small)
    loss = total[6, 0] * (0.5 / D_MODEL)
    gg_mix, gg_ffn = total[0:1], total[1:2]
    gg_conv_out, gg_attn_out = total[2:3, :D_CONV], total[2:3, D_CONV:]
    gg_q, gg_k = total[3:4, :HEAD_DIM], total[3:4, HEAD_DIM:2 * HEAD_DIM]
    conv_total = jnp.stack([total[4, :D_CONV], total[4, D_CONV:], total[5, :D_CONV]])
    g_conv = lax.dynamic_slice(conv_total, (0, device * HEAD_DIM), (3, HEAD_DIM))

    names = ["g_mix", "w_in", "conv_w", "g_q", "g_k", "g_conv_out", "g_attn_out", "w_out", "g_ffn", "w_gate", "w_up", "w_down"]
    weights = [g_mix, w_in[0], conv_w[0], g_q, g_k, g_conv_out, g_attn_out, w_out[0], g_ffn, w_gate[0], w_up[0], w_down[0]]
    grad_list = [gg_mix, None, g_conv, gg_q, gg_k, gg_conv_out, gg_attn_out, None, gg_ffn, None, None, None]
    m_list = [m_g_mix, m_w_in[0], m_conv_w[0], m_g_q, m_g_k, m_g_conv_out, m_g_attn_out, m_w_out[0], m_g_ffn, m_w_gate[0], m_w_up[0], m_w_down[0]]
    v_list = [v_g_mix, v_w_in[0], v_conv_w[0], v_g_q, v_g_k, v_g_conv_out, v_g_attn_out, v_w_out[0], v_g_ffn, v_w_gate[0], v_w_up[0], v_w_down[0]]
    stacked = {"w_in", "conv_w", "w_out", "w_gate", "w_up", "w_down"}
    transposed = {"w_gate", "w_up"}
    results = {}
    operands = {n: (w, g, m, v) for n, w, g, m, v in zip(names, weights, grad_list, m_list, v_list)}
    for name in scattered:
        w, _, m, v = operands[name]
        if name in transposed:
            w, m, v = w.T, m.T, v.T
        results[name] = _adamw_received("adamw_" + name, dev, w, *scattered[name], m, v)
        if name in transposed:
            results[name] = [t.T for t in results[name]]
    for name in names:
        if name not in results:
            w, g, m, v = operands[name]
            g = gw_in if name == "w_in" else g
            results[name] = [g, *_adamw("adamw_" + name, w, g, m, v)]
    outs = []
    for part in range(4):
        outs += [results[n][part][None] if n in stacked else results[n][part] for n in names]
    return (loss, grad_x[None], *outs)
```

```python
import jax
import jax.numpy as jnp
import numpy as np
from jax import lax
from jax.experimental import pallas as pl
from jax.experimental.pallas import tpu as pltpu

F32 = jnp.float32
BF16 = jnp.bfloat16

D_MODEL = 1024
D_CONV = 512
D_ATTN = 512
D_IN = 3072
D_FF = 2816
HEAD_DIM = 64
N_PAIRS = 4
BLOCK = 128
DILATIONS = (1, 4, 16)
N_DEV = 8
EPS = 1e-6
NEG = -1e30

ADAM_LR = 0.001
ADAM_B1 = 0.9
ADAM_B2 = 0.999
ADAM_EPS = 1e-08
ADAM_WD = 0.01
ADAM_STEP = 10

NT = (((1,), (1,)), ((), ()))
NN = (((1,), (0,)), ((), ()))
TN = (((0,), (0,)), ((), ()))
MESH = pl.DeviceIdType.MESH

VMEM_LIMIT = 56 * 1024 * 1024


def _params(semantics=None, vmem=None):
    kw = {}
    if semantics is not None:
        kw["dimension_semantics"] = semantics
    if vmem is not None:
        kw["vmem_limit_bytes"] = vmem
    return pltpu.CompilerParams(**kw)


def _dot(a, b, dims):
    return lax.dot_general(a, b, dims, preferred_element_type=F32)


def _const_spec(shape):
    n = len(shape)
    return pl.BlockSpec(shape, lambda *_: (0,) * n, pipeline_mode=pl.Buffered(1))


def _rms(x):
    r = lax.rsqrt(jnp.mean(x * x, axis=-1, keepdims=True) + EPS)
    return x * r, r


def _rms_bwd(dy, xh, r, g):
    t = dy * g
    return r * (t - xh * jnp.mean(t * xh, axis=-1, keepdims=True))


def _head_blockdiag(scale):
    i = lax.broadcasted_iota(jnp.int32, (128, 128), 0) // HEAD_DIM
    j = lax.broadcasted_iota(jnp.int32, (128, 128), 1) // HEAD_DIM
    return jnp.where(i == j, scale, 0.0).astype(BF16)


def _head_sum(x, bd):
    hi = x.astype(BF16)
    lo = (x - hi.astype(F32)).astype(BF16)
    return _dot(hi, bd, NN) + _dot(lo, bd, NN)


def _place():
    return lax.axis_index("x"), lax.axis_index("y"), lax.axis_index("c")


class _TwoLevelGather:
    def __init__(self, stacks, send_sems, recv_sems):
        self.stacks, self.send_sems, self.recv_sems = stacks, send_sems, recv_sems
        x, y, c = _place()
        self.c = c
        self.me, self.sibling = (x, y, c), (x, y, 1 - c)
        self.chips = [(x, 1 - y), (1 - x, y), (1 - x, 1 - y)]

    @staticmethod
    def index(p):
        return 4 * p[0] + 2 * p[1] + p[2]

    def copy(self, a, k, block, to, src=None):
        dst = self.stacks[a].at[self.index(block)]
        return pltpu.make_async_remote_copy(
            src_ref=dst if src is None else src, dst_ref=dst,
            send_sem=self.send_sems.at[7 * a + k], recv_sem=self.recv_sems.at[7 * a + k],
            device_id=to, device_id_type=MESH)

    def first(self, a, src=None):
        cps = [self.copy(a, 0, self.me, self.sibling, src)]
        return cps + [self.copy(a, 1 + j, self.me, (*chip, self.c), src) for j, chip in enumerate(self.chips)]

    def forwards(self, a):
        return [self.copy(a, 4 + j, (*chip, self.c), self.sibling) for j, chip in enumerate(self.chips)]

    def start(self, srcs=None):
        for a in range(len(self.stacks)):
            for cp in self.first(a, None if srcs is None else srcs[a]):
                cp.start()

    def forward(self):
        for a in range(len(self.stacks)):
            for j, (chip, fwd) in enumerate(zip(self.chips, self.forwards(a))):
                self.copy(a, 1 + j, (*chip, self.c), self.me).wait_recv()
                fwd.start()

    def finish(self):
        for a in range(len(self.stacks)):
            self.copy(a, 0, self.sibling, self.me).wait_recv()
            for j, chip in enumerate(self.chips):
                self.copy(a, 4 + j, (*chip, 1 - self.c), self.me).wait_recv()
        for a in range(len(self.stacks)):
            for cp in self.first(a) + self.forwards(a):
                cp.wait_send()


def _gather_first_weights(gathered_parts, gathered_dtypes, cast_parts):
    n, m = len(gathered_parts), len(cast_parts)

    def body(*refs):
        ins, casts_in = refs[:n], refs[n:n + m]
        outs, casts_out = refs[n + m:2 * n + m], refs[2 * n + m:2 * n + 2 * m]
        send_sems, recv_sems = refs[2 * n + 2 * m], refs[2 * n + 2 * m + 1]
        gather = _TwoLevelGather(outs, send_sems, recv_sems)
        for a in range(n):
            outs[a][gather.index(gather.me)] = ins[a][...].astype(outs[a].dtype)
        gather.start()
        for a in range(m):
            casts_out[a][...] = casts_in[a][...].astype(BF16)
        gather.forward()
        gather.finish()

    vm = pl.BlockSpec(memory_space=pltpu.VMEM)
    return pl.pallas_call(
        body, name="weight_all_gather",
        out_shape=[jax.ShapeDtypeStruct((N_DEV,) + p.shape, dt) for p, dt in zip(gathered_parts, gathered_dtypes)]
        + [jax.ShapeDtypeStruct(p.shape, BF16) for p in cast_parts],
        in_specs=[vm] * (n + m), out_specs=[vm] * (n + m),
        scratch_shapes=[pltpu.SemaphoreType.DMA((7 * n,)), pltpu.SemaphoreType.DMA((7 * n,))],
        compiler_params=_params(vmem=VMEM_LIMIT),
    )(*gathered_parts, *cast_parts)


class _DirectReduceScatter:
    def __init__(self, stacks, landing, send_sems, recv_sems):
        self.stacks, self.landing, self.send_sems, self.recv_sems = stacks, landing, send_sems, recv_sems
        self.place = _place()

    def copies(self):
        x, y, c = self.place
        cps = []
        for a in range(len(self.stacks)):
            for k in range(1, N_DEV):
                peer = (1 - x if k & 4 else x, 1 - y if k & 2 else y, 1 - c if k & 1 else c)
                cps.append(pltpu.make_async_remote_copy(
                    src_ref=self.stacks[a].at[4 * peer[0] + 2 * peer[1] + peer[2]], dst_ref=self.landing[a].at[k - 1],
                    send_sem=self.send_sems.at[7 * a + k - 1], recv_sem=self.recv_sems.at[7 * a + k - 1],
                    device_id=peer, device_id_type=MESH))
        return cps

    def start(self):
        for cp in self.copies():
            cp.start()

    def finish(self):
        for cp in self.copies():
            cp.wait_recv()
        for cp in self.copies():
            cp.wait_send()


def _all_reduce_small(v):
    def body(v_ref, out_ref, gathered, send_sems, recv_sems):
        x, y, c = _place()
        mine = 4 * x + 2 * y + c
        gathered[mine] = v_ref[...]

        def copy(k):
            peer = (1 - x if k & 4 else x, 1 - y if k & 2 else y, 1 - c if k & 1 else c)
            return pltpu.make_async_remote_copy(
                src_ref=gathered.at[mine], dst_ref=gathered.at[mine],
                send_sem=send_sems.at[k - 1], recv_sem=recv_sems.at[k - 1], device_id=peer, device_id_type=MESH)

        copies = [copy(k) for k in range(1, N_DEV)]
        for cp in copies:
            cp.start()
        for cp in copies:
            cp.wait_recv()
        for cp in copies:
            cp.wait_send()
        total = gathered[0]
        for d in range(1, N_DEV):
            total = total + gathered[d]
        out_ref[...] = total

    vm = pl.BlockSpec(memory_space=pltpu.VMEM)
    return pl.pallas_call(
        body, name="small_all_reduce",
        out_shape=jax.ShapeDtypeStruct(v.shape, F32),
        in_specs=[vm], out_specs=vm,
        scratch_shapes=[pltpu.VMEM((N_DEV,) + v.shape, F32),
                        pltpu.SemaphoreType.DMA((N_DEV - 1,)), pltpu.SemaphoreType.DMA((N_DEV - 1,))],
    )(v)


def _reduce_scatter_cores(grads):
    n = len(grads)

    def body(*refs):
        ins, outs = refs[:n], refs[n:2 * n]
        send_sems, recv_sems = refs[2 * n], refs[2 * n + 1]
        x, y, c = _place()
        copies = []
        for a in range(n):
            for k in range(4):
                copies.append(pltpu.make_async_remote_copy(
                    src_ref=ins[a].at[k, 1 - c], dst_ref=outs[a].at[k],
                    send_sem=send_sems.at[4 * a + k], recv_sem=recv_sems.at[4 * a + k],
                    device_id=(x, y, 1 - c), device_id_type=MESH))
        for cp in copies:
            cp.start()
        for cp in copies:
            cp.wait_recv()
        for cp in copies:
            cp.wait_send()

    hbm = pl.BlockSpec(memory_space=pl.ANY)
    return pl.pallas_call(
        body, name="grad_reduce_scatter_cores",
        out_shape=[jax.ShapeDtypeStruct((4,) + g.shape[2:], g.dtype) for g in grads],
        in_specs=[hbm] * n, out_specs=[hbm] * n,
        scratch_shapes=[pltpu.SemaphoreType.DMA((4 * n,)), pltpu.SemaphoreType.DMA((4 * n,))],
    )(*grads)


def _reduce_scatter_chips(partials):
    n = len(partials)

    def body(*refs):
        ins, outs = refs[:n], refs[n:2 * n]
        send_sems, recv_sems = refs[2 * n], refs[2 * n + 1]
        x, y, c = _place()
        copies = []
        for a in range(n):
            for j in (1, 2, 3):
                px = 1 - x if j & 2 else x
                py = 1 - y if j & 1 else y
                copies.append(pltpu.make_async_remote_copy(
                    src_ref=ins[a].at[2 * px + py], dst_ref=outs[a].at[j - 1],
                    send_sem=send_sems.at[3 * a + j - 1], recv_sem=recv_sems.at[3 * a + j - 1],
                    device_id=(px, py, c), device_id_type=MESH))
        for cp in copies:
            cp.start()
        for cp in copies:
            cp.wait_recv()
        for cp in copies:
            cp.wait_send()

    hbm = pl.BlockSpec(memory_space=pl.ANY)
    return pl.pallas_call(
        body, name="grad_reduce_scatter_chips",
        out_shape=[jax.ShapeDtypeStruct((3,) + p.shape[1:], p.dtype) for p in partials],
        in_specs=[hbm] * n, out_specs=[hbm] * n,
        scratch_shapes=[pltpu.SemaphoreType.DMA((3 * n,)), pltpu.SemaphoreType.DMA((3 * n,))],
    )(*partials)


def _add_core_partials(core, grads, received):
    n = len(grads)

    def body(core_ref, *refs):
        del core_ref
        gs, rs, outs = refs[:n], refs[n:2 * n], refs[2 * n:]
        for a in range(n):
            outs[a][...] = (gs[a][...].astype(F32) + rs[a][...].astype(F32)).astype(outs[a].dtype)

    in_specs = [pl.BlockSpec((None, None) + g.shape[2:], lambda k, cref: (k, cref[0], 0, 0)) for g in grads]
    in_specs += [pl.BlockSpec((None,) + r.shape[1:], lambda k, cref: (k, 0, 0)) for r in received]
    out_specs = [pl.BlockSpec((None,) + r.shape[1:], lambda k, cref: (k, 0, 0)) for r in received]
    return pl.pallas_call(
        body, name="grad_add_core_partials",
        grid_spec=pltpu.PrefetchScalarGridSpec(num_scalar_prefetch=1, grid=(4,), in_specs=in_specs, out_specs=out_specs),
        out_shape=[jax.ShapeDtypeStruct(r.shape, r.dtype) for r in received],
        compiler_params=_params(("arbitrary",)),
    )(core, *grads, *received)


def _add_chip_partials(chip, partials, received):
    n = len(partials)

    def body(chip_ref, *refs):
        del chip_ref
        ps, rs, outs = refs[:n], refs[n:2 * n], refs[2 * n:]
        for a in range(n):
            outs[a][...] = ((ps[a][...].astype(F32) + rs[a][0].astype(F32)) + rs[a][1].astype(F32)) + rs[a][2].astype(F32)

    in_specs = [pl.BlockSpec((None,) + p.shape[1:], lambda i, cref: (cref[0], 0, 0)) for p in partials]
    in_specs += [pl.BlockSpec(r.shape, lambda i, cref: (0, 0, 0)) for r in received]
    out_specs = [pl.BlockSpec(p.shape[1:], lambda i, cref: (0, 0)) for p in partials]
    return pl.pallas_call(
        body, name="grad_add_chip_partials",
        grid_spec=pltpu.PrefetchScalarGridSpec(num_scalar_prefetch=1, grid=(1,), in_specs=in_specs, out_specs=out_specs),
        out_shape=[jax.ShapeDtypeStruct(p.shape[1:], F32) for p in partials],
        compiler_params=_params(("arbitrary",), vmem=VMEM_LIMIT),
    )(chip, *partials, *received)


def _in_projection(x, g_mix, w_in_t, tm):
    s = x.shape[0]

    def body(x_ref, g_ref, w_ref, h_ref, z_ref):
        xh, _ = _rms(x_ref[...])
        h = (xh * g_ref[...]).astype(BF16)
        h_ref[...] = h
        for n0 in range(0, D_IN, 512):
            z_ref[:, n0:n0 + 512] = _dot(h, w_ref[n0:n0 + 512, :], NT).astype(BF16)

    return pl.pallas_call(
        body, name="in_projection", grid=(s // tm,),
        in_specs=[pl.BlockSpec((tm, D_MODEL), lambda i: (i, 0)), _const_spec((1, D_MODEL)), _const_spec((D_IN, D_MODEL))],
        out_specs=[pl.BlockSpec((tm, D_MODEL), lambda i: (i, 0)), pl.BlockSpec((tm, D_IN), lambda i: (i, 0))],
        out_shape=[jax.ShapeDtypeStruct((s, D_MODEL), BF16), jax.ShapeDtypeStruct((s, D_IN), BF16)],
        compiler_params=_params(("parallel",), vmem=VMEM_LIMIT),
    )(x, g_mix, w_in_t)


RESIDUES = 16
CHUNK = 512
PREFETCH = 4
GROUP = 8


def _branch_geometry(d, seg):
    n_runs = RESIDUES // d
    w = BLOCK // n_runs
    blocks = seg // w
    group = min(GROUP, blocks)
    chains = min(GROUP // group, d)
    return n_runs, w, blocks // group, group, chains


def _permuted_masks():
    out = []
    for d in DILATIONS:
        n_runs = RESIDUES // d
        w = BLOCK // n_runs
        p = np.arange(BLOCK)
        pos = (p % w) * n_runs + p // w
        dist = pos[:, None] - np.concatenate([pos - BLOCK, pos])[None, :]
        band = (dist >= 0) & (dist <= BLOCK)
        first = band & (np.arange(2 * BLOCK)[None, :] >= BLOCK)
        both = [np.where(m, 0.0, NEG).astype(np.float32) for m in (band, first)]
        out.append(np.stack([np.concatenate([m, m], axis=0) for m in both]))
    return jnp.asarray(np.stack(out))


def _run_starts(u, grp, d, seg, n_runs, w, group):
    return [(u + d * q) * seg + (w * group) * grp for q in range(n_runs)]


def _tile_rows(start, w, b):
    off = start + w * b
    if b < 0:
        off = jnp.maximum(off, 0)
    return pl.ds(pl.multiple_of(off, w), w)


def _load_tile(ref, starts, w, b):
    parts = [ref[_tile_rows(st, w, b), :] for st in starts]
    return parts[0] if len(parts) == 1 else jnp.concatenate(parts, axis=0)


def _store_tile(ref, starts, w, b, tile):
    for i, st in enumerate(starts):
        ref[_tile_rows(st, w, b), :] = tile[i * w:(i + 1) * w]


def _add_tile(ref, starts, w, b, tile):
    for i, st in enumerate(starts):
        ref[_tile_rows(st, w, b), :] += tile[i * w:(i + 1) * w]


def _to_residue_major(dst, src, c, seg):
    per = CHUNK // RESIDUES
    for r in range(RESIDUES):
        dst[pl.ds(pl.multiple_of(r * seg + c * per, per), per), :] = src[pl.ds(r, per, stride=RESIDUES), :]


def _from_residue_major(dst, dst_start, src, c, seg):
    per = CHUNK // RESIDUES
    for r in range(RESIDUES):
        dst[pl.ds(dst_start + r, per, stride=RESIDUES), :] = src[pl.ds(pl.multiple_of(r * seg + c * per, per), per), :]


def _stack_heads(t, head0):
    return jnp.concatenate([jnp.where(head0, t, 0.0), jnp.where(head0, 0.0, t)], axis=0).astype(BF16)


def _attention_forward(z, gq2, gk2, masks, shards):
    s = z.shape[0]
    seg = s // RESIDUES
    n_w = len(shards)

    def body(zq_ref, zk_ref, zv_ref, gq_ref, gk_ref, mask_ref, *rest):
        shard_refs, (o_ref, lse_ref), stacks = rest[:n_w], rest[n_w:n_w + 2], rest[n_w + 2:2 * n_w + 2]
        saved = rest[2 * n_w + 2]
        qf, kf, vf, o_st, l_st, tq, tk, tv, send_sems, recv_sems, local_sems, save_sems = rest[2 * n_w + 3:]
        pair = pl.program_id(0)
        gather = _TwoLevelGather(stacks, send_sems, recv_sems)

        def own_copies():
            mine = gather.index(gather.me)
            return [pltpu.make_async_copy(shard_refs[a], stacks[a].at[mine], local_sems.at[a]) for a in range(n_w)]

        @pl.when(pair == 0)
        def _():
            for cp in own_copies():
                cp.start()
            gather.start(shard_refs)

        @pl.when(pair == 2)
        def _():
            gather.forward()

        bd = _head_blockdiag(1.0 / HEAD_DIM)
        gq = gq_ref[...] * (HEAD_DIM ** -0.5)
        gk = gk_ref[...]

        def prep(c, carry):
            rows = pl.ds(pl.multiple_of(c * CHUNK, CHUNK), CHUNK)
            zq = zq_ref[rows, :].astype(F32)
            zk = zk_ref[rows, :].astype(F32)
            tq[...] = (zq * lax.rsqrt(_head_sum(zq * zq, bd) + EPS)) * gq
            tk[...] = (zk * lax.rsqrt(_head_sum(zk * zk, bd) + EPS)) * gk
            tv[...] = zv_ref[rows, :].astype(F32)
            _to_residue_major(qf, tq, c, seg)
            _to_residue_major(kf, tk, c, seg)
            _to_residue_major(vf, tv, c, seg)
            return carry

        lax.fori_loop(0, s // CHUNK, prep, 0)

        save_copies = [pltpu.make_async_copy(src, saved.at[pair, i], save_sems.at[i]) for i, src in enumerate((qf, kf, vf))]
        for cp in save_copies:
            cp.start()

        head0 = lax.broadcasted_iota(jnp.int32, (BLOCK, 128), 1) < HEAD_DIM

        def run_branch(branch, d):
            n_runs, w, steps, group, chains = _branch_geometry(d, seg)
            slots = [(ci, b) for ci in range(chains) for b in range(group)]

            def step(u, grp):
                starts = [_run_starts(u + ci, grp, d, seg, n_runs, w, group) for ci in range(chains)]
                kt = [[_load_tile(kf, st, w, b).astype(BF16) for b in range(-1, group)] for st in starts]
                vt = [[_load_tile(vf, st, w, b).astype(BF16) for b in range(-1, group)] for st in starts]
                scores = []
                for ci, b in slots:
                    q2 = _stack_heads(_load_tile(qf, starts[ci], w, b), head0)
                    kc = jnp.concatenate([kt[ci][b], kt[ci][b + 1]], axis=0)
                    mask = mask_ref[branch, 0]
                    if b == 0:
                        mask = jnp.where(grp == 0, mask_ref[branch, 1], mask)
                    scores.append(_dot(q2, kc, NT) + mask)
                sc = jnp.concatenate(scores, axis=0)
                m = jnp.max(sc, axis=-1, keepdims=True)
                e = jnp.exp((sc - m).astype(BF16))
                ones = jnp.ones((BLOCK, 128), BF16)
                fresh = []
                for n, (ci, b) in enumerate(slots):
                    rows = slice(2 * BLOCK * n, 2 * BLOCK * (n + 1))
                    v_aug = jnp.concatenate([jnp.concatenate([vt[ci][b], ones], axis=1),
                                             jnp.concatenate([vt[ci][b + 1], ones], axis=1)], axis=0)
                    o2 = _dot(e[rows], v_aug, NN)
                    den = o2[:, 128:]
                    o2 = o2[:, :128] * (1.0 / den)
                    l2 = m[rows] + jnp.log(den)
                    fresh.append((jnp.where(head0, o2[:BLOCK], o2[BLOCK:]), jnp.where(head0, l2[:BLOCK], l2[BLOCK:])))
                for n, (ci, b) in enumerate(slots):
                    o_new, l_new = fresh[n]
                    if branch > 0:
                        l_old = _load_tile(l_st, starts[ci], w, b)
                        diff = l_new - l_old
                        t = jnp.exp(-jnp.abs(diff))
                        big = 1.0 / (1.0 + t)
                        small = t * big
                        newer = diff >= 0.0
                        o_old = _load_tile(o_st, starts[ci], w, b)
                        o_new = o_old * jnp.where(newer, small, big) + o_new * jnp.where(newer, big, small)
                        l_new = jnp.maximum(l_old, l_new) + jnp.log(1.0 + t)
                    _store_tile(o_st, starts[ci], w, b, o_new)
                    _store_tile(l_st, starts[ci], w, b, l_new)

            def unit(it, carry):
                def inner(grp, carry2):
                    step(it * chains, grp)
                    return carry2
                return lax.fori_loop(0, steps, inner, carry)

            lax.fori_loop(0, d // chains, unit, 0)

        for branch, d in enumerate(DILATIONS):
            run_branch(branch, d)

        def finish(c, carry):
            _from_residue_major(o_ref, c * CHUNK, o_st, c, seg)
            _from_residue_major(lse_ref, c * CHUNK, l_st, c, seg)
            return carry

        lax.fori_loop(0, s // CHUNK, finish, 0)
        for cp in save_copies:
            cp.wait()

        @pl.when(pair == N_PAIRS - 1)
        def _():
            gather.finish()
            for cp in own_copies():
                cp.wait()

    def col(c0):
        return pl.BlockSpec((s, 128), lambda p: (0, c0 + p), pipeline_mode=pl.Buffered(1))

    hbm = pl.BlockSpec(memory_space=pl.ANY)
    out = pl.pallas_call(
        body, name="attention_forward", grid=(N_PAIRS,),
        in_specs=[col(12), col(16), col(20), _const_spec((1, 128)), _const_spec((1, 128)), _const_spec(masks.shape)]
        + [hbm] * n_w,
        out_specs=[pl.BlockSpec((s, 128), lambda p: (0, p)), pl.BlockSpec((s, 128), lambda p: (0, p))] + [hbm] * (n_w + 1),
        out_shape=[jax.ShapeDtypeStruct((s, D_ATTN), F32), jax.ShapeDtypeStruct((s, D_ATTN), F32)]
        + [jax.ShapeDtypeStruct((N_DEV,) + w.shape, w.dtype) for w in shards]
        + [jax.ShapeDtypeStruct((N_PAIRS, 3, s, 128), F32)],
        scratch_shapes=[pltpu.VMEM((s, 128), F32)] * 5 + [pltpu.VMEM((CHUNK, 128), F32)] * 3
        + [pltpu.SemaphoreType.DMA((7 * n_w,)), pltpu.SemaphoreType.DMA((7 * n_w,)), pltpu.SemaphoreType.DMA((n_w,)),
           pltpu.SemaphoreType.DMA((3,))],
        compiler_params=_params(("arbitrary",), vmem=VMEM_LIMIT),
    )(z, z, z, gq2, gk2, masks, *shards)
    return out[0], out[1], out[2:2 + n_w], out[2 + n_w]


def _conv_forward(zt, zprev, w, first):
    u, gb, gc = zt[:, :D_CONV], zt[:, D_CONV:2 * D_CONV], zt[:, 2 * D_CONV:]
    cu = gc * u
    cu_prev = jnp.where(first, 0.0, zprev[:, 2 * D_CONV:] * zprev[:, :D_CONV])
    row = lax.broadcasted_iota(jnp.int32, cu.shape, 0)
    cu1 = jnp.where(row >= 1, pltpu.roll(cu, 1, 0), cu_prev[7:8, :])
    cu2 = jnp.where(row >= 2, pltpu.roll(cu, 2, 0), jnp.where(row == 1, cu_prev[7:8, :], cu_prev[6:7, :]))
    cv = w[0:1, :] * cu2 + w[1:2, :] * cu1 + w[2:3, :] * cu
    return u, gb, gc, cu, cu1, cu2, cv


def _mix_out(z, y_attn, x, conv_w, g_conv_out, g_attn_out, g_ffn, w_out, tm):
    s = x.shape[0]
    hb = tm // 8

    def body(z_ref, zp_ref, ya_ref, x_ref, cw_ref, gc_ref, ga_ref, gf_ref, w_ref, mix_ref, x2_ref, h2_ref):
        first = pl.program_id(0) == 0
        zt = z_ref[...].astype(F32)
        _, gb, _, _, _, _, cv = _conv_forward(zt, zp_ref[...].astype(F32), cw_ref[...], first)
        nc, _ = _rms(gb * cv)
        na, _ = _rms(ya_ref[...])
        mix = jnp.concatenate([nc * gc_ref[...], na * ga_ref[...]], axis=-1).astype(BF16)
        mix_ref[...] = mix
        x2 = x_ref[...] + _dot(mix, w_ref[...], NN)
        x2_ref[...] = x2
        xh, _ = _rms(x2)
        h2_ref[...] = (xh * gf_ref[...]).astype(BF16)

    tile = lambda w: pl.BlockSpec((tm, w), lambda i: (i, 0))
    return pl.pallas_call(
        body, name="mix_out", grid=(s // tm,),
        in_specs=[tile(3 * D_CONV), pl.BlockSpec((8, 3 * D_CONV), lambda i: (jnp.maximum(i * hb - 1, 0), 0)),
                  tile(D_ATTN), tile(D_MODEL), _const_spec((8, D_CONV)), _const_spec((1, D_CONV)), _const_spec((1, D_ATTN)),
                  _const_spec((1, D_MODEL)), _const_spec((D_MODEL, D_MODEL))],
        out_specs=[tile(D_MODEL)] * 3,
        out_shape=[jax.ShapeDtypeStruct((s, D_MODEL), BF16), jax.ShapeDtypeStruct((s, D_MODEL), F32),
                   jax.ShapeDtypeStruct((s, D_MODEL), BF16)],
        compiler_params=_params(("arbitrary",), vmem=VMEM_LIMIT),
    )(z, z, y_attn, x, conv_w, g_conv_out, g_attn_out, g_ffn, w_out)


FF_CHUNK = 256


def _ffn_forward(h2, x2, target, w_gate_t, w_up_t, w_down, tm):
    s = h2.shape[0]

    def body(h_ref, x2_ref, t_ref, wg_ref, wu_ref, wd_ref, a_ref, b_ref, dy_ref, dyb_ref, loss_ref, f_buf):
        @pl.when(pl.program_id(0) == 0)
        def _():
            loss_ref[...] = jnp.zeros_like(loss_ref)

        h = h_ref[...]
        for c0 in range(0, D_FF, FF_CHUNK):
            rows = slice(c0, c0 + FF_CHUNK)
            a = _dot(h, wg_ref[rows, :], NT)
            b = _dot(h, wu_ref[rows, :], NT)
            a_ref[:, rows] = a.astype(BF16)
            b_ref[:, rows] = b.astype(BF16)
            f_buf[:, rows] = ((a * jax.nn.sigmoid(a)) * b).astype(BF16)
        y = x2_ref[...] + _dot(f_buf[...], wd_ref[...], NN)
        err = y - t_ref[...]
        dy = err * (1.0 / D_MODEL)
        dy_ref[...] = dy
        dyb_ref[...] = dy.astype(BF16)
        loss_ref[...] += jnp.sum(err * err)

    tile = lambda w: pl.BlockSpec((tm, w), lambda i: (i, 0))
    return pl.pallas_call(
        body, name="ffn_forward", grid=(s // tm,),
        in_specs=[tile(D_MODEL), tile(D_MODEL), tile(D_MODEL)] + [_const_spec((D_FF, D_MODEL))] * 3,
        out_specs=[tile(D_FF), tile(D_FF), tile(D_MODEL), tile(D_MODEL), pl.BlockSpec((8, 128), lambda i: (0, 0))],
        out_shape=[jax.ShapeDtypeStruct((s, D_FF), BF16), jax.ShapeDtypeStruct((s, D_FF), BF16),
                   jax.ShapeDtypeStruct((s, D_MODEL), F32), jax.ShapeDtypeStruct((s, D_MODEL), BF16),
                   jax.ShapeDtypeStruct((8, 128), F32)],
        scratch_shapes=[pltpu.VMEM((tm, D_FF), BF16)],
        compiler_params=_params(("arbitrary",), vmem=VMEM_LIMIT),
    )(h2, x2, target, w_gate_t, w_up_t, w_down)


def _ffn_backward(dy, a, b, x2, g_ffn, w_gate_t, w_up_t, w_down, tm):
    s = dy.shape[0]
    half = D_FF // 2
    pieces = [(p0, min(FF_CHUNK, half - p0)) for p0 in range(0, half, FF_CHUNK)]

    def body(dy_ref, a_ref, b_ref, x2_ref, g_ref, wg_ref, wu_ref, wd_ref, da_ref, db_ref, dx2_ref, dx2b_ref, dg_ref, acc):
        i, c = pl.program_id(0), pl.program_id(1)

        @pl.when((i == 0) & (c == 0))
        def _():
            dg_ref[...] = jnp.zeros_like(dg_ref)

        dyb = dy_ref[...].astype(BF16)
        for p0, width in pieces:
            rows = slice(p0, p0 + width)
            df = _dot(dyb, wd_ref[rows, :], NT)
            av = a_ref[:, rows].astype(F32)
            bv = b_ref[:, rows].astype(F32)
            sig = jax.nn.sigmoid(av)
            silu = av * sig
            da_ref[:, rows] = (df * bv * (sig * (1.0 + av * (1.0 - sig)))).astype(BF16)
            db_ref[:, rows] = (df * silu).astype(BF16)
        dh = _dot(da_ref[...], wg_ref[...], NN) + _dot(db_ref[...], wu_ref[...], NN)

        @pl.when(c == 0)
        def _():
            acc[...] = dh

        @pl.when(c == 1)
        def _():
            dh_all = acc[...] + dh
            xh, r = _rms(x2_ref[...])
            dg_ref[...] += jnp.sum(dh_all * xh, axis=0, keepdims=True)
            dx2 = dy_ref[...] + _rms_bwd(dh_all, xh, r, g_ref[...])
            dx2_ref[...] = dx2
            dx2b_ref[...] = dx2.astype(BF16)

    tile = lambda w: pl.BlockSpec((tm, w), lambda i, c: (i, 0))
    part = pl.BlockSpec((tm, half), lambda i, c: (i, c))
    weight = pl.BlockSpec((half, D_MODEL), lambda i, c: (c, 0))
    return pl.pallas_call(
        body, name="ffn_backward", grid=(s // tm, 2),
        in_specs=[tile(D_MODEL), part, part, tile(D_MODEL), _const_spec((1, D_MODEL)), weight, weight, weight],
        out_specs=[part, part, tile(D_MODEL), tile(D_MODEL), pl.BlockSpec((1, D_MODEL), lambda i, c: (0, 0))],
        out_shape=[jax.ShapeDtypeStruct((s, D_FF), BF16)] * 2
        + [jax.ShapeDtypeStruct((s, D_MODEL), F32), jax.ShapeDtypeStruct((s, D_MODEL), BF16),
           jax.ShapeDtypeStruct((1, D_MODEL), F32)],
        scratch_shapes=[pltpu.VMEM((tm, D_MODEL), F32)],
        compiler_params=_params(("arbitrary", "arbitrary"), vmem=VMEM_LIMIT),
    )(dy, a, b, x2, g_ffn, w_gate_t, w_up_t, w_down)


def _out_backward(dx2, z, y_attn, conv_w, g_conv_out, g_attn_out, w_out, tm):
    s = dx2.shape[0]
    hb = tm // 8

    def body(dx2_ref, z_ref, zp_ref, ya_ref, cw_ref, gc_ref, ga_ref, w_ref, dya_ref, dgb_ref, dcv_ref, dgc_ref, dga_ref):
        first = pl.program_id(0) == 0

        @pl.when(first)
        def _():
            dgc_ref[...] = jnp.zeros_like(dgc_ref)
            dga_ref[...] = jnp.zeros_like(dga_ref)

        dmix = _dot(dx2_ref[...], w_ref[...], NT)
        _, gb, _, _, _, _, cv = _conv_forward(z_ref[...].astype(F32), zp_ref[...].astype(F32), cw_ref[...], first)
        ych, rc = _rms(gb * cv)
        dnc = dmix[:, :D_CONV]
        dgc_ref[...] += jnp.sum(dnc * ych, axis=0, keepdims=True)
        dyc = _rms_bwd(dnc, ych, rc, gc_ref[...])
        dgb_ref[...] = (dyc * cv).astype(BF16)
        dcv_ref[...] = dyc * gb
        yah, ra = _rms(ya_ref[...])
        dna = dmix[:, D_CONV:]
        dga_ref[...] += jnp.sum(dna * yah, axis=0, keepdims=True)
        dya_ref[...] = _rms_bwd(dna, yah, ra, ga_ref[...])

    tile = lambda w: pl.BlockSpec((tm, w), lambda i: (i, 0))
    vec = pl.BlockSpec((1, D_CONV), lambda i: (0, 0))
    return pl.pallas_call(
        body, name="out_backward", grid=(s // tm,),
        in_specs=[tile(D_MODEL), tile(3 * D_CONV), pl.BlockSpec((8, 3 * D_CONV), lambda i: (jnp.maximum(i * hb - 1, 0), 0)),
                  tile(D_ATTN), _const_spec((8, D_CONV)), _const_spec((1, D_CONV)), _const_spec((1, D_ATTN)),
                  _const_spec((D_MODEL, D_MODEL))],
        out_specs=[tile(D_ATTN), tile(D_CONV), tile(D_CONV), vec, vec],
        out_shape=[jax.ShapeDtypeStruct((s, D_ATTN), F32), jax.ShapeDtypeStruct((s, D_CONV), BF16),
                   jax.ShapeDtypeStruct((s, D_CONV), F32), jax.ShapeDtypeStruct((1, D_CONV), F32),
                   jax.ShapeDtypeStruct((1, D_ATTN), F32)],
        compiler_params=_params(("arbitrary",), vmem=VMEM_LIMIT),
    )(dx2, z, z, y_attn, conv_w, g_conv_out, g_attn_out, w_out)


def _attention_backward(z, o, lse, dya, gq2, gk2, masks, saved, grad_stacks):
    s = z.shape[0]
    seg = s // RESIDUES
    n_g = len(grad_stacks)

    def body(z_hbm, o_hbm, lse_hbm, do_hbm, gq_ref, gk_ref, mask_ref, saved_hbm, *rest):
        stack_refs = rest[:n_g]
        dzq_ref, dzk_ref, dzv_ref, dgq_ref, dgk_ref = rest[n_g:n_g + 5]
        landing = rest[n_g + 5:2 * n_g + 5]
        (qf, kf, vf, dof, stats, dq, dk, dv, t0, t1, t2, zbuf, fbuf, sems, send_sems, recv_sems,
         load_sems) = rest[2 * n_g + 5:]
        pair = pl.program_id(0)
        scatter = _DirectReduceScatter(stack_refs, landing, send_sems, recv_sems)

        @pl.when(pair == 0)
        def _():
            scatter.start()

        saved_copies = [pltpu.make_async_copy(saved_hbm.at[pair, i], dst, load_sems.at[i]) for i, dst in enumerate((qf, kf, vf))]
        for cp in saved_copies:
            cp.start()

        n_chunks = s // CHUNK
        bd_mean = _head_blockdiag(1.0 / HEAD_DIM)
        bd_sum = _head_blockdiag(1.0)
        gq = gq_ref[...] * (HEAD_DIM ** -0.5)
        gk = gk_ref[...]
        lane = lax.broadcasted_iota(jnp.int32, (CHUNK, 128), 1) % HEAD_DIM

        def chunk_copies(c, slot, n_z, n_f):
            rows = pl.ds(pl.multiple_of(c * CHUNK, CHUNK), CHUNK)
            cps = []
            for i, col in enumerate((12, 16, 20)[:n_z]):
                cols = pl.ds(pl.multiple_of((col + pair) * 128, 128), 128)
                cps.append(pltpu.make_async_copy(z_hbm.at[rows, cols], zbuf.at[slot, i], sems.at[slot, i]))
            for i, src in enumerate((do_hbm, o_hbm, lse_hbm)[:n_f]):
                cols = pl.ds(pl.multiple_of(pair * 128, 128), 128)
                cps.append(pltpu.make_async_copy(src.at[rows, cols], fbuf.at[slot, i], sems.at[slot, 3 + i]))
            return cps

        def prime(n_z, n_f):
            for c0 in range(min(PREFETCH - 1, n_chunks)):
                for cp in chunk_copies(c0, c0, n_z, n_f):
                    cp.start()

        def prefetched(c, n_z, n_f):
            slot = c % PREFETCH
            ahead = c + PREFETCH - 1

            @pl.when(ahead < n_chunks)
            def _():
                for cp in chunk_copies(ahead, ahead % PREFETCH, n_z, n_f):
                    cp.start()

            for cp in chunk_copies(c, slot, n_z, n_f):
                cp.wait()
            return slot

        prime(0, 3)

        def prep(c, carry):
            slot = prefetched(c, 0, 3)
            rows = pl.ds(pl.multiple_of(c * CHUNK, CHUNK), CHUNK)
            _to_residue_major(dof, fbuf.at[slot, 0], c, seg)
            delta = _head_sum(fbuf[slot, 0] * fbuf[slot, 1], bd_sum)
            t0[...] = jnp.where(lane == 0, fbuf[slot, 2], jnp.where(lane == 1, delta, 0.0))
            _to_residue_major(stats, t0, c, seg)
            zero = jnp.zeros((CHUNK, 128), F32)
            dq[rows, :] = zero
            dk[rows, :] = zero
            dv[rows, :] = zero
            return carry

        lax.fori_loop(0, n_chunks, prep, 0)
        for cp in saved_copies:
            cp.wait()

        head0 = lax.broadcasted_iota(jnp.int32, (BLOCK, 128), 1) < HEAD_DIM

        def run_branch(branch, d):
            n_runs, w, steps, group, chains = _branch_geometry(d, seg)
            slots = [(ci, b) for ci in range(chains) for b in range(group)]

            def step(u, grp):
                starts = [_run_starts(u + ci, grp, d, seg, n_runs, w, group) for ci in range(chains)]
                kt = [[_load_tile(kf, st, w, b).astype(BF16) for b in range(-1, group)] for st in starts]
                vt = [[_load_tile(vf, st, w, b).astype(BF16) for b in range(-1, group)] for st in starts]
                dk_t = [[jnp.zeros((BLOCK, 128), F32)] * (group + 1) for _ in range(chains)]
                dv_t = [[jnp.zeros((BLOCK, 128), F32)] * (group + 1) for _ in range(chains)]
                dq_t = []
                q2s, do2s, kcs, scores, dps, lses, deltas = [], [], [], [], [], [], []
                for ci, b in slots:
                    q2 = _stack_heads(_load_tile(qf, starts[ci], w, b), head0)
                    do2 = _stack_heads(_load_tile(dof, starts[ci], w, b), head0)
                    st = _load_tile(stats, starts[ci], w, b)
                    lses += [st[:, 0:1], st[:, HEAD_DIM:HEAD_DIM + 1]]
                    deltas += [st[:, 1:2], st[:, HEAD_DIM + 1:HEAD_DIM + 2]]
                    kc = jnp.concatenate([kt[ci][b], kt[ci][b + 1]], axis=0)
                    vc = jnp.concatenate([vt[ci][b], vt[ci][b + 1]], axis=0)
                    mask = mask_ref[branch, 0]
                    if b == 0:
                        mask = jnp.where(grp == 0, mask_ref[branch, 1], mask)
                    scores.append(_dot(q2, kc, NT) + mask)
                    dps.append(_dot(do2, vc, NT))
                    q2s.append(q2)
                    do2s.append(do2)
                    kcs.append(kc)
                p_all = jnp.exp(jnp.concatenate(scores, axis=0) - jnp.concatenate(lses, axis=0))
                ds_all = (p_all * (jnp.concatenate(dps, axis=0) - jnp.concatenate(deltas, axis=0))).astype(BF16)
                p_all = p_all.astype(BF16)
                for n, (ci, b) in enumerate(slots):
                    rows = slice(2 * BLOCK * n, 2 * BLOCK * (n + 1))
                    ds, q2, do2, kc = ds_all[rows], q2s[n], do2s[n], kcs[n]
                    dq2 = _dot(ds, kc, NN)
                    dq_t.append(jnp.where(head0, dq2[:BLOCK], dq2[BLOCK:]))
                    dkc = _dot(ds, q2, TN)
                    dvc = _dot(p_all[rows], do2, TN)
                    dk_t[ci][b] = dk_t[ci][b] + dkc[:BLOCK]
                    dk_t[ci][b + 1] = dk_t[ci][b + 1] + dkc[BLOCK:]
                    dv_t[ci][b] = dv_t[ci][b] + dvc[:BLOCK]
                    dv_t[ci][b + 1] = dv_t[ci][b + 1] + dvc[BLOCK:]
                for n, (ci, b) in enumerate(slots):
                    _add_tile(dq, starts[ci], w, b, dq_t[n])
                for ci in range(chains):
                    for b in range(-1, group):
                        _add_tile(dk, starts[ci], w, b, dk_t[ci][b + 1])
                        _add_tile(dv, starts[ci], w, b, dv_t[ci][b + 1])

            def unit(it, carry):
                def inner(grp, carry2):
                    step(it * chains, grp)
                    return carry2
                return lax.fori_loop(0, steps, inner, carry)

            lax.fori_loop(0, d // chains, unit, 0)

        for branch, d in enumerate(DILATIONS):
            run_branch(branch, d)

        prime(2, 0)

        def finish(c, carry):
            acc_q, acc_k = carry
            rows = pl.ds(pl.multiple_of(c * CHUNK, CHUNK), CHUNK)
            _from_residue_major(t0, 0, dq, c, seg)
            _from_residue_major(t1, 0, dk, c, seg)
            _from_residue_major(t2, 0, dv, c, seg)
            slot = prefetched(c, 2, 0)
            zq = zbuf[slot, 0].astype(F32)
            rq = lax.rsqrt(_head_sum(zq * zq, bd_mean) + EPS)
            qh = zq * rq
            dqn = t0[...]
            acc_q = acc_q + jnp.sum(dqn * qh, axis=0, keepdims=True)
            t = dqn * gq
            dzq_ref[rows, :] = (rq * (t - qh * _head_sum(t * qh, bd_mean))).astype(BF16)
            zk = zbuf[slot, 1].astype(F32)
            rk = lax.rsqrt(_head_sum(zk * zk, bd_mean) + EPS)
            kh = zk * rk
            dkn = t1[...]
            acc_k = acc_k + jnp.sum(dkn * kh, axis=0, keepdims=True)
            t = dkn * gk
            dzk_ref[rows, :] = (rk * (t - kh * _head_sum(t * kh, bd_mean))).astype(BF16)
            dzv_ref[rows, :] = t2[...].astype(BF16)
            return acc_q, acc_k

        zero = jnp.zeros((1, 128), F32)
        acc_q, acc_k = lax.fori_loop(0, s // CHUNK, finish, (zero, zero))
        dgq_ref[...] = acc_q * (HEAD_DIM ** -0.5)
        dgk_ref[...] = acc_k

        @pl.when(pair == N_PAIRS - 1)
        def _():
            scatter.finish()

    hbm = pl.BlockSpec(memory_space=pl.ANY)
    gain = pl.BlockSpec((None, 1, 128), lambda p: (p, 0, 0))
    dz_spec = pl.BlockSpec((s, 128), lambda p: (0, p))
    out = pl.pallas_call(
        body, name="attention_backward", grid=(N_PAIRS,),
        in_specs=[hbm, hbm, hbm, hbm, _const_spec((1, 128)), _const_spec((1, 128)), _const_spec(masks.shape), hbm]
        + [hbm] * n_g,
        out_specs=[dz_spec, dz_spec, dz_spec, gain, gain] + [hbm] * n_g,
        out_shape=[jax.ShapeDtypeStruct((s, D_ATTN), BF16)] * 3 + [jax.ShapeDtypeStruct((N_PAIRS, 1, 128), F32)] * 2
        + [jax.ShapeDtypeStruct((N_DEV - 1,) + g.shape[1:], g.dtype) for g in grad_stacks],
        scratch_shapes=[pltpu.VMEM((s, 128), F32)] * 8 + [pltpu.VMEM((CHUNK, 128), F32)] * 3
        + [pltpu.VMEM((PREFETCH, 3, CHUNK, 128), BF16), pltpu.VMEM((PREFETCH, 3, CHUNK, 128), F32),
           pltpu.SemaphoreType.DMA((PREFETCH, 6)),
           pltpu.SemaphoreType.DMA((7 * n_g,)), pltpu.SemaphoreType.DMA((7 * n_g,)), pltpu.SemaphoreType.DMA((3,))],
        compiler_params=_params(("arbitrary",), vmem=VMEM_LIMIT),
    )(z, o, lse, dya, gq2, gk2, masks, saved, *grad_stacks)
    return out[:5], out[5:]


def _in_backward(dcv, dgb, dzq, dzk, dzv, z, x, dx2, g_mix, conv_w, w_in_t, tm):
    s = x.shape[0]
    hb = tm // 8
    last_halo = s // 8 - 1

    def body(dcv_ref, dcn_ref, dgb_ref, dzq_ref, dzk_ref, dzv_ref, z_ref, zp_ref, x_ref, dx2_ref, g_ref, cw_ref, w_ref,
             dz_ref, dx_ref, dg_ref, dcw_ref):
        i = pl.program_id(0)
        first = i == 0
        last = i == pl.num_programs(0) - 1

        @pl.when(first)
        def _():
            dg_ref[...] = jnp.zeros_like(dg_ref)
            dcw_ref[...] = jnp.zeros_like(dcw_ref)

        w = cw_ref[...]
        u, _, gc, cu, cu1, cu2, _ = _conv_forward(z_ref[...].astype(F32), zp_ref[...].astype(F32), w, first)
        dcv_t = dcv_ref[...]
        nxt = jnp.where(last, 0.0, dcn_ref[...])
        row = lax.broadcasted_iota(jnp.int32, dcv_t.shape, 0)
        up1 = jnp.where(row < tm - 1, pltpu.roll(dcv_t, tm - 1, 0), nxt[0:1, :])
        up2 = jnp.where(row < tm - 2, pltpu.roll(dcv_t, tm - 2, 0), jnp.where(row == tm - 2, nxt[0:1, :], nxt[1:2, :]))
        dcu = w[2:3, :] * dcv_t + w[1:2, :] * up1 + w[0:1, :] * up2
        dcw = jnp.concatenate([jnp.sum(dcv_t * cu2, axis=0, keepdims=True), jnp.sum(dcv_t * cu1, axis=0, keepdims=True),
                               jnp.sum(dcv_t * cu, axis=0, keepdims=True), jnp.zeros((5, D_CONV), F32)], axis=0)
        dcw_ref[...] += dcw
        dz_ref[:, :D_CONV] = (dcu * gc).astype(BF16)
        dz_ref[:, D_CONV:2 * D_CONV] = dgb_ref[...]
        dz_ref[:, 2 * D_CONV:3 * D_CONV] = (dcu * u).astype(BF16)
        dz_ref[:, 3 * D_CONV:3 * D_CONV + D_ATTN] = dzq_ref[...]
        dz_ref[:, 3 * D_CONV + D_ATTN:3 * D_CONV + 2 * D_ATTN] = dzk_ref[...]
        dz_ref[:, 3 * D_CONV + 2 * D_ATTN:] = dzv_ref[...]
        dh = _dot(dz_ref[...], w_ref[...], NN)
        xh, r = _rms(x_ref[...])
        dg_ref[...] += jnp.sum(dh * xh, axis=0, keepdims=True)
        dx_ref[...] = dx2_ref[...] + _rms_bwd(dh, xh, r, g_ref[...])

    tile = lambda w: pl.BlockSpec((tm, w), lambda i: (i, 0))
    return pl.pallas_call(
        body, name="in_backward", grid=(s // tm,),
        in_specs=[tile(D_CONV), pl.BlockSpec((8, D_CONV), lambda i: (jnp.minimum((i + 1) * hb, last_halo), 0)),
                  tile(D_CONV), tile(D_ATTN), tile(D_ATTN), tile(D_ATTN),
                  tile(3 * D_CONV), pl.BlockSpec((8, 3 * D_CONV), lambda i: (jnp.maximum(i * hb - 1, 0), 0)),
                  tile(D_MODEL), tile(D_MODEL), _const_spec((1, D_MODEL)), _const_spec((8, D_CONV)),
                  _const_spec((D_IN, D_MODEL))],
        out_specs=[tile(D_IN), tile(D_MODEL), pl.BlockSpec((1, D_MODEL), lambda i: (0, 0)),
                   pl.BlockSpec((8, D_CONV), lambda i: (0, 0))],
        out_shape=[jax.ShapeDtypeStruct((s, D_IN), BF16), jax.ShapeDtypeStruct((s, D_MODEL), F32),
                   jax.ShapeDtypeStruct((1, D_MODEL), F32), jax.ShapeDtypeStruct((8, D_CONV), F32)],
        compiler_params=_params(("arbitrary",), vmem=VMEM_LIMIT),
    )(dcv, dcv, dgb, dzq, dzk, dzv, z, z, x, dx2, g_mix, conv_w, w_in_t)


def _weight_grad(name, a, b, tn, tk, gate=None):
    s, n = a.shape
    steps = s // tk

    def body(*refs):
        gate_ref = refs[0] if gate is not None else None
        a_ref, b_ref, out_ref, acc = refs[1:] if gate is not None else refs
        k = pl.program_id(1)

        @pl.when(k == 0)
        def _():
            acc[...] = jnp.zeros_like(acc)

        lhs = a_ref[...]
        if gate is not None:
            gv = gate_ref[...].astype(F32)
            lhs = ((gv * jax.nn.sigmoid(gv)) * lhs.astype(F32)).astype(BF16)
        acc[...] += _dot(lhs, b_ref[...].astype(BF16), TN)

        @pl.when(k == steps - 1)
        def _():
            out_ref[...] = acc[...].astype(BF16)

    lhs_spec = pl.BlockSpec((tk, tn), lambda j, k: (k, j))
    return pl.pallas_call(
        body, name=name, grid=(n // tn, steps),
        in_specs=([lhs_spec] if gate is not None else []) + [lhs_spec, pl.BlockSpec((tk, D_MODEL), lambda j, k: (k, 0))],
        out_specs=pl.BlockSpec((tn, D_MODEL), lambda j, k: (j, 0)),
        out_shape=jax.ShapeDtypeStruct((n, D_MODEL), BF16),
        scratch_shapes=[pltpu.VMEM((tn, D_MODEL), F32)],
        compiler_params=_params(("parallel", "arbitrary"), vmem=VMEM_LIMIT),
    )(*([gate] if gate is not None else []), a, b)


def _adamw_math(w, g, m, v):
    nm = ADAM_B1 * m + (1.0 - ADAM_B1) * g
    nv = ADAM_B2 * v + (1.0 - ADAM_B2) * (g * g)
    m_hat = nm / (1.0 - ADAM_B1 ** ADAM_STEP)
    v_hat = nv / (1.0 - ADAM_B2 ** ADAM_STEP)
    return -ADAM_LR * (m_hat / (jnp.sqrt(v_hat) + ADAM_EPS) + ADAM_WD * w), nm, nv


def _adamw_received(name, device, w, stack, received, m, v):
    def body(dev_ref, w_ref, own_ref, recv_ref, m_ref, v_ref, g_ref, d_ref, nm_ref, nv_ref):
        del dev_ref
        g = own_ref[...].astype(F32)
        for k in range(N_DEV - 1):
            g = g + recv_ref[k].astype(F32)
        g_ref[...] = g
        d_ref[...], nm_ref[...], nv_ref[...] = _adamw_math(w_ref[...], g, m_ref[...], v_ref[...])

    rows = stack.shape[1]
    tr = rows // 4 if rows % 64 == 0 else rows // 2
    tile = pl.BlockSpec((tr, D_MODEL), lambda i, dref: (i, 0))
    return pl.pallas_call(
        body, name=name,
        grid_spec=pltpu.PrefetchScalarGridSpec(
            num_scalar_prefetch=1, grid=(rows // tr,),
            in_specs=[tile, pl.BlockSpec((None, tr, D_MODEL), lambda i, dref: (dref[0], i, 0)),
                      pl.BlockSpec((N_DEV - 1, tr, D_MODEL), lambda i, dref: (0, i, 0)), tile, tile],
            out_specs=[tile] * 4),
        out_shape=[jax.ShapeDtypeStruct((rows, D_MODEL), F32)] * 4,
        compiler_params=_params(("arbitrary",), vmem=VMEM_LIMIT),
    )(device, w, stack, received, m, v)


def _adamw(name, w, g, m, v):
    def body(w_ref, g_ref, m_ref, v_ref, d_ref, nm_ref, nv_ref):
        d_ref[...], nm_ref[...], nv_ref[...] = _adamw_math(w_ref[...], g_ref[...], m_ref[...], v_ref[...])

    rows = w.shape[0]
    tr = 256 if rows % 256 == 0 else rows
    spec = pl.BlockSpec((tr, w.shape[1]), lambda i: (i, 0))
    return pl.pallas_call(
        body, name=name, grid=(rows // tr,),
        in_specs=[spec] * 4, out_specs=[spec] * 3,
        out_shape=[jax.ShapeDtypeStruct(w.shape, F32)] * 3,
        compiler_params=_params(("parallel",)),
    )(w, g, m, v)


def kernel(x, g_mix, w_in, conv_w, g_q, g_k, g_conv_out, g_attn_out, w_out, g_ffn, w_gate, w_up, w_down, loss_target, m_g_mix, m_w_in, m_conv_w, m_g_q, m_g_k, m_g_conv_out, m_g_attn_out, m_w_out, m_g_ffn, m_w_gate, m_w_up, m_w_down, v_g_mix, v_w_in, v_conv_w, v_g_q, v_g_k, v_g_conv_out, v_g_attn_out, v_w_out, v_g_ffn, v_w_gate, v_w_up, v_w_down):
    s = x.shape[1]
    tm = min(512, s)
    xs, target = x[0], loss_target[0]
    px, py, pc = lax.axis_index("x"), lax.axis_index("y"), lax.axis_index("c")
    device = 4 * px + 2 * py + pc

    conv_block = jnp.zeros((8, 128), F32).at[:3, :HEAD_DIM].set(conv_w[0])
    first, shards = [w_in[0].T, conv_block], [w_out[0], w_gate[0].T, w_up[0].T, w_down[0]]
    w_in_g, conv_g, *shards = _gather_first_weights(first, [BF16, F32], shards)
    w_in_t = w_in_g.reshape(D_IN, D_MODEL)
    conv_full = jnp.transpose(conv_g[:, :3, :HEAD_DIM], (1, 0, 2)).reshape(3, D_CONV)
    conv_full = jnp.concatenate([conv_full, jnp.zeros((5, D_CONV), F32)], axis=0)
    gq2 = jnp.concatenate([g_q, g_q], axis=-1)
    gk2 = jnp.concatenate([g_k, g_k], axis=-1)

    h1, z = _in_projection(xs, g_mix, w_in_t, tm)
    masks = _permuted_masks()
    y_attn, lse, gathered, qkv_saved = _attention_forward(z, gq2, gk2, masks, shards)
    w_out_f = gathered[0].reshape(D_MODEL, D_MODEL)
    w_gate_t, w_up_t, w_down_f = [g.reshape(D_FF, D_MODEL) for g in gathered[1:]]
    mix, x2, h2 = _mix_out(z, y_attn, xs, conv_full, g_conv_out, g_attn_out, g_ffn, w_out_f, tm)
    a, b, dy, dy_bf, sq_err = _ffn_forward(h2, x2, target, w_gate_t, w_up_t, w_down_f, tm)

    tk = min(2048, s)
    da, db, dx2, dx2_bf, dg_ffn = _ffn_backward(dy, a, b, x2, g_ffn, w_gate_t, w_up_t, w_down_f, tm)
    dya, dgb, dcv, dg_conv_out, dg_attn_out = _out_backward(dx2_bf, z, y_attn, conv_full, g_conv_out, g_attn_out, w_out_f, tm)
    early = [
        _weight_grad("grad_w_out", mix, dx2_bf, D_MODEL, tk),
        _weight_grad("grad_w_gate", da, h2, D_FF // 2, tk),
        _weight_grad("grad_w_up", db, h2, D_FF // 2, tk),
        _weight_grad("grad_w_down", b, dy_bf, D_FF // 2, tk, gate=a),
    ]
    early = [g.reshape(N_DEV, g.shape[0] // N_DEV, D_MODEL) for g in early]
    (dzq, dzk, dzv, dgq_pairs, dgk_pairs), landed = _attention_backward(z, y_attn, lse, dya, gq2, gk2, masks, qkv_saved, early)
    dz, grad_x, dg_mix, dconv = _in_backward(dcv, dgb, dzq, dzk, dzv, z, xs, dx2, g_mix, conv_full, w_in_t, tm)
    dev = jnp.reshape(device, (1,)).astype(jnp.int32)
    scattered = dict(zip(("w_out", "w_gate", "w_up", "w_down"), zip(early, landed)))

    late = [_weight_grad("grad_w_in", dz, h1, D_IN // 2, tk).reshape(4, 2, D_IN // N_DEV, D_MODEL)]
    from_sibling = _reduce_scatter_cores(late)
    core = jnp.reshape(pc, (1,)).astype(jnp.int32)
    partials = _add_core_partials(core, late, from_sibling)
    from_chips = _reduce_scatter_chips(partials)
    chip = jnp.reshape(2 * px + py, (1,)).astype(jnp.int32)
    (gw_in_t,) = _add_chip_partials(chip, partials, from_chips)
    gw_in = gw_in_t.T

    dg_q = jnp.sum(dgq_pairs.reshape(2 * N_PAIRS, HEAD_DIM), axis=0)
    dg_k = jnp.sum(dgk_pairs.reshape(2 * N_PAIRS, HEAD_DIM), axis=0)
    zeros = lambda n: jnp.zeros((n,), F32)
    small = jnp.stack([
        dg_mix[0], dg_ffn[0],
        jnp.concatenate([dg_conv_out[0], dg_attn_out[0]]),
        jnp.concatenate([dg_q, dg_k, zeros(D_MODEL - 2 * HEAD_DIM)]),
        jnp.concatenate([dconv[0], dconv[1]]),
        jnp.concatenate([dconv[2], zeros(D_CONV)]),
        jnp.concatenate([sq_err[0, :1], zeros(D_MODEL - 1)]),
        zeros(D_MODEL),
    ])
    total = _all_reduce_small(small)
    loss = total[6, 0] * (0.5 / D_MODEL)
    gg_mix, gg_ffn = total[0:1], total[1:2]
    gg_conv_out, gg_attn_out = total[2:3, :D_CONV], total[2:3, D_CONV:]
    gg_q, gg_k = total[3:4, :HEAD_DIM], total[3:4, HEAD_DIM:2 * HEAD_DIM]
    conv_total = jnp.stack([total[4, :D_CONV], total[4, D_CONV:], total[5, :D_CONV]])
    g_conv = lax.dynamic_slice(conv_total, (0, device * HEAD_DIM), (3, HEAD_DIM))

    names = ["g_mix", "w_in", "conv_w", "g_q", "g_k", "g_conv_out", "g_attn_out", "w_out", "g_ffn", "w_gate", "w_up", "w_down"]
    weights = [g_mix, w_in[0], conv_w[0], g_q, g_k, g_conv_out, g_attn_out, w_out[0], g_ffn, w_gate[0], w_up[0], w_down[0]]
    grad_list = [gg_mix, None, g_conv, gg_q, gg_k, gg_conv_out, gg_attn_out, None, gg_ffn, None, None, None]
    m_list = [m_g_mix, m_w_in[0], m_conv_w[0], m_g_q, m_g_k, m_g_conv_out, m_g_attn_out, m_w_out[0], m_g_ffn, m_w_gate[0], m_w_up[0], m_w_down[0]]
    v_list = [v_g_mix, v_w_in[0], v_conv_w[0], v_g_q, v_g_k, v_g_conv_out, v_g_attn_out, v_w_out[0], v_g_ffn, v_w_gate[0], v_w_up[0], v_w_down[0]]
    stacked = {"w_in", "conv_w", "w_out", "w_gate", "w_up", "w_down"}
    transposed = {"w_gate", "w_up"}
    results = {}
    operands = {n: (w, g, m, v) for n, w, g, m, v in zip(names, weights, grad_list, m_list, v_list)}
    for name in scattered:
        w, _, m, v = operands[name]
        if name in transposed:
            w, m, v = w.T, m.T, v.T
        results[name] = _adamw_received("adamw_" + name, dev, w, *scattered[name], m, v)
        if name in transposed:
            results[name] = [t.T for t in results[name]]
    for name in names:
        if name not in results:
            w, g, m, v = operands[name]
            g = gw_in if name == "w_in" else g
            results[name] = [g, *_adamw("adamw_" + name, w, g, m, v)]
    outs = []
    for part in range(4):
        outs += [results[n][part][None] if n in stacked else results[n][part] for n in names]
    return (loss, grad_x[None], *outs)
```

```python
import jax
import jax.numpy as jnp
import numpy as np
from jax import lax
from jax.experimental import pallas as pl
from jax.experimental.pallas import tpu as pltpu

F32 = jnp.float32
BF16 = jnp.bfloat16

D_MODEL = 1024
D_CONV = 512
D_ATTN = 512
D_IN = 3072
D_FF = 2816
HEAD_DIM = 64
N_PAIRS = 4
BLOCK = 128
DILATIONS = (1, 4, 16)
N_DEV = 8
EPS = 1e-6
NEG = -1e30

ADAM_LR = 0.001
ADAM_B1 = 0.9
ADAM_B2 = 0.999
ADAM_EPS = 1e-08
ADAM_WD = 0.01
ADAM_STEP = 10

NT = (((1,), (1,)), ((), ()))
NN = (((1,), (0,)), ((), ()))
TN = (((0,), (0,)), ((), ()))
MESH = pl.DeviceIdType.MESH

VMEM_LIMIT = 56 * 1024 * 1024


def _params(semantics=None, vmem=None):
    kw = {}
    if semantics is not None:
        kw["dimension_semantics"] = semantics
    if vmem is not None:
        kw["vmem_limit_bytes"] = vmem
    return pltpu.CompilerParams(**kw)


def _dot(a, b, dims):
    return lax.dot_general(a, b, dims, preferred_element_type=F32)


def _const_spec(shape):
    n = len(shape)
    return pl.BlockSpec(shape, lambda *_: (0,) * n, pipeline_mode=pl.Buffered(1))


def _rms(x):
    r = lax.rsqrt(jnp.mean(x * x, axis=-1, keepdims=True) + EPS)
    return x * r, r


def _rms_bwd(dy, xh, r, g):
    t = dy * g
    return r * (t - xh * jnp.mean(t * xh, axis=-1, keepdims=True))


def _head_blockdiag(scale):
    i = lax.broadcasted_iota(jnp.int32, (128, 128), 0) // HEAD_DIM
    j = lax.broadcasted_iota(jnp.int32, (128, 128), 1) // HEAD_DIM
    return jnp.where(i == j, scale, 0.0).astype(BF16)


def _head_sum(x, bd):
    hi = x.astype(BF16)
    lo = (x - hi.astype(F32)).astype(BF16)
    return _dot(hi, bd, NN) + _dot(lo, bd, NN)


def _place():
    return lax.axis_index("x"), lax.axis_index("y"), lax.axis_index("c")


class _TwoLevelGather:
    def __init__(self, stacks, send_sems, recv_sems):
        self.stacks, self.send_sems, self.recv_sems = stacks, send_sems, recv_sems
        x, y, c = _place()
        self.c = c
        self.me, self.sibling = (x, y, c), (x, y, 1 - c)
        self.chips = [(x, 1 - y), (1 - x, y), (1 - x, 1 - y)]

    @staticmethod
    def index(p):
        return 4 * p[0] + 2 * p[1] + p[2]

    def copy(self, a, k, block, to, src=None):
        dst = self.stacks[a].at[self.index(block)]
        return pltpu.make_async_remote_copy(
            src_ref=dst if src is None else src, dst_ref=dst,
            send_sem=self.send_sems.at[7 * a + k], recv_sem=self.recv_sems.at[7 * a + k],
            device_id=to, device_id_type=MESH)

    def first(self, a, src=None):
        cps = [self.copy(a, 0, self.me, self.sibling, src)]
        return cps + [self.copy(a, 1 + j, self.me, (*chip, self.c), src) for j, chip in enumerate(self.chips)]

    def forwards(self, a):
        return [self.copy(a, 4 + j, (*chip, self.c), self.sibling) for j, chip in enumerate(self.chips)]

    def start(self, srcs=None):
        for a in range(len(self.stacks)):
            for cp in self.first(a, None if srcs is None else srcs[a]):
                cp.start()

    def forward(self):
        for a in range(len(self.stacks)):
            for j, (chip, fwd) in enumerate(zip(self.chips, self.forwards(a))):
                self.copy(a, 1 + j, (*chip, self.c), self.me).wait_recv()
                fwd.start()

    def finish(self):
        for a in range(len(self.stacks)):
            self.copy(a, 0, self.sibling, self.me).wait_recv()
            for j, chip in enumerate(self.chips):
                self.copy(a, 4 + j, (*chip, 1 - self.c), self.me).wait_recv()
        for a in range(len(self.stacks)):
            for cp in self.first(a) + self.forwards(a):
                cp.wait_send()


def _gather_first_weights(gathered_parts, gathered_dtypes, cast_parts):
    n, m = len(gathered_parts), len(cast_parts)

    def body(*refs):
        ins, casts_in = refs[:n], refs[n:n + m]
        outs, casts_out = refs[n + m:2 * n + m], refs[2 * n + m:2 * n + 2 * m]
        send_sems, recv_sems = refs[2 * n + 2 * m], refs[2 * n + 2 * m + 1]
        gather = _TwoLevelGather(outs, send_sems, recv_sems)
        for a in range(n):
            outs[a][gather.index(gather.me)] = ins[a][...].astype(outs[a].dtype)
        gather.start()
        for a in range(m):
            casts_out[a][...] = casts_in[a][...].astype(BF16)
        gather.forward()
        gather.finish()

    vm = pl.BlockSpec(memory_space=pltpu.VMEM)
    return pl.pallas_call(
        body, name="weight_all_gather",
        out_shape=[jax.ShapeDtypeStruct((N_DEV,) + p.shape, dt) for p, dt in zip(gathered_parts, gathered_dtypes)]
        + [jax.ShapeDtypeStruct(p.shape, BF16) for p in cast_parts],
        in_specs=[vm] * (n + m), out_specs=[vm] * (n + m),
        scratch_shapes=[pltpu.SemaphoreType.DMA((7 * n,)), pltpu.SemaphoreType.DMA((7 * n,))],
        compiler_params=_params(vmem=VMEM_LIMIT),
    )(*gathered_parts, *cast_parts)


class _DirectReduceScatter:
    def __init__(self, stacks, landing, send_sems, recv_sems):
        self.stacks, self.landing, self.send_sems, self.recv_sems = stacks, landing, send_sems, recv_sems
        self.place = _place()

    def copies(self):
        x, y, c = self.place
        cps = []
        for a in range(len(self.stacks)):
            for k in range(1, N_DEV):
                peer = (1 - x if k & 4 else x, 1 - y if k & 2 else y, 1 - c if k & 1 else c)
                cps.append(pltpu.make_async_remote_copy(
                    src_ref=self.stacks[a].at[4 * peer[0] + 2 * peer[1] + peer[2]], dst_ref=self.landing[a].at[k - 1],
                    send_sem=self.send_sems.at[7 * a + k - 1], recv_sem=self.recv_sems.at[7 * a + k - 1],
                    device_id=peer, device_id_type=MESH))
        return cps

    def start(self):
        for cp in self.copies():
            cp.start()

    def finish(self):
        for cp in self.copies():
            cp.wait_recv()
        for cp in self.copies():
            cp.wait_send()


def _all_reduce_small(v):
    def body(v_ref, out_ref, gathered, send_sems, recv_sems):
        x, y, c = _place()
        mine = 4 * x + 2 * y + c
        gathered[mine] = v_ref[...]

        def copy(k):
            peer = (1 - x if k & 4 else x, 1 - y if k & 2 else y, 1 - c if k & 1 else c)
            return pltpu.make_async_remote_copy(
                src_ref=gathered.at[mine], dst_ref=gathered.at[mine],
                send_sem=send_sems.at[k - 1], recv_sem=recv_sems.at[k - 1], device_id=peer, device_id_type=MESH)

        copies = [copy(k) for k in range(1, N_DEV)]
        for cp in copies:
            cp.start()
        for cp in copies:
            cp.wait_recv()
        for cp in copies:
            cp.wait_send()
        total = gathered[0]
        for d in range(1, N_DEV):
            total = total + gathered[d]
        out_ref[...] = total

    vm = pl.BlockSpec(memory_space=pltpu.VMEM)
    return pl.pallas_call(
        body, name="small_all_reduce",
        out_shape=jax.ShapeDtypeStruct(v.shape, F32),
        in_specs=[vm], out_specs=vm,
        scratch_shapes=[pltpu.VMEM((N_DEV,) + v.shape, F32),
                        pltpu.SemaphoreType.DMA((N_DEV - 1,)), pltpu.SemaphoreType.DMA((N_DEV - 1,))],
    )(v)


def _reduce_scatter_cores(grads):
    n = len(grads)

    def body(*refs):
        ins, outs = refs[:n], refs[n:2 * n]
        send_sems, recv_sems = refs[2 * n], refs[2 * n + 1]
        x, y, c = _place()
        copies = []
        for a in range(n):
            for k in range(4):
                copies.append(pltpu.make_async_remote_copy(
                    src_ref=ins[a].at[k, 1 - c], dst_ref=outs[a].at[k],
                    send_sem=send_sems.at[4 * a + k], recv_sem=recv_sems.at[4 * a + k],
                    device_id=(x, y, 1 - c), device_id_type=MESH))
        for cp in copies:
            cp.start()
        for cp in copies:
            cp.wait_recv()
        for cp in copies:
            cp.wait_send()

    hbm = pl.BlockSpec(memory_space=pl.ANY)
    return pl.pallas_call(
        body, name="grad_reduce_scatter_cores",
        out_shape=[jax.ShapeDtypeStruct((4,) + g.shape[2:], g.dtype) for g in grads],
        in_specs=[hbm] * n, out_specs=[hbm] * n,
        scratch_shapes=[pltpu.SemaphoreType.DMA((4 * n,)), pltpu.SemaphoreType.DMA((4 * n,))],
    )(*grads)


def _reduce_scatter_chips(partials):
    n = len(partials)

    def body(*refs):
        ins, outs = refs[:n], refs[n:2 * n]
        send_sems, recv_sems = refs[2 * n], refs[2 * n + 1]
        x, y, c = _place()
        copies = []
        for a in range(n):
            for j in (1, 2, 3):
                px = 1 - x if j & 2 else x
                py = 1 - y if j & 1 else y
                copies.append(pltpu.make_async_remote_copy(
                    src_ref=ins[a].at[2 * px + py], dst_ref=outs[a].at[j - 1],
                    send_sem=send_sems.at[3 * a + j - 1], recv_sem=recv_sems.at[3 * a + j - 1],
                    device_id=(px, py, c), device_id_type=MESH))
        for cp in copies:
            cp.start()
        for cp in copies:
            cp.wait_recv()
        for cp in copies:
            cp.wait_send()

    hbm = pl.BlockSpec(memory_space=pl.ANY)
    return pl.pallas_call(
        body, name="grad_reduce_scatter_chips",
        out_shape=[jax.ShapeDtypeStruct((3,) + p.shape[1:], p.dtype) for p in partials],
        in_specs=[hbm] * n, out_specs=[hbm] * n,
        scratch_shapes=[pltpu.SemaphoreType.DMA((3 * n,)), pltpu.SemaphoreType.DMA((3 * n,))],
    )(*partials)


def _add_core_partials(core, grads, received):
    n = len(grads)

    def body(core_ref, *refs):
        del core_ref
        gs, rs, outs = refs[:n], refs[n:2 * n], refs[2 * n:]
        for a in range(n):
            outs[a][...] = (gs[a][...].astype(F32) + rs[a][...].astype(F32)).astype(outs[a].dtype)

    in_specs = [pl.BlockSpec((None, None) + g.shape[2:], lambda k, cref: (k, cref[0], 0, 0)) for g in grads]
    in_specs += [pl.BlockSpec((None,) + r.shape[1:], lambda k, cref: (k, 0, 0)) for r in received]
    out_specs = [pl.BlockSpec((None,) + r.shape[1:], lambda k, cref: (k, 0, 0)) for r in received]
    return pl.pallas_call(
        body, name="grad_add_core_partials",
        grid_spec=pltpu.PrefetchScalarGridSpec(num_scalar_prefetch=1, grid=(4,), in_specs=in_specs, out_specs=out_specs),
        out_shape=[jax.ShapeDtypeStruct(r.shape, r.dtype) for r in received],
        compiler_params=_params(("arbitrary",)),
    )(core, *grads, *received)


def _add_chip_partials(chip, partials, received):
    n = len(partials)

    def body(chip_ref, *refs):
        del chip_ref
        ps, rs, outs = refs[:n], refs[n:2 * n], refs[2 * n:]
        for a in range(n):
            outs[a][...] = ((ps[a][...].astype(F32) + rs[a][0].astype(F32)) + rs[a][1].astype(F32)) + rs[a][2].astype(F32)

    in_specs = [pl.BlockSpec((None,) + p.shape[1:], lambda i, cref: (cref[0], 0, 0)) for p in partials]
    in_specs += [pl.BlockSpec(r.shape, lambda i, cref: (0, 0, 0)) for r in received]
    out_specs = [pl.BlockSpec(p.shape[1:], lambda i, cref: (0, 0)) for p in partials]
    return pl.pallas_call(
        body, name="grad_add_chip_partials",
        grid_spec=pltpu.PrefetchScalarGridSpec(num_scalar_prefetch=1, grid=(1,), in_specs=in_specs, out_specs=out_specs),
        out_shape=[jax.ShapeDtypeStruct(p.shape[1:], F32) for p in partials],
        compiler_params=_params(("arbitrary",), vmem=VMEM_LIMIT),
    )(chip, *partials, *received)


def _in_projection(x, g_mix, w_in_t, tm):
    s = x.shape[0]

    def body(x_ref, g_ref, w_ref, h_ref, z_ref):
        xh, _ = _rms(x_ref[...])
        h = (xh * g_ref[...]).astype(BF16)
        h_ref[...] = h
        for n0 in range(0, D_IN, 512):
            z_ref[:, n0:n0 + 512] = _dot(h, w_ref[n0:n0 + 512, :], NT).astype(BF16)

    return pl.pallas_call(
        body, name="in_projection", grid=(s // tm,),
        in_specs=[pl.BlockSpec((tm, D_MODEL), lambda i: (i, 0)), _const_spec((1, D_MODEL)), _const_spec((D_IN, D_MODEL))],
        out_specs=[pl.BlockSpec((tm, D_MODEL), lambda i: (i, 0)), pl.BlockSpec((tm, D_IN), lambda i: (i, 0))],
        out_shape=[jax.ShapeDtypeStruct((s, D_MODEL), BF16), jax.ShapeDtypeStruct((s, D_IN), BF16)],
        compiler_params=_params(("parallel",), vmem=VMEM_LIMIT),
    )(x, g_mix, w_in_t)


RESIDUES = 16
CHUNK = 512
PREFETCH = 4
GROUP = 8


def _branch_geometry(d, seg):
    n_runs = RESIDUES // d
    w = BLOCK // n_runs
    blocks = seg // w
    group = min(GROUP, blocks)
    chains = min(GROUP // group, d)
    return n_runs, w, blocks // group, group, chains


def _permuted_masks():
    out = []
    for d in DILATIONS:
        n_runs = RESIDUES // d
        w = BLOCK // n_runs
        p = np.arange(BLOCK)
        pos = (p % w) * n_runs + p // w
        dist = pos[:, None] - np.concatenate([pos - BLOCK, pos])[None, :]
        band = (dist >= 0) & (dist <= BLOCK)
        first = band & (np.arange(2 * BLOCK)[None, :] >= BLOCK)
        both = [np.where(m, 0.0, NEG).astype(np.float32) for m in (band, first)]
        out.append(np.stack([np.concatenate([m, m], axis=0) for m in both]))
    return jnp.asarray(np.stack(out))


def _run_starts(u, grp, d, seg, n_runs, w, group):
    return [(u + d * q) * seg + (w * group) * grp for q in range(n_runs)]


def _tile_rows(start, w, b):
    off = start + w * b
    if b < 0:
        off = jnp.maximum(off, 0)
    return pl.ds(pl.multiple_of(off, w), w)


def _load_tile(ref, starts, w, b):
    parts = [ref[_tile_rows(st, w, b), :] for st in starts]
    return parts[0] if len(parts) == 1 else jnp.concatenate(parts, axis=0)


def _store_tile(ref, starts, w, b, tile):
    for i, st in enumerate(starts):
        ref[_tile_rows(st, w, b), :] = tile[i * w:(i + 1) * w]


def _add_tile(ref, starts, w, b, tile):
    for i, st in enumerate(starts):
        ref[_tile_rows(st, w, b), :] += tile[i * w:(i + 1) * w]


def _to_residue_major(dst, src, c, seg):
    per = CHUNK // RESIDUES
    for r in range(RESIDUES):
        dst[pl.ds(pl.multiple_of(r * seg + c * per, per), per), :] = src[pl.ds(r, per, stride=RESIDUES), :]


def _from_residue_major(dst, dst_start, src, c, seg):
    per = CHUNK // RESIDUES
    for r in range(RESIDUES):
        dst[pl.ds(dst_start + r, per, stride=RESIDUES), :] = src[pl.ds(pl.multiple_of(r * seg + c * per, per), per), :]


def _stack_heads(t, head0):
    return jnp.concatenate([jnp.where(head0, t, 0.0), jnp.where(head0, 0.0, t)], axis=0).astype(BF16)


def _attention_forward(z, gq2, gk2, masks, shards):
    s = z.shape[0]
    seg = s // RESIDUES
    n_w = len(shards)

    def body(zq_ref, zk_ref, zv_ref, gq_ref, gk_ref, mask_ref, *rest):
        shard_refs, (o_ref, lse_ref), stacks = rest[:n_w], rest[n_w:n_w + 2], rest[n_w + 2:2 * n_w + 2]
        saved = rest[2 * n_w + 2]
        qf, kf, vf, o_st, l_st, tq, tk, tv, send_sems, recv_sems, local_sems, save_sems = rest[2 * n_w + 3:]
        pair = pl.program_id(0)
        gather = _TwoLevelGather(stacks, send_sems, recv_sems)

        def own_copies():
            mine = gather.index(gather.me)
            return [pltpu.make_async_copy(shard_refs[a], stacks[a].at[mine], local_sems.at[a]) for a in range(n_w)]

        @pl.when(pair == 0)
        def _():
            for cp in own_copies():
                cp.start()
            gather.start(shard_refs)

        @pl.when(pair == 2)
        def _():
            gather.forward()

        bd = _head_blockdiag(1.0 / HEAD_DIM)
        gq = gq_ref[...] * (HEAD_DIM ** -0.5)
        gk = gk_ref[...]

        def prep(c, carry):
            rows = pl.ds(pl.multiple_of(c * CHUNK, CHUNK), CHUNK)
            zq = zq_ref[rows, :].astype(F32)
            zk = zk_ref[rows, :].astype(F32)
            tq[...] = (zq * lax.rsqrt(_head_sum(zq * zq, bd) + EPS)) * gq
            tk[...] = (zk * lax.rsqrt(_head_sum(zk * zk, bd) + EPS)) * gk
            tv[...] = zv_ref[rows, :].astype(F32)
            _to_residue_major(qf, tq, c, seg)
            _to_residue_major(kf, tk, c, seg)
            _to_residue_major(vf, tv, c, seg)
            return carry

        lax.fori_loop(0, s // CHUNK, prep, 0)

        save_copies = [pltpu.make_async_copy(src, saved.at[pair, i], save_sems.at[i]) for i, src in enumerate((qf, kf, vf))]
        for cp in save_copies:
            cp.start()

        head0 = lax.broadcasted_iota(jnp.int32, (BLOCK, 128), 1) < HEAD_DIM

        def run_branch(branch, d):
            n_runs, w, steps, group, chains = _branch_geometry(d, seg)
            slots = [(ci, b) for ci in range(chains) for b in range(group)]

            def step(u, grp):
                starts = [_run_starts(u + ci, grp, d, seg, n_runs, w, group) for ci in range(chains)]
                kt = [[_load_tile(kf, st, w, b).astype(BF16) for b in range(-1, group)] for st in starts]
                vt = [[_load_tile(vf, st, w, b).astype(BF16) for b in range(-1, group)] for st in starts]
                scores = []
                for ci, b in slots:
                    q2 = _stack_heads(_load_tile(qf, starts[ci], w, b), head0)
                    kc = jnp.concatenate([kt[ci][b], kt[ci][b + 1]], axis=0)
                    mask = mask_ref[branch, 0]
                    if b == 0:
                        mask = jnp.where(grp == 0, mask_ref[branch, 1], mask)
                    scores.append(_dot(q2, kc, NT) + mask)
                sc = jnp.concatenate(scores, axis=0)
                m = jnp.max(sc, axis=-1, keepdims=True)
                e = jnp.exp((sc - m).astype(BF16))
                ones = jnp.ones((BLOCK, 128), BF16)
                fresh = []
                for n, (ci, b) in enumerate(slots):
                    rows = slice(2 * BLOCK * n, 2 * BLOCK * (n + 1))
                    v_aug = jnp.concatenate([jnp.concatenate([vt[ci][b], ones], axis=1),
                                             jnp.concatenate([vt[ci][b + 1], ones], axis=1)], axis=0)
                    o2 = _dot(e[rows], v_aug, NN)
                    den = o2[:, 128:]
                    o2 = o2[:, :128] * (1.0 / den)
                    l2 = m[rows] + jnp.log(den)
                    fresh.append((jnp.where(head0, o2[:BLOCK], o2[BLOCK:]), jnp.where(head0, l2[:BLOCK], l2[BLOCK:])))
                for n, (ci, b) in enumerate(slots):
                    o_new, l_new = fresh[n]
                    if branch > 0:
                        l_old = _load_tile(l_st, starts[ci], w, b)
                        diff = l_new - l_old
                        t = jnp.exp(-jnp.abs(diff))
                        big = 1.0 / (1.0 + t)
                        small = t * big
                        newer = diff >= 0.0
                        o_old = _load_tile(o_st, starts[ci], w, b)
                        o_new = o_old * jnp.where(newer, small, big) + o_new * jnp.where(newer, big, small)
                        l_new = jnp.maximum(l_old, l_new) + jnp.log(1.0 + t)
                    _store_tile(o_st, starts[ci], w, b, o_new)
                    _store_tile(l_st, starts[ci], w, b, l_new)

            def unit(it, carry):
                def inner(grp, carry2):
                    step(it * chains, grp)
                    return carry2
                return lax.fori_loop(0, steps, inner, carry)

            lax.fori_loop(0, d // chains, unit, 0)

        for branch, d in enumerate(DILATIONS):
            run_branch(branch, d)

        save_copies += [pltpu.make_async_copy(src, saved.at[pair, 3 + i], save_sems.at[3 + i])
                        for i, src in enumerate((o_st, l_st))]
        for cp in save_copies[3:]:
            cp.start()

        def finish(c, carry):
            _from_residue_major(o_ref, c * CHUNK, o_st, c, seg)
            _from_residue_major(lse_ref, c * CHUNK, l_st, c, seg)
            return carry

        lax.fori_loop(0, s // CHUNK, finish, 0)
        for cp in save_copies:
            cp.wait()

        @pl.when(pair == N_PAIRS - 1)
        def _():
            gather.finish()
            for cp in own_copies():
                cp.wait()

    def col(c0):
        return pl.BlockSpec((s, 128), lambda p: (0, c0 + p), pipeline_mode=pl.Buffered(1))

    hbm = pl.BlockSpec(memory_space=pl.ANY)
    out = pl.pallas_call(
        body, name="attention_forward", grid=(N_PAIRS,),
        in_specs=[col(12), col(16), col(20), _const_spec((1, 128)), _const_spec((1, 128)), _const_spec(masks.shape)]
        + [hbm] * n_w,
        out_specs=[pl.BlockSpec((s, 128), lambda p: (0, p)), pl.BlockSpec((s, 128), lambda p: (0, p))] + [hbm] * (n_w + 1),
        out_shape=[jax.ShapeDtypeStruct((s, D_ATTN), F32), jax.ShapeDtypeStruct((s, D_ATTN), F32)]
        + [jax.ShapeDtypeStruct((N_DEV,) + w.shape, w.dtype) for w in shards]
        + [jax.ShapeDtypeStruct((N_PAIRS, 5, s, 128), F32)],
        scratch_shapes=[pltpu.VMEM((s, 128), F32)] * 5 + [pltpu.VMEM((CHUNK, 128), F32)] * 3
        + [pltpu.SemaphoreType.DMA((7 * n_w,)), pltpu.SemaphoreType.DMA((7 * n_w,)), pltpu.SemaphoreType.DMA((n_w,)),
           pltpu.SemaphoreType.DMA((5,))],
        compiler_params=_params(("arbitrary",), vmem=VMEM_LIMIT),
    )(z, z, z, gq2, gk2, masks, *shards)
    return out[0], out[1], out[2:2 + n_w], out[2 + n_w]


def _conv_forward(zt, zprev, w, first):
    u, gb, gc = zt[:, :D_CONV], zt[:, D_CONV:2 * D_CONV], zt[:, 2 * D_CONV:]
    cu = gc * u
    cu_prev = jnp.where(first, 0.0, zprev[:, 2 * D_CONV:] * zprev[:, :D_CONV])
    row = lax.broadcasted_iota(jnp.int32, cu.shape, 0)
    cu1 = jnp.where(row >= 1, pltpu.roll(cu, 1, 0), cu_prev[7:8, :])
    cu2 = jnp.where(row >= 2, pltpu.roll(cu, 2, 0), jnp.where(row == 1, cu_prev[7:8, :], cu_prev[6:7, :]))
    cv = w[0:1, :] * cu2 + w[1:2, :] * cu1 + w[2:3, :] * cu
    return u, gb, gc, cu, cu1, cu2, cv


def _mix_out(z, y_attn, x, conv_w, g_conv_out, g_attn_out, g_ffn, w_out, tm):
    s = x.shape[0]
    hb = tm // 8

    def body(z_ref, zp_ref, ya_ref, x_ref, cw_ref, gc_ref, ga_ref, gf_ref, w_ref, mix_ref, x2_ref, h2_ref):
        first = pl.program_id(0) == 0
        zt = z_ref[...].astype(F32)
        _, gb, _, _, _, _, cv = _conv_forward(zt, zp_ref[...].astype(F32), cw_ref[...], first)
        nc, _ = _rms(gb * cv)
        na, _ = _rms(ya_ref[...])
        mix = jnp.concatenate([nc * gc_ref[...], na * ga_ref[...]], axis=-1).astype(BF16)
        mix_ref[...] = mix
        x2 = x_ref[...] + _dot(mix, w_ref[...], NN)
        x2_ref[...] = x2
        xh, _ = _rms(x2)
        h2_ref[...] = (xh * gf_ref[...]).astype(BF16)

    tile = lambda w: pl.BlockSpec((tm, w), lambda i: (i, 0))
    return pl.pallas_call(
        body, name="mix_out", grid=(s // tm,),
        in_specs=[tile(3 * D_CONV), pl.BlockSpec((8, 3 * D_CONV), lambda i: (jnp.maximum(i * hb - 1, 0), 0)),
                  tile(D_ATTN), tile(D_MODEL), _const_spec((8, D_CONV)), _const_spec((1, D_CONV)), _const_spec((1, D_ATTN)),
                  _const_spec((1, D_MODEL)), _const_spec((D_MODEL, D_MODEL))],
        out_specs=[tile(D_MODEL)] * 3,
        out_shape=[jax.ShapeDtypeStruct((s, D_MODEL), BF16), jax.ShapeDtypeStruct((s, D_MODEL), F32),
                   jax.ShapeDtypeStruct((s, D_MODEL), BF16)],
        compiler_params=_params(("arbitrary",), vmem=VMEM_LIMIT),
    )(z, z, y_attn, x, conv_w, g_conv_out, g_attn_out, g_ffn, w_out)


FF_CHUNK = 256


def _ffn_forward(h2, x2, target, w_gate_t, w_up_t, w_down, tm):
    s = h2.shape[0]

    def body(h_ref, x2_ref, t_ref, wg_ref, wu_ref, wd_ref, a_ref, b_ref, dy_ref, dyb_ref, loss_ref, f_buf):
        @pl.when(pl.program_id(0) == 0)
        def _():
            loss_ref[...] = jnp.zeros_like(loss_ref)

        h = h_ref[...]
        for c0 in range(0, D_FF, FF_CHUNK):
            rows = slice(c0, c0 + FF_CHUNK)
            a = _dot(h, wg_ref[rows, :], NT)
            b = _dot(h, wu_ref[rows, :], NT)
            a_ref[:, rows] = a.astype(BF16)
            b_ref[:, rows] = b.astype(BF16)
            f_buf[:, rows] = ((a * jax.nn.sigmoid(a)) * b).astype(BF16)
        y = x2_ref[...] + _dot(f_buf[...], wd_ref[...], NN)
        err = y - t_ref[...]
        dy = err * (1.0 / D_MODEL)
        dy_ref[...] = dy
        dyb_ref[...] = dy.astype(BF16)
        loss_ref[...] += jnp.sum(err * err)

    tile = lambda w: pl.BlockSpec((tm, w), lambda i: (i, 0))
    return pl.pallas_call(
        body, name="ffn_forward", grid=(s // tm,),
        in_specs=[tile(D_MODEL), tile(D_MODEL), tile(D_MODEL)] + [_const_spec((D_FF, D_MODEL))] * 3,
        out_specs=[tile(D_FF), tile(D_FF), tile(D_MODEL), tile(D_MODEL), pl.BlockSpec((8, 128), lambda i: (0, 0))],
        out_shape=[jax.ShapeDtypeStruct((s, D_FF), BF16), jax.ShapeDtypeStruct((s, D_FF), BF16),
                   jax.ShapeDtypeStruct((s, D_MODEL), F32), jax.ShapeDtypeStruct((s, D_MODEL), BF16),
                   jax.ShapeDtypeStruct((8, 128), F32)],
        scratch_shapes=[pltpu.VMEM((tm, D_FF), BF16)],
        compiler_params=_params(("arbitrary",), vmem=VMEM_LIMIT),
    )(h2, x2, target, w_gate_t, w_up_t, w_down)


def _ffn_backward(dy, a, b, x2, g_ffn, w_gate_t, w_up_t, w_down, tm):
    s = dy.shape[0]
    half = D_FF // 2
    pieces = [(p0, min(FF_CHUNK, half - p0)) for p0 in range(0, half, FF_CHUNK)]

    def body(dy_ref, a_ref, b_ref, x2_ref, g_ref, wg_ref, wu_ref, wd_ref, da_ref, db_ref, dx2_ref, dx2b_ref, dg_ref, acc):
        i, c = pl.program_id(0), pl.program_id(1)

        @pl.when((i == 0) & (c == 0))
        def _():
            dg_ref[...] = jnp.zeros_like(dg_ref)

        dyb = dy_ref[...].astype(BF16)
        for p0, width in pieces:
            rows = slice(p0, p0 + width)
            df = _dot(dyb, wd_ref[rows, :], NT)
            av = a_ref[:, rows].astype(F32)
            bv = b_ref[:, rows].astype(F32)
            sig = jax.nn.sigmoid(av)
            silu = av * sig
            da_ref[:, rows] = (df * bv * (sig * (1.0 + av * (1.0 - sig)))).astype(BF16)
            db_ref[:, rows] = (df * silu).astype(BF16)
        dh = _dot(da_ref[...], wg_ref[...], NN) + _dot(db_ref[...], wu_ref[...], NN)

        @pl.when(c == 0)
        def _():
            acc[...] = dh

        @pl.when(c == 1)
        def _():
            dh_all = acc[...] + dh
            xh, r = _rms(x2_ref[...])
            dg_ref[...] += jnp.sum(dh_all * xh, axis=0, keepdims=True)
            dx2 = dy_ref[...] + _rms_bwd(dh_all, xh, r, g_ref[...])
            dx2_ref[...] = dx2
            dx2b_ref[...] = dx2.astype(BF16)

    tile = lambda w: pl.BlockSpec((tm, w), lambda i, c: (i, 0))
    part = pl.BlockSpec((tm, half), lambda i, c: (i, c))
    weight = pl.BlockSpec((half, D_MODEL), lambda i, c: (c, 0))
    return pl.pallas_call(
        body, name="ffn_backward", grid=(s // tm, 2),
        in_specs=[tile(D_MODEL), part, part, tile(D_MODEL), _const_spec((1, D_MODEL)), weight, weight, weight],
        out_specs=[part, part, tile(D_MODEL), tile(D_MODEL), pl.BlockSpec((1, D_MODEL), lambda i, c: (0, 0))],
        out_shape=[jax.ShapeDtypeStruct((s, D_FF), BF16)] * 2
        + [jax.ShapeDtypeStruct((s, D_MODEL), F32), jax.ShapeDtypeStruct((s, D_MODEL), BF16),
           jax.ShapeDtypeStruct((1, D_MODEL), F32)],
        scratch_shapes=[pltpu.VMEM((tm, D_MODEL), F32)],
        compiler_params=_params(("arbitrary", "arbitrary"), vmem=VMEM_LIMIT),
    )(dy, a, b, x2, g_ffn, w_gate_t, w_up_t, w_down)


def _out_backward(dx2, z, y_attn, conv_w, g_conv_out, g_attn_out, w_out, tm):
    s = dx2.shape[0]
    hb = tm // 8

    def body(dx2_ref, z_ref, zp_ref, ya_ref, cw_ref, gc_ref, ga_ref, w_ref, dya_ref, dgb_ref, dcv_ref, dgc_ref, dga_ref):
        first = pl.program_id(0) == 0

        @pl.when(first)
        def _():
            dgc_ref[...] = jnp.zeros_like(dgc_ref)
            dga_ref[...] = jnp.zeros_like(dga_ref)

        dmix = _dot(dx2_ref[...], w_ref[...], NT)
        _, gb, _, _, _, _, cv = _conv_forward(z_ref[...].astype(F32), zp_ref[...].astype(F32), cw_ref[...], first)
        ych, rc = _rms(gb * cv)
        dnc = dmix[:, :D_CONV]
        dgc_ref[...] += jnp.sum(dnc * ych, axis=0, keepdims=True)
        dyc = _rms_bwd(dnc, ych, rc, gc_ref[...])
        dgb_ref[...] = (dyc * cv).astype(BF16)
        dcv_ref[...] = dyc * gb
        yah, ra = _rms(ya_ref[...])
        dna = dmix[:, D_CONV:]
        dga_ref[...] += jnp.sum(dna * yah, axis=0, keepdims=True)
        dya_ref[...] = _rms_bwd(dna, yah, ra, ga_ref[...])

    tile = lambda w: pl.BlockSpec((tm, w), lambda i: (i, 0))
    vec = pl.BlockSpec((1, D_CONV), lambda i: (0, 0))
    return pl.pallas_call(
        body, name="out_backward", grid=(s // tm,),
        in_specs=[tile(D_MODEL), tile(3 * D_CONV), pl.BlockSpec((8, 3 * D_CONV), lambda i: (jnp.maximum(i * hb - 1, 0), 0)),
                  tile(D_ATTN), _const_spec((8, D_CONV)), _const_spec((1, D_CONV)), _const_spec((1, D_ATTN)),
                  _const_spec((D_MODEL, D_MODEL))],
        out_specs=[tile(D_ATTN), tile(D_CONV), tile(D_CONV), vec, vec],
        out_shape=[jax.ShapeDtypeStruct((s, D_ATTN), F32), jax.ShapeDtypeStruct((s, D_CONV), BF16),
                   jax.ShapeDtypeStruct((s, D_CONV), F32), jax.ShapeDtypeStruct((1, D_CONV), F32),
                   jax.ShapeDtypeStruct((1, D_ATTN), F32)],
        compiler_params=_params(("arbitrary",), vmem=VMEM_LIMIT),
    )(dx2, z, z, y_attn, conv_w, g_conv_out, g_attn_out, w_out)


def _attention_backward(z, o, lse, dya, gq2, gk2, masks, saved, grad_stacks):
    s = z.shape[0]
    seg = s // RESIDUES
    n_g = len(grad_stacks)

    def body(z_hbm, o_hbm, lse_hbm, do_hbm, gq_ref, gk_ref, mask_ref, saved_hbm, *rest):
        stack_refs = rest[:n_g]
        dzq_ref, dzk_ref, dzv_ref, dgq_ref, dgk_ref = rest[n_g:n_g + 5]
        landing = rest[n_g + 5:2 * n_g + 5]
        (qf, kf, vf, dof, stats, dq, dk, dv, t0, t1, t2, zbuf, fbuf, sems, send_sems, recv_sems,
         load_sems) = rest[2 * n_g + 5:]
        pair = pl.program_id(0)
        scatter = _DirectReduceScatter(stack_refs, landing, send_sems, recv_sems)

        @pl.when(pair == 0)
        def _():
            scatter.start()

        saved_copies = [pltpu.make_async_copy(saved_hbm.at[pair, i], dst, load_sems.at[i])
                        for i, dst in enumerate((qf, kf, vf, dq, dk))]
        for cp in saved_copies:
            cp.start()

        n_chunks = s // CHUNK
        bd_mean = _head_blockdiag(1.0 / HEAD_DIM)
        bd_sum = _head_blockdiag(1.0)
        gq = gq_ref[...] * (HEAD_DIM ** -0.5)
        gk = gk_ref[...]
        lane = lax.broadcasted_iota(jnp.int32, (CHUNK, 128), 1) % HEAD_DIM

        def chunk_copies(c, slot, n_z, n_f):
            rows = pl.ds(pl.multiple_of(c * CHUNK, CHUNK), CHUNK)
            cps = []
            for i, col in enumerate((12, 16, 20)[:n_z]):
                cols = pl.ds(pl.multiple_of((col + pair) * 128, 128), 128)
                cps.append(pltpu.make_async_copy(z_hbm.at[rows, cols], zbuf.at[slot, i], sems.at[slot, i]))
            for i, src in enumerate((do_hbm, o_hbm, lse_hbm)[:n_f]):
                cols = pl.ds(pl.multiple_of(pair * 128, 128), 128)
                cps.append(pltpu.make_async_copy(src.at[rows, cols], fbuf.at[slot, i], sems.at[slot, 3 + i]))
            return cps

        def prime(n_z, n_f):
            for c0 in range(min(PREFETCH - 1, n_chunks)):
                for cp in chunk_copies(c0, c0, n_z, n_f):
                    cp.start()

        def prefetched(c, n_z, n_f):
            slot = c % PREFETCH
            ahead = c + PREFETCH - 1

            @pl.when(ahead < n_chunks)
            def _():
                for cp in chunk_copies(ahead, ahead % PREFETCH, n_z, n_f):
                    cp.start()

            for cp in chunk_copies(c, slot, n_z, n_f):
                cp.wait()
            return slot

        prime(0, 1)

        def prep(c, carry):
            slot = prefetched(c, 0, 1)
            _to_residue_major(dof, fbuf.at[slot, 0], c, seg)
            return carry

        lax.fori_loop(0, n_chunks, prep, 0)
        for cp in saved_copies:
            cp.wait()

        def pack(c, carry):
            rows = pl.ds(pl.multiple_of(c * CHUNK, CHUNK), CHUNK)
            delta = _head_sum(dof[rows, :] * dq[rows, :], bd_sum)
            stats[rows, :] = jnp.where(lane == 0, dk[rows, :], jnp.where(lane == 1, delta, 0.0))
            zero = jnp.zeros((CHUNK, 128), F32)
            dq[rows, :] = zero
            dk[rows, :] = zero
            dv[rows, :] = zero
            return carry

        lax.fori_loop(0, n_chunks, pack, 0)

        head0 = lax.broadcasted_iota(jnp.int32, (BLOCK, 128), 1) < HEAD_DIM

        def run_branch(branch, d):
            n_runs, w, steps, group, chains = _branch_geometry(d, seg)
            slots = [(ci, b) for ci in range(chains) for b in range(group)]

            def step(u, grp):
                starts = [_run_starts(u + ci, grp, d, seg, n_runs, w, group) for ci in range(chains)]
                kt = [[_load_tile(kf, st, w, b).astype(BF16) for b in range(-1, group)] for st in starts]
                vt = [[_load_tile(vf, st, w, b).astype(BF16) for b in range(-1, group)] for st in starts]
                dk_t = [[jnp.zeros((BLOCK, 128), F32)] * (group + 1) for _ in range(chains)]
                dv_t = [[jnp.zeros((BLOCK, 128), F32)] * (group + 1) for _ in range(chains)]
                dq_t = []
                q2s, do2s, kcs, scores, dps, lses, deltas = [], [], [], [], [], [], []
                for ci, b in slots:
                    q2 = _stack_heads(_load_tile(qf, starts[ci], w, b), head0)
                    do2 = _stack_heads(_load_tile(dof, starts[ci], w, b), head0)
                    st = _load_tile(stats, starts[ci], w, b)
                    lses += [st[:, 0:1], st[:, HEAD_DIM:HEAD_DIM + 1]]
                    deltas += [st[:, 1:2], st[:, HEAD_DIM + 1:HEAD_DIM + 2]]
                    kc = jnp.concatenate([kt[ci][b], kt[ci][b + 1]], axis=0)
                    vc = jnp.concatenate([vt[ci][b], vt[ci][b + 1]], axis=0)
                    mask = mask_ref[branch, 0]
                    if b == 0:
                        mask = jnp.where(grp == 0, mask_ref[branch, 1], mask)
                    scores.append(_dot(q2, kc, NT) + mask)
                    dps.append(_dot(do2, vc, NT))
                    q2s.append(q2)
                    do2s.append(do2)
                    kcs.append(kc)
                p_all = jnp.exp(jnp.concatenate(scores, axis=0) - jnp.concatenate(lses, axis=0))
                ds_all = (p_all * (jnp.concatenate(dps, axis=0) - jnp.concatenate(deltas, axis=0))).astype(BF16)
                p_all = p_all.astype(BF16)
                for n, (ci, b) in enumerate(slots):
                    rows = slice(2 * BLOCK * n, 2 * BLOCK * (n + 1))
                    ds, q2, do2, kc = ds_all[rows], q2s[n], do2s[n], kcs[n]
                    dq2 = _dot(ds, kc, NN)
                    dq_t.append(jnp.where(head0, dq2[:BLOCK], dq2[BLOCK:]))
                    dkc = _dot(ds, q2, TN)
                    dvc = _dot(p_all[rows], do2, TN)
                    dk_t[ci][b] = dk_t[ci][b] + dkc[:BLOCK]
                    dk_t[ci][b + 1] = dk_t[ci][b + 1] + dkc[BLOCK:]
                    dv_t[ci][b] = dv_t[ci][b] + dvc[:BLOCK]
                    dv_t[ci][b + 1] = dv_t[ci][b + 1] + dvc[BLOCK:]
                for n, (ci, b) in enumerate(slots):
                    _add_tile(dq, starts[ci], w, b, dq_t[n])
                for ci in range(chains):
                    for b in range(-1, group):
                        _add_tile(dk, starts[ci], w, b, dk_t[ci][b + 1])
                        _add_tile(dv, starts[ci], w, b, dv_t[ci][b + 1])

            def unit(it, carry):
                def inner(grp, carry2):
                    step(it * chains, grp)
                    return carry2
                return lax.fori_loop(0, steps, inner, carry)

            lax.fori_loop(0, d // chains, unit, 0)

        for branch, d in enumerate(DILATIONS):
            run_branch(branch, d)

        prime(2, 0)

        def finish(c, carry):
            acc_q, acc_k = carry
            rows = pl.ds(pl.multiple_of(c * CHUNK, CHUNK), CHUNK)
            _from_residue_major(t0, 0, dq, c, seg)
            _from_residue_major(t1, 0, dk, c, seg)
            _from_residue_major(t2, 0, dv, c, seg)
            slot = prefetched(c, 2, 0)
            zq = zbuf[slot, 0].astype(F32)
            rq = lax.rsqrt(_head_sum(zq * zq, bd_mean) + EPS)
            qh = zq * rq
            dqn = t0[...]
            acc_q = acc_q + jnp.sum(dqn * qh, axis=0, keepdims=True)
            t = dqn * gq
            dzq_ref[rows, :] = (rq * (t - qh * _head_sum(t * qh, bd_mean))).astype(BF16)
            zk = zbuf[slot, 1].astype(F32)
            rk = lax.rsqrt(_head_sum(zk * zk, bd_mean) + EPS)
            kh = zk * rk
            dkn = t1[...]
            acc_k = acc_k + jnp.sum(dkn * kh, axis=0, keepdims=True)
            t = dkn * gk
            dzk_ref[rows, :] = (rk * (t - kh * _head_sum(t * kh, bd_mean))).astype(BF16)
            dzv_ref[rows, :] = t2[...].astype(BF16)
            return acc_q, acc_k

        zero = jnp.zeros((1, 128), F32)
        acc_q, acc_k = lax.fori_loop(0, s // CHUNK, finish, (zero, zero))
        dgq_ref[...] = acc_q * (HEAD_DIM ** -0.5)
        dgk_ref[...] = acc_k

        @pl.when(pair == N_PAIRS - 1)
        def _():
            scatter.finish()

    hbm = pl.BlockSpec(memory_space=pl.ANY)
    gain = pl.BlockSpec((None, 1, 128), lambda p: (p, 0, 0))
    dz_spec = pl.BlockSpec((s, 128), lambda p: (0, p))
    out = pl.pallas_call(
        body, name="attention_backward", grid=(N_PAIRS,),
        in_specs=[hbm, hbm, hbm, hbm, _const_spec((1, 128)), _const_spec((1, 128)), _const_spec(masks.shape), hbm]
        + [hbm] * n_g,
        out_specs=[dz_spec, dz_spec, dz_spec, gain, gain] + [hbm] * n_g,
        out_shape=[jax.ShapeDtypeStruct((s, D_ATTN), BF16)] * 3 + [jax.ShapeDtypeStruct((N_PAIRS, 1, 128), F32)] * 2
        + [jax.ShapeDtypeStruct((N_DEV - 1,) + g.shape[1:], g.dtype) for g in grad_stacks],
        scratch_shapes=[pltpu.VMEM((s, 128), F32)] * 8 + [pltpu.VMEM((CHUNK, 128), F32)] * 3
        + [pltpu.VMEM((PREFETCH, 3, CHUNK, 128), BF16), pltpu.VMEM((PREFETCH, 3, CHUNK, 128), F32),
           pltpu.SemaphoreType.DMA((PREFETCH, 6)),
           pltpu.SemaphoreType.DMA((7 * n_g,)), pltpu.SemaphoreType.DMA((7 * n_g,)), pltpu.SemaphoreType.DMA((5,))],
        compiler_params=_params(("arbitrary",), vmem=VMEM_LIMIT),
    )(z, o, lse, dya, gq2, gk2, masks, saved, *grad_stacks)
    return out[:5], out[5:]


def _in_backward(dcv, dgb, dzq, dzk, dzv, z, x, dx2, g_mix, conv_w, w_in_t, tm):
    s = x.shape[0]
    hb = tm // 8
    last_halo = s // 8 - 1

    def body(dcv_ref, dcn_ref, dgb_ref, dzq_ref, dzk_ref, dzv_ref, z_ref, zp_ref, x_ref, dx2_ref, g_ref, cw_ref, w_ref,
             dz_ref, dx_ref, dg_ref, dcw_ref):
        i = pl.program_id(0)
        first = i == 0
        last = i == pl.num_programs(0) - 1

        @pl.when(first)
        def _():
            dg_ref[...] = jnp.zeros_like(dg_ref)
            dcw_ref[...] = jnp.zeros_like(dcw_ref)

        w = cw_ref[...]
        u, _, gc, cu, cu1, cu2, _ = _conv_forward(z_ref[...].astype(F32), zp_ref[...].astype(F32), w, first)
        dcv_t = dcv_ref[...]
        nxt = jnp.where(last, 0.0, dcn_ref[...])
        row = lax.broadcasted_iota(jnp.int32, dcv_t.shape, 0)
        up1 = jnp.where(row < tm - 1, pltpu.roll(dcv_t, tm - 1, 0), nxt[0:1, :])
        up2 = jnp.where(row < tm - 2, pltpu.roll(dcv_t, tm - 2, 0), jnp.where(row == tm - 2, nxt[0:1, :], nxt[1:2, :]))
        dcu = w[2:3, :] * dcv_t + w[1:2, :] * up1 + w[0:1, :] * up2
        dcw = jnp.concatenate([jnp.sum(dcv_t * cu2, axis=0, keepdims=True), jnp.sum(dcv_t * cu1, axis=0, keepdims=True),
                               jnp.sum(dcv_t * cu, axis=0, keepdims=True), jnp.zeros((5, D_CONV), F32)], axis=0)
        dcw_ref[...] += dcw
        dz_ref[:, :D_CONV] = (dcu * gc).astype(BF16)
        dz_ref[:, D_CONV:2 * D_CONV] = dgb_ref[...]
        dz_ref[:, 2 * D_CONV:3 * D_CONV] = (dcu * u).astype(BF16)
        dz_ref[:, 3 * D_CONV:3 * D_CONV + D_ATTN] = dzq_ref[...]
        dz_ref[:, 3 * D_CONV + D_ATTN:3 * D_CONV + 2 * D_ATTN] = dzk_ref[...]
        dz_ref[:, 3 * D_CONV + 2 * D_ATTN:] = dzv_ref[...]
        dh = _dot(dz_ref[...], w_ref[...], NN)
        xh, r = _rms(x_ref[...])
        dg_ref[...] += jnp.sum(dh * xh, axis=0, keepdims=True)
        dx_ref[...] = dx2_ref[...] + _rms_bwd(dh, xh, r, g_ref[...])

    tile = lambda w: pl.BlockSpec((tm, w), lambda i: (i, 0))
    return pl.pallas_call(
        body, name="in_backward", grid=(s // tm,),
        in_specs=[tile(D_CONV), pl.BlockSpec((8, D_CONV), lambda i: (jnp.minimum((i + 1) * hb, last_halo), 0)),
                  tile(D_CONV), tile(D_ATTN), tile(D_ATTN), tile(D_ATTN),
                  tile(3 * D_CONV), pl.BlockSpec((8, 3 * D_CONV), lambda i: (jnp.maximum(i * hb - 1, 0), 0)),
                  tile(D_MODEL), tile(D_MODEL), _const_spec((1, D_MODEL)), _const_spec((8, D_CONV)),
                  _const_spec((D_IN, D_MODEL))],
        out_specs=[tile(D_IN), tile(D_MODEL), pl.BlockSpec((1, D_MODEL), lambda i: (0, 0)),
                   pl.BlockSpec((8, D_CONV), lambda i: (0, 0))],
        out_shape=[jax.ShapeDtypeStruct((s, D_IN), BF16), jax.ShapeDtypeStruct((s, D_MODEL), F32),
                   jax.ShapeDtypeStruct((1, D_MODEL), F32), jax.ShapeDtypeStruct((8, D_CONV), F32)],
        compiler_params=_params(("arbitrary",), vmem=VMEM_LIMIT),
    )(dcv, dcv, dgb, dzq, dzk, dzv, z, z, x, dx2, g_mix, conv_w, w_in_t)


def _weight_grad(name, a, b, tn, tk, gate=None):
    s, n = a.shape
    steps = s // tk

    def body(*refs):
        gate_ref = refs[0] if gate is not None else None
        a_ref, b_ref, out_ref, acc = refs[1:] if gate is not None else refs
        k = pl.program_id(1)

        @pl.when(k == 0)
        def _():
            acc[...] = jnp.zeros_like(acc)

        lhs = a_ref[...]
        if gate is not None:
            gv = gate_ref[...].astype(F32)
            lhs = ((gv * jax.nn.sigmoid(gv)) * lhs.astype(F32)).astype(BF16)
        acc[...] += _dot(lhs, b_ref[...].astype(BF16), TN)

        @pl.when(k == steps - 1)
        def _():
            out_ref[...] = acc[...].astype(BF16)

    lhs_spec = pl.BlockSpec((tk, tn), lambda j, k: (k, j))
    return pl.pallas_call(
        body, name=name, grid=(n // tn, steps),
        in_specs=([lhs_spec] if gate is not None else []) + [lhs_spec, pl.BlockSpec((tk, D_MODEL), lambda j, k: (k, 0))],
        out_specs=pl.BlockSpec((tn, D_MODEL), lambda j, k: (j, 0)),
        out_shape=jax.ShapeDtypeStruct((n, D_MODEL), BF16),
        scratch_shapes=[pltpu.VMEM((tn, D_MODEL), F32)],
        compiler_params=_params(("parallel", "arbitrary"), vmem=VMEM_LIMIT),
    )(*([gate] if gate is not None else []), a, b)


def _adamw_math(w, g, m, v):
    nm = ADAM_B1 * m + (1.0 - ADAM_B1) * g
    nv = ADAM_B2 * v + (1.0 - ADAM_B2) * (g * g)
    m_hat = nm / (1.0 - ADAM_B1 ** ADAM_STEP)
    v_hat = nv / (1.0 - ADAM_B2 ** ADAM_STEP)
    return -ADAM_LR * (m_hat / (jnp.sqrt(v_hat) + ADAM_EPS) + ADAM_WD * w), nm, nv


def _adamw_received(name, device, w, stack, received, m, v):
    def body(dev_ref, w_ref, own_ref, recv_ref, m_ref, v_ref, g_ref, d_ref, nm_ref, nv_ref):
        del dev_ref
        g = own_ref[...].astype(F32)
        for k in range(N_DEV - 1):
            g = g + recv_ref[k].astype(F32)
        g_ref[...] = g
        d_ref[...], nm_ref[...], nv_ref[...] = _adamw_math(w_ref[...], g, m_ref[...], v_ref[...])

    rows = stack.shape[1]
    tr = rows // 4 if rows % 64 == 0 else rows // 2
    tile = pl.BlockSpec((tr, D_MODEL), lambda i, dref: (i, 0))
    return pl.pallas_call(
        body, name=name,
        grid_spec=pltpu.PrefetchScalarGridSpec(
            num_scalar_prefetch=1, grid=(rows // tr,),
            in_specs=[tile, pl.BlockSpec((None, tr, D_MODEL), lambda i, dref: (dref[0], i, 0)),
                      pl.BlockSpec((N_DEV - 1, tr, D_MODEL), lambda i, dref: (0, i, 0)), tile, tile],
            out_specs=[tile] * 4),
        out_shape=[jax.ShapeDtypeStruct((rows, D_MODEL), F32)] * 4,
        compiler_params=_params(("arbitrary",), vmem=VMEM_LIMIT),
    )(device, w, stack, received, m, v)


def _adamw(name, w, g, m, v):
    def body(w_ref, g_ref, m_ref, v_ref, d_ref, nm_ref, nv_ref):
        d_ref[...], nm_ref[...], nv_ref[...] = _adamw_math(w_ref[...], g_ref[...], m_ref[...], v_ref[...])

    rows = w.shape[0]
    tr = 256 if rows % 256 == 0 else rows
    spec = pl.BlockSpec((tr, w.shape[1]), lambda i: (i, 0))
    return pl.pallas_call(
        body, name=name, grid=(rows // tr,),
        in_specs=[spec] * 4, out_specs=[spec] * 3,
        out_shape=[jax.ShapeDtypeStruct(w.shape, F32)] * 3,
        compiler_params=_params(("parallel",)),
    )(w, g, m, v)


def kernel(x, g_mix, w_in, conv_w, g_q, g_k, g_conv_out, g_attn_out, w_out, g_ffn, w_gate, w_up, w_down, loss_target, m_g_mix, m_w_in, m_conv_w, m_g_q, m_g_k, m_g_conv_out, m_g_attn_out, m_w_out, m_g_ffn, m_w_gate, m_w_up, m_w_down, v_g_mix, v_w_in, v_conv_w, v_g_q, v_g_k, v_g_conv_out, v_g_attn_out, v_w_out, v_g_ffn, v_w_gate, v_w_up, v_w_down):
    s = x.shape[1]
    tm = min(512, s)
    xs, target = x[0], loss_target[0]
    px, py, pc = lax.axis_index("x"), lax.axis_index("y"), lax.axis_index("c")
    device = 4 * px + 2 * py + pc

    conv_block = jnp.zeros((8, 128), F32).at[:3, :HEAD_DIM].set(conv_w[0])
    first, shards = [w_in[0].T, conv_block], [w_out[0], w_gate[0].T, w_up[0].T, w_down[0]]
    w_in_g, conv_g, *shards = _gather_first_weights(first, [BF16, F32], shards)
    w_in_t = w_in_g.reshape(D_IN, D_MODEL)
    conv_full = jnp.transpose(conv_g[:, :3, :HEAD_DIM], (1, 0, 2)).reshape(3, D_CONV)
    conv_full = jnp.concatenate([conv_full, jnp.zeros((5, D_CONV), F32)], axis=0)
    gq2 = jnp.concatenate([g_q, g_q], axis=-1)
    gk2 = jnp.concatenate([g_k, g_k], axis=-1)

    h1, z = _in_projection(xs, g_mix, w_in_t, tm)
    masks = _permuted_masks()
    y_attn, lse, gathered, qkv_saved = _attention_forward(z, gq2, gk2, masks, shards)
    w_out_f = gathered[0].reshape(D_MODEL, D_MODEL)
    w_gate_t, w_up_t, w_down_f = [g.reshape(D_FF, D_MODEL) for g in gathered[1:]]
    mix, x2, h2 = _mix_out(z, y_attn, xs, conv_full, g_conv_out, g_attn_out, g_ffn, w_out_f, tm)
    a, b, dy, dy_bf, sq_err = _ffn_forward(h2, x2, target, w_gate_t, w_up_t, w_down_f, tm)

    tk = min(2048, s)
    da, db, dx2, dx2_bf, dg_ffn = _ffn_backward(dy, a, b, x2, g_ffn, w_gate_t, w_up_t, w_down_f, tm)
    dya, dgb, dcv, dg_conv_out, dg_attn_out = _out_backward(dx2_bf, z, y_attn, conv_full, g_conv_out, g_attn_out, w_out_f, tm)
    early = [
        _weight_grad("grad_w_out", mix, dx2_bf, D_MODEL, tk),
        _weight_grad("grad_w_gate", da, h2, D_FF // 2, tk),
        _weight_grad("grad_w_up", db, h2, D_FF // 2, tk),
        _weight_grad("grad_w_down", b, dy_bf, D_FF // 2, tk, gate=a),
    ]
    early = [g.reshape(N_DEV, g.shape[0] // N_DEV, D_MODEL) for g in early]
    (dzq, dzk, dzv, dgq_pairs, dgk_pairs), landed = _attention_backward(z, y_attn, lse, dya, gq2, gk2, masks, qkv_saved, early)
    dz, grad_x, dg_mix, dconv = _in_backward(dcv, dgb, dzq, dzk, dzv, z, xs, dx2, g_mix, conv_full, w_in_t, tm)
    dev = jnp.reshape(device, (1,)).astype(jnp.int32)
    scattered = dict(zip(("w_out", "w_gate", "w_up", "w_down"), zip(early, landed)))

    late = [_weight_grad("grad_w_in", dz, h1, D_IN // 2, tk).reshape(4, 2, D_IN // N_DEV, D_MODEL)]
    from_sibling = _reduce_scatter_cores(late)
    core = jnp.reshape(pc, (1,)).astype(jnp.int32)
    partials = _add_core_partials(core, late, from_sibling)
    from_chips = _reduce_scatter_chips(partials)
    chip = jnp.reshape(2 * px + py, (1,)).astype(jnp.int32)
    (gw_in_t,) = _add_chip_partials(chip, partials, from_chips)
    gw_in = gw_in_t.T

    dg_q = jnp.sum(dgq_pairs.reshape(2 * N_PAIRS, HEAD_DIM), axis=0)
    dg_k = jnp.sum(dgk_pairs.reshape(2 * N_PAIRS, HEAD_DIM), axis=0)
    zeros = lambda n: jnp.zeros((n,), F32)
    small = jnp.stack([
        dg_mix[0], dg_ffn[0],
        jnp.concatenate([dg_conv_out[0], dg_attn_out[0]]),
        jnp.concatenate([dg_q, dg_k, zeros(D_MODEL - 2 * HEAD_DIM)]),
        jnp.concatenate([dconv[0], dconv[1]]),
        jnp.concatenate([dconv[2], zeros(D_CONV)]),
        jnp.concatenate([sq_err[0, :1], zeros(D_MODEL - 1)]),
        zeros(D_MODEL),
    ])
    total = _all_reduce_small(small)
    loss = total[6, 0] * (0.5 / D_MODEL)
    gg_mix, gg_ffn = total[0:1], total[1:2]
    gg_conv_out, gg_attn_out = total[2:3, :D_CONV], total[2:3, D_CONV:]
    gg_q, gg_k = total[3:4, :HEAD_DIM], total[3:4, HEAD_DIM:2 * HEAD_DIM]
    conv_total = jnp.stack([total[4, :D_CONV], total[4, D_CONV:], total[5, :D_CONV]])
    g_conv = lax.dynamic_slice(conv_total, (0, device * HEAD_DIM), (3, HEAD_DIM))

    names = ["g_mix", "w_in", "conv_w", "g_q", "g_k", "g_conv_out", "g_attn_out", "w_out", "g_ffn", "w_gate", "w_up", "w_down"]
    weights = [g_mix, w_in[0], conv_w[0], g_q, g_k, g_conv_out, g_attn_out, w_out[0], g_ffn, w_gate[0], w_up[0], w_down[0]]
    grad_list = [gg_mix, None, g_conv, gg_q, gg_k, gg_conv_out, gg_attn_out, None, gg_ffn, None, None, None]
    m_list = [m_g_mix, m_w_in[0], m_conv_w[0], m_g_q, m_g_k, m_g_conv_out, m_g_attn_out, m_w_out[0], m_g_ffn, m_w_gate[0], m_w_up[0], m_w_down[0]]
    v_list = [v_g_mix, v_w_in[0], v_conv_w[0], v_g_q, v_g_k, v_g_conv_out, v_g_attn_out, v_w_out[0], v_g_ffn, v_w_gate[0], v_w_up[0], v_w_down[0]]
    stacked = {"w_in", "conv_w", "w_out", "w_gate", "w_up", "w_down"}
    transposed = {"w_gate", "w_up"}
    results = {}
    operands = {n: (w, g, m, v) for n, w, g, m, v in zip(names, weights, grad_list, m_list, v_list)}
    for name in scattered:
        w, _, m, v = operands[name]
        if name in transposed:
            w, m, v = w.T, m.T, v.T
        results[name] = _adamw_received("adamw_" + name, dev, w, *scattered[name], m, v)
        if name in transposed:
            results[name] = [t.T for t in results[name]]
    for name in names:
        if name not in results:
            w, g, m, v = operands[name]
            g = gw_in if name == "w_in" else g
            results[name] = [g, *_adamw("adamw_" + name, w, g, m, v)]
    outs = []
    for part in range(4):
        outs += [results[n][part][None] if n in stacked else results[n][part] for n in names]
    return (loss, grad_x[None], *outs)
```
